```python
import math
import jax, jax.numpy as jnp
from jax import lax
import numpy as np

D_MODEL = 1024
BATCH = 8
SEQ = 4096
DEPTH = 4

CHUNK = 64
N_A_LAYERS = DEPTH // 2
N_B_LAYERS = DEPTH - N_A_LAYERS
POOL_WINDOWS = (2, 4, 8, 16)
N_POOL_GROUPS = len(POOL_WINDOWS)
POOL_GROUP_DIM = D_MODEL // N_POOL_GROUPS
D_FF = -(-8 * D_MODEL // (3 * 256)) * 256
N_HEADS = 8
QK_NOPE_DIM = 128
QK_ROPE_DIM = 64
QK_HEAD_DIM = QK_NOPE_DIM + QK_ROPE_DIM
V_HEAD_DIM = 128
Q_LORA_RANK = 256
KV_LORA_RANK = 512
ROPE_THETA = 10000.0
Q_BLOCK = 128
EPS = 1e-6

kernel_name = "yoco_pool_mla_hybrid"


def rms_norm(x, g):
    xf = x.astype(jnp.float32)
    y = xf * lax.rsqrt(jnp.mean(xf * xf, axis=-1, keepdims=True) + EPS)
    return (y * g.astype(jnp.float32)).astype(x.dtype)


def swiglu(h, w_gate, w_up, w_down):
    return (jax.nn.silu(h @ w_gate) * (h @ w_up)) @ w_down


def pool_mix(h, w_pool, b_pool, scale):
    B, S, D = h.shape
    hf = h.astype(jnp.float32)
    cs0 = jnp.pad(jnp.cumsum(hf, axis=1), ((0, 0), (1, 0), (0, 0)))
    t1 = jnp.arange(1, S + 1)
    groups = []
    for g, w in enumerate(POOL_WINDOWS):
        sl = slice(g * POOL_GROUP_DIM, (g + 1) * POOL_GROUP_DIM)
        c = cs0[:, :, sl]
        lag = jnp.pad(c, ((0, 0), (w, 0), (0, 0)))[:, : S + 1]
        win = (c - lag)[:, 1:]
        cnt = jnp.minimum(t1, w).astype(jnp.float32)[None, :, None]
        groups.append(win / cnt - hf[:, :, sl])
    d = jnp.stack(groups, axis=2).astype(h.dtype)
    y = jnp.einsum('bsgc,gcd->bsgd', d, w_pool) + b_pool
    return y.reshape(B, S, D) * scale


def rope(x, positions):
    R = x.shape[-1]
    half = R // 2
    inv = ROPE_THETA ** (-jnp.arange(half, dtype=jnp.float32) * 2.0 / R)
    ang = positions.astype(jnp.float32)[..., None] * inv
    cos = jnp.cos(ang)[:, :, None, :]
    sin = jnp.sin(ang)[:, :, None, :]
    xf = x.astype(jnp.float32)
    x1, x2 = xf[..., :half], xf[..., half:]
    return jnp.concatenate([x1 * cos - x2 * sin, x2 * cos + x1 * sin], axis=-1).astype(x.dtype)


def shared_kv(h, positions, w_dkv, g_kv_latent, w_uk, w_uv, g_k):
    B, S, _ = h.shape
    ckv = h @ w_dkv
    c = rms_norm(ckv[..., :KV_LORA_RANK], g_kv_latent)
    k_pe = ckv[..., KV_LORA_RANK:]
    k_nope = (c @ w_uk).reshape(B, S, N_HEADS, QK_NOPE_DIM)
    v = (c @ w_uv).reshape(B, S, N_HEADS, V_HEAD_DIM)
    k = jnp.concatenate([k_nope, jnp.broadcast_to(k_pe[:, :, None, :], (B, S, N_HEADS, QK_ROPE_DIM))], axis=-1)
    k = rms_norm(k, g_k)
    k = jnp.concatenate([k[..., :QK_NOPE_DIM], rope(k[..., QK_NOPE_DIM:], positions)], axis=-1)
    return k, v


def mla_queries(h, positions, w_dq, g_q_latent, w_uq, g_q):
    B, S, _ = h.shape
    cq = rms_norm(h @ w_dq, g_q_latent)
    q = (cq @ w_uq).reshape(B, S, N_HEADS, QK_HEAD_DIM)
    q = rms_norm(q, g_q)
    return jnp.concatenate([q[..., :QK_NOPE_DIM], rope(q[..., QK_NOPE_DIM:], positions)], axis=-1)


def chunk_causal_attention(q, k, v):
    S = q.shape[1]
    scale = 1.0 / math.sqrt(QK_HEAD_DIM)
    outs = []
    for i in range(S // Q_BLOCK):
        start, end = i * Q_BLOCK, (i + 1) * Q_BLOCK
        s = jnp.einsum('bqhd,bkhd->bhqk', q[:, start:end], k[:, :end]).astype(jnp.float32) * scale
        qc = (start + jnp.arange(Q_BLOCK)) // CHUNK
        kc = jnp.arange(end) // CHUNK
        mask = kc[None, :] <= qc[:, None]
        s = jnp.where(mask[None, None], s, jnp.float32(-1e30))
        p = jax.nn.softmax(s, axis=-1).astype(v.dtype)
        outs.append(jnp.einsum('bhqk,bkhd->bqhd', p, v[:, :end]))
    return jnp.concatenate(outs, axis=1)


def _fwd_setup_inputs(seed: int = 0) -> dict:
    key = jax.random.key(seed)
    ks = iter(jax.random.split(key, 32))
    f32 = jnp.float32

    def w(shape, fan_in):
        return jax.random.normal(next(ks), shape, f32) * fan_in ** -0.5

    def gain(shape):
        return 1.0 + 0.02 * jax.random.normal(next(ks), shape, f32)

    x = jax.random.normal(next(ks), (BATCH, SEQ, D_MODEL), f32)
    positions = jnp.broadcast_to(jnp.arange(SEQ, dtype=jnp.int32)[None, :], (BATCH, SEQ)).astype(jnp.int32)
    return {
        "x": x,
        "positions": positions,
        "ln_mix_a": gain((N_A_LAYERS, D_MODEL)),
        "w_pool": w((N_A_LAYERS, N_POOL_GROUPS, POOL_GROUP_DIM, POOL_GROUP_DIM), POOL_GROUP_DIM),
        "b_pool": 0.02 * jax.random.normal(next(ks), (N_A_LAYERS, N_POOL_GROUPS, POOL_GROUP_DIM), f32),
        "pool_scale": gain((N_A_LAYERS, D_MODEL)),
        "ln_ffn": gain((DEPTH, D_MODEL)),
        "w_gate": w((DEPTH, D_MODEL, D_FF), D_MODEL),
        "w_up": w((DEPTH, D_MODEL, D_FF), D_MODEL),
        "w_down": w((DEPTH, D_FF, D_MODEL), D_FF),
        "ln_kv": gain((D_MODEL,)),
        "w_dkv": w((D_MODEL, KV_LORA_RANK + QK_ROPE_DIM), D_MODEL),
        "g_kv_latent": gain((KV_LORA_RANK,)),
        "w_uk": w((KV_LORA_RANK, N_HEADS * QK_NOPE_DIM), KV_LORA_RANK),
        "w_uv": w((KV_LORA_RANK, N_HEADS * V_HEAD_DIM), KV_LORA_RANK),
        "g_k": gain((QK_HEAD_DIM,)),
        "ln_mix_b": gain((N_B_LAYERS, D_MODEL)),
        "w_dq": w((N_B_LAYERS, D_MODEL, Q_LORA_RANK), D_MODEL),
        "g_q_latent": gain((N_B_LAYERS, Q_LORA_RANK)),
        "w_uq": w((N_B_LAYERS, Q_LORA_RANK, N_HEADS * QK_HEAD_DIM), Q_LORA_RANK),
        "g_q": gain((N_B_LAYERS, QK_HEAD_DIM)),
        "w_o": w((N_B_LAYERS, N_HEADS * V_HEAD_DIM, D_MODEL), N_HEADS * V_HEAD_DIM),
    }


def _fwd_reference(x, positions, ln_mix_a, w_pool, b_pool, pool_scale, ln_ffn, w_gate, w_up, w_down,
              ln_kv, w_dkv, g_kv_latent, w_uk, w_uv, g_k,
              ln_mix_b, w_dq, g_q_latent, w_uq, g_q, w_o):
    B, S, D = x.shape
    k_sh, v_sh = None, None
    for l in range(DEPTH):
        if l < N_A_LAYERS:
            x = x + pool_mix(rms_norm(x, ln_mix_a[l]), w_pool[l], b_pool[l], pool_scale[l])
        else:
            j = l - N_A_LAYERS
            q = mla_queries(rms_norm(x, ln_mix_b[j]), positions, w_dq[j], g_q_latent[j], w_uq[j], g_q[j])
            o = chunk_causal_attention(q, k_sh, v_sh)
            x = x + o.reshape(B, S, N_HEADS * V_HEAD_DIM) @ w_o[j]
        x = x + swiglu(rms_norm(x, ln_ffn[l]), w_gate[l], w_up[l], w_down[l])
        if l == N_A_LAYERS - 1:
            k_sh, v_sh = shared_kv(rms_norm(x, ln_kv), positions, w_dkv, g_kv_latent, w_uk, w_uv, g_k)
    return x


import jax as _jax
import jax.numpy as _jnp

TWIN_FORMAT = 'train_step'
FWD_PARAMS = ['x', 'positions', 'ln_mix_a', 'w_pool', 'b_pool', 'pool_scale', 'ln_ffn', 'w_gate', 'w_up', 'w_down', 'ln_kv', 'w_dkv', 'g_kv_latent', 'w_uk', 'w_uv', 'g_k', 'ln_mix_b', 'w_dq', 'g_q_latent', 'w_uq', 'g_q', 'w_o']
TWIN_WEIGHTS = ['ln_mix_a', 'w_pool', 'b_pool', 'pool_scale', 'ln_ffn', 'w_gate', 'w_up', 'w_down', 'ln_kv', 'w_dkv', 'g_kv_latent', 'w_uk', 'w_uv', 'g_k', 'ln_mix_b', 'w_dq', 'g_q_latent', 'w_uq', 'g_q', 'w_o']
TWIN_DIFF_INPUT = 'x'
TWIN_INPUTS = ['x', 'positions', 'ln_mix_a', 'w_pool', 'b_pool', 'pool_scale', 'ln_ffn', 'w_gate', 'w_up', 'w_down', 'ln_kv', 'w_dkv', 'g_kv_latent', 'w_uk', 'w_uv', 'g_k', 'ln_mix_b', 'w_dq', 'g_q_latent', 'w_uq', 'g_q', 'w_o', 'loss_target', 'm_ln_mix_a', 'm_w_pool', 'm_b_pool', 'm_pool_scale', 'm_ln_ffn', 'm_w_gate', 'm_w_up', 'm_w_down', 'm_ln_kv', 'm_w_dkv', 'm_g_kv_latent', 'm_w_uk', 'm_w_uv', 'm_g_k', 'm_ln_mix_b', 'm_w_dq', 'm_g_q_latent', 'm_w_uq', 'm_g_q', 'm_w_o', 'v_ln_mix_a', 'v_w_pool', 'v_b_pool', 'v_pool_scale', 'v_ln_ffn', 'v_w_gate', 'v_w_up', 'v_w_down', 'v_ln_kv', 'v_w_dkv', 'v_g_kv_latent', 'v_w_uk', 'v_w_uv', 'v_g_k', 'v_ln_mix_b', 'v_w_dq', 'v_g_q_latent', 'v_w_uq', 'v_g_q', 'v_w_o']
TWIN_OUTPUTS = ['loss', 'grad_x', 'grad_ln_mix_a', 'grad_w_pool', 'grad_b_pool', 'grad_pool_scale', 'grad_ln_ffn', 'grad_w_gate', 'grad_w_up', 'grad_w_down', 'grad_ln_kv', 'grad_w_dkv', 'grad_g_kv_latent', 'grad_w_uk', 'grad_w_uv', 'grad_g_k', 'grad_ln_mix_b', 'grad_w_dq', 'grad_g_q_latent', 'grad_w_uq', 'grad_g_q', 'grad_w_o', 'delta_ln_mix_a', 'delta_w_pool', 'delta_b_pool', 'delta_pool_scale', 'delta_ln_ffn', 'delta_w_gate', 'delta_w_up', 'delta_w_down', 'delta_ln_kv', 'delta_w_dkv', 'delta_g_kv_latent', 'delta_w_uk', 'delta_w_uv', 'delta_g_k', 'delta_ln_mix_b', 'delta_w_dq', 'delta_g_q_latent', 'delta_w_uq', 'delta_g_q', 'delta_w_o', 'new_m_ln_mix_a', 'new_m_w_pool', 'new_m_b_pool', 'new_m_pool_scale', 'new_m_ln_ffn', 'new_m_w_gate', 'new_m_w_up', 'new_m_w_down', 'new_m_ln_kv', 'new_m_w_dkv', 'new_m_g_kv_latent', 'new_m_w_uk', 'new_m_w_uv', 'new_m_g_k', 'new_m_ln_mix_b', 'new_m_w_dq', 'new_m_g_q_latent', 'new_m_w_uq', 'new_m_g_q', 'new_m_w_o', 'new_v_ln_mix_a', 'new_v_w_pool', 'new_v_b_pool', 'new_v_pool_scale', 'new_v_ln_ffn', 'new_v_w_gate', 'new_v_w_up', 'new_v_w_down', 'new_v_ln_kv', 'new_v_w_dkv', 'new_v_g_kv_latent', 'new_v_w_uk', 'new_v_w_uv', 'new_v_g_k', 'new_v_ln_mix_b', 'new_v_w_dq', 'new_v_g_q_latent', 'new_v_w_uq', 'new_v_g_q', 'new_v_w_o']
TWIN_LEAF_KINDS = {'loss': 'loss', 'grad_x': 'grad_x', 'grad_ln_mix_a': 'grad_w', 'grad_w_pool': 'grad_w', 'grad_b_pool': 'grad_w', 'grad_pool_scale': 'grad_w', 'grad_ln_ffn': 'grad_w', 'grad_w_gate': 'grad_w', 'grad_w_up': 'grad_w', 'grad_w_down': 'grad_w', 'grad_ln_kv': 'grad_w', 'grad_w_dkv': 'grad_w', 'grad_g_kv_latent': 'grad_w', 'grad_w_uk': 'grad_w', 'grad_w_uv': 'grad_w', 'grad_g_k': 'grad_w', 'grad_ln_mix_b': 'grad_w', 'grad_w_dq': 'grad_w', 'grad_g_q_latent': 'grad_w', 'grad_w_uq': 'grad_w', 'grad_g_q': 'grad_w', 'grad_w_o': 'grad_w', 'delta_ln_mix_a': 'delta_w', 'delta_w_pool': 'delta_w', 'delta_b_pool': 'delta_w', 'delta_pool_scale': 'delta_w', 'delta_ln_ffn': 'delta_w', 'delta_w_gate': 'delta_w', 'delta_w_up': 'delta_w', 'delta_w_down': 'delta_w', 'delta_ln_kv': 'delta_w', 'delta_w_dkv': 'delta_w', 'delta_g_kv_latent': 'delta_w', 'delta_w_uk': 'delta_w', 'delta_w_uv': 'delta_w', 'delta_g_k': 'delta_w', 'delta_ln_mix_b': 'delta_w', 'delta_w_dq': 'delta_w', 'delta_g_q_latent': 'delta_w', 'delta_w_uq': 'delta_w', 'delta_g_q': 'delta_w', 'delta_w_o': 'delta_w', 'new_m_ln_mix_a': 'new_m', 'new_m_w_pool': 'new_m', 'new_m_b_pool': 'new_m', 'new_m_pool_scale': 'new_m', 'new_m_ln_ffn': 'new_m', 'new_m_w_gate': 'new_m', 'new_m_w_up': 'new_m', 'new_m_w_down': 'new_m', 'new_m_ln_kv': 'new_m', 'new_m_w_dkv': 'new_m', 'new_m_g_kv_latent': 'new_m', 'new_m_w_uk': 'new_m', 'new_m_w_uv': 'new_m', 'new_m_g_k': 'new_m', 'new_m_ln_mix_b': 'new_m', 'new_m_w_dq': 'new_m', 'new_m_g_q_latent': 'new_m', 'new_m_w_uq': 'new_m', 'new_m_g_q': 'new_m', 'new_m_w_o': 'new_m', 'new_v_ln_mix_a': 'new_v', 'new_v_w_pool': 'new_v', 'new_v_b_pool': 'new_v', 'new_v_pool_scale': 'new_v', 'new_v_ln_ffn': 'new_v', 'new_v_w_gate': 'new_v', 'new_v_w_up': 'new_v', 'new_v_w_down': 'new_v', 'new_v_ln_kv': 'new_v', 'new_v_w_dkv': 'new_v', 'new_v_g_kv_latent': 'new_v', 'new_v_w_uk': 'new_v', 'new_v_w_uv': 'new_v', 'new_v_g_k': 'new_v', 'new_v_ln_mix_b': 'new_v', 'new_v_w_dq': 'new_v', 'new_v_g_q_latent': 'new_v', 'new_v_w_uq': 'new_v', 'new_v_g_q': 'new_v', 'new_v_w_o': 'new_v'}


def _forward(args):
    return _fwd_reference(*[args[k] for k in FWD_PARAMS])


def _output_shape():
    def fwd():
        inp = _fwd_setup_inputs(0)
        return _fwd_reference(*[inp[k] for k in FWD_PARAMS])
    out = _jax.eval_shape(fwd)
    return out.shape, out.dtype

N_MICROBATCH = 1
ADAM_LR = 0.001
ADAM_B1 = 0.9
ADAM_B2 = 0.999
ADAM_EPS = 1e-08
ADAM_WD = 0.01
ADAM_STEP = 10
PER_EXAMPLE_BATCH_AXIS = {'x': 0, 'positions': 0, 'loss_target': 0}
SHARED_INPUTS = []
_WEIGHT_DTYPES = {'ln_mix_a': _jnp.float32, 'w_pool': _jnp.float32, 'b_pool': _jnp.float32, 'pool_scale': _jnp.float32, 'ln_ffn': _jnp.float32, 'w_gate': _jnp.float32, 'w_up': _jnp.float32, 'w_down': _jnp.float32, 'ln_kv': _jnp.float32, 'w_dkv': _jnp.float32, 'g_kv_latent': _jnp.float32, 'w_uk': _jnp.float32, 'w_uv': _jnp.float32, 'g_k': _jnp.float32, 'ln_mix_b': _jnp.float32, 'w_dq': _jnp.float32, 'g_q_latent': _jnp.float32, 'w_uq': _jnp.float32, 'g_q': _jnp.float32, 'w_o': _jnp.float32}
MOMENT_SCALE = {'ln_mix_a': 2.671904e+01, 'w_pool': 3.214069e+00, 'b_pool': 6.939708e+00, 'pool_scale': 2.673091e+01, 'ln_ffn': 2.476446e+01, 'w_gate': 2.539687e-01, 'w_up': 3.273579e-01, 'w_down': 5.231990e-01, 'ln_kv': 1.838361e-01, 'w_dkv': 2.108926e-01, 'g_kv_latent': 7.778017e-01, 'w_uk': 1.214892e-01, 'w_uv': 1.606226e-01, 'g_k': 1.714365e+00, 'ln_mix_b': 8.370847e-02, 'w_dq': 1.711736e-01, 'g_q_latent': 1.695597e-01, 'w_uq': 7.007408e-02, 'g_q': 8.723972e-01, 'w_o': 1.143199e-01}


def _to_microbatches(a, axis):
    t = _jnp.moveaxis(a, axis, 0)
    t = t.reshape((N_MICROBATCH, t.shape[0] // N_MICROBATCH) + t.shape[1:])
    return _jnp.moveaxis(t, 1, axis + 1)


def setup_inputs(seed: int = 0) -> dict:
    inp = _fwd_setup_inputs(seed)
    key = _jax.random.fold_in(_jax.random.key(seed), 7919)
    shape, _ = _output_shape()
    out = dict(inp)
    out["loss_target"] = _jax.random.normal(_jax.random.fold_in(key, 0), shape, _jnp.float32)
    for i, name in enumerate(TWIN_WEIGHTS):
        w = inp[name].astype(_jnp.float32)
        if MOMENT_SCALE is None:
            s = _jnp.sqrt(_jnp.mean(_jnp.square(w)) + 1e-30)
        else:
            s = MOMENT_SCALE[name]
        km, kv = _jax.random.split(_jax.random.fold_in(key, i + 1))
        out[name] = w
        out["m_" + name] = s * _jax.random.normal(km, w.shape, _jnp.float32)
        out["v_" + name] = (s * s) * _jax.random.uniform(kv, w.shape, _jnp.float32, 0.5, 1.5)
    if N_MICROBATCH > 1:
        for name, axis in PER_EXAMPLE_BATCH_AXIS.items():
            out[name] = _to_microbatches(out[name], axis)
    return {'x': out['x'], 'positions': out['positions'], 'ln_mix_a': out['ln_mix_a'], 'w_pool': out['w_pool'], 'b_pool': out['b_pool'], 'pool_scale': out['pool_scale'], 'ln_ffn': out['ln_ffn'], 'w_gate': out['w_gate'], 'w_up': out['w_up'], 'w_down': out['w_down'], 'ln_kv': out['ln_kv'], 'w_dkv': out['w_dkv'], 'g_kv_latent': out['g_kv_latent'], 'w_uk': out['w_uk'], 'w_uv': out['w_uv'], 'g_k': out['g_k'], 'ln_mix_b': out['ln_mix_b'], 'w_dq': out['w_dq'], 'g_q_latent': out['g_q_latent'], 'w_uq': out['w_uq'], 'g_q': out['g_q'], 'w_o': out['w_o'], 'loss_target': out['loss_target'], 'm_ln_mix_a': out['m_ln_mix_a'], 'm_w_pool': out['m_w_pool'], 'm_b_pool': out['m_b_pool'], 'm_pool_scale': out['m_pool_scale'], 'm_ln_ffn': out['m_ln_ffn'], 'm_w_gate': out['m_w_gate'], 'm_w_up': out['m_w_up'], 'm_w_down': out['m_w_down'], 'm_ln_kv': out['m_ln_kv'], 'm_w_dkv': out['m_w_dkv'], 'm_g_kv_latent': out['m_g_kv_latent'], 'm_w_uk': out['m_w_uk'], 'm_w_uv': out['m_w_uv'], 'm_g_k': out['m_g_k'], 'm_ln_mix_b': out['m_ln_mix_b'], 'm_w_dq': out['m_w_dq'], 'm_g_q_latent': out['m_g_q_latent'], 'm_w_uq': out['m_w_uq'], 'm_g_q': out['m_g_q'], 'm_w_o': out['m_w_o'], 'v_ln_mix_a': out['v_ln_mix_a'], 'v_w_pool': out['v_w_pool'], 'v_b_pool': out['v_b_pool'], 'v_pool_scale': out['v_pool_scale'], 'v_ln_ffn': out['v_ln_ffn'], 'v_w_gate': out['v_w_gate'], 'v_w_up': out['v_w_up'], 'v_w_down': out['v_w_down'], 'v_ln_kv': out['v_ln_kv'], 'v_w_dkv': out['v_w_dkv'], 'v_g_kv_latent': out['v_g_kv_latent'], 'v_w_uk': out['v_w_uk'], 'v_w_uv': out['v_w_uv'], 'v_g_k': out['v_g_k'], 'v_ln_mix_b': out['v_ln_mix_b'], 'v_w_dq': out['v_w_dq'], 'v_g_q_latent': out['v_g_q_latent'], 'v_w_uq': out['v_w_uq'], 'v_g_q': out['v_g_q'], 'v_w_o': out['v_w_o']}


def _loss(weights, diff, rest, loss_target):
    with _jax.named_scope("forward"):
        args = {**rest, TWIN_DIFF_INPUT: diff, **{k: w.astype(_WEIGHT_DTYPES[k]) for k, w in weights.items()}}
        y = _forward(args)
    with _jax.named_scope("loss_head"):
        err = _jnp.square(y.astype(_jnp.float32) - loss_target)
        return 0.5 * _jnp.sum(_jnp.mean(err, axis=-1)) if err.ndim else 0.5 * err


def _adamw(w, g, m, v):
    m = ADAM_B1 * m + (1.0 - ADAM_B1) * g
    v = ADAM_B2 * v + (1.0 - ADAM_B2) * _jnp.square(g)
    m_hat = m / (1.0 - ADAM_B1 ** ADAM_STEP)
    v_hat = v / (1.0 - ADAM_B2 ** ADAM_STEP)
    delta = -ADAM_LR * (m_hat / (_jnp.sqrt(v_hat) + ADAM_EPS) + ADAM_WD * w)
    return delta, m, v


def reference(x, positions, ln_mix_a, w_pool, b_pool, pool_scale, ln_ffn, w_gate, w_up, w_down, ln_kv, w_dkv, g_kv_latent, w_uk, w_uv, g_k, ln_mix_b, w_dq, g_q_latent, w_uq, g_q, w_o, loss_target, m_ln_mix_a, m_w_pool, m_b_pool, m_pool_scale, m_ln_ffn, m_w_gate, m_w_up, m_w_down, m_ln_kv, m_w_dkv, m_g_kv_latent, m_w_uk, m_w_uv, m_g_k, m_ln_mix_b, m_w_dq, m_g_q_latent, m_w_uq, m_g_q, m_w_o, v_ln_mix_a, v_w_pool, v_b_pool, v_pool_scale, v_ln_ffn, v_w_gate, v_w_up, v_w_down, v_ln_kv, v_w_dkv, v_g_kv_latent, v_w_uk, v_w_uv, v_g_k, v_ln_mix_b, v_w_dq, v_g_q_latent, v_w_uq, v_g_q, v_w_o):
    given = dict(x=x, positions=positions, ln_mix_a=ln_mix_a, w_pool=w_pool, b_pool=b_pool, pool_scale=pool_scale, ln_ffn=ln_ffn, w_gate=w_gate, w_up=w_up, w_down=w_down, ln_kv=ln_kv, w_dkv=w_dkv, g_kv_latent=g_kv_latent, w_uk=w_uk, w_uv=w_uv, g_k=g_k, ln_mix_b=ln_mix_b, w_dq=w_dq, g_q_latent=g_q_latent, w_uq=w_uq, g_q=g_q, w_o=w_o, loss_target=loss_target, m_ln_mix_a=m_ln_mix_a, m_w_pool=m_w_pool, m_b_pool=m_b_pool, m_pool_scale=m_pool_scale, m_ln_ffn=m_ln_ffn, m_w_gate=m_w_gate, m_w_up=m_w_up, m_w_down=m_w_down, m_ln_kv=m_ln_kv, m_w_dkv=m_w_dkv, m_g_kv_latent=m_g_kv_latent, m_w_uk=m_w_uk, m_w_uv=m_w_uv, m_g_k=m_g_k, m_ln_mix_b=m_ln_mix_b, m_w_dq=m_w_dq, m_g_q_latent=m_g_q_latent, m_w_uq=m_w_uq, m_g_q=m_g_q, m_w_o=m_w_o, v_ln_mix_a=v_ln_mix_a, v_w_pool=v_w_pool, v_b_pool=v_b_pool, v_pool_scale=v_pool_scale, v_ln_ffn=v_ln_ffn, v_w_gate=v_w_gate, v_w_up=v_w_up, v_w_down=v_w_down, v_ln_kv=v_ln_kv, v_w_dkv=v_w_dkv, v_g_kv_latent=v_g_kv_latent, v_w_uk=v_w_uk, v_w_uv=v_w_uv, v_g_k=v_g_k, v_ln_mix_b=v_ln_mix_b, v_w_dq=v_w_dq, v_g_q_latent=v_g_q_latent, v_w_uq=v_w_uq, v_g_q=v_g_q, v_w_o=v_w_o)
    weights = {n: given[n] for n in TWIN_WEIGHTS}
    shared = {n: given[n] for n in SHARED_INPUTS}
    per_example = {n: given[n] for n in ['x', 'positions']}
    grad_fn = _jax.value_and_grad(_loss, argnums=(0, 1))

    def one_microbatch(ex, loss_target):
        ex = dict(ex)
        diff = ex.pop(TWIN_DIFF_INPUT)
        return grad_fn(weights, diff, {**shared, **ex}, loss_target)

    if N_MICROBATCH == 1:
        loss, (grad_w, grad_x) = one_microbatch(per_example, given["loss_target"])
    else:
        def body(carry, xs):
            loss_sum, grad_sum = carry
            l_k, (gw_k, gx_k) = one_microbatch(xs[0], xs[1])
            with _jax.named_scope("update"):
                return (loss_sum + l_k, _jax.tree.map(_jnp.add, grad_sum, gw_k)), gx_k

        init = (_jnp.zeros((), _jnp.float32), _jax.tree.map(_jnp.zeros_like, weights))
        (loss, grad_w), grad_x = _jax.lax.scan(body, init, (per_example, given["loss_target"]))
    with _jax.named_scope("update"):
        delta_w, new_m, new_v = {}, {}, {}
        for n in TWIN_WEIGHTS:
            delta_w[n], new_m[n], new_v[n] = _adamw(weights[n], grad_w[n], given["m_" + n], given["v_" + n])
    return (loss, grad_x, *[grad_w[n] for n in TWIN_WEIGHTS], *[delta_w[n] for n in TWIN_WEIGHTS],
            *[new_m[n] for n in TWIN_WEIGHTS], *[new_v[n] for n in TWIN_WEIGHTS])
```

```python
import functools
import math

import jax
import jax.numpy as jnp
from jax import lax
from jax.experimental import pallas as pl
from jax.experimental.pallas import tpu as pltpu

F32 = jnp.float32
BF16 = jnp.bfloat16
MESH = pl.DeviceIdType.MESH

D_MODEL = 1024
D_FF = 2816
N_DEV = 8
N_CHIPS = 4
FF_SHARD = D_FF // N_DEV
FF_HALF = D_FF // 2
N_HEADS = 8
NOPE = 128
ROPE = 64
QK_DIM = NOPE + ROPE
QK_PAD = 256
V_DIM = 128
Q_RANK = 256
KV_RANK = 512
POOL_WINDOWS = (2, 4, 8, 16)
GROUP_DIM = 256
HALO = 128
CHUNK = 64
ROPE_THETA = 10000.0
EPS = 1e-6
LANES = 128

ADAM_LR = 0.001
ADAM_B1 = 0.9
ADAM_B2 = 0.999
ADAM_EPS = 1e-08
ADAM_WD = 0.01
ADAM_STEP = 10

VMEM_BIG = 56 * 2**20
VMEM_MID = 40 * 2**20


def _nn(a, b):
    return lax.dot_general(a, b, (((1,), (0,)), ((), ())), preferred_element_type=F32)


def _nt(a, b):
    return lax.dot_general(a, b, (((1,), (1,)), ((), ())), preferred_element_type=F32)


def _tn(a, b):
    return lax.dot_general(a, b, (((0,), (0,)), ((), ())), preferred_element_type=F32)


def _rms(x, g, n):
    r = lax.rsqrt(jnp.sum(x * x, axis=-1, keepdims=True) * (1.0 / n) + EPS)
    return (x * r) * g, r


def _rms_bwd(x, r, g, dy, n):
    u = dy * g
    s = jnp.sum(x * u, axis=-1, keepdims=True) * (1.0 / n)
    dx = r * u - x * (r * r * r * s)
    dg = jnp.sum(dy * (x * r), axis=0, keepdims=True)
    return dx, dg


def _swap_halves(z):
    lane = lax.broadcasted_iota(jnp.int32, z.shape, 1)
    return jnp.where(lane < ROPE // 2, pltpu.roll(z, LANES - ROPE // 2, 1), pltpu.roll(z, ROPE // 2, 1))


def _sigmoid(x):
    return 1.0 / (1.0 + jnp.exp(-x))


def _cparams(n_grid, vmem=None):
    return pltpu.CompilerParams(dimension_semantics=("arbitrary",) * n_grid, vmem_limit_bytes=vmem)


def _rows(t, cols):
    return pl.BlockSpec((t, cols), lambda i: (i, 0))


def _full(shape):
    nd = len(shape)
    return pl.BlockSpec(shape, lambda *_: (0,) * nd)


ANY = pl.BlockSpec(memory_space=pl.ANY)


def _place():
    x, y, c = lax.axis_index("x"), lax.axis_index("y"), lax.axis_index("c")
    return x, y, c


def _all_gather(shards, axes, name):
    n = len(shards)
    out_shape = [jax.ShapeDtypeStruct(s.shape[:a] + (N_DEV,) + s.shape[a:], s.dtype) for s, a in zip(shards, axes)]

    def body(*refs):
        ins, outs = refs[:n], refs[n:2 * n]
        send_sems, recv_sems, local_sems = refs[2 * n:]
        x, y, c = _place()
        me, sibling = (x, y, c), (x, y, 1 - c)
        chips = [(1 - x, y), (x, 1 - y), (1 - x, 1 - y)]

        def slot(t, dev):
            idx = 4 * dev[0] + 2 * dev[1] + dev[2]
            return outs[t].at[(slice(None),) * axes[t] + (idx,)]

        def copy(t, k, block, to, src=None):
            return pltpu.make_async_remote_copy(
                src_ref=slot(t, block) if src is None else src, dst_ref=slot(t, block),
                send_sem=send_sems.at[t, k], recv_sem=recv_sems.at[t, k],
                device_id=to, device_id_type=MESH)

        mine = [pltpu.make_async_copy(ins[t], slot(t, me), local_sems.at[t]) for t in range(n)]
        for cp in mine:
            cp.start()
        first = []
        for t in range(n):
            first.append(copy(t, 0, me, sibling, src=ins[t]))
            first += [copy(t, 1 + j, me, (*chip, c), src=ins[t]) for j, chip in enumerate(chips)]
        for cp in first:
            cp.start()
        passed = []
        for j, chip in enumerate(chips):
            for t in range(n):
                copy(t, 1 + j, (*chip, c), me).wait_recv()
                cp = copy(t, 4 + j, (*chip, c), sibling)
                cp.start()
                passed.append(cp)
        for t in range(n):
            copy(t, 0, sibling, me).wait_recv()
            for j, chip in enumerate(chips):
                copy(t, 4 + j, (*chip, 1 - c), me).wait_recv()
        for cp in first + passed:
            cp.wait_send()
        for cp in mine:
            cp.wait()

    return pl.pallas_call(
        body, name=name, out_shape=out_shape,
        in_specs=[ANY] * n, out_specs=[ANY] * n,
        scratch_shapes=[pltpu.SemaphoreType.DMA((n, 7)), pltpu.SemaphoreType.DMA((n, 7)),
                        pltpu.SemaphoreType.DMA((n,))],
    )(*shards)


def _pair_exchange(grads, name):
    n = len(grads)
    out_shape = [jax.ShapeDtypeStruct((2, g.shape[0], N_CHIPS) + g.shape[3:], g.dtype) for g in grads]

    def body(*refs):
        ins, outs = refs[:n], refs[n:2 * n]
        send_sems, recv_sems, local_sems = refs[2 * n:]
        x, y, c = _place()
        sends, keeps = [], []
        for t in range(n):
            keeps.append(pltpu.make_async_copy(ins[t].at[:, :, c], outs[t].at[0], local_sems.at[t]))
            sends.append(pltpu.make_async_remote_copy(
                src_ref=ins[t].at[:, :, 1 - c], dst_ref=outs[t].at[1],
                send_sem=send_sems.at[t], recv_sem=recv_sems.at[t],
                device_id=(x, y, 1 - c), device_id_type=MESH))
        for cp in keeps + sends:
            cp.start()
        for cp in sends:
            cp.wait()
        for cp in keeps:
            cp.wait()

    return pl.pallas_call(
        body, name=name, out_shape=out_shape,
        in_specs=[ANY] * n, out_specs=[ANY] * n,
        scratch_shapes=[pltpu.SemaphoreType.DMA((n,)), pltpu.SemaphoreType.DMA((n,)),
                        pltpu.SemaphoreType.DMA((n,))],
    )(*grads)


def _chip_exchange(parts, name):
    n = len(parts)
    out_shape = [jax.ShapeDtypeStruct((4, p.shape[0]) + p.shape[2:], p.dtype) for p in parts]

    def body(*refs):
        ins, outs = refs[:n], refs[n:2 * n]
        send_sems, recv_sems, local_sems = refs[2 * n:]
        x, y, c = _place()
        chips = [(1 - x, y), (x, 1 - y), (1 - x, 1 - y)]
        sends, keeps = [], []
        for t in range(n):
            keeps.append(pltpu.make_async_copy(ins[t].at[:, 2 * x + y], outs[t].at[3], local_sems.at[t]))
            for j, (px, py) in enumerate(chips):
                sends.append(pltpu.make_async_remote_copy(
                    src_ref=ins[t].at[:, 2 * px + py], dst_ref=outs[t].at[j],
                    send_sem=send_sems.at[t, j], recv_sem=recv_sems.at[t, j],
                    device_id=(px, py, c), device_id_type=MESH))
        for cp in keeps + sends:
            cp.start()
        for cp in sends:
            cp.wait()
        for cp in keeps:
            cp.wait()

    return pl.pallas_call(
        body, name=name, out_shape=out_shape,
        in_specs=[ANY] * n, out_specs=[ANY] * n,
        scratch_shapes=[pltpu.SemaphoreType.DMA((n, 3)), pltpu.SemaphoreType.DMA((n, 3)),
                        pltpu.SemaphoreType.DMA((n,))],
    )(*parts)


def _sum_lead(a, name, out_dtype=F32):
    k = a.shape[0]
    rest = a.shape[1:]
    r, c = rest[-2], rest[-1]
    lead = math.prod(rest[:-2])
    a3 = a.reshape(k, lead * r, c)
    rows = lead * r
    tb = rows
    for cand in (512, 256, 128, 64, 32, 16, 8):
        if rows % cand == 0 and rows > cand:
            tb = cand
            break

    def body(a_ref, o_ref):
        acc = a_ref[0].astype(F32)
        for i in range(1, k):
            acc = acc + a_ref[i].astype(F32)
        o_ref[...] = acc.astype(out_dtype)

    out = pl.pallas_call(
        body, name=name, grid=(rows // tb,),
        out_shape=jax.ShapeDtypeStruct((rows, c), out_dtype),
        in_specs=[pl.BlockSpec((k, tb, c), lambda i: (0, i, 0))],
        out_specs=pl.BlockSpec((tb, c), lambda i: (i, 0)),
        compiler_params=_cparams(1),
    )(a3)
    return out.reshape(rest)


def _band(t, w, offset, valid):
    r = lax.broadcasted_iota(jnp.int32, (t, t + HALO), 0)
    col = lax.broadcasted_iota(jnp.int32, (t, t + HALO), 1)
    diff = offset(r, col)
    return jnp.where((diff >= 0) & (diff < w) & valid(col), 1.0, 0.0).astype(BF16)


def _split_dot(band, v):
    hi = v.astype(BF16)
    lo = (v - hi.astype(F32)).astype(BF16)
    return _nn(band, hi) + _nn(band, lo)


def _mix_fwd(x, g, wp, b, sc, name):
    s = x.shape[0]
    t = min(256, s)
    rb = t // HALO

    def body(x_ref, xh_ref, g_ref, wp_ref, b_ref, sc_ref, xo_ref, d_ref):
        i = pl.program_id(0)
        gg = g_ref[...]
        h, _ = _rms(x_ref[...], gg, D_MODEL)
        hh, _ = _rms(xh_ref[...], gg, D_MODEL)
        hext = jnp.concatenate([hh, h], axis=0)
        tok = i * t + lax.broadcasted_iota(jnp.int32, (t, 1), 0)
        for gi, w in enumerate(POOL_WINDOWS):
            sl = slice(gi * GROUP_DIM, (gi + 1) * GROUP_DIM)
            band = _band(t, w, lambda r, col: r + HALO - col, lambda col: (col >= HALO) | (i > 0))
            win = _split_dot(band, hext[:, sl])
            cnt = jnp.minimum(tok + 1, w).astype(F32)
            dbf = (win / cnt - h[:, sl]).astype(BF16)
            d_ref[:, sl] = dbf
            ypre = _nn(dbf, wp_ref[gi]) + b_ref[:, sl]
            xo_ref[:, sl] = x_ref[:, sl] + ypre * sc_ref[:, sl]

    return pl.pallas_call(
        body, name=name, grid=(s // t,),
        out_shape=[jax.ShapeDtypeStruct((s, D_MODEL), F32), jax.ShapeDtypeStruct((s, D_MODEL), BF16)],
        in_specs=[_rows(t, D_MODEL),
                  pl.BlockSpec((HALO, D_MODEL), lambda i: (jnp.maximum(i * rb - 1, 0), 0)),
                  _full((1, D_MODEL)), _full((4, GROUP_DIM, GROUP_DIM)), _full((1, D_MODEL)), _full((1, D_MODEL))],
        out_specs=[_rows(t, D_MODEL), _rows(t, D_MODEL)],
        compiler_params=_cparams(1, VMEM_MID),
    )(x, x, g, wp, b, sc)


def _mix_bwd(x, dy, d, g, wp, b, sc, name):
    s = x.shape[0]
    t = min(256, s)
    rb = t // HALO
    nb = s // t
    last_halo = s // HALO - 1

    def body(x_ref, dy_ref, dyn_ref, d_ref, g_ref, wp_ref, b_ref, sc_ref,
             dx_ref, dyp_ref, dsc_ref, db_ref, dln_ref):
        i = pl.program_id(0)
        x = x_ref[...]
        gg = g_ref[...]
        dy = dy_ref[...]
        sc = sc_ref[...]
        dyp32 = dy * sc
        dyp = dyp32.astype(BF16)
        dyph = (dyn_ref[...] * sc).astype(BF16)
        dyp_ref[...] = dyp
        tok = i * t + lax.broadcasted_iota(jnp.int32, (t + HALO, 1), 0)
        dh, dsc = [], []
        for gi, w in enumerate(POOL_WINDOWS):
            sl = slice(gi * GROUP_DIM, (gi + 1) * GROUP_DIM)
            ypre = _nn(d_ref[:, sl], wp_ref[gi]) + b_ref[:, sl]
            dsc.append(jnp.sum(dy[:, sl] * ypre, axis=0, keepdims=True))
            dd = _nt(dyp[:, sl], wp_ref[gi])
            ddh = _nt(dyph[:, sl], wp_ref[gi])
            cnt = jnp.minimum(tok + 1, w).astype(F32)
            ddext = jnp.concatenate([dd, ddh], axis=0) / cnt
            band = _band(t, w, lambda r, col: col - r, lambda col: (col < t) | (i < nb - 1))
            dh.append(_split_dot(band, ddext) - dd)
        dh = jnp.concatenate(dh, axis=1)
        _, r = _rms(x, gg, D_MODEL)
        dxn, dg = _rms_bwd(x, r, gg, dh, D_MODEL)
        dx_ref[...] = dy + dxn

        @pl.when(i == 0)
        def _():
            dsc_ref[...] = jnp.zeros_like(dsc_ref)
            db_ref[...] = jnp.zeros_like(db_ref)
            dln_ref[...] = jnp.zeros_like(dln_ref)

        dsc_ref[...] += jnp.concatenate(dsc, axis=1)
        db_ref[...] += jnp.sum(dyp32, axis=0, keepdims=True)
        dln_ref[...] += dg

    vec = jax.ShapeDtypeStruct((1, D_MODEL), F32)
    return pl.pallas_call(
        body, name=name, grid=(nb,),
        out_shape=[jax.ShapeDtypeStruct((s, D_MODEL), F32), jax.ShapeDtypeStruct((s, D_MODEL), BF16), vec, vec, vec],
        in_specs=[_rows(t, D_MODEL), _rows(t, D_MODEL),
                  pl.BlockSpec((HALO, D_MODEL), lambda i: (jnp.minimum((i + 1) * rb, last_halo), 0)),
                  _rows(t, D_MODEL),
                  _full((1, D_MODEL)), _full((4, GROUP_DIM, GROUP_DIM)), _full((1, D_MODEL)), _full((1, D_MODEL))],
        out_specs=[_rows(t, D_MODEL), _rows(t, D_MODEL), _full((1, D_MODEL)), _full((1, D_MODEL)), _full((1, D_MODEL))],
        compiler_params=_cparams(1, VMEM_MID),
    )(x, dy, dy, d, g, wp, b, sc)


def _load_weights(w_hbm, w_vmem, sem):
    @pl.when(pl.program_id(0) == 0)
    def _():
        cp = pltpu.make_async_copy(w_hbm, w_vmem, sem)
        cp.start()
        cp.wait()


def _ffn_fwd(x, g, w_all, layer, name):
    s = x.shape[0]
    t = min(256, s)

    def body(x_ref, g_ref, w_hbm, xo_ref, gate_ref, up_ref, w_ref, sem):
        _load_weights(w_hbm.at[layer], w_ref, sem)
        x = x_ref[...]
        hn = _rms(x, g_ref[...], D_MODEL)[0].astype(BF16)
        acc = x
        for c in range(2):
            rs = slice(c * FF_HALF, (c + 1) * FF_HALF)
            gt = _nt(hn, w_ref[0, rs, :])
            up = _nt(hn, w_ref[1, rs, :])
            gate_ref[:, rs] = gt.astype(BF16)
            up_ref[:, rs] = up.astype(BF16)
            act = ((gt * _sigmoid(gt)) * up).astype(BF16)
            acc = acc + _nn(act, w_ref[2, rs, :])
        xo_ref[...] = acc

    hid = jax.ShapeDtypeStruct((s, D_FF), BF16)
    return pl.pallas_call(
        body, name=name, grid=(s // t,),
        out_shape=[jax.ShapeDtypeStruct((s, D_MODEL), F32), hid, hid],
        in_specs=[_rows(t, D_MODEL), _full((1, D_MODEL)), ANY],
        out_specs=[_rows(t, D_MODEL), _rows(t, D_FF), _rows(t, D_FF)],
        scratch_shapes=[pltpu.VMEM((3, D_FF, D_MODEL), BF16), pltpu.SemaphoreType.DMA],
        compiler_params=_cparams(1, VMEM_BIG),
    )(x, g, w_all)


def _ffn_bwd(x, dy, gate, up, g, w_all, layer, name):
    s = x.shape[0]
    t = min(256, s)

    def body(x_ref, dy_ref, gate_ref, up_ref, g_ref, w_hbm,
             dx_ref, act_ref, dg_ref, du_ref, hn_ref, dyb_ref, dln_ref, w_ref, sem):
        _load_weights(w_hbm.at[layer], w_ref, sem)
        x = x_ref[...]
        gg = g_ref[...]
        y, r = _rms(x, gg, D_MODEL)
        hn = y.astype(BF16)
        hn_ref[...] = hn
        dy = dy_ref[...]
        dyb = dy.astype(BF16)
        dyb_ref[...] = dyb
        dh = jnp.zeros((t, D_MODEL), F32)
        for c in range(2):
            rs = slice(c * FF_HALF, (c + 1) * FF_HALF)
            gt = gate_ref[:, rs].astype(F32)
            u = up_ref[:, rs].astype(F32)
            sg = _sigmoid(gt)
            sl = gt * sg
            act_ref[:, rs] = (sl * u).astype(BF16)
            dact = _nt(dyb, w_ref[2, rs, :])
            dg = (dact * u * (sg * (1.0 + gt * (1.0 - sg)))).astype(BF16)
            du = (dact * sl).astype(BF16)
            dg_ref[:, rs] = dg
            du_ref[:, rs] = du
            dh = dh + _nn(dg, w_ref[0, rs, :]) + _nn(du, w_ref[1, rs, :])
        dxn, dgl = _rms_bwd(x, r, gg, dh, D_MODEL)
        dx_ref[...] = dy + dxn

        @pl.when(pl.program_id(0) == 0)
        def _():
            dln_ref[...] = jnp.zeros_like(dln_ref)

        dln_ref[...] += dgl

    hid = jax.ShapeDtypeStruct((s, D_FF), BF16)
    tok = jax.ShapeDtypeStruct((s, D_MODEL), BF16)
    return pl.pallas_call(
        body, name=name, grid=(s // t,),
        out_shape=[jax.ShapeDtypeStruct((s, D_MODEL), F32), hid, hid, hid, tok, tok,
                   jax.ShapeDtypeStruct((1, D_MODEL), F32)],
        in_specs=[_rows(t, D_MODEL), _rows(t, D_MODEL), _rows(t, D_FF), _rows(t, D_FF), _full((1, D_MODEL)), ANY],
        out_specs=[_rows(t, D_MODEL), _rows(t, D_FF), _rows(t, D_FF), _rows(t, D_FF),
                   _rows(t, D_MODEL), _rows(t, D_MODEL), _full((1, D_MODEL))],
        scratch_shapes=[pltpu.VMEM((3, D_FF, D_MODEL), BF16), pltpu.SemaphoreType.DMA],
        compiler_params=_cparams(1, VMEM_BIG),
    )(x, dy, gate, up, g, w_all)


def _tn_matmul(a, b, name, groups=1, m_chunk=None):
    s = a.shape[0]
    m, n = a.shape[1] // groups, b.shape[1] // groups
    mc = m if m_chunk is None else m_chunk
    nm = m // mc
    t = min(512, s)

    def body(a_ref, b_ref, o_ref):
        @pl.when(pl.program_id(2) == 0)
        def _():
            o_ref[...] = jnp.zeros_like(o_ref)

        o_ref[...] += _tn(a_ref[...], b_ref[...])

    out = pl.pallas_call(
        body, name=name, grid=(groups, nm, s // t),
        out_shape=jax.ShapeDtypeStruct((groups, m, n), F32),
        in_specs=[pl.BlockSpec((t, mc), lambda gi, mi, ti: (ti, gi * nm + mi)),
                  pl.BlockSpec((t, n), lambda gi, mi, ti: (ti, gi))],
        out_specs=pl.BlockSpec((None, mc, n), lambda gi, mi, ti: (gi, mi, 0)),
        compiler_params=_cparams(3, VMEM_MID),
    )(a, b)
    return out[0] if groups == 1 else out


def _rope_tables(positions):
    half = ROPE // 2
    inv = ROPE_THETA ** (-jnp.arange(half, dtype=F32) * 2.0 / ROPE)
    ang = positions.astype(F32)[:, None] * inv
    cos, sin = jnp.cos(ang), jnp.sin(ang)
    zero = jnp.zeros((positions.shape[0], LANES - ROPE), F32)
    return jnp.concatenate([cos, cos, zero], axis=1), jnp.concatenate([-sin, sin, zero], axis=1)


def _kv_specs(t):
    return [_full((1, D_MODEL)), _full((D_MODEL, KV_RANK)), _full((D_MODEL, LANES)), _full((1, KV_RANK)),
            _full((N_HEADS, KV_RANK, NOPE)), _full((N_HEADS, KV_RANK, V_DIM)),
            _full((1, NOPE)), _full((1, LANES)), _rows(t, LANES), _rows(t, LANES)]


def _kv_fwd(x, ln, wc, wpe, gl, wuk, wuv, gkn, gkr, cos, sin, name):
    s = x.shape[0]
    t = min(256, s)

    def body(x_ref, ln_ref, wc_ref, wpe_ref, gl_ref, wuk_ref, wuv_ref, gkn_ref, gkr_ref, cos_ref, sin_ref,
             k_ref, v_ref):
        hn = _rms(x_ref[...], ln_ref[...], D_MODEL)[0].astype(BF16)
        clat = _nn(hn, wc_ref[...])
        kpe = _nn(hn, wpe_ref[...])
        cn = _rms(clat, gl_ref[...], KV_RANK)[0].astype(BF16)
        sspe = jnp.sum(kpe * kpe, axis=-1, keepdims=True)
        cs, sn = cos_ref[...], sin_ref[...]
        for h in range(N_HEADS):
            kn = _nn(cn, wuk_ref[h])
            r = lax.rsqrt((jnp.sum(kn * kn, axis=-1, keepdims=True) + sspe) * (1.0 / QK_DIM) + EPS)
            k_ref[:, h * QK_PAD:h * QK_PAD + NOPE] = ((kn * r) * gkn_ref[...]).astype(BF16)
            z = (kpe * r) * gkr_ref[...]
            k_ref[:, h * QK_PAD + NOPE:(h + 1) * QK_PAD] = (z * cs + _swap_halves(z) * sn).astype(BF16)
            v_ref[:, h * V_DIM:(h + 1) * V_DIM] = _nn(cn, wuv_ref[h]).astype(BF16)

    return pl.pallas_call(
        body, name=name, grid=(s // t,),
        out_shape=[jax.ShapeDtypeStruct((s, N_HEADS * QK_PAD), BF16), jax.ShapeDtypeStruct((s, N_HEADS * V_DIM), BF16)],
        in_specs=[_rows(t, D_MODEL)] + _kv_specs(t),
        out_specs=[_rows(t, N_HEADS * QK_PAD), _rows(t, N_HEADS * V_DIM)],
        compiler_params=_cparams(1, VMEM_MID),
    )(x, ln, wc, wpe, gl, wuk, wuv, gkn, gkr, cos, sin)


def _kv_bwd(x, dxin, dks, dvs, ln, wc, wpe, gl, wuk, wuv, gkn, gkr, cos, sin, name):
    s = x.shape[0]
    t = min(256, s)
    nk = len(dks)

    def body(*refs):
        x_ref, dxin_ref = refs[:2]
        dk_refs = refs[2:2 + nk]
        dv_refs = refs[2 + nk:2 + 2 * nk]
        (ln_ref, wc_ref, wpe_ref, gl_ref, wuk_ref, wuv_ref, gkn_ref, gkr_ref, cos_ref, sin_ref,
         dx_ref, hn_ref, cn_ref, dkn_ref, dvb_ref, dcc_ref, dpe_ref,
         dln_ref, dgl_ref, dgkn_ref, dgkr_ref) = refs[2 + 2 * nk:]
        x = x_ref[...]
        ln = ln_ref[...]
        y, rx = _rms(x, ln, D_MODEL)
        hn = y.astype(BF16)
        hn_ref[...] = hn
        clat = _nn(hn, wc_ref[...])
        kpe = _nn(hn, wpe_ref[...])
        gl = gl_ref[...]
        cy, rc = _rms(clat, gl, KV_RANK)
        cn = cy.astype(BF16)
        cn_ref[...] = cn
        sspe = jnp.sum(kpe * kpe, axis=-1, keepdims=True)
        cs, sn = cos_ref[...], sin_ref[...]
        gkn, gkr = gkn_ref[...], gkr_ref[...]
        dc = jnp.zeros((t, KV_RANK), F32)
        dkpe = jnp.zeros((t, LANES), F32)
        dgkn = jnp.zeros((1, NOPE), F32)
        dgkr = jnp.zeros((1, LANES), F32)
        for h in range(N_HEADS):
            kn = _nn(cn, wuk_ref[h])
            r = lax.rsqrt((jnp.sum(kn * kn, axis=-1, keepdims=True) + sspe) * (1.0 / QK_DIM) + EPS)
            lo, mid, hi = h * QK_PAD, h * QK_PAD + NOPE, (h + 1) * QK_PAD
            dko = dk_refs[0][:, lo:mid]
            dkr = dk_refs[0][:, mid:hi]
            dvh = dv_refs[0][:, h * V_DIM:(h + 1) * V_DIM]
            for j in range(1, nk):
                dko = dko + dk_refs[j][:, lo:mid]
                dkr = dkr + dk_refs[j][:, mid:hi]
                dvh = dvh + dv_refs[j][:, h * V_DIM:(h + 1) * V_DIM]
            dz = dkr * cs - _swap_halves(dkr) * sn
            un = dko * gkn
            ur = dz * gkr
            sm = (jnp.sum(kn * un, axis=-1, keepdims=True) + jnp.sum(kpe * ur, axis=-1, keepdims=True)) * (1.0 / QK_DIM)
            coef = r * r * r * sm
            dkn = (r * un - kn * coef).astype(BF16)
            dkpe = dkpe + (r * ur - kpe * coef)
            dgkn = dgkn + jnp.sum(dko * (kn * r), axis=0, keepdims=True)
            dgkr = dgkr + jnp.sum(dz * (kpe * r), axis=0, keepdims=True)
            dkn_ref[:, h * NOPE:(h + 1) * NOPE] = dkn
            dvb = dvh.astype(BF16)
            dvb_ref[:, h * V_DIM:(h + 1) * V_DIM] = dvb
            dc = dc + _nt(dkn, wuk_ref[h]) + _nt(dvb, wuv_ref[h])
        dclat, dgl = _rms_bwd(clat, rc, gl, dc, KV_RANK)
        dcc = dclat.astype(BF16)
        dpe = dkpe.astype(BF16)
        dcc_ref[...] = dcc
        dpe_ref[...] = dpe
        dhn = _nt(dcc, wc_ref[...]) + _nt(dpe, wpe_ref[...])
        dxn, dln = _rms_bwd(x, rx, ln, dhn, D_MODEL)
        dx_ref[...] = dxin_ref[...] + dxn

        @pl.when(pl.program_id(0) == 0)
        def _():
            dln_ref[...] = jnp.zeros_like(dln_ref)
            dgl_ref[...] = jnp.zeros_like(dgl_ref)
            dgkn_ref[...] = jnp.zeros_like(dgkn_ref)
            dgkr_ref[...] = jnp.zeros_like(dgkr_ref)

        dln_ref[...] += dln
        dgl_ref[...] += dgl
        dgkn_ref[...] += dgkn
        dgkr_ref[...] += dgkr

    def tok(cols, dt):
        return jax.ShapeDtypeStruct((s, cols), dt)

    def vec(cols):
        return jax.ShapeDtypeStruct((1, cols), F32)

    return pl.pallas_call(
        body, name=name, grid=(s // t,),
        out_shape=[tok(D_MODEL, F32), tok(D_MODEL, BF16), tok(KV_RANK, BF16), tok(N_HEADS * NOPE, BF16),
                   tok(N_HEADS * V_DIM, BF16), tok(KV_RANK, BF16), tok(LANES, BF16),
                   vec(D_MODEL), vec(KV_RANK), vec(NOPE), vec(LANES)],
        in_specs=[_rows(t, D_MODEL), _rows(t, D_MODEL)] + [_rows(t, N_HEADS * QK_PAD)] * nk
                 + [_rows(t, N_HEADS * V_DIM)] * nk + _kv_specs(t),
        out_specs=[_rows(t, D_MODEL), _rows(t, D_MODEL), _rows(t, KV_RANK), _rows(t, N_HEADS * NOPE),
                   _rows(t, N_HEADS * V_DIM), _rows(t, KV_RANK), _rows(t, LANES),
                   _full((1, D_MODEL)), _full((1, KV_RANK)), _full((1, NOPE)), _full((1, LANES))],
        compiler_params=_cparams(1, VMEM_BIG),
    )(x, dxin, *dks, *dvs, ln, wc, wpe, gl, wuk, wuv, gkn, gkr, cos, sin)


def _q_specs(t):
    return [_full((1, D_MODEL)), _full((D_MODEL, Q_RANK)), _full((1, Q_RANK)), _full((N_HEADS, Q_RANK, QK_PAD)),
            _full((1, NOPE)), _full((1, LANES)), _rows(t, LANES), _rows(t, LANES)]


def _q_fwd(x, ln, wdq, gql, wuq, gqn, gqr, cos, sin, name):
    s = x.shape[0]
    t = min(256, s)

    def body(x_ref, ln_ref, wdq_ref, gql_ref, wuq_ref, gqn_ref, gqr_ref, cos_ref, sin_ref, q_ref):
        hn = _rms(x_ref[...], ln_ref[...], D_MODEL)[0].astype(BF16)
        cqn = _rms(_nn(hn, wdq_ref[...]), gql_ref[...], Q_RANK)[0].astype(BF16)
        cs, sn = cos_ref[...], sin_ref[...]
        for h in range(N_HEADS):
            qa = _nn(cqn, wuq_ref[h])
            r = lax.rsqrt(jnp.sum(qa * qa, axis=-1, keepdims=True) * (1.0 / QK_DIM) + EPS)
            q_ref[:, h * QK_PAD:h * QK_PAD + NOPE] = ((qa[:, :NOPE] * r) * gqn_ref[...]).astype(BF16)
            z = (qa[:, NOPE:] * r) * gqr_ref[...]
            q_ref[:, h * QK_PAD + NOPE:(h + 1) * QK_PAD] = (z * cs + _swap_halves(z) * sn).astype(BF16)

    return pl.pallas_call(
        body, name=name, grid=(s // t,),
        out_shape=jax.ShapeDtypeStruct((s, N_HEADS * QK_PAD), BF16),
        in_specs=[_rows(t, D_MODEL)] + _q_specs(t),
        out_specs=_rows(t, N_HEADS * QK_PAD),
        compiler_params=_cparams(1, VMEM_MID),
    )(x, ln, wdq, gql, wuq, gqn, gqr, cos, sin)


def _q_bwd(x, dxin, dq, ln, wdq, gql, wuq, gqn, gqr, cos, sin, name):
    s = x.shape[0]
    t = min(256, s)

    def body(x_ref, dxin_ref, dq_ref, ln_ref, wdq_ref, gql_ref, wuq_ref, gqn_ref, gqr_ref, cos_ref, sin_ref,
             dx_ref, hn_ref, cqn_ref, dqa_ref, dcq_ref, dln_ref, dgql_ref, dgqn_ref, dgqr_ref):
        x = x_ref[...]
        ln = ln_ref[...]
        y, rx = _rms(x, ln, D_MODEL)
        hn = y.astype(BF16)
        hn_ref[...] = hn
        cqp = _nn(hn, wdq_ref[...])
        gql = gql_ref[...]
        cy, rc = _rms(cqp, gql, Q_RANK)
        cqn = cy.astype(BF16)
        cqn_ref[...] = cqn
        cs, sn = cos_ref[...], sin_ref[...]
        gqn, gqr = gqn_ref[...], gqr_ref[...]
        dcq = jnp.zeros((t, Q_RANK), F32)
        dgqn = jnp.zeros((1, NOPE), F32)
        dgqr = jnp.zeros((1, LANES), F32)
        for h in range(N_HEADS):
            qa = _nn(cqn, wuq_ref[h])
            qn, qr = qa[:, :NOPE], qa[:, NOPE:]
            r = lax.rsqrt(jnp.sum(qa * qa, axis=-1, keepdims=True) * (1.0 / QK_DIM) + EPS)
            dqo = dq_ref[:, h * QK_PAD:h * QK_PAD + NOPE]
            dqr = dq_ref[:, h * QK_PAD + NOPE:(h + 1) * QK_PAD]
            dz = dqr * cs - _swap_halves(dqr) * sn
            un = dqo * gqn
            ur = dz * gqr
            sm = (jnp.sum(qn * un, axis=-1, keepdims=True) + jnp.sum(qr * ur, axis=-1, keepdims=True)) * (1.0 / QK_DIM)
            coef = r * r * r * sm
            dqa = jnp.concatenate([r * un - qn * coef, r * ur - qr * coef], axis=1).astype(BF16)
            dgqn = dgqn + jnp.sum(dqo * (qn * r), axis=0, keepdims=True)
            dgqr = dgqr + jnp.sum(dz * (qr * r), axis=0, keepdims=True)
            dqa_ref[:, h * QK_PAD:(h + 1) * QK_PAD] = dqa
            dcq = dcq + _nt(dqa, wuq_ref[h])
        dcqp, dgql = _rms_bwd(cqp, rc, gql, dcq, Q_RANK)
        dcqb = dcqp.astype(BF16)
        dcq_ref[...] = dcqb
        dhn = _nt(dcqb, wdq_ref[...])
        dxn, dln = _rms_bwd(x, rx, ln, dhn, D_MODEL)
        dx_ref[...] = dxin_ref[...] + dxn

        @pl.when(pl.program_id(0) == 0)
        def _():
            dln_ref[...] = jnp.zeros_like(dln_ref)
            dgql_ref[...] = jnp.zeros_like(dgql_ref)
            dgqn_ref[...] = jnp.zeros_like(dgqn_ref)
            dgqr_ref[...] = jnp.zeros_like(dgqr_ref)

        dln_ref[...] += dln
        dgql_ref[...] += dgql
        dgqn_ref[...] += dgqn
        dgqr_ref[...] += dgqr

    def tok(cols, dt):
        return jax.ShapeDtypeStruct((s, cols), dt)

    def vec(cols):
        return jax.ShapeDtypeStruct((1, cols), F32)

    return pl.pallas_call(
        body, name=name, grid=(s // t,),
        out_shape=[tok(D_MODEL, F32), tok(D_MODEL, BF16), tok(Q_RANK, BF16), tok(N_HEADS * QK_PAD, BF16),
                   tok(Q_RANK, BF16), vec(D_MODEL), vec(Q_RANK), vec(NOPE), vec(LANES)],
        in_specs=[_rows(t, D_MODEL), _rows(t, D_MODEL), _rows(t, N_HEADS * QK_PAD)] + _q_specs(t),
        out_specs=[_rows(t, D_MODEL), _rows(t, D_MODEL), _rows(t, Q_RANK), _rows(t, N_HEADS * QK_PAD),
                   _rows(t, Q_RANK), _full((1, D_MODEL)), _full((1, Q_RANK)), _full((1, NOPE)), _full((1, LANES))],
        compiler_params=_cparams(1, VMEM_MID),
    )(x, dxin, dq, ln, wdq, gql, wuq, gqn, gqr, cos, sin)


SM_SCALE = 1.0 / math.sqrt(QK_DIM)
NEG = -1e30


def _chunk_mask(qi, ki, t):
    qpos = qi * t + lax.broadcasted_iota(jnp.int32, (t, t), 0)
    kpos = ki * t + lax.broadcasted_iota(jnp.int32, (t, t), 1)
    return lax.shift_right_logical(kpos, 6) <= lax.shift_right_logical(qpos, 6)


def _att_fwd(q, k, v, name):
    s = q.shape[0]
    t = min(512, s)
    nb = s // t

    def body(q_ref, k_ref, v_ref, o_ref, lse_ref, m_sc, l_sc, acc_sc):
        qi, ki = pl.program_id(1), pl.program_id(2)

        @pl.when(ki == 0)
        def _():
            m_sc[...] = jnp.full_like(m_sc, NEG)
            l_sc[...] = jnp.zeros_like(l_sc)
            acc_sc[...] = jnp.zeros_like(acc_sc)

        @pl.when(ki <= qi)
        def _():
            sc = _nt(q_ref[...], k_ref[...]) * SM_SCALE
            sc = jnp.where(_chunk_mask(qi, ki, t), sc, NEG)
            m_old = m_sc[...]
            m_new = jnp.maximum(m_old, jnp.max(sc, axis=-1, keepdims=True))
            p = jnp.exp(sc - m_new)
            alpha = jnp.exp(m_old - m_new)
            l_sc[...] = alpha * l_sc[...] + jnp.sum(p, axis=-1, keepdims=True)
            acc_sc[...] = alpha * acc_sc[...] + _nn(p.astype(BF16), v_ref[...])
            m_sc[...] = m_new

        @pl.when(ki == qi)
        def _():
            o_ref[...] = (acc_sc[...] / l_sc[...]).astype(BF16)
            lse_ref[...] = jnp.broadcast_to(m_sc[...] + jnp.log(l_sc[...]), (t, LANES))

    return pl.pallas_call(
        body, name=name, grid=(N_HEADS, nb, nb),
        out_shape=[jax.ShapeDtypeStruct((s, N_HEADS * V_DIM), BF16), jax.ShapeDtypeStruct((s, N_HEADS * LANES), F32)],
        in_specs=[pl.BlockSpec((t, QK_PAD), lambda h, qi, ki: (qi, h)),
                  pl.BlockSpec((t, QK_PAD), lambda h, qi, ki: (jnp.minimum(ki, qi), h)),
                  pl.BlockSpec((t, V_DIM), lambda h, qi, ki: (jnp.minimum(ki, qi), h))],
        out_specs=[pl.BlockSpec((t, V_DIM), lambda h, qi, ki: (qi, h)),
                   pl.BlockSpec((t, LANES), lambda h, qi, ki: (qi, h))],
        scratch_shapes=[pltpu.VMEM((t, 1), F32), pltpu.VMEM((t, 1), F32), pltpu.VMEM((t, V_DIM), F32)],
        compiler_params=_cparams(3, VMEM_MID),
    )(q, k, v)


def _att_bwd(q, k, v, do, o, lse, name):
    s = q.shape[0]
    t = min(512, s)
    nb = s // t

    def body(q_ref, k_ref, v_ref, do_ref, o_ref, lse_ref, dq_ref, dk_ref, dv_ref):
        ki, qi = pl.program_id(1), pl.program_id(2)

        @pl.when(qi == 0)
        def _():
            dk_ref[...] = jnp.zeros_like(dk_ref)
            dv_ref[...] = jnp.zeros_like(dv_ref)

        @pl.when(qi >= ki)
        def _():
            qq, kk = q_ref[...], k_ref[...]
            dob = do_ref[...]
            sc = _nt(qq, kk) * SM_SCALE
            sc = jnp.where(_chunk_mask(qi, ki, t), sc, NEG)
            p = jnp.exp(sc - lse_ref[:, :1])
            dp = _nt(dob, v_ref[...])
            dsum = jnp.sum(dob.astype(F32) * o_ref[...].astype(F32), axis=-1, keepdims=True)
            ds = (p * (dp - dsum) * SM_SCALE).astype(BF16)
            dv_ref[...] += _tn(p.astype(BF16), dob)
            dk_ref[...] += _tn(ds, qq)
            rows = pl.ds(pl.multiple_of(qi * t, t), t)
            dqc = _nn(ds, kk)

            @pl.when(ki == 0)
            def _():
                dq_ref[rows, :] = dqc

            @pl.when(ki > 0)
            def _():
                dq_ref[rows, :] += dqc

    def qmap(h, ki, qi):
        return (jnp.maximum(qi, ki), h)

    return pl.pallas_call(
        body, name=name, grid=(N_HEADS, nb, nb),
        out_shape=[jax.ShapeDtypeStruct((s, N_HEADS * QK_PAD), F32), jax.ShapeDtypeStruct((s, N_HEADS * QK_PAD), F32),
                   jax.ShapeDtypeStruct((s, N_HEADS * V_DIM), F32)],
        in_specs=[pl.BlockSpec((t, QK_PAD), qmap),
                  pl.BlockSpec((t, QK_PAD), lambda h, ki, qi: (ki, h)),
                  pl.BlockSpec((t, V_DIM), lambda h, ki, qi: (ki, h)),
                  pl.BlockSpec((t, V_DIM), qmap), pl.BlockSpec((t, V_DIM), qmap), pl.BlockSpec((t, LANES), qmap)],
        out_specs=[pl.BlockSpec((s, QK_PAD), lambda h, ki, qi: (0, h)),
                   pl.BlockSpec((t, QK_PAD), lambda h, ki, qi: (ki, h)),
                   pl.BlockSpec((t, V_DIM), lambda h, ki, qi: (ki, h))],
        compiler_params=_cparams(3, VMEM_MID),
    )(q, k, v, do, o, lse)


def _o_fwd(x, o, wo, name):
    s = x.shape[0]
    t = min(512, s)

    def body(x_ref, o_ref, wo_ref, xo_ref):
        xo_ref[...] = x_ref[...] + _nn(o_ref[...], wo_ref[...])

    return pl.pallas_call(
        body, name=name, grid=(s // t,),
        out_shape=jax.ShapeDtypeStruct((s, D_MODEL), F32),
        in_specs=[_rows(t, D_MODEL), _rows(t, D_MODEL), _full((D_MODEL, D_MODEL))],
        out_specs=_rows(t, D_MODEL),
        compiler_params=_cparams(1, VMEM_MID),
    )(x, o, wo)


def _o_bwd(dx, wo, name):
    s = dx.shape[0]
    t = min(512, s)

    def body(dx_ref, wo_ref, do_ref, dxb_ref):
        dxb = dx_ref[...].astype(BF16)
        dxb_ref[...] = dxb
        do_ref[...] = _nt(dxb, wo_ref[...]).astype(BF16)

    tok = jax.ShapeDtypeStruct((s, D_MODEL), BF16)
    return pl.pallas_call(
        body, name=name, grid=(s // t,),
        out_shape=[tok, tok],
        in_specs=[_rows(t, D_MODEL), _full((D_MODEL, D_MODEL))],
        out_specs=[_rows(t, D_MODEL), _rows(t, D_MODEL)],
        compiler_params=_cparams(1, VMEM_MID),
    )(dx, wo)


def _loss_head(y, target, name):
    s = y.shape[0]
    t = min(512, s)

    def body(y_ref, t_ref, dy_ref, sq_ref):
        e = y_ref[...] - t_ref[...]
        dy_ref[...] = e * (1.0 / D_MODEL)

        @pl.when(pl.program_id(0) == 0)
        def _():
            sq_ref[...] = jnp.zeros_like(sq_ref)

        sq_ref[...] += jnp.sum(e * e, axis=0, keepdims=True)

    return pl.pallas_call(
        body, name=name, grid=(s // t,),
        out_shape=[jax.ShapeDtypeStruct((s, D_MODEL), F32), jax.ShapeDtypeStruct((1, D_MODEL), F32)],
        in_specs=[_rows(t, D_MODEL), _rows(t, D_MODEL)],
        out_specs=[_rows(t, D_MODEL), _full((1, D_MODEL))],
        compiler_params=_cparams(1),
    )(y, target)


def _adamw(w, g, m, v, name):
    shape = w.shape
    c = shape[-1]
    r = math.prod(shape[:-1])
    tb = r
    for cand in (512, 256, 128):
        if r % cand == 0 and r > cand:
            tb = cand
            break

    def body(w_ref, g_ref, m_ref, v_ref, d_ref, mo_ref, vo_ref):
        gr = g_ref[...]
        mn = ADAM_B1 * m_ref[...] + (1.0 - ADAM_B1) * gr
        vn = ADAM_B2 * v_ref[...] + (1.0 - ADAM_B2) * (gr * gr)
        m_hat = mn / (1.0 - ADAM_B1 ** ADAM_STEP)
        v_hat = vn / (1.0 - ADAM_B2 ** ADAM_STEP)
        d_ref[...] = -ADAM_LR * (m_hat / (jnp.sqrt(v_hat) + ADAM_EPS) + ADAM_WD * w_ref[...])
        mo_ref[...] = mn
        vo_ref[...] = vn

    spec = pl.BlockSpec((tb, c), lambda i: (i, 0))
    flat = jax.ShapeDtypeStruct((r, c), F32)
    outs = pl.pallas_call(
        body, name=name, grid=(r // tb,),
        out_shape=[flat, flat, flat],
        in_specs=[spec] * 4, out_specs=[spec] * 3,
        compiler_params=_cparams(1),
    )(w.reshape(r, c), g.reshape(r, c), m.reshape(r, c), v.reshape(r, c))
    return [a.reshape(shape) for a in outs]


def _pad_cols(a, width):
    return jnp.pad(a, [(0, 0)] * (a.ndim - 1) + [(0, width - a.shape[-1])])


def _owner_view(a, sz):
    return a.reshape(a.shape[0], N_CHIPS, 2, sz, a.shape[-1])


def kernel(x, positions, ln_mix_a, w_pool, b_pool, pool_scale, ln_ffn, w_gate, w_up, w_down, ln_kv, w_dkv, g_kv_latent, w_uk, w_uv, g_k, ln_mix_b, w_dq, g_q_latent, w_uq, g_q, w_o, loss_target, m_ln_mix_a, m_w_pool, m_b_pool, m_pool_scale, m_ln_ffn, m_w_gate, m_w_up, m_w_down, m_ln_kv, m_w_dkv, m_g_kv_latent, m_w_uk, m_w_uv, m_g_k, m_ln_mix_b, m_w_dq, m_g_q_latent, m_w_uq, m_g_q, m_w_o, v_ln_mix_a, v_w_pool, v_b_pool, v_pool_scale, v_ln_ffn, v_w_gate, v_w_up, v_w_down, v_ln_kv, v_w_dkv, v_g_kv_latent, v_w_uk, v_w_uv, v_g_k, v_ln_mix_b, v_w_dq, v_g_q_latent, v_w_uq, v_g_q, v_w_o):
    weights = dict(ln_mix_a=ln_mix_a, w_pool=w_pool, b_pool=b_pool, pool_scale=pool_scale, ln_ffn=ln_ffn,
                   w_gate=w_gate, w_up=w_up, w_down=w_down, ln_kv=ln_kv, w_dkv=w_dkv, g_kv_latent=g_kv_latent,
                   w_uk=w_uk, w_uv=w_uv, g_k=g_k, ln_mix_b=ln_mix_b, w_dq=w_dq, g_q_latent=g_q_latent,
                   w_uq=w_uq, g_q=g_q, w_o=w_o)
    mom1 = dict(ln_mix_a=m_ln_mix_a, w_pool=m_w_pool, b_pool=m_b_pool, pool_scale=m_pool_scale, ln_ffn=m_ln_ffn,
                w_gate=m_w_gate, w_up=m_w_up, w_down=m_w_down, ln_kv=m_ln_kv, w_dkv=m_w_dkv,
                g_kv_latent=m_g_kv_latent, w_uk=m_w_uk, w_uv=m_w_uv, g_k=m_g_k, ln_mix_b=m_ln_mix_b, w_dq=m_w_dq,
                g_q_latent=m_g_q_latent, w_uq=m_w_uq, g_q=m_g_q, w_o=m_w_o)
    mom2 = dict(ln_mix_a=v_ln_mix_a, w_pool=v_w_pool, b_pool=v_b_pool, pool_scale=v_pool_scale, ln_ffn=v_ln_ffn,
                w_gate=v_w_gate, w_up=v_w_up, w_down=v_w_down, ln_kv=v_ln_kv, w_dkv=v_w_dkv,
                g_kv_latent=v_g_kv_latent, w_uk=v_w_uk, w_uv=v_w_uv, g_k=v_g_k, ln_mix_b=v_ln_mix_b, w_dq=v_w_dq,
                g_q_latent=v_g_q_latent, w_uq=v_w_uq, g_q=v_g_q, w_o=v_w_o)
    names = list(weights)
    dev = 4 * lax.axis_index("x") + 2 * lax.axis_index("y") + lax.axis_index("c")

    xs = x[0]
    target = loss_target[0]
    cos, sin = _rope_tables(positions[0])

    ffn_sh = jnp.stack([w_gate.transpose(0, 2, 1), w_up.transpose(0, 2, 1), w_down], axis=1).astype(BF16)
    small_sh = jnp.concatenate([ln_mix_a.reshape(1, -1), pool_scale.reshape(1, -1), b_pool.reshape(1, -1)], axis=1)
    shards = [ffn_sh, w_pool.astype(BF16), w_dkv[:, :KV_RANK].astype(BF16),
              _pad_cols(w_dkv[:, KV_RANK:], LANES).astype(BF16), w_uk.astype(BF16), w_uv.astype(BF16),
              w_dq.astype(BF16), _pad_cols(w_uq, QK_PAD).astype(BF16), w_o.astype(BF16), small_sh]
    axes = [2, 2, 0, 0, 0, 0, 1, 0, 1, 0]
    (ffn_g, wp_g, wc_g, wpe_g, wuk_g, wuv_g, wdq_g, wuq_g, wo_g, small_g) = _all_gather(shards, axes, "gather_weights")
    w_ffn = ffn_g.reshape(4, 3, D_FF, D_MODEL)
    wp_all = wp_g.reshape(2, 4, GROUP_DIM, GROUP_DIM)
    wc = wc_g.reshape(D_MODEL, KV_RANK)
    wpe = wpe_g.reshape(D_MODEL, LANES)
    wdq_all = wdq_g.reshape(2, D_MODEL, Q_RANK)
    wo_all = wo_g.reshape(2, D_MODEL, D_MODEL)
    small_g = small_g.reshape(N_DEV, 3, 2, LANES)
    ln_a_all = small_g[:, 0].transpose(1, 0, 2).reshape(2, 1, D_MODEL)
    sc_all = small_g[:, 1].transpose(1, 0, 2).reshape(2, 1, D_MODEL)
    bp_all = small_g[:, 2].reshape(N_DEV, 2, 4, 32).transpose(1, 2, 0, 3).reshape(2, 1, D_MODEL)

    gkn = g_k[:NOPE].reshape(1, NOPE)
    gkr = _pad_cols(g_k[NOPE:].reshape(1, ROPE), LANES)
    gl = g_kv_latent.reshape(1, KV_RANK)
    lnkv = ln_kv.reshape(1, D_MODEL)

    x_in, x_mid, pooled, gates, ups = [], [], [], [], []
    qs, outs, lses = [], [], []
    cur = xs
    for l in range(4):
        x_in.append(cur)
        if l < 2:
            mid, dsave = _mix_fwd(cur, ln_a_all[l], wp_all[l], bp_all[l], sc_all[l], f"mix_fwd{l}")
            pooled.append(dsave)
        else:
            j = l - 2
            q = _q_fwd(cur, ln_mix_b[j].reshape(1, -1), wdq_all[j], g_q_latent[j].reshape(1, -1), wuq_g[:, j],
                       g_q[j, :NOPE].reshape(1, -1), _pad_cols(g_q[j, NOPE:].reshape(1, -1), LANES), cos, sin,
                       f"q_fwd{j}")
            o, lse = _att_fwd(q, k_sh, v_sh, f"att_fwd{j}")
            mid = _o_fwd(cur, o, wo_all[j], f"o_fwd{j}")
            qs.append(q)
            outs.append(o)
            lses.append(lse)
        x_mid.append(mid)
        cur, gate, up = _ffn_fwd(mid, ln_ffn[l].reshape(1, -1), w_ffn, l, f"ffn_fwd{l}")
        gates.append(gate)
        ups.append(up)
        if l == 1:
            x_kv = cur
            k_sh, v_sh = _kv_fwd(cur, lnkv, wc, wpe, gl, wuk_g, wuv_g, gkn, gkr, cos, sin, "kv_fwd")

    dx, sq_cols = _loss_head(cur, target, "loss_head")

    small = {}
    big = {}
    dks, dvs = [], []
    for l in (3, 2, 1, 0):
        dx, act, dgb, dub, hn, dyb, dln = _ffn_bwd(x_mid[l], dx, gates[l], ups[l], ln_ffn[l].reshape(1, -1),
                                                     w_ffn, l, f"ffn_bwd{l}")
        small[f"ln_ffn{l}"] = dln
        big[f"ffn{l}"] = jnp.stack([
            _tn_matmul(dgb, hn, f"dw_gate{l}", m_chunk=FF_HALF),
            _tn_matmul(dub, hn, f"dw_up{l}", m_chunk=FF_HALF),
            _tn_matmul(act, dyb, f"dw_down{l}", m_chunk=FF_HALF)])
        if l >= 2:
            j = l - 2
            do, dxb = _o_bwd(dx, wo_all[j], f"o_bwd{j}")
            big[f"wo{j}"] = _tn_matmul(outs[j], dxb, f"dw_o{j}")
            dq, dk, dv = _att_bwd(qs[j], k_sh, v_sh, do, outs[j], lses[j], f"att_bwd{j}")
            dks.append(dk)
            dvs.append(dv)
            dx, hnq, cqn, dqa, dcq, dln, dgql, dgqn, dgqr = _q_bwd(
                x_in[l], dx, dq, ln_mix_b[j].reshape(1, -1), wdq_all[j], g_q_latent[j].reshape(1, -1), wuq_g[:, j],
                g_q[j, :NOPE].reshape(1, -1), _pad_cols(g_q[j, NOPE:].reshape(1, -1), LANES), cos, sin, f"q_bwd{j}")
            small[f"ln_mix_b{j}"] = dln
            small[f"g_q_latent{j}"] = dgql
            small[f"g_q{j}"] = jnp.concatenate([dgqn, dgqr[:, :ROPE]], axis=1)
            big[f"wdq{j}"] = _tn_matmul(hnq, dcq, f"dw_dq{j}")
            big[f"wuqT{j}"] = _tn_matmul(dqa, cqn, f"dw_uq{j}")
            if l == 2:
                (dx, hnk, cn, dknb, dvb, dccb, dpeb, dlnkv, dgl, dgkn, dgkr) = _kv_bwd(
                    x_kv, dx, dks, dvs, lnkv, wc, wpe, gl, wuk_g, wuv_g, gkn, gkr, cos, sin, "kv_bwd")
                small["ln_kv"] = dlnkv
                small["g_kv_latent"] = dgl
                small["g_k"] = jnp.concatenate([dgkn, dgkr[:, :ROPE]], axis=1)
                big["kv512"] = jnp.stack([_tn_matmul(dknb, cn, "dw_uk"), _tn_matmul(dvb, cn, "dw_uv"),
                                          _tn_matmul(hnk, dccb, "dw_dkv_c")])
                big["dkv_pe"] = _tn_matmul(hnk, dpeb, "dw_dkv_pe")
        else:
            dx, dyp, dsc, db, dln = _mix_bwd(x_in[l], dx, pooled[l], ln_a_all[l], wp_all[l], bp_all[l], sc_all[l],
                                             f"mix_bwd{l}")
            small[f"ln_mix_a{l}"] = dln
            small[f"pool_scale{l}"] = dsc
            small[f"b_pool{l}"] = db
            big[f"wpool{l}"] = _tn_matmul(pooled[l], dyp, f"dw_pool{l}", groups=4)
    grad_x = dx[None]

    rs_names = ["ffn0", "ffn1", "ffn2", "ffn3", "wpool0", "wpool1", "kv512", "dkv_pe", "wdq", "wuqT", "wo"]
    big["wdq"] = jnp.stack([big["wdq0"], big["wdq1"]])
    big["wuqT"] = jnp.stack([big["wuqT0"], big["wuqT1"]])
    big["wo"] = jnp.stack([big["wo0"], big["wo1"]])
    big["dkv_pe"] = big["dkv_pe"][None]
    sizes = dict(ffn0=FF_SHARD, ffn1=FF_SHARD, ffn2=FF_SHARD, ffn3=FF_SHARD, wpool0=32, wpool1=32, kv512=128,
                 dkv_pe=128, wdq=128, wuqT=QK_PAD, wo=128)
    views = [_owner_view(big[nm], sizes[nm]) for nm in rs_names]
    pairs = _pair_exchange(views, "reduce_pair")
    chip_parts = [_sum_lead(p, f"pair_sum_{nm}") for p, nm in zip(pairs, rs_names)]
    landed = _chip_exchange(chip_parts, "reduce_chips")
    red = {nm: _sum_lead(a, f"chip_sum_{nm}") for a, nm in zip(landed, rs_names)}

    vec_names = (["loss"] + [f"ln_ffn{l}" for l in range(4)] + ["ln_kv", "g_kv_latent", "g_k"]
                 + [f"{p}{j}" for p in ("ln_mix_b", "g_q_latent", "g_q") for j in range(2)]
                 + [f"{p}{l}" for p in ("ln_mix_a", "pool_scale", "b_pool") for l in range(2)])
    small["loss"] = sq_cols
    widths = [small[nm].shape[1] for nm in vec_names]
    padded = [-(-w // LANES) * LANES for w in widths]
    packed = jnp.concatenate([_pad_cols(small[nm], pw) for nm, pw in zip(vec_names, padded)], axis=1)
    (all_vecs,) = _all_gather([packed], [0], "gather_vectors")
    total = _sum_lead(all_vecs, "sum_vectors")
    vec = {}
    off = 0
    for nm, w, pw in zip(vec_names, widths, padded):
        vec[nm] = total[0, off:off + w]
        off += pw
    loss = 0.5 * jnp.sum(vec["loss"]) * (1.0 / D_MODEL)

    def own_cols(full, width):
        return lax.dynamic_slice_in_dim(full, dev * width, width, axis=full.ndim - 1)

    ffn_red = [red[f"ffn{l}"] for l in range(4)]
    grads = dict(
        ln_mix_a=own_cols(jnp.stack([vec["ln_mix_a0"], vec["ln_mix_a1"]]), LANES),
        w_pool=jnp.stack([red["wpool0"], red["wpool1"]]),
        b_pool=own_cols(jnp.stack([vec["b_pool0"], vec["b_pool1"]]).reshape(2, 4, GROUP_DIM), 32),
        pool_scale=own_cols(jnp.stack([vec["pool_scale0"], vec["pool_scale1"]]), LANES),
        ln_ffn=jnp.stack([vec[f"ln_ffn{l}"] for l in range(4)]),
        w_gate=jnp.stack([f[0].T for f in ffn_red]),
        w_up=jnp.stack([f[1].T for f in ffn_red]),
        w_down=jnp.stack([f[2] for f in ffn_red]),
        ln_kv=vec["ln_kv"],
        w_dkv=jnp.concatenate([red["kv512"][2], red["dkv_pe"][0][:, :ROPE]], axis=1),
        g_kv_latent=vec["g_kv_latent"],
        w_uk=red["kv512"][0].T,
        w_uv=red["kv512"][1].T,
        g_k=vec["g_k"],
        ln_mix_b=jnp.stack([vec["ln_mix_b0"], vec["ln_mix_b1"]]),
        w_dq=red["wdq"],
        g_q_latent=jnp.stack([vec["g_q_latent0"], vec["g_q_latent1"]]),
        w_uq=red["wuqT"].transpose(0, 2, 1)[:, :, :QK_DIM],
        g_q=jnp.stack([vec["g_q0"], vec["g_q1"]]),
        w_o=red["wo"],
    )

    deltas, new_m, new_v = {}, {}, {}
    for nm in names:
        w = weights[nm]
        shape = w.shape if w.ndim > 1 else (1, w.shape[0])
        d, mo, vo = _adamw(w.reshape(shape), grads[nm].reshape(shape), mom1[nm].reshape(shape),
                           mom2[nm].reshape(shape), f"adamw_{nm}")
        deltas[nm], new_m[nm], new_v[nm] = d.reshape(w.shape), mo.reshape(w.shape), vo.reshape(w.shape)

    return (loss, grad_x, *[grads[nm].reshape(weights[nm].shape) for nm in names], *[deltas[nm] for nm in names],
            *[new_m[nm] for nm in names], *[new_v[nm] for nm in names])
```

```python
import functools
import math

import jax
import jax.numpy as jnp
from jax import lax
from jax.experimental import pallas as pl
from jax.experimental.pallas import tpu as pltpu

F32 = jnp.float32
BF16 = jnp.bfloat16
MESH = pl.DeviceIdType.MESH

D_MODEL = 1024
D_FF = 2816
N_DEV = 8
N_CHIPS = 4
FF_SHARD = D_FF // N_DEV
FF_HALF = D_FF // 2
N_HEADS = 8
NOPE = 128
ROPE = 64
QK_DIM = NOPE + ROPE
QK_PAD = 256
V_DIM = 128
Q_RANK = 256
KV_RANK = 512
POOL_WINDOWS = (2, 4, 8, 16)
GROUP_DIM = 256
HALO = 128
CHUNK = 64
ROPE_THETA = 10000.0
EPS = 1e-6
LANES = 128

ADAM_LR = 0.001
ADAM_B1 = 0.9
ADAM_B2 = 0.999
ADAM_EPS = 1e-08
ADAM_WD = 0.01
ADAM_STEP = 10

VMEM_BIG = 56 * 2**20
VMEM_MID = 40 * 2**20


def _nn(a, b):
    return lax.dot_general(a, b, (((1,), (0,)), ((), ())), preferred_element_type=F32)


def _nt(a, b):
    return lax.dot_general(a, b, (((1,), (1,)), ((), ())), preferred_element_type=F32)


def _tn(a, b):
    return lax.dot_general(a, b, (((0,), (0,)), ((), ())), preferred_element_type=F32)


def _rms(x, g, n):
    r = lax.rsqrt(jnp.sum(x * x, axis=-1, keepdims=True) * (1.0 / n) + EPS)
    return (x * r) * g, r


def _rms_bwd(x, r, g, dy, n):
    u = dy * g
    s = jnp.sum(x * u, axis=-1, keepdims=True) * (1.0 / n)
    dx = r * u - x * (r * r * r * s)
    dg = jnp.sum(dy * (x * r), axis=0, keepdims=True)
    return dx, dg


def _swap_halves(z):
    lane = lax.broadcasted_iota(jnp.int32, z.shape, 1)
    return jnp.where(lane < ROPE // 2, pltpu.roll(z, LANES - ROPE // 2, 1), pltpu.roll(z, ROPE // 2, 1))


def _sigmoid(x):
    return 1.0 / (1.0 + jnp.exp(-x))


def _cparams(n_grid, vmem=None):
    return pltpu.CompilerParams(dimension_semantics=("arbitrary",) * n_grid, vmem_limit_bytes=vmem)


def _rows(t, cols):
    return pl.BlockSpec((t, cols), lambda i: (i, 0))


def _full(shape):
    nd = len(shape)
    return pl.BlockSpec(shape, lambda *_: (0,) * nd)


ANY = pl.BlockSpec(memory_space=pl.ANY)


def _place():
    x, y, c = lax.axis_index("x"), lax.axis_index("y"), lax.axis_index("c")
    return x, y, c


def _all_gather(shards, axes, name):
    n = len(shards)
    out_shape = [jax.ShapeDtypeStruct(s.shape[:a] + (N_DEV,) + s.shape[a:], s.dtype) for s, a in zip(shards, axes)]

    def body(*refs):
        ins, outs = refs[:n], refs[n:2 * n]
        send_sems, recv_sems, local_sems = refs[2 * n:]
        x, y, c = _place()
        me, sibling = (x, y, c), (x, y, 1 - c)
        chips = [(1 - x, y), (x, 1 - y), (1 - x, 1 - y)]

        def slot(t, dev):
            idx = 4 * dev[0] + 2 * dev[1] + dev[2]
            return outs[t].at[(slice(None),) * axes[t] + (idx,)]

        def copy(t, k, block, to, src=None):
            return pltpu.make_async_remote_copy(
                src_ref=slot(t, block) if src is None else src, dst_ref=slot(t, block),
                send_sem=send_sems.at[t, k], recv_sem=recv_sems.at[t, k],
                device_id=to, device_id_type=MESH)

        mine = [pltpu.make_async_copy(ins[t], slot(t, me), local_sems.at[t]) for t in range(n)]
        for cp in mine:
            cp.start()
        first = []
        for t in range(n):
            first.append(copy(t, 0, me, sibling, src=ins[t]))
            first += [copy(t, 1 + j, me, (*chip, c), src=ins[t]) for j, chip in enumerate(chips)]
        for cp in first:
            cp.start()
        passed = []
        for j, chip in enumerate(chips):
            for t in range(n):
                copy(t, 1 + j, (*chip, c), me).wait_recv()
                cp = copy(t, 4 + j, (*chip, c), sibling)
                cp.start()
                passed.append(cp)
        for t in range(n):
            copy(t, 0, sibling, me).wait_recv()
            for j, chip in enumerate(chips):
                copy(t, 4 + j, (*chip, 1 - c), me).wait_recv()
        for cp in first + passed:
            cp.wait_send()
        for cp in mine:
            cp.wait()

    return pl.pallas_call(
        body, name=name, out_shape=out_shape,
        in_specs=[ANY] * n, out_specs=[ANY] * n,
        scratch_shapes=[pltpu.SemaphoreType.DMA((n, 7)), pltpu.SemaphoreType.DMA((n, 7)),
                        pltpu.SemaphoreType.DMA((n,))],
    )(*shards)


def _pair_exchange(grads, name):
    n = len(grads)
    out_shape = [jax.ShapeDtypeStruct((g.shape[0], N_CHIPS) + g.shape[3:], g.dtype) for g in grads]

    def body(*refs):
        ins, outs = refs[:n], refs[n:2 * n]
        send_sems, recv_sems = refs[2 * n:]
        x, y, c = _place()
        sibling = (x, y, 1 - c)
        for t in range(n):
            for p in range(ins[t].shape[0]):
                for k in range(N_CHIPS):
                    pltpu.make_async_remote_copy(
                        src_ref=ins[t].at[p, k, 1 - c], dst_ref=outs[t].at[p, k],
                        send_sem=send_sems.at[t], recv_sem=recv_sems.at[t],
                        device_id=sibling, device_id_type=MESH).start()
        for t in range(n):
            pltpu.make_async_remote_copy(
                src_ref=outs[t], dst_ref=outs[t], send_sem=send_sems.at[t], recv_sem=recv_sems.at[t],
                device_id=sibling, device_id_type=MESH).wait()

    return pl.pallas_call(
        body, name=name, out_shape=out_shape,
        in_specs=[ANY] * n, out_specs=[ANY] * n,
        scratch_shapes=[pltpu.SemaphoreType.DMA((n,)), pltpu.SemaphoreType.DMA((n,))],
    )(*grads)


def _chip_exchange(parts, name):
    n = len(parts)
    out_shape = [jax.ShapeDtypeStruct((3, p.shape[0]) + p.shape[2:], p.dtype) for p in parts]

    def body(*refs):
        ins, outs = refs[:n], refs[n:2 * n]
        send_sems, recv_sems = refs[2 * n:]
        x, y, c = _place()
        chips = [(1 - x, y), (x, 1 - y), (1 - x, 1 - y)]
        for t in range(n):
            for j, (px, py) in enumerate(chips):
                for p in range(ins[t].shape[0]):
                    pltpu.make_async_remote_copy(
                        src_ref=ins[t].at[p, 2 * px + py], dst_ref=outs[t].at[j, p],
                        send_sem=send_sems.at[t, j], recv_sem=recv_sems.at[t, j],
                        device_id=(px, py, c), device_id_type=MESH).start()
        for t in range(n):
            for j, (px, py) in enumerate(chips):
                pltpu.make_async_remote_copy(
                    src_ref=outs[t].at[j], dst_ref=outs[t].at[j],
                    send_sem=send_sems.at[t, j], recv_sem=recv_sems.at[t, j],
                    device_id=(px, py, c), device_id_type=MESH).wait()

    return pl.pallas_call(
        body, name=name, out_shape=out_shape,
        in_specs=[ANY] * n, out_specs=[ANY] * n,
        scratch_shapes=[pltpu.SemaphoreType.DMA((n, 3)), pltpu.SemaphoreType.DMA((n, 3))],
    )(*parts)


def _pair_sum(grad, landed, core, name):
    p, _, _, sz, c = grad.shape

    def body(core_ref, g_ref, l_ref, o_ref):
        o_ref[...] = (g_ref[...].astype(F32) + l_ref[...].astype(F32)).astype(o_ref.dtype)

    out = pl.pallas_call(
        body, name=name,
        grid_spec=pltpu.PrefetchScalarGridSpec(
            num_scalar_prefetch=1, grid=(p * N_CHIPS,),
            in_specs=[pl.BlockSpec((None, None, sz, c), lambda i, cr: (i, cr[0], 0, 0)),
                      pl.BlockSpec((None, sz, c), lambda i, cr: (i, 0, 0))],
            out_specs=pl.BlockSpec((None, sz, c), lambda i, cr: (i, 0, 0))),
        out_shape=jax.ShapeDtypeStruct((p * N_CHIPS, sz, c), grad.dtype),
        compiler_params=_cparams(1),
    )(core, grad.reshape(p * N_CHIPS, 2, sz, c), landed.reshape(p * N_CHIPS, sz, c))
    return out.reshape(p, N_CHIPS, sz, c)


def _chip_sum(parts, landed, chip, name):
    p, _, sz, c = parts.shape

    def body(chip_ref, a_ref, l_ref, o_ref):
        acc = a_ref[...].astype(F32)
        for j in range(3):
            acc = acc + l_ref[j].astype(F32)
        o_ref[...] = acc

    return pl.pallas_call(
        body, name=name,
        grid_spec=pltpu.PrefetchScalarGridSpec(
            num_scalar_prefetch=1, grid=(p,),
            in_specs=[pl.BlockSpec((None, None, sz, c), lambda i, cr: (i, cr[0], 0, 0)),
                      pl.BlockSpec((3, None, sz, c), lambda i, cr: (0, i, 0, 0))],
            out_specs=pl.BlockSpec((None, sz, c), lambda i, cr: (i, 0, 0))),
        out_shape=jax.ShapeDtypeStruct((p, sz, c), F32),
        compiler_params=_cparams(1),
    )(chip, parts, landed)


def _sum_lead(a, name, out_dtype=F32):
    k = a.shape[0]
    rest = a.shape[1:]
    r, c = rest[-2], rest[-1]
    lead = math.prod(rest[:-2])
    a3 = a.reshape(k, lead * r, c)
    rows = lead * r
    tb = rows
    for cand in (512, 256, 128, 64, 32, 16, 8):
        if rows % cand == 0 and rows > cand:
            tb = cand
            break

    def body(a_ref, o_ref):
        acc = a_ref[0].astype(F32)
        for i in range(1, k):
            acc = acc + a_ref[i].astype(F32)
        o_ref[...] = acc.astype(out_dtype)

    out = pl.pallas_call(
        body, name=name, grid=(rows // tb,),
        out_shape=jax.ShapeDtypeStruct((rows, c), out_dtype),
        in_specs=[pl.BlockSpec((k, tb, c), lambda i: (0, i, 0))],
        out_specs=pl.BlockSpec((tb, c), lambda i: (i, 0)),
        compiler_params=_cparams(1),
    )(a3)
    return out.reshape(rest)


def _band(t, w, offset, valid):
    r = lax.broadcasted_iota(jnp.int32, (t, t + HALO), 0)
    col = lax.broadcasted_iota(jnp.int32, (t, t + HALO), 1)
    diff = offset(r, col)
    return jnp.where((diff >= 0) & (diff < w) & valid(col), 1.0, 0.0).astype(BF16)


def _split_dot(band, v):
    hi = v.astype(BF16)
    lo = (v - hi.astype(F32)).astype(BF16)
    return _nn(band, hi) + _nn(band, lo)


def _mix_fwd(x, g, wp, b, sc, name):
    s = x.shape[0]
    t = min(256, s)
    rb = t // HALO

    def body(x_ref, xh_ref, g_ref, wp_ref, b_ref, sc_ref, xo_ref, d_ref):
        i = pl.program_id(0)
        gg = g_ref[...]
        h, _ = _rms(x_ref[...], gg, D_MODEL)
        hh, _ = _rms(xh_ref[...], gg, D_MODEL)
        hext = jnp.concatenate([hh, h], axis=0)
        tok = i * t + lax.broadcasted_iota(jnp.int32, (t, 1), 0)
        for gi, w in enumerate(POOL_WINDOWS):
            sl = slice(gi * GROUP_DIM, (gi + 1) * GROUP_DIM)
            band = _band(t, w, lambda r, col: r + HALO - col, lambda col: (col >= HALO) | (i > 0))
            win = _split_dot(band, hext[:, sl])
            cnt = jnp.minimum(tok + 1, w).astype(F32)
            dbf = (win / cnt - h[:, sl]).astype(BF16)
            d_ref[:, sl] = dbf
            ypre = _nn(dbf, wp_ref[gi]) + b_ref[:, sl]
            xo_ref[:, sl] = x_ref[:, sl] + ypre * sc_ref[:, sl]

    return pl.pallas_call(
        body, name=name, grid=(s // t,),
        out_shape=[jax.ShapeDtypeStruct((s, D_MODEL), F32), jax.ShapeDtypeStruct((s, D_MODEL), BF16)],
        in_specs=[_rows(t, D_MODEL),
                  pl.BlockSpec((HALO, D_MODEL), lambda i: (jnp.maximum(i * rb - 1, 0), 0)),
                  _full((1, D_MODEL)), _full((4, GROUP_DIM, GROUP_DIM)), _full((1, D_MODEL)), _full((1, D_MODEL))],
        out_specs=[_rows(t, D_MODEL), _rows(t, D_MODEL)],
        compiler_params=_cparams(1, VMEM_MID),
    )(x, x, g, wp, b, sc)


def _mix_bwd(x, dy, d, g, wp, b, sc, name):
    s = x.shape[0]
    t = min(256, s)
    rb = t // HALO
    nb = s // t
    last_halo = s // HALO - 1

    def body(x_ref, dy_ref, dyn_ref, d_ref, g_ref, wp_ref, b_ref, sc_ref,
             dx_ref, dyp_ref, dsc_ref, db_ref, dln_ref):
        i = pl.program_id(0)
        x = x_ref[...]
        gg = g_ref[...]
        dy = dy_ref[...]
        sc = sc_ref[...]
        dyp32 = dy * sc
        dyp = dyp32.astype(BF16)
        dyph = (dyn_ref[...] * sc).astype(BF16)
        dyp_ref[...] = dyp
        tok = i * t + lax.broadcasted_iota(jnp.int32, (t + HALO, 1), 0)
        dh, dsc = [], []
        for gi, w in enumerate(POOL_WINDOWS):
            sl = slice(gi * GROUP_DIM, (gi + 1) * GROUP_DIM)
            ypre = _nn(d_ref[:, sl], wp_ref[gi]) + b_ref[:, sl]
            dsc.append(jnp.sum(dy[:, sl] * ypre, axis=0, keepdims=True))
            dd = _nt(dyp[:, sl], wp_ref[gi])
            ddh = _nt(dyph[:, sl], wp_ref[gi])
            cnt = jnp.minimum(tok + 1, w).astype(F32)
            ddext = jnp.concatenate([dd, ddh], axis=0) / cnt
            band = _band(t, w, lambda r, col: col - r, lambda col: (col < t) | (i < nb - 1))
            dh.append(_split_dot(band, ddext) - dd)
        dh = jnp.concatenate(dh, axis=1)
        _, r = _rms(x, gg, D_MODEL)
        dxn, dg = _rms_bwd(x, r, gg, dh, D_MODEL)
        dx_ref[...] = dy + dxn

        @pl.when(i == 0)
        def _():
            dsc_ref[...] = jnp.zeros_like(dsc_ref)
            db_ref[...] = jnp.zeros_like(db_ref)
            dln_ref[...] = jnp.zeros_like(dln_ref)

        dsc_ref[...] += jnp.concatenate(dsc, axis=1)
        db_ref[...] += jnp.sum(dyp32, axis=0, keepdims=True)
        dln_ref[...] += dg

    vec = jax.ShapeDtypeStruct((1, D_MODEL), F32)
    return pl.pallas_call(
        body, name=name, grid=(nb,),
        out_shape=[jax.ShapeDtypeStruct((s, D_MODEL), F32), jax.ShapeDtypeStruct((s, D_MODEL), BF16), vec, vec, vec],
        in_specs=[_rows(t, D_MODEL), _rows(t, D_MODEL),
                  pl.BlockSpec((HALO, D_MODEL), lambda i: (jnp.minimum((i + 1) * rb, last_halo), 0)),
                  _rows(t, D_MODEL),
                  _full((1, D_MODEL)), _full((4, GROUP_DIM, GROUP_DIM)), _full((1, D_MODEL)), _full((1, D_MODEL))],
        out_specs=[_rows(t, D_MODEL), _rows(t, D_MODEL), _full((1, D_MODEL)), _full((1, D_MODEL)), _full((1, D_MODEL))],
        compiler_params=_cparams(1, VMEM_MID),
    )(x, dy, dy, d, g, wp, b, sc)


def _load_weights(w_hbm, w_vmem, sem):
    @pl.when(pl.program_id(0) == 0)
    def _():
        cp = pltpu.make_async_copy(w_hbm, w_vmem, sem)
        cp.start()
        cp.wait()


def _ffn_fwd(x, g, w_all, layer, name):
    s = x.shape[0]
    t = min(256, s)

    def body(x_ref, g_ref, w_hbm, xo_ref, gate_ref, up_ref, w_ref, sem):
        _load_weights(w_hbm.at[layer], w_ref, sem)
        x = x_ref[...]
        hn = _rms(x, g_ref[...], D_MODEL)[0].astype(BF16)
        acc = x
        for c in range(2):
            rs = slice(c * FF_HALF, (c + 1) * FF_HALF)
            gt = _nt(hn, w_ref[0, rs, :])
            up = _nt(hn, w_ref[1, rs, :])
            gate_ref[:, rs] = gt.astype(BF16)
            up_ref[:, rs] = up.astype(BF16)
            act = ((gt * _sigmoid(gt)) * up).astype(BF16)
            acc = acc + _nn(act, w_ref[2, rs, :])
        xo_ref[...] = acc

    hid = jax.ShapeDtypeStruct((s, D_FF), BF16)
    return pl.pallas_call(
        body, name=name, grid=(s // t,),
        out_shape=[jax.ShapeDtypeStruct((s, D_MODEL), F32), hid, hid],
        in_specs=[_rows(t, D_MODEL), _full((1, D_MODEL)), ANY],
        out_specs=[_rows(t, D_MODEL), _rows(t, D_FF), _rows(t, D_FF)],
        scratch_shapes=[pltpu.VMEM((3, D_FF, D_MODEL), BF16), pltpu.SemaphoreType.DMA],
        compiler_params=_cparams(1, VMEM_BIG),
    )(x, g, w_all)


def _ffn_bwd(x, dy, gate, up, g, w_all, layer, name):
    s = x.shape[0]
    t = min(256, s)

    def body(x_ref, dy_ref, gate_ref, up_ref, g_ref, w_hbm,
             dx_ref, act_ref, dg_ref, du_ref, hn_ref, dyb_ref, dln_ref, w_ref, sem):
        _load_weights(w_hbm.at[layer], w_ref, sem)
        x = x_ref[...]
        gg = g_ref[...]
        y, r = _rms(x, gg, D_MODEL)
        hn = y.astype(BF16)
        hn_ref[...] = hn
        dy = dy_ref[...]
        dyb = dy.astype(BF16)
        dyb_ref[...] = dyb
        dh = jnp.zeros((t, D_MODEL), F32)
        for c in range(2):
            rs = slice(c * FF_HALF, (c + 1) * FF_HALF)
            gt = gate_ref[:, rs].astype(F32)
            u = up_ref[:, rs].astype(F32)
            sg = _sigmoid(gt)
            sl = gt * sg
            act_ref[:, rs] = (sl * u).astype(BF16)
            dact = _nt(dyb, w_ref[2, rs, :])
            dg = (dact * u * (sg * (1.0 + gt * (1.0 - sg)))).astype(BF16)
            du = (dact * sl).astype(BF16)
            dg_ref[:, rs] = dg
            du_ref[:, rs] = du
            dh = dh + _nn(dg, w_ref[0, rs, :]) + _nn(du, w_ref[1, rs, :])
        dxn, dgl = _rms_bwd(x, r, gg, dh, D_MODEL)
        dx_ref[...] = dy + dxn

        @pl.when(pl.program_id(0) == 0)
        def _():
            dln_ref[...] = jnp.zeros_like(dln_ref)

        dln_ref[...] += dgl

    hid = jax.ShapeDtypeStruct((s, D_FF), BF16)
    tok = jax.ShapeDtypeStruct((s, D_MODEL), BF16)
    return pl.pallas_call(
        body, name=name, grid=(s // t,),
        out_shape=[jax.ShapeDtypeStruct((s, D_MODEL), F32), hid, hid, hid, tok, tok,
                   jax.ShapeDtypeStruct((1, D_MODEL), F32)],
        in_specs=[_rows(t, D_MODEL), _rows(t, D_MODEL), _rows(t, D_FF), _rows(t, D_FF), _full((1, D_MODEL)), ANY],
        out_specs=[_rows(t, D_MODEL), _rows(t, D_FF), _rows(t, D_FF), _rows(t, D_FF),
                   _rows(t, D_MODEL), _rows(t, D_MODEL), _full((1, D_MODEL))],
        scratch_shapes=[pltpu.VMEM((3, D_FF, D_MODEL), BF16), pltpu.SemaphoreType.DMA],
        compiler_params=_cparams(1, VMEM_BIG),
    )(x, dy, gate, up, g, w_all)


def _tn_matmul(a, b, into, p0, name, groups=1, m_chunk=None):
    s = a.shape[0]
    m, n = a.shape[1] // groups, b.shape[1] // groups
    assert into.shape[1:] == (m, n)
    mc = m if m_chunk is None else m_chunk
    nm = m // mc
    t = min(512, s)
    nt = s // t

    def body(a_ref, b_ref, into_ref, o_ref, acc):
        ti = pl.program_id(2)

        @pl.when(ti == 0)
        def _():
            acc[...] = jnp.zeros_like(acc)

        acc[...] += _tn(a_ref[...], b_ref[...])

        @pl.when(ti == nt - 1)
        def _():
            o_ref[...] = acc[...].astype(o_ref.dtype)

    return pl.pallas_call(
        body, name=name, grid=(groups, nm, nt),
        out_shape=jax.ShapeDtypeStruct(into.shape, into.dtype),
        in_specs=[pl.BlockSpec((t, mc), lambda gi, mi, ti: (ti, gi * nm + mi)),
                  pl.BlockSpec((t, n), lambda gi, mi, ti: (ti, gi)), ANY],
        out_specs=pl.BlockSpec((None, mc, n), lambda gi, mi, ti: (p0 + gi, mi, 0)),
        scratch_shapes=[pltpu.VMEM((mc, n), F32)],
        input_output_aliases={2: 0},
        compiler_params=_cparams(3, VMEM_MID),
    )(a, b, into)


def _rope_tables(positions):
    half = ROPE // 2
    inv = ROPE_THETA ** (-jnp.arange(half, dtype=F32) * 2.0 / ROPE)
    ang = positions.astype(F32)[:, None] * inv
    cos, sin = jnp.cos(ang), jnp.sin(ang)
    zero = jnp.zeros((positions.shape[0], LANES - ROPE), F32)
    return jnp.concatenate([cos, cos, zero], axis=1), jnp.concatenate([-sin, sin, zero], axis=1)


def _kv_specs(t):
    return [_full((1, D_MODEL)), _full((D_MODEL, KV_RANK)), _full((D_MODEL, LANES)), _full((1, KV_RANK)),
            _full((N_HEADS, KV_RANK, NOPE)), _full((N_HEADS, KV_RANK, V_DIM)),
            _full((1, NOPE)), _full((1, LANES)), _rows(t, LANES), _rows(t, LANES)]


def _kv_fwd(x, ln, wc, wpe, gl, wuk, wuv, gkn, gkr, cos, sin, name):
    s = x.shape[0]
    t = min(256, s)

    def body(x_ref, ln_ref, wc_ref, wpe_ref, gl_ref, wuk_ref, wuv_ref, gkn_ref, gkr_ref, cos_ref, sin_ref,
             k_ref, v_ref):
        hn = _rms(x_ref[...], ln_ref[...], D_MODEL)[0].astype(BF16)
        clat = _nn(hn, wc_ref[...])
        kpe = _nn(hn, wpe_ref[...])
        cn = _rms(clat, gl_ref[...], KV_RANK)[0].astype(BF16)
        sspe = jnp.sum(kpe * kpe, axis=-1, keepdims=True)
        cs, sn = cos_ref[...], sin_ref[...]
        for h in range(N_HEADS):
            kn = _nn(cn, wuk_ref[h])
            r = lax.rsqrt((jnp.sum(kn * kn, axis=-1, keepdims=True) + sspe) * (1.0 / QK_DIM) + EPS)
            k_ref[:, h * QK_PAD:h * QK_PAD + NOPE] = ((kn * r) * gkn_ref[...]).astype(BF16)
            z = (kpe * r) * gkr_ref[...]
            k_ref[:, h * QK_PAD + NOPE:(h + 1) * QK_PAD] = (z * cs + _swap_halves(z) * sn).astype(BF16)
            v_ref[:, h * V_DIM:(h + 1) * V_DIM] = _nn(cn, wuv_ref[h]).astype(BF16)

    return pl.pallas_call(
        body, name=name, grid=(s // t,),
        out_shape=[jax.ShapeDtypeStruct((s, N_HEADS * QK_PAD), BF16), jax.ShapeDtypeStruct((s, N_HEADS * V_DIM), BF16)],
        in_specs=[_rows(t, D_MODEL)] + _kv_specs(t),
        out_specs=[_rows(t, N_HEADS * QK_PAD), _rows(t, N_HEADS * V_DIM)],
        compiler_params=_cparams(1, VMEM_MID),
    )(x, ln, wc, wpe, gl, wuk, wuv, gkn, gkr, cos, sin)


def _kv_bwd(x, dxin, dks, dvs, ln, wc, wpe, gl, wuk, wuv, gkn, gkr, cos, sin, name):
    s = x.shape[0]
    t = min(256, s)
    nk = len(dks)

    def body(*refs):
        x_ref, dxin_ref = refs[:2]
        dk_refs = refs[2:2 + nk]
        dv_refs = refs[2 + nk:2 + 2 * nk]
        (ln_ref, wc_ref, wpe_ref, gl_ref, wuk_ref, wuv_ref, gkn_ref, gkr_ref, cos_ref, sin_ref,
         dx_ref, hn_ref, cn_ref, dkn_ref, dvb_ref, dcc_ref, dpe_ref,
         dln_ref, dgl_ref, dgkn_ref, dgkr_ref) = refs[2 + 2 * nk:]
        x = x_ref[...]
        ln = ln_ref[...]
        y, rx = _rms(x, ln, D_MODEL)
        hn = y.astype(BF16)
        hn_ref[...] = hn
        clat = _nn(hn, wc_ref[...])
        kpe = _nn(hn, wpe_ref[...])
        gl = gl_ref[...]
        cy, rc = _rms(clat, gl, KV_RANK)
        cn = cy.astype(BF16)
        cn_ref[...] = cn
        sspe = jnp.sum(kpe * kpe, axis=-1, keepdims=True)
        cs, sn = cos_ref[...], sin_ref[...]
        gkn, gkr = gkn_ref[...], gkr_ref[...]
        dc = jnp.zeros((t, KV_RANK), F32)
        dkpe = jnp.zeros((t, LANES), F32)
        dgkn = jnp.zeros((1, NOPE), F32)
        dgkr = jnp.zeros((1, LANES), F32)
        for h in range(N_HEADS):
            kn = _nn(cn, wuk_ref[h])
            r = lax.rsqrt((jnp.sum(kn * kn, axis=-1, keepdims=True) + sspe) * (1.0 / QK_DIM) + EPS)
            lo, mid, hi = h * QK_PAD, h * QK_PAD + NOPE, (h + 1) * QK_PAD
            dko = dk_refs[0][:, lo:mid]
            dkr = dk_refs[0][:, mid:hi]
            dvh = dv_refs[0][:, h * V_DIM:(h + 1) * V_DIM]
            for j in range(1, nk):
                dko = dko + dk_refs[j][:, lo:mid]
                dkr = dkr + dk_refs[j][:, mid:hi]
                dvh = dvh + dv_refs[j][:, h * V_DIM:(h + 1) * V_DIM]
            dz = dkr * cs - _swap_halves(dkr) * sn
            un = dko * gkn
            ur = dz * gkr
            sm = (jnp.sum(kn * un, axis=-1, keepdims=True) + jnp.sum(kpe * ur, axis=-1, keepdims=True)) * (1.0 / QK_DIM)
            coef = r * r * r * sm
            dkn = (r * un - kn * coef).astype(BF16)
            dkpe = dkpe + (r * ur - kpe * coef)
            dgkn = dgkn + jnp.sum(dko * (kn * r), axis=0, keepdims=True)
            dgkr = dgkr + jnp.sum(dz * (kpe * r), axis=0, keepdims=True)
            dkn_ref[:, h * NOPE:(h + 1) * NOPE] = dkn
            dvb = dvh.astype(BF16)
            dvb_ref[:, h * V_DIM:(h + 1) * V_DIM] = dvb
            dc = dc + _nt(dkn, wuk_ref[h]) + _nt(dvb, wuv_ref[h])
        dclat, dgl = _rms_bwd(clat, rc, gl, dc, KV_RANK)
        dcc = dclat.astype(BF16)
        dpe = dkpe.astype(BF16)
        dcc_ref[...] = dcc
        dpe_ref[...] = dpe
        dhn = _nt(dcc, wc_ref[...]) + _nt(dpe, wpe_ref[...])
        dxn, dln = _rms_bwd(x, rx, ln, dhn, D_MODEL)
        dx_ref[...] = dxin_ref[...] + dxn

        @pl.when(pl.program_id(0) == 0)
        def _():
            dln_ref[...] = jnp.zeros_like(dln_ref)
            dgl_ref[...] = jnp.zeros_like(dgl_ref)
            dgkn_ref[...] = jnp.zeros_like(dgkn_ref)
            dgkr_ref[...] = jnp.zeros_like(dgkr_ref)

        dln_ref[...] += dln
        dgl_ref[...] += dgl
        dgkn_ref[...] += dgkn
        dgkr_ref[...] += dgkr

    def tok(cols, dt):
        return jax.ShapeDtypeStruct((s, cols), dt)

    def vec(cols):
        return jax.ShapeDtypeStruct((1, cols), F32)

    return pl.pallas_call(
        body, name=name, grid=(s // t,),
        out_shape=[tok(D_MODEL, F32), tok(D_MODEL, BF16), tok(KV_RANK, BF16), tok(N_HEADS * NOPE, BF16),
                   tok(N_HEADS * V_DIM, BF16), tok(KV_RANK, BF16), tok(LANES, BF16),
                   vec(D_MODEL), vec(KV_RANK), vec(NOPE), vec(LANES)],
        in_specs=[_rows(t, D_MODEL), _rows(t, D_MODEL)] + [_rows(t, N_HEADS * QK_PAD)] * nk
                 + [_rows(t, N_HEADS * V_DIM)] * nk + _kv_specs(t),
        out_specs=[_rows(t, D_MODEL), _rows(t, D_MODEL), _rows(t, KV_RANK), _rows(t, N_HEADS * NOPE),
                   _rows(t, N_HEADS * V_DIM), _rows(t, KV_RANK), _rows(t, LANES),
                   _full((1, D_MODEL)), _full((1, KV_RANK)), _full((1, NOPE)), _full((1, LANES))],
        compiler_params=_cparams(1, VMEM_BIG),
    )(x, dxin, *dks, *dvs, ln, wc, wpe, gl, wuk, wuv, gkn, gkr, cos, sin)


def _q_specs(t):
    return [_full((1, D_MODEL)), _full((D_MODEL, Q_RANK)), _full((1, Q_RANK)), _full((N_HEADS, Q_RANK, QK_PAD)),
            _full((1, NOPE)), _full((1, LANES)), _rows(t, LANES), _rows(t, LANES)]


def _q_fwd(x, ln, wdq, gql, wuq, gqn, gqr, cos, sin, name):
    s = x.shape[0]
    t = min(256, s)

    def body(x_ref, ln_ref, wdq_ref, gql_ref, wuq_ref, gqn_ref, gqr_ref, cos_ref, sin_ref, q_ref):
        hn = _rms(x_ref[...], ln_ref[...], D_MODEL)[0].astype(BF16)
        cqn = _rms(_nn(hn, wdq_ref[...]), gql_ref[...], Q_RANK)[0].astype(BF16)
        cs, sn = cos_ref[...], sin_ref[...]
        for h in range(N_HEADS):
            qa = _nn(cqn, wuq_ref[h])
            r = lax.rsqrt(jnp.sum(qa * qa, axis=-1, keepdims=True) * (1.0 / QK_DIM) + EPS)
            q_ref[:, h * QK_PAD:h * QK_PAD + NOPE] = ((qa[:, :NOPE] * r) * gqn_ref[...]).astype(BF16)
            z = (qa[:, NOPE:] * r) * gqr_ref[...]
            q_ref[:, h * QK_PAD + NOPE:(h + 1) * QK_PAD] = (z * cs + _swap_halves(z) * sn).astype(BF16)

    return pl.pallas_call(
        body, name=name, grid=(s // t,),
        out_shape=jax.ShapeDtypeStruct((s, N_HEADS * QK_PAD), BF16),
        in_specs=[_rows(t, D_MODEL)] + _q_specs(t),
        out_specs=_rows(t, N_HEADS * QK_PAD),
        compiler_params=_cparams(1, VMEM_MID),
    )(x, ln, wdq, gql, wuq, gqn, gqr, cos, sin)


def _q_bwd(x, dxin, dq, ln, wdq, gql, wuq, gqn, gqr, cos, sin, name):
    s = x.shape[0]
    t = min(256, s)

    def body(x_ref, dxin_ref, dq_ref, ln_ref, wdq_ref, gql_ref, wuq_ref, gqn_ref, gqr_ref, cos_ref, sin_ref,
             dx_ref, hn_ref, cqn_ref, dqa_ref, dcq_ref, dln_ref, dgql_ref, dgqn_ref, dgqr_ref):
        x = x_ref[...]
        ln = ln_ref[...]
        y, rx = _rms(x, ln, D_MODEL)
        hn = y.astype(BF16)
        hn_ref[...] = hn
        cqp = _nn(hn, wdq_ref[...])
        gql = gql_ref[...]
        cy, rc = _rms(cqp, gql, Q_RANK)
        cqn = cy.astype(BF16)
        cqn_ref[...] = cqn
        cs, sn = cos_ref[...], sin_ref[...]
        gqn, gqr = gqn_ref[...], gqr_ref[...]
        dcq = jnp.zeros((t, Q_RANK), F32)
        dgqn = jnp.zeros((1, NOPE), F32)
        dgqr = jnp.zeros((1, LANES), F32)
        for h in range(N_HEADS):
            qa = _nn(cqn, wuq_ref[h])
            qn, qr = qa[:, :NOPE], qa[:, NOPE:]
            r = lax.rsqrt(jnp.sum(qa * qa, axis=-1, keepdims=True) * (1.0 / QK_DIM) + EPS)
            dqo = dq_ref[:, h * QK_PAD:h * QK_PAD + NOPE]
            dqr = dq_ref[:, h * QK_PAD + NOPE:(h + 1) * QK_PAD]
            dz = dqr * cs - _swap_halves(dqr) * sn
            un = dqo * gqn
            ur = dz * gqr
            sm = (jnp.sum(qn * un, axis=-1, keepdims=True) + jnp.sum(qr * ur, axis=-1, keepdims=True)) * (1.0 / QK_DIM)
            coef = r * r * r * sm
            dqa = jnp.concatenate([r * un - qn * coef, r * ur - qr * coef], axis=1).astype(BF16)
            dgqn = dgqn + jnp.sum(dqo * (qn * r), axis=0, keepdims=True)
            dgqr = dgqr + jnp.sum(dz * (qr * r), axis=0, keepdims=True)
            dqa_ref[:, h * QK_PAD:(h + 1) * QK_PAD] = dqa
            dcq = dcq + _nt(dqa, wuq_ref[h])
        dcqp, dgql = _rms_bwd(cqp, rc, gql, dcq, Q_RANK)
        dcqb = dcqp.astype(BF16)
        dcq_ref[...] = dcqb
        dhn = _nt(dcqb, wdq_ref[...])
        dxn, dln = _rms_bwd(x, rx, ln, dhn, D_MODEL)
        dx_ref[...] = dxin_ref[...] + dxn

        @pl.when(pl.program_id(0) == 0)
        def _():
            dln_ref[...] = jnp.zeros_like(dln_ref)
            dgql_ref[...] = jnp.zeros_like(dgql_ref)
            dgqn_ref[...] = jnp.zeros_like(dgqn_ref)
            dgqr_ref[...] = jnp.zeros_like(dgqr_ref)

        dln_ref[...] += dln
        dgql_ref[...] += dgql
        dgqn_ref[...] += dgqn
        dgqr_ref[...] += dgqr

    def tok(cols, dt):
        return jax.ShapeDtypeStruct((s, cols), dt)

    def vec(cols):
        return jax.ShapeDtypeStruct((1, cols), F32)

    return pl.pallas_call(
        body, name=name, grid=(s // t,),
        out_shape=[tok(D_MODEL, F32), tok(D_MODEL, BF16), tok(Q_RANK, BF16), tok(N_HEADS * QK_PAD, BF16),
                   tok(Q_RANK, BF16), vec(D_MODEL), vec(Q_RANK), vec(NOPE), vec(LANES)],
        in_specs=[_rows(t, D_MODEL), _rows(t, D_MODEL), _rows(t, N_HEADS * QK_PAD)] + _q_specs(t),
        out_specs=[_rows(t, D_MODEL), _rows(t, D_MODEL), _rows(t, Q_RANK), _rows(t, N_HEADS * QK_PAD),
                   _rows(t, Q_RANK), _full((1, D_MODEL)), _full((1, Q_RANK)), _full((1, NOPE)), _full((1, LANES))],
        compiler_params=_cparams(1, VMEM_MID),
    )(x, dxin, dq, ln, wdq, gql, wuq, gqn, gqr, cos, sin)


SM_SCALE = 1.0 / math.sqrt(QK_DIM)
NEG = -1e30


def _diag_mask(t):
    qpos = lax.broadcasted_iota(jnp.int32, (t, t), 0)
    kpos = lax.broadcasted_iota(jnp.int32, (t, t), 1)
    return lax.shift_right_logical(kpos, 6) <= lax.shift_right_logical(qpos, 6)


def _att_fwd(q, k, v, name):
    s = q.shape[0]
    t = min(512, s)
    nb = s // t

    def body(q_ref, k_ref, v_ref, o_ref, lse_ref):
        qi = pl.program_id(1)
        qq = q_ref[...]

        def block(ki, carry, masked):
            m_old, l_old, acc = carry
            rows = pl.ds(pl.multiple_of(ki * t, t), t)
            sc = _nt(qq, k_ref[rows, :]) * SM_SCALE
            if masked:
                sc = jnp.where(_diag_mask(t), sc, NEG)
            m_new = jnp.maximum(m_old, jnp.max(sc, axis=-1, keepdims=True))
            p = jnp.exp(sc - m_new)
            alpha = jnp.exp(m_old - m_new)
            l_new = alpha * l_old + jnp.sum(p, axis=-1, keepdims=True)
            acc = alpha * acc + _nn(p.astype(BF16), v_ref[rows, :])
            return m_new, l_new, acc

        init = (jnp.full((t, 1), NEG, F32), jnp.zeros((t, 1), F32), jnp.zeros((t, V_DIM), F32))
        carry = lax.fori_loop(0, qi, lambda ki, c: block(ki, c, False), init)
        m_fin, l_fin, acc = block(qi, carry, True)
        o_ref[...] = (acc / l_fin).astype(BF16)
        lse_ref[...] = jnp.broadcast_to(m_fin + jnp.log(l_fin), (t, LANES))

    return pl.pallas_call(
        body, name=name, grid=(N_HEADS, nb),
        out_shape=[jax.ShapeDtypeStruct((s, N_HEADS * V_DIM), BF16), jax.ShapeDtypeStruct((s, N_HEADS * LANES), F32)],
        in_specs=[pl.BlockSpec((t, QK_PAD), lambda h, qi: (qi, h)),
                  pl.BlockSpec((s, QK_PAD), lambda h, qi: (0, h)),
                  pl.BlockSpec((s, V_DIM), lambda h, qi: (0, h))],
        out_specs=[pl.BlockSpec((t, V_DIM), lambda h, qi: (qi, h)),
                   pl.BlockSpec((t, LANES), lambda h, qi: (qi, h))],
        compiler_params=_cparams(2, VMEM_MID),
    )(q, k, v)


def _att_bwd(q, k, v, do, o, lse, name):
    s = q.shape[0]
    t = min(512, s)
    nb = s // t

    def body(q_ref, k_ref, v_ref, do_ref, o_ref, lse_ref, dq_ref, dk_ref, dv_ref):
        ki = pl.program_id(1)
        kk, vv = k_ref[...], v_ref[...]

        @pl.when(ki == 0)
        def _():
            dq_ref[...] = jnp.zeros_like(dq_ref)

        def block(qi, carry, masked):
            dk, dv = carry
            rows = pl.ds(pl.multiple_of(qi * t, t), t)
            qq, dob = q_ref[rows, :], do_ref[rows, :]
            sc = _nt(qq, kk) * SM_SCALE
            if masked:
                sc = jnp.where(_diag_mask(t), sc, NEG)
            p = jnp.exp(sc - lse_ref[rows, :][:, :1])
            dp = _nt(dob, vv)
            dsum = jnp.sum(dob.astype(F32) * o_ref[rows, :].astype(F32), axis=-1, keepdims=True)
            ds = (p * (dp - dsum) * SM_SCALE).astype(BF16)
            dq_ref[rows, :] += _nn(ds, kk)
            return dk + _tn(ds, qq), dv + _tn(p.astype(BF16), dob)

        carry = block(ki, (jnp.zeros((t, QK_PAD), F32), jnp.zeros((t, V_DIM), F32)), True)
        dk, dv = lax.fori_loop(ki + 1, nb, lambda qi, c: block(qi, c, False), carry)
        dk_ref[...] = dk
        dv_ref[...] = dv

    def head(h, ki):
        return (0, h)

    def kblock(h, ki):
        return (ki, h)

    return pl.pallas_call(
        body, name=name, grid=(N_HEADS, nb),
        out_shape=[jax.ShapeDtypeStruct((s, N_HEADS * QK_PAD), F32), jax.ShapeDtypeStruct((s, N_HEADS * QK_PAD), F32),
                   jax.ShapeDtypeStruct((s, N_HEADS * V_DIM), F32)],
        in_specs=[pl.BlockSpec((s, QK_PAD), head), pl.BlockSpec((t, QK_PAD), kblock), pl.BlockSpec((t, V_DIM), kblock),
                  pl.BlockSpec((s, V_DIM), head), pl.BlockSpec((s, V_DIM), head), pl.BlockSpec((s, LANES), head)],
        out_specs=[pl.BlockSpec((s, QK_PAD), head), pl.BlockSpec((t, QK_PAD), kblock), pl.BlockSpec((t, V_DIM), kblock)],
        compiler_params=_cparams(2, VMEM_MID),
    )(q, k, v, do, o, lse)


def _o_fwd(x, o, wo, name):
    s = x.shape[0]
    t = min(512, s)

    def body(x_ref, o_ref, wo_ref, xo_ref):
        xo_ref[...] = x_ref[...] + _nn(o_ref[...], wo_ref[...])

    return pl.pallas_call(
        body, name=name, grid=(s // t,),
        out_shape=jax.ShapeDtypeStruct((s, D_MODEL), F32),
        in_specs=[_rows(t, D_MODEL), _rows(t, D_MODEL), _full((D_MODEL, D_MODEL))],
        out_specs=_rows(t, D_MODEL),
        compiler_params=_cparams(1, VMEM_MID),
    )(x, o, wo)


def _o_bwd(dx, wo, name):
    s = dx.shape[0]
    t = min(512, s)

    def body(dx_ref, wo_ref, do_ref, dxb_ref):
        dxb = dx_ref[...].astype(BF16)
        dxb_ref[...] = dxb
        do_ref[...] = _nt(dxb, wo_ref[...]).astype(BF16)

    tok = jax.ShapeDtypeStruct((s, D_MODEL), BF16)
    return pl.pallas_call(
        body, name=name, grid=(s // t,),
        out_shape=[tok, tok],
        in_specs=[_rows(t, D_MODEL), _full((D_MODEL, D_MODEL))],
        out_specs=[_rows(t, D_MODEL), _rows(t, D_MODEL)],
        compiler_params=_cparams(1, VMEM_MID),
    )(dx, wo)


def _loss_head(y, target, name):
    s = y.shape[0]
    t = min(512, s)

    def body(y_ref, t_ref, dy_ref, sq_ref):
        e = y_ref[...] - t_ref[...]
        dy_ref[...] = e * (1.0 / D_MODEL)

        @pl.when(pl.program_id(0) == 0)
        def _():
            sq_ref[...] = jnp.zeros_like(sq_ref)

        sq_ref[...] += jnp.sum(e * e, axis=0, keepdims=True)

    return pl.pallas_call(
        body, name=name, grid=(s // t,),
        out_shape=[jax.ShapeDtypeStruct((s, D_MODEL), F32), jax.ShapeDtypeStruct((1, D_MODEL), F32)],
        in_specs=[_rows(t, D_MODEL), _rows(t, D_MODEL)],
        out_specs=[_rows(t, D_MODEL), _full((1, D_MODEL))],
        compiler_params=_cparams(1),
    )(y, target)


def _adamw(w, g, m, v, name):
    shape = w.shape
    c = shape[-1]
    r = math.prod(shape[:-1])
    tb = r
    for cand in (512, 256, 128):
        if r % cand == 0 and r > cand:
            tb = cand
            break

    def body(w_ref, g_ref, m_ref, v_ref, d_ref, mo_ref, vo_ref):
        gr = g_ref[...]
        mn = ADAM_B1 * m_ref[...] + (1.0 - ADAM_B1) * gr
        vn = ADAM_B2 * v_ref[...] + (1.0 - ADAM_B2) * (gr * gr)
        m_hat = mn / (1.0 - ADAM_B1 ** ADAM_STEP)
        v_hat = vn / (1.0 - ADAM_B2 ** ADAM_STEP)
        d_ref[...] = -ADAM_LR * (m_hat / (jnp.sqrt(v_hat) + ADAM_EPS) + ADAM_WD * w_ref[...])
        mo_ref[...] = mn
        vo_ref[...] = vn

    spec = pl.BlockSpec((tb, c), lambda i: (i, 0))
    flat = jax.ShapeDtypeStruct((r, c), F32)
    outs = pl.pallas_call(
        body, name=name, grid=(r // tb,),
        out_shape=[flat, flat, flat],
        in_specs=[spec] * 4, out_specs=[spec] * 3,
        compiler_params=_cparams(1),
    )(w.reshape(r, c), g.reshape(r, c), m.reshape(r, c), v.reshape(r, c))
    return [a.reshape(shape) for a in outs]


def _pad_cols(a, width):
    return jnp.pad(a, [(0, 0)] * (a.ndim - 1) + [(0, width - a.shape[-1])])


def _owner_view(a, sz):
    return a.reshape(a.shape[0], N_CHIPS, 2, sz, a.shape[-1])


def kernel(x, positions, ln_mix_a, w_pool, b_pool, pool_scale, ln_ffn, w_gate, w_up, w_down, ln_kv, w_dkv, g_kv_latent, w_uk, w_uv, g_k, ln_mix_b, w_dq, g_q_latent, w_uq, g_q, w_o, loss_target, m_ln_mix_a, m_w_pool, m_b_pool, m_pool_scale, m_ln_ffn, m_w_gate, m_w_up, m_w_down, m_ln_kv, m_w_dkv, m_g_kv_latent, m_w_uk, m_w_uv, m_g_k, m_ln_mix_b, m_w_dq, m_g_q_latent, m_w_uq, m_g_q, m_w_o, v_ln_mix_a, v_w_pool, v_b_pool, v_pool_scale, v_ln_ffn, v_w_gate, v_w_up, v_w_down, v_ln_kv, v_w_dkv, v_g_kv_latent, v_w_uk, v_w_uv, v_g_k, v_ln_mix_b, v_w_dq, v_g_q_latent, v_w_uq, v_g_q, v_w_o):
    weights = dict(ln_mix_a=ln_mix_a, w_pool=w_pool, b_pool=b_pool, pool_scale=pool_scale, ln_ffn=ln_ffn,
                   w_gate=w_gate, w_up=w_up, w_down=w_down, ln_kv=ln_kv, w_dkv=w_dkv, g_kv_latent=g_kv_latent,
                   w_uk=w_uk, w_uv=w_uv, g_k=g_k, ln_mix_b=ln_mix_b, w_dq=w_dq, g_q_latent=g_q_latent,
                   w_uq=w_uq, g_q=g_q, w_o=w_o)
    mom1 = dict(ln_mix_a=m_ln_mix_a, w_pool=m_w_pool, b_pool=m_b_pool, pool_scale=m_pool_scale, ln_ffn=m_ln_ffn,
                w_gate=m_w_gate, w_up=m_w_up, w_down=m_w_down, ln_kv=m_ln_kv, w_dkv=m_w_dkv,
                g_kv_latent=m_g_kv_latent, w_uk=m_w_uk, w_uv=m_w_uv, g_k=m_g_k, ln_mix_b=m_ln_mix_b, w_dq=m_w_dq,
                g_q_latent=m_g_q_latent, w_uq=m_w_uq, g_q=m_g_q, w_o=m_w_o)
    mom2 = dict(ln_mix_a=v_ln_mix_a, w_pool=v_w_pool, b_pool=v_b_pool, pool_scale=v_pool_scale, ln_ffn=v_ln_ffn,
                w_gate=v_w_gate, w_up=v_w_up, w_down=v_w_down, ln_kv=v_ln_kv, w_dkv=v_w_dkv,
                g_kv_latent=v_g_kv_latent, w_uk=v_w_uk, w_uv=v_w_uv, g_k=v_g_k, ln_mix_b=v_ln_mix_b, w_dq=v_w_dq,
                g_q_latent=v_g_q_latent, w_uq=v_w_uq, g_q=v_g_q, w_o=v_w_o)
    names = list(weights)
    dev = 4 * lax.axis_index("x") + 2 * lax.axis_index("y") + lax.axis_index("c")

    xs = x[0]
    target = loss_target[0]
    cos, sin = _rope_tables(positions[0])

    ffn_sh = jnp.stack([w_gate.transpose(0, 2, 1), w_up.transpose(0, 2, 1), w_down], axis=1).astype(BF16)
    small_sh = jnp.concatenate([ln_mix_a.reshape(1, -1), pool_scale.reshape(1, -1), b_pool.reshape(1, -1)], axis=1)
    shards = [ffn_sh, w_pool.astype(BF16), w_dkv[:, :KV_RANK].astype(BF16),
              _pad_cols(w_dkv[:, KV_RANK:], LANES).astype(BF16), w_uk.astype(BF16), w_uv.astype(BF16),
              w_dq.astype(BF16), _pad_cols(w_uq, QK_PAD).astype(BF16), w_o.astype(BF16), small_sh]
    axes = [2, 2, 0, 0, 0, 0, 1, 0, 1, 0]
    (ffn_g, wp_g, wc_g, wpe_g, wuk_g, wuv_g, wdq_g, wuq_g, wo_g, small_g) = _all_gather(shards, axes, "gather_weights")
    w_ffn = ffn_g.reshape(4, 3, D_FF, D_MODEL)
    wp_all = wp_g.reshape(2, 4, GROUP_DIM, GROUP_DIM)
    wc = wc_g.reshape(D_MODEL, KV_RANK)
    wpe = wpe_g.reshape(D_MODEL, LANES)
    wdq_all = wdq_g.reshape(2, D_MODEL, Q_RANK)
    wo_all = wo_g.reshape(2, D_MODEL, D_MODEL)
    small_g = small_g.reshape(N_DEV, 3, 2, LANES)
    ln_a_all = small_g[:, 0].transpose(1, 0, 2).reshape(2, 1, D_MODEL)
    sc_all = small_g[:, 1].transpose(1, 0, 2).reshape(2, 1, D_MODEL)
    bp_all = small_g[:, 2].reshape(N_DEV, 2, 4, 32).transpose(1, 2, 0, 3).reshape(2, 1, D_MODEL)

    gkn = g_k[:NOPE].reshape(1, NOPE)
    gkr = _pad_cols(g_k[NOPE:].reshape(1, ROPE), LANES)
    gl = g_kv_latent.reshape(1, KV_RANK)
    lnkv = ln_kv.reshape(1, D_MODEL)

    x_in, x_mid, pooled, gates, ups = [], [], [], [], []
    qs, outs, lses = [], [], []
    cur = xs
    for l in range(4):
        x_in.append(cur)
        if l < 2:
            mid, dsave = _mix_fwd(cur, ln_a_all[l], wp_all[l], bp_all[l], sc_all[l], f"mix_fwd{l}")
            pooled.append(dsave)
        else:
            j = l - 2
            q = _q_fwd(cur, ln_mix_b[j].reshape(1, -1), wdq_all[j], g_q_latent[j].reshape(1, -1), wuq_g[:, j],
                       g_q[j, :NOPE].reshape(1, -1), _pad_cols(g_q[j, NOPE:].reshape(1, -1), LANES), cos, sin,
                       f"q_fwd{j}")
            o, lse = _att_fwd(q, k_sh, v_sh, f"att_fwd{j}")
            mid = _o_fwd(cur, o, wo_all[j], f"o_fwd{j}")
            qs.append(q)
            outs.append(o)
            lses.append(lse)
        x_mid.append(mid)
        cur, gate, up = _ffn_fwd(mid, ln_ffn[l].reshape(1, -1), w_ffn, l, f"ffn_fwd{l}")
        gates.append(gate)
        ups.append(up)
        if l == 1:
            x_kv = cur
            k_sh, v_sh = _kv_fwd(cur, lnkv, wc, wpe, gl, wuk_g, wuv_g, gkn, gkr, cos, sin, "kv_fwd")

    dx, sq_cols = _loss_head(cur, target, "loss_head")

    small = {}
    big = dict(ffn=lax.empty((12, D_FF, D_MODEL), BF16), wo=lax.empty((2, D_MODEL, D_MODEL), BF16),
               kv512=lax.empty((3, D_MODEL, KV_RANK), BF16), dkv_pe=lax.empty((1, D_MODEL, LANES), BF16),
               wdq=lax.empty((2, D_MODEL, Q_RANK), BF16), wuqT=lax.empty((2, N_HEADS * QK_PAD, Q_RANK), BF16),
               wpool=lax.empty((8, GROUP_DIM, GROUP_DIM), BF16))
    dks, dvs = [], []
    for l in (3, 2, 1, 0):
        dx, act, dgb, dub, hn, dyb, dln = _ffn_bwd(x_mid[l], dx, gates[l], ups[l], ln_ffn[l].reshape(1, -1),
                                                     w_ffn, l, f"ffn_bwd{l}")
        small[f"ln_ffn{l}"] = dln
        big["ffn"] = _tn_matmul(dgb, hn, big["ffn"], 3 * l, f"dw_gate{l}", m_chunk=FF_HALF)
        big["ffn"] = _tn_matmul(dub, hn, big["ffn"], 3 * l + 1, f"dw_up{l}", m_chunk=FF_HALF)
        big["ffn"] = _tn_matmul(act, dyb, big["ffn"], 3 * l + 2, f"dw_down{l}", m_chunk=FF_HALF)
        if l >= 2:
            j = l - 2
            do, dxb = _o_bwd(dx, wo_all[j], f"o_bwd{j}")
            big["wo"] = _tn_matmul(outs[j], dxb, big["wo"], j, f"dw_o{j}")
            dq, dk, dv = _att_bwd(qs[j], k_sh, v_sh, do, outs[j], lses[j], f"att_bwd{j}")
            dks.append(dk)
            dvs.append(dv)
            dx, hnq, cqn, dqa, dcq, dln, dgql, dgqn, dgqr = _q_bwd(
                x_in[l], dx, dq, ln_mix_b[j].reshape(1, -1), wdq_all[j], g_q_latent[j].reshape(1, -1), wuq_g[:, j],
                g_q[j, :NOPE].reshape(1, -1), _pad_cols(g_q[j, NOPE:].reshape(1, -1), LANES), cos, sin, f"q_bwd{j}")
            small[f"ln_mix_b{j}"] = dln
            small[f"g_q_latent{j}"] = dgql
            small[f"g_q{j}"] = jnp.concatenate([dgqn, dgqr[:, :ROPE]], axis=1)
            big["wdq"] = _tn_matmul(hnq, dcq, big["wdq"], j, f"dw_dq{j}")
            big["wuqT"] = _tn_matmul(dqa, cqn, big["wuqT"], j, f"dw_uq{j}")
            if l == 2:
                (dx, hnk, cn, dknb, dvb, dccb, dpeb, dlnkv, dgl, dgkn, dgkr) = _kv_bwd(
                    x_kv, dx, dks, dvs, lnkv, wc, wpe, gl, wuk_g, wuv_g, gkn, gkr, cos, sin, "kv_bwd")
                small["ln_kv"] = dlnkv
                small["g_kv_latent"] = dgl
                small["g_k"] = jnp.concatenate([dgkn, dgkr[:, :ROPE]], axis=1)
                big["kv512"] = _tn_matmul(dknb, cn, big["kv512"], 0, "dw_uk")
                big["kv512"] = _tn_matmul(dvb, cn, big["kv512"], 1, "dw_uv")
                big["kv512"] = _tn_matmul(hnk, dccb, big["kv512"], 2, "dw_dkv_c")
                big["dkv_pe"] = _tn_matmul(hnk, dpeb, big["dkv_pe"], 0, "dw_dkv_pe")
        else:
            dx, dyp, dsc, db, dln = _mix_bwd(x_in[l], dx, pooled[l], ln_a_all[l], wp_all[l], bp_all[l], sc_all[l],
                                             f"mix_bwd{l}")
            small[f"ln_mix_a{l}"] = dln
            small[f"pool_scale{l}"] = dsc
            small[f"b_pool{l}"] = db
            big["wpool"] = _tn_matmul(pooled[l], dyp, big["wpool"], 4 * l, f"dw_pool{l}", groups=4)
    grad_x = dx[None]

    sizes = dict(ffn=FF_SHARD, wo=128, kv512=128, dkv_pe=128, wdq=128, wuqT=QK_PAD, wpool=32)
    rs_names = list(sizes)
    core = lax.axis_index("c").astype(jnp.int32).reshape(1)
    chip = (2 * lax.axis_index("x") + lax.axis_index("y")).astype(jnp.int32).reshape(1)
    views = [_owner_view(big[nm], sizes[nm]) for nm in rs_names]
    from_sibling = _pair_exchange(views, "reduce_pair")
    chip_parts = [_pair_sum(v, ls, core, f"pair_sum_{nm}") for v, ls, nm in zip(views, from_sibling, rs_names)]
    from_chips = _chip_exchange(chip_parts, "reduce_chips")
    red = {nm: _chip_sum(p, lc, chip, f"chip_sum_{nm}") for p, lc, nm in zip(chip_parts, from_chips, rs_names)}

    vec_names = (["loss"] + [f"ln_ffn{l}" for l in range(4)] + ["ln_kv", "g_kv_latent", "g_k"]
                 + [f"{p}{j}" for p in ("ln_mix_b", "g_q_latent", "g_q") for j in range(2)]
                 + [f"{p}{l}" for p in ("ln_mix_a", "pool_scale", "b_pool") for l in range(2)])
    small["loss"] = sq_cols
    widths = [small[nm].shape[1] for nm in vec_names]
    padded = [-(-w // LANES) * LANES for w in widths]
    packed = jnp.concatenate([_pad_cols(small[nm], pw) for nm, pw in zip(vec_names, padded)], axis=1)
    (all_vecs,) = _all_gather([packed], [0], "gather_vectors")
    total = _sum_lead(all_vecs, "sum_vectors")
    vec = {}
    off = 0
    for nm, w, pw in zip(vec_names, widths, padded):
        vec[nm] = total[0, off:off + w]
        off += pw
    loss = 0.5 * jnp.sum(vec["loss"]) * (1.0 / D_MODEL)

    def own_cols(full, width):
        return lax.dynamic_slice_in_dim(full, dev * width, width, axis=full.ndim - 1)

    grads = dict(
        ln_mix_a=own_cols(jnp.stack([vec["ln_mix_a0"], vec["ln_mix_a1"]]), LANES),
        w_pool=red["wpool"].reshape(2, 4, 32, GROUP_DIM),
        b_pool=own_cols(jnp.stack([vec["b_pool0"], vec["b_pool1"]]).reshape(2, 4, GROUP_DIM), 32),
        pool_scale=own_cols(jnp.stack([vec["pool_scale0"], vec["pool_scale1"]]), LANES),
        ln_ffn=jnp.stack([vec[f"ln_ffn{l}"] for l in range(4)]),
        w_gate=red["ffn"][0::3].transpose(0, 2, 1),
        w_up=red["ffn"][1::3].transpose(0, 2, 1),
        w_down=red["ffn"][2::3],
        ln_kv=vec["ln_kv"],
        w_dkv=jnp.concatenate([red["kv512"][2], red["dkv_pe"][0][:, :ROPE]], axis=1),
        g_kv_latent=vec["g_kv_latent"],
        w_uk=red["kv512"][0].T,
        w_uv=red["kv512"][1].T,
        g_k=vec["g_k"],
        ln_mix_b=jnp.stack([vec["ln_mix_b0"], vec["ln_mix_b1"]]),
        w_dq=red["wdq"],
        g_q_latent=jnp.stack([vec["g_q_latent0"], vec["g_q_latent1"]]),
        w_uq=red["wuqT"].transpose(0, 2, 1)[:, :, :QK_DIM],
        g_q=jnp.stack([vec["g_q0"], vec["g_q1"]]),
        w_o=red["wo"],
    )

    deltas, new_m, new_v = {}, {}, {}
    for nm in names:
        w = weights[nm]
        shape = w.shape if w.ndim > 1 else (1, w.shape[0])
        d, mo, vo = _adamw(w.reshape(shape), grads[nm].reshape(shape), mom1[nm].reshape(shape),
                           mom2[nm].reshape(shape), f"adamw_{nm}")
        deltas[nm], new_m[nm], new_v[nm] = d.reshape(w.shape), mo.reshape(w.shape), vo.reshape(w.shape)

    return (loss, grad_x, *[grads[nm].reshape(weights[nm].shape) for nm in names], *[deltas[nm] for nm in names],
            *[new_m[nm] for nm in names], *[new_v[nm] for nm in names])
```

```python
import functools
import math

import jax
import jax.numpy as jnp
from jax import lax
from jax.experimental import pallas as pl
from jax.experimental.pallas import tpu as pltpu

F32 = jnp.float32
BF16 = jnp.bfloat16
MESH = pl.DeviceIdType.MESH

D_MODEL = 1024
D_FF = 2816
N_DEV = 8
N_CHIPS = 4
FF_SHARD = D_FF // N_DEV
FF_HALF = D_FF // 2
N_HEADS = 8
NOPE = 128
ROPE = 64
QK_DIM = NOPE + ROPE
QK_PAD = 256
V_DIM = 128
Q_RANK = 256
KV_RANK = 512
POOL_WINDOWS = (2, 4, 8, 16)
GROUP_DIM = 256
HALO = 128
CHUNK = 64
ROPE_THETA = 10000.0
EPS = 1e-6
LANES = 128

ADAM_LR = 0.001
ADAM_B1 = 0.9
ADAM_B2 = 0.999
ADAM_EPS = 1e-08
ADAM_WD = 0.01
ADAM_STEP = 10

VMEM_BIG = 56 * 2**20
VMEM_MID = 40 * 2**20


def _nn(a, b):
    return lax.dot_general(a, b, (((1,), (0,)), ((), ())), preferred_element_type=F32)


def _nt(a, b):
    return lax.dot_general(a, b, (((1,), (1,)), ((), ())), preferred_element_type=F32)


def _tn(a, b):
    return lax.dot_general(a, b, (((0,), (0,)), ((), ())), preferred_element_type=F32)


def _rms(x, g, n):
    r = lax.rsqrt(jnp.sum(x * x, axis=-1, keepdims=True) * (1.0 / n) + EPS)
    return (x * r) * g, r


def _rms_bwd(x, r, g, dy, n):
    u = dy * g
    s = jnp.sum(x * u, axis=-1, keepdims=True) * (1.0 / n)
    dx = r * u - x * (r * r * r * s)
    dg = jnp.sum(dy * (x * r), axis=0, keepdims=True)
    return dx, dg


def _swap_halves(z):
    lane = lax.broadcasted_iota(jnp.int32, z.shape, 1)
    return jnp.where(lane < ROPE // 2, pltpu.roll(z, LANES - ROPE // 2, 1), pltpu.roll(z, ROPE // 2, 1))


def _sigmoid(x):
    return 1.0 / (1.0 + jnp.exp(-x))


def _cparams(n_grid, vmem=None):
    return pltpu.CompilerParams(dimension_semantics=("arbitrary",) * n_grid, vmem_limit_bytes=vmem)


def _rows(t, cols):
    return pl.BlockSpec((t, cols), lambda i: (i, 0))


def _full(shape):
    nd = len(shape)
    return pl.BlockSpec(shape, lambda *_: (0,) * nd)


ANY = pl.BlockSpec(memory_space=pl.ANY)


def _pcall(body, args, deps, *, in_specs, **kw):
    n_in, n_dep = len(args), len(deps)

    def ordered(*refs):
        body(*refs[:n_in], *refs[n_in + n_dep:])

    return pl.pallas_call(ordered, in_specs=list(in_specs) + [ANY] * n_dep, **kw)(*args, *deps)


def _place():
    x, y, c = lax.axis_index("x"), lax.axis_index("y"), lax.axis_index("c")
    return x, y, c


def _all_gather(shards, axes, name):
    n = len(shards)
    out_shape = [jax.ShapeDtypeStruct(s.shape[:a] + (N_DEV,) + s.shape[a:], s.dtype) for s, a in zip(shards, axes)]

    def body(*refs):
        ins, outs = refs[:n], refs[n:2 * n]
        send_sems, recv_sems, local_sems = refs[2 * n:]
        x, y, c = _place()
        me, sibling = (x, y, c), (x, y, 1 - c)
        chips = [(1 - x, y), (x, 1 - y), (1 - x, 1 - y)]

        def slot(t, dev):
            idx = 4 * dev[0] + 2 * dev[1] + dev[2]
            return outs[t].at[(slice(None),) * axes[t] + (idx,)]

        def copy(t, k, block, to, src=None):
            return pltpu.make_async_remote_copy(
                src_ref=slot(t, block) if src is None else src, dst_ref=slot(t, block),
                send_sem=send_sems.at[t, k], recv_sem=recv_sems.at[t, k],
                device_id=to, device_id_type=MESH)

        mine = [pltpu.make_async_copy(ins[t], slot(t, me), local_sems.at[t]) for t in range(n)]
        for cp in mine:
            cp.start()
        first = []
        for t in range(n):
            first.append(copy(t, 0, me, sibling, src=ins[t]))
            first += [copy(t, 1 + j, me, (*chip, c), src=ins[t]) for j, chip in enumerate(chips)]
        for cp in first:
            cp.start()
        passed = []
        for j, chip in enumerate(chips):
            for t in range(n):
                copy(t, 1 + j, (*chip, c), me).wait_recv()
                cp = copy(t, 4 + j, (*chip, c), sibling)
                cp.start()
                passed.append(cp)
        for t in range(n):
            copy(t, 0, sibling, me).wait_recv()
            for j, chip in enumerate(chips):
                copy(t, 4 + j, (*chip, 1 - c), me).wait_recv()
        for cp in first + passed:
            cp.wait_send()
        for cp in mine:
            cp.wait()

    return pl.pallas_call(
        body, name=name, out_shape=out_shape,
        in_specs=[ANY] * n, out_specs=[ANY] * n,
        scratch_shapes=[pltpu.SemaphoreType.DMA((n, 7)), pltpu.SemaphoreType.DMA((n, 7)),
                        pltpu.SemaphoreType.DMA((n,))],
    )(*shards)


HBM = pl.BlockSpec(memory_space=pltpu.HBM)
SEM = pl.BlockSpec(memory_space=pltpu.SEMAPHORE)
EFFECT = pltpu.SideEffectType.DATAFLOW_SIDE_EFFECTING


def _copies_start(arrays, n_sems, plan, name):
    n = len(arrays)

    def body(*refs):
        for cp in plan(refs[:n], refs[n], refs[n + 1]):
            cp.start()
        refs[-1][...] = jnp.zeros_like(refs[-1])

    outs = pl.pallas_call(
        body, name=name,
        out_shape=(pltpu.SemaphoreType.DMA((n_sems,)), pltpu.SemaphoreType.DMA((n_sems,)),
                   *[pltpu.HBM(a.shape, a.dtype) for a in arrays], jax.ShapeDtypeStruct((8, LANES), F32)),
        in_specs=[HBM] * n,
        out_specs=(SEM, SEM, *[HBM] * n, pl.BlockSpec(memory_space=pltpu.VMEM)),
        input_output_aliases={i: 2 + i for i in range(n)},
        compiler_params=pltpu.CompilerParams(has_side_effects=EFFECT),
    )(*[pltpu.with_memory_space_constraint(a, pltpu.HBM) for a in arrays])
    return outs[0], outs[1], list(outs[2:2 + n]), outs[-1]


def _copies_wait(arrays, send_sems, recv_sems, after, plan, name):
    n = len(arrays)

    def body(*refs):
        for cp in plan(refs[:n], refs[n], refs[n + 1]):
            cp.wait_send()
            cp.wait_recv()

    outs = pl.pallas_call(
        body, name=name,
        out_shape=tuple(pltpu.HBM(a.shape, a.dtype) for a in arrays),
        in_specs=[HBM] * n + [SEM, SEM, ANY], out_specs=tuple([HBM] * n),
        input_output_aliases={i: i for i in range(n)},
        compiler_params=pltpu.CompilerParams(has_side_effects=EFFECT),
    )(*arrays, send_sems, recv_sems, after)
    return list(outs)


def _remote(src, dst, send_sems, recv_sems, t, to):
    return pltpu.make_async_remote_copy(src_ref=src, dst_ref=dst, send_sem=send_sems.at[t], recv_sem=recv_sems.at[t],
                                        device_id=to, device_id_type=MESH)


def _dev_index(x, y, c):
    return 4 * x + 2 * y + c


def _gather_spread(bufs, send_sems, recv_sems):
    x, y, c = _place()
    mine = _dev_index(x, y, c)
    peers = [(x, y, 1 - c), (1 - x, y, c), (x, 1 - y, c), (1 - x, 1 - y, c)]
    return [_remote(g.at[k, mine], g.at[k, mine], send_sems, recv_sems, t, peer)
            for t, g in enumerate(bufs) for peer in peers for k in range(g.shape[0])]


def _gather_relay(bufs, send_sems, recv_sems):
    x, y, c = _place()
    blocks = [_dev_index(1 - x, y, c), _dev_index(x, 1 - y, c), _dev_index(1 - x, 1 - y, c)]
    return [_remote(g.at[k, b], g.at[k, b], send_sems, recv_sems, t, (x, y, 1 - c))
            for t, g in enumerate(bufs) for b in blocks for k in range(g.shape[0])]


def _blocks_moved(count):
    def plan(bufs, send_sems, recv_sems):
        x, y, c = _place()
        return [_remote(g.at[:, pl.ds(0, count)], g.at[:, pl.ds(0, count)], send_sems, recv_sems, t, (x, y, 1 - c))
                for t, g in enumerate(bufs)]
    return plan


def _pair_send(arrs, send_sems, recv_sems):
    x, y, c = _place()
    return [_remote(arrs[2 * t].at[p, k, 1 - c], arrs[2 * t + 1].at[p, k], send_sems, recv_sems, t, (x, y, 1 - c))
            for t in range(len(arrs) // 2) for p in range(arrs[2 * t].shape[0]) for k in range(N_CHIPS)]


def _chip_send(arrs, send_sems, recv_sems):
    x, y, c = _place()
    chips = [(1 - x, y), (x, 1 - y), (1 - x, 1 - y)]
    return [_remote(arrs[2 * t].at[p, 2 * px + py], arrs[2 * t + 1].at[j, p], send_sems, recv_sems, t, (px, py, c))
            for t in range(len(arrs) // 2) for j, (px, py) in enumerate(chips) for p in range(arrs[2 * t].shape[0])]


def _landed(arrs, send_sems, recv_sems):
    x, y, c = _place()
    return [_remote(arrs[2 * t + 1], arrs[2 * t + 1], send_sems, recv_sems, t, (x, y, 1 - c))
            for t in range(len(arrs) // 2)]


def _pair_sum(grad, landed, core, name):
    p, _, _, sz, c = grad.shape

    def body(core_ref, g_ref, l_ref, o_ref):
        o_ref[...] = (g_ref[...].astype(F32) + l_ref[...].astype(F32)).astype(o_ref.dtype)

    out = pl.pallas_call(
        body, name=name,
        grid_spec=pltpu.PrefetchScalarGridSpec(
            num_scalar_prefetch=1, grid=(p * N_CHIPS,),
            in_specs=[pl.BlockSpec((None, None, sz, c), lambda i, cr: (i, cr[0], 0, 0)),
                      pl.BlockSpec((None, sz, c), lambda i, cr: (i, 0, 0))],
            out_specs=pl.BlockSpec((None, sz, c), lambda i, cr: (i, 0, 0))),
        out_shape=jax.ShapeDtypeStruct((p * N_CHIPS, sz, c), grad.dtype),
        compiler_params=_cparams(1),
    )(core, grad.reshape(p * N_CHIPS, 2, sz, c), landed.reshape(p * N_CHIPS, sz, c))
    return out.reshape(p, N_CHIPS, sz, c)


def _chip_sum(parts, landed, chip, name):
    p, _, sz, c = parts.shape

    def body(chip_ref, a_ref, l_ref, o_ref):
        acc = a_ref[...].astype(F32)
        for j in range(3):
            acc = acc + l_ref[j].astype(F32)
        o_ref[...] = acc

    return pl.pallas_call(
        body, name=name,
        grid_spec=pltpu.PrefetchScalarGridSpec(
            num_scalar_prefetch=1, grid=(p,),
            in_specs=[pl.BlockSpec((None, None, sz, c), lambda i, cr: (i, cr[0], 0, 0)),
                      pl.BlockSpec((3, None, sz, c), lambda i, cr: (0, i, 0, 0))],
            out_specs=pl.BlockSpec((None, sz, c), lambda i, cr: (i, 0, 0))),
        out_shape=jax.ShapeDtypeStruct((p, sz, c), F32),
        compiler_params=_cparams(1),
    )(chip, parts, landed)


def _sum_lead(a, name, out_dtype=F32):
    k = a.shape[0]
    rest = a.shape[1:]
    r, c = rest[-2], rest[-1]
    lead = math.prod(rest[:-2])
    a3 = a.reshape(k, lead * r, c)
    rows = lead * r
    tb = rows
    for cand in (512, 256, 128, 64, 32, 16, 8):
        if rows % cand == 0 and rows > cand:
            tb = cand
            break

    def body(a_ref, o_ref):
        acc = a_ref[0].astype(F32)
        for i in range(1, k):
            acc = acc + a_ref[i].astype(F32)
        o_ref[...] = acc.astype(out_dtype)

    out = pl.pallas_call(
        body, name=name, grid=(rows // tb,),
        out_shape=jax.ShapeDtypeStruct((rows, c), out_dtype),
        in_specs=[pl.BlockSpec((k, tb, c), lambda i: (0, i, 0))],
        out_specs=pl.BlockSpec((tb, c), lambda i: (i, 0)),
        compiler_params=_cparams(1),
    )(a3)
    return out.reshape(rest)


def _band(t, w, offset, valid):
    r = lax.broadcasted_iota(jnp.int32, (t, t + HALO), 0)
    col = lax.broadcasted_iota(jnp.int32, (t, t + HALO), 1)
    diff = offset(r, col)
    return jnp.where((diff >= 0) & (diff < w) & valid(col), 1.0, 0.0).astype(BF16)


def _split_dot(band, v):
    hi = v.astype(BF16)
    lo = (v - hi.astype(F32)).astype(BF16)
    return _nn(band, hi) + _nn(band, lo)


def _mix_fwd(x, g, wp, b, sc, name, deps=()):
    s = x.shape[0]
    t = min(256, s)
    rb = t // HALO

    def body(x_ref, xh_ref, g_ref, wp_ref, b_ref, sc_ref, xo_ref, d_ref):
        i = pl.program_id(0)
        gg = g_ref[...]
        h, _ = _rms(x_ref[...], gg, D_MODEL)
        hh, _ = _rms(xh_ref[...], gg, D_MODEL)
        hext = jnp.concatenate([hh, h], axis=0)
        tok = i * t + lax.broadcasted_iota(jnp.int32, (t, 1), 0)
        for gi, w in enumerate(POOL_WINDOWS):
            sl = slice(gi * GROUP_DIM, (gi + 1) * GROUP_DIM)
            band = _band(t, w, lambda r, col: r + HALO - col, lambda col: (col >= HALO) | (i > 0))
            win = _split_dot(band, hext[:, sl])
            cnt = jnp.minimum(tok + 1, w).astype(F32)
            dbf = (win / cnt - h[:, sl]).astype(BF16)
            d_ref[:, sl] = dbf
            ypre = _nn(dbf, wp_ref[gi]) + b_ref[:, sl]
            xo_ref[:, sl] = x_ref[:, sl] + ypre * sc_ref[:, sl]

    return _pcall(
        body, (x, x, g, wp, b, sc), deps, name=name, grid=(s // t,),
        out_shape=[jax.ShapeDtypeStruct((s, D_MODEL), F32), jax.ShapeDtypeStruct((s, D_MODEL), BF16)],
        in_specs=[_rows(t, D_MODEL),
                  pl.BlockSpec((HALO, D_MODEL), lambda i: (jnp.maximum(i * rb - 1, 0), 0)),
                  _full((1, D_MODEL)), _full((4, GROUP_DIM, GROUP_DIM)), _full((1, D_MODEL)), _full((1, D_MODEL))],
        out_specs=[_rows(t, D_MODEL), _rows(t, D_MODEL)],
        compiler_params=_cparams(1, VMEM_MID),
    )


def _mix_bwd(x, dy, d, g, wp, b, sc, name, deps=()):
    s = x.shape[0]
    t = min(256, s)
    rb = t // HALO
    nb = s // t
    last_halo = s // HALO - 1

    def body(x_ref, dy_ref, dyn_ref, d_ref, g_ref, wp_ref, b_ref, sc_ref,
             dx_ref, dyp_ref, dsc_ref, db_ref, dln_ref):
        i = pl.program_id(0)
        x = x_ref[...]
        gg = g_ref[...]
        dy = dy_ref[...]
        sc = sc_ref[...]
        dyp32 = dy * sc
        dyp = dyp32.astype(BF16)
        dyph = (dyn_ref[...] * sc).astype(BF16)
        dyp_ref[...] = dyp
        tok = i * t + lax.broadcasted_iota(jnp.int32, (t + HALO, 1), 0)
        dh, dsc = [], []
        for gi, w in enumerate(POOL_WINDOWS):
            sl = slice(gi * GROUP_DIM, (gi + 1) * GROUP_DIM)
            ypre = _nn(d_ref[:, sl], wp_ref[gi]) + b_ref[:, sl]
            dsc.append(jnp.sum(dy[:, sl] * ypre, axis=0, keepdims=True))
            dd = _nt(dyp[:, sl], wp_ref[gi])
            ddh = _nt(dyph[:, sl], wp_ref[gi])
            cnt = jnp.minimum(tok + 1, w).astype(F32)
            ddext = jnp.concatenate([dd, ddh], axis=0) / cnt
            band = _band(t, w, lambda r, col: col - r, lambda col: (col < t) | (i < nb - 1))
            dh.append(_split_dot(band, ddext) - dd)
        dh = jnp.concatenate(dh, axis=1)
        _, r = _rms(x, gg, D_MODEL)
        dxn, dg = _rms_bwd(x, r, gg, dh, D_MODEL)
        dx_ref[...] = dy + dxn

        @pl.when(i == 0)
        def _():
            dsc_ref[...] = jnp.zeros_like(dsc_ref)
            db_ref[...] = jnp.zeros_like(db_ref)
            dln_ref[...] = jnp.zeros_like(dln_ref)

        dsc_ref[...] += jnp.concatenate(dsc, axis=1)
        db_ref[...] += jnp.sum(dyp32, axis=0, keepdims=True)
        dln_ref[...] += dg

    vec = jax.ShapeDtypeStruct((1, D_MODEL), F32)
    return _pcall(
        body, (x, dy, dy, d, g, wp, b, sc), deps, name=name, grid=(nb,),
        out_shape=[jax.ShapeDtypeStruct((s, D_MODEL), F32), jax.ShapeDtypeStruct((s, D_MODEL), BF16), vec, vec, vec],
        in_specs=[_rows(t, D_MODEL), _rows(t, D_MODEL),
                  pl.BlockSpec((HALO, D_MODEL), lambda i: (jnp.minimum((i + 1) * rb, last_halo), 0)),
                  _rows(t, D_MODEL),
                  _full((1, D_MODEL)), _full((4, GROUP_DIM, GROUP_DIM)), _full((1, D_MODEL)), _full((1, D_MODEL))],
        out_specs=[_rows(t, D_MODEL), _rows(t, D_MODEL), _full((1, D_MODEL)), _full((1, D_MODEL)), _full((1, D_MODEL))],
        compiler_params=_cparams(1, VMEM_MID),
    )


def _load_weights(w_hbm, w_vmem, sem):
    @pl.when(pl.program_id(0) == 0)
    def _():
        cp = pltpu.make_async_copy(w_hbm, w_vmem, sem)
        cp.start()
        cp.wait()


def _ffn_fwd(x, g, w, name):
    s = x.shape[0]
    t = min(256, s)

    def body(x_ref, g_ref, w_hbm, xo_ref, gate_ref, up_ref, w_ref, sem):
        _load_weights(w_hbm, w_ref, sem)
        x = x_ref[...]
        hn = _rms(x, g_ref[...], D_MODEL)[0].astype(BF16)
        acc = x
        for c in range(2):
            rs = slice(c * FF_HALF, (c + 1) * FF_HALF)
            gt = _nt(hn, w_ref[0, rs, :])
            up = _nt(hn, w_ref[1, rs, :])
            gate_ref[:, rs] = gt.astype(BF16)
            up_ref[:, rs] = up.astype(BF16)
            act = ((gt * _sigmoid(gt)) * up).astype(BF16)
            acc = acc + _nn(act, w_ref[2, rs, :])
        xo_ref[...] = acc

    hid = jax.ShapeDtypeStruct((s, D_FF), BF16)
    return pl.pallas_call(
        body, name=name, grid=(s // t,),
        out_shape=[jax.ShapeDtypeStruct((s, D_MODEL), F32), hid, hid],
        in_specs=[_rows(t, D_MODEL), _full((1, D_MODEL)), ANY],
        out_specs=[_rows(t, D_MODEL), _rows(t, D_FF), _rows(t, D_FF)],
        scratch_shapes=[pltpu.VMEM((3, D_FF, D_MODEL), BF16), pltpu.SemaphoreType.DMA],
        compiler_params=_cparams(1, VMEM_BIG),
    )(x, g, w)


def _ffn_bwd(x, dy, gate, up, g, w, name, deps=()):
    s = x.shape[0]
    t = min(256, s)

    def body(x_ref, dy_ref, gate_ref, up_ref, g_ref, w_hbm,
             dx_ref, act_ref, dg_ref, du_ref, hn_ref, dyb_ref, dln_ref, w_ref, sem):
        _load_weights(w_hbm, w_ref, sem)
        x = x_ref[...]
        gg = g_ref[...]
        y, r = _rms(x, gg, D_MODEL)
        hn = y.astype(BF16)
        hn_ref[...] = hn
        dy = dy_ref[...]
        dyb = dy.astype(BF16)
        dyb_ref[...] = dyb
        dh = jnp.zeros((t, D_MODEL), F32)
        for c in range(2):
            rs = slice(c * FF_HALF, (c + 1) * FF_HALF)
            gt = gate_ref[:, rs].astype(F32)
            u = up_ref[:, rs].astype(F32)
            sg = _sigmoid(gt)
            sl = gt * sg
            act_ref[:, rs] = (sl * u).astype(BF16)
            dact = _nt(dyb, w_ref[2, rs, :])
            dg = (dact * u * (sg * (1.0 + gt * (1.0 - sg)))).astype(BF16)
            du = (dact * sl).astype(BF16)
            dg_ref[:, rs] = dg
            du_ref[:, rs] = du
            dh = dh + _nn(dg, w_ref[0, rs, :]) + _nn(du, w_ref[1, rs, :])
        dxn, dgl = _rms_bwd(x, r, gg, dh, D_MODEL)
        dx_ref[...] = dy + dxn

        @pl.when(pl.program_id(0) == 0)
        def _():
            dln_ref[...] = jnp.zeros_like(dln_ref)

        dln_ref[...] += dgl

    hid = jax.ShapeDtypeStruct((s, D_FF), BF16)
    tok = jax.ShapeDtypeStruct((s, D_MODEL), BF16)
    return _pcall(
        body, (x, dy, gate, up, g, w), deps, name=name, grid=(s // t,),
        out_shape=[jax.ShapeDtypeStruct((s, D_MODEL), F32), hid, hid, hid, tok, tok,
                   jax.ShapeDtypeStruct((1, D_MODEL), F32)],
        in_specs=[_rows(t, D_MODEL), _rows(t, D_MODEL), _rows(t, D_FF), _rows(t, D_FF), _full((1, D_MODEL)), ANY],
        out_specs=[_rows(t, D_MODEL), _rows(t, D_FF), _rows(t, D_FF), _rows(t, D_FF),
                   _rows(t, D_MODEL), _rows(t, D_MODEL), _full((1, D_MODEL))],
        scratch_shapes=[pltpu.VMEM((3, D_FF, D_MODEL), BF16), pltpu.SemaphoreType.DMA],
        compiler_params=_cparams(1, VMEM_BIG),
    )


def _tn_matmul(a, b, into, p0, name, groups=1, m_chunk=None, deps=()):
    s = a.shape[0]
    m, n = a.shape[1] // groups, b.shape[1] // groups
    assert into.shape[1:] == (m, n)
    mc = m if m_chunk is None else m_chunk
    nm = m // mc
    t = min(512, s)
    nt = s // t

    def body(a_ref, b_ref, into_ref, o_ref, acc):
        ti = pl.program_id(2)

        @pl.when(ti == 0)
        def _():
            acc[...] = jnp.zeros_like(acc)

        acc[...] += _tn(a_ref[...], b_ref[...])

        @pl.when(ti == nt - 1)
        def _():
            o_ref[...] = acc[...].astype(o_ref.dtype)

    return _pcall(
        body, (a, b, into), deps, name=name, grid=(groups, nm, nt),
        out_shape=jax.ShapeDtypeStruct(into.shape, into.dtype),
        in_specs=[pl.BlockSpec((t, mc), lambda gi, mi, ti: (ti, gi * nm + mi)),
                  pl.BlockSpec((t, n), lambda gi, mi, ti: (ti, gi)), ANY],
        out_specs=pl.BlockSpec((None, mc, n), lambda gi, mi, ti: (p0 + gi, mi, 0)),
        scratch_shapes=[pltpu.VMEM((mc, n), F32)],
        input_output_aliases={2: 0},
        compiler_params=_cparams(3, VMEM_MID),
    )


def _rope_tables(positions):
    half = ROPE // 2
    inv = ROPE_THETA ** (-jnp.arange(half, dtype=F32) * 2.0 / ROPE)
    ang = positions.astype(F32)[:, None] * inv
    cos, sin = jnp.cos(ang), jnp.sin(ang)
    zero = jnp.zeros((positions.shape[0], LANES - ROPE), F32)
    return jnp.concatenate([cos, cos, zero], axis=1), jnp.concatenate([-sin, sin, zero], axis=1)


def _kv_specs(t):
    return [_full((1, D_MODEL)), _full((D_MODEL, KV_RANK)), _full((D_MODEL, LANES)), _full((1, KV_RANK)),
            _full((N_HEADS, KV_RANK, NOPE)), _full((N_HEADS, KV_RANK, V_DIM)),
            _full((1, NOPE)), _full((1, LANES)), _rows(t, LANES), _rows(t, LANES)]


def _kv_fwd(x, ln, wc, wpe, gl, wuk, wuv, gkn, gkr, cos, sin, name, deps=()):
    s = x.shape[0]
    t = min(256, s)

    def body(x_ref, ln_ref, wc_ref, wpe_ref, gl_ref, wuk_ref, wuv_ref, gkn_ref, gkr_ref, cos_ref, sin_ref,
             k_ref, v_ref):
        hn = _rms(x_ref[...], ln_ref[...], D_MODEL)[0].astype(BF16)
        clat = _nn(hn, wc_ref[...])
        kpe = _nn(hn, wpe_ref[...])
        cn = _rms(clat, gl_ref[...], KV_RANK)[0].astype(BF16)
        sspe = jnp.sum(kpe * kpe, axis=-1, keepdims=True)
        cs, sn = cos_ref[...], sin_ref[...]
        for h in range(N_HEADS):
            kn = _nn(cn, wuk_ref[h])
            r = lax.rsqrt((jnp.sum(kn * kn, axis=-1, keepdims=True) + sspe) * (1.0 / QK_DIM) + EPS)
            k_ref[:, h * QK_PAD:h * QK_PAD + NOPE] = ((kn * r) * gkn_ref[...]).astype(BF16)
            z = (kpe * r) * gkr_ref[...]
            k_ref[:, h * QK_PAD + NOPE:(h + 1) * QK_PAD] = (z * cs + _swap_halves(z) * sn).astype(BF16)
            v_ref[:, h * V_DIM:(h + 1) * V_DIM] = _nn(cn, wuv_ref[h]).astype(BF16)

    return _pcall(
        body, (x, ln, wc, wpe, gl, wuk, wuv, gkn, gkr, cos, sin), deps, name=name, grid=(s // t,),
        out_shape=[jax.ShapeDtypeStruct((s, N_HEADS * QK_PAD), BF16), jax.ShapeDtypeStruct((s, N_HEADS * V_DIM), BF16)],
        in_specs=[_rows(t, D_MODEL)] + _kv_specs(t),
        out_specs=[_rows(t, N_HEADS * QK_PAD), _rows(t, N_HEADS * V_DIM)],
        compiler_params=_cparams(1, VMEM_MID),
    )


def _kv_bwd(x, dxin, dks, dvs, ln, wc, wpe, gl, wuk, wuv, gkn, gkr, cos, sin, name):
    s = x.shape[0]
    t = min(256, s)
    nk = len(dks)

    def body(*refs):
        x_ref, dxin_ref = refs[:2]
        dk_refs = refs[2:2 + nk]
        dv_refs = refs[2 + nk:2 + 2 * nk]
        (ln_ref, wc_ref, wpe_ref, gl_ref, wuk_ref, wuv_ref, gkn_ref, gkr_ref, cos_ref, sin_ref,
         dx_ref, hn_ref, cn_ref, dkn_ref, dvb_ref, dcc_ref, dpe_ref,
         dln_ref, dgl_ref, dgkn_ref, dgkr_ref) = refs[2 + 2 * nk:]
        x = x_ref[...]
        ln = ln_ref[...]
        y, rx = _rms(x, ln, D_MODEL)
        hn = y.astype(BF16)
        hn_ref[...] = hn
        clat = _nn(hn, wc_ref[...])
        kpe = _nn(hn, wpe_ref[...])
        gl = gl_ref[...]
        cy, rc = _rms(clat, gl, KV_RANK)
        cn = cy.astype(BF16)
        cn_ref[...] = cn
        sspe = jnp.sum(kpe * kpe, axis=-1, keepdims=True)
        cs, sn = cos_ref[...], sin_ref[...]
        gkn, gkr = gkn_ref[...], gkr_ref[...]
        dc = jnp.zeros((t, KV_RANK), F32)
        dkpe = jnp.zeros((t, LANES), F32)
        dgkn = jnp.zeros((1, NOPE), F32)
        dgkr = jnp.zeros((1, LANES), F32)
        for h in range(N_HEADS):
            kn = _nn(cn, wuk_ref[h])
            r = lax.rsqrt((jnp.sum(kn * kn, axis=-1, keepdims=True) + sspe) * (1.0 / QK_DIM) + EPS)
            lo, mid, hi = h * QK_PAD, h * QK_PAD + NOPE, (h + 1) * QK_PAD
            dko = dk_refs[0][:, lo:mid]
            dkr = dk_refs[0][:, mid:hi]
            dvh = dv_refs[0][:, h * V_DIM:(h + 1) * V_DIM]
            for j in range(1, nk):
                dko = dko + dk_refs[j][:, lo:mid]
                dkr = dkr + dk_refs[j][:, mid:hi]
                dvh = dvh + dv_refs[j][:, h * V_DIM:(h + 1) * V_DIM]
            dz = dkr * cs - _swap_halves(dkr) * sn
            un = dko * gkn
            ur = dz * gkr
            sm = (jnp.sum(kn * un, axis=-1, keepdims=True) + jnp.sum(kpe * ur, axis=-1, keepdims=True)) * (1.0 / QK_DIM)
            coef = r * r * r * sm
            dkn = (r * un - kn * coef).astype(BF16)
            dkpe = dkpe + (r * ur - kpe * coef)
            dgkn = dgkn + jnp.sum(dko * (kn * r), axis=0, keepdims=True)
            dgkr = dgkr + jnp.sum(dz * (kpe * r), axis=0, keepdims=True)
            dkn_ref[:, h * NOPE:(h + 1) * NOPE] = dkn
            dvb = dvh.astype(BF16)
            dvb_ref[:, h * V_DIM:(h + 1) * V_DIM] = dvb
            dc = dc + _nt(dkn, wuk_ref[h]) + _nt(dvb, wuv_ref[h])
        dclat, dgl = _rms_bwd(clat, rc, gl, dc, KV_RANK)
        dcc = dclat.astype(BF16)
        dpe = dkpe.astype(BF16)
        dcc_ref[...] = dcc
        dpe_ref[...] = dpe
        dhn = _nt(dcc, wc_ref[...]) + _nt(dpe, wpe_ref[...])
        dxn, dln = _rms_bwd(x, rx, ln, dhn, D_MODEL)
        dx_ref[...] = dxin_ref[...] + dxn

        @pl.when(pl.program_id(0) == 0)
        def _():
            dln_ref[...] = jnp.zeros_like(dln_ref)
            dgl_ref[...] = jnp.zeros_like(dgl_ref)
            dgkn_ref[...] = jnp.zeros_like(dgkn_ref)
            dgkr_ref[...] = jnp.zeros_like(dgkr_ref)

        dln_ref[...] += dln
        dgl_ref[...] += dgl
        dgkn_ref[...] += dgkn
        dgkr_ref[...] += dgkr

    def tok(cols, dt):
        return jax.ShapeDtypeStruct((s, cols), dt)

    def vec(cols):
        return jax.ShapeDtypeStruct((1, cols), F32)

    return pl.pallas_call(
        body, name=name, grid=(s // t,),
        out_shape=[tok(D_MODEL, F32), tok(D_MODEL, BF16), tok(KV_RANK, BF16), tok(N_HEADS * NOPE, BF16),
                   tok(N_HEADS * V_DIM, BF16), tok(KV_RANK, BF16), tok(LANES, BF16),
                   vec(D_MODEL), vec(KV_RANK), vec(NOPE), vec(LANES)],
        in_specs=[_rows(t, D_MODEL), _rows(t, D_MODEL)] + [_rows(t, N_HEADS * QK_PAD)] * nk
                 + [_rows(t, N_HEADS * V_DIM)] * nk + _kv_specs(t),
        out_specs=[_rows(t, D_MODEL), _rows(t, D_MODEL), _rows(t, KV_RANK), _rows(t, N_HEADS * NOPE),
                   _rows(t, N_HEADS * V_DIM), _rows(t, KV_RANK), _rows(t, LANES),
                   _full((1, D_MODEL)), _full((1, KV_RANK)), _full((1, NOPE)), _full((1, LANES))],
        compiler_params=_cparams(1, VMEM_BIG),
    )(x, dxin, *dks, *dvs, ln, wc, wpe, gl, wuk, wuv, gkn, gkr, cos, sin)


def _q_specs(t):
    return [_full((1, D_MODEL)), _full((D_MODEL, Q_RANK)), _full((1, Q_RANK)), _full((N_HEADS, Q_RANK, QK_PAD)),
            _full((1, NOPE)), _full((1, LANES)), _rows(t, LANES), _rows(t, LANES)]


def _q_fwd(x, ln, wdq, gql, wuq, gqn, gqr, cos, sin, name, deps=()):
    s = x.shape[0]
    t = min(256, s)

    def body(x_ref, ln_ref, wdq_ref, gql_ref, wuq_ref, gqn_ref, gqr_ref, cos_ref, sin_ref, q_ref):
        hn = _rms(x_ref[...], ln_ref[...], D_MODEL)[0].astype(BF16)
        cqn = _rms(_nn(hn, wdq_ref[...]), gql_ref[...], Q_RANK)[0].astype(BF16)
        cs, sn = cos_ref[...], sin_ref[...]
        for h in range(N_HEADS):
            qa = _nn(cqn, wuq_ref[h])
            r = lax.rsqrt(jnp.sum(qa * qa, axis=-1, keepdims=True) * (1.0 / QK_DIM) + EPS)
            q_ref[:, h * QK_PAD:h * QK_PAD + NOPE] = ((qa[:, :NOPE] * r) * gqn_ref[...]).astype(BF16)
            z = (qa[:, NOPE:] * r) * gqr_ref[...]
            q_ref[:, h * QK_PAD + NOPE:(h + 1) * QK_PAD] = (z * cs + _swap_halves(z) * sn).astype(BF16)

    return _pcall(
        body, (x, ln, wdq, gql, wuq, gqn, gqr, cos, sin), deps, name=name, grid=(s // t,),
        out_shape=jax.ShapeDtypeStruct((s, N_HEADS * QK_PAD), BF16),
        in_specs=[_rows(t, D_MODEL)] + _q_specs(t),
        out_specs=_rows(t, N_HEADS * QK_PAD),
        compiler_params=_cparams(1, VMEM_MID),
    )


def _q_bwd(x, dxin, dq, ln, wdq, gql, wuq, gqn, gqr, cos, sin, name):
    s = x.shape[0]
    t = min(256, s)

    def body(x_ref, dxin_ref, dq_ref, ln_ref, wdq_ref, gql_ref, wuq_ref, gqn_ref, gqr_ref, cos_ref, sin_ref,
             dx_ref, hn_ref, cqn_ref, dqa_ref, dcq_ref, dln_ref, dgql_ref, dgqn_ref, dgqr_ref):
        x = x_ref[...]
        ln = ln_ref[...]
        y, rx = _rms(x, ln, D_MODEL)
        hn = y.astype(BF16)
        hn_ref[...] = hn
        cqp = _nn(hn, wdq_ref[...])
        gql = gql_ref[...]
        cy, rc = _rms(cqp, gql, Q_RANK)
        cqn = cy.astype(BF16)
        cqn_ref[...] = cqn
        cs, sn = cos_ref[...], sin_ref[...]
        gqn, gqr = gqn_ref[...], gqr_ref[...]
        dcq = jnp.zeros((t, Q_RANK), F32)
        dgqn = jnp.zeros((1, NOPE), F32)
        dgqr = jnp.zeros((1, LANES), F32)
        for h in range(N_HEADS):
            qa = _nn(cqn, wuq_ref[h])
            qn, qr = qa[:, :NOPE], qa[:, NOPE:]
            r = lax.rsqrt(jnp.sum(qa * qa, axis=-1, keepdims=True) * (1.0 / QK_DIM) + EPS)
            dqo = dq_ref[:, h * QK_PAD:h * QK_PAD + NOPE]
            dqr = dq_ref[:, h * QK_PAD + NOPE:(h + 1) * QK_PAD]
            dz = dqr * cs - _swap_halves(dqr) * sn
            un = dqo * gqn
            ur = dz * gqr
            sm = (jnp.sum(qn * un, axis=-1, keepdims=True) + jnp.sum(qr * ur, axis=-1, keepdims=True)) * (1.0 / QK_DIM)
            coef = r * r * r * sm
            dqa = jnp.concatenate([r * un - qn * coef, r * ur - qr * coef], axis=1).astype(BF16)
            dgqn = dgqn + jnp.sum(dqo * (qn * r), axis=0, keepdims=True)
            dgqr = dgqr + jnp.sum(dz * (qr * r), axis=0, keepdims=True)
            dqa_ref[:, h * QK_PAD:(h + 1) * QK_PAD] = dqa
            dcq = dcq + _nt(dqa, wuq_ref[h])
        dcqp, dgql = _rms_bwd(cqp, rc, gql, dcq, Q_RANK)
        dcqb = dcqp.astype(BF16)
        dcq_ref[...] = dcqb
        dhn = _nt(dcqb, wdq_ref[...])
        dxn, dln = _rms_bwd(x, rx, ln, dhn, D_MODEL)
        dx_ref[...] = dxin_ref[...] + dxn

        @pl.when(pl.program_id(0) == 0)
        def _():
            dln_ref[...] = jnp.zeros_like(dln_ref)
            dgql_ref[...] = jnp.zeros_like(dgql_ref)
            dgqn_ref[...] = jnp.zeros_like(dgqn_ref)
            dgqr_ref[...] = jnp.zeros_like(dgqr_ref)

        dln_ref[...] += dln
        dgql_ref[...] += dgql
        dgqn_ref[...] += dgqn
        dgqr_ref[...] += dgqr

    def tok(cols, dt):
        return jax.ShapeDtypeStruct((s, cols), dt)

    def vec(cols):
        return jax.ShapeDtypeStruct((1, cols), F32)

    return pl.pallas_call(
        body, name=name, grid=(s // t,),
        out_shape=[tok(D_MODEL, F32), tok(D_MODEL, BF16), tok(Q_RANK, BF16), tok(N_HEADS * QK_PAD, BF16),
                   tok(Q_RANK, BF16), vec(D_MODEL), vec(Q_RANK), vec(NOPE), vec(LANES)],
        in_specs=[_rows(t, D_MODEL), _rows(t, D_MODEL), _rows(t, N_HEADS * QK_PAD)] + _q_specs(t),
        out_specs=[_rows(t, D_MODEL), _rows(t, D_MODEL), _rows(t, Q_RANK), _rows(t, N_HEADS * QK_PAD),
                   _rows(t, Q_RANK), _full((1, D_MODEL)), _full((1, Q_RANK)), _full((1, NOPE)), _full((1, LANES))],
        compiler_params=_cparams(1, VMEM_MID),
    )(x, dxin, dq, ln, wdq, gql, wuq, gqn, gqr, cos, sin)


SM_SCALE = 1.0 / math.sqrt(QK_DIM)
NEG = -1e30


def _diag_mask(t):
    qpos = lax.broadcasted_iota(jnp.int32, (t, t), 0)
    kpos = lax.broadcasted_iota(jnp.int32, (t, t), 1)
    return lax.shift_right_logical(kpos, 6) <= lax.shift_right_logical(qpos, 6)


def _att_fwd(q, k, v, name):
    s = q.shape[0]
    t = min(512, s)
    nb = s // t

    def body(q_ref, k_ref, v_ref, o_ref, lse_ref):
        qi = pl.program_id(1)
        qq = q_ref[...]

        def block(ki, carry, masked):
            m_old, l_old, acc = carry
            rows = pl.ds(pl.multiple_of(ki * t, t), t)
            sc = _nt(qq, k_ref[rows, :]) * SM_SCALE
            if masked:
                sc = jnp.where(_diag_mask(t), sc, NEG)
            m_new = jnp.maximum(m_old, jnp.max(sc, axis=-1, keepdims=True))
            p = jnp.exp(sc - m_new)
            alpha = jnp.exp(m_old - m_new)
            l_new = alpha * l_old + jnp.sum(p, axis=-1, keepdims=True)
            acc = alpha * acc + _nn(p.astype(BF16), v_ref[rows, :])
            return m_new, l_new, acc

        init = (jnp.full((t, 1), NEG, F32), jnp.zeros((t, 1), F32), jnp.zeros((t, V_DIM), F32))
        carry = lax.fori_loop(0, qi, lambda ki, c: block(ki, c, False), init)
        m_fin, l_fin, acc = block(qi, carry, True)
        o_ref[...] = (acc / l_fin).astype(BF16)
        lse_ref[...] = jnp.broadcast_to(m_fin + jnp.log(l_fin), (t, LANES))

    return pl.pallas_call(
        body, name=name, grid=(N_HEADS, nb),
        out_shape=[jax.ShapeDtypeStruct((s, N_HEADS * V_DIM), BF16), jax.ShapeDtypeStruct((s, N_HEADS * LANES), F32)],
        in_specs=[pl.BlockSpec((t, QK_PAD), lambda h, qi: (qi, h)),
                  pl.BlockSpec((s, QK_PAD), lambda h, qi: (0, h)),
                  pl.BlockSpec((s, V_DIM), lambda h, qi: (0, h))],
        out_specs=[pl.BlockSpec((t, V_DIM), lambda h, qi: (qi, h)),
                   pl.BlockSpec((t, LANES), lambda h, qi: (qi, h))],
        compiler_params=_cparams(2, VMEM_MID),
    )(q, k, v)


def _att_bwd(q, k, v, do, o, lse, name, deps=()):
    s = q.shape[0]
    t = min(512, s)
    nb = s // t

    def body(q_ref, k_ref, v_ref, do_ref, o_ref, lse_ref, dq_ref, dk_ref, dv_ref):
        ki = pl.program_id(1)
        kk, vv = k_ref[...], v_ref[...]

        @pl.when(ki == 0)
        def _():
            dq_ref[...] = jnp.zeros_like(dq_ref)

        def block(qi, carry, masked):
            dk, dv = carry
            rows = pl.ds(pl.multiple_of(qi * t, t), t)
            qq, dob = q_ref[rows, :], do_ref[rows, :]
            sc = _nt(qq, kk) * SM_SCALE
            if masked:
                sc = jnp.where(_diag_mask(t), sc, NEG)
            p = jnp.exp(sc - lse_ref[rows, :][:, :1])
            dp = _nt(dob, vv)
            dsum = jnp.sum(dob.astype(F32) * o_ref[rows, :].astype(F32), axis=-1, keepdims=True)
            ds = (p * (dp - dsum) * SM_SCALE).astype(BF16)
            dq_ref[rows, :] += _nn(ds, kk)
            return dk + _tn(ds, qq), dv + _tn(p.astype(BF16), dob)

        carry = block(ki, (jnp.zeros((t, QK_PAD), F32), jnp.zeros((t, V_DIM), F32)), True)
        dk, dv = lax.fori_loop(ki + 1, nb, lambda qi, c: block(qi, c, False), carry)
        dk_ref[...] = dk
        dv_ref[...] = dv

    def head(h, ki):
        return (0, h)

    def kblock(h, ki):
        return (ki, h)

    return _pcall(
        body, (q, k, v, do, o, lse), deps, name=name, grid=(N_HEADS, nb),
        out_shape=[jax.ShapeDtypeStruct((s, N_HEADS * QK_PAD), F32), jax.ShapeDtypeStruct((s, N_HEADS * QK_PAD), F32),
                   jax.ShapeDtypeStruct((s, N_HEADS * V_DIM), F32)],
        in_specs=[pl.BlockSpec((s, QK_PAD), head), pl.BlockSpec((t, QK_PAD), kblock), pl.BlockSpec((t, V_DIM), kblock),
                  pl.BlockSpec((s, V_DIM), head), pl.BlockSpec((s, V_DIM), head), pl.BlockSpec((s, LANES), head)],
        out_specs=[pl.BlockSpec((s, QK_PAD), head), pl.BlockSpec((t, QK_PAD), kblock), pl.BlockSpec((t, V_DIM), kblock)],
        compiler_params=_cparams(2, VMEM_MID),
    )


def _o_fwd(x, o, wo, name):
    s = x.shape[0]
    t = min(512, s)

    def body(x_ref, o_ref, wo_ref, xo_ref):
        xo_ref[...] = x_ref[...] + _nn(o_ref[...], wo_ref[...])

    return pl.pallas_call(
        body, name=name, grid=(s // t,),
        out_shape=jax.ShapeDtypeStruct((s, D_MODEL), F32),
        in_specs=[_rows(t, D_MODEL), _rows(t, D_MODEL), _full((D_MODEL, D_MODEL))],
        out_specs=_rows(t, D_MODEL),
        compiler_params=_cparams(1, VMEM_MID),
    )(x, o, wo)


def _o_bwd(dx, wo, name, deps=()):
    s = dx.shape[0]
    t = min(512, s)

    def body(dx_ref, wo_ref, do_ref, dxb_ref):
        dxb = dx_ref[...].astype(BF16)
        dxb_ref[...] = dxb
        do_ref[...] = _nt(dxb, wo_ref[...]).astype(BF16)

    tok = jax.ShapeDtypeStruct((s, D_MODEL), BF16)
    return _pcall(
        body, (dx, wo), deps, name=name, grid=(s // t,),
        out_shape=[tok, tok],
        in_specs=[_rows(t, D_MODEL), _full((D_MODEL, D_MODEL))],
        out_specs=[_rows(t, D_MODEL), _rows(t, D_MODEL)],
        compiler_params=_cparams(1, VMEM_MID),
    )


def _loss_head(y, target, name):
    s = y.shape[0]
    t = min(512, s)

    def body(y_ref, t_ref, dy_ref, sq_ref):
        e = y_ref[...] - t_ref[...]
        dy_ref[...] = e * (1.0 / D_MODEL)

        @pl.when(pl.program_id(0) == 0)
        def _():
            sq_ref[...] = jnp.zeros_like(sq_ref)

        sq_ref[...] += jnp.sum(e * e, axis=0, keepdims=True)

    return pl.pallas_call(
        body, name=name, grid=(s // t,),
        out_shape=[jax.ShapeDtypeStruct((s, D_MODEL), F32), jax.ShapeDtypeStruct((1, D_MODEL), F32)],
        in_specs=[_rows(t, D_MODEL), _rows(t, D_MODEL)],
        out_specs=[_rows(t, D_MODEL), _full((1, D_MODEL))],
        compiler_params=_cparams(1),
    )(y, target)


def _adamw(w, g, m, v, name):
    shape = w.shape
    c = shape[-1]
    r = math.prod(shape[:-1])
    tb = r
    for cand in (512, 256, 128):
        if r % cand == 0 and r > cand:
            tb = cand
            break

    def body(w_ref, g_ref, m_ref, v_ref, d_ref, mo_ref, vo_ref):
        gr = g_ref[...]
        mn = ADAM_B1 * m_ref[...] + (1.0 - ADAM_B1) * gr
        vn = ADAM_B2 * v_ref[...] + (1.0 - ADAM_B2) * (gr * gr)
        m_hat = mn / (1.0 - ADAM_B1 ** ADAM_STEP)
        v_hat = vn / (1.0 - ADAM_B2 ** ADAM_STEP)
        d_ref[...] = -ADAM_LR * (m_hat / (jnp.sqrt(v_hat) + ADAM_EPS) + ADAM_WD * w_ref[...])
        mo_ref[...] = mn
        vo_ref[...] = vn

    spec = pl.BlockSpec((tb, c), lambda i: (i, 0))
    flat = jax.ShapeDtypeStruct((r, c), F32)
    outs = pl.pallas_call(
        body, name=name, grid=(r // tb,),
        out_shape=[flat, flat, flat],
        in_specs=[spec] * 4, out_specs=[spec] * 3,
        compiler_params=_cparams(1),
    )(w.reshape(r, c), g.reshape(r, c), m.reshape(r, c), v.reshape(r, c))
    return [a.reshape(shape) for a in outs]


def _pad_cols(a, width):
    return jnp.pad(a, [(0, 0)] * (a.ndim - 1) + [(0, width - a.shape[-1])])


def _owner_view(a, sz):
    return a.reshape(a.shape[0], N_CHIPS, 2, sz, a.shape[-1])


def kernel(x, positions, ln_mix_a, w_pool, b_pool, pool_scale, ln_ffn, w_gate, w_up, w_down, ln_kv, w_dkv, g_kv_latent, w_uk, w_uv, g_k, ln_mix_b, w_dq, g_q_latent, w_uq, g_q, w_o, loss_target, m_ln_mix_a, m_w_pool, m_b_pool, m_pool_scale, m_ln_ffn, m_w_gate, m_w_up, m_w_down, m_ln_kv, m_w_dkv, m_g_kv_latent, m_w_uk, m_w_uv, m_g_k, m_ln_mix_b, m_w_dq, m_g_q_latent, m_w_uq, m_g_q, m_w_o, v_ln_mix_a, v_w_pool, v_b_pool, v_pool_scale, v_ln_ffn, v_w_gate, v_w_up, v_w_down, v_ln_kv, v_w_dkv, v_g_kv_latent, v_w_uk, v_w_uv, v_g_k, v_ln_mix_b, v_w_dq, v_g_q_latent, v_w_uq, v_g_q, v_w_o):
    weights = dict(ln_mix_a=ln_mix_a, w_pool=w_pool, b_pool=b_pool, pool_scale=pool_scale, ln_ffn=ln_ffn,
                   w_gate=w_gate, w_up=w_up, w_down=w_down, ln_kv=ln_kv, w_dkv=w_dkv, g_kv_latent=g_kv_latent,
                   w_uk=w_uk, w_uv=w_uv, g_k=g_k, ln_mix_b=ln_mix_b, w_dq=w_dq, g_q_latent=g_q_latent,
                   w_uq=w_uq, g_q=g_q, w_o=w_o)
    mom1 = dict(ln_mix_a=m_ln_mix_a, w_pool=m_w_pool, b_pool=m_b_pool, pool_scale=m_pool_scale, ln_ffn=m_ln_ffn,
                w_gate=m_w_gate, w_up=m_w_up, w_down=m_w_down, ln_kv=m_ln_kv, w_dkv=m_w_dkv,
                g_kv_latent=m_g_kv_latent, w_uk=m_w_uk, w_uv=m_w_uv, g_k=m_g_k, ln_mix_b=m_ln_mix_b, w_dq=m_w_dq,
                g_q_latent=m_g_q_latent, w_uq=m_w_uq, g_q=m_g_q, w_o=m_w_o)
    mom2 = dict(ln_mix_a=v_ln_mix_a, w_pool=v_w_pool, b_pool=v_b_pool, pool_scale=v_pool_scale, ln_ffn=v_ln_ffn,
                w_gate=v_w_gate, w_up=v_w_up, w_down=v_w_down, ln_kv=v_ln_kv, w_dkv=v_w_dkv,
                g_kv_latent=v_g_kv_latent, w_uk=v_w_uk, w_uv=v_w_uv, g_k=v_g_k, ln_mix_b=v_ln_mix_b, w_dq=v_w_dq,
                g_q_latent=v_g_q_latent, w_uq=v_w_uq, g_q=v_g_q, w_o=v_w_o)
    names = list(weights)
    dev = 4 * lax.axis_index("x") + 2 * lax.axis_index("y") + lax.axis_index("c")
    core = lax.axis_index("c").astype(jnp.int32).reshape(1)
    chip = (2 * lax.axis_index("x") + lax.axis_index("y")).astype(jnp.int32).reshape(1)

    xs = x[0]
    target = loss_target[0]
    cos, sin = _rope_tables(positions[0])

    small_sh = jnp.concatenate([ln_mix_a.reshape(1, -1), pool_scale.reshape(1, -1), b_pool.reshape(1, -1)], axis=1)
    wp_g, small_g = _all_gather([w_pool.astype(BF16), small_sh], [2, 0], "gather_first")
    wp_all = wp_g.reshape(2, 4, GROUP_DIM, GROUP_DIM)
    small_g = small_g.reshape(N_DEV, 3, 2, LANES)
    ln_a_all = small_g[:, 0].transpose(1, 0, 2).reshape(2, 1, D_MODEL)
    sc_all = small_g[:, 1].transpose(1, 0, 2).reshape(2, 1, D_MODEL)
    bp_all = small_g[:, 2].reshape(N_DEV, 2, 4, 32).transpose(1, 2, 0, 3).reshape(2, 1, D_MODEL)

    def placed(shard):
        buf = lax.empty((shard.shape[0], N_DEV) + shard.shape[1:], shard.dtype)
        return lax.dynamic_update_slice(buf, shard[:, None], (0, dev, 0, 0))

    ffn_sh = jnp.stack([w_gate.transpose(0, 2, 1), w_up.transpose(0, 2, 1), w_down], axis=1).astype(BF16)
    groups = {f"ffn{l}": [placed(ffn_sh[l])] for l in range(4)}
    groups["att"] = [placed(a.astype(BF16)) for a in (
        w_dkv[None, :, :KV_RANK], _pad_cols(w_dkv[None, :, KV_RANK:], LANES), w_uk[None], w_uv[None],
        w_dq, _pad_cols(w_uq, QK_PAD), w_o)]
    spread = {}
    for nm in ("ffn0", "ffn1", "att", "ffn2", "ffn3"):
        spread[nm] = _copies_start(groups[nm], len(groups[nm]), _gather_spread, f"spread_{nm}")
    first_tokens = [st[3] for st in spread.values()]

    def relay(nm, after):
        ssem, rsem, bufs, _ = spread[nm]
        bufs = _copies_wait(bufs, ssem, rsem, after, _blocks_moved(4), f"spread_done_{nm}")
        return _copies_start(bufs, len(bufs), _gather_relay, f"relay_{nm}")

    def gathered(nm, state, after):
        ssem, rsem, bufs, _ = state
        return _copies_wait(bufs, ssem, rsem, after, _blocks_moved(3), f"relay_done_{nm}")

    gkn = g_k[:NOPE].reshape(1, NOPE)
    gkr = _pad_cols(g_k[NOPE:].reshape(1, ROPE), LANES)
    gl = g_kv_latent.reshape(1, KV_RANK)
    lnkv = ln_kv.reshape(1, D_MODEL)

    x_in, x_mid, pooled, gates, ups, w_ffn = [], [], [], [], [], []
    qs, outs, lses = [], [], []
    cur = xs
    rel = None
    for l in range(4):
        x_in.append(cur)
        if l < 2:
            deps = first_tokens if l == 0 else [rel[3]]
            mid, dsave = _mix_fwd(cur, ln_a_all[l], wp_all[l], bp_all[l], sc_all[l], f"mix_fwd{l}", deps=deps)
            pooled.append(dsave)
            if l == 0:
                rel = relay("ffn0", mid)
        else:
            j = l - 2
            qargs = (ln_mix_b[j].reshape(1, -1), wdq_all[j], g_q_latent[j].reshape(1, -1), wuq_all[j],
                     g_q[j, :NOPE].reshape(1, -1), _pad_cols(g_q[j, NOPE:].reshape(1, -1), LANES), cos, sin)
            q = _q_fwd(cur, *qargs, f"q_fwd{j}", deps=[rel[3]])
            o, lse = _att_fwd(q, k_sh, v_sh, f"att_fwd{j}")
            mid = _o_fwd(cur, o, wo_all[j], f"o_fwd{j}")
            qs.append(q)
            outs.append(o)
            lses.append(lse)
        x_mid.append(mid)
        (w_l,) = gathered(f"ffn{l}", rel, mid)
        w_l = w_l.reshape(3, D_FF, D_MODEL)
        w_ffn.append(w_l)
        cur, gate, up = _ffn_fwd(mid, ln_ffn[l].reshape(1, -1), w_l, f"ffn_fwd{l}")
        gates.append(gate)
        ups.append(up)
        if l == 1:
            x_kv = cur
            att_bufs = gathered("att", relay("att", cur), cur)
            wc = att_bufs[0].reshape(D_MODEL, KV_RANK)
            wpe = att_bufs[1].reshape(D_MODEL, LANES)
            wuk_g = att_bufs[2].reshape(N_HEADS, KV_RANK, NOPE)
            wuv_g = att_bufs[3].reshape(N_HEADS, KV_RANK, V_DIM)
            wdq_all = att_bufs[4].reshape(2, D_MODEL, Q_RANK)
            wuq_all = att_bufs[5]
            wo_all = att_bufs[6].reshape(2, D_MODEL, D_MODEL)
            rel = relay("ffn2", cur)
            k_sh, v_sh = _kv_fwd(cur, lnkv, wc, wpe, gl, wuk_g, wuv_g, gkn, gkr, cos, sin, "kv_fwd", deps=[rel[3]])
        elif l < 3:
            rel = relay(f"ffn{l + 1}", cur)

    dx, sq_cols = _loss_head(cur, target, "loss_head")

    small = {}
    sizes = dict(ffn0=FF_SHARD, ffn1=FF_SHARD, ffn2=FF_SHARD, ffn3=FF_SHARD, wo=128, kv512=128, dkv_pe=128,
                 wdq=128, wuqT=QK_PAD, wpool=32)
    big = dict(wo=lax.empty((2, D_MODEL, D_MODEL), BF16), kv512=lax.empty((3, D_MODEL, KV_RANK), BF16),
               dkv_pe=lax.empty((1, D_MODEL, LANES), BF16), wdq=lax.empty((2, D_MODEL, Q_RANK), BF16),
               wuqT=lax.empty((2, N_HEADS * QK_PAD, Q_RANK), BF16), wpool=lax.empty((8, GROUP_DIM, GROUP_DIM), BF16))
    for l in range(4):
        big[f"ffn{l}"] = lax.empty((3, D_FF, D_MODEL), BF16)
    red = {}

    def pair_start(nms, tag):
        arrs = []
        for nm in nms:
            view = _owner_view(big[nm], sizes[nm])
            arrs += [view, lax.empty((view.shape[0], N_CHIPS) + view.shape[3:], BF16)]
        return nms, tag, _copies_start(arrs, len(nms), _pair_send, f"pair_start_{tag}")

    def chip_start(state, after):
        nms, tag, (ssem, rsem, arrs, _) = state
        arrs = _copies_wait(arrs, ssem, rsem, after, _landed, f"pair_done_{tag}")
        out = []
        for t, nm in enumerate(nms):
            part = _pair_sum(arrs[2 * t], arrs[2 * t + 1], core, f"pair_sum_{nm}")
            out += [part, lax.empty((3, part.shape[0]) + part.shape[2:], BF16)]
        return nms, tag, _copies_start(out, len(nms), _chip_send, f"chip_start_{tag}")

    def chip_finish(state, after):
        nms, tag, (ssem, rsem, arrs, _) = state
        arrs = _copies_wait(arrs, ssem, rsem, after, _landed, f"chip_done_{tag}")
        for t, nm in enumerate(nms):
            red[nm] = _chip_sum(arrs[2 * t], arrs[2 * t + 1], chip, f"chip_sum_{nm}")

    dks, dvs = [], []
    pending = None
    bwd_deps = []
    for l in (3, 2, 1, 0):
        key = f"ffn{l}"
        dx, act, dgb, dub, hn, dyb, dln = _ffn_bwd(x_mid[l], dx, gates[l], ups[l], ln_ffn[l].reshape(1, -1),
                                                     w_ffn[l], f"ffn_bwd{l}", deps=bwd_deps)
        bwd_deps = []
        small[f"ln_ffn{l}"] = dln
        if l == 1:
            att_chip = chip_start(att_pair, dx)
            tn_deps = [att_chip[2][3]]
        else:
            tn_deps = []
        if pending:
            chip_finish(pending, dx)
            pending = None
        big[key] = _tn_matmul(dgb, hn, big[key], 0, f"dw_gate{l}", m_chunk=FF_HALF, deps=tn_deps)
        big[key] = _tn_matmul(dub, hn, big[key], 1, f"dw_up{l}", m_chunk=FF_HALF)
        big[key] = _tn_matmul(act, dyb, big[key], 2, f"dw_down{l}", m_chunk=FF_HALF)
        if l == 1:
            chip_finish(att_chip, big[key])
        ffn_pair = pair_start([key], key)
        if l >= 2:
            j = l - 2
            do, dxb = _o_bwd(dx, wo_all[j], f"o_bwd{j}", deps=[ffn_pair[2][3]])
            big["wo"] = _tn_matmul(outs[j], dxb, big["wo"], j, f"dw_o{j}")
            ffn_chip = chip_start(ffn_pair, big["wo"])
            dq, dk, dv = _att_bwd(qs[j], k_sh, v_sh, do, outs[j], lses[j], f"att_bwd{j}", deps=[ffn_chip[2][3]])
            chip_finish(ffn_chip, dq)
            dks.append(dk)
            dvs.append(dv)
            qargs = (ln_mix_b[j].reshape(1, -1), wdq_all[j], g_q_latent[j].reshape(1, -1), wuq_all[j],
                     g_q[j, :NOPE].reshape(1, -1), _pad_cols(g_q[j, NOPE:].reshape(1, -1), LANES), cos, sin)
            dx, hnq, cqn, dqa, dcq, dln, dgql, dgqn, dgqr = _q_bwd(x_in[l], dx, dq, *qargs, f"q_bwd{j}")
            small[f"ln_mix_b{j}"] = dln
            small[f"g_q_latent{j}"] = dgql
            small[f"g_q{j}"] = jnp.concatenate([dgqn, dgqr[:, :ROPE]], axis=1)
            big["wdq"] = _tn_matmul(hnq, dcq, big["wdq"], j, f"dw_dq{j}")
            big["wuqT"] = _tn_matmul(dqa, cqn, big["wuqT"], j, f"dw_uq{j}")
            if l == 2:
                (dx, hnk, cn, dknb, dvb, dccb, dpeb, dlnkv, dgl, dgkn, dgkr) = _kv_bwd(
                    x_kv, dx, dks, dvs, lnkv, wc, wpe, gl, wuk_g, wuv_g, gkn, gkr, cos, sin, "kv_bwd")
                small["ln_kv"] = dlnkv
                small["g_kv_latent"] = dgl
                small["g_k"] = jnp.concatenate([dgkn, dgkr[:, :ROPE]], axis=1)
                big["kv512"] = _tn_matmul(dknb, cn, big["kv512"], 0, "dw_uk")
                big["kv512"] = _tn_matmul(dvb, cn, big["kv512"], 1, "dw_uv")
                big["kv512"] = _tn_matmul(hnk, dccb, big["kv512"], 2, "dw_dkv_c")
                big["dkv_pe"] = _tn_matmul(hnk, dpeb, big["dkv_pe"], 0, "dw_dkv_pe")
                att_pair = pair_start(["wo", "kv512", "dkv_pe", "wdq", "wuqT"], "att")
                bwd_deps = [att_pair[2][3]]
        else:
            dx, dyp, dsc, db, dln = _mix_bwd(x_in[l], dx, pooled[l], ln_a_all[l], wp_all[l], bp_all[l], sc_all[l],
                                             f"mix_bwd{l}", deps=[ffn_pair[2][3]])
            small[f"ln_mix_a{l}"] = dln
            small[f"pool_scale{l}"] = dsc
            small[f"b_pool{l}"] = db
            ffn_chip = chip_start(ffn_pair, dx)
            big["wpool"] = _tn_matmul(pooled[l], dyp, big["wpool"], 4 * l, f"dw_pool{l}", groups=4,
                                      deps=[ffn_chip[2][3]])
            if l == 1:
                pending = ffn_chip
            else:
                chip_finish(ffn_chip, big["wpool"])
    grad_x = dx[None]
    pool_pair = pair_start(["wpool"], "wpool")
    pool_chip = chip_start(pool_pair, pool_pair[2][3])
    chip_finish(pool_chip, pool_chip[2][3])

    vec_names = (["loss"] + [f"ln_ffn{l}" for l in range(4)] + ["ln_kv", "g_kv_latent", "g_k"]
                 + [f"{p}{j}" for p in ("ln_mix_b", "g_q_latent", "g_q") for j in range(2)]
                 + [f"{p}{l}" for p in ("ln_mix_a", "pool_scale", "b_pool") for l in range(2)])
    small["loss"] = sq_cols
    widths = [small[nm].shape[1] for nm in vec_names]
    padded = [-(-w // LANES) * LANES for w in widths]
    packed = jnp.concatenate([_pad_cols(small[nm], pw) for nm, pw in zip(vec_names, padded)], axis=1)
    (all_vecs,) = _all_gather([packed], [0], "gather_vectors")
    total = _sum_lead(all_vecs, "sum_vectors")
    vec = {}
    off = 0
    for nm, w, pw in zip(vec_names, widths, padded):
        vec[nm] = total[0, off:off + w]
        off += pw
    loss = 0.5 * jnp.sum(vec["loss"]) * (1.0 / D_MODEL)

    def own_cols(full, width):
        return lax.dynamic_slice_in_dim(full, dev * width, width, axis=full.ndim - 1)

    grads = dict(
        ln_mix_a=own_cols(jnp.stack([vec["ln_mix_a0"], vec["ln_mix_a1"]]), LANES),
        w_pool=red["wpool"].reshape(2, 4, 32, GROUP_DIM),
        b_pool=own_cols(jnp.stack([vec["b_pool0"], vec["b_pool1"]]).reshape(2, 4, GROUP_DIM), 32),
        pool_scale=own_cols(jnp.stack([vec["pool_scale0"], vec["pool_scale1"]]), LANES),
        ln_ffn=jnp.stack([vec[f"ln_ffn{l}"] for l in range(4)]),
        w_gate=jnp.stack([red[f"ffn{l}"][0] for l in range(4)]).transpose(0, 2, 1),
        w_up=jnp.stack([red[f"ffn{l}"][1] for l in range(4)]).transpose(0, 2, 1),
        w_down=jnp.stack([red[f"ffn{l}"][2] for l in range(4)]),
        ln_kv=vec["ln_kv"],
        w_dkv=jnp.concatenate([red["kv512"][2], red["dkv_pe"][0][:, :ROPE]], axis=1),
        g_kv_latent=vec["g_kv_latent"],
        w_uk=red["kv512"][0].T,
        w_uv=red["kv512"][1].T,
        g_k=vec["g_k"],
        ln_mix_b=jnp.stack([vec["ln_mix_b0"], vec["ln_mix_b1"]]),
        w_dq=red["wdq"],
        g_q_latent=jnp.stack([vec["g_q_latent0"], vec["g_q_latent1"]]),
        w_uq=red["wuqT"].transpose(0, 2, 1)[:, :, :QK_DIM],
        g_q=jnp.stack([vec["g_q0"], vec["g_q1"]]),
        w_o=red["wo"],
    )

    deltas, new_m, new_v = {}, {}, {}
    for nm in names:
        w = weights[nm]
        shape = w.shape if w.ndim > 1 else (1, w.shape[0])
        d, mo, vo = _adamw(w.reshape(shape), grads[nm].reshape(shape), mom1[nm].reshape(shape),
                           mom2[nm].reshape(shape), f"adamw_{nm}")
        deltas[nm], new_m[nm], new_v[nm] = d.reshape(w.shape), mo.reshape(w.shape), vo.reshape(w.shape)

    return (loss, grad_x, *[grads[nm].reshape(weights[nm].shape) for nm in names], *[deltas[nm] for nm in names],
            *[new_m[nm] for nm in names], *[new_v[nm] for nm in names])
```

```python
import functools
import math

import jax
import jax.numpy as jnp
from jax import lax
from jax.experimental import pallas as pl
from jax.experimental.pallas import tpu as pltpu

F32 = jnp.float32
BF16 = jnp.bfloat16
MESH = pl.DeviceIdType.MESH

D_MODEL = 1024
D_FF = 2816
N_DEV = 8
N_CHIPS = 4
FF_SHARD = D_FF // N_DEV
FF_HALF = D_FF // 2
N_HEADS = 8
NOPE = 128
ROPE = 64
QK_DIM = NOPE + ROPE
QK_PAD = 256
V_DIM = 128
Q_RANK = 256
KV_RANK = 512
POOL_WINDOWS = (2, 4, 8, 16)
GROUP_DIM = 256
HALO = 128
CHUNK = 64
ROPE_THETA = 10000.0
EPS = 1e-6
LANES = 128

ADAM_LR = 0.001
ADAM_B1 = 0.9
ADAM_B2 = 0.999
ADAM_EPS = 1e-08
ADAM_WD = 0.01
ADAM_STEP = 10

VMEM_BIG = 56 * 2**20
VMEM_MID = 40 * 2**20


def _nn(a, b):
    return lax.dot_general(a, b, (((1,), (0,)), ((), ())), preferred_element_type=F32)


def _nt(a, b):
    return lax.dot_general(a, b, (((1,), (1,)), ((), ())), preferred_element_type=F32)


def _tn(a, b):
    return lax.dot_general(a, b, (((0,), (0,)), ((), ())), preferred_element_type=F32)


def _rms(x, g, n):
    r = lax.rsqrt(jnp.sum(x * x, axis=-1, keepdims=True) * (1.0 / n) + EPS)
    return (x * r) * g, r


def _rms_bwd(x, r, g, dy, n):
    u = dy * g
    s = jnp.sum(x * u, axis=-1, keepdims=True) * (1.0 / n)
    dx = r * u - x * (r * r * r * s)
    dg = jnp.sum(dy * (x * r), axis=0, keepdims=True)
    return dx, dg


def _swap_halves(z):
    lane = lax.broadcasted_iota(jnp.int32, z.shape, 1)
    return jnp.where(lane < ROPE // 2, pltpu.roll(z, LANES - ROPE // 2, 1), pltpu.roll(z, ROPE // 2, 1))


def _sigmoid(x):
    return 1.0 / (1.0 + jnp.exp(-x))


def _cparams(n_grid, vmem=None):
    return pltpu.CompilerParams(dimension_semantics=("arbitrary",) * n_grid, vmem_limit_bytes=vmem)


def _rows(t, cols):
    return pl.BlockSpec((t, cols), lambda i: (i, 0))


def _full(shape):
    nd = len(shape)
    return pl.BlockSpec(shape, lambda *_: (0,) * nd)


ANY = pl.BlockSpec(memory_space=pl.ANY)


def _pcall(body, args, deps, *, in_specs, **kw):
    n_in, n_dep = len(args), len(deps)

    def ordered(*refs):
        body(*refs[:n_in], *refs[n_in + n_dep:])

    return pl.pallas_call(ordered, in_specs=list(in_specs) + [ANY] * n_dep, **kw)(*args, *deps)


def _place():
    x, y, c = lax.axis_index("x"), lax.axis_index("y"), lax.axis_index("c")
    return x, y, c


def _all_gather(shards, axes, name):
    n = len(shards)
    out_shape = [jax.ShapeDtypeStruct(s.shape[:a] + (N_DEV,) + s.shape[a:], s.dtype) for s, a in zip(shards, axes)]

    def body(*refs):
        ins, outs = refs[:n], refs[n:2 * n]
        send_sems, recv_sems, local_sems = refs[2 * n:]
        x, y, c = _place()
        me, sibling = (x, y, c), (x, y, 1 - c)
        chips = [(1 - x, y), (x, 1 - y), (1 - x, 1 - y)]

        def slot(t, dev):
            idx = 4 * dev[0] + 2 * dev[1] + dev[2]
            return outs[t].at[(slice(None),) * axes[t] + (idx,)]

        def copy(t, k, block, to, src=None):
            return pltpu.make_async_remote_copy(
                src_ref=slot(t, block) if src is None else src, dst_ref=slot(t, block),
                send_sem=send_sems.at[t, k], recv_sem=recv_sems.at[t, k],
                device_id=to, device_id_type=MESH)

        mine = [pltpu.make_async_copy(ins[t], slot(t, me), local_sems.at[t]) for t in range(n)]
        for cp in mine:
            cp.start()
        first = []
        for t in range(n):
            first.append(copy(t, 0, me, sibling, src=ins[t]))
            first += [copy(t, 1 + j, me, (*chip, c), src=ins[t]) for j, chip in enumerate(chips)]
        for cp in first:
            cp.start()
        passed = []
        for j, chip in enumerate(chips):
            for t in range(n):
                copy(t, 1 + j, (*chip, c), me).wait_recv()
                cp = copy(t, 4 + j, (*chip, c), sibling)
                cp.start()
                passed.append(cp)
        for t in range(n):
            copy(t, 0, sibling, me).wait_recv()
            for j, chip in enumerate(chips):
                copy(t, 4 + j, (*chip, 1 - c), me).wait_recv()
        for cp in first + passed:
            cp.wait_send()
        for cp in mine:
            cp.wait()

    return pl.pallas_call(
        body, name=name, out_shape=out_shape,
        in_specs=[ANY] * n, out_specs=[ANY] * n,
        scratch_shapes=[pltpu.SemaphoreType.DMA((n, 7)), pltpu.SemaphoreType.DMA((n, 7)),
                        pltpu.SemaphoreType.DMA((n,))],
    )(*shards)


HBM = pl.BlockSpec(memory_space=pltpu.HBM)
SEM = pl.BlockSpec(memory_space=pltpu.SEMAPHORE)
EFFECT = pltpu.SideEffectType.DATAFLOW_SIDE_EFFECTING


def _copies_start(arrays, n_sems, plan, name, deps=()):
    n, nd = len(arrays), len(deps)

    def body(*refs):
        for cp in plan(refs[:n], refs[n + nd], refs[n + nd + 1]):
            cp.start()
        refs[-1][...] = jnp.zeros_like(refs[-1])

    outs = pl.pallas_call(
        body, name=name,
        out_shape=(pltpu.SemaphoreType.DMA((n_sems,)), pltpu.SemaphoreType.DMA((n_sems,)),
                   *[pltpu.HBM(a.shape, a.dtype) for a in arrays], jax.ShapeDtypeStruct((8, LANES), F32)),
        in_specs=[HBM] * n + [ANY] * nd,
        out_specs=(SEM, SEM, *[HBM] * n, pl.BlockSpec(memory_space=pltpu.VMEM)),
        input_output_aliases={i: 2 + i for i in range(n)},
        compiler_params=pltpu.CompilerParams(has_side_effects=EFFECT),
    )(*[pltpu.with_memory_space_constraint(a, pltpu.HBM) for a in arrays], *deps)
    return outs[0], outs[1], list(outs[2:2 + n]), outs[-1]


def _copies_wait(arrays, send_sems, recv_sems, after, plan, name):
    n = len(arrays)

    def body(*refs):
        for cp in plan(refs[:n], refs[n], refs[n + 1]):
            cp.wait_send()
            cp.wait_recv()

    outs = pl.pallas_call(
        body, name=name,
        out_shape=tuple(pltpu.HBM(a.shape, a.dtype) for a in arrays),
        in_specs=[HBM] * n + [SEM, SEM, ANY], out_specs=tuple([HBM] * n),
        input_output_aliases={i: i for i in range(n)},
        compiler_params=pltpu.CompilerParams(has_side_effects=EFFECT),
    )(*arrays, send_sems, recv_sems, after)
    return list(outs)


def _remote(src, dst, send_sems, recv_sems, t, to):
    return pltpu.make_async_remote_copy(src_ref=src, dst_ref=dst, send_sem=send_sems.at[t], recv_sem=recv_sems.at[t],
                                        device_id=to, device_id_type=MESH)


def _dev_index(x, y, c):
    return 4 * x + 2 * y + c


def _gather_spread(bufs, send_sems, recv_sems):
    x, y, c = _place()
    mine = _dev_index(x, y, c)
    peers = [(x, y, 1 - c), (1 - x, y, c), (x, 1 - y, c), (1 - x, 1 - y, c)]
    return [_remote(g.at[k, mine], g.at[k, mine], send_sems, recv_sems, t, peer)
            for t, g in enumerate(bufs) for peer in peers for k in range(g.shape[0])]


def _gather_relay(bufs, send_sems, recv_sems):
    x, y, c = _place()
    blocks = [_dev_index(1 - x, y, c), _dev_index(x, 1 - y, c), _dev_index(1 - x, 1 - y, c)]
    return [_remote(g.at[k, b], g.at[k, b], send_sems, recv_sems, t, (x, y, 1 - c))
            for t, g in enumerate(bufs) for b in blocks for k in range(g.shape[0])]


def _blocks_moved(count):
    def plan(bufs, send_sems, recv_sems):
        x, y, c = _place()
        return [_remote(g.at[:, pl.ds(0, count)], g.at[:, pl.ds(0, count)], send_sems, recv_sems, t, (x, y, 1 - c))
                for t, g in enumerate(bufs)]
    return plan


def _pair_send(arrs, send_sems, recv_sems):
    x, y, c = _place()
    return [_remote(arrs[2 * t].at[p, k, 1 - c], arrs[2 * t + 1].at[p, k], send_sems, recv_sems, t, (x, y, 1 - c))
            for t in range(len(arrs) // 2) for p in range(arrs[2 * t].shape[0]) for k in range(N_CHIPS)]


def _chip_send(arrs, send_sems, recv_sems):
    x, y, c = _place()
    chips = [(1 - x, y), (x, 1 - y), (1 - x, 1 - y)]
    return [_remote(arrs[2 * t].at[p, 2 * px + py], arrs[2 * t + 1].at[j, p], send_sems, recv_sems, t, (px, py, c))
            for t in range(len(arrs) // 2) for j, (px, py) in enumerate(chips) for p in range(arrs[2 * t].shape[0])]


def _landed(arrs, send_sems, recv_sems):
    x, y, c = _place()
    return [_remote(arrs[2 * t + 1], arrs[2 * t + 1], send_sems, recv_sems, t, (x, y, 1 - c))
            for t in range(len(arrs) // 2)]


def _pair_sum(grad, landed, core, name):
    p, _, _, sz, c = grad.shape

    def body(core_ref, g_ref, l_ref, o_ref):
        o_ref[...] = (g_ref[...].astype(F32) + l_ref[...].astype(F32)).astype(o_ref.dtype)

    out = pl.pallas_call(
        body, name=name,
        grid_spec=pltpu.PrefetchScalarGridSpec(
            num_scalar_prefetch=1, grid=(p * N_CHIPS,),
            in_specs=[pl.BlockSpec((None, None, sz, c), lambda i, cr: (i, cr[0], 0, 0)),
                      pl.BlockSpec((None, sz, c), lambda i, cr: (i, 0, 0))],
            out_specs=pl.BlockSpec((None, sz, c), lambda i, cr: (i, 0, 0))),
        out_shape=jax.ShapeDtypeStruct((p * N_CHIPS, sz, c), grad.dtype),
        compiler_params=_cparams(1),
    )(core, grad.reshape(p * N_CHIPS, 2, sz, c), landed.reshape(p * N_CHIPS, sz, c))
    return out.reshape(p, N_CHIPS, sz, c)


def _chip_sum(parts, landed, chip, name):
    p, _, sz, c = parts.shape

    def body(chip_ref, a_ref, l_ref, o_ref):
        acc = a_ref[...].astype(F32)
        for j in range(3):
            acc = acc + l_ref[j].astype(F32)
        o_ref[...] = acc

    return pl.pallas_call(
        body, name=name,
        grid_spec=pltpu.PrefetchScalarGridSpec(
            num_scalar_prefetch=1, grid=(p,),
            in_specs=[pl.BlockSpec((None, None, sz, c), lambda i, cr: (i, cr[0], 0, 0)),
                      pl.BlockSpec((3, None, sz, c), lambda i, cr: (0, i, 0, 0))],
            out_specs=pl.BlockSpec((None, sz, c), lambda i, cr: (i, 0, 0))),
        out_shape=jax.ShapeDtypeStruct((p, sz, c), F32),
        compiler_params=_cparams(1),
    )(chip, parts, landed)


def _sum_lead(a, name, out_dtype=F32):
    k = a.shape[0]
    rest = a.shape[1:]
    r, c = rest[-2], rest[-1]
    lead = math.prod(rest[:-2])
    a3 = a.reshape(k, lead * r, c)
    rows = lead * r
    tb = rows
    for cand in (512, 256, 128, 64, 32, 16, 8):
        if rows % cand == 0 and rows > cand:
            tb = cand
            break

    def body(a_ref, o_ref):
        acc = a_ref[0].astype(F32)
        for i in range(1, k):
            acc = acc + a_ref[i].astype(F32)
        o_ref[...] = acc.astype(out_dtype)

    out = pl.pallas_call(
        body, name=name, grid=(rows // tb,),
        out_shape=jax.ShapeDtypeStruct((rows, c), out_dtype),
        in_specs=[pl.BlockSpec((k, tb, c), lambda i: (0, i, 0))],
        out_specs=pl.BlockSpec((tb, c), lambda i: (i, 0)),
        compiler_params=_cparams(1),
    )(a3)
    return out.reshape(rest)


def _band(t, w, offset, valid):
    r = lax.broadcasted_iota(jnp.int32, (t, t + HALO), 0)
    col = lax.broadcasted_iota(jnp.int32, (t, t + HALO), 1)
    diff = offset(r, col)
    return jnp.where((diff >= 0) & (diff < w) & valid(col), 1.0, 0.0).astype(BF16)


def _split_dot(band, v):
    hi = v.astype(BF16)
    lo = (v - hi.astype(F32)).astype(BF16)
    return _nn(band, hi) + _nn(band, lo)


def _mix_fwd(x, g, wp, b, sc, name, deps=()):
    s = x.shape[0]
    t = min(256, s)
    rb = t // HALO

    def body(x_ref, xh_ref, g_ref, wp_ref, b_ref, sc_ref, xo_ref, d_ref):
        i = pl.program_id(0)
        gg = g_ref[...]
        h, _ = _rms(x_ref[...], gg, D_MODEL)
        hh, _ = _rms(xh_ref[...], gg, D_MODEL)
        hext = jnp.concatenate([hh, h], axis=0)
        tok = i * t + lax.broadcasted_iota(jnp.int32, (t, 1), 0)
        for gi, w in enumerate(POOL_WINDOWS):
            sl = slice(gi * GROUP_DIM, (gi + 1) * GROUP_DIM)
            band = _band(t, w, lambda r, col: r + HALO - col, lambda col: (col >= HALO) | (i > 0))
            win = _split_dot(band, hext[:, sl])
            cnt = jnp.minimum(tok + 1, w).astype(F32)
            dbf = (win / cnt - h[:, sl]).astype(BF16)
            d_ref[:, sl] = dbf
            ypre = _nn(dbf, wp_ref[gi]) + b_ref[:, sl]
            xo_ref[:, sl] = x_ref[:, sl] + ypre * sc_ref[:, sl]

    return _pcall(
        body, (x, x, g, wp, b, sc), deps, name=name, grid=(s // t,),
        out_shape=[jax.ShapeDtypeStruct((s, D_MODEL), F32), jax.ShapeDtypeStruct((s, D_MODEL), BF16)],
        in_specs=[_rows(t, D_MODEL),
                  pl.BlockSpec((HALO, D_MODEL), lambda i: (jnp.maximum(i * rb - 1, 0), 0)),
                  _full((1, D_MODEL)), _full((4, GROUP_DIM, GROUP_DIM)), _full((1, D_MODEL)), _full((1, D_MODEL))],
        out_specs=[_rows(t, D_MODEL), _rows(t, D_MODEL)],
        compiler_params=_cparams(1, VMEM_MID),
    )


def _mix_bwd(x, dy, d, g, wp, b, sc, name, deps=()):
    s = x.shape[0]
    t = min(256, s)
    rb = t // HALO
    nb = s // t
    last_halo = s // HALO - 1

    def body(x_ref, dy_ref, dyn_ref, d_ref, g_ref, wp_ref, b_ref, sc_ref,
             dx_ref, dyp_ref, dsc_ref, db_ref, dln_ref):
        i = pl.program_id(0)
        x = x_ref[...]
        gg = g_ref[...]
        dy = dy_ref[...]
        sc = sc_ref[...]
        dyp32 = dy * sc
        dyp = dyp32.astype(BF16)
        dyph = (dyn_ref[...] * sc).astype(BF16)
        dyp_ref[...] = dyp
        tok = i * t + lax.broadcasted_iota(jnp.int32, (t + HALO, 1), 0)
        dh, dsc = [], []
        for gi, w in enumerate(POOL_WINDOWS):
            sl = slice(gi * GROUP_DIM, (gi + 1) * GROUP_DIM)
            ypre = _nn(d_ref[:, sl], wp_ref[gi]) + b_ref[:, sl]
            dsc.append(jnp.sum(dy[:, sl] * ypre, axis=0, keepdims=True))
            dd = _nt(dyp[:, sl], wp_ref[gi])
            ddh = _nt(dyph[:, sl], wp_ref[gi])
            cnt = jnp.minimum(tok + 1, w).astype(F32)
            ddext = jnp.concatenate([dd, ddh], axis=0) / cnt
            band = _band(t, w, lambda r, col: col - r, lambda col: (col < t) | (i < nb - 1))
            dh.append(_split_dot(band, ddext) - dd)
        dh = jnp.concatenate(dh, axis=1)
        _, r = _rms(x, gg, D_MODEL)
        dxn, dg = _rms_bwd(x, r, gg, dh, D_MODEL)
        dx_ref[...] = dy + dxn

        @pl.when(i == 0)
        def _():
            dsc_ref[...] = jnp.zeros_like(dsc_ref)
            db_ref[...] = jnp.zeros_like(db_ref)
            dln_ref[...] = jnp.zeros_like(dln_ref)

        dsc_ref[...] += jnp.concatenate(dsc, axis=1)
        db_ref[...] += jnp.sum(dyp32, axis=0, keepdims=True)
        dln_ref[...] += dg

    vec = jax.ShapeDtypeStruct((1, D_MODEL), F32)
    return _pcall(
        body, (x, dy, dy, d, g, wp, b, sc), deps, name=name, grid=(nb,),
        out_shape=[jax.ShapeDtypeStruct((s, D_MODEL), F32), jax.ShapeDtypeStruct((s, D_MODEL), BF16), vec, vec, vec],
        in_specs=[_rows(t, D_MODEL), _rows(t, D_MODEL),
                  pl.BlockSpec((HALO, D_MODEL), lambda i: (jnp.minimum((i + 1) * rb, last_halo), 0)),
                  _rows(t, D_MODEL),
                  _full((1, D_MODEL)), _full((4, GROUP_DIM, GROUP_DIM)), _full((1, D_MODEL)), _full((1, D_MODEL))],
        out_specs=[_rows(t, D_MODEL), _rows(t, D_MODEL), _full((1, D_MODEL)), _full((1, D_MODEL)), _full((1, D_MODEL))],
        compiler_params=_cparams(1, VMEM_MID),
    )


def _load_weights(w_hbm, w_vmem, sem):
    @pl.when(pl.program_id(0) == 0)
    def _():
        cp = pltpu.make_async_copy(w_hbm, w_vmem, sem)
        cp.start()
        cp.wait()


def _ffn_fwd(x, g, w, name):
    s = x.shape[0]
    t = min(256, s)

    def body(x_ref, g_ref, w_hbm, xo_ref, gate_ref, up_ref, w_ref, sem):
        _load_weights(w_hbm, w_ref, sem)
        x = x_ref[...]
        hn = _rms(x, g_ref[...], D_MODEL)[0].astype(BF16)
        acc = x
        for c in range(2):
            rs = slice(c * FF_HALF, (c + 1) * FF_HALF)
            gt = _nt(hn, w_ref[0, rs, :])
            up = _nt(hn, w_ref[1, rs, :])
            gate_ref[:, rs] = gt.astype(BF16)
            up_ref[:, rs] = up.astype(BF16)
            act = ((gt * _sigmoid(gt)) * up).astype(BF16)
            acc = acc + _nn(act, w_ref[2, rs, :])
        xo_ref[...] = acc

    hid = jax.ShapeDtypeStruct((s, D_FF), BF16)
    return pl.pallas_call(
        body, name=name, grid=(s // t,),
        out_shape=[jax.ShapeDtypeStruct((s, D_MODEL), F32), hid, hid],
        in_specs=[_rows(t, D_MODEL), _full((1, D_MODEL)), ANY],
        out_specs=[_rows(t, D_MODEL), _rows(t, D_FF), _rows(t, D_FF)],
        scratch_shapes=[pltpu.VMEM((3, D_FF, D_MODEL), BF16), pltpu.SemaphoreType.DMA],
        compiler_params=_cparams(1, VMEM_BIG),
    )(x, g, w)


def _ffn_bwd(x, dy, gate, up, g, w, name, deps=()):
    s = x.shape[0]
    t = min(256, s)

    def body(x_ref, dy_ref, gate_ref, up_ref, g_ref, w_hbm,
             dx_ref, act_ref, dg_ref, du_ref, hn_ref, dyb_ref, dln_ref, w_ref, sem):
        _load_weights(w_hbm, w_ref, sem)
        x = x_ref[...]
        gg = g_ref[...]
        y, r = _rms(x, gg, D_MODEL)
        hn = y.astype(BF16)
        hn_ref[...] = hn
        dy = dy_ref[...]
        dyb = dy.astype(BF16)
        dyb_ref[...] = dyb
        dh = jnp.zeros((t, D_MODEL), F32)
        for c in range(2):
            rs = slice(c * FF_HALF, (c + 1) * FF_HALF)
            gt = gate_ref[:, rs].astype(F32)
            u = up_ref[:, rs].astype(F32)
            sg = _sigmoid(gt)
            sl = gt * sg
            act_ref[:, rs] = (sl * u).astype(BF16)
            dact = _nt(dyb, w_ref[2, rs, :])
            dg = (dact * u * (sg * (1.0 + gt * (1.0 - sg)))).astype(BF16)
            du = (dact * sl).astype(BF16)
            dg_ref[:, rs] = dg
            du_ref[:, rs] = du
            dh = dh + _nn(dg, w_ref[0, rs, :]) + _nn(du, w_ref[1, rs, :])
        dxn, dgl = _rms_bwd(x, r, gg, dh, D_MODEL)
        dx_ref[...] = dy + dxn

        @pl.when(pl.program_id(0) == 0)
        def _():
            dln_ref[...] = jnp.zeros_like(dln_ref)

        dln_ref[...] += dgl

    hid = jax.ShapeDtypeStruct((s, D_FF), BF16)
    tok = jax.ShapeDtypeStruct((s, D_MODEL), BF16)
    return _pcall(
        body, (x, dy, gate, up, g, w), deps, name=name, grid=(s // t,),
        out_shape=[jax.ShapeDtypeStruct((s, D_MODEL), F32), hid, hid, hid, tok, tok,
                   jax.ShapeDtypeStruct((1, D_MODEL), F32)],
        in_specs=[_rows(t, D_MODEL), _rows(t, D_MODEL), _rows(t, D_FF), _rows(t, D_FF), _full((1, D_MODEL)), ANY],
        out_specs=[_rows(t, D_MODEL), _rows(t, D_FF), _rows(t, D_FF), _rows(t, D_FF),
                   _rows(t, D_MODEL), _rows(t, D_MODEL), _full((1, D_MODEL))],
        scratch_shapes=[pltpu.VMEM((3, D_FF, D_MODEL), BF16), pltpu.SemaphoreType.DMA],
        compiler_params=_cparams(1, VMEM_BIG),
    )


def _tn_matmul(a, b, into, p0, name, groups=1, m_chunk=None, deps=()):
    s = a.shape[0]
    m, n = a.shape[1] // groups, b.shape[1] // groups
    assert into.shape[1:] == (m, n)
    mc = m if m_chunk is None else m_chunk
    nm = m // mc
    t = min(512, s)
    nt = s // t

    def body(a_ref, b_ref, into_ref, o_ref, acc):
        ti = pl.program_id(2)

        @pl.when(ti == 0)
        def _():
            acc[...] = jnp.zeros_like(acc)

        acc[...] += _tn(a_ref[...], b_ref[...])

        @pl.when(ti == nt - 1)
        def _():
            o_ref[...] = acc[...].astype(o_ref.dtype)

    return _pcall(
        body, (a, b, into), deps, name=name, grid=(groups, nm, nt),
        out_shape=jax.ShapeDtypeStruct(into.shape, into.dtype),
        in_specs=[pl.BlockSpec((t, mc), lambda gi, mi, ti: (ti, gi * nm + mi)),
                  pl.BlockSpec((t, n), lambda gi, mi, ti: (ti, gi)), ANY],
        out_specs=pl.BlockSpec((None, mc, n), lambda gi, mi, ti: (p0 + gi, mi, 0)),
        scratch_shapes=[pltpu.VMEM((mc, n), F32)],
        input_output_aliases={2: 0},
        compiler_params=_cparams(3, VMEM_MID),
    )


def _rope_tables(positions):
    half = ROPE // 2
    inv = ROPE_THETA ** (-jnp.arange(half, dtype=F32) * 2.0 / ROPE)
    ang = positions.astype(F32)[:, None] * inv
    cos, sin = jnp.cos(ang), jnp.sin(ang)
    zero = jnp.zeros((positions.shape[0], LANES - ROPE), F32)
    return jnp.concatenate([cos, cos, zero], axis=1), jnp.concatenate([-sin, sin, zero], axis=1)


def _kv_specs(t):
    return [_full((1, D_MODEL)), _full((D_MODEL, KV_RANK)), _full((D_MODEL, LANES)), _full((1, KV_RANK)),
            _full((N_HEADS, KV_RANK, NOPE)), _full((N_HEADS, KV_RANK, V_DIM)),
            _full((1, NOPE)), _full((1, LANES)), _rows(t, LANES), _rows(t, LANES)]


def _kv_fwd(x, ln, wc, wpe, gl, wuk, wuv, gkn, gkr, cos, sin, name, deps=()):
    s = x.shape[0]
    t = min(256, s)

    def body(x_ref, ln_ref, wc_ref, wpe_ref, gl_ref, wuk_ref, wuv_ref, gkn_ref, gkr_ref, cos_ref, sin_ref,
             k_ref, v_ref):
        hn = _rms(x_ref[...], ln_ref[...], D_MODEL)[0].astype(BF16)
        clat = _nn(hn, wc_ref[...])
        kpe = _nn(hn, wpe_ref[...])
        cn = _rms(clat, gl_ref[...], KV_RANK)[0].astype(BF16)
        sspe = jnp.sum(kpe * kpe, axis=-1, keepdims=True)
        cs, sn = cos_ref[...], sin_ref[...]
        for h in range(N_HEADS):
            kn = _nn(cn, wuk_ref[h])
            r = lax.rsqrt((jnp.sum(kn * kn, axis=-1, keepdims=True) + sspe) * (1.0 / QK_DIM) + EPS)
            k_ref[:, h * QK_PAD:h * QK_PAD + NOPE] = ((kn * r) * gkn_ref[...]).astype(BF16)
            z = (kpe * r) * gkr_ref[...]
            k_ref[:, h * QK_PAD + NOPE:(h + 1) * QK_PAD] = (z * cs + _swap_halves(z) * sn).astype(BF16)
            v_ref[:, h * V_DIM:(h + 1) * V_DIM] = _nn(cn, wuv_ref[h]).astype(BF16)

    return _pcall(
        body, (x, ln, wc, wpe, gl, wuk, wuv, gkn, gkr, cos, sin), deps, name=name, grid=(s // t,),
        out_shape=[jax.ShapeDtypeStruct((s, N_HEADS * QK_PAD), BF16), jax.ShapeDtypeStruct((s, N_HEADS * V_DIM), BF16)],
        in_specs=[_rows(t, D_MODEL)] + _kv_specs(t),
        out_specs=[_rows(t, N_HEADS * QK_PAD), _rows(t, N_HEADS * V_DIM)],
        compiler_params=_cparams(1, VMEM_MID),
    )


def _kv_bwd(x, dxin, dks, dvs, ln, wc, wpe, gl, wuk, wuv, gkn, gkr, cos, sin, name):
    s = x.shape[0]
    t = min(256, s)
    nk = len(dks)

    def body(*refs):
        x_ref, dxin_ref = refs[:2]
        dk_refs = refs[2:2 + nk]
        dv_refs = refs[2 + nk:2 + 2 * nk]
        (ln_ref, wc_ref, wpe_ref, gl_ref, wuk_ref, wuv_ref, gkn_ref, gkr_ref, cos_ref, sin_ref,
         dx_ref, hn_ref, cn_ref, dkn_ref, dvb_ref, dcc_ref, dpe_ref,
         dln_ref, dgl_ref, dgkn_ref, dgkr_ref) = refs[2 + 2 * nk:]
        x = x_ref[...]
        ln = ln_ref[...]
        y, rx = _rms(x, ln, D_MODEL)
        hn = y.astype(BF16)
        hn_ref[...] = hn
        clat = _nn(hn, wc_ref[...])
        kpe = _nn(hn, wpe_ref[...])
        gl = gl_ref[...]
        cy, rc = _rms(clat, gl, KV_RANK)
        cn = cy.astype(BF16)
        cn_ref[...] = cn
        sspe = jnp.sum(kpe * kpe, axis=-1, keepdims=True)
        cs, sn = cos_ref[...], sin_ref[...]
        gkn, gkr = gkn_ref[...], gkr_ref[...]
        dc = jnp.zeros((t, KV_RANK), F32)
        dkpe = jnp.zeros((t, LANES), F32)
        dgkn = jnp.zeros((1, NOPE), F32)
        dgkr = jnp.zeros((1, LANES), F32)
        for h in range(N_HEADS):
            kn = _nn(cn, wuk_ref[h])
            r = lax.rsqrt((jnp.sum(kn * kn, axis=-1, keepdims=True) + sspe) * (1.0 / QK_DIM) + EPS)
            lo, mid, hi = h * QK_PAD, h * QK_PAD + NOPE, (h + 1) * QK_PAD
            dko = dk_refs[0][:, lo:mid]
            dkr = dk_refs[0][:, mid:hi]
            dvh = dv_refs[0][:, h * V_DIM:(h + 1) * V_DIM]
            for j in range(1, nk):
                dko = dko + dk_refs[j][:, lo:mid]
                dkr = dkr + dk_refs[j][:, mid:hi]
                dvh = dvh + dv_refs[j][:, h * V_DIM:(h + 1) * V_DIM]
            dz = dkr * cs - _swap_halves(dkr) * sn
            un = dko * gkn
            ur = dz * gkr
            sm = (jnp.sum(kn * un, axis=-1, keepdims=True) + jnp.sum(kpe * ur, axis=-1, keepdims=True)) * (1.0 / QK_DIM)
            coef = r * r * r * sm
            dkn = (r * un - kn * coef).astype(BF16)
            dkpe = dkpe + (r * ur - kpe * coef)
            dgkn = dgkn + jnp.sum(dko * (kn * r), axis=0, keepdims=True)
            dgkr = dgkr + jnp.sum(dz * (kpe * r), axis=0, keepdims=True)
            dkn_ref[:, h * NOPE:(h + 1) * NOPE] = dkn
            dvb = dvh.astype(BF16)
            dvb_ref[:, h * V_DIM:(h + 1) * V_DIM] = dvb
            dc = dc + _nt(dkn, wuk_ref[h]) + _nt(dvb, wuv_ref[h])
        dclat, dgl = _rms_bwd(clat, rc, gl, dc, KV_RANK)
        dcc = dclat.astype(BF16)
        dpe = dkpe.astype(BF16)
        dcc_ref[...] = dcc
        dpe_ref[...] = dpe
        dhn = _nt(dcc, wc_ref[...]) + _nt(dpe, wpe_ref[...])
        dxn, dln = _rms_bwd(x, rx, ln, dhn, D_MODEL)
        dx_ref[...] = dxin_ref[...] + dxn

        @pl.when(pl.program_id(0) == 0)
        def _():
            dln_ref[...] = jnp.zeros_like(dln_ref)
            dgl_ref[...] = jnp.zeros_like(dgl_ref)
            dgkn_ref[...] = jnp.zeros_like(dgkn_ref)
            dgkr_ref[...] = jnp.zeros_like(dgkr_ref)

        dln_ref[...] += dln
        dgl_ref[...] += dgl
        dgkn_ref[...] += dgkn
        dgkr_ref[...] += dgkr

    def tok(cols, dt):
        return jax.ShapeDtypeStruct((s, cols), dt)

    def vec(cols):
        return jax.ShapeDtypeStruct((1, cols), F32)

    return pl.pallas_call(
        body, name=name, grid=(s // t,),
        out_shape=[tok(D_MODEL, F32), tok(D_MODEL, BF16), tok(KV_RANK, BF16), tok(N_HEADS * NOPE, BF16),
                   tok(N_HEADS * V_DIM, BF16), tok(KV_RANK, BF16), tok(LANES, BF16),
                   vec(D_MODEL), vec(KV_RANK), vec(NOPE), vec(LANES)],
        in_specs=[_rows(t, D_MODEL), _rows(t, D_MODEL)] + [_rows(t, N_HEADS * QK_PAD)] * nk
                 + [_rows(t, N_HEADS * V_DIM)] * nk + _kv_specs(t),
        out_specs=[_rows(t, D_MODEL), _rows(t, D_MODEL), _rows(t, KV_RANK), _rows(t, N_HEADS * NOPE),
                   _rows(t, N_HEADS * V_DIM), _rows(t, KV_RANK), _rows(t, LANES),
                   _full((1, D_MODEL)), _full((1, KV_RANK)), _full((1, NOPE)), _full((1, LANES))],
        compiler_params=_cparams(1, VMEM_BIG),
    )(x, dxin, *dks, *dvs, ln, wc, wpe, gl, wuk, wuv, gkn, gkr, cos, sin)


def _q_specs(t):
    return [_full((1, D_MODEL)), _full((D_MODEL, Q_RANK)), _full((1, Q_RANK)), _full((N_HEADS, Q_RANK, QK_PAD)),
            _full((1, NOPE)), _full((1, LANES)), _rows(t, LANES), _rows(t, LANES)]


def _q_fwd(x, ln, wdq, gql, wuq, gqn, gqr, cos, sin, name, deps=()):
    s = x.shape[0]
    t = min(256, s)

    def body(x_ref, ln_ref, wdq_ref, gql_ref, wuq_ref, gqn_ref, gqr_ref, cos_ref, sin_ref, q_ref):
        hn = _rms(x_ref[...], ln_ref[...], D_MODEL)[0].astype(BF16)
        cqn = _rms(_nn(hn, wdq_ref[...]), gql_ref[...], Q_RANK)[0].astype(BF16)
        cs, sn = cos_ref[...], sin_ref[...]
        for h in range(N_HEADS):
            qa = _nn(cqn, wuq_ref[h])
            r = lax.rsqrt(jnp.sum(qa * qa, axis=-1, keepdims=True) * (1.0 / QK_DIM) + EPS)
            q_ref[:, h * QK_PAD:h * QK_PAD + NOPE] = ((qa[:, :NOPE] * r) * gqn_ref[...]).astype(BF16)
            z = (qa[:, NOPE:] * r) * gqr_ref[...]
            q_ref[:, h * QK_PAD + NOPE:(h + 1) * QK_PAD] = (z * cs + _swap_halves(z) * sn).astype(BF16)

    return _pcall(
        body, (x, ln, wdq, gql, wuq, gqn, gqr, cos, sin), deps, name=name, grid=(s // t,),
        out_shape=jax.ShapeDtypeStruct((s, N_HEADS * QK_PAD), BF16),
        in_specs=[_rows(t, D_MODEL)] + _q_specs(t),
        out_specs=_rows(t, N_HEADS * QK_PAD),
        compiler_params=_cparams(1, VMEM_MID),
    )


def _q_bwd(x, dxin, dq, ln, wdq, gql, wuq, gqn, gqr, cos, sin, name):
    s = x.shape[0]
    t = min(256, s)

    def body(x_ref, dxin_ref, dq_ref, ln_ref, wdq_ref, gql_ref, wuq_ref, gqn_ref, gqr_ref, cos_ref, sin_ref,
             dx_ref, hn_ref, cqn_ref, dqa_ref, dcq_ref, dln_ref, dgql_ref, dgqn_ref, dgqr_ref):
        x = x_ref[...]
        ln = ln_ref[...]
        y, rx = _rms(x, ln, D_MODEL)
        hn = y.astype(BF16)
        hn_ref[...] = hn
        cqp = _nn(hn, wdq_ref[...])
        gql = gql_ref[...]
        cy, rc = _rms(cqp, gql, Q_RANK)
        cqn = cy.astype(BF16)
        cqn_ref[...] = cqn
        cs, sn = cos_ref[...], sin_ref[...]
        gqn, gqr = gqn_ref[...], gqr_ref[...]
        dcq = jnp.zeros((t, Q_RANK), F32)
        dgqn = jnp.zeros((1, NOPE), F32)
        dgqr = jnp.zeros((1, LANES), F32)
        for h in range(N_HEADS):
            qa = _nn(cqn, wuq_ref[h])
            qn, qr = qa[:, :NOPE], qa[:, NOPE:]
            r = lax.rsqrt(jnp.sum(qa * qa, axis=-1, keepdims=True) * (1.0 / QK_DIM) + EPS)
            dqo = dq_ref[:, h * QK_PAD:h * QK_PAD + NOPE]
            dqr = dq_ref[:, h * QK_PAD + NOPE:(h + 1) * QK_PAD]
            dz = dqr * cs - _swap_halves(dqr) * sn
            un = dqo * gqn
            ur = dz * gqr
            sm = (jnp.sum(qn * un, axis=-1, keepdims=True) + jnp.sum(qr * ur, axis=-1, keepdims=True)) * (1.0 / QK_DIM)
            coef = r * r * r * sm
            dqa = jnp.concatenate([r * un - qn * coef, r * ur - qr * coef], axis=1).astype(BF16)
            dgqn = dgqn + jnp.sum(dqo * (qn * r), axis=0, keepdims=True)
            dgqr = dgqr + jnp.sum(dz * (qr * r), axis=0, keepdims=True)
            dqa_ref[:, h * QK_PAD:(h + 1) * QK_PAD] = dqa
            dcq = dcq + _nt(dqa, wuq_ref[h])
        dcqp, dgql = _rms_bwd(cqp, rc, gql, dcq, Q_RANK)
        dcqb = dcqp.astype(BF16)
        dcq_ref[...] = dcqb
        dhn = _nt(dcqb, wdq_ref[...])
        dxn, dln = _rms_bwd(x, rx, ln, dhn, D_MODEL)
        dx_ref[...] = dxin_ref[...] + dxn

        @pl.when(pl.program_id(0) == 0)
        def _():
            dln_ref[...] = jnp.zeros_like(dln_ref)
            dgql_ref[...] = jnp.zeros_like(dgql_ref)
            dgqn_ref[...] = jnp.zeros_like(dgqn_ref)
            dgqr_ref[...] = jnp.zeros_like(dgqr_ref)

        dln_ref[...] += dln
        dgql_ref[...] += dgql
        dgqn_ref[...] += dgqn
        dgqr_ref[...] += dgqr

    def tok(cols, dt):
        return jax.ShapeDtypeStruct((s, cols), dt)

    def vec(cols):
        return jax.ShapeDtypeStruct((1, cols), F32)

    return pl.pallas_call(
        body, name=name, grid=(s // t,),
        out_shape=[tok(D_MODEL, F32), tok(D_MODEL, BF16), tok(Q_RANK, BF16), tok(N_HEADS * QK_PAD, BF16),
                   tok(Q_RANK, BF16), vec(D_MODEL), vec(Q_RANK), vec(NOPE), vec(LANES)],
        in_specs=[_rows(t, D_MODEL), _rows(t, D_MODEL), _rows(t, N_HEADS * QK_PAD)] + _q_specs(t),
        out_specs=[_rows(t, D_MODEL), _rows(t, D_MODEL), _rows(t, Q_RANK), _rows(t, N_HEADS * QK_PAD),
                   _rows(t, Q_RANK), _full((1, D_MODEL)), _full((1, Q_RANK)), _full((1, NOPE)), _full((1, LANES))],
        compiler_params=_cparams(1, VMEM_MID),
    )(x, dxin, dq, ln, wdq, gql, wuq, gqn, gqr, cos, sin)


SM_SCALE = 1.0 / math.sqrt(QK_DIM)
LOG2_E = math.log2(math.e)
EXP2_SCALE = SM_SCALE * LOG2_E
NEG = -1e30


def _diag_mask(t):
    qpos = lax.broadcasted_iota(jnp.int32, (t, t), 0)
    kpos = lax.broadcasted_iota(jnp.int32, (t, t), 1)
    return lax.shift_right_logical(kpos, 6) <= lax.shift_right_logical(qpos, 6)


def _att_fwd(q, k, v, name):
    s = q.shape[0]
    t = min(512, s)
    nb = s // t

    def body(q_ref, k_ref, v_ref, o_ref, lse_ref):
        qi = pl.program_id(1)
        qq = q_ref[...]

        def block(ki, carry, masked):
            m_old, l_old, acc = carry
            rows = pl.ds(pl.multiple_of(ki * t, t), t)
            sc = _nt(qq, k_ref[rows, :])
            if masked:
                sc = jnp.where(_diag_mask(t), sc, NEG)
            m_new = jnp.maximum(m_old, jnp.max(sc, axis=-1, keepdims=True))
            p = jnp.exp2((sc - m_new) * EXP2_SCALE)
            alpha = jnp.exp2((m_old - m_new) * EXP2_SCALE)
            l_new = alpha * l_old + jnp.sum(p, axis=-1, keepdims=True)
            acc = alpha * acc + _nn(p.astype(BF16), v_ref[rows, :])
            return m_new, l_new, acc

        init = (jnp.full((t, 1), NEG, F32), jnp.zeros((t, 1), F32), jnp.zeros((t, V_DIM), F32))
        carry = lax.fori_loop(0, qi, lambda ki, c: block(ki, c, False), init)
        m_fin, l_fin, acc = block(qi, carry, True)
        o_ref[...] = (acc / l_fin).astype(BF16)
        lse_ref[...] = jnp.broadcast_to(m_fin * SM_SCALE + jnp.log(l_fin), (t, LANES))

    return pl.pallas_call(
        body, name=name, grid=(N_HEADS, nb),
        out_shape=[jax.ShapeDtypeStruct((s, N_HEADS * V_DIM), BF16), jax.ShapeDtypeStruct((s, N_HEADS * LANES), F32)],
        in_specs=[pl.BlockSpec((t, QK_PAD), lambda h, qi: (qi, h)),
                  pl.BlockSpec((s, QK_PAD), lambda h, qi: (0, h)),
                  pl.BlockSpec((s, V_DIM), lambda h, qi: (0, h))],
        out_specs=[pl.BlockSpec((t, V_DIM), lambda h, qi: (qi, h)),
                   pl.BlockSpec((t, LANES), lambda h, qi: (qi, h))],
        compiler_params=_cparams(2, VMEM_MID),
    )(q, k, v)


def _att_bwd(q, k, v, do, o, lse, name, deps=()):
    s = q.shape[0]
    t = min(512, s)
    nb = s // t

    def body(q_ref, k_ref, v_ref, do_ref, o_ref, lse_ref, dq_ref, dk_ref, dv_ref):
        ki = pl.program_id(1)
        kk, vv = k_ref[...], v_ref[...]

        @pl.when(ki == 0)
        def _():
            dq_ref[...] = jnp.zeros_like(dq_ref)

        def block(qi, carry, masked):
            dk, dv = carry
            rows = pl.ds(pl.multiple_of(qi * t, t), t)
            qq, dob = q_ref[rows, :], do_ref[rows, :]
            sc = _nt(qq, kk)
            if masked:
                sc = jnp.where(_diag_mask(t), sc, NEG)
            p = jnp.exp2(sc * EXP2_SCALE - lse_ref[rows, :][:, :1] * LOG2_E)
            dp = _nt(dob, vv)
            dsum = jnp.sum(dob.astype(F32) * o_ref[rows, :].astype(F32), axis=-1, keepdims=True)
            ds = (p * (dp - dsum)).astype(BF16)
            dq_ref[rows, :] += _nn(ds, kk)
            return dk + _tn(ds, qq), dv + _tn(p.astype(BF16), dob)

        carry = block(ki, (jnp.zeros((t, QK_PAD), F32), jnp.zeros((t, V_DIM), F32)), True)
        dk, dv = lax.fori_loop(ki + 1, nb, lambda qi, c: block(qi, c, False), carry)
        dk_ref[...] = dk * SM_SCALE
        dv_ref[...] = dv

        @pl.when(ki == nb - 1)
        def _():
            dq_ref[...] = dq_ref[...] * SM_SCALE

    def head(h, ki):
        return (0, h)

    def kblock(h, ki):
        return (ki, h)

    return _pcall(
        body, (q, k, v, do, o, lse), deps, name=name, grid=(N_HEADS, nb),
        out_shape=[jax.ShapeDtypeStruct((s, N_HEADS * QK_PAD), F32), jax.ShapeDtypeStruct((s, N_HEADS * QK_PAD), F32),
                   jax.ShapeDtypeStruct((s, N_HEADS * V_DIM), F32)],
        in_specs=[pl.BlockSpec((s, QK_PAD), head), pl.BlockSpec((t, QK_PAD), kblock), pl.BlockSpec((t, V_DIM), kblock),
                  pl.BlockSpec((s, V_DIM), head), pl.BlockSpec((s, V_DIM), head), pl.BlockSpec((s, LANES), head)],
        out_specs=[pl.BlockSpec((s, QK_PAD), head), pl.BlockSpec((t, QK_PAD), kblock), pl.BlockSpec((t, V_DIM), kblock)],
        compiler_params=_cparams(2, VMEM_MID),
    )


def _o_fwd(x, o, wo, name):
    s = x.shape[0]
    t = min(512, s)

    def body(x_ref, o_ref, wo_ref, xo_ref):
        xo_ref[...] = x_ref[...] + _nn(o_ref[...], wo_ref[...])

    return pl.pallas_call(
        body, name=name, grid=(s // t,),
        out_shape=jax.ShapeDtypeStruct((s, D_MODEL), F32),
        in_specs=[_rows(t, D_MODEL), _rows(t, D_MODEL), _full((D_MODEL, D_MODEL))],
        out_specs=_rows(t, D_MODEL),
        compiler_params=_cparams(1, VMEM_MID),
    )(x, o, wo)


def _o_bwd(dx, wo, name, deps=()):
    s = dx.shape[0]
    t = min(512, s)

    def body(dx_ref, wo_ref, do_ref, dxb_ref):
        dxb = dx_ref[...].astype(BF16)
        dxb_ref[...] = dxb
        do_ref[...] = _nt(dxb, wo_ref[...]).astype(BF16)

    tok = jax.ShapeDtypeStruct((s, D_MODEL), BF16)
    return _pcall(
        body, (dx, wo), deps, name=name, grid=(s // t,),
        out_shape=[tok, tok],
        in_specs=[_rows(t, D_MODEL), _full((D_MODEL, D_MODEL))],
        out_specs=[_rows(t, D_MODEL), _rows(t, D_MODEL)],
        compiler_params=_cparams(1, VMEM_MID),
    )


def _loss_head(y, target, name):
    s = y.shape[0]
    t = min(512, s)

    def body(y_ref, t_ref, dy_ref, sq_ref):
        e = y_ref[...] - t_ref[...]
        dy_ref[...] = e * (1.0 / D_MODEL)

        @pl.when(pl.program_id(0) == 0)
        def _():
            sq_ref[...] = jnp.zeros_like(sq_ref)

        sq_ref[...] += jnp.sum(e * e, axis=0, keepdims=True)

    return pl.pallas_call(
        body, name=name, grid=(s // t,),
        out_shape=[jax.ShapeDtypeStruct((s, D_MODEL), F32), jax.ShapeDtypeStruct((1, D_MODEL), F32)],
        in_specs=[_rows(t, D_MODEL), _rows(t, D_MODEL)],
        out_specs=[_rows(t, D_MODEL), _full((1, D_MODEL))],
        compiler_params=_cparams(1),
    )(y, target)


def _adamw(w, g, m, v, name):
    shape = w.shape
    c = shape[-1]
    r = math.prod(shape[:-1])
    tb = r
    for cand in (512, 256, 128):
        if r % cand == 0 and r > cand:
            tb = cand
            break

    def body(w_ref, g_ref, m_ref, v_ref, d_ref, mo_ref, vo_ref):
        gr = g_ref[...]
        mn = ADAM_B1 * m_ref[...] + (1.0 - ADAM_B1) * gr
        vn = ADAM_B2 * v_ref[...] + (1.0 - ADAM_B2) * (gr * gr)
        m_hat = mn / (1.0 - ADAM_B1 ** ADAM_STEP)
        v_hat = vn / (1.0 - ADAM_B2 ** ADAM_STEP)
        d_ref[...] = -ADAM_LR * (m_hat / (jnp.sqrt(v_hat) + ADAM_EPS) + ADAM_WD * w_ref[...])
        mo_ref[...] = mn
        vo_ref[...] = vn

    spec = pl.BlockSpec((tb, c), lambda i: (i, 0))
    flat = jax.ShapeDtypeStruct((r, c), F32)
    outs = pl.pallas_call(
        body, name=name, grid=(r // tb,),
        out_shape=[flat, flat, flat],
        in_specs=[spec] * 4, out_specs=[spec] * 3,
        compiler_params=_cparams(1),
    )(w.reshape(r, c), g.reshape(r, c), m.reshape(r, c), v.reshape(r, c))
    return [a.reshape(shape) for a in outs]


def _pad_cols(a, width):
    return jnp.pad(a, [(0, 0)] * (a.ndim - 1) + [(0, width - a.shape[-1])])


def _owner_view(a, sz):
    return a.reshape(a.shape[0], N_CHIPS, 2, sz, a.shape[-1])


def kernel(x, positions, ln_mix_a, w_pool, b_pool, pool_scale, ln_ffn, w_gate, w_up, w_down, ln_kv, w_dkv, g_kv_latent, w_uk, w_uv, g_k, ln_mix_b, w_dq, g_q_latent, w_uq, g_q, w_o, loss_target, m_ln_mix_a, m_w_pool, m_b_pool, m_pool_scale, m_ln_ffn, m_w_gate, m_w_up, m_w_down, m_ln_kv, m_w_dkv, m_g_kv_latent, m_w_uk, m_w_uv, m_g_k, m_ln_mix_b, m_w_dq, m_g_q_latent, m_w_uq, m_g_q, m_w_o, v_ln_mix_a, v_w_pool, v_b_pool, v_pool_scale, v_ln_ffn, v_w_gate, v_w_up, v_w_down, v_ln_kv, v_w_dkv, v_g_kv_latent, v_w_uk, v_w_uv, v_g_k, v_ln_mix_b, v_w_dq, v_g_q_latent, v_w_uq, v_g_q, v_w_o):
    weights = dict(ln_mix_a=ln_mix_a, w_pool=w_pool, b_pool=b_pool, pool_scale=pool_scale, ln_ffn=ln_ffn,
                   w_gate=w_gate, w_up=w_up, w_down=w_down, ln_kv=ln_kv, w_dkv=w_dkv, g_kv_latent=g_kv_latent,
                   w_uk=w_uk, w_uv=w_uv, g_k=g_k, ln_mix_b=ln_mix_b, w_dq=w_dq, g_q_latent=g_q_latent,
                   w_uq=w_uq, g_q=g_q, w_o=w_o)
    mom1 = dict(ln_mix_a=m_ln_mix_a, w_pool=m_w_pool, b_pool=m_b_pool, pool_scale=m_pool_scale, ln_ffn=m_ln_ffn,
                w_gate=m_w_gate, w_up=m_w_up, w_down=m_w_down, ln_kv=m_ln_kv, w_dkv=m_w_dkv,
                g_kv_latent=m_g_kv_latent, w_uk=m_w_uk, w_uv=m_w_uv, g_k=m_g_k, ln_mix_b=m_ln_mix_b, w_dq=m_w_dq,
                g_q_latent=m_g_q_latent, w_uq=m_w_uq, g_q=m_g_q, w_o=m_w_o)
    mom2 = dict(ln_mix_a=v_ln_mix_a, w_pool=v_w_pool, b_pool=v_b_pool, pool_scale=v_pool_scale, ln_ffn=v_ln_ffn,
                w_gate=v_w_gate, w_up=v_w_up, w_down=v_w_down, ln_kv=v_ln_kv, w_dkv=v_w_dkv,
                g_kv_latent=v_g_kv_latent, w_uk=v_w_uk, w_uv=v_w_uv, g_k=v_g_k, ln_mix_b=v_ln_mix_b, w_dq=v_w_dq,
                g_q_latent=v_g_q_latent, w_uq=v_w_uq, g_q=v_g_q, w_o=v_w_o)
    names = list(weights)
    dev = 4 * lax.axis_index("x") + 2 * lax.axis_index("y") + lax.axis_index("c")
    core = lax.axis_index("c").astype(jnp.int32).reshape(1)
    chip = (2 * lax.axis_index("x") + lax.axis_index("y")).astype(jnp.int32).reshape(1)

    xs = x[0]
    target = loss_target[0]
    cos, sin = _rope_tables(positions[0])

    small_sh = jnp.concatenate([ln_mix_a.reshape(1, -1), pool_scale.reshape(1, -1), b_pool.reshape(1, -1)], axis=1)
    wp_g, small_g = _all_gather([w_pool.astype(BF16), small_sh], [2, 0], "gather_first")
    wp_all = wp_g.reshape(2, 4, GROUP_DIM, GROUP_DIM)
    small_g = small_g.reshape(N_DEV, 3, 2, LANES)
    ln_a_all = small_g[:, 0].transpose(1, 0, 2).reshape(2, 1, D_MODEL)
    sc_all = small_g[:, 1].transpose(1, 0, 2).reshape(2, 1, D_MODEL)
    bp_all = small_g[:, 2].reshape(N_DEV, 2, 4, 32).transpose(1, 2, 0, 3).reshape(2, 1, D_MODEL)

    def placed(shard):
        buf = lax.empty((shard.shape[0], N_DEV) + shard.shape[1:], shard.dtype)
        return lax.dynamic_update_slice(buf, shard[:, None], (0, dev, 0, 0))

    ffn_sh = jnp.stack([w_gate.transpose(0, 2, 1), w_up.transpose(0, 2, 1), w_down], axis=1).astype(BF16)
    groups = {f"ffn{l}": [placed(ffn_sh[l])] for l in range(4)}
    groups["att"] = [placed(a.astype(BF16)) for a in (
        w_dkv[None, :, :KV_RANK], _pad_cols(w_dkv[None, :, KV_RANK:], LANES), w_uk[None], w_uv[None],
        w_dq, _pad_cols(w_uq, QK_PAD), w_o)]
    spread = {}
    for nm in ("ffn0", "ffn1", "att", "ffn2", "ffn3"):
        spread[nm] = _copies_start(groups[nm], len(groups[nm]), _gather_spread, f"spread_{nm}", deps=[small_g])
    first_tokens = [st[3] for st in spread.values()]

    def relay(nm, after):
        ssem, rsem, bufs, _ = spread[nm]
        bufs = _copies_wait(bufs, ssem, rsem, after, _blocks_moved(4), f"spread_done_{nm}")
        return _copies_start(bufs, len(bufs), _gather_relay, f"relay_{nm}")

    def gathered(nm, state, after):
        ssem, rsem, bufs, _ = state
        return _copies_wait(bufs, ssem, rsem, after, _blocks_moved(3), f"relay_done_{nm}")

    gkn = g_k[:NOPE].reshape(1, NOPE)
    gkr = _pad_cols(g_k[NOPE:].reshape(1, ROPE), LANES)
    gl = g_kv_latent.reshape(1, KV_RANK)
    lnkv = ln_kv.reshape(1, D_MODEL)

    x_in, x_mid, pooled, gates, ups, w_ffn = [], [], [], [], [], []
    qs, outs, lses = [], [], []
    cur = xs
    rel = None
    for l in range(4):
        x_in.append(cur)
        if l < 2:
            deps = first_tokens if l == 0 else [rel[3]]
            mid, dsave = _mix_fwd(cur, ln_a_all[l], wp_all[l], bp_all[l], sc_all[l], f"mix_fwd{l}", deps=deps)
            pooled.append(dsave)
            if l == 0:
                rel = relay("ffn0", mid)
        else:
            j = l - 2
            qargs = (ln_mix_b[j].reshape(1, -1), wdq_all[j], g_q_latent[j].reshape(1, -1), wuq_all[j],
                     g_q[j, :NOPE].reshape(1, -1), _pad_cols(g_q[j, NOPE:].reshape(1, -1), LANES), cos, sin)
            q = _q_fwd(cur, *qargs, f"q_fwd{j}", deps=[rel[3]])
            o, lse = _att_fwd(q, k_sh, v_sh, f"att_fwd{j}")
            mid = _o_fwd(cur, o, wo_all[j], f"o_fwd{j}")
            qs.append(q)
            outs.append(o)
            lses.append(lse)
        x_mid.append(mid)
        (w_l,) = gathered(f"ffn{l}", rel, mid)
        w_l = w_l.reshape(3, D_FF, D_MODEL)
        w_ffn.append(w_l)
        cur, gate, up = _ffn_fwd(mid, ln_ffn[l].reshape(1, -1), w_l, f"ffn_fwd{l}")
        gates.append(gate)
        ups.append(up)
        if l == 1:
            x_kv = cur
            att_bufs = gathered("att", relay("att", cur), cur)
            wc = att_bufs[0].reshape(D_MODEL, KV_RANK)
            wpe = att_bufs[1].reshape(D_MODEL, LANES)
            wuk_g = att_bufs[2].reshape(N_HEADS, KV_RANK, NOPE)
            wuv_g = att_bufs[3].reshape(N_HEADS, KV_RANK, V_DIM)
            wdq_all = att_bufs[4].reshape(2, D_MODEL, Q_RANK)
            wuq_all = att_bufs[5]
            wo_all = att_bufs[6].reshape(2, D_MODEL, D_MODEL)
            rel = relay("ffn2", cur)
            k_sh, v_sh = _kv_fwd(cur, lnkv, wc, wpe, gl, wuk_g, wuv_g, gkn, gkr, cos, sin, "kv_fwd", deps=[rel[3]])
        elif l < 3:
            rel = relay(f"ffn{l + 1}", cur)

    dx, sq_cols = _loss_head(cur, target, "loss_head")

    small = {}
    sizes = dict(ffn0=FF_SHARD, ffn1=FF_SHARD, ffn2=FF_SHARD, ffn3=FF_SHARD, wo=128, kv512=128, dkv_pe=128,
                 wdq=128, wuqT=QK_PAD, wpool=32)
    big = dict(wo=lax.empty((2, D_MODEL, D_MODEL), BF16), kv512=lax.empty((3, D_MODEL, KV_RANK), BF16),
               dkv_pe=lax.empty((1, D_MODEL, LANES), BF16), wdq=lax.empty((2, D_MODEL, Q_RANK), BF16),
               wuqT=lax.empty((2, N_HEADS * QK_PAD, Q_RANK), BF16), wpool=lax.empty((8, GROUP_DIM, GROUP_DIM), BF16))
    for l in range(4):
        big[f"ffn{l}"] = lax.empty((3, D_FF, D_MODEL), BF16)
    red = {}

    def pair_start(nms, tag):
        arrs = []
        for nm in nms:
            view = _owner_view(big[nm], sizes[nm])
            arrs += [view, lax.empty((view.shape[0], N_CHIPS) + view.shape[3:], BF16)]
        return nms, tag, _copies_start(arrs, len(nms), _pair_send, f"pair_start_{tag}")

    def chip_start(state, after):
        nms, tag, (ssem, rsem, arrs, _) = state
        arrs = _copies_wait(arrs, ssem, rsem, after, _landed, f"pair_done_{tag}")
        out = []
        for t, nm in enumerate(nms):
            part = _pair_sum(arrs[2 * t], arrs[2 * t + 1], core, f"pair_sum_{nm}")
            out += [part, lax.empty((3, part.shape[0]) + part.shape[2:], BF16)]
        return nms, tag, _copies_start(out, len(nms), _chip_send, f"chip_start_{tag}")

    def chip_finish(state, after):
        nms, tag, (ssem, rsem, arrs, _) = state
        arrs = _copies_wait(arrs, ssem, rsem, after, _landed, f"chip_done_{tag}")
        for t, nm in enumerate(nms):
            red[nm] = _chip_sum(arrs[2 * t], arrs[2 * t + 1], chip, f"chip_sum_{nm}")

    dks, dvs = [], []
    pending = None
    bwd_deps = []
    for l in (3, 2, 1, 0):
        key = f"ffn{l}"
        dx, act, dgb, dub, hn, dyb, dln = _ffn_bwd(x_mid[l], dx, gates[l], ups[l], ln_ffn[l].reshape(1, -1),
                                                     w_ffn[l], f"ffn_bwd{l}", deps=bwd_deps)
        bwd_deps = []
        small[f"ln_ffn{l}"] = dln
        if l == 1:
            att_chip = chip_start(att_pair, dx)
            tn_deps = [att_chip[2][3]]
        else:
            tn_deps = []
        if pending:
            chip_finish(pending, dx)
            pending = None
        big[key] = _tn_matmul(dgb, hn, big[key], 0, f"dw_gate{l}", m_chunk=FF_HALF, deps=tn_deps)
        big[key] = _tn_matmul(dub, hn, big[key], 1, f"dw_up{l}", m_chunk=FF_HALF)
        big[key] = _tn_matmul(act, dyb, big[key], 2, f"dw_down{l}", m_chunk=FF_HALF)
        if l == 1:
            chip_finish(att_chip, big[key])
        ffn_pair = pair_start([key], key)
        if l >= 2:
            j = l - 2
            do, dxb = _o_bwd(dx, wo_all[j], f"o_bwd{j}", deps=[ffn_pair[2][3]])
            big["wo"] = _tn_matmul(outs[j], dxb, big["wo"], j, f"dw_o{j}")
            ffn_chip = chip_start(ffn_pair, big["wo"])
            dq, dk, dv = _att_bwd(qs[j], k_sh, v_sh, do, outs[j], lses[j], f"att_bwd{j}", deps=[ffn_chip[2][3]])
            chip_finish(ffn_chip, dq)
            dks.append(dk)
            dvs.append(dv)
            qargs = (ln_mix_b[j].reshape(1, -1), wdq_all[j], g_q_latent[j].reshape(1, -1), wuq_all[j],
                     g_q[j, :NOPE].reshape(1, -1), _pad_cols(g_q[j, NOPE:].reshape(1, -1), LANES), cos, sin)
            dx, hnq, cqn, dqa, dcq, dln, dgql, dgqn, dgqr = _q_bwd(x_in[l], dx, dq, *qargs, f"q_bwd{j}")
            small[f"ln_mix_b{j}"] = dln
            small[f"g_q_latent{j}"] = dgql
            small[f"g_q{j}"] = jnp.concatenate([dgqn, dgqr[:, :ROPE]], axis=1)
            big["wdq"] = _tn_matmul(hnq, dcq, big["wdq"], j, f"dw_dq{j}")
            big["wuqT"] = _tn_matmul(dqa, cqn, big["wuqT"], j, f"dw_uq{j}")
            if l == 2:
                (dx, hnk, cn, dknb, dvb, dccb, dpeb, dlnkv, dgl, dgkn, dgkr) = _kv_bwd(
                    x_kv, dx, dks, dvs, lnkv, wc, wpe, gl, wuk_g, wuv_g, gkn, gkr, cos, sin, "kv_bwd")
                small["ln_kv"] = dlnkv
                small["g_kv_latent"] = dgl
                small["g_k"] = jnp.concatenate([dgkn, dgkr[:, :ROPE]], axis=1)
                big["kv512"] = _tn_matmul(dknb, cn, big["kv512"], 0, "dw_uk")
                big["kv512"] = _tn_matmul(dvb, cn, big["kv512"], 1, "dw_uv")
                big["kv512"] = _tn_matmul(hnk, dccb, big["kv512"], 2, "dw_dkv_c")
                big["dkv_pe"] = _tn_matmul(hnk, dpeb, big["dkv_pe"], 0, "dw_dkv_pe")
                att_pair = pair_start(["wo", "kv512", "dkv_pe", "wdq", "wuqT"], "att")
                bwd_deps = [att_pair[2][3]]
        else:
            dx, dyp, dsc, db, dln = _mix_bwd(x_in[l], dx, pooled[l], ln_a_all[l], wp_all[l], bp_all[l], sc_all[l],
                                             f"mix_bwd{l}", deps=[ffn_pair[2][3]])
            small[f"ln_mix_a{l}"] = dln
            small[f"pool_scale{l}"] = dsc
            small[f"b_pool{l}"] = db
            ffn_chip = chip_start(ffn_pair, dx)
            big["wpool"] = _tn_matmul(pooled[l], dyp, big["wpool"], 4 * l, f"dw_pool{l}", groups=4,
                                      deps=[ffn_chip[2][3]])
            if l == 1:
                pending = ffn_chip
            else:
                chip_finish(ffn_chip, big["wpool"])
    grad_x = dx[None]
    pool_pair = pair_start(["wpool"], "wpool")
    pool_chip = chip_start(pool_pair, pool_pair[2][3])
    chip_finish(pool_chip, pool_chip[2][3])

    vec_names = (["loss"] + [f"ln_ffn{l}" for l in range(4)] + ["ln_kv", "g_kv_latent", "g_k"]
                 + [f"{p}{j}" for p in ("ln_mix_b", "g_q_latent", "g_q") for j in range(2)]
                 + [f"{p}{l}" for p in ("ln_mix_a", "pool_scale", "b_pool") for l in range(2)])
    small["loss"] = sq_cols
    widths = [small[nm].shape[1] for nm in vec_names]
    padded = [-(-w // LANES) * LANES for w in widths]
    packed = jnp.concatenate([_pad_cols(small[nm], pw) for nm, pw in zip(vec_names, padded)], axis=1)
    (all_vecs,) = _all_gather([packed], [0], "gather_vectors")
    total = _sum_lead(all_vecs, "sum_vectors")
    vec = {}
    off = 0
    for nm, w, pw in zip(vec_names, widths, padded):
        vec[nm] = total[0, off:off + w]
        off += pw
    loss = 0.5 * jnp.sum(vec["loss"]) * (1.0 / D_MODEL)

    def own_cols(full, width):
        return lax.dynamic_slice_in_dim(full, dev * width, width, axis=full.ndim - 1)

    grads = dict(
        ln_mix_a=own_cols(jnp.stack([vec["ln_mix_a0"], vec["ln_mix_a1"]]), LANES),
        w_pool=red["wpool"].reshape(2, 4, 32, GROUP_DIM),
        b_pool=own_cols(jnp.stack([vec["b_pool0"], vec["b_pool1"]]).reshape(2, 4, GROUP_DIM), 32),
        pool_scale=own_cols(jnp.stack([vec["pool_scale0"], vec["pool_scale1"]]), LANES),
        ln_ffn=jnp.stack([vec[f"ln_ffn{l}"] for l in range(4)]),
        w_gate=jnp.stack([red[f"ffn{l}"][0] for l in range(4)]).transpose(0, 2, 1),
        w_up=jnp.stack([red[f"ffn{l}"][1] for l in range(4)]).transpose(0, 2, 1),
        w_down=jnp.stack([red[f"ffn{l}"][2] for l in range(4)]),
        ln_kv=vec["ln_kv"],
        w_dkv=jnp.concatenate([red["kv512"][2], red["dkv_pe"][0][:, :ROPE]], axis=1),
        g_kv_latent=vec["g_kv_latent"],
        w_uk=red["kv512"][0].T,
        w_uv=red["kv512"][1].T,
        g_k=vec["g_k"],
        ln_mix_b=jnp.stack([vec["ln_mix_b0"], vec["ln_mix_b1"]]),
        w_dq=red["wdq"],
        g_q_latent=jnp.stack([vec["g_q_latent0"], vec["g_q_latent1"]]),
        w_uq=red["wuqT"].transpose(0, 2, 1)[:, :, :QK_DIM],
        g_q=jnp.stack([vec["g_q0"], vec["g_q1"]]),
        w_o=red["wo"],
    )

    deltas, new_m, new_v = {}, {}, {}
    for nm in names:
        w = weights[nm]
        shape = w.shape if w.ndim > 1 else (1, w.shape[0])
        d, mo, vo = _adamw(w.reshape(shape), grads[nm].reshape(shape), mom1[nm].reshape(shape),
                           mom2[nm].reshape(shape), f"adamw_{nm}")
        deltas[nm], new_m[nm], new_v[nm] = d.reshape(w.shape), mo.reshape(w.shape), vo.reshape(w.shape)

    return (loss, grad_x, *[grads[nm].reshape(weights[nm].shape) for nm in names], *[deltas[nm] for nm in names],
            *[new_m[nm] for nm in names], *[new_v[nm] for nm in names])
```

```python
import functools
import math

import jax
import jax.numpy as jnp
from jax import lax
from jax.experimental import pallas as pl
from jax.experimental.pallas import tpu as pltpu

F32 = jnp.float32
BF16 = jnp.bfloat16
MESH = pl.DeviceIdType.MESH

D_MODEL = 1024
D_FF = 2816
N_DEV = 8
N_CHIPS = 4
FF_SHARD = D_FF // N_DEV
FF_HALF = D_FF // 2
N_HEADS = 8
NOPE = 128
ROPE = 64
QK_DIM = NOPE + ROPE
QK_PAD = 256
V_DIM = 128
Q_RANK = 256
KV_RANK = 512
POOL_WINDOWS = (2, 4, 8, 16)
GROUP_DIM = 256
HALO = 128
CHUNK = 64
ROPE_THETA = 10000.0
EPS = 1e-6
LANES = 128

ADAM_LR = 0.001
ADAM_B1 = 0.9
ADAM_B2 = 0.999
ADAM_EPS = 1e-08
ADAM_WD = 0.01
ADAM_STEP = 10

VMEM_BIG = 56 * 2**20
VMEM_MID = 40 * 2**20


def _nn(a, b):
    return lax.dot_general(a, b, (((1,), (0,)), ((), ())), preferred_element_type=F32)


def _nt(a, b):
    return lax.dot_general(a, b, (((1,), (1,)), ((), ())), preferred_element_type=F32)


def _tn(a, b):
    return lax.dot_general(a, b, (((0,), (0,)), ((), ())), preferred_element_type=F32)


def _rms(x, g, n):
    r = lax.rsqrt(jnp.sum(x * x, axis=-1, keepdims=True) * (1.0 / n) + EPS)
    return (x * r) * g, r


def _rms_bwd(x, r, g, dy, n):
    u = dy * g
    s = jnp.sum(x * u, axis=-1, keepdims=True) * (1.0 / n)
    dx = r * u - x * (r * r * r * s)
    dg = jnp.sum(dy * (x * r), axis=0, keepdims=True)
    return dx, dg


def _swap_halves(z):
    lane = lax.broadcasted_iota(jnp.int32, z.shape, 1)
    return jnp.where(lane < ROPE // 2, pltpu.roll(z, LANES - ROPE // 2, 1), pltpu.roll(z, ROPE // 2, 1))


def _sigmoid(x):
    return 1.0 / (1.0 + jnp.exp(-x))


def _cparams(n_grid, vmem=None):
    return pltpu.CompilerParams(dimension_semantics=("arbitrary",) * n_grid, vmem_limit_bytes=vmem)


def _rows(t, cols):
    return pl.BlockSpec((t, cols), lambda i: (i, 0))


def _full(shape):
    nd = len(shape)
    return pl.BlockSpec(shape, lambda *_: (0,) * nd)


ANY = pl.BlockSpec(memory_space=pl.ANY)


def _pcall(body, args, deps, *, in_specs, **kw):
    n_in, n_dep = len(args), len(deps)

    def ordered(*refs):
        body(*refs[:n_in], *refs[n_in + n_dep:])

    return pl.pallas_call(ordered, in_specs=list(in_specs) + [ANY] * n_dep, **kw)(*args, *deps)


def _place():
    x, y, c = lax.axis_index("x"), lax.axis_index("y"), lax.axis_index("c")
    return x, y, c


def _all_gather(shards, axes, name):
    n = len(shards)
    out_shape = [jax.ShapeDtypeStruct(s.shape[:a] + (N_DEV,) + s.shape[a:], s.dtype) for s, a in zip(shards, axes)]

    def body(*refs):
        ins, outs = refs[:n], refs[n:2 * n]
        send_sems, recv_sems, local_sems = refs[2 * n:]
        x, y, c = _place()
        me, sibling = (x, y, c), (x, y, 1 - c)
        chips = [(1 - x, y), (x, 1 - y), (1 - x, 1 - y)]

        def slot(t, dev):
            idx = 4 * dev[0] + 2 * dev[1] + dev[2]
            return outs[t].at[(slice(None),) * axes[t] + (idx,)]

        def copy(t, k, block, to, src=None):
            return pltpu.make_async_remote_copy(
                src_ref=slot(t, block) if src is None else src, dst_ref=slot(t, block),
                send_sem=send_sems.at[t, k], recv_sem=recv_sems.at[t, k],
                device_id=to, device_id_type=MESH)

        mine = [pltpu.make_async_copy(ins[t], slot(t, me), local_sems.at[t]) for t in range(n)]
        for cp in mine:
            cp.start()
        first = []
        for t in range(n):
            first.append(copy(t, 0, me, sibling, src=ins[t]))
            first += [copy(t, 1 + j, me, (*chip, c), src=ins[t]) for j, chip in enumerate(chips)]
        for cp in first:
            cp.start()
        passed = []
        for j, chip in enumerate(chips):
            for t in range(n):
                copy(t, 1 + j, (*chip, c), me).wait_recv()
                cp = copy(t, 4 + j, (*chip, c), sibling)
                cp.start()
                passed.append(cp)
        for t in range(n):
            copy(t, 0, sibling, me).wait_recv()
            for j, chip in enumerate(chips):
                copy(t, 4 + j, (*chip, 1 - c), me).wait_recv()
        for cp in first + passed:
            cp.wait_send()
        for cp in mine:
            cp.wait()

    return pl.pallas_call(
        body, name=name, out_shape=out_shape,
        in_specs=[ANY] * n, out_specs=[ANY] * n,
        scratch_shapes=[pltpu.SemaphoreType.DMA((n, 7)), pltpu.SemaphoreType.DMA((n, 7)),
                        pltpu.SemaphoreType.DMA((n,))],
    )(*shards)


HBM = pl.BlockSpec(memory_space=pltpu.HBM)
SEM = pl.BlockSpec(memory_space=pltpu.SEMAPHORE)
EFFECT = pltpu.SideEffectType.DATAFLOW_SIDE_EFFECTING


def _copies_start(arrays, n_sems, plan, name, deps=()):
    n, nd = len(arrays), len(deps)

    def body(*refs):
        for cp in plan(refs[:n], refs[n + nd], refs[n + nd + 1]):
            cp.start()
        refs[-1][...] = jnp.zeros_like(refs[-1])

    outs = pl.pallas_call(
        body, name=name,
        out_shape=(pltpu.SemaphoreType.DMA((n_sems,)), pltpu.SemaphoreType.DMA((n_sems,)),
                   *[pltpu.HBM(a.shape, a.dtype) for a in arrays], jax.ShapeDtypeStruct((8, LANES), F32)),
        in_specs=[HBM] * n + [ANY] * nd,
        out_specs=(SEM, SEM, *[HBM] * n, pl.BlockSpec(memory_space=pltpu.VMEM)),
        input_output_aliases={i: 2 + i for i in range(n)},
        compiler_params=pltpu.CompilerParams(has_side_effects=EFFECT),
    )(*[pltpu.with_memory_space_constraint(a, pltpu.HBM) for a in arrays], *deps)
    return outs[0], outs[1], list(outs[2:2 + n]), outs[-1]


def _copies_wait(arrays, send_sems, recv_sems, after, plan, name):
    n = len(arrays)

    def body(*refs):
        for cp in plan(refs[:n], refs[n], refs[n + 1]):
            cp.wait_send()
            cp.wait_recv()

    outs = pl.pallas_call(
        body, name=name,
        out_shape=tuple(pltpu.HBM(a.shape, a.dtype) for a in arrays),
        in_specs=[HBM] * n + [SEM, SEM, ANY], out_specs=tuple([HBM] * n),
        input_output_aliases={i: i for i in range(n)},
        compiler_params=pltpu.CompilerParams(has_side_effects=EFFECT),
    )(*arrays, send_sems, recv_sems, after)
    return list(outs)


def _remote(src, dst, send_sems, recv_sems, t, to):
    return pltpu.make_async_remote_copy(src_ref=src, dst_ref=dst, send_sem=send_sems.at[t], recv_sem=recv_sems.at[t],
                                        device_id=to, device_id_type=MESH)


def _dev_index(x, y, c):
    return 4 * x + 2 * y + c


def _gather_spread(bufs, send_sems, recv_sems):
    x, y, c = _place()
    mine = _dev_index(x, y, c)
    peers = [(x, y, 1 - c), (1 - x, y, c), (x, 1 - y, c), (1 - x, 1 - y, c)]
    return [_remote(g.at[k, mine], g.at[k, mine], send_sems, recv_sems, t, peer)
            for t, g in enumerate(bufs) for peer in peers for k in range(g.shape[0])]


def _gather_relay(bufs, send_sems, recv_sems):
    x, y, c = _place()
    blocks = [_dev_index(1 - x, y, c), _dev_index(x, 1 - y, c), _dev_index(1 - x, 1 - y, c)]
    return [_remote(g.at[k, b], g.at[k, b], send_sems, recv_sems, t, (x, y, 1 - c))
            for t, g in enumerate(bufs) for b in blocks for k in range(g.shape[0])]


def _blocks_moved(count):
    def plan(bufs, send_sems, recv_sems):
        x, y, c = _place()
        return [_remote(g.at[:, pl.ds(0, count)], g.at[:, pl.ds(0, count)], send_sems, recv_sems, t, (x, y, 1 - c))
                for t, g in enumerate(bufs)]
    return plan


def _pair_send(arrs, send_sems, recv_sems):
    x, y, c = _place()
    return [_remote(arrs[2 * t].at[p, k, 1 - c], arrs[2 * t + 1].at[p, k], send_sems, recv_sems, t, (x, y, 1 - c))
            for t in range(len(arrs) // 2) for p in range(arrs[2 * t].shape[0]) for k in range(N_CHIPS)]


def _chip_send(arrs, send_sems, recv_sems):
    x, y, c = _place()
    chips = [(1 - x, y), (x, 1 - y), (1 - x, 1 - y)]
    return [_remote(arrs[2 * t].at[p, 2 * px + py], arrs[2 * t + 1].at[j, p], send_sems, recv_sems, t, (px, py, c))
            for t in range(len(arrs) // 2) for j, (px, py) in enumerate(chips) for p in range(arrs[2 * t].shape[0])]


def _landed(arrs, send_sems, recv_sems):
    x, y, c = _place()
    return [_remote(arrs[2 * t + 1], arrs[2 * t + 1], send_sems, recv_sems, t, (x, y, 1 - c))
            for t in range(len(arrs) // 2)]


def _pair_sum(grad, landed, core, name):
    p, _, _, sz, c = grad.shape

    def body(core_ref, g_ref, l_ref, o_ref):
        o_ref[...] = (g_ref[...].astype(F32) + l_ref[...].astype(F32)).astype(o_ref.dtype)

    out = pl.pallas_call(
        body, name=name,
        grid_spec=pltpu.PrefetchScalarGridSpec(
            num_scalar_prefetch=1, grid=(p * N_CHIPS,),
            in_specs=[pl.BlockSpec((None, None, sz, c), lambda i, cr: (i, cr[0], 0, 0)),
                      pl.BlockSpec((None, sz, c), lambda i, cr: (i, 0, 0))],
            out_specs=pl.BlockSpec((None, sz, c), lambda i, cr: (i, 0, 0))),
        out_shape=jax.ShapeDtypeStruct((p * N_CHIPS, sz, c), grad.dtype),
        compiler_params=_cparams(1),
    )(core, grad.reshape(p * N_CHIPS, 2, sz, c), landed.reshape(p * N_CHIPS, sz, c))
    return out.reshape(p, N_CHIPS, sz, c)


def _chip_sum(parts, landed, chip, name):
    p, _, sz, c = parts.shape

    def body(chip_ref, a_ref, l_ref, o_ref):
        acc = a_ref[...].astype(F32)
        for j in range(3):
            acc = acc + l_ref[j].astype(F32)
        o_ref[...] = acc

    return pl.pallas_call(
        body, name=name,
        grid_spec=pltpu.PrefetchScalarGridSpec(
            num_scalar_prefetch=1, grid=(p,),
            in_specs=[pl.BlockSpec((None, None, sz, c), lambda i, cr: (i, cr[0], 0, 0)),
                      pl.BlockSpec((3, None, sz, c), lambda i, cr: (0, i, 0, 0))],
            out_specs=pl.BlockSpec((None, sz, c), lambda i, cr: (i, 0, 0))),
        out_shape=jax.ShapeDtypeStruct((p, sz, c), F32),
        compiler_params=_cparams(1),
    )(chip, parts, landed)


def _sum_lead(a, name, out_dtype=F32):
    k = a.shape[0]
    rest = a.shape[1:]
    r, c = rest[-2], rest[-1]
    lead = math.prod(rest[:-2])
    a3 = a.reshape(k, lead * r, c)
    rows = lead * r
    tb = rows
    for cand in (512, 256, 128, 64, 32, 16, 8):
        if rows % cand == 0 and rows > cand:
            tb = cand
            break

    def body(a_ref, o_ref):
        acc = a_ref[0].astype(F32)
        for i in range(1, k):
            acc = acc + a_ref[i].astype(F32)
        o_ref[...] = acc.astype(out_dtype)

    out = pl.pallas_call(
        body, name=name, grid=(rows // tb,),
        out_shape=jax.ShapeDtypeStruct((rows, c), out_dtype),
        in_specs=[pl.BlockSpec((k, tb, c), lambda i: (0, i, 0))],
        out_specs=pl.BlockSpec((tb, c), lambda i: (i, 0)),
        compiler_params=_cparams(1),
    )(a3)
    return out.reshape(rest)


def _band(t, w, offset, valid):
    r = lax.broadcasted_iota(jnp.int32, (t, t + HALO), 0)
    col = lax.broadcasted_iota(jnp.int32, (t, t + HALO), 1)
    diff = offset(r, col)
    return jnp.where((diff >= 0) & (diff < w) & valid(col), 1.0, 0.0).astype(BF16)


def _split_dot(band, v):
    hi = v.astype(BF16)
    lo = (v - hi.astype(F32)).astype(BF16)
    return _nn(band, hi) + _nn(band, lo)


def _mix_fwd(x, g, wp, b, sc, name, deps=()):
    s = x.shape[0]
    t = min(256, s)
    rb = t // HALO

    def body(x_ref, xh_ref, g_ref, wp_ref, b_ref, sc_ref, xo_ref, d_ref):
        i = pl.program_id(0)
        gg = g_ref[...]
        h, _ = _rms(x_ref[...], gg, D_MODEL)
        hh, _ = _rms(xh_ref[...], gg, D_MODEL)
        hext = jnp.concatenate([hh, h], axis=0)
        tok = i * t + lax.broadcasted_iota(jnp.int32, (t, 1), 0)
        for gi, w in enumerate(POOL_WINDOWS):
            sl = slice(gi * GROUP_DIM, (gi + 1) * GROUP_DIM)
            band = _band(t, w, lambda r, col: r + HALO - col, lambda col: (col >= HALO) | (i > 0))
            win = _split_dot(band, hext[:, sl])
            cnt = jnp.minimum(tok + 1, w).astype(F32)
            dbf = (win / cnt - h[:, sl]).astype(BF16)
            d_ref[:, sl] = dbf
            ypre = _nn(dbf, wp_ref[gi]) + b_ref[:, sl]
            xo_ref[:, sl] = x_ref[:, sl] + ypre * sc_ref[:, sl]

    return _pcall(
        body, (x, x, g, wp, b, sc), deps, name=name, grid=(s // t,),
        out_shape=[jax.ShapeDtypeStruct((s, D_MODEL), F32), jax.ShapeDtypeStruct((s, D_MODEL), BF16)],
        in_specs=[_rows(t, D_MODEL),
                  pl.BlockSpec((HALO, D_MODEL), lambda i: (jnp.maximum(i * rb - 1, 0), 0)),
                  _full((1, D_MODEL)), _full((4, GROUP_DIM, GROUP_DIM)), _full((1, D_MODEL)), _full((1, D_MODEL))],
        out_specs=[_rows(t, D_MODEL), _rows(t, D_MODEL)],
        compiler_params=_cparams(1, VMEM_MID),
    )


def _mix_bwd(x, dy, d, g, wp, b, sc, name, deps=()):
    s = x.shape[0]
    t = min(256, s)
    rb = t // HALO
    nb = s // t
    last_halo = s // HALO - 1

    def body(x_ref, dy_ref, dyn_ref, d_ref, g_ref, wp_ref, b_ref, sc_ref,
             dx_ref, dyp_ref, dsc_ref, db_ref, dln_ref):
        i = pl.program_id(0)
        x = x_ref[...]
        gg = g_ref[...]
        dy = dy_ref[...]
        sc = sc_ref[...]
        dyp32 = dy * sc
        dyp = dyp32.astype(BF16)
        dyph = (dyn_ref[...] * sc).astype(BF16)
        dyp_ref[...] = dyp
        tok = i * t + lax.broadcasted_iota(jnp.int32, (t + HALO, 1), 0)
        dh, dsc = [], []
        for gi, w in enumerate(POOL_WINDOWS):
            sl = slice(gi * GROUP_DIM, (gi + 1) * GROUP_DIM)
            ypre = _nn(d_ref[:, sl], wp_ref[gi]) + b_ref[:, sl]
            dsc.append(jnp.sum(dy[:, sl] * ypre, axis=0, keepdims=True))
            dd = _nt(dyp[:, sl], wp_ref[gi])
            ddh = _nt(dyph[:, sl], wp_ref[gi])
            cnt = jnp.minimum(tok + 1, w).astype(F32)
            ddext = jnp.concatenate([dd, ddh], axis=0) / cnt
            band = _band(t, w, lambda r, col: col - r, lambda col: (col < t) | (i < nb - 1))
            dh.append(_split_dot(band, ddext) - dd)
        dh = jnp.concatenate(dh, axis=1)
        _, r = _rms(x, gg, D_MODEL)
        dxn, dg = _rms_bwd(x, r, gg, dh, D_MODEL)
        dx_ref[...] = dy + dxn

        @pl.when(i == 0)
        def _():
            dsc_ref[...] = jnp.zeros_like(dsc_ref)
            db_ref[...] = jnp.zeros_like(db_ref)
            dln_ref[...] = jnp.zeros_like(dln_ref)

        dsc_ref[...] += jnp.concatenate(dsc, axis=1)
        db_ref[...] += jnp.sum(dyp32, axis=0, keepdims=True)
        dln_ref[...] += dg

    vec = jax.ShapeDtypeStruct((1, D_MODEL), F32)
    return _pcall(
        body, (x, dy, dy, d, g, wp, b, sc), deps, name=name, grid=(nb,),
        out_shape=[jax.ShapeDtypeStruct((s, D_MODEL), F32), jax.ShapeDtypeStruct((s, D_MODEL), BF16), vec, vec, vec],
        in_specs=[_rows(t, D_MODEL), _rows(t, D_MODEL),
                  pl.BlockSpec((HALO, D_MODEL), lambda i: (jnp.minimum((i + 1) * rb, last_halo), 0)),
                  _rows(t, D_MODEL),
                  _full((1, D_MODEL)), _full((4, GROUP_DIM, GROUP_DIM)), _full((1, D_MODEL)), _full((1, D_MODEL))],
        out_specs=[_rows(t, D_MODEL), _rows(t, D_MODEL), _full((1, D_MODEL)), _full((1, D_MODEL)), _full((1, D_MODEL))],
        compiler_params=_cparams(1, VMEM_MID),
    )


def _load_weights(w_hbm, w_vmem, sem):
    @pl.when(pl.program_id(0) == 0)
    def _():
        cp = pltpu.make_async_copy(w_hbm, w_vmem, sem)
        cp.start()
        cp.wait()


def _ffn_fwd(x, g, w, name):
    s = x.shape[0]
    t = min(256, s)

    def body(x_ref, g_ref, w_hbm, xo_ref, gate_ref, up_ref, w_ref, sem):
        _load_weights(w_hbm, w_ref, sem)
        x = x_ref[...]
        hn = _rms(x, g_ref[...], D_MODEL)[0].astype(BF16)
        acc = x
        for c in range(2):
            rs = slice(c * FF_HALF, (c + 1) * FF_HALF)
            gt = _nt(hn, w_ref[0, rs, :])
            up = _nt(hn, w_ref[1, rs, :])
            gate_ref[:, rs] = gt.astype(BF16)
            up_ref[:, rs] = up.astype(BF16)
            act = ((gt * _sigmoid(gt)) * up).astype(BF16)
            acc = acc + _nn(act, w_ref[2, rs, :])
        xo_ref[...] = acc

    hid = jax.ShapeDtypeStruct((s, D_FF), BF16)
    return pl.pallas_call(
        body, name=name, grid=(s // t,),
        out_shape=[jax.ShapeDtypeStruct((s, D_MODEL), F32), hid, hid],
        in_specs=[_rows(t, D_MODEL), _full((1, D_MODEL)), ANY],
        out_specs=[_rows(t, D_MODEL), _rows(t, D_FF), _rows(t, D_FF)],
        scratch_shapes=[pltpu.VMEM((3, D_FF, D_MODEL), BF16), pltpu.SemaphoreType.DMA],
        compiler_params=_cparams(1, VMEM_BIG),
    )(x, g, w)


def _ffn_bwd(x, dy, gate, up, g, w, name, deps=()):
    s = x.shape[0]
    t = min(256, s)

    def body(x_ref, dy_ref, gate_ref, up_ref, g_ref, w_hbm,
             dx_ref, act_ref, dg_ref, du_ref, hn_ref, dyb_ref, dln_ref, w_ref, sem):
        _load_weights(w_hbm, w_ref, sem)
        x = x_ref[...]
        gg = g_ref[...]
        y, r = _rms(x, gg, D_MODEL)
        hn = y.astype(BF16)
        hn_ref[...] = hn
        dy = dy_ref[...]
        dyb = dy.astype(BF16)
        dyb_ref[...] = dyb
        dh = jnp.zeros((t, D_MODEL), F32)
        for c in range(2):
            rs = slice(c * FF_HALF, (c + 1) * FF_HALF)
            gt = gate_ref[:, rs].astype(F32)
            u = up_ref[:, rs].astype(F32)
            sg = _sigmoid(gt)
            sl = gt * sg
            act_ref[:, rs] = (sl * u).astype(BF16)
            dact = _nt(dyb, w_ref[2, rs, :])
            dg = (dact * u * (sg * (1.0 + gt * (1.0 - sg)))).astype(BF16)
            du = (dact * sl).astype(BF16)
            dg_ref[:, rs] = dg
            du_ref[:, rs] = du
            dh = dh + _nn(dg, w_ref[0, rs, :]) + _nn(du, w_ref[1, rs, :])
        dxn, dgl = _rms_bwd(x, r, gg, dh, D_MODEL)
        dx_ref[...] = dy + dxn

        @pl.when(pl.program_id(0) == 0)
        def _():
            dln_ref[...] = jnp.zeros_like(dln_ref)

        dln_ref[...] += dgl

    hid = jax.ShapeDtypeStruct((s, D_FF), BF16)
    tok = jax.ShapeDtypeStruct((s, D_MODEL), BF16)
    return _pcall(
        body, (x, dy, gate, up, g, w), deps, name=name, grid=(s // t,),
        out_shape=[jax.ShapeDtypeStruct((s, D_MODEL), F32), hid, hid, hid, tok, tok,
                   jax.ShapeDtypeStruct((1, D_MODEL), F32)],
        in_specs=[_rows(t, D_MODEL), _rows(t, D_MODEL), _rows(t, D_FF), _rows(t, D_FF), _full((1, D_MODEL)), ANY],
        out_specs=[_rows(t, D_MODEL), _rows(t, D_FF), _rows(t, D_FF), _rows(t, D_FF),
                   _rows(t, D_MODEL), _rows(t, D_MODEL), _full((1, D_MODEL))],
        scratch_shapes=[pltpu.VMEM((3, D_FF, D_MODEL), BF16), pltpu.SemaphoreType.DMA],
        compiler_params=_cparams(1, VMEM_BIG),
    )


def _tn_matmul(a, b, into, p0, name, groups=1, m_chunk=None, deps=()):
    s = a.shape[0]
    m, n = a.shape[1] // groups, b.shape[1] // groups
    assert into.shape[1:] == (m, n)
    mc = m if m_chunk is None else m_chunk
    nm = m // mc
    t = min(512, s)
    nt = s // t

    def body(a_ref, b_ref, into_ref, o_ref, acc):
        ti = pl.program_id(2)

        @pl.when(ti == 0)
        def _():
            acc[...] = jnp.zeros_like(acc)

        acc[...] += _tn(a_ref[...], b_ref[...])

        @pl.when(ti == nt - 1)
        def _():
            o_ref[...] = acc[...].astype(o_ref.dtype)

    return _pcall(
        body, (a, b, into), deps, name=name, grid=(groups, nm, nt),
        out_shape=jax.ShapeDtypeStruct(into.shape, into.dtype),
        in_specs=[pl.BlockSpec((t, mc), lambda gi, mi, ti: (ti, gi * nm + mi)),
                  pl.BlockSpec((t, n), lambda gi, mi, ti: (ti, gi)), ANY],
        out_specs=pl.BlockSpec((None, mc, n), lambda gi, mi, ti: (p0 + gi, mi, 0)),
        scratch_shapes=[pltpu.VMEM((mc, n), F32)],
        input_output_aliases={2: 0},
        compiler_params=_cparams(3, VMEM_MID),
    )


def _rope_tables(positions):
    half = ROPE // 2
    inv = ROPE_THETA ** (-jnp.arange(half, dtype=F32) * 2.0 / ROPE)
    ang = positions.astype(F32)[:, None] * inv
    cos, sin = jnp.cos(ang), jnp.sin(ang)
    zero = jnp.zeros((positions.shape[0], LANES - ROPE), F32)
    return jnp.concatenate([cos, cos, zero], axis=1), jnp.concatenate([-sin, sin, zero], axis=1)


def _kv_specs(t):
    return [_full((1, D_MODEL)), _full((D_MODEL, KV_RANK)), _full((D_MODEL, LANES)), _full((1, KV_RANK)),
            _full((N_HEADS, KV_RANK, NOPE)), _full((N_HEADS, KV_RANK, V_DIM)),
            _full((1, NOPE)), _full((1, LANES)), _rows(t, LANES), _rows(t, LANES)]


def _kv_fwd(x, ln, wc, wpe, gl, wuk, wuv, gkn, gkr, cos, sin, name, deps=()):
    s = x.shape[0]
    t = min(256, s)

    def body(x_ref, ln_ref, wc_ref, wpe_ref, gl_ref, wuk_ref, wuv_ref, gkn_ref, gkr_ref, cos_ref, sin_ref,
             k_ref, v_ref):
        hn = _rms(x_ref[...], ln_ref[...], D_MODEL)[0].astype(BF16)
        clat = _nn(hn, wc_ref[...])
        kpe = _nn(hn, wpe_ref[...])
        cn = _rms(clat, gl_ref[...], KV_RANK)[0].astype(BF16)
        sspe = jnp.sum(kpe * kpe, axis=-1, keepdims=True)
        cs, sn = cos_ref[...], sin_ref[...]
        for h in range(N_HEADS):
            kn = _nn(cn, wuk_ref[h])
            r = lax.rsqrt((jnp.sum(kn * kn, axis=-1, keepdims=True) + sspe) * (1.0 / QK_DIM) + EPS)
            k_ref[:, h * QK_PAD:h * QK_PAD + NOPE] = ((kn * r) * gkn_ref[...]).astype(BF16)
            z = (kpe * r) * gkr_ref[...]
            k_ref[:, h * QK_PAD + NOPE:(h + 1) * QK_PAD] = (z * cs + _swap_halves(z) * sn).astype(BF16)
            v_ref[:, h * V_DIM:(h + 1) * V_DIM] = _nn(cn, wuv_ref[h]).astype(BF16)

    return _pcall(
        body, (x, ln, wc, wpe, gl, wuk, wuv, gkn, gkr, cos, sin), deps, name=name, grid=(s // t,),
        out_shape=[jax.ShapeDtypeStruct((s, N_HEADS * QK_PAD), BF16), jax.ShapeDtypeStruct((s, N_HEADS * V_DIM), BF16)],
        in_specs=[_rows(t, D_MODEL)] + _kv_specs(t),
        out_specs=[_rows(t, N_HEADS * QK_PAD), _rows(t, N_HEADS * V_DIM)],
        compiler_params=_cparams(1, VMEM_MID),
    )


def _kv_bwd(x, dxin, dks, dvs, ln, wc, wpe, gl, wuk, wuv, gkn, gkr, cos, sin, name):
    s = x.shape[0]
    t = min(256, s)
    nk = len(dks)

    def body(*refs):
        x_ref, dxin_ref = refs[:2]
        dk_refs = refs[2:2 + nk]
        dv_refs = refs[2 + nk:2 + 2 * nk]
        (ln_ref, wc_ref, wpe_ref, gl_ref, wuk_ref, wuv_ref, gkn_ref, gkr_ref, cos_ref, sin_ref,
         dx_ref, hn_ref, cn_ref, dkn_ref, dvb_ref, dcc_ref, dpe_ref,
         dln_ref, dgl_ref, dgkn_ref, dgkr_ref) = refs[2 + 2 * nk:]
        x = x_ref[...]
        ln = ln_ref[...]
        y, rx = _rms(x, ln, D_MODEL)
        hn = y.astype(BF16)
        hn_ref[...] = hn
        clat = _nn(hn, wc_ref[...])
        kpe = _nn(hn, wpe_ref[...])
        gl = gl_ref[...]
        cy, rc = _rms(clat, gl, KV_RANK)
        cn = cy.astype(BF16)
        cn_ref[...] = cn
        sspe = jnp.sum(kpe * kpe, axis=-1, keepdims=True)
        cs, sn = cos_ref[...], sin_ref[...]
        gkn, gkr = gkn_ref[...], gkr_ref[...]
        dc = jnp.zeros((t, KV_RANK), F32)
        dkpe = jnp.zeros((t, LANES), F32)
        dgkn = jnp.zeros((1, NOPE), F32)
        dgkr = jnp.zeros((1, LANES), F32)
        for h in range(N_HEADS):
            kn = _nn(cn, wuk_ref[h])
            r = lax.rsqrt((jnp.sum(kn * kn, axis=-1, keepdims=True) + sspe) * (1.0 / QK_DIM) + EPS)
            lo, mid, hi = h * QK_PAD, h * QK_PAD + NOPE, (h + 1) * QK_PAD
            dko = dk_refs[0][:, lo:mid]
            dkr = dk_refs[0][:, mid:hi]
            dvh = dv_refs[0][:, h * V_DIM:(h + 1) * V_DIM]
            for j in range(1, nk):
                dko = dko + dk_refs[j][:, lo:mid]
                dkr = dkr + dk_refs[j][:, mid:hi]
                dvh = dvh + dv_refs[j][:, h * V_DIM:(h + 1) * V_DIM]
            dz = dkr * cs - _swap_halves(dkr) * sn
            un = dko * gkn
            ur = dz * gkr
            sm = (jnp.sum(kn * un, axis=-1, keepdims=True) + jnp.sum(kpe * ur, axis=-1, keepdims=True)) * (1.0 / QK_DIM)
            coef = r * r * r * sm
            dkn = (r * un - kn * coef).astype(BF16)
            dkpe = dkpe + (r * ur - kpe * coef)
            dgkn = dgkn + jnp.sum(dko * (kn * r), axis=0, keepdims=True)
            dgkr = dgkr + jnp.sum(dz * (kpe * r), axis=0, keepdims=True)
            dkn_ref[:, h * NOPE:(h + 1) * NOPE] = dkn
            dvb = dvh.astype(BF16)
            dvb_ref[:, h * V_DIM:(h + 1) * V_DIM] = dvb
            dc = dc + _nt(dkn, wuk_ref[h]) + _nt(dvb, wuv_ref[h])
        dclat, dgl = _rms_bwd(clat, rc, gl, dc, KV_RANK)
        dcc = dclat.astype(BF16)
        dpe = dkpe.astype(BF16)
        dcc_ref[...] = dcc
        dpe_ref[...] = dpe
        dhn = _nt(dcc, wc_ref[...]) + _nt(dpe, wpe_ref[...])
        dxn, dln = _rms_bwd(x, rx, ln, dhn, D_MODEL)
        dx_ref[...] = dxin_ref[...] + dxn

        @pl.when(pl.program_id(0) == 0)
        def _():
            dln_ref[...] = jnp.zeros_like(dln_ref)
            dgl_ref[...] = jnp.zeros_like(dgl_ref)
            dgkn_ref[...] = jnp.zeros_like(dgkn_ref)
            dgkr_ref[...] = jnp.zeros_like(dgkr_ref)

        dln_ref[...] += dln
        dgl_ref[...] += dgl
        dgkn_ref[...] += dgkn
        dgkr_ref[...] += dgkr

    def tok(cols, dt):
        return jax.ShapeDtypeStruct((s, cols), dt)

    def vec(cols):
        return jax.ShapeDtypeStruct((1, cols), F32)

    return pl.pallas_call(
        body, name=name, grid=(s // t,),
        out_shape=[tok(D_MODEL, F32), tok(D_MODEL, BF16), tok(KV_RANK, BF16), tok(N_HEADS * NOPE, BF16),
                   tok(N_HEADS * V_DIM, BF16), tok(KV_RANK, BF16), tok(LANES, BF16),
                   vec(D_MODEL), vec(KV_RANK), vec(NOPE), vec(LANES)],
        in_specs=[_rows(t, D_MODEL), _rows(t, D_MODEL)] + [_rows(t, N_HEADS * QK_PAD)] * nk
                 + [_rows(t, N_HEADS * V_DIM)] * nk + _kv_specs(t),
        out_specs=[_rows(t, D_MODEL), _rows(t, D_MODEL), _rows(t, KV_RANK), _rows(t, N_HEADS * NOPE),
                   _rows(t, N_HEADS * V_DIM), _rows(t, KV_RANK), _rows(t, LANES),
                   _full((1, D_MODEL)), _full((1, KV_RANK)), _full((1, NOPE)), _full((1, LANES))],
        compiler_params=_cparams(1, VMEM_BIG),
    )(x, dxin, *dks, *dvs, ln, wc, wpe, gl, wuk, wuv, gkn, gkr, cos, sin)


def _q_specs(t):
    return [_full((1, D_MODEL)), _full((D_MODEL, Q_RANK)), _full((1, Q_RANK)), _full((N_HEADS, Q_RANK, QK_PAD)),
            _full((1, NOPE)), _full((1, LANES)), _rows(t, LANES), _rows(t, LANES)]


def _q_fwd(x, ln, wdq, gql, wuq, gqn, gqr, cos, sin, name, deps=()):
    s = x.shape[0]
    t = min(256, s)

    def body(x_ref, ln_ref, wdq_ref, gql_ref, wuq_ref, gqn_ref, gqr_ref, cos_ref, sin_ref, q_ref):
        hn = _rms(x_ref[...], ln_ref[...], D_MODEL)[0].astype(BF16)
        cqn = _rms(_nn(hn, wdq_ref[...]), gql_ref[...], Q_RANK)[0].astype(BF16)
        cs, sn = cos_ref[...], sin_ref[...]
        for h in range(N_HEADS):
            qa = _nn(cqn, wuq_ref[h])
            r = lax.rsqrt(jnp.sum(qa * qa, axis=-1, keepdims=True) * (1.0 / QK_DIM) + EPS)
            q_ref[:, h * QK_PAD:h * QK_PAD + NOPE] = ((qa[:, :NOPE] * r) * gqn_ref[...]).astype(BF16)
            z = (qa[:, NOPE:] * r) * gqr_ref[...]
            q_ref[:, h * QK_PAD + NOPE:(h + 1) * QK_PAD] = (z * cs + _swap_halves(z) * sn).astype(BF16)

    return _pcall(
        body, (x, ln, wdq, gql, wuq, gqn, gqr, cos, sin), deps, name=name, grid=(s // t,),
        out_shape=jax.ShapeDtypeStruct((s, N_HEADS * QK_PAD), BF16),
        in_specs=[_rows(t, D_MODEL)] + _q_specs(t),
        out_specs=_rows(t, N_HEADS * QK_PAD),
        compiler_params=_cparams(1, VMEM_MID),
    )


def _q_bwd(x, dxin, dq, ln, wdq, gql, wuq, gqn, gqr, cos, sin, name):
    s = x.shape[0]
    t = min(256, s)

    def body(x_ref, dxin_ref, dq_ref, ln_ref, wdq_ref, gql_ref, wuq_ref, gqn_ref, gqr_ref, cos_ref, sin_ref,
             dx_ref, hn_ref, cqn_ref, dqa_ref, dcq_ref, dln_ref, dgql_ref, dgqn_ref, dgqr_ref):
        x = x_ref[...]
        ln = ln_ref[...]
        y, rx = _rms(x, ln, D_MODEL)
        hn = y.astype(BF16)
        hn_ref[...] = hn
        cqp = _nn(hn, wdq_ref[...])
        gql = gql_ref[...]
        cy, rc = _rms(cqp, gql, Q_RANK)
        cqn = cy.astype(BF16)
        cqn_ref[...] = cqn
        cs, sn = cos_ref[...], sin_ref[...]
        gqn, gqr = gqn_ref[...], gqr_ref[...]
        dcq = jnp.zeros((t, Q_RANK), F32)
        dgqn = jnp.zeros((1, NOPE), F32)
        dgqr = jnp.zeros((1, LANES), F32)
        for h in range(N_HEADS):
            qa = _nn(cqn, wuq_ref[h])
            qn, qr = qa[:, :NOPE], qa[:, NOPE:]
            r = lax.rsqrt(jnp.sum(qa * qa, axis=-1, keepdims=True) * (1.0 / QK_DIM) + EPS)
            dqo = dq_ref[:, h * QK_PAD:h * QK_PAD + NOPE]
            dqr = dq_ref[:, h * QK_PAD + NOPE:(h + 1) * QK_PAD]
            dz = dqr * cs - _swap_halves(dqr) * sn
            un = dqo * gqn
            ur = dz * gqr
            sm = (jnp.sum(qn * un, axis=-1, keepdims=True) + jnp.sum(qr * ur, axis=-1, keepdims=True)) * (1.0 / QK_DIM)
            coef = r * r * r * sm
            dqa = jnp.concatenate([r * un - qn * coef, r * ur - qr * coef], axis=1).astype(BF16)
            dgqn = dgqn + jnp.sum(dqo * (qn * r), axis=0, keepdims=True)
            dgqr = dgqr + jnp.sum(dz * (qr * r), axis=0, keepdims=True)
            dqa_ref[:, h * QK_PAD:(h + 1) * QK_PAD] = dqa
            dcq = dcq + _nt(dqa, wuq_ref[h])
        dcqp, dgql = _rms_bwd(cqp, rc, gql, dcq, Q_RANK)
        dcqb = dcqp.astype(BF16)
        dcq_ref[...] = dcqb
        dhn = _nt(dcqb, wdq_ref[...])
        dxn, dln = _rms_bwd(x, rx, ln, dhn, D_MODEL)
        dx_ref[...] = dxin_ref[...] + dxn

        @pl.when(pl.program_id(0) == 0)
        def _():
            dln_ref[...] = jnp.zeros_like(dln_ref)
            dgql_ref[...] = jnp.zeros_like(dgql_ref)
            dgqn_ref[...] = jnp.zeros_like(dgqn_ref)
            dgqr_ref[...] = jnp.zeros_like(dgqr_ref)

        dln_ref[...] += dln
        dgql_ref[...] += dgql
        dgqn_ref[...] += dgqn
        dgqr_ref[...] += dgqr

    def tok(cols, dt):
        return jax.ShapeDtypeStruct((s, cols), dt)

    def vec(cols):
        return jax.ShapeDtypeStruct((1, cols), F32)

    return pl.pallas_call(
        body, name=name, grid=(s // t,),
        out_shape=[tok(D_MODEL, F32), tok(D_MODEL, BF16), tok(Q_RANK, BF16), tok(N_HEADS * QK_PAD, BF16),
                   tok(Q_RANK, BF16), vec(D_MODEL), vec(Q_RANK), vec(NOPE), vec(LANES)],
        in_specs=[_rows(t, D_MODEL), _rows(t, D_MODEL), _rows(t, N_HEADS * QK_PAD)] + _q_specs(t),
        out_specs=[_rows(t, D_MODEL), _rows(t, D_MODEL), _rows(t, Q_RANK), _rows(t, N_HEADS * QK_PAD),
                   _rows(t, Q_RANK), _full((1, D_MODEL)), _full((1, Q_RANK)), _full((1, NOPE)), _full((1, LANES))],
        compiler_params=_cparams(1, VMEM_MID),
    )(x, dxin, dq, ln, wdq, gql, wuq, gqn, gqr, cos, sin)


SM_SCALE = 1.0 / math.sqrt(QK_DIM)
LOG2_E = math.log2(math.e)
EXP2_SCALE = SM_SCALE * LOG2_E
NEG = -1e30


def _diag_mask(t):
    qpos = lax.broadcasted_iota(jnp.int32, (t, t), 0)
    kpos = lax.broadcasted_iota(jnp.int32, (t, t), 1)
    return lax.shift_right_logical(kpos, 6) <= lax.shift_right_logical(qpos, 6)


def _att_fwd(q, k, v, name):
    s = q.shape[0]
    t = min(512, s)
    nb = s // t

    def body(q_ref, k_ref, v_ref, o_ref, lse_ref):
        qi = pl.program_id(1)
        qq = q_ref[...]

        def block(ki, carry, masked):
            m_old, l_old, acc = carry
            rows = pl.ds(pl.multiple_of(ki * t, t), t)
            sc = _nt(qq, k_ref[rows, :])
            if masked:
                sc = jnp.where(_diag_mask(t), sc, NEG)
            m_new = jnp.maximum(m_old, jnp.max(sc, axis=-1, keepdims=True))
            p = jnp.exp2((sc - m_new) * EXP2_SCALE)
            alpha = jnp.exp2((m_old - m_new) * EXP2_SCALE)
            l_new = alpha * l_old + jnp.sum(p, axis=-1, keepdims=True)
            acc = alpha * acc + _nn(p.astype(BF16), v_ref[rows, :])
            return m_new, l_new, acc

        init = (jnp.full((t, 1), NEG, F32), jnp.zeros((t, 1), F32), jnp.zeros((t, V_DIM), F32))
        carry = lax.fori_loop(0, qi, lambda ki, c: block(ki, c, False), init)
        m_fin, l_fin, acc = block(qi, carry, True)
        o_ref[...] = (acc / l_fin).astype(BF16)
        lse_ref[...] = jnp.broadcast_to(m_fin * SM_SCALE + jnp.log(l_fin), (t, LANES))

    return pl.pallas_call(
        body, name=name, grid=(N_HEADS, nb),
        out_shape=[jax.ShapeDtypeStruct((s, N_HEADS * V_DIM), BF16), jax.ShapeDtypeStruct((s, N_HEADS * LANES), F32)],
        in_specs=[pl.BlockSpec((t, QK_PAD), lambda h, qi: (qi, h)),
                  pl.BlockSpec((s, QK_PAD), lambda h, qi: (0, h)),
                  pl.BlockSpec((s, V_DIM), lambda h, qi: (0, h))],
        out_specs=[pl.BlockSpec((t, V_DIM), lambda h, qi: (qi, h)),
                   pl.BlockSpec((t, LANES), lambda h, qi: (qi, h))],
        compiler_params=_cparams(2, VMEM_MID),
    )(q, k, v)


def _att_bwd(q, k, v, do, o, lse, name, deps=()):
    s = q.shape[0]
    t = min(512, s)
    nb = s // t

    def body(q_ref, k_ref, v_ref, do_ref, o_ref, lse_ref, dq_ref, dk_ref, dv_ref):
        ki = pl.program_id(1)
        kk, vv = k_ref[...], v_ref[...]

        @pl.when(ki == 0)
        def _():
            dq_ref[...] = jnp.zeros_like(dq_ref)

        def block(qi, carry, masked):
            dk, dv = carry
            rows = pl.ds(pl.multiple_of(qi * t, t), t)
            qq, dob = q_ref[rows, :], do_ref[rows, :]
            sc = _nt(qq, kk)
            if masked:
                sc = jnp.where(_diag_mask(t), sc, NEG)
            p = jnp.exp2(sc * EXP2_SCALE - lse_ref[rows, :][:, :1] * LOG2_E)
            dp = _nt(dob, vv)
            dsum = jnp.sum(dob.astype(F32) * o_ref[rows, :].astype(F32), axis=-1, keepdims=True)
            ds = (p * (dp - dsum)).astype(BF16)
            dq_ref[rows, :] += _nn(ds, kk)
            return dk + _tn(ds, qq), dv + _tn(p.astype(BF16), dob)

        carry = block(ki, (jnp.zeros((t, QK_PAD), F32), jnp.zeros((t, V_DIM), F32)), True)
        dk, dv = lax.fori_loop(ki + 1, nb, lambda qi, c: block(qi, c, False), carry)
        dk_ref[...] = dk * SM_SCALE
        dv_ref[...] = dv

        @pl.when(ki == nb - 1)
        def _():
            dq_ref[...] = dq_ref[...] * SM_SCALE

    def head(h, ki):
        return (0, h)

    def kblock(h, ki):
        return (ki, h)

    return _pcall(
        body, (q, k, v, do, o, lse), deps, name=name, grid=(N_HEADS, nb),
        out_shape=[jax.ShapeDtypeStruct((s, N_HEADS * QK_PAD), F32), jax.ShapeDtypeStruct((s, N_HEADS * QK_PAD), F32),
                   jax.ShapeDtypeStruct((s, N_HEADS * V_DIM), F32)],
        in_specs=[pl.BlockSpec((s, QK_PAD), head), pl.BlockSpec((t, QK_PAD), kblock), pl.BlockSpec((t, V_DIM), kblock),
                  pl.BlockSpec((s, V_DIM), head), pl.BlockSpec((s, V_DIM), head), pl.BlockSpec((s, LANES), head)],
        out_specs=[pl.BlockSpec((s, QK_PAD), head), pl.BlockSpec((t, QK_PAD), kblock), pl.BlockSpec((t, V_DIM), kblock)],
        compiler_params=_cparams(2, VMEM_MID),
    )


def _o_fwd(x, o, wo, name):
    s = x.shape[0]
    t = min(512, s)

    def body(x_ref, o_ref, wo_ref, xo_ref):
        xo_ref[...] = x_ref[...] + _nn(o_ref[...], wo_ref[...])

    return pl.pallas_call(
        body, name=name, grid=(s // t,),
        out_shape=jax.ShapeDtypeStruct((s, D_MODEL), F32),
        in_specs=[_rows(t, D_MODEL), _rows(t, D_MODEL), _full((D_MODEL, D_MODEL))],
        out_specs=_rows(t, D_MODEL),
        compiler_params=_cparams(1, VMEM_MID),
    )(x, o, wo)


def _o_bwd(dx, wo, name, deps=()):
    s = dx.shape[0]
    t = min(512, s)

    def body(dx_ref, wo_ref, do_ref, dxb_ref):
        dxb = dx_ref[...].astype(BF16)
        dxb_ref[...] = dxb
        do_ref[...] = _nt(dxb, wo_ref[...]).astype(BF16)

    tok = jax.ShapeDtypeStruct((s, D_MODEL), BF16)
    return _pcall(
        body, (dx, wo), deps, name=name, grid=(s // t,),
        out_shape=[tok, tok],
        in_specs=[_rows(t, D_MODEL), _full((D_MODEL, D_MODEL))],
        out_specs=[_rows(t, D_MODEL), _rows(t, D_MODEL)],
        compiler_params=_cparams(1, VMEM_MID),
    )


def _loss_head(y, target, name):
    s = y.shape[0]
    t = min(512, s)

    def body(y_ref, t_ref, dy_ref, sq_ref):
        e = y_ref[...] - t_ref[...]
        dy_ref[...] = e * (1.0 / D_MODEL)

        @pl.when(pl.program_id(0) == 0)
        def _():
            sq_ref[...] = jnp.zeros_like(sq_ref)

        sq_ref[...] += jnp.sum(e * e, axis=0, keepdims=True)

    return pl.pallas_call(
        body, name=name, grid=(s // t,),
        out_shape=[jax.ShapeDtypeStruct((s, D_MODEL), F32), jax.ShapeDtypeStruct((1, D_MODEL), F32)],
        in_specs=[_rows(t, D_MODEL), _rows(t, D_MODEL)],
        out_specs=[_rows(t, D_MODEL), _full((1, D_MODEL))],
        compiler_params=_cparams(1),
    )(y, target)


def _adamw(w, g, m, v, name):
    shape = w.shape
    c = shape[-1]
    r = math.prod(shape[:-1])
    tb = r
    for cand in (512, 256, 128):
        if r % cand == 0 and r > cand:
            tb = cand
            break

    def body(w_ref, g_ref, m_ref, v_ref, d_ref, mo_ref, vo_ref):
        gr = g_ref[...]
        mn = ADAM_B1 * m_ref[...] + (1.0 - ADAM_B1) * gr
        vn = ADAM_B2 * v_ref[...] + (1.0 - ADAM_B2) * (gr * gr)
        m_hat = mn / (1.0 - ADAM_B1 ** ADAM_STEP)
        v_hat = vn / (1.0 - ADAM_B2 ** ADAM_STEP)
        d_ref[...] = -ADAM_LR * (m_hat / (jnp.sqrt(v_hat) + ADAM_EPS) + ADAM_WD * w_ref[...])
        mo_ref[...] = mn
        vo_ref[...] = vn

    spec = pl.BlockSpec((tb, c), lambda i: (i, 0))
    flat = jax.ShapeDtypeStruct((r, c), F32)
    outs = pl.pallas_call(
        body, name=name, grid=(r // tb,),
        out_shape=[flat, flat, flat],
        in_specs=[spec] * 4, out_specs=[spec] * 3,
        compiler_params=_cparams(1),
    )(w.reshape(r, c), g.reshape(r, c), m.reshape(r, c), v.reshape(r, c))
    return [a.reshape(shape) for a in outs]


def _pad_cols(a, width):
    return jnp.pad(a, [(0, 0)] * (a.ndim - 1) + [(0, width - a.shape[-1])])


def _owner_view(a, sz):
    return a.reshape(a.shape[0], N_CHIPS, 2, sz, a.shape[-1])


def kernel(x, positions, ln_mix_a, w_pool, b_pool, pool_scale, ln_ffn, w_gate, w_up, w_down, ln_kv, w_dkv, g_kv_latent, w_uk, w_uv, g_k, ln_mix_b, w_dq, g_q_latent, w_uq, g_q, w_o, loss_target, m_ln_mix_a, m_w_pool, m_b_pool, m_pool_scale, m_ln_ffn, m_w_gate, m_w_up, m_w_down, m_ln_kv, m_w_dkv, m_g_kv_latent, m_w_uk, m_w_uv, m_g_k, m_ln_mix_b, m_w_dq, m_g_q_latent, m_w_uq, m_g_q, m_w_o, v_ln_mix_a, v_w_pool, v_b_pool, v_pool_scale, v_ln_ffn, v_w_gate, v_w_up, v_w_down, v_ln_kv, v_w_dkv, v_g_kv_latent, v_w_uk, v_w_uv, v_g_k, v_ln_mix_b, v_w_dq, v_g_q_latent, v_w_uq, v_g_q, v_w_o):
    weights = dict(ln_mix_a=ln_mix_a, w_pool=w_pool, b_pool=b_pool, pool_scale=pool_scale, ln_ffn=ln_ffn,
                   w_gate=w_gate, w_up=w_up, w_down=w_down, ln_kv=ln_kv, w_dkv=w_dkv, g_kv_latent=g_kv_latent,
                   w_uk=w_uk, w_uv=w_uv, g_k=g_k, ln_mix_b=ln_mix_b, w_dq=w_dq, g_q_latent=g_q_latent,
                   w_uq=w_uq, g_q=g_q, w_o=w_o)
    mom1 = dict(ln_mix_a=m_ln_mix_a, w_pool=m_w_pool, b_pool=m_b_pool, pool_scale=m_pool_scale, ln_ffn=m_ln_ffn,
                w_gate=m_w_gate, w_up=m_w_up, w_down=m_w_down, ln_kv=m_ln_kv, w_dkv=m_w_dkv,
                g_kv_latent=m_g_kv_latent, w_uk=m_w_uk, w_uv=m_w_uv, g_k=m_g_k, ln_mix_b=m_ln_mix_b, w_dq=m_w_dq,
                g_q_latent=m_g_q_latent, w_uq=m_w_uq, g_q=m_g_q, w_o=m_w_o)
    mom2 = dict(ln_mix_a=v_ln_mix_a, w_pool=v_w_pool, b_pool=v_b_pool, pool_scale=v_pool_scale, ln_ffn=v_ln_ffn,
                w_gate=v_w_gate, w_up=v_w_up, w_down=v_w_down, ln_kv=v_ln_kv, w_dkv=v_w_dkv,
                g_kv_latent=v_g_kv_latent, w_uk=v_w_uk, w_uv=v_w_uv, g_k=v_g_k, ln_mix_b=v_ln_mix_b, w_dq=v_w_dq,
                g_q_latent=v_g_q_latent, w_uq=v_w_uq, g_q=v_g_q, w_o=v_w_o)
    names = list(weights)
    dev = 4 * lax.axis_index("x") + 2 * lax.axis_index("y") + lax.axis_index("c")
    core = lax.axis_index("c").astype(jnp.int32).reshape(1)
    chip = (2 * lax.axis_index("x") + lax.axis_index("y")).astype(jnp.int32).reshape(1)

    xs = x[0]
    target = loss_target[0]
    cos, sin = _rope_tables(positions[0])

    small_sh = jnp.concatenate([ln_mix_a.reshape(1, -1), pool_scale.reshape(1, -1), b_pool.reshape(1, -1)], axis=1)
    wp_g, small_g = _all_gather([w_pool.astype(BF16), small_sh], [2, 0], "gather_first")
    wp_all = wp_g.reshape(2, 4, GROUP_DIM, GROUP_DIM)
    small_g = small_g.reshape(N_DEV, 3, 2, LANES)
    ln_a_all = small_g[:, 0].transpose(1, 0, 2).reshape(2, 1, D_MODEL)
    sc_all = small_g[:, 1].transpose(1, 0, 2).reshape(2, 1, D_MODEL)
    bp_all = small_g[:, 2].reshape(N_DEV, 2, 4, 32).transpose(1, 2, 0, 3).reshape(2, 1, D_MODEL)

    def placed(shard):
        buf = lax.empty((shard.shape[0], N_DEV) + shard.shape[1:], shard.dtype)
        return lax.dynamic_update_slice(buf, shard[:, None], (0, dev, 0, 0))

    ffn_sh = jnp.stack([w_gate.transpose(0, 2, 1), w_up.transpose(0, 2, 1), w_down], axis=1).astype(BF16)
    groups = {f"ffn{l}": [placed(ffn_sh[l])] for l in range(4)}
    groups["att"] = [placed(a.astype(BF16)) for a in (
        w_dkv[None, :, :KV_RANK], _pad_cols(w_dkv[None, :, KV_RANK:], LANES), w_uk[None], w_uv[None],
        w_dq, _pad_cols(w_uq, QK_PAD), w_o)]
    def spread_start(nm, deps):
        return _copies_start(groups[nm], len(groups[nm]), _gather_spread, f"spread_{nm}", deps=deps)

    def spread_wait(nm, state, after):
        ssem, rsem, bufs, _ = state
        return _copies_wait(bufs, ssem, rsem, after, _blocks_moved(4), f"spread_done_{nm}")

    def relay_start(nm, bufs, deps=()):
        return _copies_start(bufs, len(bufs), _gather_relay, f"relay_{nm}", deps=deps)

    def relay_wait(nm, state, after):
        ssem, rsem, bufs, _ = state
        return _copies_wait(bufs, ssem, rsem, after, _blocks_moved(3), f"relay_done_{nm}")

    gkn = g_k[:NOPE].reshape(1, NOPE)
    gkr = _pad_cols(g_k[NOPE:].reshape(1, ROPE), LANES)
    gl = g_kv_latent.reshape(1, KV_RANK)
    lnkv = ln_kv.reshape(1, D_MODEL)

    x_in, x_mid, pooled, gates, ups, w_ffn = [], [], [], [], [], []
    qs, outs, lses = [], [], []

    def mixer(l, cur, deps):
        x_in.append(cur)
        mid, dsave = _mix_fwd(cur, ln_a_all[l], wp_all[l], bp_all[l], sc_all[l], f"mix_fwd{l}", deps=deps)
        pooled.append(dsave)
        x_mid.append(mid)
        return mid

    def q_args(j):
        return (ln_mix_b[j].reshape(1, -1), wdq_all[j], g_q_latent[j].reshape(1, -1), wuq_all[j],
                g_q[j, :NOPE].reshape(1, -1), _pad_cols(g_q[j, NOPE:].reshape(1, -1), LANES), cos, sin)

    def attention(j, cur, deps):
        x_in.append(cur)
        q = _q_fwd(cur, *q_args(j), f"q_fwd{j}", deps=deps)
        o, lse = _att_fwd(q, k_sh, v_sh, f"att_fwd{j}")
        mid = _o_fwd(cur, o, wo_all[j], f"o_fwd{j}")
        qs.append(q)
        outs.append(o)
        lses.append(lse)
        x_mid.append(mid)
        return mid

    def ffn(l, mid, relayed):
        w_l = relayed[0].reshape(3, D_FF, D_MODEL)
        w_ffn.append(w_l)
        cur, gate, up = _ffn_fwd(mid, ln_ffn[l].reshape(1, -1), w_l, f"ffn_fwd{l}")
        gates.append(gate)
        ups.append(up)
        return cur

    sp0 = spread_start("ffn0", [small_g])
    mid = mixer(0, xs, [sp0[3]])
    landed0 = spread_wait("ffn0", sp0, mid)
    sp1 = spread_start("ffn1", [landed0[0]])
    rl0 = relay_start("ffn0", landed0, [sp1[3]])
    cur = ffn(0, mid, relay_wait("ffn0", rl0, rl0[3]))

    landed1 = spread_wait("ffn1", sp1, cur)
    sp_att = spread_start("att", [landed1[0]])
    sp2 = spread_start("ffn2", [landed1[0]])
    rl1 = relay_start("ffn1", landed1, [sp_att[3], sp2[3]])
    mid = mixer(1, cur, [rl1[3]])
    cur = ffn(1, mid, relay_wait("ffn1", rl1, mid))
    x_kv = cur

    landed_att = spread_wait("att", sp_att, cur)
    landed2 = spread_wait("ffn2", sp2, cur)
    sp3 = spread_start("ffn3", [landed2[0]])
    rl_att = relay_start("att", landed_att, [sp3[3]])
    rl2 = relay_start("ffn2", landed2, [sp3[3]])
    att_bufs = relay_wait("att", rl_att, rl2[3])
    wc = att_bufs[0].reshape(D_MODEL, KV_RANK)
    wpe = att_bufs[1].reshape(D_MODEL, LANES)
    wuk_g = att_bufs[2].reshape(N_HEADS, KV_RANK, NOPE)
    wuv_g = att_bufs[3].reshape(N_HEADS, KV_RANK, V_DIM)
    wdq_all = att_bufs[4].reshape(2, D_MODEL, Q_RANK)
    wuq_all = att_bufs[5]
    wo_all = att_bufs[6].reshape(2, D_MODEL, D_MODEL)
    k_sh, v_sh = _kv_fwd(cur, lnkv, wc, wpe, gl, wuk_g, wuv_g, gkn, gkr, cos, sin, "kv_fwd")
    mid = attention(0, cur, [])
    cur = ffn(2, mid, relay_wait("ffn2", rl2, mid))

    landed3 = spread_wait("ffn3", sp3, cur)
    rl3 = relay_start("ffn3", landed3)
    mid = attention(1, cur, [rl3[3]])
    cur = ffn(3, mid, relay_wait("ffn3", rl3, mid))

    dx, sq_cols = _loss_head(cur, target, "loss_head")

    small = {}
    sizes = dict(ffn0=FF_SHARD, ffn1=FF_SHARD, ffn2=FF_SHARD, ffn3=FF_SHARD, wo=128, kv512=128, dkv_pe=128,
                 wdq=128, wuqT=QK_PAD, wpool=32)
    big = dict(wo=lax.empty((2, D_MODEL, D_MODEL), BF16), kv512=lax.empty((3, D_MODEL, KV_RANK), BF16),
               dkv_pe=lax.empty((1, D_MODEL, LANES), BF16), wdq=lax.empty((2, D_MODEL, Q_RANK), BF16),
               wuqT=lax.empty((2, N_HEADS * QK_PAD, Q_RANK), BF16), wpool=lax.empty((8, GROUP_DIM, GROUP_DIM), BF16))
    for l in range(4):
        big[f"ffn{l}"] = lax.empty((3, D_FF, D_MODEL), BF16)
    red = {}

    def pair_start(nms, tag):
        arrs = []
        for nm in nms:
            view = _owner_view(big[nm], sizes[nm])
            arrs += [view, lax.empty((view.shape[0], N_CHIPS) + view.shape[3:], BF16)]
        return nms, tag, _copies_start(arrs, len(nms), _pair_send, f"pair_start_{tag}")

    def chip_start(state, after):
        nms, tag, (ssem, rsem, arrs, _) = state
        arrs = _copies_wait(arrs, ssem, rsem, after, _landed, f"pair_done_{tag}")
        out = []
        for t, nm in enumerate(nms):
            part = _pair_sum(arrs[2 * t], arrs[2 * t + 1], core, f"pair_sum_{nm}")
            out += [part, lax.empty((3, part.shape[0]) + part.shape[2:], BF16)]
        return nms, tag, _copies_start(out, len(nms), _chip_send, f"chip_start_{tag}")

    def chip_finish(state, after):
        nms, tag, (ssem, rsem, arrs, _) = state
        arrs = _copies_wait(arrs, ssem, rsem, after, _landed, f"chip_done_{tag}")
        for t, nm in enumerate(nms):
            red[nm] = _chip_sum(arrs[2 * t], arrs[2 * t + 1], chip, f"chip_sum_{nm}")

    dks, dvs = [], []
    pending = None
    bwd_deps = []
    for l in (3, 2, 1, 0):
        key = f"ffn{l}"
        dx, act, dgb, dub, hn, dyb, dln = _ffn_bwd(x_mid[l], dx, gates[l], ups[l], ln_ffn[l].reshape(1, -1),
                                                     w_ffn[l], f"ffn_bwd{l}", deps=bwd_deps)
        bwd_deps = []
        small[f"ln_ffn{l}"] = dln
        if l == 1:
            att_chip = chip_start(att_pair, dx)
            tn_deps = [att_chip[2][3]]
        else:
            tn_deps = []
        if pending:
            chip_finish(pending, dx)
            pending = None
        big[key] = _tn_matmul(dgb, hn, big[key], 0, f"dw_gate{l}", m_chunk=FF_HALF, deps=tn_deps)
        big[key] = _tn_matmul(dub, hn, big[key], 1, f"dw_up{l}", m_chunk=FF_HALF)
        big[key] = _tn_matmul(act, dyb, big[key], 2, f"dw_down{l}", m_chunk=FF_HALF)
        if l == 1:
            chip_finish(att_chip, big[key])
        ffn_pair = pair_start([key], key)
        if l >= 2:
            j = l - 2
            do, dxb = _o_bwd(dx, wo_all[j], f"o_bwd{j}", deps=[ffn_pair[2][3]])
            big["wo"] = _tn_matmul(outs[j], dxb, big["wo"], j, f"dw_o{j}")
            ffn_chip = chip_start(ffn_pair, big["wo"])
            dq, dk, dv = _att_bwd(qs[j], k_sh, v_sh, do, outs[j], lses[j], f"att_bwd{j}", deps=[ffn_chip[2][3]])
            chip_finish(ffn_chip, dq)
            dks.append(dk)
            dvs.append(dv)
            dx, hnq, cqn, dqa, dcq, dln, dgql, dgqn, dgqr = _q_bwd(x_in[l], dx, dq, *q_args(j), f"q_bwd{j}")
            small[f"ln_mix_b{j}"] = dln
            small[f"g_q_latent{j}"] = dgql
            small[f"g_q{j}"] = jnp.concatenate([dgqn, dgqr[:, :ROPE]], axis=1)
            big["wdq"] = _tn_matmul(hnq, dcq, big["wdq"], j, f"dw_dq{j}")
            big["wuqT"] = _tn_matmul(dqa, cqn, big["wuqT"], j, f"dw_uq{j}")
            if l == 2:
                (dx, hnk, cn, dknb, dvb, dccb, dpeb, dlnkv, dgl, dgkn, dgkr) = _kv_bwd(
                    x_kv, dx, dks, dvs, lnkv, wc, wpe, gl, wuk_g, wuv_g, gkn, gkr, cos, sin, "kv_bwd")
                small["ln_kv"] = dlnkv
                small["g_kv_latent"] = dgl
                small["g_k"] = jnp.concatenate([dgkn, dgkr[:, :ROPE]], axis=1)
                big["kv512"] = _tn_matmul(dknb, cn, big["kv512"], 0, "dw_uk")
                big["kv512"] = _tn_matmul(dvb, cn, big["kv512"], 1, "dw_uv")
                big["kv512"] = _tn_matmul(hnk, dccb, big["kv512"], 2, "dw_dkv_c")
                big["dkv_pe"] = _tn_matmul(hnk, dpeb, big["dkv_pe"], 0, "dw_dkv_pe")
                att_pair = pair_start(["wo", "kv512", "dkv_pe", "wdq", "wuqT"], "att")
                bwd_deps = [att_pair[2][3]]
        else:
            dx, dyp, dsc, db, dln = _mix_bwd(x_in[l], dx, pooled[l], ln_a_all[l], wp_all[l], bp_all[l], sc_all[l],
                                             f"mix_bwd{l}", deps=[ffn_pair[2][3]])
            small[f"ln_mix_a{l}"] = dln
            small[f"pool_scale{l}"] = dsc
            small[f"b_pool{l}"] = db
            ffn_chip = chip_start(ffn_pair, dx)
            big["wpool"] = _tn_matmul(pooled[l], dyp, big["wpool"], 4 * l, f"dw_pool{l}", groups=4,
                                      deps=[ffn_chip[2][3]])
            if l == 1:
                pending = ffn_chip
            else:
                chip_finish(ffn_chip, big["wpool"])
    grad_x = dx[None]
    pool_pair = pair_start(["wpool"], "wpool")
    pool_chip = chip_start(pool_pair, pool_pair[2][3])
    chip_finish(pool_chip, pool_chip[2][3])

    vec_names = (["loss"] + [f"ln_ffn{l}" for l in range(4)] + ["ln_kv", "g_kv_latent", "g_k"]
                 + [f"{p}{j}" for p in ("ln_mix_b", "g_q_latent", "g_q") for j in range(2)]
                 + [f"{p}{l}" for p in ("ln_mix_a", "pool_scale", "b_pool") for l in range(2)])
    small["loss"] = sq_cols
    widths = [small[nm].shape[1] for nm in vec_names]
    padded = [-(-w // LANES) * LANES for w in widths]
    packed = jnp.concatenate([_pad_cols(small[nm], pw) for nm, pw in zip(vec_names, padded)], axis=1)
    (all_vecs,) = _all_gather([packed], [0], "gather_vectors")
    total = _sum_lead(all_vecs, "sum_vectors")
    vec = {}
    off = 0
    for nm, w, pw in zip(vec_names, widths, padded):
        vec[nm] = total[0, off:off + w]
        off += pw
    loss = 0.5 * jnp.sum(vec["loss"]) * (1.0 / D_MODEL)

    def own_cols(full, width):
        return lax.dynamic_slice_in_dim(full, dev * width, width, axis=full.ndim - 1)

    grads = dict(
        ln_mix_a=own_cols(jnp.stack([vec["ln_mix_a0"], vec["ln_mix_a1"]]), LANES),
        w_pool=red["wpool"].reshape(2, 4, 32, GROUP_DIM),
        b_pool=own_cols(jnp.stack([vec["b_pool0"], vec["b_pool1"]]).reshape(2, 4, GROUP_DIM), 32),
        pool_scale=own_cols(jnp.stack([vec["pool_scale0"], vec["pool_scale1"]]), LANES),
        ln_ffn=jnp.stack([vec[f"ln_ffn{l}"] for l in range(4)]),
        w_gate=jnp.stack([red[f"ffn{l}"][0] for l in range(4)]).transpose(0, 2, 1),
        w_up=jnp.stack([red[f"ffn{l}"][1] for l in range(4)]).transpose(0, 2, 1),
        w_down=jnp.stack([red[f"ffn{l}"][2] for l in range(4)]),
        ln_kv=vec["ln_kv"],
        w_dkv=jnp.concatenate([red["kv512"][2], red["dkv_pe"][0][:, :ROPE]], axis=1),
        g_kv_latent=vec["g_kv_latent"],
        w_uk=red["kv512"][0].T,
        w_uv=red["kv512"][1].T,
        g_k=vec["g_k"],
        ln_mix_b=jnp.stack([vec["ln_mix_b0"], vec["ln_mix_b1"]]),
        w_dq=red["wdq"],
        g_q_latent=jnp.stack([vec["g_q_latent0"], vec["g_q_latent1"]]),
        w_uq=red["wuqT"].transpose(0, 2, 1)[:, :, :QK_DIM],
        g_q=jnp.stack([vec["g_q0"], vec["g_q1"]]),
        w_o=red["wo"],
    )

    deltas, new_m, new_v = {}, {}, {}
    for nm in names:
        w = weights[nm]
        shape = w.shape if w.ndim > 1 else (1, w.shape[0])
        d, mo, vo = _adamw(w.reshape(shape), grads[nm].reshape(shape), mom1[nm].reshape(shape),
                           mom2[nm].reshape(shape), f"adamw_{nm}")
        deltas[nm], new_m[nm], new_v[nm] = d.reshape(w.shape), mo.reshape(w.shape), vo.reshape(w.shape)

    return (loss, grad_x, *[grads[nm].reshape(weights[nm].shape) for nm in names], *[deltas[nm] for nm in names],
            *[new_m[nm] for nm in names], *[new_v[nm] for nm in names])
```

```python
import functools
import math

import jax
import jax.numpy as jnp
from jax import lax
from jax.experimental import pallas as pl
from jax.experimental.pallas import tpu as pltpu

F32 = jnp.float32
BF16 = jnp.bfloat16
MESH = pl.DeviceIdType.MESH

D_MODEL = 1024
D_FF = 2816
N_DEV = 8
N_CHIPS = 4
FF_SHARD = D_FF // N_DEV
FF_HALF = D_FF // 2
N_HEADS = 8
NOPE = 128
ROPE = 64
QK_DIM = NOPE + ROPE
QK_PAD = 256
V_DIM = 128
Q_RANK = 256
KV_RANK = 512
POOL_WINDOWS = (2, 4, 8, 16)
GROUP_DIM = 256
HALO = 128
CHUNK = 64
ROPE_THETA = 10000.0
EPS = 1e-6
LANES = 128

ADAM_LR = 0.001
ADAM_B1 = 0.9
ADAM_B2 = 0.999
ADAM_EPS = 1e-08
ADAM_WD = 0.01
ADAM_STEP = 10

VMEM_BIG = 56 * 2**20
VMEM_MID = 40 * 2**20


def _nn(a, b):
    return lax.dot_general(a, b, (((1,), (0,)), ((), ())), preferred_element_type=F32)


def _nt(a, b):
    return lax.dot_general(a, b, (((1,), (1,)), ((), ())), preferred_element_type=F32)


def _tn(a, b):
    return lax.dot_general(a, b, (((0,), (0,)), ((), ())), preferred_element_type=F32)


def _rms(x, g, n):
    r = lax.rsqrt(jnp.sum(x * x, axis=-1, keepdims=True) * (1.0 / n) + EPS)
    return (x * r) * g, r


def _rms_bwd(x, r, g, dy, n):
    u = dy * g
    s = jnp.sum(x * u, axis=-1, keepdims=True) * (1.0 / n)
    dx = r * u - x * (r * r * r * s)
    dg = jnp.sum(dy * (x * r), axis=0, keepdims=True)
    return dx, dg


def _swap_halves(z):
    lane = lax.broadcasted_iota(jnp.int32, z.shape, 1)
    return jnp.where(lane < ROPE // 2, pltpu.roll(z, LANES - ROPE // 2, 1), pltpu.roll(z, ROPE // 2, 1))


def _sigmoid(x):
    return 1.0 / (1.0 + jnp.exp(-x))


def _cparams(n_grid, vmem=None):
    return pltpu.CompilerParams(dimension_semantics=("arbitrary",) * n_grid, vmem_limit_bytes=vmem)


def _rows(t, cols):
    return pl.BlockSpec((t, cols), lambda i: (i, 0))


def _full(shape):
    nd = len(shape)
    return pl.BlockSpec(shape, lambda *_: (0,) * nd)


ANY = pl.BlockSpec(memory_space=pl.ANY)


def _pcall(body, args, deps, *, in_specs, **kw):
    n_in, n_dep = len(args), len(deps)

    def ordered(*refs):
        body(*refs[:n_in], *refs[n_in + n_dep:])

    return pl.pallas_call(ordered, in_specs=list(in_specs) + [ANY] * n_dep, **kw)(*args, *deps)


def _place():
    x, y, c = lax.axis_index("x"), lax.axis_index("y"), lax.axis_index("c")
    return x, y, c


def _all_gather(shards, axes, name):
    n = len(shards)
    out_shape = [jax.ShapeDtypeStruct(s.shape[:a] + (N_DEV,) + s.shape[a:], s.dtype) for s, a in zip(shards, axes)]

    def body(*refs):
        ins, outs = refs[:n], refs[n:2 * n]
        send_sems, recv_sems, local_sems = refs[2 * n:]
        x, y, c = _place()
        me, sibling = (x, y, c), (x, y, 1 - c)
        chips = [(1 - x, y), (x, 1 - y), (1 - x, 1 - y)]

        def slot(t, dev):
            idx = 4 * dev[0] + 2 * dev[1] + dev[2]
            return outs[t].at[(slice(None),) * axes[t] + (idx,)]

        def copy(t, k, block, to, src=None):
            return pltpu.make_async_remote_copy(
                src_ref=slot(t, block) if src is None else src, dst_ref=slot(t, block),
                send_sem=send_sems.at[t, k], recv_sem=recv_sems.at[t, k],
                device_id=to, device_id_type=MESH)

        mine = [pltpu.make_async_copy(ins[t], slot(t, me), local_sems.at[t]) for t in range(n)]
        for cp in mine:
            cp.start()
        first = []
        for t in range(n):
            first.append(copy(t, 0, me, sibling, src=ins[t]))
            first += [copy(t, 1 + j, me, (*chip, c), src=ins[t]) for j, chip in enumerate(chips)]
        for cp in first:
            cp.start()
        passed = []
        for j, chip in enumerate(chips):
            for t in range(n):
                copy(t, 1 + j, (*chip, c), me).wait_recv()
                cp = copy(t, 4 + j, (*chip, c), sibling)
                cp.start()
                passed.append(cp)
        for t in range(n):
            copy(t, 0, sibling, me).wait_recv()
            for j, chip in enumerate(chips):
                copy(t, 4 + j, (*chip, 1 - c), me).wait_recv()
        for cp in first + passed:
            cp.wait_send()
        for cp in mine:
            cp.wait()

    return pl.pallas_call(
        body, name=name, out_shape=out_shape,
        in_specs=[ANY] * n, out_specs=[ANY] * n,
        scratch_shapes=[pltpu.SemaphoreType.DMA((n, 7)), pltpu.SemaphoreType.DMA((n, 7)),
                        pltpu.SemaphoreType.DMA((n,))],
    )(*shards)


HBM = pl.BlockSpec(memory_space=pltpu.HBM)
SEM = pl.BlockSpec(memory_space=pltpu.SEMAPHORE)
EFFECT = pltpu.SideEffectType.DATAFLOW_SIDE_EFFECTING


def _copies_start(arrays, n_sems, plan, name, deps=()):
    n, nd = len(arrays), len(deps)

    def body(*refs):
        for cp in plan(refs[:n], refs[n + nd], refs[n + nd + 1]):
            cp.start()
        refs[-1][...] = jnp.zeros_like(refs[-1])

    outs = pl.pallas_call(
        body, name=name,
        out_shape=(pltpu.SemaphoreType.DMA((n_sems,)), pltpu.SemaphoreType.DMA((n_sems,)),
                   *[pltpu.HBM(a.shape, a.dtype) for a in arrays], jax.ShapeDtypeStruct((8, LANES), F32)),
        in_specs=[HBM] * n + [ANY] * nd,
        out_specs=(SEM, SEM, *[HBM] * n, pl.BlockSpec(memory_space=pltpu.VMEM)),
        input_output_aliases={i: 2 + i for i in range(n)},
        compiler_params=pltpu.CompilerParams(has_side_effects=EFFECT),
    )(*[pltpu.with_memory_space_constraint(a, pltpu.HBM) for a in arrays], *deps)
    return outs[0], outs[1], list(outs[2:2 + n]), outs[-1]


def _copies_wait(arrays, send_sems, recv_sems, after, plan, name):
    n = len(arrays)

    def body(*refs):
        for cp in plan(refs[:n], refs[n], refs[n + 1]):
            cp.wait_send()
            cp.wait_recv()

    outs = pl.pallas_call(
        body, name=name,
        out_shape=tuple(pltpu.HBM(a.shape, a.dtype) for a in arrays),
        in_specs=[HBM] * n + [SEM, SEM, ANY], out_specs=tuple([HBM] * n),
        input_output_aliases={i: i for i in range(n)},
        compiler_params=pltpu.CompilerParams(has_side_effects=EFFECT),
    )(*arrays, send_sems, recv_sems, after)
    return list(outs)


def _remote(src, dst, send_sems, recv_sems, t, to):
    return pltpu.make_async_remote_copy(src_ref=src, dst_ref=dst, send_sem=send_sems.at[t], recv_sem=recv_sems.at[t],
                                        device_id=to, device_id_type=MESH)


def _dev_index(x, y, c):
    return 4 * x + 2 * y + c


def _gather_spread(bufs, send_sems, recv_sems):
    x, y, c = _place()
    mine = _dev_index(x, y, c)
    peers = [(x, y, 1 - c), (1 - x, y, c), (x, 1 - y, c), (1 - x, 1 - y, c)]
    return [_remote(g.at[k, mine], g.at[k, mine], send_sems, recv_sems, t, peer)
            for t, g in enumerate(bufs) for peer in peers for k in range(g.shape[0])]


def _gather_relay(bufs, send_sems, recv_sems):
    x, y, c = _place()
    blocks = [_dev_index(1 - x, y, c), _dev_index(x, 1 - y, c), _dev_index(1 - x, 1 - y, c)]
    return [_remote(g.at[k, b], g.at[k, b], send_sems, recv_sems, t, (x, y, 1 - c))
            for t, g in enumerate(bufs) for b in blocks for k in range(g.shape[0])]


def _blocks_moved(count):
    def plan(bufs, send_sems, recv_sems):
        x, y, c = _place()
        return [_remote(g.at[:, pl.ds(0, count)], g.at[:, pl.ds(0, count)], send_sems, recv_sems, t, (x, y, 1 - c))
                for t, g in enumerate(bufs)]
    return plan


def _pair_send(arrs, send_sems, recv_sems):
    x, y, c = _place()
    return [_remote(arrs[2 * t].at[p, k, 1 - c], arrs[2 * t + 1].at[p, k], send_sems, recv_sems, t, (x, y, 1 - c))
            for t in range(len(arrs) // 2) for p in range(arrs[2 * t].shape[0]) for k in range(N_CHIPS)]


def _chip_send(arrs, send_sems, recv_sems):
    x, y, c = _place()
    chips = [(1 - x, y), (x, 1 - y), (1 - x, 1 - y)]
    return [_remote(arrs[2 * t].at[p, 2 * px + py], arrs[2 * t + 1].at[j, p], send_sems, recv_sems, t, (px, py, c))
            for t in range(len(arrs) // 2) for j, (px, py) in enumerate(chips) for p in range(arrs[2 * t].shape[0])]


def _landed(arrs, send_sems, recv_sems):
    x, y, c = _place()
    return [_remote(arrs[2 * t + 1], arrs[2 * t + 1], send_sems, recv_sems, t, (x, y, 1 - c))
            for t in range(len(arrs) // 2)]


def _pair_sum(grad, landed, core, name):
    p, _, _, sz, c = grad.shape

    def body(core_ref, g_ref, l_ref, o_ref):
        o_ref[...] = (g_ref[...].astype(F32) + l_ref[...].astype(F32)).astype(o_ref.dtype)

    out = pl.pallas_call(
        body, name=name,
        grid_spec=pltpu.PrefetchScalarGridSpec(
            num_scalar_prefetch=1, grid=(p * N_CHIPS,),
            in_specs=[pl.BlockSpec((None, None, sz, c), lambda i, cr: (i, cr[0], 0, 0)),
                      pl.BlockSpec((None, sz, c), lambda i, cr: (i, 0, 0))],
            out_specs=pl.BlockSpec((None, sz, c), lambda i, cr: (i, 0, 0))),
        out_shape=jax.ShapeDtypeStruct((p * N_CHIPS, sz, c), grad.dtype),
        compiler_params=_cparams(1),
    )(core, grad.reshape(p * N_CHIPS, 2, sz, c), landed.reshape(p * N_CHIPS, sz, c))
    return out.reshape(p, N_CHIPS, sz, c)


def _chip_sum(parts, landed, chip, name):
    p, _, sz, c = parts.shape

    def body(chip_ref, a_ref, l_ref, o_ref):
        acc = a_ref[...].astype(F32)
        for j in range(3):
            acc = acc + l_ref[j].astype(F32)
        o_ref[...] = acc

    return pl.pallas_call(
        body, name=name,
        grid_spec=pltpu.PrefetchScalarGridSpec(
            num_scalar_prefetch=1, grid=(p,),
            in_specs=[pl.BlockSpec((None, None, sz, c), lambda i, cr: (i, cr[0], 0, 0)),
                      pl.BlockSpec((3, None, sz, c), lambda i, cr: (0, i, 0, 0))],
            out_specs=pl.BlockSpec((None, sz, c), lambda i, cr: (i, 0, 0))),
        out_shape=jax.ShapeDtypeStruct((p, sz, c), F32),
        compiler_params=_cparams(1),
    )(chip, parts, landed)


def _sum_lead(a, name, out_dtype=F32):
    k = a.shape[0]
    rest = a.shape[1:]
    r, c = rest[-2], rest[-1]
    lead = math.prod(rest[:-2])
    a3 = a.reshape(k, lead * r, c)
    rows = lead * r
    tb = rows
    for cand in (512, 256, 128, 64, 32, 16, 8):
        if rows % cand == 0 and rows > cand:
            tb = cand
            break

    def body(a_ref, o_ref):
        acc = a_ref[0].astype(F32)
        for i in range(1, k):
            acc = acc + a_ref[i].astype(F32)
        o_ref[...] = acc.astype(out_dtype)

    out = pl.pallas_call(
        body, name=name, grid=(rows // tb,),
        out_shape=jax.ShapeDtypeStruct((rows, c), out_dtype),
        in_specs=[pl.BlockSpec((k, tb, c), lambda i: (0, i, 0))],
        out_specs=pl.BlockSpec((tb, c), lambda i: (i, 0)),
        compiler_params=_cparams(1),
    )(a3)
    return out.reshape(rest)


def _band(t, w, offset, valid):
    r = lax.broadcasted_iota(jnp.int32, (t, t + HALO), 0)
    col = lax.broadcasted_iota(jnp.int32, (t, t + HALO), 1)
    diff = offset(r, col)
    return jnp.where((diff >= 0) & (diff < w) & valid(col), 1.0, 0.0).astype(BF16)


def _split_dot(band, v):
    hi = v.astype(BF16)
    lo = (v - hi.astype(F32)).astype(BF16)
    return _nn(band, hi) + _nn(band, lo)


def _mix_fwd(x, g, wp, b, sc, name, deps=()):
    s = x.shape[0]
    t = min(256, s)
    rb = t // HALO

    def body(x_ref, xh_ref, g_ref, wp_ref, b_ref, sc_ref, xo_ref, d_ref):
        i = pl.program_id(0)
        gg = g_ref[...]
        h, _ = _rms(x_ref[...], gg, D_MODEL)
        hh, _ = _rms(xh_ref[...], gg, D_MODEL)
        hext = jnp.concatenate([hh, h], axis=0)
        tok = i * t + lax.broadcasted_iota(jnp.int32, (t, 1), 0)
        for gi, w in enumerate(POOL_WINDOWS):
            sl = slice(gi * GROUP_DIM, (gi + 1) * GROUP_DIM)
            band = _band(t, w, lambda r, col: r + HALO - col, lambda col: (col >= HALO) | (i > 0))
            win = _split_dot(band, hext[:, sl])
            cnt = jnp.minimum(tok + 1, w).astype(F32)
            dbf = (win / cnt - h[:, sl]).astype(BF16)
            d_ref[:, sl] = dbf
            ypre = _nn(dbf, wp_ref[gi]) + b_ref[:, sl]
            xo_ref[:, sl] = x_ref[:, sl] + ypre * sc_ref[:, sl]

    return _pcall(
        body, (x, x, g, wp, b, sc), deps, name=name, grid=(s // t,),
        out_shape=[jax.ShapeDtypeStruct((s, D_MODEL), F32), jax.ShapeDtypeStruct((s, D_MODEL), BF16)],
        in_specs=[_rows(t, D_MODEL),
                  pl.BlockSpec((HALO, D_MODEL), lambda i: (jnp.maximum(i * rb - 1, 0), 0)),
                  _full((1, D_MODEL)), _full((4, GROUP_DIM, GROUP_DIM)), _full((1, D_MODEL)), _full((1, D_MODEL))],
        out_specs=[_rows(t, D_MODEL), _rows(t, D_MODEL)],
        compiler_params=_cparams(1, VMEM_MID),
    )


def _mix_bwd(x, dy, d, g, wp, b, sc, name, deps=()):
    s = x.shape[0]
    t = min(256, s)
    rb = t // HALO
    nb = s // t
    last_halo = s // HALO - 1

    def body(x_ref, dy_ref, dyn_ref, d_ref, g_ref, wp_ref, b_ref, sc_ref,
             dx_ref, dyp_ref, dsc_ref, db_ref, dln_ref):
        i = pl.program_id(0)
        x = x_ref[...]
        gg = g_ref[...]
        dy = dy_ref[...]
        sc = sc_ref[...]
        dyp32 = dy * sc
        dyp = dyp32.astype(BF16)
        dyph = (dyn_ref[...] * sc).astype(BF16)
        dyp_ref[...] = dyp
        tok = i * t + lax.broadcasted_iota(jnp.int32, (t + HALO, 1), 0)
        dh, dsc = [], []
        for gi, w in enumerate(POOL_WINDOWS):
            sl = slice(gi * GROUP_DIM, (gi + 1) * GROUP_DIM)
            ypre = _nn(d_ref[:, sl], wp_ref[gi]) + b_ref[:, sl]
            dsc.append(jnp.sum(dy[:, sl] * ypre, axis=0, keepdims=True))
            dd = _nt(dyp[:, sl], wp_ref[gi])
            ddh = _nt(dyph[:, sl], wp_ref[gi])
            cnt = jnp.minimum(tok + 1, w).astype(F32)
            ddext = jnp.concatenate([dd, ddh], axis=0) / cnt
            band = _band(t, w, lambda r, col: col - r, lambda col: (col < t) | (i < nb - 1))
            dh.append(_split_dot(band, ddext) - dd)
        dh = jnp.concatenate(dh, axis=1)
        _, r = _rms(x, gg, D_MODEL)
        dxn, dg = _rms_bwd(x, r, gg, dh, D_MODEL)
        dx_ref[...] = dy + dxn

        @pl.when(i == 0)
        def _():
            dsc_ref[...] = jnp.zeros_like(dsc_ref)
            db_ref[...] = jnp.zeros_like(db_ref)
            dln_ref[...] = jnp.zeros_like(dln_ref)

        dsc_ref[...] += jnp.concatenate(dsc, axis=1)
        db_ref[...] += jnp.sum(dyp32, axis=0, keepdims=True)
        dln_ref[...] += dg

    vec = jax.ShapeDtypeStruct((1, D_MODEL), F32)
    return _pcall(
        body, (x, dy, dy, d, g, wp, b, sc), deps, name=name, grid=(nb,),
        out_shape=[jax.ShapeDtypeStruct((s, D_MODEL), F32), jax.ShapeDtypeStruct((s, D_MODEL), BF16), vec, vec, vec],
        in_specs=[_rows(t, D_MODEL), _rows(t, D_MODEL),
                  pl.BlockSpec((HALO, D_MODEL), lambda i: (jnp.minimum((i + 1) * rb, last_halo), 0)),
                  _rows(t, D_MODEL),
                  _full((1, D_MODEL)), _full((4, GROUP_DIM, GROUP_DIM)), _full((1, D_MODEL)), _full((1, D_MODEL))],
        out_specs=[_rows(t, D_MODEL), _rows(t, D_MODEL), _full((1, D_MODEL)), _full((1, D_MODEL)), _full((1, D_MODEL))],
        compiler_params=_cparams(1, VMEM_MID),
    )


def _load_weights(w_hbm, w_vmem, sem):
    @pl.when(pl.program_id(0) == 0)
    def _():
        cp = pltpu.make_async_copy(w_hbm, w_vmem, sem)
        cp.start()
        cp.wait()


def _ffn_fwd(x, g, w, name):
    s = x.shape[0]
    t = min(512, s)

    def body(x_ref, g_ref, w_hbm, xo_ref, gate_ref, up_ref, w_ref, sem):
        _load_weights(w_hbm, w_ref, sem)
        x = x_ref[...]
        hn = _rms(x, g_ref[...], D_MODEL)[0].astype(BF16)
        acc = x
        for c in range(2):
            rs = slice(c * FF_HALF, (c + 1) * FF_HALF)
            gt = _nt(hn, w_ref[0, rs, :])
            up = _nt(hn, w_ref[1, rs, :])
            gate_ref[:, rs] = gt.astype(BF16)
            up_ref[:, rs] = up.astype(BF16)
            act = ((gt * _sigmoid(gt)) * up).astype(BF16)
            acc = acc + _nn(act, w_ref[2, rs, :])
        xo_ref[...] = acc

    hid = jax.ShapeDtypeStruct((s, D_FF), BF16)
    return pl.pallas_call(
        body, name=name, grid=(s // t,),
        out_shape=[jax.ShapeDtypeStruct((s, D_MODEL), F32), hid, hid],
        in_specs=[_rows(t, D_MODEL), _full((1, D_MODEL)), ANY],
        out_specs=[_rows(t, D_MODEL), _rows(t, D_FF), _rows(t, D_FF)],
        scratch_shapes=[pltpu.VMEM((3, D_FF, D_MODEL), BF16), pltpu.SemaphoreType.DMA],
        compiler_params=_cparams(1, VMEM_BIG),
    )(x, g, w)


def _ffn_bwd_hidden(dy, gate, up, w, name, deps=()):
    s = dy.shape[0]
    t = min(512, s)

    def body(dy_ref, gate_ref, up_ref, wd_ref, act_ref, dg_ref, du_ref):
        dyb = dy_ref[...].astype(BF16)
        gt = gate_ref[...].astype(F32)
        u = up_ref[...].astype(F32)
        sg = _sigmoid(gt)
        sl = gt * sg
        act_ref[...] = (sl * u).astype(BF16)
        dact = _nt(dyb, wd_ref[...])
        dg_ref[...] = (dact * u * (sg * (1.0 + gt * (1.0 - sg)))).astype(BF16)
        du_ref[...] = (dact * sl).astype(BF16)

    hid = jax.ShapeDtypeStruct((s, D_FF), BF16)
    half = pl.BlockSpec((t, FF_HALF), lambda c, i: (i, c))
    tok = pl.BlockSpec((t, D_MODEL), lambda c, i: (i, 0))
    return _pcall(
        body, (dy, gate, up, w), deps, name=name, grid=(2, s // t),
        out_shape=[hid, hid, hid],
        in_specs=[tok, half, half, pl.BlockSpec((None, FF_HALF, D_MODEL), lambda c, i: (2, c, 0))],
        out_specs=[half, half, half],
        compiler_params=_cparams(2, VMEM_BIG),
    )


def _ffn_bwd_input(x, dy, dg, du, g, w, name):
    s = x.shape[0]
    t = min(512, s)

    def body(x_ref, dy_ref, dg_ref, du_ref, g_ref, wg_ref, wu_ref, dx_ref, hn_ref, dyb_ref, dln_ref, dh_sc):
        i, c = pl.program_id(0), pl.program_id(1)
        part = _nn(dg_ref[...], wg_ref[...]) + _nn(du_ref[...], wu_ref[...])

        @pl.when(c == 0)
        def _():
            dh_sc[...] = part

        @pl.when(c == 1)
        def _():
            x = x_ref[...]
            gg = g_ref[...]
            y, r = _rms(x, gg, D_MODEL)
            hn_ref[...] = y.astype(BF16)
            dxn, dgl = _rms_bwd(x, r, gg, dh_sc[...] + part, D_MODEL)
            dy = dy_ref[...]
            dyb_ref[...] = dy.astype(BF16)
            dx_ref[...] = dy + dxn

            @pl.when(i == 0)
            def _():
                dln_ref[...] = jnp.zeros_like(dln_ref)

            dln_ref[...] += dgl

    half = pl.BlockSpec((t, FF_HALF), lambda i, c: (i, c))
    tok = pl.BlockSpec((t, D_MODEL), lambda i, c: (i, 0))
    vec = pl.BlockSpec((1, D_MODEL), lambda i, c: (0, 0))
    return pl.pallas_call(
        body, name=name, grid=(s // t, 2),
        out_shape=[jax.ShapeDtypeStruct((s, D_MODEL), F32), jax.ShapeDtypeStruct((s, D_MODEL), BF16),
                   jax.ShapeDtypeStruct((s, D_MODEL), BF16), jax.ShapeDtypeStruct((1, D_MODEL), F32)],
        in_specs=[tok, tok, half, half, vec,
                  pl.BlockSpec((None, FF_HALF, D_MODEL), lambda i, c: (0, c, 0)),
                  pl.BlockSpec((None, FF_HALF, D_MODEL), lambda i, c: (1, c, 0))],
        out_specs=[tok, tok, tok, vec],
        scratch_shapes=[pltpu.VMEM((t, D_MODEL), F32)],
        compiler_params=_cparams(2, VMEM_BIG),
    )(x, dy, dg, du, g, w, w)


def _tn_matmul(a, b, into, p0, name, groups=1, m_chunk=None, deps=()):
    s = a.shape[0]
    m, n = a.shape[1] // groups, b.shape[1] // groups
    assert into.shape[1:] == (m, n)
    mc = m if m_chunk is None else m_chunk
    nm = m // mc
    t = min(1024, s)
    nt = s // t

    def body(a_ref, b_ref, into_ref, o_ref, acc):
        ti = pl.program_id(2)

        @pl.when(ti == 0)
        def _():
            acc[...] = jnp.zeros_like(acc)

        acc[...] += _tn(a_ref[...], b_ref[...])

        @pl.when(ti == nt - 1)
        def _():
            o_ref[...] = acc[...].astype(o_ref.dtype)

    return _pcall(
        body, (a, b, into), deps, name=name, grid=(groups, nm, nt),
        out_shape=jax.ShapeDtypeStruct(into.shape, into.dtype),
        in_specs=[pl.BlockSpec((t, mc), lambda gi, mi, ti: (ti, gi * nm + mi)),
                  pl.BlockSpec((t, n), lambda gi, mi, ti: (ti, gi)), ANY],
        out_specs=pl.BlockSpec((None, mc, n), lambda gi, mi, ti: (p0 + gi, mi, 0)),
        scratch_shapes=[pltpu.VMEM((mc, n), F32)],
        input_output_aliases={2: 0},
        compiler_params=_cparams(3, VMEM_MID),
    )


def _rope_tables(positions):
    half = ROPE // 2
    inv = ROPE_THETA ** (-jnp.arange(half, dtype=F32) * 2.0 / ROPE)
    ang = positions.astype(F32)[:, None] * inv
    cos, sin = jnp.cos(ang), jnp.sin(ang)
    zero = jnp.zeros((positions.shape[0], LANES - ROPE), F32)
    return jnp.concatenate([cos, cos, zero], axis=1), jnp.concatenate([-sin, sin, zero], axis=1)


def _kv_specs(t):
    return [_full((1, D_MODEL)), _full((D_MODEL, KV_RANK)), _full((D_MODEL, LANES)), _full((1, KV_RANK)),
            _full((N_HEADS, KV_RANK, NOPE)), _full((N_HEADS, KV_RANK, V_DIM)),
            _full((1, NOPE)), _full((1, LANES)), _rows(t, LANES), _rows(t, LANES)]


def _kv_fwd(x, ln, wc, wpe, gl, wuk, wuv, gkn, gkr, cos, sin, name, deps=()):
    s = x.shape[0]
    t = min(256, s)

    def body(x_ref, ln_ref, wc_ref, wpe_ref, gl_ref, wuk_ref, wuv_ref, gkn_ref, gkr_ref, cos_ref, sin_ref,
             k_ref, v_ref):
        hn = _rms(x_ref[...], ln_ref[...], D_MODEL)[0].astype(BF16)
        clat = _nn(hn, wc_ref[...])
        kpe = _nn(hn, wpe_ref[...])
        cn = _rms(clat, gl_ref[...], KV_RANK)[0].astype(BF16)
        sspe = jnp.sum(kpe * kpe, axis=-1, keepdims=True)
        cs, sn = cos_ref[...], sin_ref[...]
        for h in range(N_HEADS):
            kn = _nn(cn, wuk_ref[h])
            r = lax.rsqrt((jnp.sum(kn * kn, axis=-1, keepdims=True) + sspe) * (1.0 / QK_DIM) + EPS)
            k_ref[:, h * QK_PAD:h * QK_PAD + NOPE] = ((kn * r) * gkn_ref[...]).astype(BF16)
            z = (kpe * r) * gkr_ref[...]
            k_ref[:, h * QK_PAD + NOPE:(h + 1) * QK_PAD] = (z * cs + _swap_halves(z) * sn).astype(BF16)
            v_ref[:, h * V_DIM:(h + 1) * V_DIM] = _nn(cn, wuv_ref[h]).astype(BF16)

    return _pcall(
        body, (x, ln, wc, wpe, gl, wuk, wuv, gkn, gkr, cos, sin), deps, name=name, grid=(s // t,),
        out_shape=[jax.ShapeDtypeStruct((s, N_HEADS * QK_PAD), BF16), jax.ShapeDtypeStruct((s, N_HEADS * V_DIM), BF16)],
        in_specs=[_rows(t, D_MODEL)] + _kv_specs(t),
        out_specs=[_rows(t, N_HEADS * QK_PAD), _rows(t, N_HEADS * V_DIM)],
        compiler_params=_cparams(1, VMEM_MID),
    )


def _kv_bwd(x, dxin, dks, dvs, ln, wc, wpe, gl, wuk, wuv, gkn, gkr, cos, sin, name):
    s = x.shape[0]
    t = min(256, s)
    nk = len(dks)

    def body(*refs):
        x_ref, dxin_ref = refs[:2]
        dk_refs = refs[2:2 + nk]
        dv_refs = refs[2 + nk:2 + 2 * nk]
        (ln_ref, wc_ref, wpe_ref, gl_ref, wuk_ref, wuv_ref, gkn_ref, gkr_ref, cos_ref, sin_ref,
         dx_ref, hn_ref, cn_ref, dkn_ref, dvb_ref, dcc_ref, dpe_ref,
         dln_ref, dgl_ref, dgkn_ref, dgkr_ref) = refs[2 + 2 * nk:]
        x = x_ref[...]
        ln = ln_ref[...]
        y, rx = _rms(x, ln, D_MODEL)
        hn = y.astype(BF16)
        hn_ref[...] = hn
        clat = _nn(hn, wc_ref[...])
        kpe = _nn(hn, wpe_ref[...])
        gl = gl_ref[...]
        cy, rc = _rms(clat, gl, KV_RANK)
        cn = cy.astype(BF16)
        cn_ref[...] = cn
        sspe = jnp.sum(kpe * kpe, axis=-1, keepdims=True)
        cs, sn = cos_ref[...], sin_ref[...]
        gkn, gkr = gkn_ref[...], gkr_ref[...]
        dc = jnp.zeros((t, KV_RANK), F32)
        dkpe = jnp.zeros((t, LANES), F32)
        dgkn = jnp.zeros((1, NOPE), F32)
        dgkr = jnp.zeros((1, LANES), F32)
        for h in range(N_HEADS):
            kn = _nn(cn, wuk_ref[h])
            r = lax.rsqrt((jnp.sum(kn * kn, axis=-1, keepdims=True) + sspe) * (1.0 / QK_DIM) + EPS)
            lo, mid, hi = h * QK_PAD, h * QK_PAD + NOPE, (h + 1) * QK_PAD
            dko = dk_refs[0][:, lo:mid]
            dkr = dk_refs[0][:, mid:hi]
            dvh = dv_refs[0][:, h * V_DIM:(h + 1) * V_DIM]
            for j in range(1, nk):
                dko = dko + dk_refs[j][:, lo:mid]
                dkr = dkr + dk_refs[j][:, mid:hi]
                dvh = dvh + dv_refs[j][:, h * V_DIM:(h + 1) * V_DIM]
            dz = dkr * cs - _swap_halves(dkr) * sn
            un = dko * gkn
            ur = dz * gkr
            sm = (jnp.sum(kn * un, axis=-1, keepdims=True) + jnp.sum(kpe * ur, axis=-1, keepdims=True)) * (1.0 / QK_DIM)
            coef = r * r * r * sm
            dkn = (r * un - kn * coef).astype(BF16)
            dkpe = dkpe + (r * ur - kpe * coef)
            dgkn = dgkn + jnp.sum(dko * (kn * r), axis=0, keepdims=True)
            dgkr = dgkr + jnp.sum(dz * (kpe * r), axis=0, keepdims=True)
            dkn_ref[:, h * NOPE:(h + 1) * NOPE] = dkn
            dvb = dvh.astype(BF16)
            dvb_ref[:, h * V_DIM:(h + 1) * V_DIM] = dvb
            dc = dc + _nt(dkn, wuk_ref[h]) + _nt(dvb, wuv_ref[h])
        dclat, dgl = _rms_bwd(clat, rc, gl, dc, KV_RANK)
        dcc = dclat.astype(BF16)
        dpe = dkpe.astype(BF16)
        dcc_ref[...] = dcc
        dpe_ref[...] = dpe
        dhn = _nt(dcc, wc_ref[...]) + _nt(dpe, wpe_ref[...])
        dxn, dln = _rms_bwd(x, rx, ln, dhn, D_MODEL)
        dx_ref[...] = dxin_ref[...] + dxn

        @pl.when(pl.program_id(0) == 0)
        def _():
            dln_ref[...] = jnp.zeros_like(dln_ref)
            dgl_ref[...] = jnp.zeros_like(dgl_ref)
            dgkn_ref[...] = jnp.zeros_like(dgkn_ref)
            dgkr_ref[...] = jnp.zeros_like(dgkr_ref)

        dln_ref[...] += dln
        dgl_ref[...] += dgl
        dgkn_ref[...] += dgkn
        dgkr_ref[...] += dgkr

    def tok(cols, dt):
        return jax.ShapeDtypeStruct((s, cols), dt)

    def vec(cols):
        return jax.ShapeDtypeStruct((1, cols), F32)

    return pl.pallas_call(
        body, name=name, grid=(s // t,),
        out_shape=[tok(D_MODEL, F32), tok(D_MODEL, BF16), tok(KV_RANK, BF16), tok(N_HEADS * NOPE, BF16),
                   tok(N_HEADS * V_DIM, BF16), tok(KV_RANK, BF16), tok(LANES, BF16),
                   vec(D_MODEL), vec(KV_RANK), vec(NOPE), vec(LANES)],
        in_specs=[_rows(t, D_MODEL), _rows(t, D_MODEL)] + [_rows(t, N_HEADS * QK_PAD)] * nk
                 + [_rows(t, N_HEADS * V_DIM)] * nk + _kv_specs(t),
        out_specs=[_rows(t, D_MODEL), _rows(t, D_MODEL), _rows(t, KV_RANK), _rows(t, N_HEADS * NOPE),
                   _rows(t, N_HEADS * V_DIM), _rows(t, KV_RANK), _rows(t, LANES),
                   _full((1, D_MODEL)), _full((1, KV_RANK)), _full((1, NOPE)), _full((1, LANES))],
        compiler_params=_cparams(1, VMEM_BIG),
    )(x, dxin, *dks, *dvs, ln, wc, wpe, gl, wuk, wuv, gkn, gkr, cos, sin)


def _q_specs(t):
    return [_full((1, D_MODEL)), _full((D_MODEL, Q_RANK)), _full((1, Q_RANK)), _full((N_HEADS, Q_RANK, QK_PAD)),
            _full((1, NOPE)), _full((1, LANES)), _rows(t, LANES), _rows(t, LANES)]


def _q_fwd(x, ln, wdq, gql, wuq, gqn, gqr, cos, sin, name, deps=()):
    s = x.shape[0]
    t = min(256, s)

    def body(x_ref, ln_ref, wdq_ref, gql_ref, wuq_ref, gqn_ref, gqr_ref, cos_ref, sin_ref, q_ref):
        hn = _rms(x_ref[...], ln_ref[...], D_MODEL)[0].astype(BF16)
        cqn = _rms(_nn(hn, wdq_ref[...]), gql_ref[...], Q_RANK)[0].astype(BF16)
        cs, sn = cos_ref[...], sin_ref[...]
        for h in range(N_HEADS):
            qa = _nn(cqn, wuq_ref[h])
            r = lax.rsqrt(jnp.sum(qa * qa, axis=-1, keepdims=True) * (1.0 / QK_DIM) + EPS)
            q_ref[:, h * QK_PAD:h * QK_PAD + NOPE] = ((qa[:, :NOPE] * r) * gqn_ref[...]).astype(BF16)
            z = (qa[:, NOPE:] * r) * gqr_ref[...]
            q_ref[:, h * QK_PAD + NOPE:(h + 1) * QK_PAD] = (z * cs + _swap_halves(z) * sn).astype(BF16)

    return _pcall(
        body, (x, ln, wdq, gql, wuq, gqn, gqr, cos, sin), deps, name=name, grid=(s // t,),
        out_shape=jax.ShapeDtypeStruct((s, N_HEADS * QK_PAD), BF16),
        in_specs=[_rows(t, D_MODEL)] + _q_specs(t),
        out_specs=_rows(t, N_HEADS * QK_PAD),
        compiler_params=_cparams(1, VMEM_MID),
    )


def _q_bwd(x, dxin, dq, ln, wdq, gql, wuq, gqn, gqr, cos, sin, name):
    s = x.shape[0]
    t = min(256, s)

    def body(x_ref, dxin_ref, dq_ref, ln_ref, wdq_ref, gql_ref, wuq_ref, gqn_ref, gqr_ref, cos_ref, sin_ref,
             dx_ref, hn_ref, cqn_ref, dqa_ref, dcq_ref, dln_ref, dgql_ref, dgqn_ref, dgqr_ref):
        x = x_ref[...]
        ln = ln_ref[...]
        y, rx = _rms(x, ln, D_MODEL)
        hn = y.astype(BF16)
        hn_ref[...] = hn
        cqp = _nn(hn, wdq_ref[...])
        gql = gql_ref[...]
        cy, rc = _rms(cqp, gql, Q_RANK)
        cqn = cy.astype(BF16)
        cqn_ref[...] = cqn
        cs, sn = cos_ref[...], sin_ref[...]
        gqn, gqr = gqn_ref[...], gqr_ref[...]
        dcq = jnp.zeros((t, Q_RANK), F32)
        dgqn = jnp.zeros((1, NOPE), F32)
        dgqr = jnp.zeros((1, LANES), F32)
        for h in range(N_HEADS):
            qa = _nn(cqn, wuq_ref[h])
            qn, qr = qa[:, :NOPE], qa[:, NOPE:]
            r = lax.rsqrt(jnp.sum(qa * qa, axis=-1, keepdims=True) * (1.0 / QK_DIM) + EPS)
            dqo = dq_ref[:, h * QK_PAD:h * QK_PAD + NOPE]
            dqr = dq_ref[:, h * QK_PAD + NOPE:(h + 1) * QK_PAD]
            dz = dqr * cs - _swap_halves(dqr) * sn
            un = dqo * gqn
            ur = dz * gqr
            sm = (jnp.sum(qn * un, axis=-1, keepdims=True) + jnp.sum(qr * ur, axis=-1, keepdims=True)) * (1.0 / QK_DIM)
            coef = r * r * r * sm
            dqa = jnp.concatenate([r * un - qn * coef, r * ur - qr * coef], axis=1).astype(BF16)
            dgqn = dgqn + jnp.sum(dqo * (qn * r), axis=0, keepdims=True)
            dgqr = dgqr + jnp.sum(dz * (qr * r), axis=0, keepdims=True)
            dqa_ref[:, h * QK_PAD:(h + 1) * QK_PAD] = dqa
            dcq = dcq + _nt(dqa, wuq_ref[h])
        dcqp, dgql = _rms_bwd(cqp, rc, gql, dcq, Q_RANK)
        dcqb = dcqp.astype(BF16)
        dcq_ref[...] = dcqb
        dhn = _nt(dcqb, wdq_ref[...])
        dxn, dln = _rms_bwd(x, rx, ln, dhn, D_MODEL)
        dx_ref[...] = dxin_ref[...] + dxn

        @pl.when(pl.program_id(0) == 0)
        def _():
            dln_ref[...] = jnp.zeros_like(dln_ref)
            dgql_ref[...] = jnp.zeros_like(dgql_ref)
            dgqn_ref[...] = jnp.zeros_like(dgqn_ref)
            dgqr_ref[...] = jnp.zeros_like(dgqr_ref)

        dln_ref[...] += dln
        dgql_ref[...] += dgql
        dgqn_ref[...] += dgqn
        dgqr_ref[...] += dgqr

    def tok(cols, dt):
        return jax.ShapeDtypeStruct((s, cols), dt)

    def vec(cols):
        return jax.ShapeDtypeStruct((1, cols), F32)

    return pl.pallas_call(
        body, name=name, grid=(s // t,),
        out_shape=[tok(D_MODEL, F32), tok(D_MODEL, BF16), tok(Q_RANK, BF16), tok(N_HEADS * QK_PAD, BF16),
                   tok(Q_RANK, BF16), vec(D_MODEL), vec(Q_RANK), vec(NOPE), vec(LANES)],
        in_specs=[_rows(t, D_MODEL), _rows(t, D_MODEL), _rows(t, N_HEADS * QK_PAD)] + _q_specs(t),
        out_specs=[_rows(t, D_MODEL), _rows(t, D_MODEL), _rows(t, Q_RANK), _rows(t, N_HEADS * QK_PAD),
                   _rows(t, Q_RANK), _full((1, D_MODEL)), _full((1, Q_RANK)), _full((1, NOPE)), _full((1, LANES))],
        compiler_params=_cparams(1, VMEM_MID),
    )(x, dxin, dq, ln, wdq, gql, wuq, gqn, gqr, cos, sin)


SM_SCALE = 1.0 / math.sqrt(QK_DIM)
LOG2_E = math.log2(math.e)
EXP2_SCALE = SM_SCALE * LOG2_E
NEG = -1e30


def _diag_mask(t):
    qpos = lax.broadcasted_iota(jnp.int32, (t, t), 0)
    kpos = lax.broadcasted_iota(jnp.int32, (t, t), 1)
    return lax.shift_right_logical(kpos, 6) <= lax.shift_right_logical(qpos, 6)


def _att_fwd(q, k, v, name):
    s = q.shape[0]
    t = min(512, s)
    nb = s // t

    def body(q_ref, k_ref, v_ref, o_ref, lse_ref):
        qi = pl.program_id(1)
        qq = q_ref[...]

        def block(ki, carry, masked):
            m_old, l_old, acc = carry
            rows = pl.ds(pl.multiple_of(ki * t, t), t)
            sc = _nt(qq, k_ref[rows, :])
            if masked:
                sc = jnp.where(_diag_mask(t), sc, NEG)
            m_new = jnp.maximum(m_old, jnp.max(sc, axis=-1, keepdims=True))
            p = jnp.exp2((sc - m_new) * EXP2_SCALE)
            alpha = jnp.exp2((m_old - m_new) * EXP2_SCALE)
            l_new = alpha * l_old + jnp.sum(p, axis=-1, keepdims=True)
            acc = alpha * acc + _nn(p.astype(BF16), v_ref[rows, :])
            return m_new, l_new, acc

        init = (jnp.full((t, 1), NEG, F32), jnp.zeros((t, 1), F32), jnp.zeros((t, V_DIM), F32))
        carry = lax.fori_loop(0, qi, lambda ki, c: block(ki, c, False), init)
        m_fin, l_fin, acc = block(qi, carry, True)
        o_ref[...] = (acc / l_fin).astype(BF16)
        lse_ref[...] = jnp.broadcast_to(m_fin * SM_SCALE + jnp.log(l_fin), (t, LANES))

    return pl.pallas_call(
        body, name=name, grid=(N_HEADS, nb),
        out_shape=[jax.ShapeDtypeStruct((s, N_HEADS * V_DIM), BF16), jax.ShapeDtypeStruct((s, N_HEADS * LANES), F32)],
        in_specs=[pl.BlockSpec((t, QK_PAD), lambda h, qi: (qi, h)),
                  pl.BlockSpec((s, QK_PAD), lambda h, qi: (0, h)),
                  pl.BlockSpec((s, V_DIM), lambda h, qi: (0, h))],
        out_specs=[pl.BlockSpec((t, V_DIM), lambda h, qi: (qi, h)),
                   pl.BlockSpec((t, LANES), lambda h, qi: (qi, h))],
        compiler_params=_cparams(2, VMEM_MID),
    )(q, k, v)


def _att_bwd(q, k, v, do, o, lse, name, deps=()):
    s = q.shape[0]
    t = min(512, s)
    nb = s // t

    def body(q_ref, k_ref, v_ref, do_ref, o_ref, lse_ref, dq_ref, dk_ref, dv_ref):
        ki = pl.program_id(1)
        kk, vv = k_ref[...], v_ref[...]

        @pl.when(ki == 0)
        def _():
            dq_ref[...] = jnp.zeros_like(dq_ref)

        def block(qi, carry, masked):
            dk, dv = carry
            rows = pl.ds(pl.multiple_of(qi * t, t), t)
            qq, dob = q_ref[rows, :], do_ref[rows, :]
            sc = _nt(qq, kk)
            if masked:
                sc = jnp.where(_diag_mask(t), sc, NEG)
            p = jnp.exp2(sc * EXP2_SCALE - lse_ref[rows, :][:, :1] * LOG2_E)
            dp = _nt(dob, vv)
            dsum = jnp.sum(dob.astype(F32) * o_ref[rows, :].astype(F32), axis=-1, keepdims=True)
            ds = (p * (dp - dsum)).astype(BF16)
            dq_ref[rows, :] += _nn(ds, kk)
            return dk + _tn(ds, qq), dv + _tn(p.astype(BF16), dob)

        carry = block(ki, (jnp.zeros((t, QK_PAD), F32), jnp.zeros((t, V_DIM), F32)), True)
        dk, dv = lax.fori_loop(ki + 1, nb, lambda qi, c: block(qi, c, False), carry)
        dk_ref[...] = dk * SM_SCALE
        dv_ref[...] = dv

        @pl.when(ki == nb - 1)
        def _():
            dq_ref[...] = dq_ref[...] * SM_SCALE

    def head(h, ki):
        return (0, h)

    def kblock(h, ki):
        return (ki, h)

    return _pcall(
        body, (q, k, v, do, o, lse), deps, name=name, grid=(N_HEADS, nb),
        out_shape=[jax.ShapeDtypeStruct((s, N_HEADS * QK_PAD), F32), jax.ShapeDtypeStruct((s, N_HEADS * QK_PAD), F32),
                   jax.ShapeDtypeStruct((s, N_HEADS * V_DIM), F32)],
        in_specs=[pl.BlockSpec((s, QK_PAD), head), pl.BlockSpec((t, QK_PAD), kblock), pl.BlockSpec((t, V_DIM), kblock),
                  pl.BlockSpec((s, V_DIM), head), pl.BlockSpec((s, V_DIM), head), pl.BlockSpec((s, LANES), head)],
        out_specs=[pl.BlockSpec((s, QK_PAD), head), pl.BlockSpec((t, QK_PAD), kblock), pl.BlockSpec((t, V_DIM), kblock)],
        compiler_params=_cparams(2, VMEM_MID),
    )


def _o_fwd(x, o, wo, name):
    s = x.shape[0]
    t = min(512, s)

    def body(x_ref, o_ref, wo_ref, xo_ref):
        xo_ref[...] = x_ref[...] + _nn(o_ref[...], wo_ref[...])

    return pl.pallas_call(
        body, name=name, grid=(s // t,),
        out_shape=jax.ShapeDtypeStruct((s, D_MODEL), F32),
        in_specs=[_rows(t, D_MODEL), _rows(t, D_MODEL), _full((D_MODEL, D_MODEL))],
        out_specs=_rows(t, D_MODEL),
        compiler_params=_cparams(1, VMEM_MID),
    )(x, o, wo)


def _o_bwd(dx, wo, name, deps=()):
    s = dx.shape[0]
    t = min(512, s)

    def body(dx_ref, wo_ref, do_ref, dxb_ref):
        dxb = dx_ref[...].astype(BF16)
        dxb_ref[...] = dxb
        do_ref[...] = _nt(dxb, wo_ref[...]).astype(BF16)

    tok = jax.ShapeDtypeStruct((s, D_MODEL), BF16)
    return _pcall(
        body, (dx, wo), deps, name=name, grid=(s // t,),
        out_shape=[tok, tok],
        in_specs=[_rows(t, D_MODEL), _full((D_MODEL, D_MODEL))],
        out_specs=[_rows(t, D_MODEL), _rows(t, D_MODEL)],
        compiler_params=_cparams(1, VMEM_MID),
    )


def _loss_head(y, target, name):
    s = y.shape[0]
    t = min(512, s)

    def body(y_ref, t_ref, dy_ref, sq_ref):
        e = y_ref[...] - t_ref[...]
        dy_ref[...] = e * (1.0 / D_MODEL)

        @pl.when(pl.program_id(0) == 0)
        def _():
            sq_ref[...] = jnp.zeros_like(sq_ref)

        sq_ref[...] += jnp.sum(e * e, axis=0, keepdims=True)

    return pl.pallas_call(
        body, name=name, grid=(s // t,),
        out_shape=[jax.ShapeDtypeStruct((s, D_MODEL), F32), jax.ShapeDtypeStruct((1, D_MODEL), F32)],
        in_specs=[_rows(t, D_MODEL), _rows(t, D_MODEL)],
        out_specs=[_rows(t, D_MODEL), _full((1, D_MODEL))],
        compiler_params=_cparams(1),
    )(y, target)


def _adamw(w, g, m, v, name):
    shape = w.shape
    c = shape[-1]
    r = math.prod(shape[:-1])
    tb = r
    for cand in (512, 256, 128):
        if r % cand == 0 and r > cand:
            tb = cand
            break

    def body(w_ref, g_ref, m_ref, v_ref, d_ref, mo_ref, vo_ref):
        gr = g_ref[...]
        mn = ADAM_B1 * m_ref[...] + (1.0 - ADAM_B1) * gr
        vn = ADAM_B2 * v_ref[...] + (1.0 - ADAM_B2) * (gr * gr)
        m_hat = mn / (1.0 - ADAM_B1 ** ADAM_STEP)
        v_hat = vn / (1.0 - ADAM_B2 ** ADAM_STEP)
        d_ref[...] = -ADAM_LR * (m_hat / (jnp.sqrt(v_hat) + ADAM_EPS) + ADAM_WD * w_ref[...])
        mo_ref[...] = mn
        vo_ref[...] = vn

    spec = pl.BlockSpec((tb, c), lambda i: (i, 0))
    flat = jax.ShapeDtypeStruct((r, c), F32)
    outs = pl.pallas_call(
        body, name=name, grid=(r // tb,),
        out_shape=[flat, flat, flat],
        in_specs=[spec] * 4, out_specs=[spec] * 3,
        compiler_params=_cparams(1),
    )(w.reshape(r, c), g.reshape(r, c), m.reshape(r, c), v.reshape(r, c))
    return [a.reshape(shape) for a in outs]


def _pad_cols(a, width):
    return jnp.pad(a, [(0, 0)] * (a.ndim - 1) + [(0, width - a.shape[-1])])


def _owner_view(a, sz):
    return a.reshape(a.shape[0], N_CHIPS, 2, sz, a.shape[-1])


def kernel(x, positions, ln_mix_a, w_pool, b_pool, pool_scale, ln_ffn, w_gate, w_up, w_down, ln_kv, w_dkv, g_kv_latent, w_uk, w_uv, g_k, ln_mix_b, w_dq, g_q_latent, w_uq, g_q, w_o, loss_target, m_ln_mix_a, m_w_pool, m_b_pool, m_pool_scale, m_ln_ffn, m_w_gate, m_w_up, m_w_down, m_ln_kv, m_w_dkv, m_g_kv_latent, m_w_uk, m_w_uv, m_g_k, m_ln_mix_b, m_w_dq, m_g_q_latent, m_w_uq, m_g_q, m_w_o, v_ln_mix_a, v_w_pool, v_b_pool, v_pool_scale, v_ln_ffn, v_w_gate, v_w_up, v_w_down, v_ln_kv, v_w_dkv, v_g_kv_latent, v_w_uk, v_w_uv, v_g_k, v_ln_mix_b, v_w_dq, v_g_q_latent, v_w_uq, v_g_q, v_w_o):
    weights = dict(ln_mix_a=ln_mix_a, w_pool=w_pool, b_pool=b_pool, pool_scale=pool_scale, ln_ffn=ln_ffn,
                   w_gate=w_gate, w_up=w_up, w_down=w_down, ln_kv=ln_kv, w_dkv=w_dkv, g_kv_latent=g_kv_latent,
                   w_uk=w_uk, w_uv=w_uv, g_k=g_k, ln_mix_b=ln_mix_b, w_dq=w_dq, g_q_latent=g_q_latent,
                   w_uq=w_uq, g_q=g_q, w_o=w_o)
    mom1 = dict(ln_mix_a=m_ln_mix_a, w_pool=m_w_pool, b_pool=m_b_pool, pool_scale=m_pool_scale, ln_ffn=m_ln_ffn,
                w_gate=m_w_gate, w_up=m_w_up, w_down=m_w_down, ln_kv=m_ln_kv, w_dkv=m_w_dkv,
                g_kv_latent=m_g_kv_latent, w_uk=m_w_uk, w_uv=m_w_uv, g_k=m_g_k, ln_mix_b=m_ln_mix_b, w_dq=m_w_dq,
                g_q_latent=m_g_q_latent, w_uq=m_w_uq, g_q=m_g_q, w_o=m_w_o)
    mom2 = dict(ln_mix_a=v_ln_mix_a, w_pool=v_w_pool, b_pool=v_b_pool, pool_scale=v_pool_scale, ln_ffn=v_ln_ffn,
                w_gate=v_w_gate, w_up=v_w_up, w_down=v_w_down, ln_kv=v_ln_kv, w_dkv=v_w_dkv,
                g_kv_latent=v_g_kv_latent, w_uk=v_w_uk, w_uv=v_w_uv, g_k=v_g_k, ln_mix_b=v_ln_mix_b, w_dq=v_w_dq,
                g_q_latent=v_g_q_latent, w_uq=v_w_uq, g_q=v_g_q, w_o=v_w_o)
    names = list(weights)
    dev = 4 * lax.axis_index("x") + 2 * lax.axis_index("y") + lax.axis_index("c")
    core = lax.axis_index("c").astype(jnp.int32).reshape(1)
    chip = (2 * lax.axis_index("x") + lax.axis_index("y")).astype(jnp.int32).reshape(1)

    xs = x[0]
    target = loss_target[0]
    cos, sin = _rope_tables(positions[0])

    small_sh = jnp.concatenate([ln_mix_a.reshape(1, -1), pool_scale.reshape(1, -1), b_pool.reshape(1, -1)], axis=1)
    wp_g, small_g = _all_gather([w_pool.astype(BF16), small_sh], [2, 0], "gather_first")
    wp_all = wp_g.reshape(2, 4, GROUP_DIM, GROUP_DIM)
    small_g = small_g.reshape(N_DEV, 3, 2, LANES)
    ln_a_all = small_g[:, 0].transpose(1, 0, 2).reshape(2, 1, D_MODEL)
    sc_all = small_g[:, 1].transpose(1, 0, 2).reshape(2, 1, D_MODEL)
    bp_all = small_g[:, 2].reshape(N_DEV, 2, 4, 32).transpose(1, 2, 0, 3).reshape(2, 1, D_MODEL)

    def placed(shard):
        buf = lax.empty((shard.shape[0], N_DEV) + shard.shape[1:], shard.dtype)
        return lax.dynamic_update_slice(buf, shard[:, None], (0, dev, 0, 0))

    groups = {f"ffn{l}": [placed(jnp.stack([w_gate[l].T, w_up[l].T, w_down[l]]).astype(BF16))] for l in range(4)}
    groups["att"] = [placed(a.astype(BF16)) for a in (
        w_dkv[None, :, :KV_RANK], _pad_cols(w_dkv[None, :, KV_RANK:], LANES), w_uk[None], w_uv[None],
        w_dq, _pad_cols(w_uq, QK_PAD), w_o)]
    def spread_start(nm, deps):
        return _copies_start(groups[nm], len(groups[nm]), _gather_spread, f"spread_{nm}", deps=deps)

    def spread_wait(nm, state, after):
        ssem, rsem, bufs, _ = state
        return _copies_wait(bufs, ssem, rsem, after, _blocks_moved(4), f"spread_done_{nm}")

    def relay_start(nm, bufs, deps=()):
        return _copies_start(bufs, len(bufs), _gather_relay, f"relay_{nm}", deps=deps)

    def relay_wait(nm, state, after):
        ssem, rsem, bufs, _ = state
        return _copies_wait(bufs, ssem, rsem, after, _blocks_moved(3), f"relay_done_{nm}")

    gkn = g_k[:NOPE].reshape(1, NOPE)
    gkr = _pad_cols(g_k[NOPE:].reshape(1, ROPE), LANES)
    gl = g_kv_latent.reshape(1, KV_RANK)
    lnkv = ln_kv.reshape(1, D_MODEL)

    x_in, x_mid, pooled, gates, ups, w_ffn = [], [], [], [], [], []
    qs, outs, lses = [], [], []

    def mixer(l, cur, deps):
        x_in.append(cur)
        mid, dsave = _mix_fwd(cur, ln_a_all[l], wp_all[l], bp_all[l], sc_all[l], f"mix_fwd{l}", deps=deps)
        pooled.append(dsave)
        x_mid.append(mid)
        return mid

    def q_args(j):
        return (ln_mix_b[j].reshape(1, -1), wdq_all[j], g_q_latent[j].reshape(1, -1), wuq_all[j],
                g_q[j, :NOPE].reshape(1, -1), _pad_cols(g_q[j, NOPE:].reshape(1, -1), LANES), cos, sin)

    def attention(j, cur, deps):
        x_in.append(cur)
        q = _q_fwd(cur, *q_args(j), f"q_fwd{j}", deps=deps)
        o, lse = _att_fwd(q, k_sh, v_sh, f"att_fwd{j}")
        mid = _o_fwd(cur, o, wo_all[j], f"o_fwd{j}")
        qs.append(q)
        outs.append(o)
        lses.append(lse)
        x_mid.append(mid)
        return mid

    def ffn(l, mid, relayed):
        w_l = relayed[0].reshape(3, D_FF, D_MODEL)
        w_ffn.append(w_l)
        cur, gate, up = _ffn_fwd(mid, ln_ffn[l].reshape(1, -1), w_l, f"ffn_fwd{l}")
        gates.append(gate)
        ups.append(up)
        return cur

    sp0 = spread_start("ffn0", [small_g])
    mid = mixer(0, xs, [sp0[3]])
    landed0 = spread_wait("ffn0", sp0, mid)
    sp1 = spread_start("ffn1", [landed0[0]])
    rl0 = relay_start("ffn0", landed0, [sp1[3]])
    cur = ffn(0, mid, relay_wait("ffn0", rl0, rl0[3]))

    landed1 = spread_wait("ffn1", sp1, cur)
    sp_att = spread_start("att", [landed1[0]])
    sp2 = spread_start("ffn2", [landed1[0]])
    rl1 = relay_start("ffn1", landed1, [sp_att[3], sp2[3]])
    mid = mixer(1, cur, [rl1[3]])
    cur = ffn(1, mid, relay_wait("ffn1", rl1, mid))
    x_kv = cur

    landed_att = spread_wait("att", sp_att, cur)
    landed2 = spread_wait("ffn2", sp2, cur)
    sp3 = spread_start("ffn3", [landed2[0]])
    rl_att = relay_start("att", landed_att, [sp3[3]])
    rl2 = relay_start("ffn2", landed2, [sp3[3]])
    att_bufs = relay_wait("att", rl_att, rl2[3])
    wc = att_bufs[0].reshape(D_MODEL, KV_RANK)
    wpe = att_bufs[1].reshape(D_MODEL, LANES)
    wuk_g = att_bufs[2].reshape(N_HEADS, KV_RANK, NOPE)
    wuv_g = att_bufs[3].reshape(N_HEADS, KV_RANK, V_DIM)
    wdq_all = att_bufs[4].reshape(2, D_MODEL, Q_RANK)
    wuq_all = att_bufs[5]
    wo_all = att_bufs[6].reshape(2, D_MODEL, D_MODEL)
    k_sh, v_sh = _kv_fwd(cur, lnkv, wc, wpe, gl, wuk_g, wuv_g, gkn, gkr, cos, sin, "kv_fwd")
    mid = attention(0, cur, [])
    cur = ffn(2, mid, relay_wait("ffn2", rl2, mid))

    landed3 = spread_wait("ffn3", sp3, cur)
    rl3 = relay_start("ffn3", landed3)
    mid = attention(1, cur, [rl3[3]])
    cur = ffn(3, mid, relay_wait("ffn3", rl3, mid))

    dx, sq_cols = _loss_head(cur, target, "loss_head")

    small = {}
    sizes = dict(ffn0=FF_SHARD, ffn1=FF_SHARD, ffn2=FF_SHARD, ffn3=FF_SHARD, wo=128, kv512=128, dkv_pe=128,
                 wdq=128, wuqT=QK_PAD, wpool=32)
    big = dict(wo=lax.empty((2, D_MODEL, D_MODEL), BF16), kv512=lax.empty((3, D_MODEL, KV_RANK), BF16),
               dkv_pe=lax.empty((1, D_MODEL, LANES), BF16), wdq=lax.empty((2, D_MODEL, Q_RANK), BF16),
               wuqT=lax.empty((2, N_HEADS * QK_PAD, Q_RANK), BF16), wpool=lax.empty((8, GROUP_DIM, GROUP_DIM), BF16))
    for l in range(4):
        big[f"ffn{l}"] = lax.empty((3, D_FF, D_MODEL), BF16)
    red = {}

    def pair_start(nms, tag):
        arrs = []
        for nm in nms:
            view = _owner_view(big[nm], sizes[nm])
            arrs += [view, lax.empty((view.shape[0], N_CHIPS) + view.shape[3:], BF16)]
        return nms, tag, _copies_start(arrs, len(nms), _pair_send, f"pair_start_{tag}")

    def chip_start(state, after):
        nms, tag, (ssem, rsem, arrs, _) = state
        arrs = _copies_wait(arrs, ssem, rsem, after, _landed, f"pair_done_{tag}")
        out = []
        for t, nm in enumerate(nms):
            part = _pair_sum(arrs[2 * t], arrs[2 * t + 1], core, f"pair_sum_{nm}")
            out += [part, lax.empty((3, part.shape[0]) + part.shape[2:], BF16)]
        return nms, tag, _copies_start(out, len(nms), _chip_send, f"chip_start_{tag}")

    def chip_finish(state, after):
        nms, tag, (ssem, rsem, arrs, _) = state
        arrs = _copies_wait(arrs, ssem, rsem, after, _landed, f"chip_done_{tag}")
        for t, nm in enumerate(nms):
            red[nm] = _chip_sum(arrs[2 * t], arrs[2 * t + 1], chip, f"chip_sum_{nm}")

    dks, dvs = [], []
    pending = None
    bwd_deps = []
    for l in (3, 2, 1, 0):
        key = f"ffn{l}"
        act, dgb, dub = _ffn_bwd_hidden(dx, gates[l], ups[l], w_ffn[l], f"ffn_bwd_h{l}", deps=bwd_deps)
        dx, hn, dyb, dln = _ffn_bwd_input(x_mid[l], dx, dgb, dub, ln_ffn[l].reshape(1, -1), w_ffn[l],
                                          f"ffn_bwd_x{l}")
        bwd_deps = []
        small[f"ln_ffn{l}"] = dln
        if l == 1:
            att_chip = chip_start(att_pair, dx)
            tn_deps = [att_chip[2][3]]
        else:
            tn_deps = []
        if pending:
            chip_finish(pending, dx)
            pending = None
        big[key] = _tn_matmul(dgb, hn, big[key], 0, f"dw_gate{l}", m_chunk=FF_HALF, deps=tn_deps)
        big[key] = _tn_matmul(dub, hn, big[key], 1, f"dw_up{l}", m_chunk=FF_HALF)
        big[key] = _tn_matmul(act, dyb, big[key], 2, f"dw_down{l}", m_chunk=FF_HALF)
        if l == 1:
            chip_finish(att_chip, big[key])
        ffn_pair = pair_start([key], key)
        if l >= 2:
            j = l - 2
            do, dxb = _o_bwd(dx, wo_all[j], f"o_bwd{j}", deps=[ffn_pair[2][3]])
            big["wo"] = _tn_matmul(outs[j], dxb, big["wo"], j, f"dw_o{j}")
            ffn_chip = chip_start(ffn_pair, big["wo"])
            dq, dk, dv = _att_bwd(qs[j], k_sh, v_sh, do, outs[j], lses[j], f"att_bwd{j}", deps=[ffn_chip[2][3]])
            chip_finish(ffn_chip, dq)
            dks.append(dk)
            dvs.append(dv)
            dx, hnq, cqn, dqa, dcq, dln, dgql, dgqn, dgqr = _q_bwd(x_in[l], dx, dq, *q_args(j), f"q_bwd{j}")
            small[f"ln_mix_b{j}"] = dln
            small[f"g_q_latent{j}"] = dgql
            small[f"g_q{j}"] = jnp.concatenate([dgqn, dgqr[:, :ROPE]], axis=1)
            big["wdq"] = _tn_matmul(hnq, dcq, big["wdq"], j, f"dw_dq{j}")
            big["wuqT"] = _tn_matmul(dqa, cqn, big["wuqT"], j, f"dw_uq{j}")
            if l == 2:
                (dx, hnk, cn, dknb, dvb, dccb, dpeb, dlnkv, dgl, dgkn, dgkr) = _kv_bwd(
                    x_kv, dx, dks, dvs, lnkv, wc, wpe, gl, wuk_g, wuv_g, gkn, gkr, cos, sin, "kv_bwd")
                small["ln_kv"] = dlnkv
                small["g_kv_latent"] = dgl
                small["g_k"] = jnp.concatenate([dgkn, dgkr[:, :ROPE]], axis=1)
                big["kv512"] = _tn_matmul(dknb, cn, big["kv512"], 0, "dw_uk")
                big["kv512"] = _tn_matmul(dvb, cn, big["kv512"], 1, "dw_uv")
                big["kv512"] = _tn_matmul(hnk, dccb, big["kv512"], 2, "dw_dkv_c")
                big["dkv_pe"] = _tn_matmul(hnk, dpeb, big["dkv_pe"], 0, "dw_dkv_pe")
                att_pair = pair_start(["wo", "kv512", "dkv_pe", "wdq", "wuqT"], "att")
                bwd_deps = [att_pair[2][3]]
        else:
            dx, dyp, dsc, db, dln = _mix_bwd(x_in[l], dx, pooled[l], ln_a_all[l], wp_all[l], bp_all[l], sc_all[l],
                                             f"mix_bwd{l}", deps=[ffn_pair[2][3]])
            small[f"ln_mix_a{l}"] = dln
            small[f"pool_scale{l}"] = dsc
            small[f"b_pool{l}"] = db
            ffn_chip = chip_start(ffn_pair, dx)
            big["wpool"] = _tn_matmul(pooled[l], dyp, big["wpool"], 4 * l, f"dw_pool{l}", groups=4,
                                      deps=[ffn_chip[2][3]])
            if l == 1:
                pending = ffn_chip
            else:
                chip_finish(ffn_chip, big["wpool"])
    grad_x = dx[None]
    pool_pair = pair_start(["wpool"], "wpool")
    pool_chip = chip_start(pool_pair, pool_pair[2][3])
    chip_finish(pool_chip, pool_chip[2][3])

    vec_names = (["loss"] + [f"ln_ffn{l}" for l in range(4)] + ["ln_kv", "g_kv_latent", "g_k"]
                 + [f"{p}{j}" for p in ("ln_mix_b", "g_q_latent", "g_q") for j in range(2)]
                 + [f"{p}{l}" for p in ("ln_mix_a", "pool_scale", "b_pool") for l in range(2)])
    small["loss"] = sq_cols
    widths = [small[nm].shape[1] for nm in vec_names]
    padded = [-(-w // LANES) * LANES for w in widths]
    packed = jnp.concatenate([_pad_cols(small[nm], pw) for nm, pw in zip(vec_names, padded)], axis=1)
    (all_vecs,) = _all_gather([packed], [0], "gather_vectors")
    total = _sum_lead(all_vecs, "sum_vectors")
    vec = {}
    off = 0
    for nm, w, pw in zip(vec_names, widths, padded):
        vec[nm] = total[0, off:off + w]
        off += pw
    loss = 0.5 * jnp.sum(vec["loss"]) * (1.0 / D_MODEL)

    def own_cols(full, width):
        return lax.dynamic_slice_in_dim(full, dev * width, width, axis=full.ndim - 1)

    grads = dict(
        ln_mix_a=own_cols(jnp.stack([vec["ln_mix_a0"], vec["ln_mix_a1"]]), LANES),
        w_pool=red["wpool"].reshape(2, 4, 32, GROUP_DIM),
        b_pool=own_cols(jnp.stack([vec["b_pool0"], vec["b_pool1"]]).reshape(2, 4, GROUP_DIM), 32),
        pool_scale=own_cols(jnp.stack([vec["pool_scale0"], vec["pool_scale1"]]), LANES),
        ln_ffn=jnp.stack([vec[f"ln_ffn{l}"] for l in range(4)]),
        w_gate=jnp.stack([red[f"ffn{l}"][0] for l in range(4)]).transpose(0, 2, 1),
        w_up=jnp.stack([red[f"ffn{l}"][1] for l in range(4)]).transpose(0, 2, 1),
        w_down=jnp.stack([red[f"ffn{l}"][2] for l in range(4)]),
        ln_kv=vec["ln_kv"],
        w_dkv=jnp.concatenate([red["kv512"][2], red["dkv_pe"][0][:, :ROPE]], axis=1),
        g_kv_latent=vec["g_kv_latent"],
        w_uk=red["kv512"][0].T,
        w_uv=red["kv512"][1].T,
        g_k=vec["g_k"],
        ln_mix_b=jnp.stack([vec["ln_mix_b0"], vec["ln_mix_b1"]]),
        w_dq=red["wdq"],
        g_q_latent=jnp.stack([vec["g_q_latent0"], vec["g_q_latent1"]]),
        w_uq=red["wuqT"].transpose(0, 2, 1)[:, :, :QK_DIM],
        g_q=jnp.stack([vec["g_q0"], vec["g_q1"]]),
        w_o=red["wo"],
    )

    deltas, new_m, new_v = {}, {}, {}
    for nm in names:
        w = weights[nm]
        shape = w.shape if w.ndim > 1 else (1, w.shape[0])
        d, mo, vo = _adamw(w.reshape(shape), grads[nm].reshape(shape), mom1[nm].reshape(shape),
                           mom2[nm].reshape(shape), f"adamw_{nm}")
        deltas[nm], new_m[nm], new_v[nm] = d.reshape(w.shape), mo.reshape(w.shape), vo.reshape(w.shape)

    return (loss, grad_x, *[grads[nm].reshape(weights[nm].shape) for nm in names], *[deltas[nm] for nm in names],
            *[new_m[nm] for nm in names], *[new_v[nm] for nm in names])
```

```python
import functools
import math

import jax
import jax.numpy as jnp
from jax import lax
from jax.experimental import pallas as pl
from jax.experimental.pallas import tpu as pltpu

F32 = jnp.float32
BF16 = jnp.bfloat16
MESH = pl.DeviceIdType.MESH

D_MODEL = 1024
D_FF = 2816
N_DEV = 8
N_CHIPS = 4
FF_SHARD = D_FF // N_DEV
FF_HALF = D_FF // 2
N_HEADS = 8
NOPE = 128
ROPE = 64
QK_DIM = NOPE + ROPE
QK_PAD = 256
V_DIM = 128
Q_RANK = 256
KV_RANK = 512
POOL_WINDOWS = (2, 4, 8, 16)
GROUP_DIM = 256
HALO = 128
CHUNK = 64
ROPE_THETA = 10000.0
EPS = 1e-6
LANES = 128

ADAM_LR = 0.001
ADAM_B1 = 0.9
ADAM_B2 = 0.999
ADAM_EPS = 1e-08
ADAM_WD = 0.01
ADAM_STEP = 10

VMEM_BIG = 56 * 2**20
VMEM_MID = 40 * 2**20


def _nn(a, b):
    return lax.dot_general(a, b, (((1,), (0,)), ((), ())), preferred_element_type=F32)


def _nt(a, b):
    return lax.dot_general(a, b, (((1,), (1,)), ((), ())), preferred_element_type=F32)


def _tn(a, b):
    return lax.dot_general(a, b, (((0,), (0,)), ((), ())), preferred_element_type=F32)


def _rms(x, g, n):
    r = lax.rsqrt(jnp.sum(x * x, axis=-1, keepdims=True) * (1.0 / n) + EPS)
    return (x * r) * g, r


def _rms_bwd(x, r, g, dy, n):
    u = dy * g
    s = jnp.sum(x * u, axis=-1, keepdims=True) * (1.0 / n)
    dx = r * u - x * (r * r * r * s)
    dg = jnp.sum(dy * (x * r), axis=0, keepdims=True)
    return dx, dg


def _swap_halves(z):
    lane = lax.broadcasted_iota(jnp.int32, z.shape, 1)
    return jnp.where(lane < ROPE // 2, pltpu.roll(z, LANES - ROPE // 2, 1), pltpu.roll(z, ROPE // 2, 1))


def _sigmoid(x):
    return 1.0 / (1.0 + jnp.exp(-x))


def _cparams(n_grid, vmem=None):
    return pltpu.CompilerParams(dimension_semantics=("arbitrary",) * n_grid, vmem_limit_bytes=vmem)


def _rows(t, cols):
    return pl.BlockSpec((t, cols), lambda i: (i, 0))


def _full(shape):
    nd = len(shape)
    return pl.BlockSpec(shape, lambda *_: (0,) * nd)


ANY = pl.BlockSpec(memory_space=pl.ANY)


def _pcall(body, args, deps, *, in_specs, **kw):
    n_in, n_dep = len(args), len(deps)

    def ordered(*refs):
        body(*refs[:n_in], *refs[n_in + n_dep:])

    return pl.pallas_call(ordered, in_specs=list(in_specs) + [ANY] * n_dep, **kw)(*args, *deps)


def _place():
    x, y, c = lax.axis_index("x"), lax.axis_index("y"), lax.axis_index("c")
    return x, y, c


def _all_gather(shards, axes, name):
    n = len(shards)
    out_shape = [jax.ShapeDtypeStruct(s.shape[:a] + (N_DEV,) + s.shape[a:], s.dtype) for s, a in zip(shards, axes)]

    def body(*refs):
        ins, outs = refs[:n], refs[n:2 * n]
        send_sems, recv_sems, local_sems = refs[2 * n:]
        x, y, c = _place()
        me, sibling = (x, y, c), (x, y, 1 - c)
        chips = [(1 - x, y), (x, 1 - y), (1 - x, 1 - y)]

        def slot(t, dev):
            idx = 4 * dev[0] + 2 * dev[1] + dev[2]
            return outs[t].at[(slice(None),) * axes[t] + (idx,)]

        def copy(t, k, block, to, src=None):
            return pltpu.make_async_remote_copy(
                src_ref=slot(t, block) if src is None else src, dst_ref=slot(t, block),
                send_sem=send_sems.at[t, k], recv_sem=recv_sems.at[t, k],
                device_id=to, device_id_type=MESH)

        mine = [pltpu.make_async_copy(ins[t], slot(t, me), local_sems.at[t]) for t in range(n)]
        for cp in mine:
            cp.start()
        first = []
        for t in range(n):
            first.append(copy(t, 0, me, sibling, src=ins[t]))
            first += [copy(t, 1 + j, me, (*chip, c), src=ins[t]) for j, chip in enumerate(chips)]
        for cp in first:
            cp.start()
        passed = []
        for j, chip in enumerate(chips):
            for t in range(n):
                copy(t, 1 + j, (*chip, c), me).wait_recv()
                cp = copy(t, 4 + j, (*chip, c), sibling)
                cp.start()
                passed.append(cp)
        for t in range(n):
            copy(t, 0, sibling, me).wait_recv()
            for j, chip in enumerate(chips):
                copy(t, 4 + j, (*chip, 1 - c), me).wait_recv()
        for cp in first + passed:
            cp.wait_send()
        for cp in mine:
            cp.wait()

    return pl.pallas_call(
        body, name=name, out_shape=out_shape,
        in_specs=[ANY] * n, out_specs=[ANY] * n,
        scratch_shapes=[pltpu.SemaphoreType.DMA((n, 7)), pltpu.SemaphoreType.DMA((n, 7)),
                        pltpu.SemaphoreType.DMA((n,))],
    )(*shards)


HBM = pl.BlockSpec(memory_space=pltpu.HBM)
SEM = pl.BlockSpec(memory_space=pltpu.SEMAPHORE)
EFFECT = pltpu.SideEffectType.DATAFLOW_SIDE_EFFECTING


def _copies_start(arrays, n_sems, plan, name, deps=()):
    n, nd = len(arrays), len(deps)

    def body(*refs):
        for cp in plan(refs[:n], refs[n + nd], refs[n + nd + 1]):
            cp.start()
        refs[-1][...] = jnp.zeros_like(refs[-1])

    outs = pl.pallas_call(
        body, name=name,
        out_shape=(pltpu.SemaphoreType.DMA((n_sems,)), pltpu.SemaphoreType.DMA((n_sems,)),
                   *[pltpu.HBM(a.shape, a.dtype) for a in arrays], jax.ShapeDtypeStruct((8, LANES), F32)),
        in_specs=[HBM] * n + [ANY] * nd,
        out_specs=(SEM, SEM, *[HBM] * n, pl.BlockSpec(memory_space=pltpu.VMEM)),
        input_output_aliases={i: 2 + i for i in range(n)},
        compiler_params=pltpu.CompilerParams(has_side_effects=EFFECT),
    )(*[pltpu.with_memory_space_constraint(a, pltpu.HBM) for a in arrays], *deps)
    return outs[0], outs[1], list(outs[2:2 + n]), outs[-1]


def _copies_wait(arrays, send_sems, recv_sems, after, plan, name):
    n = len(arrays)

    def body(*refs):
        for cp in plan(refs[:n], refs[n], refs[n + 1]):
            cp.wait_send()
            cp.wait_recv()

    outs = pl.pallas_call(
        body, name=name,
        out_shape=tuple(pltpu.HBM(a.shape, a.dtype) for a in arrays),
        in_specs=[HBM] * n + [SEM, SEM, ANY], out_specs=tuple([HBM] * n),
        input_output_aliases={i: i for i in range(n)},
        compiler_params=pltpu.CompilerParams(has_side_effects=EFFECT),
    )(*arrays, send_sems, recv_sems, after)
    return list(outs)


def _remote(src, dst, send_sems, recv_sems, t, to):
    return pltpu.make_async_remote_copy(src_ref=src, dst_ref=dst, send_sem=send_sems.at[t], recv_sem=recv_sems.at[t],
                                        device_id=to, device_id_type=MESH)


def _dev_index(x, y, c):
    return 4 * x + 2 * y + c


def _gather_spread(bufs, send_sems, recv_sems):
    x, y, c = _place()
    mine = _dev_index(x, y, c)
    peers = [(x, y, 1 - c), (1 - x, y, c), (x, 1 - y, c), (1 - x, 1 - y, c)]
    return [_remote(g.at[k, mine], g.at[k, mine], send_sems, recv_sems, t, peer)
            for t, g in enumerate(bufs) for peer in peers for k in range(g.shape[0])]


def _gather_relay(bufs, send_sems, recv_sems):
    x, y, c = _place()
    blocks = [_dev_index(1 - x, y, c), _dev_index(x, 1 - y, c), _dev_index(1 - x, 1 - y, c)]
    return [_remote(g.at[k, b], g.at[k, b], send_sems, recv_sems, t, (x, y, 1 - c))
            for t, g in enumerate(bufs) for b in blocks for k in range(g.shape[0])]


def _blocks_moved(count):
    def plan(bufs, send_sems, recv_sems):
        x, y, c = _place()
        return [_remote(g.at[:, pl.ds(0, count)], g.at[:, pl.ds(0, count)], send_sems, recv_sems, t, (x, y, 1 - c))
                for t, g in enumerate(bufs)]
    return plan


def _pair_send(arrs, send_sems, recv_sems):
    x, y, c = _place()
    return [_remote(arrs[2 * t].at[p, k, 1 - c], arrs[2 * t + 1].at[p, k], send_sems, recv_sems, t, (x, y, 1 - c))
            for t in range(len(arrs) // 2) for p in range(arrs[2 * t].shape[0]) for k in range(N_CHIPS)]


def _chip_send(arrs, send_sems, recv_sems):
    x, y, c = _place()
    chips = [(1 - x, y), (x, 1 - y), (1 - x, 1 - y)]
    return [_remote(arrs[2 * t].at[p, 2 * px + py], arrs[2 * t + 1].at[j, p], send_sems, recv_sems, t, (px, py, c))
            for t in range(len(arrs) // 2) for j, (px, py) in enumerate(chips) for p in range(arrs[2 * t].shape[0])]


def _landed(arrs, send_sems, recv_sems):
    x, y, c = _place()
    return [_remote(arrs[2 * t + 1], arrs[2 * t + 1], send_sems, recv_sems, t, (x, y, 1 - c))
            for t in range(len(arrs) // 2)]


def _rows_per_step(rows, row_elems):
    best = 1
    for cand in range(1, rows + 1):
        if rows % cand == 0 and cand * row_elems <= 256 * 1024:
            best = cand
    return best


def _pair_sum(grad, landed, core, name):
    p, _, _, sz, c = grad.shape
    r = _rows_per_step(p * N_CHIPS, sz * c)

    def body(core_ref, g_ref, l_ref, o_ref):
        o_ref[...] = (g_ref[...].astype(F32) + l_ref[...].astype(F32)).astype(o_ref.dtype)

    out = pl.pallas_call(
        body, name=name,
        grid_spec=pltpu.PrefetchScalarGridSpec(
            num_scalar_prefetch=1, grid=(p * N_CHIPS // r,),
            in_specs=[pl.BlockSpec((r, None, sz, c), lambda i, cr: (i, cr[0], 0, 0)),
                      pl.BlockSpec((r, sz, c), lambda i, cr: (i, 0, 0))],
            out_specs=pl.BlockSpec((r, sz, c), lambda i, cr: (i, 0, 0))),
        out_shape=jax.ShapeDtypeStruct((p * N_CHIPS, sz, c), grad.dtype),
        compiler_params=_cparams(1),
    )(core, grad.reshape(p * N_CHIPS, 2, sz, c), landed.reshape(p * N_CHIPS, sz, c))
    return out.reshape(p, N_CHIPS, sz, c)


def _chip_sum(parts, landed, chip, name):
    p, _, sz, c = parts.shape
    r = _rows_per_step(p, sz * c)

    def body(chip_ref, a_ref, l_ref, o_ref):
        acc = a_ref[...].astype(F32)
        for j in range(3):
            acc = acc + l_ref[j].astype(F32)
        o_ref[...] = acc

    return pl.pallas_call(
        body, name=name,
        grid_spec=pltpu.PrefetchScalarGridSpec(
            num_scalar_prefetch=1, grid=(p // r,),
            in_specs=[pl.BlockSpec((r, None, sz, c), lambda i, cr: (i, cr[0], 0, 0)),
                      pl.BlockSpec((3, r, sz, c), lambda i, cr: (0, i, 0, 0))],
            out_specs=pl.BlockSpec((r, sz, c), lambda i, cr: (i, 0, 0))),
        out_shape=jax.ShapeDtypeStruct((p, sz, c), F32),
        compiler_params=_cparams(1),
    )(chip, parts, landed)


def _sum_lead(a, name, out_dtype=F32):
    k = a.shape[0]
    rest = a.shape[1:]
    r, c = rest[-2], rest[-1]
    lead = math.prod(rest[:-2])
    a3 = a.reshape(k, lead * r, c)
    rows = lead * r
    tb = rows
    for cand in (512, 256, 128, 64, 32, 16, 8):
        if rows % cand == 0 and rows > cand:
            tb = cand
            break

    def body(a_ref, o_ref):
        acc = a_ref[0].astype(F32)
        for i in range(1, k):
            acc = acc + a_ref[i].astype(F32)
        o_ref[...] = acc.astype(out_dtype)

    out = pl.pallas_call(
        body, name=name, grid=(rows // tb,),
        out_shape=jax.ShapeDtypeStruct((rows, c), out_dtype),
        in_specs=[pl.BlockSpec((k, tb, c), lambda i: (0, i, 0))],
        out_specs=pl.BlockSpec((tb, c), lambda i: (i, 0)),
        compiler_params=_cparams(1),
    )(a3)
    return out.reshape(rest)


def _bands(t, causal):
    r = lax.broadcasted_iota(jnp.int32, (t, t + HALO), 0)
    col = lax.broadcasted_iota(jnp.int32, (t, t + HALO), 1)
    diff = r + HALO - col if causal else col - r
    return jnp.stack([jnp.where((diff >= 0) & (diff < w), 1.0, 0.0) for w in POOL_WINDOWS]).astype(BF16)


def _split_dot(band, v):
    hi = v.astype(BF16)
    lo = (v - hi.astype(F32)).astype(BF16)
    return _nn(band, hi) + _nn(band, lo)


def _mix_fwd(x, g, wp, b, sc, name, deps=()):
    s = x.shape[0]
    t = min(256, s)
    rb = t // HALO

    def body(x_ref, xh_ref, g_ref, wp_ref, b_ref, sc_ref, band_ref, xo_ref, d_ref):
        i = pl.program_id(0)
        gg = g_ref[...]
        h, _ = _rms(x_ref[...], gg, D_MODEL)
        hh, _ = _rms(xh_ref[...], gg, D_MODEL)
        hh = jnp.where(i > 0, hh, 0.0)
        hext = jnp.concatenate([hh, h], axis=0)
        tok = i * t + lax.broadcasted_iota(jnp.int32, (t, 1), 0)
        for gi, w in enumerate(POOL_WINDOWS):
            sl = slice(gi * GROUP_DIM, (gi + 1) * GROUP_DIM)
            win = _split_dot(band_ref[gi], hext[:, sl])
            cnt = jnp.minimum(tok + 1, w).astype(F32)
            dbf = (win / cnt - h[:, sl]).astype(BF16)
            d_ref[:, sl] = dbf
            ypre = _nn(dbf, wp_ref[gi]) + b_ref[:, sl]
            xo_ref[:, sl] = x_ref[:, sl] + ypre * sc_ref[:, sl]

    return _pcall(
        body, (x, x, g, wp, b, sc, _bands(t, True)), deps, name=name, grid=(s // t,),
        out_shape=[jax.ShapeDtypeStruct((s, D_MODEL), F32), jax.ShapeDtypeStruct((s, D_MODEL), BF16)],
        in_specs=[_rows(t, D_MODEL),
                  pl.BlockSpec((HALO, D_MODEL), lambda i: (jnp.maximum(i * rb - 1, 0), 0)),
                  _full((1, D_MODEL)), _full((4, GROUP_DIM, GROUP_DIM)), _full((1, D_MODEL)), _full((1, D_MODEL)),
                  _full((4, t, t + HALO))],
        out_specs=[_rows(t, D_MODEL), _rows(t, D_MODEL)],
        compiler_params=_cparams(1, VMEM_MID),
    )


def _mix_bwd(x, dy, d, g, wp, b, sc, name, deps=()):
    s = x.shape[0]
    t = min(256, s)
    rb = t // HALO
    nb = s // t
    last_halo = s // HALO - 1

    def body(x_ref, dy_ref, dyn_ref, d_ref, g_ref, wp_ref, b_ref, sc_ref, band_ref,
             dx_ref, dyp_ref, dsc_ref, db_ref, dln_ref):
        i = pl.program_id(0)
        x = x_ref[...]
        gg = g_ref[...]
        dy = dy_ref[...]
        sc = sc_ref[...]
        dyp32 = dy * sc
        dyp = dyp32.astype(BF16)
        dyph = (dyn_ref[...] * sc).astype(BF16)
        dyp_ref[...] = dyp
        tok = i * t + lax.broadcasted_iota(jnp.int32, (t + HALO, 1), 0)
        dh, dsc = [], []
        for gi, w in enumerate(POOL_WINDOWS):
            sl = slice(gi * GROUP_DIM, (gi + 1) * GROUP_DIM)
            ypre = _nn(d_ref[:, sl], wp_ref[gi]) + b_ref[:, sl]
            dsc.append(jnp.sum(dy[:, sl] * ypre, axis=0, keepdims=True))
            dd = _nt(dyp[:, sl], wp_ref[gi])
            ddh = jnp.where(i < nb - 1, _nt(dyph[:, sl], wp_ref[gi]), 0.0)
            cnt = jnp.minimum(tok + 1, w).astype(F32)
            ddext = jnp.concatenate([dd, ddh], axis=0) / cnt
            dh.append(_split_dot(band_ref[gi], ddext) - dd)
        dh = jnp.concatenate(dh, axis=1)
        _, r = _rms(x, gg, D_MODEL)
        dxn, dg = _rms_bwd(x, r, gg, dh, D_MODEL)
        dx_ref[...] = dy + dxn

        @pl.when(i == 0)
        def _():
            dsc_ref[...] = jnp.zeros_like(dsc_ref)
            db_ref[...] = jnp.zeros_like(db_ref)
            dln_ref[...] = jnp.zeros_like(dln_ref)

        dsc_ref[...] += jnp.concatenate(dsc, axis=1)
        db_ref[...] += jnp.sum(dyp32, axis=0, keepdims=True)
        dln_ref[...] += dg

    vec = jax.ShapeDtypeStruct((1, D_MODEL), F32)
    return _pcall(
        body, (x, dy, dy, d, g, wp, b, sc, _bands(t, False)), deps, name=name, grid=(nb,),
        out_shape=[jax.ShapeDtypeStruct((s, D_MODEL), F32), jax.ShapeDtypeStruct((s, D_MODEL), BF16), vec, vec, vec],
        in_specs=[_rows(t, D_MODEL), _rows(t, D_MODEL),
                  pl.BlockSpec((HALO, D_MODEL), lambda i: (jnp.minimum((i + 1) * rb, last_halo), 0)),
                  _rows(t, D_MODEL),
                  _full((1, D_MODEL)), _full((4, GROUP_DIM, GROUP_DIM)), _full((1, D_MODEL)), _full((1, D_MODEL)),
                  _full((4, t, t + HALO))],
        out_specs=[_rows(t, D_MODEL), _rows(t, D_MODEL), _full((1, D_MODEL)), _full((1, D_MODEL)), _full((1, D_MODEL))],
        compiler_params=_cparams(1, VMEM_MID),
    )


def _load_weights(w_hbm, w_vmem, sem):
    @pl.when(pl.program_id(0) == 0)
    def _():
        cp = pltpu.make_async_copy(w_hbm, w_vmem, sem)
        cp.start()
        cp.wait()


def _ffn_fwd(x, g, w, name):
    s = x.shape[0]
    t = min(512, s)

    def body(x_ref, g_ref, w_hbm, xo_ref, gate_ref, up_ref, w_ref, sem):
        _load_weights(w_hbm, w_ref, sem)
        x = x_ref[...]
        hn = _rms(x, g_ref[...], D_MODEL)[0].astype(BF16)
        acc = x
        for c in range(2):
            rs = slice(c * FF_HALF, (c + 1) * FF_HALF)
            gt = _nt(hn, w_ref[0, rs, :])
            up = _nt(hn, w_ref[1, rs, :])
            gate_ref[:, rs] = gt.astype(BF16)
            up_ref[:, rs] = up.astype(BF16)
            act = ((gt * _sigmoid(gt)) * up).astype(BF16)
            acc = acc + _nn(act, w_ref[2, rs, :])
        xo_ref[...] = acc

    hid = jax.ShapeDtypeStruct((s, D_FF), BF16)
    return pl.pallas_call(
        body, name=name, grid=(s // t,),
        out_shape=[jax.ShapeDtypeStruct((s, D_MODEL), F32), hid, hid],
        in_specs=[_rows(t, D_MODEL), _full((1, D_MODEL)), ANY],
        out_specs=[_rows(t, D_MODEL), _rows(t, D_FF), _rows(t, D_FF)],
        scratch_shapes=[pltpu.VMEM((3, D_FF, D_MODEL), BF16), pltpu.SemaphoreType.DMA],
        compiler_params=_cparams(1, VMEM_BIG),
    )(x, g, w)


def _ffn_bwd(x, dy, gate, up, g, w, name, deps=()):
    s = x.shape[0]
    t = min(256, s)

    def body(x_ref, dy_ref, gate_ref, up_ref, g_ref, w_hbm,
             dx_ref, act_ref, dg_ref, du_ref, hn_ref, dyb_ref, dln_ref, w_ref, sem):
        _load_weights(w_hbm, w_ref, sem)
        x = x_ref[...]
        gg = g_ref[...]
        y, r = _rms(x, gg, D_MODEL)
        hn = y.astype(BF16)
        hn_ref[...] = hn
        dy = dy_ref[...]
        dyb = dy.astype(BF16)
        dyb_ref[...] = dyb
        dh = jnp.zeros((t, D_MODEL), F32)
        for c in range(2):
            rs = slice(c * FF_HALF, (c + 1) * FF_HALF)
            gt = gate_ref[:, rs].astype(F32)
            u = up_ref[:, rs].astype(F32)
            sg = _sigmoid(gt)
            sl = gt * sg
            act_ref[:, rs] = (sl * u).astype(BF16)
            dact = _nt(dyb, w_ref[2, rs, :])
            dg = (dact * u * (sg * (1.0 + gt * (1.0 - sg)))).astype(BF16)
            du = (dact * sl).astype(BF16)
            dg_ref[:, rs] = dg
            du_ref[:, rs] = du
            dh = dh + _nn(dg, w_ref[0, rs, :]) + _nn(du, w_ref[1, rs, :])
        dxn, dgl = _rms_bwd(x, r, gg, dh, D_MODEL)
        dx_ref[...] = dy + dxn

        @pl.when(pl.program_id(0) == 0)
        def _():
            dln_ref[...] = jnp.zeros_like(dln_ref)

        dln_ref[...] += dgl

    hid = jax.ShapeDtypeStruct((s, D_FF), BF16)
    tok = jax.ShapeDtypeStruct((s, D_MODEL), BF16)
    return _pcall(
        body, (x, dy, gate, up, g, w), deps, name=name, grid=(s // t,),
        out_shape=[jax.ShapeDtypeStruct((s, D_MODEL), F32), hid, hid, hid, tok, tok,
                   jax.ShapeDtypeStruct((1, D_MODEL), F32)],
        in_specs=[_rows(t, D_MODEL), _rows(t, D_MODEL), _rows(t, D_FF), _rows(t, D_FF), _full((1, D_MODEL)), ANY],
        out_specs=[_rows(t, D_MODEL), _rows(t, D_FF), _rows(t, D_FF), _rows(t, D_FF),
                   _rows(t, D_MODEL), _rows(t, D_MODEL), _full((1, D_MODEL))],
        scratch_shapes=[pltpu.VMEM((3, D_FF, D_MODEL), BF16), pltpu.SemaphoreType.DMA],
        compiler_params=_cparams(1, VMEM_BIG),
    )


def _tn_matmul(a, b, into, p0, name, groups=1, m_chunk=None, deps=()):
    s = a.shape[0]
    m, n = a.shape[1] // groups, b.shape[1] // groups
    assert into.shape[1:] == (m, n)
    mc = m if m_chunk is None else m_chunk
    nm = m // mc
    t = min(1024, s)
    nt = s // t

    def body(a_ref, b_ref, into_ref, o_ref, acc):
        ti = pl.program_id(2)

        @pl.when(ti == 0)
        def _():
            acc[...] = jnp.zeros_like(acc)

        acc[...] += _tn(a_ref[...], b_ref[...])

        @pl.when(ti == nt - 1)
        def _():
            o_ref[...] = acc[...].astype(o_ref.dtype)

    return _pcall(
        body, (a, b, into), deps, name=name, grid=(groups, nm, nt),
        out_shape=jax.ShapeDtypeStruct(into.shape, into.dtype),
        in_specs=[pl.BlockSpec((t, mc), lambda gi, mi, ti: (ti, gi * nm + mi)),
                  pl.BlockSpec((t, n), lambda gi, mi, ti: (ti, gi)), ANY],
        out_specs=pl.BlockSpec((None, mc, n), lambda gi, mi, ti: (p0 + gi, mi, 0)),
        scratch_shapes=[pltpu.VMEM((mc, n), F32)],
        input_output_aliases={2: 0},
        compiler_params=_cparams(3, VMEM_MID),
    )


def _rope_tables(positions):
    half = ROPE // 2
    inv = ROPE_THETA ** (-jnp.arange(half, dtype=F32) * 2.0 / ROPE)
    ang = positions.astype(F32)[:, None] * inv
    cos, sin = jnp.cos(ang), jnp.sin(ang)
    zero = jnp.zeros((positions.shape[0], LANES - ROPE), F32)
    return jnp.concatenate([cos, cos, zero], axis=1), jnp.concatenate([-sin, sin, zero], axis=1)


def _kv_specs(t):
    return [_full((1, D_MODEL)), _full((D_MODEL, KV_RANK)), _full((D_MODEL, LANES)), _full((1, KV_RANK)),
            _full((N_HEADS, KV_RANK, NOPE)), _full((N_HEADS, KV_RANK, V_DIM)),
            _full((1, NOPE)), _full((1, LANES)), _rows(t, LANES), _rows(t, LANES)]


def _kv_fwd(x, ln, wc, wpe, gl, wuk, wuv, gkn, gkr, cos, sin, name, deps=()):
    s = x.shape[0]
    t = min(256, s)

    def body(x_ref, ln_ref, wc_ref, wpe_ref, gl_ref, wuk_ref, wuv_ref, gkn_ref, gkr_ref, cos_ref, sin_ref,
             k_ref, v_ref):
        hn = _rms(x_ref[...], ln_ref[...], D_MODEL)[0].astype(BF16)
        clat = _nn(hn, wc_ref[...])
        kpe = _nn(hn, wpe_ref[...])
        cn = _rms(clat, gl_ref[...], KV_RANK)[0].astype(BF16)
        sspe = jnp.sum(kpe * kpe, axis=-1, keepdims=True)
        cs, sn = cos_ref[...], sin_ref[...]
        for h in range(N_HEADS):
            kn = _nn(cn, wuk_ref[h])
            r = lax.rsqrt((jnp.sum(kn * kn, axis=-1, keepdims=True) + sspe) * (1.0 / QK_DIM) + EPS)
            k_ref[:, h * QK_PAD:h * QK_PAD + NOPE] = ((kn * r) * gkn_ref[...]).astype(BF16)
            z = (kpe * r) * gkr_ref[...]
            k_ref[:, h * QK_PAD + NOPE:(h + 1) * QK_PAD] = (z * cs + _swap_halves(z) * sn).astype(BF16)
            v_ref[:, h * V_DIM:(h + 1) * V_DIM] = _nn(cn, wuv_ref[h]).astype(BF16)

    return _pcall(
        body, (x, ln, wc, wpe, gl, wuk, wuv, gkn, gkr, cos, sin), deps, name=name, grid=(s // t,),
        out_shape=[jax.ShapeDtypeStruct((s, N_HEADS * QK_PAD), BF16), jax.ShapeDtypeStruct((s, N_HEADS * V_DIM), BF16)],
        in_specs=[_rows(t, D_MODEL)] + _kv_specs(t),
        out_specs=[_rows(t, N_HEADS * QK_PAD), _rows(t, N_HEADS * V_DIM)],
        compiler_params=_cparams(1, VMEM_MID),
    )


def _kv_bwd(x, dxin, dks, dvs, ln, wc, wpe, gl, wuk, wuv, gkn, gkr, cos, sin, name):
    s = x.shape[0]
    t = min(256, s)
    nk = len(dks)

    def body(*refs):
        x_ref, dxin_ref = refs[:2]
        dk_refs = refs[2:2 + nk]
        dv_refs = refs[2 + nk:2 + 2 * nk]
        (ln_ref, wc_ref, wpe_ref, gl_ref, wuk_ref, wuv_ref, gkn_ref, gkr_ref, cos_ref, sin_ref,
         dx_ref, hn_ref, cn_ref, dkn_ref, dvb_ref, dcc_ref, dpe_ref,
         dln_ref, dgl_ref, dgkn_ref, dgkr_ref) = refs[2 + 2 * nk:]
        x = x_ref[...]
        ln = ln_ref[...]
        y, rx = _rms(x, ln, D_MODEL)
        hn = y.astype(BF16)
        hn_ref[...] = hn
        clat = _nn(hn, wc_ref[...])
        kpe = _nn(hn, wpe_ref[...])
        gl = gl_ref[...]
        cy, rc = _rms(clat, gl, KV_RANK)
        cn = cy.astype(BF16)
        cn_ref[...] = cn
        sspe = jnp.sum(kpe * kpe, axis=-1, keepdims=True)
        cs, sn = cos_ref[...], sin_ref[...]
        gkn, gkr = gkn_ref[...], gkr_ref[...]
        dc = jnp.zeros((t, KV_RANK), F32)
        dkpe = jnp.zeros((t, LANES), F32)
        dgkn = jnp.zeros((1, NOPE), F32)
        dgkr = jnp.zeros((1, LANES), F32)
        for h in range(N_HEADS):
            kn = _nn(cn, wuk_ref[h])
            r = lax.rsqrt((jnp.sum(kn * kn, axis=-1, keepdims=True) + sspe) * (1.0 / QK_DIM) + EPS)
            lo, mid, hi = h * QK_PAD, h * QK_PAD + NOPE, (h + 1) * QK_PAD
            dko = dk_refs[0][:, lo:mid]
            dkr = dk_refs[0][:, mid:hi]
            dvh = dv_refs[0][:, h * V_DIM:(h + 1) * V_DIM]
            for j in range(1, nk):
                dko = dko + dk_refs[j][:, lo:mid]
                dkr = dkr + dk_refs[j][:, mid:hi]
                dvh = dvh + dv_refs[j][:, h * V_DIM:(h + 1) * V_DIM]
            dz = dkr * cs - _swap_halves(dkr) * sn
            un = dko * gkn
            ur = dz * gkr
            sm = (jnp.sum(kn * un, axis=-1, keepdims=True) + jnp.sum(kpe * ur, axis=-1, keepdims=True)) * (1.0 / QK_DIM)
            coef = r * r * r * sm
            dkn = (r * un - kn * coef).astype(BF16)
            dkpe = dkpe + (r * ur - kpe * coef)
            dgkn = dgkn + jnp.sum(dko * (kn * r), axis=0, keepdims=True)
            dgkr = dgkr + jnp.sum(dz * (kpe * r), axis=0, keepdims=True)
            dkn_ref[:, h * NOPE:(h + 1) * NOPE] = dkn
            dvb = dvh.astype(BF16)
            dvb_ref[:, h * V_DIM:(h + 1) * V_DIM] = dvb
            dc = dc + _nt(dkn, wuk_ref[h]) + _nt(dvb, wuv_ref[h])
        dclat, dgl = _rms_bwd(clat, rc, gl, dc, KV_RANK)
        dcc = dclat.astype(BF16)
        dpe = dkpe.astype(BF16)
        dcc_ref[...] = dcc
        dpe_ref[...] = dpe
        dhn = _nt(dcc, wc_ref[...]) + _nt(dpe, wpe_ref[...])
        dxn, dln = _rms_bwd(x, rx, ln, dhn, D_MODEL)
        dx_ref[...] = dxin_ref[...] + dxn

        @pl.when(pl.program_id(0) == 0)
        def _():
            dln_ref[...] = jnp.zeros_like(dln_ref)
            dgl_ref[...] = jnp.zeros_like(dgl_ref)
            dgkn_ref[...] = jnp.zeros_like(dgkn_ref)
            dgkr_ref[...] = jnp.zeros_like(dgkr_ref)

        dln_ref[...] += dln
        dgl_ref[...] += dgl
        dgkn_ref[...] += dgkn
        dgkr_ref[...] += dgkr

    def tok(cols, dt):
        return jax.ShapeDtypeStruct((s, cols), dt)

    def vec(cols):
        return jax.ShapeDtypeStruct((1, cols), F32)

    return pl.pallas_call(
        body, name=name, grid=(s // t,),
        out_shape=[tok(D_MODEL, F32), tok(D_MODEL, BF16), tok(KV_RANK, BF16), tok(N_HEADS * NOPE, BF16),
                   tok(N_HEADS * V_DIM, BF16), tok(KV_RANK, BF16), tok(LANES, BF16),
                   vec(D_MODEL), vec(KV_RANK), vec(NOPE), vec(LANES)],
        in_specs=[_rows(t, D_MODEL), _rows(t, D_MODEL)] + [_rows(t, N_HEADS * QK_PAD)] * nk
                 + [_rows(t, N_HEADS * V_DIM)] * nk + _kv_specs(t),
        out_specs=[_rows(t, D_MODEL), _rows(t, D_MODEL), _rows(t, KV_RANK), _rows(t, N_HEADS * NOPE),
                   _rows(t, N_HEADS * V_DIM), _rows(t, KV_RANK), _rows(t, LANES),
                   _full((1, D_MODEL)), _full((1, KV_RANK)), _full((1, NOPE)), _full((1, LANES))],
        compiler_params=_cparams(1, VMEM_BIG),
    )(x, dxin, *dks, *dvs, ln, wc, wpe, gl, wuk, wuv, gkn, gkr, cos, sin)


def _q_specs(t):
    return [_full((1, D_MODEL)), _full((D_MODEL, Q_RANK)), _full((1, Q_RANK)), _full((N_HEADS, Q_RANK, QK_PAD)),
            _full((1, NOPE)), _full((1, LANES)), _rows(t, LANES), _rows(t, LANES)]


def _q_fwd(x, ln, wdq, gql, wuq, gqn, gqr, cos, sin, name, deps=()):
    s = x.shape[0]
    t = min(256, s)

    def body(x_ref, ln_ref, wdq_ref, gql_ref, wuq_ref, gqn_ref, gqr_ref, cos_ref, sin_ref, q_ref):
        hn = _rms(x_ref[...], ln_ref[...], D_MODEL)[0].astype(BF16)
        cqn = _rms(_nn(hn, wdq_ref[...]), gql_ref[...], Q_RANK)[0].astype(BF16)
        cs, sn = cos_ref[...], sin_ref[...]
        for h in range(N_HEADS):
            qa = _nn(cqn, wuq_ref[h])
            r = lax.rsqrt(jnp.sum(qa * qa, axis=-1, keepdims=True) * (1.0 / QK_DIM) + EPS)
            q_ref[:, h * QK_PAD:h * QK_PAD + NOPE] = ((qa[:, :NOPE] * r) * gqn_ref[...]).astype(BF16)
            z = (qa[:, NOPE:] * r) * gqr_ref[...]
            q_ref[:, h * QK_PAD + NOPE:(h + 1) * QK_PAD] = (z * cs + _swap_halves(z) * sn).astype(BF16)

    return _pcall(
        body, (x, ln, wdq, gql, wuq, gqn, gqr, cos, sin), deps, name=name, grid=(s // t,),
        out_shape=jax.ShapeDtypeStruct((s, N_HEADS * QK_PAD), BF16),
        in_specs=[_rows(t, D_MODEL)] + _q_specs(t),
        out_specs=_rows(t, N_HEADS * QK_PAD),
        compiler_params=_cparams(1, VMEM_MID),
    )


def _q_bwd(x, dxin, dq, ln, wdq, gql, wuq, gqn, gqr, cos, sin, name):
    s = x.shape[0]
    t = min(256, s)

    def body(x_ref, dxin_ref, dq_ref, ln_ref, wdq_ref, gql_ref, wuq_ref, gqn_ref, gqr_ref, cos_ref, sin_ref,
             dx_ref, hn_ref, cqn_ref, dqa_ref, dcq_ref, dln_ref, dgql_ref, dgqn_ref, dgqr_ref):
        x = x_ref[...]
        ln = ln_ref[...]
        y, rx = _rms(x, ln, D_MODEL)
        hn = y.astype(BF16)
        hn_ref[...] = hn
        cqp = _nn(hn, wdq_ref[...])
        gql = gql_ref[...]
        cy, rc = _rms(cqp, gql, Q_RANK)
        cqn = cy.astype(BF16)
        cqn_ref[...] = cqn
        cs, sn = cos_ref[...], sin_ref[...]
        gqn, gqr = gqn_ref[...], gqr_ref[...]
        dcq = jnp.zeros((t, Q_RANK), F32)
        dgqn = jnp.zeros((1, NOPE), F32)
        dgqr = jnp.zeros((1, LANES), F32)
        for h in range(N_HEADS):
            qa = _nn(cqn, wuq_ref[h])
            qn, qr = qa[:, :NOPE], qa[:, NOPE:]
            r = lax.rsqrt(jnp.sum(qa * qa, axis=-1, keepdims=True) * (1.0 / QK_DIM) + EPS)
            dqo = dq_ref[:, h * QK_PAD:h * QK_PAD + NOPE]
            dqr = dq_ref[:, h * QK_PAD + NOPE:(h + 1) * QK_PAD]
            dz = dqr * cs - _swap_halves(dqr) * sn
            un = dqo * gqn
            ur = dz * gqr
            sm = (jnp.sum(qn * un, axis=-1, keepdims=True) + jnp.sum(qr * ur, axis=-1, keepdims=True)) * (1.0 / QK_DIM)
            coef = r * r * r * sm
            dqa = jnp.concatenate([r * un - qn * coef, r * ur - qr * coef], axis=1).astype(BF16)
            dgqn = dgqn + jnp.sum(dqo * (qn * r), axis=0, keepdims=True)
            dgqr = dgqr + jnp.sum(dz * (qr * r), axis=0, keepdims=True)
            dqa_ref[:, h * QK_PAD:(h + 1) * QK_PAD] = dqa
            dcq = dcq + _nt(dqa, wuq_ref[h])
        dcqp, dgql = _rms_bwd(cqp, rc, gql, dcq, Q_RANK)
        dcqb = dcqp.astype(BF16)
        dcq_ref[...] = dcqb
        dhn = _nt(dcqb, wdq_ref[...])
        dxn, dln = _rms_bwd(x, rx, ln, dhn, D_MODEL)
        dx_ref[...] = dxin_ref[...] + dxn

        @pl.when(pl.program_id(0) == 0)
        def _():
            dln_ref[...] = jnp.zeros_like(dln_ref)
            dgql_ref[...] = jnp.zeros_like(dgql_ref)
            dgqn_ref[...] = jnp.zeros_like(dgqn_ref)
            dgqr_ref[...] = jnp.zeros_like(dgqr_ref)

        dln_ref[...] += dln
        dgql_ref[...] += dgql
        dgqn_ref[...] += dgqn
        dgqr_ref[...] += dgqr

    def tok(cols, dt):
        return jax.ShapeDtypeStruct((s, cols), dt)

    def vec(cols):
        return jax.ShapeDtypeStruct((1, cols), F32)

    return pl.pallas_call(
        body, name=name, grid=(s // t,),
        out_shape=[tok(D_MODEL, F32), tok(D_MODEL, BF16), tok(Q_RANK, BF16), tok(N_HEADS * QK_PAD, BF16),
                   tok(Q_RANK, BF16), vec(D_MODEL), vec(Q_RANK), vec(NOPE), vec(LANES)],
        in_specs=[_rows(t, D_MODEL), _rows(t, D_MODEL), _rows(t, N_HEADS * QK_PAD)] + _q_specs(t),
        out_specs=[_rows(t, D_MODEL), _rows(t, D_MODEL), _rows(t, Q_RANK), _rows(t, N_HEADS * QK_PAD),
                   _rows(t, Q_RANK), _full((1, D_MODEL)), _full((1, Q_RANK)), _full((1, NOPE)), _full((1, LANES))],
        compiler_params=_cparams(1, VMEM_MID),
    )(x, dxin, dq, ln, wdq, gql, wuq, gqn, gqr, cos, sin)


SM_SCALE = 1.0 / math.sqrt(QK_DIM)
LOG2_E = math.log2(math.e)
EXP2_SCALE = SM_SCALE * LOG2_E
NEG = -1e30


def _diag_mask(t):
    qpos = lax.broadcasted_iota(jnp.int32, (t, t), 0)
    kpos = lax.broadcasted_iota(jnp.int32, (t, t), 1)
    return lax.shift_right_logical(kpos, 6) <= lax.shift_right_logical(qpos, 6)


def _att_fwd(q, k, v, name):
    s = q.shape[0]
    t = min(512, s)
    nb = s // t

    def body(q_ref, k_ref, v_ref, o_ref, lse_ref):
        qi = pl.program_id(1)
        qq = q_ref[...]

        def block(ki, carry, masked):
            m_old, l_old, acc = carry
            rows = pl.ds(pl.multiple_of(ki * t, t), t)
            sc = _nt(qq, k_ref[rows, :])
            if masked:
                sc = jnp.where(_diag_mask(t), sc, NEG)
            m_new = jnp.maximum(m_old, jnp.max(sc, axis=-1, keepdims=True))
            p = jnp.exp2((sc - m_new) * EXP2_SCALE)
            alpha = jnp.exp2((m_old - m_new) * EXP2_SCALE)
            l_new = alpha * l_old + jnp.sum(p, axis=-1, keepdims=True)
            acc = alpha * acc + _nn(p.astype(BF16), v_ref[rows, :])
            return m_new, l_new, acc

        init = (jnp.full((t, 1), NEG, F32), jnp.zeros((t, 1), F32), jnp.zeros((t, V_DIM), F32))
        carry = lax.fori_loop(0, qi, lambda ki, c: block(ki, c, False), init)
        m_fin, l_fin, acc = block(qi, carry, True)
        o_ref[...] = (acc / l_fin).astype(BF16)
        lse_ref[...] = jnp.broadcast_to(m_fin * SM_SCALE + jnp.log(l_fin), (t, LANES))

    return pl.pallas_call(
        body, name=name, grid=(N_HEADS, nb),
        out_shape=[jax.ShapeDtypeStruct((s, N_HEADS * V_DIM), BF16), jax.ShapeDtypeStruct((s, N_HEADS * LANES), F32)],
        in_specs=[pl.BlockSpec((t, QK_PAD), lambda h, qi: (qi, h)),
                  pl.BlockSpec((s, QK_PAD), lambda h, qi: (0, h)),
                  pl.BlockSpec((s, V_DIM), lambda h, qi: (0, h))],
        out_specs=[pl.BlockSpec((t, V_DIM), lambda h, qi: (qi, h)),
                   pl.BlockSpec((t, LANES), lambda h, qi: (qi, h))],
        compiler_params=_cparams(2, VMEM_MID),
    )(q, k, v)


def _att_bwd(q, k, v, do, o, lse, name, deps=()):
    s = q.shape[0]
    t = min(512, s)
    nb = s // t

    def body(q_ref, k_ref, v_ref, do_ref, o_ref, lse_ref, dq_ref, dk_ref, dv_ref):
        ki = pl.program_id(1)
        kk, vv = k_ref[...], v_ref[...]

        @pl.when(ki == 0)
        def _():
            dq_ref[...] = jnp.zeros_like(dq_ref)

        def block(qi, carry, masked):
            dk, dv = carry
            rows = pl.ds(pl.multiple_of(qi * t, t), t)
            qq, dob = q_ref[rows, :], do_ref[rows, :]
            sc = _nt(qq, kk)
            if masked:
                sc = jnp.where(_diag_mask(t), sc, NEG)
            p = jnp.exp2(sc * EXP2_SCALE - lse_ref[rows, :][:, :1] * LOG2_E)
            dp = _nt(dob, vv)
            dsum = jnp.sum(dob.astype(F32) * o_ref[rows, :].astype(F32), axis=-1, keepdims=True)
            ds = (p * (dp - dsum)).astype(BF16)
            dq_ref[rows, :] += _nn(ds, kk)
            return dk + _tn(ds, qq), dv + _tn(p.astype(BF16), dob)

        carry = block(ki, (jnp.zeros((t, QK_PAD), F32), jnp.zeros((t, V_DIM), F32)), True)
        dk, dv = lax.fori_loop(ki + 1, nb, lambda qi, c: block(qi, c, False), carry)
        dk_ref[...] = dk * SM_SCALE
        dv_ref[...] = dv

        @pl.when(ki == nb - 1)
        def _():
            dq_ref[...] = dq_ref[...] * SM_SCALE

    def head(h, ki):
        return (0, h)

    def kblock(h, ki):
        return (ki, h)

    return _pcall(
        body, (q, k, v, do, o, lse), deps, name=name, grid=(N_HEADS, nb),
        out_shape=[jax.ShapeDtypeStruct((s, N_HEADS * QK_PAD), F32), jax.ShapeDtypeStruct((s, N_HEADS * QK_PAD), F32),
                   jax.ShapeDtypeStruct((s, N_HEADS * V_DIM), F32)],
        in_specs=[pl.BlockSpec((s, QK_PAD), head), pl.BlockSpec((t, QK_PAD), kblock), pl.BlockSpec((t, V_DIM), kblock),
                  pl.BlockSpec((s, V_DIM), head), pl.BlockSpec((s, V_DIM), head), pl.BlockSpec((s, LANES), head)],
        out_specs=[pl.BlockSpec((s, QK_PAD), head), pl.BlockSpec((t, QK_PAD), kblock), pl.BlockSpec((t, V_DIM), kblock)],
        compiler_params=_cparams(2, VMEM_MID),
    )


def _o_fwd(x, o, wo, name):
    s = x.shape[0]
    t = min(512, s)

    def body(x_ref, o_ref, wo_ref, xo_ref):
        xo_ref[...] = x_ref[...] + _nn(o_ref[...], wo_ref[...])

    return pl.pallas_call(
        body, name=name, grid=(s // t,),
        out_shape=jax.ShapeDtypeStruct((s, D_MODEL), F32),
        in_specs=[_rows(t, D_MODEL), _rows(t, D_MODEL), _full((D_MODEL, D_MODEL))],
        out_specs=_rows(t, D_MODEL),
        compiler_params=_cparams(1, VMEM_MID),
    )(x, o, wo)


def _o_bwd(dx, wo, name, deps=()):
    s = dx.shape[0]
    t = min(512, s)

    def body(dx_ref, wo_ref, do_ref, dxb_ref):
        dxb = dx_ref[...].astype(BF16)
        dxb_ref[...] = dxb
        do_ref[...] = _nt(dxb, wo_ref[...]).astype(BF16)

    tok = jax.ShapeDtypeStruct((s, D_MODEL), BF16)
    return _pcall(
        body, (dx, wo), deps, name=name, grid=(s // t,),
        out_shape=[tok, tok],
        in_specs=[_rows(t, D_MODEL), _full((D_MODEL, D_MODEL))],
        out_specs=[_rows(t, D_MODEL), _rows(t, D_MODEL)],
        compiler_params=_cparams(1, VMEM_MID),
    )


def _loss_head(y, target, name):
    s = y.shape[0]
    t = min(512, s)

    def body(y_ref, t_ref, dy_ref, sq_ref):
        e = y_ref[...] - t_ref[...]
        dy_ref[...] = e * (1.0 / D_MODEL)

        @pl.when(pl.program_id(0) == 0)
        def _():
            sq_ref[...] = jnp.zeros_like(sq_ref)

        sq_ref[...] += jnp.sum(e * e, axis=0, keepdims=True)

    return pl.pallas_call(
        body, name=name, grid=(s // t,),
        out_shape=[jax.ShapeDtypeStruct((s, D_MODEL), F32), jax.ShapeDtypeStruct((1, D_MODEL), F32)],
        in_specs=[_rows(t, D_MODEL), _rows(t, D_MODEL)],
        out_specs=[_rows(t, D_MODEL), _full((1, D_MODEL))],
        compiler_params=_cparams(1),
    )(y, target)


def _adamw(w, g, m, v, name):
    shape = w.shape
    c = shape[-1]
    r = math.prod(shape[:-1])
    tb = r
    for cand in (512, 256, 128):
        if r % cand == 0 and r > cand:
            tb = cand
            break

    def body(w_ref, g_ref, m_ref, v_ref, d_ref, mo_ref, vo_ref):
        gr = g_ref[...]
        mn = ADAM_B1 * m_ref[...] + (1.0 - ADAM_B1) * gr
        vn = ADAM_B2 * v_ref[...] + (1.0 - ADAM_B2) * (gr * gr)
        m_hat = mn / (1.0 - ADAM_B1 ** ADAM_STEP)
        v_hat = vn / (1.0 - ADAM_B2 ** ADAM_STEP)
        d_ref[...] = -ADAM_LR * (m_hat / (jnp.sqrt(v_hat) + ADAM_EPS) + ADAM_WD * w_ref[...])
        mo_ref[...] = mn
        vo_ref[...] = vn

    spec = pl.BlockSpec((tb, c), lambda i: (i, 0))
    flat = jax.ShapeDtypeStruct((r, c), F32)
    outs = pl.pallas_call(
        body, name=name, grid=(r // tb,),
        out_shape=[flat, flat, flat],
        in_specs=[spec] * 4, out_specs=[spec] * 3,
        compiler_params=_cparams(1),
    )(w.reshape(r, c), g.reshape(r, c), m.reshape(r, c), v.reshape(r, c))
    return [a.reshape(shape) for a in outs]


def _pad_cols(a, width):
    return jnp.pad(a, [(0, 0)] * (a.ndim - 1) + [(0, width - a.shape[-1])])


def _owner_view(a, sz):
    return a.reshape(a.shape[0], N_CHIPS, 2, sz, a.shape[-1])


def kernel(x, positions, ln_mix_a, w_pool, b_pool, pool_scale, ln_ffn, w_gate, w_up, w_down, ln_kv, w_dkv, g_kv_latent, w_uk, w_uv, g_k, ln_mix_b, w_dq, g_q_latent, w_uq, g_q, w_o, loss_target, m_ln_mix_a, m_w_pool, m_b_pool, m_pool_scale, m_ln_ffn, m_w_gate, m_w_up, m_w_down, m_ln_kv, m_w_dkv, m_g_kv_latent, m_w_uk, m_w_uv, m_g_k, m_ln_mix_b, m_w_dq, m_g_q_latent, m_w_uq, m_g_q, m_w_o, v_ln_mix_a, v_w_pool, v_b_pool, v_pool_scale, v_ln_ffn, v_w_gate, v_w_up, v_w_down, v_ln_kv, v_w_dkv, v_g_kv_latent, v_w_uk, v_w_uv, v_g_k, v_ln_mix_b, v_w_dq, v_g_q_latent, v_w_uq, v_g_q, v_w_o):
    weights = dict(ln_mix_a=ln_mix_a, w_pool=w_pool, b_pool=b_pool, pool_scale=pool_scale, ln_ffn=ln_ffn,
                   w_gate=w_gate, w_up=w_up, w_down=w_down, ln_kv=ln_kv, w_dkv=w_dkv, g_kv_latent=g_kv_latent,
                   w_uk=w_uk, w_uv=w_uv, g_k=g_k, ln_mix_b=ln_mix_b, w_dq=w_dq, g_q_latent=g_q_latent,
                   w_uq=w_uq, g_q=g_q, w_o=w_o)
    mom1 = dict(ln_mix_a=m_ln_mix_a, w_pool=m_w_pool, b_pool=m_b_pool, pool_scale=m_pool_scale, ln_ffn=m_ln_ffn,
                w_gate=m_w_gate, w_up=m_w_up, w_down=m_w_down, ln_kv=m_ln_kv, w_dkv=m_w_dkv,
                g_kv_latent=m_g_kv_latent, w_uk=m_w_uk, w_uv=m_w_uv, g_k=m_g_k, ln_mix_b=m_ln_mix_b, w_dq=m_w_dq,
                g_q_latent=m_g_q_latent, w_uq=m_w_uq, g_q=m_g_q, w_o=m_w_o)
    mom2 = dict(ln_mix_a=v_ln_mix_a, w_pool=v_w_pool, b_pool=v_b_pool, pool_scale=v_pool_scale, ln_ffn=v_ln_ffn,
                w_gate=v_w_gate, w_up=v_w_up, w_down=v_w_down, ln_kv=v_ln_kv, w_dkv=v_w_dkv,
                g_kv_latent=v_g_kv_latent, w_uk=v_w_uk, w_uv=v_w_uv, g_k=v_g_k, ln_mix_b=v_ln_mix_b, w_dq=v_w_dq,
                g_q_latent=v_g_q_latent, w_uq=v_w_uq, g_q=v_g_q, w_o=v_w_o)
    names = list(weights)
    dev = 4 * lax.axis_index("x") + 2 * lax.axis_index("y") + lax.axis_index("c")
    core = lax.axis_index("c").astype(jnp.int32).reshape(1)
    chip = (2 * lax.axis_index("x") + lax.axis_index("y")).astype(jnp.int32).reshape(1)

    xs = x[0]
    target = loss_target[0]
    cos, sin = _rope_tables(positions[0])

    small_sh = jnp.concatenate([ln_mix_a.reshape(1, -1), pool_scale.reshape(1, -1), b_pool.reshape(1, -1)], axis=1)
    wp_g, small_g = _all_gather([w_pool.astype(BF16), small_sh], [2, 0], "gather_first")
    wp_all = wp_g.reshape(2, 4, GROUP_DIM, GROUP_DIM)
    small_g = small_g.reshape(N_DEV, 3, 2, LANES)
    ln_a_all = small_g[:, 0].transpose(1, 0, 2).reshape(2, 1, D_MODEL)
    sc_all = small_g[:, 1].transpose(1, 0, 2).reshape(2, 1, D_MODEL)
    bp_all = small_g[:, 2].reshape(N_DEV, 2, 4, 32).transpose(1, 2, 0, 3).reshape(2, 1, D_MODEL)

    def placed(shard):
        buf = lax.empty((shard.shape[0], N_DEV) + shard.shape[1:], shard.dtype)
        return lax.dynamic_update_slice(buf, shard[:, None], (0, dev, 0, 0))

    groups = {f"ffn{l}": [placed(jnp.stack([w_gate[l].T, w_up[l].T, w_down[l]]).astype(BF16))] for l in range(4)}
    groups["att"] = [placed(a.astype(BF16)) for a in (
        w_dkv[None, :, :KV_RANK], _pad_cols(w_dkv[None, :, KV_RANK:], LANES), w_uk[None], w_uv[None],
        w_dq, _pad_cols(w_uq, QK_PAD), w_o)]
    def spread_start(nm, deps):
        return _copies_start(groups[nm], len(groups[nm]), _gather_spread, f"spread_{nm}", deps=deps)

    def spread_wait(nm, state, after):
        ssem, rsem, bufs, _ = state
        return _copies_wait(bufs, ssem, rsem, after, _blocks_moved(4), f"spread_done_{nm}")

    def relay_start(nm, bufs, deps=()):
        return _copies_start(bufs, len(bufs), _gather_relay, f"relay_{nm}", deps=deps)

    def relay_wait(nm, state, after):
        ssem, rsem, bufs, _ = state
        return _copies_wait(bufs, ssem, rsem, after, _blocks_moved(3), f"relay_done_{nm}")

    gkn = g_k[:NOPE].reshape(1, NOPE)
    gkr = _pad_cols(g_k[NOPE:].reshape(1, ROPE), LANES)
    gl = g_kv_latent.reshape(1, KV_RANK)
    lnkv = ln_kv.reshape(1, D_MODEL)

    x_in, x_mid, pooled, gates, ups, w_ffn = [], [], [], [], [], []
    qs, outs, lses = [], [], []

    def mixer(l, cur, deps):
        x_in.append(cur)
        mid, dsave = _mix_fwd(cur, ln_a_all[l], wp_all[l], bp_all[l], sc_all[l], f"mix_fwd{l}", deps=deps)
        pooled.append(dsave)
        x_mid.append(mid)
        return mid

    def q_args(j):
        return (ln_mix_b[j].reshape(1, -1), wdq_all[j], g_q_latent[j].reshape(1, -1), wuq_all[j],
                g_q[j, :NOPE].reshape(1, -1), _pad_cols(g_q[j, NOPE:].reshape(1, -1), LANES), cos, sin)

    def attention(j, cur, deps):
        x_in.append(cur)
        q = _q_fwd(cur, *q_args(j), f"q_fwd{j}", deps=deps)
        o, lse = _att_fwd(q, k_sh, v_sh, f"att_fwd{j}")
        mid = _o_fwd(cur, o, wo_all[j], f"o_fwd{j}")
        qs.append(q)
        outs.append(o)
        lses.append(lse)
        x_mid.append(mid)
        return mid

    def ffn(l, mid, relayed):
        w_l = relayed[0].reshape(3, D_FF, D_MODEL)
        w_ffn.append(w_l)
        cur, gate, up = _ffn_fwd(mid, ln_ffn[l].reshape(1, -1), w_l, f"ffn_fwd{l}")
        gates.append(gate)
        ups.append(up)
        return cur

    sp0 = spread_start("ffn0", [small_g])
    mid = mixer(0, xs, [sp0[3]])
    landed0 = spread_wait("ffn0", sp0, mid)
    sp1 = spread_start("ffn1", [landed0[0]])
    rl0 = relay_start("ffn0", landed0, [sp1[3]])
    cur = ffn(0, mid, relay_wait("ffn0", rl0, rl0[3]))

    landed1 = spread_wait("ffn1", sp1, cur)
    sp_att = spread_start("att", [landed1[0]])
    sp2 = spread_start("ffn2", [landed1[0]])
    rl1 = relay_start("ffn1", landed1, [sp_att[3], sp2[3]])
    mid = mixer(1, cur, [rl1[3]])
    cur = ffn(1, mid, relay_wait("ffn1", rl1, mid))
    x_kv = cur

    landed_att = spread_wait("att", sp_att, cur)
    landed2 = spread_wait("ffn2", sp2, cur)
    sp3 = spread_start("ffn3", [landed2[0]])
    rl_att = relay_start("att", landed_att, [sp3[3]])
    rl2 = relay_start("ffn2", landed2, [sp3[3]])
    att_bufs = relay_wait("att", rl_att, rl2[3])
    wc = att_bufs[0].reshape(D_MODEL, KV_RANK)
    wpe = att_bufs[1].reshape(D_MODEL, LANES)
    wuk_g = att_bufs[2].reshape(N_HEADS, KV_RANK, NOPE)
    wuv_g = att_bufs[3].reshape(N_HEADS, KV_RANK, V_DIM)
    wdq_all = att_bufs[4].reshape(2, D_MODEL, Q_RANK)
    wuq_all = att_bufs[5]
    wo_all = att_bufs[6].reshape(2, D_MODEL, D_MODEL)
    k_sh, v_sh = _kv_fwd(cur, lnkv, wc, wpe, gl, wuk_g, wuv_g, gkn, gkr, cos, sin, "kv_fwd")
    mid = attention(0, cur, [])
    cur = ffn(2, mid, relay_wait("ffn2", rl2, mid))

    landed3 = spread_wait("ffn3", sp3, cur)
    rl3 = relay_start("ffn3", landed3)
    mid = attention(1, cur, [rl3[3]])
    cur = ffn(3, mid, relay_wait("ffn3", rl3, mid))

    dx, sq_cols = _loss_head(cur, target, "loss_head")

    small = {}
    sizes = dict(ffn0=FF_SHARD, ffn1=FF_SHARD, ffn2=FF_SHARD, ffn3=FF_SHARD, wo=128, kv512=128, dkv_pe=128,
                 wdq=128, wuqT=QK_PAD, wpool=32)
    big = dict(wo=lax.empty((2, D_MODEL, D_MODEL), BF16), kv512=lax.empty((3, D_MODEL, KV_RANK), BF16),
               dkv_pe=lax.empty((1, D_MODEL, LANES), BF16), wdq=lax.empty((2, D_MODEL, Q_RANK), BF16),
               wuqT=lax.empty((2, N_HEADS * QK_PAD, Q_RANK), BF16), wpool=lax.empty((8, GROUP_DIM, GROUP_DIM), BF16))
    for l in range(4):
        big[f"ffn{l}"] = lax.empty((3, D_FF, D_MODEL), BF16)
    red = {}

    def pair_start(nms, tag):
        arrs = []
        for nm in nms:
            view = _owner_view(big[nm], sizes[nm])
            arrs += [view, lax.empty((view.shape[0], N_CHIPS) + view.shape[3:], BF16)]
        return nms, tag, _copies_start(arrs, len(nms), _pair_send, f"pair_start_{tag}")

    def chip_start(state, after):
        nms, tag, (ssem, rsem, arrs, _) = state
        arrs = _copies_wait(arrs, ssem, rsem, after, _landed, f"pair_done_{tag}")
        out = []
        for t, nm in enumerate(nms):
            part = _pair_sum(arrs[2 * t], arrs[2 * t + 1], core, f"pair_sum_{nm}")
            out += [part, lax.empty((3, part.shape[0]) + part.shape[2:], BF16)]
        return nms, tag, _copies_start(out, len(nms), _chip_send, f"chip_start_{tag}")

    def chip_finish(state, after):
        nms, tag, (ssem, rsem, arrs, _) = state
        arrs = _copies_wait(arrs, ssem, rsem, after, _landed, f"chip_done_{tag}")
        for t, nm in enumerate(nms):
            red[nm] = _chip_sum(arrs[2 * t], arrs[2 * t + 1], chip, f"chip_sum_{nm}")

    dks, dvs = [], []
    pending = None
    bwd_deps = []
    for l in (3, 2, 1, 0):
        key = f"ffn{l}"
        dx, act, dgb, dub, hn, dyb, dln = _ffn_bwd(x_mid[l], dx, gates[l], ups[l], ln_ffn[l].reshape(1, -1),
                                                     w_ffn[l], f"ffn_bwd{l}", deps=bwd_deps)
        bwd_deps = []
        small[f"ln_ffn{l}"] = dln
        if l == 1:
            att_chip = chip_start(att_pair, dx)
            tn_deps = [att_chip[2][3]]
        else:
            tn_deps = []
        if pending:
            chip_finish(pending, dx)
            pending = None
        big[key] = _tn_matmul(dgb, hn, big[key], 0, f"dw_gate{l}", m_chunk=FF_HALF, deps=tn_deps)
        big[key] = _tn_matmul(dub, hn, big[key], 1, f"dw_up{l}", m_chunk=FF_HALF)
        big[key] = _tn_matmul(act, dyb, big[key], 2, f"dw_down{l}", m_chunk=FF_HALF)
        if l == 1:
            chip_finish(att_chip, big[key])
        ffn_pair = pair_start([key], key)
        if l >= 2:
            j = l - 2
            do, dxb = _o_bwd(dx, wo_all[j], f"o_bwd{j}", deps=[ffn_pair[2][3]])
            big["wo"] = _tn_matmul(outs[j], dxb, big["wo"], j, f"dw_o{j}")
            ffn_chip = chip_start(ffn_pair, big["wo"])
            dq, dk, dv = _att_bwd(qs[j], k_sh, v_sh, do, outs[j], lses[j], f"att_bwd{j}", deps=[ffn_chip[2][3]])
            chip_finish(ffn_chip, dq)
            dks.append(dk)
            dvs.append(dv)
            dx, hnq, cqn, dqa, dcq, dln, dgql, dgqn, dgqr = _q_bwd(x_in[l], dx, dq, *q_args(j), f"q_bwd{j}")
            small[f"ln_mix_b{j}"] = dln
            small[f"g_q_latent{j}"] = dgql
            small[f"g_q{j}"] = jnp.concatenate([dgqn, dgqr[:, :ROPE]], axis=1)
            big["wdq"] = _tn_matmul(hnq, dcq, big["wdq"], j, f"dw_dq{j}")
            big["wuqT"] = _tn_matmul(dqa, cqn, big["wuqT"], j, f"dw_uq{j}")
            if l == 2:
                (dx, hnk, cn, dknb, dvb, dccb, dpeb, dlnkv, dgl, dgkn, dgkr) = _kv_bwd(
                    x_kv, dx, dks, dvs, lnkv, wc, wpe, gl, wuk_g, wuv_g, gkn, gkr, cos, sin, "kv_bwd")
                small["ln_kv"] = dlnkv
                small["g_kv_latent"] = dgl
                small["g_k"] = jnp.concatenate([dgkn, dgkr[:, :ROPE]], axis=1)
                big["kv512"] = _tn_matmul(dknb, cn, big["kv512"], 0, "dw_uk")
                big["kv512"] = _tn_matmul(dvb, cn, big["kv512"], 1, "dw_uv")
                big["kv512"] = _tn_matmul(hnk, dccb, big["kv512"], 2, "dw_dkv_c")
                big["dkv_pe"] = _tn_matmul(hnk, dpeb, big["dkv_pe"], 0, "dw_dkv_pe")
                att_pair = pair_start(["wo", "kv512", "dkv_pe", "wdq", "wuqT"], "att")
                bwd_deps = [att_pair[2][3]]
        else:
            dx, dyp, dsc, db, dln = _mix_bwd(x_in[l], dx, pooled[l], ln_a_all[l], wp_all[l], bp_all[l], sc_all[l],
                                             f"mix_bwd{l}", deps=[ffn_pair[2][3]])
            small[f"ln_mix_a{l}"] = dln
            small[f"pool_scale{l}"] = dsc
            small[f"b_pool{l}"] = db
            ffn_chip = chip_start(ffn_pair, dx)
            big["wpool"] = _tn_matmul(pooled[l], dyp, big["wpool"], 4 * l, f"dw_pool{l}", groups=4,
                                      deps=[ffn_chip[2][3]])
            if l == 1:
                pending = ffn_chip
            else:
                chip_finish(ffn_chip, big["wpool"])
    grad_x = dx[None]
    pool_pair = pair_start(["wpool"], "wpool")
    pool_chip = chip_start(pool_pair, pool_pair[2][3])
    chip_finish(pool_chip, pool_chip[2][3])

    vec_names = (["loss"] + [f"ln_ffn{l}" for l in range(4)] + ["ln_kv", "g_kv_latent", "g_k"]
                 + [f"{p}{j}" for p in ("ln_mix_b", "g_q_latent", "g_q") for j in range(2)]
                 + [f"{p}{l}" for p in ("ln_mix_a", "pool_scale", "b_pool") for l in range(2)])
    small["loss"] = sq_cols
    widths = [small[nm].shape[1] for nm in vec_names]
    padded = [-(-w // LANES) * LANES for w in widths]
    packed = jnp.concatenate([_pad_cols(small[nm], pw) for nm, pw in zip(vec_names, padded)], axis=1)
    (all_vecs,) = _all_gather([packed], [0], "gather_vectors")
    total = _sum_lead(all_vecs, "sum_vectors")
    vec = {}
    off = 0
    for nm, w, pw in zip(vec_names, widths, padded):
        vec[nm] = total[0, off:off + w]
        off += pw
    loss = 0.5 * jnp.sum(vec["loss"]) * (1.0 / D_MODEL)

    def own_cols(full, width):
        return lax.dynamic_slice_in_dim(full, dev * width, width, axis=full.ndim - 1)

    grads = dict(
        ln_mix_a=own_cols(jnp.stack([vec["ln_mix_a0"], vec["ln_mix_a1"]]), LANES),
        w_pool=red["wpool"].reshape(2, 4, 32, GROUP_DIM),
        b_pool=own_cols(jnp.stack([vec["b_pool0"], vec["b_pool1"]]).reshape(2, 4, GROUP_DIM), 32),
        pool_scale=own_cols(jnp.stack([vec["pool_scale0"], vec["pool_scale1"]]), LANES),
        ln_ffn=jnp.stack([vec[f"ln_ffn{l}"] for l in range(4)]),
        w_gate=jnp.stack([red[f"ffn{l}"][0] for l in range(4)]).transpose(0, 2, 1),
        w_up=jnp.stack([red[f"ffn{l}"][1] for l in range(4)]).transpose(0, 2, 1),
        w_down=jnp.stack([red[f"ffn{l}"][2] for l in range(4)]),
        ln_kv=vec["ln_kv"],
        w_dkv=jnp.concatenate([red["kv512"][2], red["dkv_pe"][0][:, :ROPE]], axis=1),
        g_kv_latent=vec["g_kv_latent"],
        w_uk=red["kv512"][0].T,
        w_uv=red["kv512"][1].T,
        g_k=vec["g_k"],
        ln_mix_b=jnp.stack([vec["ln_mix_b0"], vec["ln_mix_b1"]]),
        w_dq=red["wdq"],
        g_q_latent=jnp.stack([vec["g_q_latent0"], vec["g_q_latent1"]]),
        w_uq=red["wuqT"].transpose(0, 2, 1)[:, :, :QK_DIM],
        g_q=jnp.stack([vec["g_q0"], vec["g_q1"]]),
        w_o=red["wo"],
    )

    deltas, new_m, new_v = {}, {}, {}
    for nm in names:
        w = weights[nm]
        shape = w.shape if w.ndim > 1 else (1, w.shape[0])
        d, mo, vo = _adamw(w.reshape(shape), grads[nm].reshape(shape), mom1[nm].reshape(shape),
                           mom2[nm].reshape(shape), f"adamw_{nm}")
        deltas[nm], new_m[nm], new_v[nm] = d.reshape(w.shape), mo.reshape(w.shape), vo.reshape(w.shape)

    return (loss, grad_x, *[grads[nm].reshape(weights[nm].shape) for nm in names], *[deltas[nm] for nm in names],
            *[new_m[nm] for nm in names], *[new_v[nm] for nm in names])
```

```python
import functools
import math

import jax
import jax.numpy as jnp
from jax import lax
from jax.experimental import pallas as pl
from jax.experimental.pallas import tpu as pltpu

F32 = jnp.float32
BF16 = jnp.bfloat16
MESH = pl.DeviceIdType.MESH

D_MODEL = 1024
D_FF = 2816
N_DEV = 8
N_CHIPS = 4
FF_SHARD = D_FF // N_DEV
FF_HALF = D_FF // 2
N_HEADS = 8
NOPE = 128
ROPE = 64
QK_DIM = NOPE + ROPE
QK_PAD = 256
V_DIM = 128
Q_RANK = 256
KV_RANK = 512
POOL_WINDOWS = (2, 4, 8, 16)
GROUP_DIM = 256
HALO = 128
CHUNK = 64
ROPE_THETA = 10000.0
EPS = 1e-6
LANES = 128

ADAM_LR = 0.001
ADAM_B1 = 0.9
ADAM_B2 = 0.999
ADAM_EPS = 1e-08
ADAM_WD = 0.01
ADAM_STEP = 10

PROJ_ROWS = 256
VMEM_BIG = 56 * 2**20
VMEM_MID = 40 * 2**20


def _nn(a, b):
    return lax.dot_general(a, b, (((1,), (0,)), ((), ())), preferred_element_type=F32)


def _nt(a, b):
    return lax.dot_general(a, b, (((1,), (1,)), ((), ())), preferred_element_type=F32)


def _tn(a, b):
    return lax.dot_general(a, b, (((0,), (0,)), ((), ())), preferred_element_type=F32)


def _rms(x, g, n):
    r = lax.rsqrt(jnp.sum(x * x, axis=-1, keepdims=True) * (1.0 / n) + EPS)
    return (x * r) * g, r


def _rms_bwd(x, r, g, dy, n):
    u = dy * g
    s = jnp.sum(x * u, axis=-1, keepdims=True) * (1.0 / n)
    dx = r * u - x * (r * r * r * s)
    dg = jnp.sum(dy * (x * r), axis=0, keepdims=True)
    return dx, dg


def _swap_perm():
    i = lax.broadcasted_iota(jnp.int32, (LANES, LANES), 0)
    j = lax.broadcasted_iota(jnp.int32, (LANES, LANES), 1)
    half = ROPE // 2
    hit = ((j < half) & (i == j + half)) | ((j >= half) & (j < ROPE) & (i == j - half))
    return jnp.where(hit, 1.0, 0.0).astype(BF16)


def _swap_halves(z, perm):
    hi = z.astype(BF16)
    lo = (z - hi.astype(F32)).astype(BF16)
    return _nn(hi, perm) + _nn(lo, perm)


def _sigmoid(x):
    return 1.0 / (1.0 + jnp.exp(-x))


def _cparams(n_grid, vmem=None):
    return pltpu.CompilerParams(dimension_semantics=("arbitrary",) * n_grid, vmem_limit_bytes=vmem)


def _rows(t, cols):
    return pl.BlockSpec((t, cols), lambda i: (i, 0))


def _full(shape):
    nd = len(shape)
    return pl.BlockSpec(shape, lambda *_: (0,) * nd)


ANY = pl.BlockSpec(memory_space=pl.ANY)


def _pcall(body, args, deps, *, in_specs, **kw):
    n_in, n_dep = len(args), len(deps)

    def ordered(*refs):
        body(*refs[:n_in], *refs[n_in + n_dep:])

    return pl.pallas_call(ordered, in_specs=list(in_specs) + [ANY] * n_dep, **kw)(*args, *deps)


def _place():
    x, y, c = lax.axis_index("x"), lax.axis_index("y"), lax.axis_index("c")
    return x, y, c


def _all_gather(shards, axes, name):
    n = len(shards)
    out_shape = [jax.ShapeDtypeStruct(s.shape[:a] + (N_DEV,) + s.shape[a:], s.dtype) for s, a in zip(shards, axes)]

    def body(*refs):
        ins, outs = refs[:n], refs[n:2 * n]
        send_sems, recv_sems, local_sems = refs[2 * n:]
        x, y, c = _place()
        me, sibling = (x, y, c), (x, y, 1 - c)
        chips = [(1 - x, y), (x, 1 - y), (1 - x, 1 - y)]

        def slot(t, dev):
            idx = 4 * dev[0] + 2 * dev[1] + dev[2]
            return outs[t].at[(slice(None),) * axes[t] + (idx,)]

        def copy(t, k, block, to, src=None):
            return pltpu.make_async_remote_copy(
                src_ref=slot(t, block) if src is None else src, dst_ref=slot(t, block),
                send_sem=send_sems.at[t, k], recv_sem=recv_sems.at[t, k],
                device_id=to, device_id_type=MESH)

        mine = [pltpu.make_async_copy(ins[t], slot(t, me), local_sems.at[t]) for t in range(n)]
        for cp in mine:
            cp.start()
        first = []
        for t in range(n):
            first.append(copy(t, 0, me, sibling, src=ins[t]))
            first += [copy(t, 1 + j, me, (*chip, c), src=ins[t]) for j, chip in enumerate(chips)]
        for cp in first:
            cp.start()
        passed = []
        for j, chip in enumerate(chips):
            for t in range(n):
                copy(t, 1 + j, (*chip, c), me).wait_recv()
                cp = copy(t, 4 + j, (*chip, c), sibling)
                cp.start()
                passed.append(cp)
        for t in range(n):
            copy(t, 0, sibling, me).wait_recv()
            for j, chip in enumerate(chips):
                copy(t, 4 + j, (*chip, 1 - c), me).wait_recv()
        for cp in first + passed:
            cp.wait_send()
        for cp in mine:
            cp.wait()

    return pl.pallas_call(
        body, name=name, out_shape=out_shape,
        in_specs=[ANY] * n, out_specs=[ANY] * n,
        scratch_shapes=[pltpu.SemaphoreType.DMA((n, 7)), pltpu.SemaphoreType.DMA((n, 7)),
                        pltpu.SemaphoreType.DMA((n,))],
    )(*shards)


HBM = pl.BlockSpec(memory_space=pltpu.HBM)
SEM = pl.BlockSpec(memory_space=pltpu.SEMAPHORE)
EFFECT = pltpu.SideEffectType.DATAFLOW_SIDE_EFFECTING


def _copies_start(arrays, n_sems, plan, name, deps=()):
    n, nd = len(arrays), len(deps)

    def body(*refs):
        for cp in plan(refs[:n], refs[n + nd], refs[n + nd + 1]):
            cp.start()
        refs[-1][...] = jnp.zeros_like(refs[-1])

    outs = pl.pallas_call(
        body, name=name,
        out_shape=(pltpu.SemaphoreType.DMA((n_sems,)), pltpu.SemaphoreType.DMA((n_sems,)),
                   *[pltpu.HBM(a.shape, a.dtype) for a in arrays], jax.ShapeDtypeStruct((8, LANES), F32)),
        in_specs=[HBM] * n + [ANY] * nd,
        out_specs=(SEM, SEM, *[HBM] * n, pl.BlockSpec(memory_space=pltpu.VMEM)),
        input_output_aliases={i: 2 + i for i in range(n)},
        compiler_params=pltpu.CompilerParams(has_side_effects=EFFECT),
    )(*[pltpu.with_memory_space_constraint(a, pltpu.HBM) for a in arrays], *deps)
    return outs[0], outs[1], list(outs[2:2 + n]), outs[-1]


def _copies_wait(arrays, send_sems, recv_sems, after, plan, name):
    n = len(arrays)

    def body(*refs):
        for cp in plan(refs[:n], refs[n], refs[n + 1]):
            cp.wait_send()
            cp.wait_recv()

    outs = pl.pallas_call(
        body, name=name,
        out_shape=tuple(pltpu.HBM(a.shape, a.dtype) for a in arrays),
        in_specs=[HBM] * n + [SEM, SEM, ANY], out_specs=tuple([HBM] * n),
        input_output_aliases={i: i for i in range(n)},
        compiler_params=pltpu.CompilerParams(has_side_effects=EFFECT),
    )(*arrays, send_sems, recv_sems, after)
    return list(outs)


def _remote(src, dst, send_sems, recv_sems, t, to):
    return pltpu.make_async_remote_copy(src_ref=src, dst_ref=dst, send_sem=send_sems.at[t], recv_sem=recv_sems.at[t],
                                        device_id=to, device_id_type=MESH)


def _dev_index(x, y, c):
    return 4 * x + 2 * y + c


def _gather_spread(bufs, send_sems, recv_sems):
    x, y, c = _place()
    mine = _dev_index(x, y, c)
    peers = [(x, y, 1 - c), (1 - x, y, c), (x, 1 - y, c), (1 - x, 1 - y, c)]
    return [_remote(g.at[k, mine], g.at[k, mine], send_sems, recv_sems, t, peer)
            for t, g in enumerate(bufs) for peer in peers for k in range(g.shape[0])]


def _gather_relay(bufs, send_sems, recv_sems):
    x, y, c = _place()
    blocks = [_dev_index(1 - x, y, c), _dev_index(x, 1 - y, c), _dev_index(1 - x, 1 - y, c)]
    return [_remote(g.at[k, b], g.at[k, b], send_sems, recv_sems, t, (x, y, 1 - c))
            for t, g in enumerate(bufs) for b in blocks for k in range(g.shape[0])]


def _blocks_moved(count):
    def plan(bufs, send_sems, recv_sems):
        x, y, c = _place()
        return [_remote(g.at[:, pl.ds(0, count)], g.at[:, pl.ds(0, count)], send_sems, recv_sems, t, (x, y, 1 - c))
                for t, g in enumerate(bufs)]
    return plan


def _pair_send(arrs, send_sems, recv_sems):
    x, y, c = _place()
    return [_remote(arrs[2 * t].at[p, k, 1 - c], arrs[2 * t + 1].at[p, k], send_sems, recv_sems, t, (x, y, 1 - c))
            for t in range(len(arrs) // 2) for p in range(arrs[2 * t].shape[0]) for k in range(N_CHIPS)]


def _chip_send(arrs, send_sems, recv_sems):
    x, y, c = _place()
    chips = [(1 - x, y), (x, 1 - y), (1 - x, 1 - y)]
    return [_remote(arrs[2 * t].at[p, 2 * px + py], arrs[2 * t + 1].at[j, p], send_sems, recv_sems, t, (px, py, c))
            for t in range(len(arrs) // 2) for j, (px, py) in enumerate(chips) for p in range(arrs[2 * t].shape[0])]


def _landed(arrs, send_sems, recv_sems):
    x, y, c = _place()
    return [_remote(arrs[2 * t + 1], arrs[2 * t + 1], send_sems, recv_sems, t, (x, y, 1 - c))
            for t in range(len(arrs) // 2)]


def _rows_per_step(rows, row_elems):
    best = 1
    for cand in range(1, rows + 1):
        if rows % cand == 0 and cand * row_elems <= 256 * 1024:
            best = cand
    return best


def _pair_sum(grad, landed, core, name):
    p, _, _, sz, c = grad.shape
    r = _rows_per_step(p * N_CHIPS, sz * c)

    def body(core_ref, g_ref, l_ref, o_ref):
        o_ref[...] = (g_ref[...].astype(F32) + l_ref[...].astype(F32)).astype(o_ref.dtype)

    out = pl.pallas_call(
        body, name=name,
        grid_spec=pltpu.PrefetchScalarGridSpec(
            num_scalar_prefetch=1, grid=(p * N_CHIPS // r,),
            in_specs=[pl.BlockSpec((r, None, sz, c), lambda i, cr: (i, cr[0], 0, 0)),
                      pl.BlockSpec((r, sz, c), lambda i, cr: (i, 0, 0))],
            out_specs=pl.BlockSpec((r, sz, c), lambda i, cr: (i, 0, 0))),
        out_shape=jax.ShapeDtypeStruct((p * N_CHIPS, sz, c), grad.dtype),
        compiler_params=_cparams(1),
    )(core, grad.reshape(p * N_CHIPS, 2, sz, c), landed.reshape(p * N_CHIPS, sz, c))
    return out.reshape(p, N_CHIPS, sz, c)


def _chip_sum(parts, landed, chip, name):
    p, _, sz, c = parts.shape
    r = _rows_per_step(p, sz * c)

    def body(chip_ref, a_ref, l_ref, o_ref):
        acc = a_ref[...].astype(F32)
        for j in range(3):
            acc = acc + l_ref[j].astype(F32)
        o_ref[...] = acc

    return pl.pallas_call(
        body, name=name,
        grid_spec=pltpu.PrefetchScalarGridSpec(
            num_scalar_prefetch=1, grid=(p // r,),
            in_specs=[pl.BlockSpec((r, None, sz, c), lambda i, cr: (i, cr[0], 0, 0)),
                      pl.BlockSpec((3, r, sz, c), lambda i, cr: (0, i, 0, 0))],
            out_specs=pl.BlockSpec((r, sz, c), lambda i, cr: (i, 0, 0))),
        out_shape=jax.ShapeDtypeStruct((p, sz, c), F32),
        compiler_params=_cparams(1),
    )(chip, parts, landed)


def _sum_lead(a, name, out_dtype=F32):
    k = a.shape[0]
    rest = a.shape[1:]
    r, c = rest[-2], rest[-1]
    lead = math.prod(rest[:-2])
    a3 = a.reshape(k, lead * r, c)
    rows = lead * r
    tb = rows
    for cand in (512, 256, 128, 64, 32, 16, 8):
        if rows % cand == 0 and rows > cand:
            tb = cand
            break

    def body(a_ref, o_ref):
        acc = a_ref[0].astype(F32)
        for i in range(1, k):
            acc = acc + a_ref[i].astype(F32)
        o_ref[...] = acc.astype(out_dtype)

    out = pl.pallas_call(
        body, name=name, grid=(rows // tb,),
        out_shape=jax.ShapeDtypeStruct((rows, c), out_dtype),
        in_specs=[pl.BlockSpec((k, tb, c), lambda i: (0, i, 0))],
        out_specs=pl.BlockSpec((tb, c), lambda i: (i, 0)),
        compiler_params=_cparams(1),
    )(a3)
    return out.reshape(rest)


def _bands(t, causal):
    r = lax.broadcasted_iota(jnp.int32, (t, t + HALO), 0)
    col = lax.broadcasted_iota(jnp.int32, (t, t + HALO), 1)
    diff = r + HALO - col if causal else col - r
    return jnp.stack([jnp.where((diff >= 0) & (diff < w), 1.0, 0.0) for w in POOL_WINDOWS]).astype(BF16)


def _split_dot(band, v):
    hi = v.astype(BF16)
    lo = (v - hi.astype(F32)).astype(BF16)
    return _nn(band, hi) + _nn(band, lo)


def _mix_fwd(x, g, wp, b, sc, name, deps=()):
    s = x.shape[0]
    t = min(256, s)
    rb = t // HALO

    def body(x_ref, xh_ref, g_ref, wp_ref, b_ref, sc_ref, band_ref, xo_ref, d_ref):
        i = pl.program_id(0)
        gg = g_ref[...]
        h, _ = _rms(x_ref[...], gg, D_MODEL)
        hh, _ = _rms(xh_ref[...], gg, D_MODEL)
        hh = jnp.where(i > 0, hh, 0.0)
        hext = jnp.concatenate([hh, h], axis=0)
        tok = i * t + lax.broadcasted_iota(jnp.int32, (t, 1), 0)
        for gi, w in enumerate(POOL_WINDOWS):
            sl = slice(gi * GROUP_DIM, (gi + 1) * GROUP_DIM)
            win = _split_dot(band_ref[gi], hext[:, sl])
            cnt = jnp.minimum(tok + 1, w).astype(F32)
            dbf = (win / cnt - h[:, sl]).astype(BF16)
            d_ref[:, sl] = dbf
            ypre = _nn(dbf, wp_ref[gi]) + b_ref[:, sl]
            xo_ref[:, sl] = x_ref[:, sl] + ypre * sc_ref[:, sl]

    return _pcall(
        body, (x, x, g, wp, b, sc, _bands(t, True)), deps, name=name, grid=(s // t,),
        out_shape=[jax.ShapeDtypeStruct((s, D_MODEL), F32), jax.ShapeDtypeStruct((s, D_MODEL), BF16)],
        in_specs=[_rows(t, D_MODEL),
                  pl.BlockSpec((HALO, D_MODEL), lambda i: (jnp.maximum(i * rb - 1, 0), 0)),
                  _full((1, D_MODEL)), _full((4, GROUP_DIM, GROUP_DIM)), _full((1, D_MODEL)), _full((1, D_MODEL)),
                  _full((4, t, t + HALO))],
        out_specs=[_rows(t, D_MODEL), _rows(t, D_MODEL)],
        compiler_params=_cparams(1, VMEM_MID),
    )


def _mix_bwd(x, dy, d, g, wp, b, sc, name, deps=()):
    s = x.shape[0]
    t = min(256, s)
    rb = t // HALO
    nb = s // t
    last_halo = s // HALO - 1

    def body(x_ref, dy_ref, dyn_ref, d_ref, g_ref, wp_ref, b_ref, sc_ref, band_ref,
             dx_ref, dyp_ref, dsc_ref, db_ref, dln_ref):
        i = pl.program_id(0)
        x = x_ref[...]
        gg = g_ref[...]
        dy = dy_ref[...]
        sc = sc_ref[...]
        dyp32 = dy * sc
        dyp = dyp32.astype(BF16)
        dyph = (dyn_ref[...] * sc).astype(BF16)
        dyp_ref[...] = dyp
        tok = i * t + lax.broadcasted_iota(jnp.int32, (t + HALO, 1), 0)
        dh, dsc = [], []
        for gi, w in enumerate(POOL_WINDOWS):
            sl = slice(gi * GROUP_DIM, (gi + 1) * GROUP_DIM)
            ypre = _nn(d_ref[:, sl], wp_ref[gi]) + b_ref[:, sl]
            dsc.append(jnp.sum(dy[:, sl] * ypre, axis=0, keepdims=True))
            dd = _nt(dyp[:, sl], wp_ref[gi])
            ddh = jnp.where(i < nb - 1, _nt(dyph[:, sl], wp_ref[gi]), 0.0)
            cnt = jnp.minimum(tok + 1, w).astype(F32)
            ddext = jnp.concatenate([dd, ddh], axis=0) / cnt
            dh.append(_split_dot(band_ref[gi], ddext) - dd)
        dh = jnp.concatenate(dh, axis=1)
        _, r = _rms(x, gg, D_MODEL)
        dxn, dg = _rms_bwd(x, r, gg, dh, D_MODEL)
        dx_ref[...] = dy + dxn

        @pl.when(i == 0)
        def _():
            dsc_ref[...] = jnp.zeros_like(dsc_ref)
            db_ref[...] = jnp.zeros_like(db_ref)
            dln_ref[...] = jnp.zeros_like(dln_ref)

        dsc_ref[...] += jnp.concatenate(dsc, axis=1)
        db_ref[...] += jnp.sum(dyp32, axis=0, keepdims=True)
        dln_ref[...] += dg

    vec = jax.ShapeDtypeStruct((1, D_MODEL), F32)
    return _pcall(
        body, (x, dy, dy, d, g, wp, b, sc, _bands(t, False)), deps, name=name, grid=(nb,),
        out_shape=[jax.ShapeDtypeStruct((s, D_MODEL), F32), jax.ShapeDtypeStruct((s, D_MODEL), BF16), vec, vec, vec],
        in_specs=[_rows(t, D_MODEL), _rows(t, D_MODEL),
                  pl.BlockSpec((HALO, D_MODEL), lambda i: (jnp.minimum((i + 1) * rb, last_halo), 0)),
                  _rows(t, D_MODEL),
                  _full((1, D_MODEL)), _full((4, GROUP_DIM, GROUP_DIM)), _full((1, D_MODEL)), _full((1, D_MODEL)),
                  _full((4, t, t + HALO))],
        out_specs=[_rows(t, D_MODEL), _rows(t, D_MODEL), _full((1, D_MODEL)), _full((1, D_MODEL)), _full((1, D_MODEL))],
        compiler_params=_cparams(1, VMEM_MID),
    )


def _load_weights(w_hbm, w_vmem, sem):
    @pl.when(pl.program_id(0) == 0)
    def _():
        cp = pltpu.make_async_copy(w_hbm, w_vmem, sem)
        cp.start()
        cp.wait()


def _ffn_fwd(x, g, w, name):
    s = x.shape[0]
    t = min(512, s)

    def body(x_ref, g_ref, w_hbm, xo_ref, gate_ref, up_ref, w_ref, sem):
        _load_weights(w_hbm, w_ref, sem)
        x = x_ref[...]
        hn = _rms(x, g_ref[...], D_MODEL)[0].astype(BF16)
        acc = x
        for c in range(2):
            rs = slice(c * FF_HALF, (c + 1) * FF_HALF)
            gt = _nt(hn, w_ref[0, rs, :])
            up = _nt(hn, w_ref[1, rs, :])
            gate_ref[:, rs] = gt.astype(BF16)
            up_ref[:, rs] = up.astype(BF16)
            act = ((gt * _sigmoid(gt)) * up).astype(BF16)
            acc = acc + _nn(act, w_ref[2, rs, :])
        xo_ref[...] = acc

    hid = jax.ShapeDtypeStruct((s, D_FF), BF16)
    return pl.pallas_call(
        body, name=name, grid=(s // t,),
        out_shape=[jax.ShapeDtypeStruct((s, D_MODEL), F32), hid, hid],
        in_specs=[_rows(t, D_MODEL), _full((1, D_MODEL)), ANY],
        out_specs=[_rows(t, D_MODEL), _rows(t, D_FF), _rows(t, D_FF)],
        scratch_shapes=[pltpu.VMEM((3, D_FF, D_MODEL), BF16), pltpu.SemaphoreType.DMA],
        compiler_params=_cparams(1, VMEM_BIG),
    )(x, g, w)


def _ffn_bwd(x, dy, gate, up, g, w, name, deps=()):
    s = x.shape[0]
    t = min(256, s)

    def body(x_ref, dy_ref, gate_ref, up_ref, g_ref, w_hbm,
             dx_ref, act_ref, dg_ref, du_ref, hn_ref, dyb_ref, dln_ref, w_ref, sem):
        _load_weights(w_hbm, w_ref, sem)
        x = x_ref[...]
        gg = g_ref[...]
        y, r = _rms(x, gg, D_MODEL)
        hn = y.astype(BF16)
        hn_ref[...] = hn
        dy = dy_ref[...]
        dyb = dy.astype(BF16)
        dyb_ref[...] = dyb
        dh = jnp.zeros((t, D_MODEL), F32)
        for c in range(2):
            rs = slice(c * FF_HALF, (c + 1) * FF_HALF)
            gt = gate_ref[:, rs].astype(F32)
            u = up_ref[:, rs].astype(F32)
            sg = _sigmoid(gt)
            sl = gt * sg
            act_ref[:, rs] = (sl * u).astype(BF16)
            dact = _nt(dyb, w_ref[2, rs, :])
            dg = (dact * u * (sg * (1.0 + gt * (1.0 - sg)))).astype(BF16)
            du = (dact * sl).astype(BF16)
            dg_ref[:, rs] = dg
            du_ref[:, rs] = du
            dh = dh + _nn(dg, w_ref[0, rs, :]) + _nn(du, w_ref[1, rs, :])
        dxn, dgl = _rms_bwd(x, r, gg, dh, D_MODEL)
        dx_ref[...] = dy + dxn

        @pl.when(pl.program_id(0) == 0)
        def _():
            dln_ref[...] = jnp.zeros_like(dln_ref)

        dln_ref[...] += dgl

    hid = jax.ShapeDtypeStruct((s, D_FF), BF16)
    tok = jax.ShapeDtypeStruct((s, D_MODEL), BF16)
    return _pcall(
        body, (x, dy, gate, up, g, w), deps, name=name, grid=(s // t,),
        out_shape=[jax.ShapeDtypeStruct((s, D_MODEL), F32), hid, hid, hid, tok, tok,
                   jax.ShapeDtypeStruct((1, D_MODEL), F32)],
        in_specs=[_rows(t, D_MODEL), _rows(t, D_MODEL), _rows(t, D_FF), _rows(t, D_FF), _full((1, D_MODEL)), ANY],
        out_specs=[_rows(t, D_MODEL), _rows(t, D_FF), _rows(t, D_FF), _rows(t, D_FF),
                   _rows(t, D_MODEL), _rows(t, D_MODEL), _full((1, D_MODEL))],
        scratch_shapes=[pltpu.VMEM((3, D_FF, D_MODEL), BF16), pltpu.SemaphoreType.DMA],
        compiler_params=_cparams(1, VMEM_BIG),
    )


def _tn_matmul(a, b, into, p0, name, groups=1, m_chunk=None, deps=()):
    s = a.shape[0]
    m, n = a.shape[1] // groups, b.shape[1] // groups
    assert into.shape[1:] == (m, n)
    mc = m if m_chunk is None else m_chunk
    nm = m // mc
    t = min(1024, s)
    nt = s // t

    def body(a_ref, b_ref, into_ref, o_ref, acc):
        ti = pl.program_id(2)

        @pl.when(ti == 0)
        def _():
            acc[...] = jnp.zeros_like(acc)

        acc[...] += _tn(a_ref[...], b_ref[...])

        @pl.when(ti == nt - 1)
        def _():
            o_ref[...] = acc[...].astype(o_ref.dtype)

    return _pcall(
        body, (a, b, into), deps, name=name, grid=(groups, nm, nt),
        out_shape=jax.ShapeDtypeStruct(into.shape, into.dtype),
        in_specs=[pl.BlockSpec((t, mc), lambda gi, mi, ti: (ti, gi * nm + mi)),
                  pl.BlockSpec((t, n), lambda gi, mi, ti: (ti, gi)), ANY],
        out_specs=pl.BlockSpec((None, mc, n), lambda gi, mi, ti: (p0 + gi, mi, 0)),
        scratch_shapes=[pltpu.VMEM((mc, n), F32)],
        input_output_aliases={2: 0},
        compiler_params=_cparams(3, VMEM_MID),
    )


def _rope_tables(positions):
    half = ROPE // 2
    inv = ROPE_THETA ** (-jnp.arange(half, dtype=F32) * 2.0 / ROPE)
    ang = positions.astype(F32)[:, None] * inv
    cos, sin = jnp.cos(ang), jnp.sin(ang)
    zero = jnp.zeros((positions.shape[0], LANES - ROPE), F32)
    return jnp.concatenate([cos, cos, zero], axis=1), jnp.concatenate([-sin, sin, zero], axis=1)


def _kv_specs(t):
    return [_full((1, D_MODEL)), _full((D_MODEL, KV_RANK)), _full((D_MODEL, LANES)), _full((1, KV_RANK)),
            _full((N_HEADS, KV_RANK, NOPE)), _full((N_HEADS, KV_RANK, V_DIM)),
            _full((1, NOPE)), _full((1, LANES)), _rows(t, LANES), _rows(t, LANES)]


def _kv_fwd(x, ln, wc, wpe, gl, wuk, wuv, gkn, gkr, cos, sin, name, deps=()):
    s = x.shape[0]
    t = min(PROJ_ROWS, s)

    def body(x_ref, ln_ref, wc_ref, wpe_ref, gl_ref, wuk_ref, wuv_ref, gkn_ref, gkr_ref, cos_ref, sin_ref,
             k_ref, v_ref):
        hn = _rms(x_ref[...], ln_ref[...], D_MODEL)[0].astype(BF16)
        clat = _nn(hn, wc_ref[...])
        kpe = _nn(hn, wpe_ref[...])
        cn = _rms(clat, gl_ref[...], KV_RANK)[0].astype(BF16)
        sspe = jnp.sum(kpe * kpe, axis=-1, keepdims=True)
        base = kpe * gkr_ref[...]
        rot = base * cos_ref[...] + _swap_halves(base, _swap_perm()) * sin_ref[...]
        for h in range(N_HEADS):
            kn = _nn(cn, wuk_ref[h])
            r = lax.rsqrt((jnp.sum(kn * kn, axis=-1, keepdims=True) + sspe) * (1.0 / QK_DIM) + EPS)
            k_ref[:, h * QK_PAD:h * QK_PAD + NOPE] = ((kn * r) * gkn_ref[...]).astype(BF16)
            k_ref[:, h * QK_PAD + NOPE:(h + 1) * QK_PAD] = (rot * r).astype(BF16)
            v_ref[:, h * V_DIM:(h + 1) * V_DIM] = _nn(cn, wuv_ref[h]).astype(BF16)

    return _pcall(
        body, (x, ln, wc, wpe, gl, wuk, wuv, gkn, gkr, cos, sin), deps, name=name, grid=(s // t,),
        out_shape=[jax.ShapeDtypeStruct((s, N_HEADS * QK_PAD), BF16), jax.ShapeDtypeStruct((s, N_HEADS * V_DIM), BF16)],
        in_specs=[_rows(t, D_MODEL)] + _kv_specs(t),
        out_specs=[_rows(t, N_HEADS * QK_PAD), _rows(t, N_HEADS * V_DIM)],
        compiler_params=_cparams(1, VMEM_MID),
    )


def _kv_bwd(x, dxin, dks, dvs, ln, wc, wpe, gl, wuk, wuv, gkn, gkr, cos, sin, name):
    s = x.shape[0]
    t = min(PROJ_ROWS, s)
    nk = len(dks)

    def body(*refs):
        x_ref, dxin_ref = refs[:2]
        dk_refs = refs[2:2 + nk]
        dv_refs = refs[2 + nk:2 + 2 * nk]
        (ln_ref, wc_ref, wpe_ref, gl_ref, wuk_ref, wuv_ref, gkn_ref, gkr_ref, cos_ref, sin_ref,
         dx_ref, hn_ref, cn_ref, dkn_ref, dvb_ref, dcc_ref, dpe_ref,
         dln_ref, dgl_ref, dgkn_ref, dgkr_ref) = refs[2 + 2 * nk:]
        x = x_ref[...]
        ln = ln_ref[...]
        y, rx = _rms(x, ln, D_MODEL)
        hn = y.astype(BF16)
        hn_ref[...] = hn
        clat = _nn(hn, wc_ref[...])
        kpe = _nn(hn, wpe_ref[...])
        gl = gl_ref[...]
        cy, rc = _rms(clat, gl, KV_RANK)
        cn = cy.astype(BF16)
        cn_ref[...] = cn
        sspe = jnp.sum(kpe * kpe, axis=-1, keepdims=True)
        cs, sn, perm = cos_ref[...], sin_ref[...], _swap_perm()
        gkn, gkr = gkn_ref[...], gkr_ref[...]
        base = kpe * gkr
        rot = base * cs + _swap_halves(base, perm) * sn
        dc = jnp.zeros((t, KV_RANK), F32)
        dkr_sum = jnp.zeros((t, LANES), F32)
        coef_sum = jnp.zeros((t, 1), F32)
        dgkn = jnp.zeros((1, NOPE), F32)
        for h in range(N_HEADS):
            kn = _nn(cn, wuk_ref[h])
            r = lax.rsqrt((jnp.sum(kn * kn, axis=-1, keepdims=True) + sspe) * (1.0 / QK_DIM) + EPS)
            lo, mid, hi = h * QK_PAD, h * QK_PAD + NOPE, (h + 1) * QK_PAD
            dko = dk_refs[0][:, lo:mid]
            dkr = dk_refs[0][:, mid:hi]
            dvh = dv_refs[0][:, h * V_DIM:(h + 1) * V_DIM]
            for j in range(1, nk):
                dko = dko + dk_refs[j][:, lo:mid]
                dkr = dkr + dk_refs[j][:, mid:hi]
                dvh = dvh + dv_refs[j][:, h * V_DIM:(h + 1) * V_DIM]
            un = dko * gkn
            sm = (jnp.sum(kn * un, axis=-1, keepdims=True) + jnp.sum(rot * dkr, axis=-1, keepdims=True)) * (1.0 / QK_DIM)
            coef = r * r * r * sm
            dkn = (r * un - kn * coef).astype(BF16)
            dkr_sum = dkr_sum + r * dkr
            coef_sum = coef_sum + coef
            dgkn = dgkn + jnp.sum(dko * (kn * r), axis=0, keepdims=True)
            dkn_ref[:, h * NOPE:(h + 1) * NOPE] = dkn
            dvb = dvh.astype(BF16)
            dvb_ref[:, h * V_DIM:(h + 1) * V_DIM] = dvb
            dc = dc + _nt(dkn, wuk_ref[h]) + _nt(dvb, wuv_ref[h])
        dz = dkr_sum * cs - _swap_halves(dkr_sum, perm) * sn
        dkpe = dz * gkr - kpe * coef_sum
        dgkr = jnp.sum(dz * kpe, axis=0, keepdims=True)
        dclat, dgl = _rms_bwd(clat, rc, gl, dc, KV_RANK)
        dcc = dclat.astype(BF16)
        dpe = dkpe.astype(BF16)
        dcc_ref[...] = dcc
        dpe_ref[...] = dpe
        dhn = _nt(dcc, wc_ref[...]) + _nt(dpe, wpe_ref[...])
        dxn, dln = _rms_bwd(x, rx, ln, dhn, D_MODEL)
        dx_ref[...] = dxin_ref[...] + dxn

        @pl.when(pl.program_id(0) == 0)
        def _():
            dln_ref[...] = jnp.zeros_like(dln_ref)
            dgl_ref[...] = jnp.zeros_like(dgl_ref)
            dgkn_ref[...] = jnp.zeros_like(dgkn_ref)
            dgkr_ref[...] = jnp.zeros_like(dgkr_ref)

        dln_ref[...] += dln
        dgl_ref[...] += dgl
        dgkn_ref[...] += dgkn
        dgkr_ref[...] += dgkr

    def tok(cols, dt):
        return jax.ShapeDtypeStruct((s, cols), dt)

    def vec(cols):
        return jax.ShapeDtypeStruct((1, cols), F32)

    return pl.pallas_call(
        body, name=name, grid=(s // t,),
        out_shape=[tok(D_MODEL, F32), tok(D_MODEL, BF16), tok(KV_RANK, BF16), tok(N_HEADS * NOPE, BF16),
                   tok(N_HEADS * V_DIM, BF16), tok(KV_RANK, BF16), tok(LANES, BF16),
                   vec(D_MODEL), vec(KV_RANK), vec(NOPE), vec(LANES)],
        in_specs=[_rows(t, D_MODEL), _rows(t, D_MODEL)] + [_rows(t, N_HEADS * QK_PAD)] * nk
                 + [_rows(t, N_HEADS * V_DIM)] * nk + _kv_specs(t),
        out_specs=[_rows(t, D_MODEL), _rows(t, D_MODEL), _rows(t, KV_RANK), _rows(t, N_HEADS * NOPE),
                   _rows(t, N_HEADS * V_DIM), _rows(t, KV_RANK), _rows(t, LANES),
                   _full((1, D_MODEL)), _full((1, KV_RANK)), _full((1, NOPE)), _full((1, LANES))],
        compiler_params=_cparams(1, VMEM_BIG),
    )(x, dxin, *dks, *dvs, ln, wc, wpe, gl, wuk, wuv, gkn, gkr, cos, sin)


def _q_specs(t):
    return [_full((1, D_MODEL)), _full((D_MODEL, Q_RANK)), _full((1, Q_RANK)), _full((N_HEADS, Q_RANK, QK_PAD)),
            _full((1, NOPE)), _full((1, LANES)), _rows(t, LANES), _rows(t, LANES)]


def _q_fwd(x, ln, wdq, gql, wuq, gqn, gqr, cos, sin, name, deps=()):
    s = x.shape[0]
    t = min(PROJ_ROWS, s)

    def body(x_ref, ln_ref, wdq_ref, gql_ref, wuq_ref, gqn_ref, gqr_ref, cos_ref, sin_ref, q_ref):
        hn = _rms(x_ref[...], ln_ref[...], D_MODEL)[0].astype(BF16)
        cqn = _rms(_nn(hn, wdq_ref[...]), gql_ref[...], Q_RANK)[0].astype(BF16)
        cs, sn, perm = cos_ref[...], sin_ref[...], _swap_perm()
        for h in range(N_HEADS):
            qa = _nn(cqn, wuq_ref[h])
            r = lax.rsqrt(jnp.sum(qa * qa, axis=-1, keepdims=True) * (1.0 / QK_DIM) + EPS)
            q_ref[:, h * QK_PAD:h * QK_PAD + NOPE] = ((qa[:, :NOPE] * r) * gqn_ref[...]).astype(BF16)
            z = (qa[:, NOPE:] * r) * gqr_ref[...]
            q_ref[:, h * QK_PAD + NOPE:(h + 1) * QK_PAD] = (z * cs + _swap_halves(z, perm) * sn).astype(BF16)

    return _pcall(
        body, (x, ln, wdq, gql, wuq, gqn, gqr, cos, sin), deps, name=name, grid=(s // t,),
        out_shape=jax.ShapeDtypeStruct((s, N_HEADS * QK_PAD), BF16),
        in_specs=[_rows(t, D_MODEL)] + _q_specs(t),
        out_specs=_rows(t, N_HEADS * QK_PAD),
        compiler_params=_cparams(1, VMEM_MID),
    )


def _q_bwd(x, dxin, dq, ln, wdq, gql, wuq, gqn, gqr, cos, sin, name):
    s = x.shape[0]
    t = min(PROJ_ROWS, s)

    def body(x_ref, dxin_ref, dq_ref, ln_ref, wdq_ref, gql_ref, wuq_ref, gqn_ref, gqr_ref, cos_ref, sin_ref,
             dx_ref, hn_ref, cqn_ref, dqa_ref, dcq_ref, dln_ref, dgql_ref, dgqn_ref, dgqr_ref):
        x = x_ref[...]
        ln = ln_ref[...]
        y, rx = _rms(x, ln, D_MODEL)
        hn = y.astype(BF16)
        hn_ref[...] = hn
        cqp = _nn(hn, wdq_ref[...])
        gql = gql_ref[...]
        cy, rc = _rms(cqp, gql, Q_RANK)
        cqn = cy.astype(BF16)
        cqn_ref[...] = cqn
        cs, sn, perm = cos_ref[...], sin_ref[...], _swap_perm()
        gqn, gqr = gqn_ref[...], gqr_ref[...]
        dcq = jnp.zeros((t, Q_RANK), F32)
        dgqn = jnp.zeros((1, NOPE), F32)
        dgqr = jnp.zeros((1, LANES), F32)
        for h in range(N_HEADS):
            qa = _nn(cqn, wuq_ref[h])
            qn, qr = qa[:, :NOPE], qa[:, NOPE:]
            r = lax.rsqrt(jnp.sum(qa * qa, axis=-1, keepdims=True) * (1.0 / QK_DIM) + EPS)
            dqo = dq_ref[:, h * QK_PAD:h * QK_PAD + NOPE]
            dqr = dq_ref[:, h * QK_PAD + NOPE:(h + 1) * QK_PAD]
            dz = dqr * cs - _swap_halves(dqr, perm) * sn
            un = dqo * gqn
            ur = dz * gqr
            sm = (jnp.sum(qn * un, axis=-1, keepdims=True) + jnp.sum(qr * ur, axis=-1, keepdims=True)) * (1.0 / QK_DIM)
            coef = r * r * r * sm
            dqa = jnp.concatenate([r * un - qn * coef, r * ur - qr * coef], axis=1).astype(BF16)
            dgqn = dgqn + jnp.sum(dqo * (qn * r), axis=0, keepdims=True)
            dgqr = dgqr + jnp.sum(dz * (qr * r), axis=0, keepdims=True)
            dqa_ref[:, h * QK_PAD:(h + 1) * QK_PAD] = dqa
            dcq = dcq + _nt(dqa, wuq_ref[h])
        dcqp, dgql = _rms_bwd(cqp, rc, gql, dcq, Q_RANK)
        dcqb = dcqp.astype(BF16)
        dcq_ref[...] = dcqb
        dhn = _nt(dcqb, wdq_ref[...])
        dxn, dln = _rms_bwd(x, rx, ln, dhn, D_MODEL)
        dx_ref[...] = dxin_ref[...] + dxn

        @pl.when(pl.program_id(0) == 0)
        def _():
            dln_ref[...] = jnp.zeros_like(dln_ref)
            dgql_ref[...] = jnp.zeros_like(dgql_ref)
            dgqn_ref[...] = jnp.zeros_like(dgqn_ref)
            dgqr_ref[...] = jnp.zeros_like(dgqr_ref)

        dln_ref[...] += dln
        dgql_ref[...] += dgql
        dgqn_ref[...] += dgqn
        dgqr_ref[...] += dgqr

    def tok(cols, dt):
        return jax.ShapeDtypeStruct((s, cols), dt)

    def vec(cols):
        return jax.ShapeDtypeStruct((1, cols), F32)

    return pl.pallas_call(
        body, name=name, grid=(s // t,),
        out_shape=[tok(D_MODEL, F32), tok(D_MODEL, BF16), tok(Q_RANK, BF16), tok(N_HEADS * QK_PAD, BF16),
                   tok(Q_RANK, BF16), vec(D_MODEL), vec(Q_RANK), vec(NOPE), vec(LANES)],
        in_specs=[_rows(t, D_MODEL), _rows(t, D_MODEL), _rows(t, N_HEADS * QK_PAD)] + _q_specs(t),
        out_specs=[_rows(t, D_MODEL), _rows(t, D_MODEL), _rows(t, Q_RANK), _rows(t, N_HEADS * QK_PAD),
                   _rows(t, Q_RANK), _full((1, D_MODEL)), _full((1, Q_RANK)), _full((1, NOPE)), _full((1, LANES))],
        compiler_params=_cparams(1, VMEM_MID),
    )(x, dxin, dq, ln, wdq, gql, wuq, gqn, gqr, cos, sin)


SM_SCALE = 1.0 / math.sqrt(QK_DIM)
LOG2_E = math.log2(math.e)
EXP2_SCALE = SM_SCALE * LOG2_E
NEG = -1e30


def _diag_mask(t):
    qpos = lax.broadcasted_iota(jnp.int32, (t, t), 0)
    kpos = lax.broadcasted_iota(jnp.int32, (t, t), 1)
    return lax.shift_right_logical(kpos, 6) <= lax.shift_right_logical(qpos, 6)


def _att_fwd(q, k, v, name):
    s = q.shape[0]
    t = min(512, s)
    nb = s // t

    def body(q_ref, k_ref, v_ref, o_ref, lse_ref):
        qi = pl.program_id(1)
        qq = q_ref[...]

        def block(ki, carry, masked):
            m_old, l_old, acc = carry
            rows = pl.ds(pl.multiple_of(ki * t, t), t)
            sc = _nt(qq, k_ref[rows, :])
            if masked:
                sc = jnp.where(_diag_mask(t), sc, NEG)
            m_new = jnp.maximum(m_old, jnp.max(sc, axis=-1, keepdims=True))
            p = jnp.exp2((sc - m_new) * EXP2_SCALE)
            alpha = jnp.exp2((m_old - m_new) * EXP2_SCALE)
            l_new = alpha * l_old + jnp.sum(p, axis=-1, keepdims=True)
            acc = alpha * acc + _nn(p.astype(BF16), v_ref[rows, :])
            return m_new, l_new, acc

        init = (jnp.full((t, 1), NEG, F32), jnp.zeros((t, 1), F32), jnp.zeros((t, V_DIM), F32))
        carry = lax.fori_loop(0, qi, lambda ki, c: block(ki, c, False), init)
        m_fin, l_fin, acc = block(qi, carry, True)
        o_ref[...] = (acc / l_fin).astype(BF16)
        lse_ref[...] = jnp.broadcast_to(m_fin * SM_SCALE + jnp.log(l_fin), (t, LANES))

    return pl.pallas_call(
        body, name=name, grid=(N_HEADS, nb),
        out_shape=[jax.ShapeDtypeStruct((s, N_HEADS * V_DIM), BF16), jax.ShapeDtypeStruct((s, N_HEADS * LANES), F32)],
        in_specs=[pl.BlockSpec((t, QK_PAD), lambda h, qi: (qi, h)),
                  pl.BlockSpec((s, QK_PAD), lambda h, qi: (0, h)),
                  pl.BlockSpec((s, V_DIM), lambda h, qi: (0, h))],
        out_specs=[pl.BlockSpec((t, V_DIM), lambda h, qi: (qi, h)),
                   pl.BlockSpec((t, LANES), lambda h, qi: (qi, h))],
        compiler_params=_cparams(2, VMEM_MID),
    )(q, k, v)


def _att_bwd(q, k, v, do, o, lse, name, deps=()):
    s = q.shape[0]
    t = min(512, s)
    nb = s // t

    def body(q_ref, k_ref, v_ref, do_ref, o_ref, lse_ref, dq_ref, dk_ref, dv_ref):
        ki = pl.program_id(1)
        kk, vv = k_ref[...], v_ref[...]

        @pl.when(ki == 0)
        def _():
            dq_ref[...] = jnp.zeros_like(dq_ref)

        def block(qi, carry, masked):
            dk, dv = carry
            rows = pl.ds(pl.multiple_of(qi * t, t), t)
            qq, dob = q_ref[rows, :], do_ref[rows, :]
            sc = _nt(qq, kk)
            if masked:
                sc = jnp.where(_diag_mask(t), sc, NEG)
            p = jnp.exp2(sc * EXP2_SCALE - lse_ref[rows, :][:, :1] * LOG2_E)
            dp = _nt(dob, vv)
            dsum = jnp.sum(dob.astype(F32) * o_ref[rows, :].astype(F32), axis=-1, keepdims=True)
            ds = (p * (dp - dsum)).astype(BF16)
            dq_ref[rows, :] += _nn(ds, kk)
            return dk + _tn(ds, qq), dv + _tn(p.astype(BF16), dob)

        carry = block(ki, (jnp.zeros((t, QK_PAD), F32), jnp.zeros((t, V_DIM), F32)), True)
        dk, dv = lax.fori_loop(ki + 1, nb, lambda qi, c: block(qi, c, False), carry)
        dk_ref[...] = dk * SM_SCALE
        dv_ref[...] = dv

        @pl.when(ki == nb - 1)
        def _():
            dq_ref[...] = dq_ref[...] * SM_SCALE

    def head(h, ki):
        return (0, h)

    def kblock(h, ki):
        return (ki, h)

    return _pcall(
        body, (q, k, v, do, o, lse), deps, name=name, grid=(N_HEADS, nb),
        out_shape=[jax.ShapeDtypeStruct((s, N_HEADS * QK_PAD), F32), jax.ShapeDtypeStruct((s, N_HEADS * QK_PAD), F32),
                   jax.ShapeDtypeStruct((s, N_HEADS * V_DIM), F32)],
        in_specs=[pl.BlockSpec((s, QK_PAD), head), pl.BlockSpec((t, QK_PAD), kblock), pl.BlockSpec((t, V_DIM), kblock),
                  pl.BlockSpec((s, V_DIM), head), pl.BlockSpec((s, V_DIM), head), pl.BlockSpec((s, LANES), head)],
        out_specs=[pl.BlockSpec((s, QK_PAD), head), pl.BlockSpec((t, QK_PAD), kblock), pl.BlockSpec((t, V_DIM), kblock)],
        compiler_params=_cparams(2, VMEM_MID),
    )


def _o_fwd(x, o, wo, name):
    s = x.shape[0]
    t = min(512, s)

    def body(x_ref, o_ref, wo_ref, xo_ref):
        xo_ref[...] = x_ref[...] + _nn(o_ref[...], wo_ref[...])

    return pl.pallas_call(
        body, name=name, grid=(s // t,),
        out_shape=jax.ShapeDtypeStruct((s, D_MODEL), F32),
        in_specs=[_rows(t, D_MODEL), _rows(t, D_MODEL), _full((D_MODEL, D_MODEL))],
        out_specs=_rows(t, D_MODEL),
        compiler_params=_cparams(1, VMEM_MID),
    )(x, o, wo)


def _o_bwd(dx, wo, name, deps=()):
    s = dx.shape[0]
    t = min(512, s)

    def body(dx_ref, wo_ref, do_ref, dxb_ref):
        dxb = dx_ref[...].astype(BF16)
        dxb_ref[...] = dxb
        do_ref[...] = _nt(dxb, wo_ref[...]).astype(BF16)

    tok = jax.ShapeDtypeStruct((s, D_MODEL), BF16)
    return _pcall(
        body, (dx, wo), deps, name=name, grid=(s // t,),
        out_shape=[tok, tok],
        in_specs=[_rows(t, D_MODEL), _full((D_MODEL, D_MODEL))],
        out_specs=[_rows(t, D_MODEL), _rows(t, D_MODEL)],
        compiler_params=_cparams(1, VMEM_MID),
    )


def _loss_head(y, target, name):
    s = y.shape[0]
    t = min(512, s)

    def body(y_ref, t_ref, dy_ref, sq_ref):
        e = y_ref[...] - t_ref[...]
        dy_ref[...] = e * (1.0 / D_MODEL)

        @pl.when(pl.program_id(0) == 0)
        def _():
            sq_ref[...] = jnp.zeros_like(sq_ref)

        sq_ref[...] += jnp.sum(e * e, axis=0, keepdims=True)

    return pl.pallas_call(
        body, name=name, grid=(s // t,),
        out_shape=[jax.ShapeDtypeStruct((s, D_MODEL), F32), jax.ShapeDtypeStruct((1, D_MODEL), F32)],
        in_specs=[_rows(t, D_MODEL), _rows(t, D_MODEL)],
        out_specs=[_rows(t, D_MODEL), _full((1, D_MODEL))],
        compiler_params=_cparams(1),
    )(y, target)


def _adamw(w, g, m, v, name):
    shape = w.shape
    c = shape[-1]
    r = math.prod(shape[:-1])
    tb = r
    for cand in (512, 256, 128):
        if r % cand == 0 and r > cand:
            tb = cand
            break

    def body(w_ref, g_ref, m_ref, v_ref, d_ref, mo_ref, vo_ref):
        gr = g_ref[...]
        mn = ADAM_B1 * m_ref[...] + (1.0 - ADAM_B1) * gr
        vn = ADAM_B2 * v_ref[...] + (1.0 - ADAM_B2) * (gr * gr)
        m_hat = mn / (1.0 - ADAM_B1 ** ADAM_STEP)
        v_hat = vn / (1.0 - ADAM_B2 ** ADAM_STEP)
        d_ref[...] = -ADAM_LR * (m_hat / (jnp.sqrt(v_hat) + ADAM_EPS) + ADAM_WD * w_ref[...])
        mo_ref[...] = mn
        vo_ref[...] = vn

    spec = pl.BlockSpec((tb, c), lambda i: (i, 0))
    flat = jax.ShapeDtypeStruct((r, c), F32)
    outs = pl.pallas_call(
        body, name=name, grid=(r // tb,),
        out_shape=[flat, flat, flat],
        in_specs=[spec] * 4, out_specs=[spec] * 3,
        compiler_params=_cparams(1),
    )(w.reshape(r, c), g.reshape(r, c), m.reshape(r, c), v.reshape(r, c))
    return [a.reshape(shape) for a in outs]


def _pad_cols(a, width):
    return jnp.pad(a, [(0, 0)] * (a.ndim - 1) + [(0, width - a.shape[-1])])


def _owner_view(a, sz):
    return a.reshape(a.shape[0], N_CHIPS, 2, sz, a.shape[-1])


def kernel(x, positions, ln_mix_a, w_pool, b_pool, pool_scale, ln_ffn, w_gate, w_up, w_down, ln_kv, w_dkv, g_kv_latent, w_uk, w_uv, g_k, ln_mix_b, w_dq, g_q_latent, w_uq, g_q, w_o, loss_target, m_ln_mix_a, m_w_pool, m_b_pool, m_pool_scale, m_ln_ffn, m_w_gate, m_w_up, m_w_down, m_ln_kv, m_w_dkv, m_g_kv_latent, m_w_uk, m_w_uv, m_g_k, m_ln_mix_b, m_w_dq, m_g_q_latent, m_w_uq, m_g_q, m_w_o, v_ln_mix_a, v_w_pool, v_b_pool, v_pool_scale, v_ln_ffn, v_w_gate, v_w_up, v_w_down, v_ln_kv, v_w_dkv, v_g_kv_latent, v_w_uk, v_w_uv, v_g_k, v_ln_mix_b, v_w_dq, v_g_q_latent, v_w_uq, v_g_q, v_w_o):
    weights = dict(ln_mix_a=ln_mix_a, w_pool=w_pool, b_pool=b_pool, pool_scale=pool_scale, ln_ffn=ln_ffn,
                   w_gate=w_gate, w_up=w_up, w_down=w_down, ln_kv=ln_kv, w_dkv=w_dkv, g_kv_latent=g_kv_latent,
                   w_uk=w_uk, w_uv=w_uv, g_k=g_k, ln_mix_b=ln_mix_b, w_dq=w_dq, g_q_latent=g_q_latent,
                   w_uq=w_uq, g_q=g_q, w_o=w_o)
    mom1 = dict(ln_mix_a=m_ln_mix_a, w_pool=m_w_pool, b_pool=m_b_pool, pool_scale=m_pool_scale, ln_ffn=m_ln_ffn,
                w_gate=m_w_gate, w_up=m_w_up, w_down=m_w_down, ln_kv=m_ln_kv, w_dkv=m_w_dkv,
                g_kv_latent=m_g_kv_latent, w_uk=m_w_uk, w_uv=m_w_uv, g_k=m_g_k, ln_mix_b=m_ln_mix_b, w_dq=m_w_dq,
                g_q_latent=m_g_q_latent, w_uq=m_w_uq, g_q=m_g_q, w_o=m_w_o)
    mom2 = dict(ln_mix_a=v_ln_mix_a, w_pool=v_w_pool, b_pool=v_b_pool, pool_scale=v_pool_scale, ln_ffn=v_ln_ffn,
                w_gate=v_w_gate, w_up=v_w_up, w_down=v_w_down, ln_kv=v_ln_kv, w_dkv=v_w_dkv,
                g_kv_latent=v_g_kv_latent, w_uk=v_w_uk, w_uv=v_w_uv, g_k=v_g_k, ln_mix_b=v_ln_mix_b, w_dq=v_w_dq,
                g_q_latent=v_g_q_latent, w_uq=v_w_uq, g_q=v_g_q, w_o=v_w_o)
    names = list(weights)
    dev = 4 * lax.axis_index("x") + 2 * lax.axis_index("y") + lax.axis_index("c")
    core = lax.axis_index("c").astype(jnp.int32).reshape(1)
    chip = (2 * lax.axis_index("x") + lax.axis_index("y")).astype(jnp.int32).reshape(1)

    xs = x[0]
    target = loss_target[0]
    cos, sin = _rope_tables(positions[0])

    small_sh = jnp.concatenate([ln_mix_a.reshape(1, -1), pool_scale.reshape(1, -1), b_pool.reshape(1, -1)], axis=1)
    wp_g, small_g = _all_gather([w_pool.astype(BF16), small_sh], [2, 0], "gather_first")
    wp_all = wp_g.reshape(2, 4, GROUP_DIM, GROUP_DIM)
    small_g = small_g.reshape(N_DEV, 3, 2, LANES)
    ln_a_all = small_g[:, 0].transpose(1, 0, 2).reshape(2, 1, D_MODEL)
    sc_all = small_g[:, 1].transpose(1, 0, 2).reshape(2, 1, D_MODEL)
    bp_all = small_g[:, 2].reshape(N_DEV, 2, 4, 32).transpose(1, 2, 0, 3).reshape(2, 1, D_MODEL)

    def placed(shard):
        buf = lax.empty((shard.shape[0], N_DEV) + shard.shape[1:], shard.dtype)
        return lax.dynamic_update_slice(buf, shard[:, None], (0, dev, 0, 0))

    groups = {f"ffn{l}": [placed(jnp.stack([w_gate[l].T, w_up[l].T, w_down[l]]).astype(BF16))] for l in range(4)}
    groups["att"] = [placed(a.astype(BF16)) for a in (
        w_dkv[None, :, :KV_RANK], _pad_cols(w_dkv[None, :, KV_RANK:], LANES), w_uk[None], w_uv[None],
        w_dq, _pad_cols(w_uq, QK_PAD), w_o)]
    def spread_start(nm, deps):
        return _copies_start(groups[nm], len(groups[nm]), _gather_spread, f"spread_{nm}", deps=deps)

    def spread_wait(nm, state, after):
        ssem, rsem, bufs, _ = state
        return _copies_wait(bufs, ssem, rsem, after, _blocks_moved(4), f"spread_done_{nm}")

    def relay_start(nm, bufs, deps=()):
        return _copies_start(bufs, len(bufs), _gather_relay, f"relay_{nm}", deps=deps)

    def relay_wait(nm, state, after):
        ssem, rsem, bufs, _ = state
        return _copies_wait(bufs, ssem, rsem, after, _blocks_moved(3), f"relay_done_{nm}")

    gkn = g_k[:NOPE].reshape(1, NOPE)
    gkr = _pad_cols(g_k[NOPE:].reshape(1, ROPE), LANES)
    gl = g_kv_latent.reshape(1, KV_RANK)
    lnkv = ln_kv.reshape(1, D_MODEL)

    x_in, x_mid, pooled, gates, ups, w_ffn = [], [], [], [], [], []
    qs, outs, lses = [], [], []

    def mixer(l, cur, deps):
        x_in.append(cur)
        mid, dsave = _mix_fwd(cur, ln_a_all[l], wp_all[l], bp_all[l], sc_all[l], f"mix_fwd{l}", deps=deps)
        pooled.append(dsave)
        x_mid.append(mid)
        return mid

    def q_args(j):
        return (ln_mix_b[j].reshape(1, -1), wdq_all[j], g_q_latent[j].reshape(1, -1), wuq_all[j],
                g_q[j, :NOPE].reshape(1, -1), _pad_cols(g_q[j, NOPE:].reshape(1, -1), LANES), cos, sin)

    def attention(j, cur, deps):
        x_in.append(cur)
        q = _q_fwd(cur, *q_args(j), f"q_fwd{j}", deps=deps)
        o, lse = _att_fwd(q, k_sh, v_sh, f"att_fwd{j}")
        mid = _o_fwd(cur, o, wo_all[j], f"o_fwd{j}")
        qs.append(q)
        outs.append(o)
        lses.append(lse)
        x_mid.append(mid)
        return mid

    def ffn(l, mid, relayed):
        w_l = relayed[0].reshape(3, D_FF, D_MODEL)
        w_ffn.append(w_l)
        cur, gate, up = _ffn_fwd(mid, ln_ffn[l].reshape(1, -1), w_l, f"ffn_fwd{l}")
        gates.append(gate)
        ups.append(up)
        return cur

    sp0 = spread_start("ffn0", [small_g])
    mid = mixer(0, xs, [sp0[3]])
    landed0 = spread_wait("ffn0", sp0, mid)
    sp1 = spread_start("ffn1", [landed0[0]])
    rl0 = relay_start("ffn0", landed0, [sp1[3]])
    cur = ffn(0, mid, relay_wait("ffn0", rl0, rl0[3]))

    landed1 = spread_wait("ffn1", sp1, cur)
    sp_att = spread_start("att", [landed1[0]])
    sp2 = spread_start("ffn2", [landed1[0]])
    rl1 = relay_start("ffn1", landed1, [sp_att[3], sp2[3]])
    mid = mixer(1, cur, [rl1[3]])
    cur = ffn(1, mid, relay_wait("ffn1", rl1, mid))
    x_kv = cur

    landed_att = spread_wait("att", sp_att, cur)
    landed2 = spread_wait("ffn2", sp2, cur)
    sp3 = spread_start("ffn3", [landed2[0]])
    rl_att = relay_start("att", landed_att, [sp3[3]])
    rl2 = relay_start("ffn2", landed2, [sp3[3]])
    att_bufs = relay_wait("att", rl_att, rl2[3])
    wc = att_bufs[0].reshape(D_MODEL, KV_RANK)
    wpe = att_bufs[1].reshape(D_MODEL, LANES)
    wuk_g = att_bufs[2].reshape(N_HEADS, KV_RANK, NOPE)
    wuv_g = att_bufs[3].reshape(N_HEADS, KV_RANK, V_DIM)
    wdq_all = att_bufs[4].reshape(2, D_MODEL, Q_RANK)
    wuq_all = att_bufs[5]
    wo_all = att_bufs[6].reshape(2, D_MODEL, D_MODEL)
    k_sh, v_sh = _kv_fwd(cur, lnkv, wc, wpe, gl, wuk_g, wuv_g, gkn, gkr, cos, sin, "kv_fwd")
    mid = attention(0, cur, [])
    cur = ffn(2, mid, relay_wait("ffn2", rl2, mid))

    landed3 = spread_wait("ffn3", sp3, cur)
    rl3 = relay_start("ffn3", landed3)
    mid = attention(1, cur, [rl3[3]])
    cur = ffn(3, mid, relay_wait("ffn3", rl3, mid))

    dx, sq_cols = _loss_head(cur, target, "loss_head")

    small = {}
    sizes = dict(ffn0=FF_SHARD, ffn1=FF_SHARD, ffn2=FF_SHARD, ffn3=FF_SHARD, wo=128, kv512=128, dkv_pe=128,
                 wdq=128, wuqT=QK_PAD, wpool=32)
    big = dict(wo=lax.empty((2, D_MODEL, D_MODEL), BF16), kv512=lax.empty((3, D_MODEL, KV_RANK), BF16),
               dkv_pe=lax.empty((1, D_MODEL, LANES), BF16), wdq=lax.empty((2, D_MODEL, Q_RANK), BF16),
               wuqT=lax.empty((2, N_HEADS * QK_PAD, Q_RANK), BF16), wpool=lax.empty((8, GROUP_DIM, GROUP_DIM), BF16))
    for l in range(4):
        big[f"ffn{l}"] = lax.empty((3, D_FF, D_MODEL), BF16)
    red = {}

    def pair_start(nms, tag):
        arrs = []
        for nm in nms:
            view = _owner_view(big[nm], sizes[nm])
            arrs += [view, lax.empty((view.shape[0], N_CHIPS) + view.shape[3:], BF16)]
        return nms, tag, _copies_start(arrs, len(nms), _pair_send, f"pair_start_{tag}")

    def chip_start(state, after):
        nms, tag, (ssem, rsem, arrs, _) = state
        arrs = _copies_wait(arrs, ssem, rsem, after, _landed, f"pair_done_{tag}")
        out = []
        for t, nm in enumerate(nms):
            part = _pair_sum(arrs[2 * t], arrs[2 * t + 1], core, f"pair_sum_{nm}")
            out += [part, lax.empty((3, part.shape[0]) + part.shape[2:], BF16)]
        return nms, tag, _copies_start(out, len(nms), _chip_send, f"chip_start_{tag}")

    def chip_finish(state, after):
        nms, tag, (ssem, rsem, arrs, _) = state
        arrs = _copies_wait(arrs, ssem, rsem, after, _landed, f"chip_done_{tag}")
        for t, nm in enumerate(nms):
            red[nm] = _chip_sum(arrs[2 * t], arrs[2 * t + 1], chip, f"chip_sum_{nm}")

    dks, dvs = [], []
    pending = None
    bwd_deps = []
    for l in (3, 2, 1, 0):
        key = f"ffn{l}"
        dx, act, dgb, dub, hn, dyb, dln = _ffn_bwd(x_mid[l], dx, gates[l], ups[l], ln_ffn[l].reshape(1, -1),
                                                     w_ffn[l], f"ffn_bwd{l}", deps=bwd_deps)
        bwd_deps = []
        small[f"ln_ffn{l}"] = dln
        if l == 1:
            att_chip = chip_start(att_pair, dx)
            tn_deps = [att_chip[2][3]]
        else:
            tn_deps = []
        if pending:
            chip_finish(pending, dx)
            pending = None
        big[key] = _tn_matmul(dgb, hn, big[key], 0, f"dw_gate{l}", m_chunk=FF_HALF, deps=tn_deps)
        big[key] = _tn_matmul(dub, hn, big[key], 1, f"dw_up{l}", m_chunk=FF_HALF)
        big[key] = _tn_matmul(act, dyb, big[key], 2, f"dw_down{l}", m_chunk=FF_HALF)
        if l == 1:
            chip_finish(att_chip, big[key])
        ffn_pair = pair_start([key], key)
        if l >= 2:
            j = l - 2
            do, dxb = _o_bwd(dx, wo_all[j], f"o_bwd{j}", deps=[ffn_pair[2][3]])
            big["wo"] = _tn_matmul(outs[j], dxb, big["wo"], j, f"dw_o{j}")
            ffn_chip = chip_start(ffn_pair, big["wo"])
            dq, dk, dv = _att_bwd(qs[j], k_sh, v_sh, do, outs[j], lses[j], f"att_bwd{j}", deps=[ffn_chip[2][3]])
            chip_finish(ffn_chip, dq)
            dks.append(dk)
            dvs.append(dv)
            dx, hnq, cqn, dqa, dcq, dln, dgql, dgqn, dgqr = _q_bwd(x_in[l], dx, dq, *q_args(j), f"q_bwd{j}")
            small[f"ln_mix_b{j}"] = dln
            small[f"g_q_latent{j}"] = dgql
            small[f"g_q{j}"] = jnp.concatenate([dgqn, dgqr[:, :ROPE]], axis=1)
            big["wdq"] = _tn_matmul(hnq, dcq, big["wdq"], j, f"dw_dq{j}")
            big["wuqT"] = _tn_matmul(dqa, cqn, big["wuqT"], j, f"dw_uq{j}")
            if l == 2:
                (dx, hnk, cn, dknb, dvb, dccb, dpeb, dlnkv, dgl, dgkn, dgkr) = _kv_bwd(
                    x_kv, dx, dks, dvs, lnkv, wc, wpe, gl, wuk_g, wuv_g, gkn, gkr, cos, sin, "kv_bwd")
                small["ln_kv"] = dlnkv
                small["g_kv_latent"] = dgl
                small["g_k"] = jnp.concatenate([dgkn, dgkr[:, :ROPE]], axis=1)
                big["kv512"] = _tn_matmul(dknb, cn, big["kv512"], 0, "dw_uk")
                big["kv512"] = _tn_matmul(dvb, cn, big["kv512"], 1, "dw_uv")
                big["kv512"] = _tn_matmul(hnk, dccb, big["kv512"], 2, "dw_dkv_c")
                big["dkv_pe"] = _tn_matmul(hnk, dpeb, big["dkv_pe"], 0, "dw_dkv_pe")
                att_pair = pair_start(["wo", "kv512", "dkv_pe", "wdq", "wuqT"], "att")
                bwd_deps = [att_pair[2][3]]
        else:
            dx, dyp, dsc, db, dln = _mix_bwd(x_in[l], dx, pooled[l], ln_a_all[l], wp_all[l], bp_all[l], sc_all[l],
                                             f"mix_bwd{l}", deps=[ffn_pair[2][3]])
            small[f"ln_mix_a{l}"] = dln
            small[f"pool_scale{l}"] = dsc
            small[f"b_pool{l}"] = db
            ffn_chip = chip_start(ffn_pair, dx)
            big["wpool"] = _tn_matmul(pooled[l], dyp, big["wpool"], 4 * l, f"dw_pool{l}", groups=4,
                                      deps=[ffn_chip[2][3]])
            if l == 1:
                pending = ffn_chip
            else:
                chip_finish(ffn_chip, big["wpool"])
    grad_x = dx[None]
    pool_pair = pair_start(["wpool"], "wpool")
    pool_chip = chip_start(pool_pair, pool_pair[2][3])
    chip_finish(pool_chip, pool_chip[2][3])

    vec_names = (["loss"] + [f"ln_ffn{l}" for l in range(4)] + ["ln_kv", "g_kv_latent", "g_k"]
                 + [f"{p}{j}" for p in ("ln_mix_b", "g_q_latent", "g_q") for j in range(2)]
                 + [f"{p}{l}" for p in ("ln_mix_a", "pool_scale", "b_pool") for l in range(2)])
    small["loss"] = sq_cols
    widths = [small[nm].shape[1] for nm in vec_names]
    padded = [-(-w // LANES) * LANES for w in widths]
    packed = jnp.concatenate([_pad_cols(small[nm], pw) for nm, pw in zip(vec_names, padded)], axis=1)
    (all_vecs,) = _all_gather([packed], [0], "gather_vectors")
    total = _sum_lead(all_vecs, "sum_vectors")
    vec = {}
    off = 0
    for nm, w, pw in zip(vec_names, widths, padded):
        vec[nm] = total[0, off:off + w]
        off += pw
    loss = 0.5 * jnp.sum(vec["loss"]) * (1.0 / D_MODEL)

    def own_cols(full, width):
        return lax.dynamic_slice_in_dim(full, dev * width, width, axis=full.ndim - 1)

    grads = dict(
        ln_mix_a=own_cols(jnp.stack([vec["ln_mix_a0"], vec["ln_mix_a1"]]), LANES),
        w_pool=red["wpool"].reshape(2, 4, 32, GROUP_DIM),
        b_pool=own_cols(jnp.stack([vec["b_pool0"], vec["b_pool1"]]).reshape(2, 4, GROUP_DIM), 32),
        pool_scale=own_cols(jnp.stack([vec["pool_scale0"], vec["pool_scale1"]]), LANES),
        ln_ffn=jnp.stack([vec[f"ln_ffn{l}"] for l in range(4)]),
        w_gate=jnp.stack([red[f"ffn{l}"][0] for l in range(4)]).transpose(0, 2, 1),
        w_up=jnp.stack([red[f"ffn{l}"][1] for l in range(4)]).transpose(0, 2, 1),
        w_down=jnp.stack([red[f"ffn{l}"][2] for l in range(4)]),
        ln_kv=vec["ln_kv"],
        w_dkv=jnp.concatenate([red["kv512"][2], red["dkv_pe"][0][:, :ROPE]], axis=1),
        g_kv_latent=vec["g_kv_latent"],
        w_uk=red["kv512"][0].T,
        w_uv=red["kv512"][1].T,
        g_k=vec["g_k"],
        ln_mix_b=jnp.stack([vec["ln_mix_b0"], vec["ln_mix_b1"]]),
        w_dq=red["wdq"],
        g_q_latent=jnp.stack([vec["g_q_latent0"], vec["g_q_latent1"]]),
        w_uq=red["wuqT"].transpose(0, 2, 1)[:, :, :QK_DIM],
        g_q=jnp.stack([vec["g_q0"], vec["g_q1"]]),
        w_o=red["wo"],
    )

    deltas, new_m, new_v = {}, {}, {}
    for nm in names:
        w = weights[nm]
        shape = w.shape if w.ndim > 1 else (1, w.shape[0])
        d, mo, vo = _adamw(w.reshape(shape), grads[nm].reshape(shape), mom1[nm].reshape(shape),
                           mom2[nm].reshape(shape), f"adamw_{nm}")
        deltas[nm], new_m[nm], new_v[nm] = d.reshape(w.shape), mo.reshape(w.shape), vo.reshape(w.shape)

    return (loss, grad_x, *[grads[nm].reshape(weights[nm].shape) for nm in names], *[deltas[nm] for nm in names],
            *[new_m[nm] for nm in names], *[new_v[nm] for nm in names])
```

```python
import functools
import math

import jax
import jax.numpy as jnp
from jax import lax
from jax.experimental import pallas as pl
from jax.experimental.pallas import tpu as pltpu

F32 = jnp.float32
BF16 = jnp.bfloat16
MESH = pl.DeviceIdType.MESH

D_MODEL = 1024
D_FF = 2816
N_DEV = 8
N_CHIPS = 4
FF_SHARD = D_FF // N_DEV
FF_HALF = D_FF // 2
N_HEADS = 8
NOPE = 128
ROPE = 64
QK_DIM = NOPE + ROPE
QK_PAD = 256
V_DIM = 128
Q_RANK = 256
KV_RANK = 512
POOL_WINDOWS = (2, 4, 8, 16)
GROUP_DIM = 256
HALO = 128
CHUNK = 64
ROPE_THETA = 10000.0
EPS = 1e-6
LANES = 128

ADAM_LR = 0.001
ADAM_B1 = 0.9
ADAM_B2 = 0.999
ADAM_EPS = 1e-08
ADAM_WD = 0.01
ADAM_STEP = 10

PROJ_ROWS = 256
VMEM_BIG = 56 * 2**20
VMEM_MID = 40 * 2**20


def _nn(a, b):
    return lax.dot_general(a, b, (((1,), (0,)), ((), ())), preferred_element_type=F32)


def _nt(a, b):
    return lax.dot_general(a, b, (((1,), (1,)), ((), ())), preferred_element_type=F32)


def _tn(a, b):
    return lax.dot_general(a, b, (((0,), (0,)), ((), ())), preferred_element_type=F32)


def _rms(x, g, n):
    r = lax.rsqrt(jnp.sum(x * x, axis=-1, keepdims=True) * (1.0 / n) + EPS)
    return (x * r) * g, r


def _rms_bwd(x, r, g, dy, n):
    u = dy * g
    s = jnp.sum(x * u, axis=-1, keepdims=True) * (1.0 / n)
    dx = r * u - x * (r * r * r * s)
    dg = jnp.sum(dy * (x * r), axis=0, keepdims=True)
    return dx, dg


def _swap_perm():
    i = lax.broadcasted_iota(jnp.int32, (LANES, LANES), 0)
    j = lax.broadcasted_iota(jnp.int32, (LANES, LANES), 1)
    half = ROPE // 2
    hit = ((j < half) & (i == j + half)) | ((j >= half) & (j < ROPE) & (i == j - half))
    return jnp.where(hit, 1.0, 0.0).astype(BF16)


def _swap_halves(z, perm):
    hi = z.astype(BF16)
    lo = (z - hi.astype(F32)).astype(BF16)
    return _nn(hi, perm) + _nn(lo, perm)


def _sigmoid(x):
    return 1.0 / (1.0 + jnp.exp(-x))


def _cparams(n_grid, vmem=None):
    return pltpu.CompilerParams(dimension_semantics=("arbitrary",) * n_grid, vmem_limit_bytes=vmem)


def _rows(t, cols):
    return pl.BlockSpec((t, cols), lambda i: (i, 0))


def _full(shape):
    nd = len(shape)
    return pl.BlockSpec(shape, lambda *_: (0,) * nd)


ANY = pl.BlockSpec(memory_space=pl.ANY)


def _pcall(body, args, deps, *, in_specs, **kw):
    n_in, n_dep = len(args), len(deps)

    def ordered(*refs):
        body(*refs[:n_in], *refs[n_in + n_dep:])

    return pl.pallas_call(ordered, in_specs=list(in_specs) + [ANY] * n_dep, **kw)(*args, *deps)


def _place():
    x, y, c = lax.axis_index("x"), lax.axis_index("y"), lax.axis_index("c")
    return x, y, c


def _all_gather(shards, axes, name):
    n = len(shards)
    out_shape = [jax.ShapeDtypeStruct(s.shape[:a] + (N_DEV,) + s.shape[a:], s.dtype) for s, a in zip(shards, axes)]

    def body(*refs):
        ins, outs = refs[:n], refs[n:2 * n]
        send_sems, recv_sems, local_sems = refs[2 * n:]
        x, y, c = _place()
        me, sibling = (x, y, c), (x, y, 1 - c)
        chips = [(1 - x, y), (x, 1 - y), (1 - x, 1 - y)]

        def slot(t, dev):
            idx = 4 * dev[0] + 2 * dev[1] + dev[2]
            return outs[t].at[(slice(None),) * axes[t] + (idx,)]

        def copy(t, k, block, to, src=None):
            return pltpu.make_async_remote_copy(
                src_ref=slot(t, block) if src is None else src, dst_ref=slot(t, block),
                send_sem=send_sems.at[t, k], recv_sem=recv_sems.at[t, k],
                device_id=to, device_id_type=MESH)

        mine = [pltpu.make_async_copy(ins[t], slot(t, me), local_sems.at[t]) for t in range(n)]
        for cp in mine:
            cp.start()
        first = []
        for t in range(n):
            first.append(copy(t, 0, me, sibling, src=ins[t]))
            first += [copy(t, 1 + j, me, (*chip, c), src=ins[t]) for j, chip in enumerate(chips)]
        for cp in first:
            cp.start()
        passed = []
        for j, chip in enumerate(chips):
            for t in range(n):
                copy(t, 1 + j, (*chip, c), me).wait_recv()
                cp = copy(t, 4 + j, (*chip, c), sibling)
                cp.start()
                passed.append(cp)
        for t in range(n):
            copy(t, 0, sibling, me).wait_recv()
            for j, chip in enumerate(chips):
                copy(t, 4 + j, (*chip, 1 - c), me).wait_recv()
        for cp in first + passed:
            cp.wait_send()
        for cp in mine:
            cp.wait()

    return pl.pallas_call(
        body, name=name, out_shape=out_shape,
        in_specs=[ANY] * n, out_specs=[ANY] * n,
        scratch_shapes=[pltpu.SemaphoreType.DMA((n, 7)), pltpu.SemaphoreType.DMA((n, 7)),
                        pltpu.SemaphoreType.DMA((n,))],
    )(*shards)


HBM = pl.BlockSpec(memory_space=pltpu.HBM)
SEM = pl.BlockSpec(memory_space=pltpu.SEMAPHORE)
EFFECT = pltpu.SideEffectType.DATAFLOW_SIDE_EFFECTING


def _copies_start(arrays, n_sems, plan, name, deps=()):
    n, nd = len(arrays), len(deps)

    def body(*refs):
        for cp in plan(refs[:n], refs[n + nd], refs[n + nd + 1]):
            cp.start()
        refs[-1][...] = jnp.zeros_like(refs[-1])

    outs = pl.pallas_call(
        body, name=name,
        out_shape=(pltpu.SemaphoreType.DMA((n_sems,)), pltpu.SemaphoreType.DMA((n_sems,)),
                   *[pltpu.HBM(a.shape, a.dtype) for a in arrays], jax.ShapeDtypeStruct((8, LANES), F32)),
        in_specs=[HBM] * n + [ANY] * nd,
        out_specs=(SEM, SEM, *[HBM] * n, pl.BlockSpec(memory_space=pltpu.VMEM)),
        input_output_aliases={i: 2 + i for i in range(n)},
        compiler_params=pltpu.CompilerParams(has_side_effects=EFFECT),
    )(*[pltpu.with_memory_space_constraint(a, pltpu.HBM) for a in arrays], *deps)
    return outs[0], outs[1], list(outs[2:2 + n]), outs[-1]


def _copies_wait(arrays, send_sems, recv_sems, after, plan, name):
    n = len(arrays)

    def body(*refs):
        for cp in plan(refs[:n], refs[n], refs[n + 1]):
            cp.wait_send()
            cp.wait_recv()

    outs = pl.pallas_call(
        body, name=name,
        out_shape=tuple(pltpu.HBM(a.shape, a.dtype) for a in arrays),
        in_specs=[HBM] * n + [SEM, SEM, ANY], out_specs=tuple([HBM] * n),
        input_output_aliases={i: i for i in range(n)},
        compiler_params=pltpu.CompilerParams(has_side_effects=EFFECT),
    )(*arrays, send_sems, recv_sems, after)
    return list(outs)


def _remote(src, dst, send_sems, recv_sems, t, to):
    return pltpu.make_async_remote_copy(src_ref=src, dst_ref=dst, send_sem=send_sems.at[t], recv_sem=recv_sems.at[t],
                                        device_id=to, device_id_type=MESH)


def _dev_index(x, y, c):
    return 4 * x + 2 * y + c


def _gather_spread(bufs, send_sems, recv_sems):
    x, y, c = _place()
    mine = _dev_index(x, y, c)
    peers = [(x, y, 1 - c), (1 - x, y, c), (x, 1 - y, c), (1 - x, 1 - y, c)]
    return [_remote(g.at[k, mine], g.at[k, mine], send_sems, recv_sems, t, peer)
            for t, g in enumerate(bufs) for peer in peers for k in range(g.shape[0])]


def _gather_relay(bufs, send_sems, recv_sems):
    x, y, c = _place()
    blocks = [_dev_index(1 - x, y, c), _dev_index(x, 1 - y, c), _dev_index(1 - x, 1 - y, c)]
    return [_remote(g.at[k, b], g.at[k, b], send_sems, recv_sems, t, (x, y, 1 - c))
            for t, g in enumerate(bufs) for b in blocks for k in range(g.shape[0])]


def _blocks_moved(count):
    def plan(bufs, send_sems, recv_sems):
        x, y, c = _place()
        return [_remote(g.at[:, pl.ds(0, count)], g.at[:, pl.ds(0, count)], send_sems, recv_sems, t, (x, y, 1 - c))
                for t, g in enumerate(bufs)]
    return plan


def _pair_send(arrs, send_sems, recv_sems):
    x, y, c = _place()
    return [_remote(arrs[2 * t].at[p, k, 1 - c], arrs[2 * t + 1].at[p, k], send_sems, recv_sems, t, (x, y, 1 - c))
            for t in range(len(arrs) // 2) for p in range(arrs[2 * t].shape[0]) for k in range(N_CHIPS)]


def _chip_send(arrs, send_sems, recv_sems):
    x, y, c = _place()
    chips = [(1 - x, y), (x, 1 - y), (1 - x, 1 - y)]
    return [_remote(arrs[2 * t].at[p, 2 * px + py], arrs[2 * t + 1].at[j, p], send_sems, recv_sems, t, (px, py, c))
            for t in range(len(arrs) // 2) for j, (px, py) in enumerate(chips) for p in range(arrs[2 * t].shape[0])]


def _landed(arrs, send_sems, recv_sems):
    x, y, c = _place()
    return [_remote(arrs[2 * t + 1], arrs[2 * t + 1], send_sems, recv_sems, t, (x, y, 1 - c))
            for t in range(len(arrs) // 2)]


def _rows_per_step(rows, row_elems):
    best = 1
    for cand in range(1, rows + 1):
        if rows % cand == 0 and cand * row_elems <= 256 * 1024:
            best = cand
    return best


def _pair_sum(grad, landed, core, name):
    p, _, _, sz, c = grad.shape
    r = _rows_per_step(p * N_CHIPS, sz * c)

    def body(core_ref, g_ref, l_ref, o_ref):
        o_ref[...] = (g_ref[...].astype(F32) + l_ref[...].astype(F32)).astype(o_ref.dtype)

    out = pl.pallas_call(
        body, name=name,
        grid_spec=pltpu.PrefetchScalarGridSpec(
            num_scalar_prefetch=1, grid=(p * N_CHIPS // r,),
            in_specs=[pl.BlockSpec((r, None, sz, c), lambda i, cr: (i, cr[0], 0, 0)),
                      pl.BlockSpec((r, sz, c), lambda i, cr: (i, 0, 0))],
            out_specs=pl.BlockSpec((r, sz, c), lambda i, cr: (i, 0, 0))),
        out_shape=jax.ShapeDtypeStruct((p * N_CHIPS, sz, c), grad.dtype),
        compiler_params=_cparams(1),
    )(core, grad.reshape(p * N_CHIPS, 2, sz, c), landed.reshape(p * N_CHIPS, sz, c))
    return out.reshape(p, N_CHIPS, sz, c)


def _chip_sum(parts, landed, chip, name):
    p, _, sz, c = parts.shape
    r = _rows_per_step(p, sz * c)

    def body(chip_ref, a_ref, l_ref, o_ref):
        acc = a_ref[...].astype(F32)
        for j in range(3):
            acc = acc + l_ref[j].astype(F32)
        o_ref[...] = acc

    return pl.pallas_call(
        body, name=name,
        grid_spec=pltpu.PrefetchScalarGridSpec(
            num_scalar_prefetch=1, grid=(p // r,),
            in_specs=[pl.BlockSpec((r, None, sz, c), lambda i, cr: (i, cr[0], 0, 0)),
                      pl.BlockSpec((3, r, sz, c), lambda i, cr: (0, i, 0, 0))],
            out_specs=pl.BlockSpec((r, sz, c), lambda i, cr: (i, 0, 0))),
        out_shape=jax.ShapeDtypeStruct((p, sz, c), F32),
        compiler_params=_cparams(1),
    )(chip, parts, landed)


def _sum_lead(a, name, out_dtype=F32):
    k = a.shape[0]
    rest = a.shape[1:]
    r, c = rest[-2], rest[-1]
    lead = math.prod(rest[:-2])
    a3 = a.reshape(k, lead * r, c)
    rows = lead * r
    tb = rows
    for cand in (512, 256, 128, 64, 32, 16, 8):
        if rows % cand == 0 and rows > cand:
            tb = cand
            break

    def body(a_ref, o_ref):
        acc = a_ref[0].astype(F32)
        for i in range(1, k):
            acc = acc + a_ref[i].astype(F32)
        o_ref[...] = acc.astype(out_dtype)

    out = pl.pallas_call(
        body, name=name, grid=(rows // tb,),
        out_shape=jax.ShapeDtypeStruct((rows, c), out_dtype),
        in_specs=[pl.BlockSpec((k, tb, c), lambda i: (0, i, 0))],
        out_specs=pl.BlockSpec((tb, c), lambda i: (i, 0)),
        compiler_params=_cparams(1),
    )(a3)
    return out.reshape(rest)


def _bands(t, causal):
    r = lax.broadcasted_iota(jnp.int32, (t, t + HALO), 0)
    col = lax.broadcasted_iota(jnp.int32, (t, t + HALO), 1)
    diff = r + HALO - col if causal else col - r
    return jnp.stack([jnp.where((diff >= 0) & (diff < w), 1.0, 0.0) for w in POOL_WINDOWS]).astype(BF16)


def _split_dot(band, v):
    hi = v.astype(BF16)
    lo = (v - hi.astype(F32)).astype(BF16)
    return _nn(band, hi) + _nn(band, lo)


def _mix_fwd(x, g, wp, b, sc, name, deps=()):
    s = x.shape[0]
    t = min(256, s)
    rb = t // HALO

    def body(x_ref, xh_ref, g_ref, wp_ref, b_ref, sc_ref, band_ref, xo_ref, d_ref):
        i = pl.program_id(0)
        gg = g_ref[...]
        h, _ = _rms(x_ref[...], gg, D_MODEL)
        hh, _ = _rms(xh_ref[...], gg, D_MODEL)
        hh = jnp.where(i > 0, hh, 0.0)
        hext = jnp.concatenate([hh, h], axis=0)
        tok = i * t + lax.broadcasted_iota(jnp.int32, (t, 1), 0)
        for gi, w in enumerate(POOL_WINDOWS):
            sl = slice(gi * GROUP_DIM, (gi + 1) * GROUP_DIM)
            win = _split_dot(band_ref[gi], hext[:, sl])
            cnt = jnp.minimum(tok + 1, w).astype(F32)
            dbf = (win / cnt - h[:, sl]).astype(BF16)
            d_ref[:, sl] = dbf
            ypre = _nn(dbf, wp_ref[gi]) + b_ref[:, sl]
            xo_ref[:, sl] = x_ref[:, sl] + ypre * sc_ref[:, sl]

    return _pcall(
        body, (x, x, g, wp, b, sc, _bands(t, True)), deps, name=name, grid=(s // t,),
        out_shape=[jax.ShapeDtypeStruct((s, D_MODEL), F32), jax.ShapeDtypeStruct((s, D_MODEL), BF16)],
        in_specs=[_rows(t, D_MODEL),
                  pl.BlockSpec((HALO, D_MODEL), lambda i: (jnp.maximum(i * rb - 1, 0), 0)),
                  _full((1, D_MODEL)), _full((4, GROUP_DIM, GROUP_DIM)), _full((1, D_MODEL)), _full((1, D_MODEL)),
                  _full((4, t, t + HALO))],
        out_specs=[_rows(t, D_MODEL), _rows(t, D_MODEL)],
        compiler_params=_cparams(1, VMEM_MID),
    )


def _mix_bwd(x, dy, d, g, wp, b, sc, name, deps=()):
    s = x.shape[0]
    t = min(256, s)
    rb = t // HALO
    nb = s // t
    last_halo = s // HALO - 1

    def body(x_ref, dy_ref, dyn_ref, d_ref, g_ref, wp_ref, b_ref, sc_ref, band_ref,
             dx_ref, dyp_ref, dsc_ref, db_ref, dln_ref):
        i = pl.program_id(0)
        x = x_ref[...]
        gg = g_ref[...]
        dy = dy_ref[...]
        sc = sc_ref[...]
        dyp32 = dy * sc
        dyp = dyp32.astype(BF16)
        dyph = (dyn_ref[...] * sc).astype(BF16)
        dyp_ref[...] = dyp
        tok = i * t + lax.broadcasted_iota(jnp.int32, (t + HALO, 1), 0)
        dh, dsc = [], []
        for gi, w in enumerate(POOL_WINDOWS):
            sl = slice(gi * GROUP_DIM, (gi + 1) * GROUP_DIM)
            ypre = _nn(d_ref[:, sl], wp_ref[gi]) + b_ref[:, sl]
            dsc.append(jnp.sum(dy[:, sl] * ypre, axis=0, keepdims=True))
            dd = _nt(dyp[:, sl], wp_ref[gi])
            ddh = jnp.where(i < nb - 1, _nt(dyph[:, sl], wp_ref[gi]), 0.0)
            cnt = jnp.minimum(tok + 1, w).astype(F32)
            ddext = jnp.concatenate([dd, ddh], axis=0) / cnt
            dh.append(_split_dot(band_ref[gi], ddext) - dd)
        dh = jnp.concatenate(dh, axis=1)
        _, r = _rms(x, gg, D_MODEL)
        dxn, dg = _rms_bwd(x, r, gg, dh, D_MODEL)
        dx_ref[...] = dy + dxn

        @pl.when(i == 0)
        def _():
            dsc_ref[...] = jnp.zeros_like(dsc_ref)
            db_ref[...] = jnp.zeros_like(db_ref)
            dln_ref[...] = jnp.zeros_like(dln_ref)

        dsc_ref[...] += jnp.concatenate(dsc, axis=1)
        db_ref[...] += jnp.sum(dyp32, axis=0, keepdims=True)
        dln_ref[...] += dg

    vec = jax.ShapeDtypeStruct((1, D_MODEL), F32)
    return _pcall(
        body, (x, dy, dy, d, g, wp, b, sc, _bands(t, False)), deps, name=name, grid=(nb,),
        out_shape=[jax.ShapeDtypeStruct((s, D_MODEL), F32), jax.ShapeDtypeStruct((s, D_MODEL), BF16), vec, vec, vec],
        in_specs=[_rows(t, D_MODEL), _rows(t, D_MODEL),
                  pl.BlockSpec((HALO, D_MODEL), lambda i: (jnp.minimum((i + 1) * rb, last_halo), 0)),
                  _rows(t, D_MODEL),
                  _full((1, D_MODEL)), _full((4, GROUP_DIM, GROUP_DIM)), _full((1, D_MODEL)), _full((1, D_MODEL)),
                  _full((4, t, t + HALO))],
        out_specs=[_rows(t, D_MODEL), _rows(t, D_MODEL), _full((1, D_MODEL)), _full((1, D_MODEL)), _full((1, D_MODEL))],
        compiler_params=_cparams(1, VMEM_MID),
    )


def _load_weights(w_hbm, w_vmem, sem):
    @pl.when(pl.program_id(0) == 0)
    def _():
        cp = pltpu.make_async_copy(w_hbm, w_vmem, sem)
        cp.start()
        cp.wait()


def _ffn_fwd(x, g, w, name):
    s = x.shape[0]
    t = min(512, s)

    def body(x_ref, g_ref, w_hbm, xo_ref, gate_ref, up_ref, w_ref, sem):
        _load_weights(w_hbm, w_ref, sem)
        x = x_ref[...]
        hn = _rms(x, g_ref[...], D_MODEL)[0].astype(BF16)
        acc = x
        for c in range(2):
            rs = slice(c * FF_HALF, (c + 1) * FF_HALF)
            gt = _nt(hn, w_ref[0, rs, :])
            up = _nt(hn, w_ref[1, rs, :])
            gate_ref[:, rs] = gt.astype(BF16)
            up_ref[:, rs] = up.astype(BF16)
            act = ((gt * _sigmoid(gt)) * up).astype(BF16)
            acc = acc + _nn(act, w_ref[2, rs, :])
        xo_ref[...] = acc

    hid = jax.ShapeDtypeStruct((s, D_FF), BF16)
    return pl.pallas_call(
        body, name=name, grid=(s // t,),
        out_shape=[jax.ShapeDtypeStruct((s, D_MODEL), F32), hid, hid],
        in_specs=[_rows(t, D_MODEL), _full((1, D_MODEL)), ANY],
        out_specs=[_rows(t, D_MODEL), _rows(t, D_FF), _rows(t, D_FF)],
        scratch_shapes=[pltpu.VMEM((3, D_FF, D_MODEL), BF16), pltpu.SemaphoreType.DMA],
        compiler_params=_cparams(1, VMEM_BIG),
    )(x, g, w)


def _ffn_bwd(x, dy, gate, up, g, w, name, deps=()):
    s = x.shape[0]
    t = min(256, s)

    def body(x_ref, dy_ref, gate_ref, up_ref, g_ref, w_hbm,
             dx_ref, act_ref, dg_ref, du_ref, hn_ref, dyb_ref, dln_ref, w_ref, sem):
        _load_weights(w_hbm, w_ref, sem)
        x = x_ref[...]
        gg = g_ref[...]
        y, r = _rms(x, gg, D_MODEL)
        hn = y.astype(BF16)
        hn_ref[...] = hn
        dy = dy_ref[...]
        dyb = dy.astype(BF16)
        dyb_ref[...] = dyb
        dh = jnp.zeros((t, D_MODEL), F32)
        for c in range(2):
            rs = slice(c * FF_HALF, (c + 1) * FF_HALF)
            gt = gate_ref[:, rs].astype(F32)
            u = up_ref[:, rs].astype(F32)
            sg = _sigmoid(gt)
            sl = gt * sg
            act_ref[:, rs] = (sl * u).astype(BF16)
            dact = _nt(dyb, w_ref[2, rs, :])
            dg = (dact * u * (sg * (1.0 + gt * (1.0 - sg)))).astype(BF16)
            du = (dact * sl).astype(BF16)
            dg_ref[:, rs] = dg
            du_ref[:, rs] = du
            dh = dh + _nn(dg, w_ref[0, rs, :]) + _nn(du, w_ref[1, rs, :])
        dxn, dgl = _rms_bwd(x, r, gg, dh, D_MODEL)
        dx_ref[...] = dy + dxn

        @pl.when(pl.program_id(0) == 0)
        def _():
            dln_ref[...] = jnp.zeros_like(dln_ref)

        dln_ref[...] += dgl

    hid = jax.ShapeDtypeStruct((s, D_FF), BF16)
    tok = jax.ShapeDtypeStruct((s, D_MODEL), BF16)
    return _pcall(
        body, (x, dy, gate, up, g, w), deps, name=name, grid=(s // t,),
        out_shape=[jax.ShapeDtypeStruct((s, D_MODEL), F32), hid, hid, hid, tok, tok,
                   jax.ShapeDtypeStruct((1, D_MODEL), F32)],
        in_specs=[_rows(t, D_MODEL), _rows(t, D_MODEL), _rows(t, D_FF), _rows(t, D_FF), _full((1, D_MODEL)), ANY],
        out_specs=[_rows(t, D_MODEL), _rows(t, D_FF), _rows(t, D_FF), _rows(t, D_FF),
                   _rows(t, D_MODEL), _rows(t, D_MODEL), _full((1, D_MODEL))],
        scratch_shapes=[pltpu.VMEM((3, D_FF, D_MODEL), BF16), pltpu.SemaphoreType.DMA],
        compiler_params=_cparams(1, VMEM_BIG),
    )


def _tn_matmul(a, b, into, p0, name, groups=1, m_chunk=None, deps=()):
    s = a.shape[0]
    m, n = a.shape[1] // groups, b.shape[1] // groups
    assert into.shape[1:] == (m, n)
    mc = m if m_chunk is None else m_chunk
    nm = m // mc
    t = min(2048, s)
    nt = s // t

    def body(a_ref, b_ref, into_ref, o_ref, acc):
        ti = pl.program_id(2)

        @pl.when(ti == 0)
        def _():
            acc[...] = jnp.zeros_like(acc)

        acc[...] += _tn(a_ref[...], b_ref[...])

        @pl.when(ti == nt - 1)
        def _():
            o_ref[...] = acc[...].astype(o_ref.dtype)

    return _pcall(
        body, (a, b, into), deps, name=name, grid=(groups, nm, nt),
        out_shape=jax.ShapeDtypeStruct(into.shape, into.dtype),
        in_specs=[pl.BlockSpec((t, mc), lambda gi, mi, ti: (ti, gi * nm + mi)),
                  pl.BlockSpec((t, n), lambda gi, mi, ti: (ti, gi)), ANY],
        out_specs=pl.BlockSpec((None, mc, n), lambda gi, mi, ti: (p0 + gi, mi, 0)),
        scratch_shapes=[pltpu.VMEM((mc, n), F32)],
        input_output_aliases={2: 0},
        compiler_params=_cparams(3, VMEM_BIG),
    )


def _rope_tables(positions):
    half = ROPE // 2
    inv = ROPE_THETA ** (-jnp.arange(half, dtype=F32) * 2.0 / ROPE)
    ang = positions.astype(F32)[:, None] * inv
    cos, sin = jnp.cos(ang), jnp.sin(ang)
    zero = jnp.zeros((positions.shape[0], LANES - ROPE), F32)
    return jnp.concatenate([cos, cos, zero], axis=1), jnp.concatenate([-sin, sin, zero], axis=1)


def _kv_specs(t):
    return [_full((1, D_MODEL)), _full((D_MODEL, KV_RANK)), _full((D_MODEL, LANES)), _full((1, KV_RANK)),
            _full((N_HEADS, KV_RANK, NOPE)), _full((N_HEADS, KV_RANK, V_DIM)),
            _full((1, NOPE)), _full((1, LANES)), _rows(t, LANES), _rows(t, LANES)]


def _kv_fwd(x, ln, wc, wpe, gl, wuk, wuv, gkn, gkr, cos, sin, name, deps=()):
    s = x.shape[0]
    t = min(PROJ_ROWS, s)

    def body(x_ref, ln_ref, wc_ref, wpe_ref, gl_ref, wuk_ref, wuv_ref, gkn_ref, gkr_ref, cos_ref, sin_ref,
             k_ref, v_ref):
        hn = _rms(x_ref[...], ln_ref[...], D_MODEL)[0].astype(BF16)
        clat = _nn(hn, wc_ref[...])
        kpe = _nn(hn, wpe_ref[...])
        cn = _rms(clat, gl_ref[...], KV_RANK)[0].astype(BF16)
        sspe = jnp.sum(kpe * kpe, axis=-1, keepdims=True)
        base = kpe * gkr_ref[...]
        rot = base * cos_ref[...] + _swap_halves(base, _swap_perm()) * sin_ref[...]
        for h in range(N_HEADS):
            kn = _nn(cn, wuk_ref[h])
            r = lax.rsqrt((jnp.sum(kn * kn, axis=-1, keepdims=True) + sspe) * (1.0 / QK_DIM) + EPS)
            k_ref[:, h * QK_PAD:h * QK_PAD + NOPE] = ((kn * r) * gkn_ref[...]).astype(BF16)
            k_ref[:, h * QK_PAD + NOPE:(h + 1) * QK_PAD] = (rot * r).astype(BF16)
            v_ref[:, h * V_DIM:(h + 1) * V_DIM] = _nn(cn, wuv_ref[h]).astype(BF16)

    return _pcall(
        body, (x, ln, wc, wpe, gl, wuk, wuv, gkn, gkr, cos, sin), deps, name=name, grid=(s // t,),
        out_shape=[jax.ShapeDtypeStruct((s, N_HEADS * QK_PAD), BF16), jax.ShapeDtypeStruct((s, N_HEADS * V_DIM), BF16)],
        in_specs=[_rows(t, D_MODEL)] + _kv_specs(t),
        out_specs=[_rows(t, N_HEADS * QK_PAD), _rows(t, N_HEADS * V_DIM)],
        compiler_params=_cparams(1, VMEM_MID),
    )


def _kv_bwd(x, dxin, dks, dvs, ln, wc, wpe, gl, wuk, wuv, gkn, gkr, cos, sin, name):
    s = x.shape[0]
    t = min(PROJ_ROWS, s)
    nk = len(dks)

    def body(*refs):
        x_ref, dxin_ref = refs[:2]
        dk_refs = refs[2:2 + nk]
        dv_refs = refs[2 + nk:2 + 2 * nk]
        (ln_ref, wc_ref, wpe_ref, gl_ref, wuk_ref, wuv_ref, gkn_ref, gkr_ref, cos_ref, sin_ref,
         dx_ref, hn_ref, cn_ref, dkn_ref, dvb_ref, dcc_ref, dpe_ref,
         dln_ref, dgl_ref, dgkn_ref, dgkr_ref) = refs[2 + 2 * nk:]
        x = x_ref[...]
        ln = ln_ref[...]
        y, rx = _rms(x, ln, D_MODEL)
        hn = y.astype(BF16)
        hn_ref[...] = hn
        clat = _nn(hn, wc_ref[...])
        kpe = _nn(hn, wpe_ref[...])
        gl = gl_ref[...]
        cy, rc = _rms(clat, gl, KV_RANK)
        cn = cy.astype(BF16)
        cn_ref[...] = cn
        sspe = jnp.sum(kpe * kpe, axis=-1, keepdims=True)
        cs, sn, perm = cos_ref[...], sin_ref[...], _swap_perm()
        gkn, gkr = gkn_ref[...], gkr_ref[...]
        base = kpe * gkr
        rot = base * cs + _swap_halves(base, perm) * sn
        dc = jnp.zeros((t, KV_RANK), F32)
        dkr_sum = jnp.zeros((t, LANES), F32)
        coef_sum = jnp.zeros((t, 1), F32)
        dgkn = jnp.zeros((1, NOPE), F32)
        for h in range(N_HEADS):
            kn = _nn(cn, wuk_ref[h])
            r = lax.rsqrt((jnp.sum(kn * kn, axis=-1, keepdims=True) + sspe) * (1.0 / QK_DIM) + EPS)
            lo, mid, hi = h * QK_PAD, h * QK_PAD + NOPE, (h + 1) * QK_PAD
            dko = dk_refs[0][:, lo:mid]
            dkr = dk_refs[0][:, mid:hi]
            dvh = dv_refs[0][:, h * V_DIM:(h + 1) * V_DIM]
            for j in range(1, nk):
                dko = dko + dk_refs[j][:, lo:mid]
                dkr = dkr + dk_refs[j][:, mid:hi]
                dvh = dvh + dv_refs[j][:, h * V_DIM:(h + 1) * V_DIM]
            un = dko * gkn
            sm = (jnp.sum(kn * un, axis=-1, keepdims=True) + jnp.sum(rot * dkr, axis=-1, keepdims=True)) * (1.0 / QK_DIM)
            coef = r * r * r * sm
            dkn = (r * un - kn * coef).astype(BF16)
            dkr_sum = dkr_sum + r * dkr
            coef_sum = coef_sum + coef
            dgkn = dgkn + jnp.sum(dko * (kn * r), axis=0, keepdims=True)
            dkn_ref[:, h * NOPE:(h + 1) * NOPE] = dkn
            dvb = dvh.astype(BF16)
            dvb_ref[:, h * V_DIM:(h + 1) * V_DIM] = dvb
            dc = dc + _nt(dkn, wuk_ref[h]) + _nt(dvb, wuv_ref[h])
        dz = dkr_sum * cs - _swap_halves(dkr_sum, perm) * sn
        dkpe = dz * gkr - kpe * coef_sum
        dgkr = jnp.sum(dz * kpe, axis=0, keepdims=True)
        dclat, dgl = _rms_bwd(clat, rc, gl, dc, KV_RANK)
        dcc = dclat.astype(BF16)
        dpe = dkpe.astype(BF16)
        dcc_ref[...] = dcc
        dpe_ref[...] = dpe
        dhn = _nt(dcc, wc_ref[...]) + _nt(dpe, wpe_ref[...])
        dxn, dln = _rms_bwd(x, rx, ln, dhn, D_MODEL)
        dx_ref[...] = dxin_ref[...] + dxn

        @pl.when(pl.program_id(0) == 0)
        def _():
            dln_ref[...] = jnp.zeros_like(dln_ref)
            dgl_ref[...] = jnp.zeros_like(dgl_ref)
            dgkn_ref[...] = jnp.zeros_like(dgkn_ref)
            dgkr_ref[...] = jnp.zeros_like(dgkr_ref)

        dln_ref[...] += dln
        dgl_ref[...] += dgl
        dgkn_ref[...] += dgkn
        dgkr_ref[...] += dgkr

    def tok(cols, dt):
        return jax.ShapeDtypeStruct((s, cols), dt)

    def vec(cols):
        return jax.ShapeDtypeStruct((1, cols), F32)

    return pl.pallas_call(
        body, name=name, grid=(s // t,),
        out_shape=[tok(D_MODEL, F32), tok(D_MODEL, BF16), tok(KV_RANK, BF16), tok(N_HEADS * NOPE, BF16),
                   tok(N_HEADS * V_DIM, BF16), tok(KV_RANK, BF16), tok(LANES, BF16),
                   vec(D_MODEL), vec(KV_RANK), vec(NOPE), vec(LANES)],
        in_specs=[_rows(t, D_MODEL), _rows(t, D_MODEL)] + [_rows(t, N_HEADS * QK_PAD)] * nk
                 + [_rows(t, N_HEADS * V_DIM)] * nk + _kv_specs(t),
        out_specs=[_rows(t, D_MODEL), _rows(t, D_MODEL), _rows(t, KV_RANK), _rows(t, N_HEADS * NOPE),
                   _rows(t, N_HEADS * V_DIM), _rows(t, KV_RANK), _rows(t, LANES),
                   _full((1, D_MODEL)), _full((1, KV_RANK)), _full((1, NOPE)), _full((1, LANES))],
        compiler_params=_cparams(1, VMEM_BIG),
    )(x, dxin, *dks, *dvs, ln, wc, wpe, gl, wuk, wuv, gkn, gkr, cos, sin)


def _q_specs(t):
    return [_full((1, D_MODEL)), _full((D_MODEL, Q_RANK)), _full((1, Q_RANK)), _full((N_HEADS, Q_RANK, QK_PAD)),
            _full((1, NOPE)), _full((1, LANES)), _rows(t, LANES), _rows(t, LANES)]


def _q_fwd(x, ln, wdq, gql, wuq, gqn, gqr, cos, sin, name, deps=()):
    s = x.shape[0]
    t = min(PROJ_ROWS, s)

    def body(x_ref, ln_ref, wdq_ref, gql_ref, wuq_ref, gqn_ref, gqr_ref, cos_ref, sin_ref, q_ref):
        hn = _rms(x_ref[...], ln_ref[...], D_MODEL)[0].astype(BF16)
        cqn = _rms(_nn(hn, wdq_ref[...]), gql_ref[...], Q_RANK)[0].astype(BF16)
        cs, sn, perm = cos_ref[...], sin_ref[...], _swap_perm()
        for h in range(N_HEADS):
            qa = _nn(cqn, wuq_ref[h])
            r = lax.rsqrt(jnp.sum(qa * qa, axis=-1, keepdims=True) * (1.0 / QK_DIM) + EPS)
            q_ref[:, h * QK_PAD:h * QK_PAD + NOPE] = ((qa[:, :NOPE] * r) * gqn_ref[...]).astype(BF16)
            z = (qa[:, NOPE:] * r) * gqr_ref[...]
            q_ref[:, h * QK_PAD + NOPE:(h + 1) * QK_PAD] = (z * cs + _swap_halves(z, perm) * sn).astype(BF16)

    return _pcall(
        body, (x, ln, wdq, gql, wuq, gqn, gqr, cos, sin), deps, name=name, grid=(s // t,),
        out_shape=jax.ShapeDtypeStruct((s, N_HEADS * QK_PAD), BF16),
        in_specs=[_rows(t, D_MODEL)] + _q_specs(t),
        out_specs=_rows(t, N_HEADS * QK_PAD),
        compiler_params=_cparams(1, VMEM_MID),
    )


def _q_bwd(x, dxin, dq, ln, wdq, gql, wuq, gqn, gqr, cos, sin, name):
    s = x.shape[0]
    t = min(PROJ_ROWS, s)

    def body(x_ref, dxin_ref, dq_ref, ln_ref, wdq_ref, gql_ref, wuq_ref, gqn_ref, gqr_ref, cos_ref, sin_ref,
             dx_ref, hn_ref, cqn_ref, dqa_ref, dcq_ref, dln_ref, dgql_ref, dgqn_ref, dgqr_ref):
        x = x_ref[...]
        ln = ln_ref[...]
        y, rx = _rms(x, ln, D_MODEL)
        hn = y.astype(BF16)
        hn_ref[...] = hn
        cqp = _nn(hn, wdq_ref[...])
        gql = gql_ref[...]
        cy, rc = _rms(cqp, gql, Q_RANK)
        cqn = cy.astype(BF16)
        cqn_ref[...] = cqn
        cs, sn, perm = cos_ref[...], sin_ref[...], _swap_perm()
        gqn, gqr = gqn_ref[...], gqr_ref[...]
        dcq = jnp.zeros((t, Q_RANK), F32)
        dgqn = jnp.zeros((1, NOPE), F32)
        dgqr = jnp.zeros((1, LANES), F32)
        for h in range(N_HEADS):
            qa = _nn(cqn, wuq_ref[h])
            qn, qr = qa[:, :NOPE], qa[:, NOPE:]
            r = lax.rsqrt(jnp.sum(qa * qa, axis=-1, keepdims=True) * (1.0 / QK_DIM) + EPS)
            dqo = dq_ref[:, h * QK_PAD:h * QK_PAD + NOPE]
            dqr = dq_ref[:, h * QK_PAD + NOPE:(h + 1) * QK_PAD]
            dz = dqr * cs - _swap_halves(dqr, perm) * sn
            un = dqo * gqn
            ur = dz * gqr
            sm = (jnp.sum(qn * un, axis=-1, keepdims=True) + jnp.sum(qr * ur, axis=-1, keepdims=True)) * (1.0 / QK_DIM)
            coef = r * r * r * sm
            dqa = jnp.concatenate([r * un - qn * coef, r * ur - qr * coef], axis=1).astype(BF16)
            dgqn = dgqn + jnp.sum(dqo * (qn * r), axis=0, keepdims=True)
            dgqr = dgqr + jnp.sum(dz * (qr * r), axis=0, keepdims=True)
            dqa_ref[:, h * QK_PAD:(h + 1) * QK_PAD] = dqa
            dcq = dcq + _nt(dqa, wuq_ref[h])
        dcqp, dgql = _rms_bwd(cqp, rc, gql, dcq, Q_RANK)
        dcqb = dcqp.astype(BF16)
        dcq_ref[...] = dcqb
        dhn = _nt(dcqb, wdq_ref[...])
        dxn, dln = _rms_bwd(x, rx, ln, dhn, D_MODEL)
        dx_ref[...] = dxin_ref[...] + dxn

        @pl.when(pl.program_id(0) == 0)
        def _():
            dln_ref[...] = jnp.zeros_like(dln_ref)
            dgql_ref[...] = jnp.zeros_like(dgql_ref)
            dgqn_ref[...] = jnp.zeros_like(dgqn_ref)
            dgqr_ref[...] = jnp.zeros_like(dgqr_ref)

        dln_ref[...] += dln
        dgql_ref[...] += dgql
        dgqn_ref[...] += dgqn
        dgqr_ref[...] += dgqr

    def tok(cols, dt):
        return jax.ShapeDtypeStruct((s, cols), dt)

    def vec(cols):
        return jax.ShapeDtypeStruct((1, cols), F32)

    return pl.pallas_call(
        body, name=name, grid=(s // t,),
        out_shape=[tok(D_MODEL, F32), tok(D_MODEL, BF16), tok(Q_RANK, BF16), tok(N_HEADS * QK_PAD, BF16),
                   tok(Q_RANK, BF16), vec(D_MODEL), vec(Q_RANK), vec(NOPE), vec(LANES)],
        in_specs=[_rows(t, D_MODEL), _rows(t, D_MODEL), _rows(t, N_HEADS * QK_PAD)] + _q_specs(t),
        out_specs=[_rows(t, D_MODEL), _rows(t, D_MODEL), _rows(t, Q_RANK), _rows(t, N_HEADS * QK_PAD),
                   _rows(t, Q_RANK), _full((1, D_MODEL)), _full((1, Q_RANK)), _full((1, NOPE)), _full((1, LANES))],
        compiler_params=_cparams(1, VMEM_MID),
    )(x, dxin, dq, ln, wdq, gql, wuq, gqn, gqr, cos, sin)


SM_SCALE = 1.0 / math.sqrt(QK_DIM)
LOG2_E = math.log2(math.e)
EXP2_SCALE = SM_SCALE * LOG2_E
NEG = -1e30


def _diag_mask(t):
    qpos = lax.broadcasted_iota(jnp.int32, (t, t), 0)
    kpos = lax.broadcasted_iota(jnp.int32, (t, t), 1)
    return lax.shift_right_logical(kpos, 6) <= lax.shift_right_logical(qpos, 6)


def _att_fwd(q, k, v, name):
    s = q.shape[0]
    t = min(512, s)
    nb = s // t

    def body(q_ref, k_ref, v_ref, o_ref, lse_ref):
        qi = pl.program_id(1)
        qq = q_ref[...]

        def block(ki, carry, masked):
            m_old, l_old, acc = carry
            rows = pl.ds(pl.multiple_of(ki * t, t), t)
            sc = _nt(qq, k_ref[rows, :])
            if masked:
                sc = jnp.where(_diag_mask(t), sc, NEG)
            m_new = jnp.maximum(m_old, jnp.max(sc, axis=-1, keepdims=True))
            p = jnp.exp2((sc - m_new) * EXP2_SCALE)
            alpha = jnp.exp2((m_old - m_new) * EXP2_SCALE)
            l_new = alpha * l_old + jnp.sum(p, axis=-1, keepdims=True)
            acc = alpha * acc + _nn(p.astype(BF16), v_ref[rows, :])
            return m_new, l_new, acc

        init = (jnp.full((t, 1), NEG, F32), jnp.zeros((t, 1), F32), jnp.zeros((t, V_DIM), F32))
        carry = lax.fori_loop(0, qi // 2, lambda j, c: block(2 * j + 1, block(2 * j, c, False), False), init)
        carry = lax.cond(qi % 2 == 1, lambda c: block(qi - 1, c, False), lambda c: c, carry)
        m_fin, l_fin, acc = block(qi, carry, True)
        o_ref[...] = (acc / l_fin).astype(BF16)
        lse_ref[...] = jnp.broadcast_to(m_fin * SM_SCALE + jnp.log(l_fin), (t, LANES))

    return pl.pallas_call(
        body, name=name, grid=(N_HEADS, nb),
        out_shape=[jax.ShapeDtypeStruct((s, N_HEADS * V_DIM), BF16), jax.ShapeDtypeStruct((s, N_HEADS * LANES), F32)],
        in_specs=[pl.BlockSpec((t, QK_PAD), lambda h, qi: (qi, h)),
                  pl.BlockSpec((s, QK_PAD), lambda h, qi: (0, h)),
                  pl.BlockSpec((s, V_DIM), lambda h, qi: (0, h))],
        out_specs=[pl.BlockSpec((t, V_DIM), lambda h, qi: (qi, h)),
                   pl.BlockSpec((t, LANES), lambda h, qi: (qi, h))],
        compiler_params=_cparams(2, VMEM_MID),
    )(q, k, v)


def _att_bwd(q, k, v, do, o, lse, name, deps=()):
    s = q.shape[0]
    t = min(512, s)
    nb = s // t

    def body(q_ref, k_ref, v_ref, do_ref, o_ref, lse_ref, dq_ref, dk_ref, dv_ref):
        ki = pl.program_id(1)
        kk, vv = k_ref[...], v_ref[...]

        @pl.when(ki == 0)
        def _():
            dq_ref[...] = jnp.zeros_like(dq_ref)

        def block(qi, carry, masked):
            dk, dv = carry
            rows = pl.ds(pl.multiple_of(qi * t, t), t)
            qq, dob = q_ref[rows, :], do_ref[rows, :]
            sc = _nt(qq, kk)
            if masked:
                sc = jnp.where(_diag_mask(t), sc, NEG)
            p = jnp.exp2(sc * EXP2_SCALE - lse_ref[rows, :][:, :1] * LOG2_E)
            dp = _nt(dob, vv)
            dsum = jnp.sum(dob.astype(F32) * o_ref[rows, :].astype(F32), axis=-1, keepdims=True)
            ds = (p * (dp - dsum)).astype(BF16)
            dq_ref[rows, :] += _nn(ds, kk)
            return dk + _tn(ds, qq), dv + _tn(p.astype(BF16), dob)

        carry = block(ki, (jnp.zeros((t, QK_PAD), F32), jnp.zeros((t, V_DIM), F32)), True)
        rest = nb - 1 - ki
        carry = lax.fori_loop(
            0, rest // 2, lambda j, c: block(ki + 2 * j + 2, block(ki + 2 * j + 1, c, False), False), carry)
        dk, dv = lax.cond(rest % 2 == 1, lambda c: block(nb - 1, c, False), lambda c: c, carry)
        dk_ref[...] = dk * SM_SCALE
        dv_ref[...] = dv

        @pl.when(ki == nb - 1)
        def _():
            dq_ref[...] = dq_ref[...] * SM_SCALE

    def head(h, ki):
        return (0, h)

    def kblock(h, ki):
        return (ki, h)

    return _pcall(
        body, (q, k, v, do, o, lse), deps, name=name, grid=(N_HEADS, nb),
        out_shape=[jax.ShapeDtypeStruct((s, N_HEADS * QK_PAD), F32), jax.ShapeDtypeStruct((s, N_HEADS * QK_PAD), F32),
                   jax.ShapeDtypeStruct((s, N_HEADS * V_DIM), F32)],
        in_specs=[pl.BlockSpec((s, QK_PAD), head), pl.BlockSpec((t, QK_PAD), kblock), pl.BlockSpec((t, V_DIM), kblock),
                  pl.BlockSpec((s, V_DIM), head), pl.BlockSpec((s, V_DIM), head), pl.BlockSpec((s, LANES), head)],
        out_specs=[pl.BlockSpec((s, QK_PAD), head), pl.BlockSpec((t, QK_PAD), kblock), pl.BlockSpec((t, V_DIM), kblock)],
        compiler_params=_cparams(2, VMEM_MID),
    )


def _o_fwd(x, o, wo, name):
    s = x.shape[0]
    t = min(512, s)

    def body(x_ref, o_ref, wo_ref, xo_ref):
        xo_ref[...] = x_ref[...] + _nn(o_ref[...], wo_ref[...])

    return pl.pallas_call(
        body, name=name, grid=(s // t,),
        out_shape=jax.ShapeDtypeStruct((s, D_MODEL), F32),
        in_specs=[_rows(t, D_MODEL), _rows(t, D_MODEL), _full((D_MODEL, D_MODEL))],
        out_specs=_rows(t, D_MODEL),
        compiler_params=_cparams(1, VMEM_MID),
    )(x, o, wo)


def _o_bwd(dx, wo, name, deps=()):
    s = dx.shape[0]
    t = min(512, s)

    def body(dx_ref, wo_ref, do_ref, dxb_ref):
        dxb = dx_ref[...].astype(BF16)
        dxb_ref[...] = dxb
        do_ref[...] = _nt(dxb, wo_ref[...]).astype(BF16)

    tok = jax.ShapeDtypeStruct((s, D_MODEL), BF16)
    return _pcall(
        body, (dx, wo), deps, name=name, grid=(s // t,),
        out_shape=[tok, tok],
        in_specs=[_rows(t, D_MODEL), _full((D_MODEL, D_MODEL))],
        out_specs=[_rows(t, D_MODEL), _rows(t, D_MODEL)],
        compiler_params=_cparams(1, VMEM_MID),
    )


def _loss_head(y, target, name):
    s = y.shape[0]
    t = min(512, s)

    def body(y_ref, t_ref, dy_ref, sq_ref):
        e = y_ref[...] - t_ref[...]
        dy_ref[...] = e * (1.0 / D_MODEL)

        @pl.when(pl.program_id(0) == 0)
        def _():
            sq_ref[...] = jnp.zeros_like(sq_ref)

        sq_ref[...] += jnp.sum(e * e, axis=0, keepdims=True)

    return pl.pallas_call(
        body, name=name, grid=(s // t,),
        out_shape=[jax.ShapeDtypeStruct((s, D_MODEL), F32), jax.ShapeDtypeStruct((1, D_MODEL), F32)],
        in_specs=[_rows(t, D_MODEL), _rows(t, D_MODEL)],
        out_specs=[_rows(t, D_MODEL), _full((1, D_MODEL))],
        compiler_params=_cparams(1),
    )(y, target)


def _adamw(w, g, m, v, name):
    shape = w.shape
    c = shape[-1]
    r = math.prod(shape[:-1])
    tb = r
    for cand in (512, 256, 128):
        if r % cand == 0 and r > cand:
            tb = cand
            break

    def body(w_ref, g_ref, m_ref, v_ref, d_ref, mo_ref, vo_ref):
        gr = g_ref[...]
        mn = ADAM_B1 * m_ref[...] + (1.0 - ADAM_B1) * gr
        vn = ADAM_B2 * v_ref[...] + (1.0 - ADAM_B2) * (gr * gr)
        m_hat = mn / (1.0 - ADAM_B1 ** ADAM_STEP)
        v_hat = vn / (1.0 - ADAM_B2 ** ADAM_STEP)
        d_ref[...] = -ADAM_LR * (m_hat / (jnp.sqrt(v_hat) + ADAM_EPS) + ADAM_WD * w_ref[...])
        mo_ref[...] = mn
        vo_ref[...] = vn

    spec = pl.BlockSpec((tb, c), lambda i: (i, 0))
    flat = jax.ShapeDtypeStruct((r, c), F32)
    outs = pl.pallas_call(
        body, name=name, grid=(r // tb,),
        out_shape=[flat, flat, flat],
        in_specs=[spec] * 4, out_specs=[spec] * 3,
        compiler_params=_cparams(1),
    )(w.reshape(r, c), g.reshape(r, c), m.reshape(r, c), v.reshape(r, c))
    return [a.reshape(shape) for a in outs]


def _pad_cols(a, width):
    return jnp.pad(a, [(0, 0)] * (a.ndim - 1) + [(0, width - a.shape[-1])])


def _owner_view(a, sz):
    return a.reshape(a.shape[0], N_CHIPS, 2, sz, a.shape[-1])


def kernel(x, positions, ln_mix_a, w_pool, b_pool, pool_scale, ln_ffn, w_gate, w_up, w_down, ln_kv, w_dkv, g_kv_latent, w_uk, w_uv, g_k, ln_mix_b, w_dq, g_q_latent, w_uq, g_q, w_o, loss_target, m_ln_mix_a, m_w_pool, m_b_pool, m_pool_scale, m_ln_ffn, m_w_gate, m_w_up, m_w_down, m_ln_kv, m_w_dkv, m_g_kv_latent, m_w_uk, m_w_uv, m_g_k, m_ln_mix_b, m_w_dq, m_g_q_latent, m_w_uq, m_g_q, m_w_o, v_ln_mix_a, v_w_pool, v_b_pool, v_pool_scale, v_ln_ffn, v_w_gate, v_w_up, v_w_down, v_ln_kv, v_w_dkv, v_g_kv_latent, v_w_uk, v_w_uv, v_g_k, v_ln_mix_b, v_w_dq, v_g_q_latent, v_w_uq, v_g_q, v_w_o):
    weights = dict(ln_mix_a=ln_mix_a, w_pool=w_pool, b_pool=b_pool, pool_scale=pool_scale, ln_ffn=ln_ffn,
                   w_gate=w_gate, w_up=w_up, w_down=w_down, ln_kv=ln_kv, w_dkv=w_dkv, g_kv_latent=g_kv_latent,
                   w_uk=w_uk, w_uv=w_uv, g_k=g_k, ln_mix_b=ln_mix_b, w_dq=w_dq, g_q_latent=g_q_latent,
                   w_uq=w_uq, g_q=g_q, w_o=w_o)
    mom1 = dict(ln_mix_a=m_ln_mix_a, w_pool=m_w_pool, b_pool=m_b_pool, pool_scale=m_pool_scale, ln_ffn=m_ln_ffn,
                w_gate=m_w_gate, w_up=m_w_up, w_down=m_w_down, ln_kv=m_ln_kv, w_dkv=m_w_dkv,
                g_kv_latent=m_g_kv_latent, w_uk=m_w_uk, w_uv=m_w_uv, g_k=m_g_k, ln_mix_b=m_ln_mix_b, w_dq=m_w_dq,
                g_q_latent=m_g_q_latent, w_uq=m_w_uq, g_q=m_g_q, w_o=m_w_o)
    mom2 = dict(ln_mix_a=v_ln_mix_a, w_pool=v_w_pool, b_pool=v_b_pool, pool_scale=v_pool_scale, ln_ffn=v_ln_ffn,
                w_gate=v_w_gate, w_up=v_w_up, w_down=v_w_down, ln_kv=v_ln_kv, w_dkv=v_w_dkv,
                g_kv_latent=v_g_kv_latent, w_uk=v_w_uk, w_uv=v_w_uv, g_k=v_g_k, ln_mix_b=v_ln_mix_b, w_dq=v_w_dq,
                g_q_latent=v_g_q_latent, w_uq=v_w_uq, g_q=v_g_q, w_o=v_w_o)
    names = list(weights)
    dev = 4 * lax.axis_index("x") + 2 * lax.axis_index("y") + lax.axis_index("c")
    core = lax.axis_index("c").astype(jnp.int32).reshape(1)
    chip = (2 * lax.axis_index("x") + lax.axis_index("y")).astype(jnp.int32).reshape(1)

    xs = x[0]
    target = loss_target[0]
    cos, sin = _rope_tables(positions[0])

    small_sh = jnp.concatenate([ln_mix_a.reshape(1, -1), pool_scale.reshape(1, -1), b_pool.reshape(1, -1)], axis=1)
    wp_g, small_g = _all_gather([w_pool.astype(BF16), small_sh], [2, 0], "gather_first")
    wp_all = wp_g.reshape(2, 4, GROUP_DIM, GROUP_DIM)
    small_g = small_g.reshape(N_DEV, 3, 2, LANES)
    ln_a_all = small_g[:, 0].transpose(1, 0, 2).reshape(2, 1, D_MODEL)
    sc_all = small_g[:, 1].transpose(1, 0, 2).reshape(2, 1, D_MODEL)
    bp_all = small_g[:, 2].reshape(N_DEV, 2, 4, 32).transpose(1, 2, 0, 3).reshape(2, 1, D_MODEL)

    def placed(shard):
        buf = lax.empty((shard.shape[0], N_DEV) + shard.shape[1:], shard.dtype)
        return lax.dynamic_update_slice(buf, shard[:, None], (0, dev, 0, 0))

    groups = {f"ffn{l}": [placed(jnp.stack([w_gate[l].T, w_up[l].T, w_down[l]]).astype(BF16))] for l in range(4)}
    groups["att"] = [placed(a.astype(BF16)) for a in (
        w_dkv[None, :, :KV_RANK], _pad_cols(w_dkv[None, :, KV_RANK:], LANES), w_uk[None], w_uv[None],
        w_dq, _pad_cols(w_uq, QK_PAD), w_o)]
    def spread_start(nm, deps):
        return _copies_start(groups[nm], len(groups[nm]), _gather_spread, f"spread_{nm}", deps=deps)

    def spread_wait(nm, state, after):
        ssem, rsem, bufs, _ = state
        return _copies_wait(bufs, ssem, rsem, after, _blocks_moved(4), f"spread_done_{nm}")

    def relay_start(nm, bufs, deps=()):
        return _copies_start(bufs, len(bufs), _gather_relay, f"relay_{nm}", deps=deps)

    def relay_wait(nm, state, after):
        ssem, rsem, bufs, _ = state
        return _copies_wait(bufs, ssem, rsem, after, _blocks_moved(3), f"relay_done_{nm}")

    gkn = g_k[:NOPE].reshape(1, NOPE)
    gkr = _pad_cols(g_k[NOPE:].reshape(1, ROPE), LANES)
    gl = g_kv_latent.reshape(1, KV_RANK)
    lnkv = ln_kv.reshape(1, D_MODEL)

    x_in, x_mid, pooled, gates, ups, w_ffn = [], [], [], [], [], []
    qs, outs, lses = [], [], []

    def mixer(l, cur, deps):
        x_in.append(cur)
        mid, dsave = _mix_fwd(cur, ln_a_all[l], wp_all[l], bp_all[l], sc_all[l], f"mix_fwd{l}", deps=deps)
        pooled.append(dsave)
        x_mid.append(mid)
        return mid

    def q_args(j):
        return (ln_mix_b[j].reshape(1, -1), wdq_all[j], g_q_latent[j].reshape(1, -1), wuq_all[j],
                g_q[j, :NOPE].reshape(1, -1), _pad_cols(g_q[j, NOPE:].reshape(1, -1), LANES), cos, sin)

    def attention(j, cur, deps):
        x_in.append(cur)
        q = _q_fwd(cur, *q_args(j), f"q_fwd{j}", deps=deps)
        o, lse = _att_fwd(q, k_sh, v_sh, f"att_fwd{j}")
        mid = _o_fwd(cur, o, wo_all[j], f"o_fwd{j}")
        qs.append(q)
        outs.append(o)
        lses.append(lse)
        x_mid.append(mid)
        return mid

    def ffn(l, mid, relayed):
        w_l = relayed[0].reshape(3, D_FF, D_MODEL)
        w_ffn.append(w_l)
        cur, gate, up = _ffn_fwd(mid, ln_ffn[l].reshape(1, -1), w_l, f"ffn_fwd{l}")
        gates.append(gate)
        ups.append(up)
        return cur

    sp0 = spread_start("ffn0", [small_g])
    mid = mixer(0, xs, [sp0[3]])
    landed0 = spread_wait("ffn0", sp0, mid)
    sp1 = spread_start("ffn1", [landed0[0]])
    rl0 = relay_start("ffn0", landed0, [sp1[3]])
    cur = ffn(0, mid, relay_wait("ffn0", rl0, rl0[3]))

    landed1 = spread_wait("ffn1", sp1, cur)
    sp_att = spread_start("att", [landed1[0]])
    sp2 = spread_start("ffn2", [landed1[0]])
    rl1 = relay_start("ffn1", landed1, [sp_att[3], sp2[3]])
    mid = mixer(1, cur, [rl1[3]])
    cur = ffn(1, mid, relay_wait("ffn1", rl1, mid))
    x_kv = cur

    landed_att = spread_wait("att", sp_att, cur)
    landed2 = spread_wait("ffn2", sp2, cur)
    sp3 = spread_start("ffn3", [landed2[0]])
    rl_att = relay_start("att", landed_att, [sp3[3]])
    rl2 = relay_start("ffn2", landed2, [sp3[3]])
    att_bufs = relay_wait("att", rl_att, rl2[3])
    wc = att_bufs[0].reshape(D_MODEL, KV_RANK)
    wpe = att_bufs[1].reshape(D_MODEL, LANES)
    wuk_g = att_bufs[2].reshape(N_HEADS, KV_RANK, NOPE)
    wuv_g = att_bufs[3].reshape(N_HEADS, KV_RANK, V_DIM)
    wdq_all = att_bufs[4].reshape(2, D_MODEL, Q_RANK)
    wuq_all = att_bufs[5]
    wo_all = att_bufs[6].reshape(2, D_MODEL, D_MODEL)
    k_sh, v_sh = _kv_fwd(cur, lnkv, wc, wpe, gl, wuk_g, wuv_g, gkn, gkr, cos, sin, "kv_fwd")
    mid = attention(0, cur, [])
    cur = ffn(2, mid, relay_wait("ffn2", rl2, mid))

    landed3 = spread_wait("ffn3", sp3, cur)
    rl3 = relay_start("ffn3", landed3)
    mid = attention(1, cur, [rl3[3]])
    cur = ffn(3, mid, relay_wait("ffn3", rl3, mid))

    dx, sq_cols = _loss_head(cur, target, "loss_head")

    small = {}
    sizes = dict(ffn0=FF_SHARD, ffn1=FF_SHARD, ffn2=FF_SHARD, ffn3=FF_SHARD, wo=128, kv512=128, dkv_pe=128,
                 wdq=128, wuqT=QK_PAD, wpool=32)
    big = dict(wo=lax.empty((2, D_MODEL, D_MODEL), BF16), kv512=lax.empty((3, D_MODEL, KV_RANK), BF16),
               dkv_pe=lax.empty((1, D_MODEL, LANES), BF16), wdq=lax.empty((2, D_MODEL, Q_RANK), BF16),
               wuqT=lax.empty((2, N_HEADS * QK_PAD, Q_RANK), BF16), wpool=lax.empty((8, GROUP_DIM, GROUP_DIM), BF16))
    for l in range(4):
        big[f"ffn{l}"] = lax.empty((3, D_FF, D_MODEL), BF16)
    red = {}

    def pair_start(nms, tag):
        arrs = []
        for nm in nms:
            view = _owner_view(big[nm], sizes[nm])
            arrs += [view, lax.empty((view.shape[0], N_CHIPS) + view.shape[3:], BF16)]
        return nms, tag, _copies_start(arrs, len(nms), _pair_send, f"pair_start_{tag}")

    def chip_start(state, after):
        nms, tag, (ssem, rsem, arrs, _) = state
        arrs = _copies_wait(arrs, ssem, rsem, after, _landed, f"pair_done_{tag}")
        out = []
        for t, nm in enumerate(nms):
            part = _pair_sum(arrs[2 * t], arrs[2 * t + 1], core, f"pair_sum_{nm}")
            out += [part, lax.empty((3, part.shape[0]) + part.shape[2:], BF16)]
        return nms, tag, _copies_start(out, len(nms), _chip_send, f"chip_start_{tag}")

    def chip_finish(state, after):
        nms, tag, (ssem, rsem, arrs, _) = state
        arrs = _copies_wait(arrs, ssem, rsem, after, _landed, f"chip_done_{tag}")
        for t, nm in enumerate(nms):
            red[nm] = _chip_sum(arrs[2 * t], arrs[2 * t + 1], chip, f"chip_sum_{nm}")

    dks, dvs = [], []
    pending = None
    bwd_deps = []
    for l in (3, 2, 1, 0):
        key = f"ffn{l}"
        dx, act, dgb, dub, hn, dyb, dln = _ffn_bwd(x_mid[l], dx, gates[l], ups[l], ln_ffn[l].reshape(1, -1),
                                                     w_ffn[l], f"ffn_bwd{l}", deps=bwd_deps)
        bwd_deps = []
        small[f"ln_ffn{l}"] = dln
        if l == 1:
            att_chip = chip_start(att_pair, dx)
            tn_deps = [att_chip[2][3]]
        else:
            tn_deps = []
        if pending:
            chip_finish(pending, dx)
            pending = None
        big[key] = _tn_matmul(dgb, hn, big[key], 0, f"dw_gate{l}", m_chunk=FF_HALF, deps=tn_deps)
        big[key] = _tn_matmul(dub, hn, big[key], 1, f"dw_up{l}", m_chunk=FF_HALF)
        big[key] = _tn_matmul(act, dyb, big[key], 2, f"dw_down{l}", m_chunk=FF_HALF)
        if l == 1:
            chip_finish(att_chip, big[key])
        ffn_pair = pair_start([key], key)
        if l >= 2:
            j = l - 2
            do, dxb = _o_bwd(dx, wo_all[j], f"o_bwd{j}", deps=[ffn_pair[2][3]])
            big["wo"] = _tn_matmul(outs[j], dxb, big["wo"], j, f"dw_o{j}")
            ffn_chip = chip_start(ffn_pair, big["wo"])
            dq, dk, dv = _att_bwd(qs[j], k_sh, v_sh, do, outs[j], lses[j], f"att_bwd{j}", deps=[ffn_chip[2][3]])
            chip_finish(ffn_chip, dq)
            dks.append(dk)
            dvs.append(dv)
            dx, hnq, cqn, dqa, dcq, dln, dgql, dgqn, dgqr = _q_bwd(x_in[l], dx, dq, *q_args(j), f"q_bwd{j}")
            small[f"ln_mix_b{j}"] = dln
            small[f"g_q_latent{j}"] = dgql
            small[f"g_q{j}"] = jnp.concatenate([dgqn, dgqr[:, :ROPE]], axis=1)
            big["wdq"] = _tn_matmul(hnq, dcq, big["wdq"], j, f"dw_dq{j}")
            big["wuqT"] = _tn_matmul(dqa, cqn, big["wuqT"], j, f"dw_uq{j}")
            if l == 2:
                (dx, hnk, cn, dknb, dvb, dccb, dpeb, dlnkv, dgl, dgkn, dgkr) = _kv_bwd(
                    x_kv, dx, dks, dvs, lnkv, wc, wpe, gl, wuk_g, wuv_g, gkn, gkr, cos, sin, "kv_bwd")
                small["ln_kv"] = dlnkv
                small["g_kv_latent"] = dgl
                small["g_k"] = jnp.concatenate([dgkn, dgkr[:, :ROPE]], axis=1)
                big["kv512"] = _tn_matmul(dknb, cn, big["kv512"], 0, "dw_uk")
                big["kv512"] = _tn_matmul(dvb, cn, big["kv512"], 1, "dw_uv")
                big["kv512"] = _tn_matmul(hnk, dccb, big["kv512"], 2, "dw_dkv_c")
                big["dkv_pe"] = _tn_matmul(hnk, dpeb, big["dkv_pe"], 0, "dw_dkv_pe")
                att_pair = pair_start(["wo", "kv512", "dkv_pe", "wdq", "wuqT"], "att")
                bwd_deps = [att_pair[2][3]]
        else:
            dx, dyp, dsc, db, dln = _mix_bwd(x_in[l], dx, pooled[l], ln_a_all[l], wp_all[l], bp_all[l], sc_all[l],
                                             f"mix_bwd{l}", deps=[ffn_pair[2][3]])
            small[f"ln_mix_a{l}"] = dln
            small[f"pool_scale{l}"] = dsc
            small[f"b_pool{l}"] = db
            ffn_chip = chip_start(ffn_pair, dx)
            big["wpool"] = _tn_matmul(pooled[l], dyp, big["wpool"], 4 * l, f"dw_pool{l}", groups=4,
                                      deps=[ffn_chip[2][3]])
            if l == 1:
                pending = ffn_chip
            else:
                chip_finish(ffn_chip, big["wpool"])
    grad_x = dx[None]
    pool_pair = pair_start(["wpool"], "wpool")
    pool_chip = chip_start(pool_pair, pool_pair[2][3])
    chip_finish(pool_chip, pool_chip[2][3])

    vec_names = (["loss"] + [f"ln_ffn{l}" for l in range(4)] + ["ln_kv", "g_kv_latent", "g_k"]
                 + [f"{p}{j}" for p in ("ln_mix_b", "g_q_latent", "g_q") for j in range(2)]
                 + [f"{p}{l}" for p in ("ln_mix_a", "pool_scale", "b_pool") for l in range(2)])
    small["loss"] = sq_cols
    widths = [small[nm].shape[1] for nm in vec_names]
    padded = [-(-w // LANES) * LANES for w in widths]
    packed = jnp.concatenate([_pad_cols(small[nm], pw) for nm, pw in zip(vec_names, padded)], axis=1)
    (all_vecs,) = _all_gather([packed], [0], "gather_vectors")
    total = _sum_lead(all_vecs, "sum_vectors")
    vec = {}
    off = 0
    for nm, w, pw in zip(vec_names, widths, padded):
        vec[nm] = total[0, off:off + w]
        off += pw
    loss = 0.5 * jnp.sum(vec["loss"]) * (1.0 / D_MODEL)

    def own_cols(full, width):
        return lax.dynamic_slice_in_dim(full, dev * width, width, axis=full.ndim - 1)

    grads = dict(
        ln_mix_a=own_cols(jnp.stack([vec["ln_mix_a0"], vec["ln_mix_a1"]]), LANES),
        w_pool=red["wpool"].reshape(2, 4, 32, GROUP_DIM),
        b_pool=own_cols(jnp.stack([vec["b_pool0"], vec["b_pool1"]]).reshape(2, 4, GROUP_DIM), 32),
        pool_scale=own_cols(jnp.stack([vec["pool_scale0"], vec["pool_scale1"]]), LANES),
        ln_ffn=jnp.stack([vec[f"ln_ffn{l}"] for l in range(4)]),
        w_gate=jnp.stack([red[f"ffn{l}"][0] for l in range(4)]).transpose(0, 2, 1),
        w_up=jnp.stack([red[f"ffn{l}"][1] for l in range(4)]).transpose(0, 2, 1),
        w_down=jnp.stack([red[f"ffn{l}"][2] for l in range(4)]),
        ln_kv=vec["ln_kv"],
        w_dkv=jnp.concatenate([red["kv512"][2], red["dkv_pe"][0][:, :ROPE]], axis=1),
        g_kv_latent=vec["g_kv_latent"],
        w_uk=red["kv512"][0].T,
        w_uv=red["kv512"][1].T,
        g_k=vec["g_k"],
        ln_mix_b=jnp.stack([vec["ln_mix_b0"], vec["ln_mix_b1"]]),
        w_dq=red["wdq"],
        g_q_latent=jnp.stack([vec["g_q_latent0"], vec["g_q_latent1"]]),
        w_uq=red["wuqT"].transpose(0, 2, 1)[:, :, :QK_DIM],
        g_q=jnp.stack([vec["g_q0"], vec["g_q1"]]),
        w_o=red["wo"],
    )

    deltas, new_m, new_v = {}, {}, {}
    for nm in names:
        w = weights[nm]
        shape = w.shape if w.ndim > 1 else (1, w.shape[0])
        d, mo, vo = _adamw(w.reshape(shape), grads[nm].reshape(shape), mom1[nm].reshape(shape),
                           mom2[nm].reshape(shape), f"adamw_{nm}")
        deltas[nm], new_m[nm], new_v[nm] = d.reshape(w.shape), mo.reshape(w.shape), vo.reshape(w.shape)

    return (loss, grad_x, *[grads[nm].reshape(weights[nm].shape) for nm in names], *[deltas[nm] for nm in names],
            *[new_m[nm] for nm in names], *[new_v[nm] for nm in names])
```

```python
import functools
import math

import jax
import jax.numpy as jnp
from jax import lax
from jax.experimental import pallas as pl
from jax.experimental.pallas import tpu as pltpu

F32 = jnp.float32
BF16 = jnp.bfloat16
MESH = pl.DeviceIdType.MESH

D_MODEL = 1024
D_FF = 2816
N_DEV = 8
N_CHIPS = 4
FF_SHARD = D_FF // N_DEV
FF_HALF = D_FF // 2
N_HEADS = 8
NOPE = 128
ROPE = 64
QK_DIM = NOPE + ROPE
QK_PAD = 256
V_DIM = 128
Q_RANK = 256
KV_RANK = 512
POOL_WINDOWS = (2, 4, 8, 16)
GROUP_DIM = 256
HALO = 128
CHUNK = 64
ROPE_THETA = 10000.0
EPS = 1e-6
LANES = 128

ADAM_LR = 0.001
ADAM_B1 = 0.9
ADAM_B2 = 0.999
ADAM_EPS = 1e-08
ADAM_WD = 0.01
ADAM_STEP = 10

PROJ_ROWS = 256
VMEM_BIG = 56 * 2**20
VMEM_MID = 40 * 2**20


def _nn(a, b):
    return lax.dot_general(a, b, (((1,), (0,)), ((), ())), preferred_element_type=F32)


def _nt(a, b):
    return lax.dot_general(a, b, (((1,), (1,)), ((), ())), preferred_element_type=F32)


def _tn(a, b):
    return lax.dot_general(a, b, (((0,), (0,)), ((), ())), preferred_element_type=F32)


def _rms(x, g, n):
    r = lax.rsqrt(jnp.sum(x * x, axis=-1, keepdims=True) * (1.0 / n) + EPS)
    return (x * r) * g, r


def _rms_bwd(x, r, g, dy, n):
    u = dy * g
    s = jnp.sum(x * u, axis=-1, keepdims=True) * (1.0 / n)
    dx = r * u - x * (r * r * r * s)
    dg = jnp.sum(dy * (x * r), axis=0, keepdims=True)
    return dx, dg


def _swap_perm():
    i = lax.broadcasted_iota(jnp.int32, (LANES, LANES), 0)
    j = lax.broadcasted_iota(jnp.int32, (LANES, LANES), 1)
    half = ROPE // 2
    hit = ((j < half) & (i == j + half)) | ((j >= half) & (j < ROPE) & (i == j - half))
    return jnp.where(hit, 1.0, 0.0).astype(BF16)


def _swap_halves(z, perm):
    hi = z.astype(BF16)
    lo = (z - hi.astype(F32)).astype(BF16)
    return _nn(hi, perm) + _nn(lo, perm)


def _sigmoid(x):
    return 1.0 / (1.0 + jnp.exp(-x))


def _cparams(n_grid, vmem=None):
    return pltpu.CompilerParams(dimension_semantics=("arbitrary",) * n_grid, vmem_limit_bytes=vmem)


def _rows(t, cols):
    return pl.BlockSpec((t, cols), lambda i: (i, 0))


def _full(shape):
    nd = len(shape)
    return pl.BlockSpec(shape, lambda *_: (0,) * nd)


ANY = pl.BlockSpec(memory_space=pl.ANY)


def _pcall(body, args, deps, *, in_specs, **kw):
    n_in, n_dep = len(args), len(deps)

    def ordered(*refs):
        body(*refs[:n_in], *refs[n_in + n_dep:])

    return pl.pallas_call(ordered, in_specs=list(in_specs) + [ANY] * n_dep, **kw)(*args, *deps)


def _place():
    x, y, c = lax.axis_index("x"), lax.axis_index("y"), lax.axis_index("c")
    return x, y, c


def _all_gather(shards, axes, name, deps=()):
    n, nd = len(shards), len(deps)
    out_shape = [jax.ShapeDtypeStruct(s.shape[:a] + (N_DEV,) + s.shape[a:], s.dtype) for s, a in zip(shards, axes)]

    def body(*refs):
        ins, outs = refs[:n], refs[n + nd:2 * n + nd]
        send_sems, recv_sems, local_sems = refs[2 * n + nd:]
        x, y, c = _place()
        me, sibling = (x, y, c), (x, y, 1 - c)
        chips = [(1 - x, y), (x, 1 - y), (1 - x, 1 - y)]

        def slot(t, dev):
            idx = 4 * dev[0] + 2 * dev[1] + dev[2]
            return outs[t].at[(slice(None),) * axes[t] + (idx,)]

        def copy(t, k, block, to, src=None):
            return pltpu.make_async_remote_copy(
                src_ref=slot(t, block) if src is None else src, dst_ref=slot(t, block),
                send_sem=send_sems.at[t, k], recv_sem=recv_sems.at[t, k],
                device_id=to, device_id_type=MESH)

        mine = [pltpu.make_async_copy(ins[t], slot(t, me), local_sems.at[t]) for t in range(n)]
        for cp in mine:
            cp.start()
        first = []
        for t in range(n):
            first.append(copy(t, 0, me, sibling, src=ins[t]))
            first += [copy(t, 1 + j, me, (*chip, c), src=ins[t]) for j, chip in enumerate(chips)]
        for cp in first:
            cp.start()
        passed = []
        for j, chip in enumerate(chips):
            for t in range(n):
                copy(t, 1 + j, (*chip, c), me).wait_recv()
                cp = copy(t, 4 + j, (*chip, c), sibling)
                cp.start()
                passed.append(cp)
        for t in range(n):
            copy(t, 0, sibling, me).wait_recv()
            for j, chip in enumerate(chips):
                copy(t, 4 + j, (*chip, 1 - c), me).wait_recv()
        for cp in first + passed:
            cp.wait_send()
        for cp in mine:
            cp.wait()

    return pl.pallas_call(
        body, name=name, out_shape=out_shape,
        in_specs=[ANY] * (n + nd), out_specs=[ANY] * n,
        scratch_shapes=[pltpu.SemaphoreType.DMA((n, 7)), pltpu.SemaphoreType.DMA((n, 7)),
                        pltpu.SemaphoreType.DMA((n,))],
    )(*shards, *deps)


HBM = pl.BlockSpec(memory_space=pltpu.HBM)
SEM = pl.BlockSpec(memory_space=pltpu.SEMAPHORE)
EFFECT = pltpu.SideEffectType.DATAFLOW_SIDE_EFFECTING


def _copies_start(arrays, n_sems, plan, name, deps=()):
    n, nd = len(arrays), len(deps)

    def body(*refs):
        for cp in plan(refs[:n], refs[n + nd], refs[n + nd + 1]):
            cp.start()
        refs[-1][...] = jnp.zeros_like(refs[-1])

    outs = pl.pallas_call(
        body, name=name,
        out_shape=(pltpu.SemaphoreType.DMA((n_sems,)), pltpu.SemaphoreType.DMA((n_sems,)),
                   *[pltpu.HBM(a.shape, a.dtype) for a in arrays], jax.ShapeDtypeStruct((8, LANES), F32)),
        in_specs=[HBM] * n + [ANY] * nd,
        out_specs=(SEM, SEM, *[HBM] * n, pl.BlockSpec(memory_space=pltpu.VMEM)),
        input_output_aliases={i: 2 + i for i in range(n)},
        compiler_params=pltpu.CompilerParams(has_side_effects=EFFECT),
    )(*[pltpu.with_memory_space_constraint(a, pltpu.HBM) for a in arrays], *deps)
    return outs[0], outs[1], list(outs[2:2 + n]), outs[-1]


def _copies_wait(arrays, send_sems, recv_sems, after, plan, name):
    n = len(arrays)
    after = list(after) if isinstance(after, (list, tuple)) else [after]

    def body(*refs):
        for cp in plan(refs[:n], refs[n], refs[n + 1]):
            cp.wait_send()
            cp.wait_recv()

    outs = pl.pallas_call(
        body, name=name,
        out_shape=tuple(pltpu.HBM(a.shape, a.dtype) for a in arrays),
        in_specs=[HBM] * n + [SEM, SEM] + [ANY] * len(after), out_specs=tuple([HBM] * n),
        input_output_aliases={i: i for i in range(n)},
        compiler_params=pltpu.CompilerParams(has_side_effects=EFFECT),
    )(*arrays, send_sems, recv_sems, *after)
    return list(outs)


def _remote(src, dst, send_sems, recv_sems, t, to):
    return pltpu.make_async_remote_copy(src_ref=src, dst_ref=dst, send_sem=send_sems.at[t], recv_sem=recv_sems.at[t],
                                        device_id=to, device_id_type=MESH)


def _dev_index(x, y, c):
    return 4 * x + 2 * y + c


def _gather_spread(bufs, send_sems, recv_sems):
    x, y, c = _place()
    mine = _dev_index(x, y, c)
    peers = [(x, y, 1 - c), (1 - x, y, c), (x, 1 - y, c), (1 - x, 1 - y, c)]
    return [_remote(g.at[k, mine], g.at[k, mine], send_sems, recv_sems, t, peer)
            for t, g in enumerate(bufs) for peer in peers for k in range(g.shape[0])]


def _gather_relay(bufs, send_sems, recv_sems):
    x, y, c = _place()
    blocks = [_dev_index(1 - x, y, c), _dev_index(x, 1 - y, c), _dev_index(1 - x, 1 - y, c)]
    return [_remote(g.at[k, b], g.at[k, b], send_sems, recv_sems, t, (x, y, 1 - c))
            for t, g in enumerate(bufs) for b in blocks for k in range(g.shape[0])]


def _blocks_moved(count):
    def plan(bufs, send_sems, recv_sems):
        x, y, c = _place()
        return [_remote(g.at[:, pl.ds(0, count)], g.at[:, pl.ds(0, count)], send_sems, recv_sems, t, (x, y, 1 - c))
                for t, g in enumerate(bufs)]
    return plan


def _pair_send(arrs, send_sems, recv_sems):
    x, y, c = _place()
    return [_remote(arrs[2 * t].at[p, k, 1 - c], arrs[2 * t + 1].at[p, k], send_sems, recv_sems, t, (x, y, 1 - c))
            for t in range(len(arrs) // 2) for p in range(arrs[2 * t].shape[0]) for k in range(N_CHIPS)]


def _chip_send(arrs, send_sems, recv_sems):
    x, y, c = _place()
    chips = [(1 - x, y), (x, 1 - y), (1 - x, 1 - y)]
    return [_remote(arrs[2 * t].at[p, 2 * px + py], arrs[2 * t + 1].at[j, p], send_sems, recv_sems, t, (px, py, c))
            for t in range(len(arrs) // 2) for j, (px, py) in enumerate(chips) for p in range(arrs[2 * t].shape[0])]


def _landed(arrs, send_sems, recv_sems):
    x, y, c = _place()
    return [_remote(arrs[2 * t + 1], arrs[2 * t + 1], send_sems, recv_sems, t, (x, y, 1 - c))
            for t in range(len(arrs) // 2)]


def _rows_per_step(rows, row_elems):
    best = 1
    for cand in range(1, rows + 1):
        if rows % cand == 0 and cand * row_elems <= 256 * 1024:
            best = cand
    return best


def _pair_sum(grad, landed, core, name):
    p, _, _, sz, c = grad.shape
    r = _rows_per_step(p * N_CHIPS, sz * c)

    def body(core_ref, g_ref, l_ref, o_ref):
        o_ref[...] = (g_ref[...].astype(F32) + l_ref[...].astype(F32)).astype(o_ref.dtype)

    out = pl.pallas_call(
        body, name=name,
        grid_spec=pltpu.PrefetchScalarGridSpec(
            num_scalar_prefetch=1, grid=(p * N_CHIPS // r,),
            in_specs=[pl.BlockSpec((r, None, sz, c), lambda i, cr: (i, cr[0], 0, 0)),
                      pl.BlockSpec((r, sz, c), lambda i, cr: (i, 0, 0))],
            out_specs=pl.BlockSpec((r, sz, c), lambda i, cr: (i, 0, 0))),
        out_shape=jax.ShapeDtypeStruct((p * N_CHIPS, sz, c), grad.dtype),
        compiler_params=_cparams(1),
    )(core, grad.reshape(p * N_CHIPS, 2, sz, c), landed.reshape(p * N_CHIPS, sz, c))
    return out.reshape(p, N_CHIPS, sz, c)


def _chip_sum(parts, landed, chip, name):
    p, _, sz, c = parts.shape
    r = _rows_per_step(p, sz * c)

    def body(chip_ref, a_ref, l_ref, o_ref):
        acc = a_ref[...].astype(F32)
        for j in range(3):
            acc = acc + l_ref[j].astype(F32)
        o_ref[...] = acc

    return pl.pallas_call(
        body, name=name,
        grid_spec=pltpu.PrefetchScalarGridSpec(
            num_scalar_prefetch=1, grid=(p // r,),
            in_specs=[pl.BlockSpec((r, None, sz, c), lambda i, cr: (i, cr[0], 0, 0)),
                      pl.BlockSpec((3, r, sz, c), lambda i, cr: (0, i, 0, 0))],
            out_specs=pl.BlockSpec((r, sz, c), lambda i, cr: (i, 0, 0))),
        out_shape=jax.ShapeDtypeStruct((p, sz, c), F32),
        compiler_params=_cparams(1),
    )(chip, parts, landed)


def _sum_lead(a, name, out_dtype=F32):
    k = a.shape[0]
    rest = a.shape[1:]
    r, c = rest[-2], rest[-1]
    lead = math.prod(rest[:-2])
    a3 = a.reshape(k, lead * r, c)
    rows = lead * r
    tb = rows
    for cand in (512, 256, 128, 64, 32, 16, 8):
        if rows % cand == 0 and rows > cand:
            tb = cand
            break

    def body(a_ref, o_ref):
        acc = a_ref[0].astype(F32)
        for i in range(1, k):
            acc = acc + a_ref[i].astype(F32)
        o_ref[...] = acc.astype(out_dtype)

    out = pl.pallas_call(
        body, name=name, grid=(rows // tb,),
        out_shape=jax.ShapeDtypeStruct((rows, c), out_dtype),
        in_specs=[pl.BlockSpec((k, tb, c), lambda i: (0, i, 0))],
        out_specs=pl.BlockSpec((tb, c), lambda i: (i, 0)),
        compiler_params=_cparams(1),
    )(a3)
    return out.reshape(rest)


def _bands(t, causal):
    r = lax.broadcasted_iota(jnp.int32, (t, t + HALO), 0)
    col = lax.broadcasted_iota(jnp.int32, (t, t + HALO), 1)
    diff = r + HALO - col if causal else col - r
    return jnp.stack([jnp.where((diff >= 0) & (diff < w), 1.0, 0.0) for w in POOL_WINDOWS]).astype(BF16)


def _split_dot(band, v):
    hi = v.astype(BF16)
    lo = (v - hi.astype(F32)).astype(BF16)
    return _nn(band, hi) + _nn(band, lo)


def _mix_fwd(x, g, wp, b, sc, name, deps=()):
    s = x.shape[0]
    t = min(256, s)
    rb = t // HALO

    def body(x_ref, xh_ref, g_ref, wp_ref, b_ref, sc_ref, band_ref, xo_ref, d_ref):
        i = pl.program_id(0)
        gg = g_ref[...]
        h, _ = _rms(x_ref[...], gg, D_MODEL)
        hh, _ = _rms(xh_ref[...], gg, D_MODEL)
        hh = jnp.where(i > 0, hh, 0.0)
        hext = jnp.concatenate([hh, h], axis=0)
        tok = i * t + lax.broadcasted_iota(jnp.int32, (t, 1), 0)
        for gi, w in enumerate(POOL_WINDOWS):
            sl = slice(gi * GROUP_DIM, (gi + 1) * GROUP_DIM)
            win = _split_dot(band_ref[gi], hext[:, sl])
            cnt = jnp.minimum(tok + 1, w).astype(F32)
            dbf = (win / cnt - h[:, sl]).astype(BF16)
            d_ref[:, sl] = dbf
            ypre = _nn(dbf, wp_ref[gi]) + b_ref[:, sl]
            xo_ref[:, sl] = x_ref[:, sl] + ypre * sc_ref[:, sl]

    return _pcall(
        body, (x, x, g, wp, b, sc, _bands(t, True)), deps, name=name, grid=(s // t,),
        out_shape=[jax.ShapeDtypeStruct((s, D_MODEL), F32), jax.ShapeDtypeStruct((s, D_MODEL), BF16)],
        in_specs=[_rows(t, D_MODEL),
                  pl.BlockSpec((HALO, D_MODEL), lambda i: (jnp.maximum(i * rb - 1, 0), 0)),
                  _full((1, D_MODEL)), _full((4, GROUP_DIM, GROUP_DIM)), _full((1, D_MODEL)), _full((1, D_MODEL)),
                  _full((4, t, t + HALO))],
        out_specs=[_rows(t, D_MODEL), _rows(t, D_MODEL)],
        compiler_params=_cparams(1, VMEM_MID),
    )


def _mix_bwd(x, dy, d, g, wp, b, sc, name, deps=()):
    s = x.shape[0]
    t = min(256, s)
    rb = t // HALO
    nb = s // t
    last_halo = s // HALO - 1

    def body(x_ref, dy_ref, dyn_ref, d_ref, g_ref, wp_ref, b_ref, sc_ref, band_ref,
             dx_ref, dyp_ref, dsc_ref, db_ref, dln_ref):
        i = pl.program_id(0)
        x = x_ref[...]
        gg = g_ref[...]
        dy = dy_ref[...]
        sc = sc_ref[...]
        dyp32 = dy * sc
        dyp = dyp32.astype(BF16)
        dyph = (dyn_ref[...] * sc).astype(BF16)
        dyp_ref[...] = dyp
        tok = i * t + lax.broadcasted_iota(jnp.int32, (t + HALO, 1), 0)
        dh, dsc = [], []
        for gi, w in enumerate(POOL_WINDOWS):
            sl = slice(gi * GROUP_DIM, (gi + 1) * GROUP_DIM)
            ypre = _nn(d_ref[:, sl], wp_ref[gi]) + b_ref[:, sl]
            dsc.append(jnp.sum(dy[:, sl] * ypre, axis=0, keepdims=True))
            dd = _nt(dyp[:, sl], wp_ref[gi])
            ddh = jnp.where(i < nb - 1, _nt(dyph[:, sl], wp_ref[gi]), 0.0)
            cnt = jnp.minimum(tok + 1, w).astype(F32)
            ddext = jnp.concatenate([dd, ddh], axis=0) / cnt
            dh.append(_split_dot(band_ref[gi], ddext) - dd)
        dh = jnp.concatenate(dh, axis=1)
        _, r = _rms(x, gg, D_MODEL)
        dxn, dg = _rms_bwd(x, r, gg, dh, D_MODEL)
        dx_ref[...] = dy + dxn

        @pl.when(i == 0)
        def _():
            dsc_ref[...] = jnp.zeros_like(dsc_ref)
            db_ref[...] = jnp.zeros_like(db_ref)
            dln_ref[...] = jnp.zeros_like(dln_ref)

        dsc_ref[...] += jnp.concatenate(dsc, axis=1)
        db_ref[...] += jnp.sum(dyp32, axis=0, keepdims=True)
        dln_ref[...] += dg

    vec = jax.ShapeDtypeStruct((1, D_MODEL), F32)
    return _pcall(
        body, (x, dy, dy, d, g, wp, b, sc, _bands(t, False)), deps, name=name, grid=(nb,),
        out_shape=[jax.ShapeDtypeStruct((s, D_MODEL), F32), jax.ShapeDtypeStruct((s, D_MODEL), BF16), vec, vec, vec],
        in_specs=[_rows(t, D_MODEL), _rows(t, D_MODEL),
                  pl.BlockSpec((HALO, D_MODEL), lambda i: (jnp.minimum((i + 1) * rb, last_halo), 0)),
                  _rows(t, D_MODEL),
                  _full((1, D_MODEL)), _full((4, GROUP_DIM, GROUP_DIM)), _full((1, D_MODEL)), _full((1, D_MODEL)),
                  _full((4, t, t + HALO))],
        out_specs=[_rows(t, D_MODEL), _rows(t, D_MODEL), _full((1, D_MODEL)), _full((1, D_MODEL)), _full((1, D_MODEL))],
        compiler_params=_cparams(1, VMEM_MID),
    )


def _load_weights(w_hbm, w_vmem, sem):
    @pl.when(pl.program_id(0) == 0)
    def _():
        cp = pltpu.make_async_copy(w_hbm, w_vmem, sem)
        cp.start()
        cp.wait()


def _ffn_fwd(x, g, w, name):
    s = x.shape[0]
    t = min(512, s)

    def body(x_ref, g_ref, w_hbm, xo_ref, gate_ref, up_ref, w_ref, sem):
        _load_weights(w_hbm, w_ref, sem)
        x = x_ref[...]
        hn = _rms(x, g_ref[...], D_MODEL)[0].astype(BF16)
        acc = x
        for c in range(2):
            rs = slice(c * FF_HALF, (c + 1) * FF_HALF)
            gt = _nt(hn, w_ref[0, rs, :])
            up = _nt(hn, w_ref[1, rs, :])
            gate_ref[:, rs] = gt.astype(BF16)
            up_ref[:, rs] = up.astype(BF16)
            act = ((gt * _sigmoid(gt)) * up).astype(BF16)
            acc = acc + _nn(act, w_ref[2, rs, :])
        xo_ref[...] = acc

    hid = jax.ShapeDtypeStruct((s, D_FF), BF16)
    return pl.pallas_call(
        body, name=name, grid=(s // t,),
        out_shape=[jax.ShapeDtypeStruct((s, D_MODEL), F32), hid, hid],
        in_specs=[_rows(t, D_MODEL), _full((1, D_MODEL)), ANY],
        out_specs=[_rows(t, D_MODEL), _rows(t, D_FF), _rows(t, D_FF)],
        scratch_shapes=[pltpu.VMEM((3, D_FF, D_MODEL), BF16), pltpu.SemaphoreType.DMA],
        compiler_params=_cparams(1, VMEM_BIG),
    )(x, g, w)


def _ffn_bwd(x, dy, gate, up, g, w, name, deps=()):
    s = x.shape[0]
    t = min(256, s)

    def body(x_ref, dy_ref, gate_ref, up_ref, g_ref, w_hbm,
             dx_ref, act_ref, dg_ref, du_ref, hn_ref, dyb_ref, dln_ref, w_ref, sem):
        _load_weights(w_hbm, w_ref, sem)
        x = x_ref[...]
        gg = g_ref[...]
        y, r = _rms(x, gg, D_MODEL)
        hn = y.astype(BF16)
        hn_ref[...] = hn
        dy = dy_ref[...]
        dyb = dy.astype(BF16)
        dyb_ref[...] = dyb
        dh = jnp.zeros((t, D_MODEL), F32)
        for c in range(2):
            rs = slice(c * FF_HALF, (c + 1) * FF_HALF)
            gt = gate_ref[:, rs].astype(F32)
            u = up_ref[:, rs].astype(F32)
            sg = _sigmoid(gt)
            sl = gt * sg
            act_ref[:, rs] = (sl * u).astype(BF16)
            dact = _nt(dyb, w_ref[2, rs, :])
            dg = (dact * u * (sg * (1.0 + gt * (1.0 - sg)))).astype(BF16)
            du = (dact * sl).astype(BF16)
            dg_ref[:, rs] = dg
            du_ref[:, rs] = du
            dh = dh + _nn(dg, w_ref[0, rs, :]) + _nn(du, w_ref[1, rs, :])
        dxn, dgl = _rms_bwd(x, r, gg, dh, D_MODEL)
        dx_ref[...] = dy + dxn

        @pl.when(pl.program_id(0) == 0)
        def _():
            dln_ref[...] = jnp.zeros_like(dln_ref)

        dln_ref[...] += dgl

    hid = jax.ShapeDtypeStruct((s, D_FF), BF16)
    tok = jax.ShapeDtypeStruct((s, D_MODEL), BF16)
    return _pcall(
        body, (x, dy, gate, up, g, w), deps, name=name, grid=(s // t,),
        out_shape=[jax.ShapeDtypeStruct((s, D_MODEL), F32), hid, hid, hid, tok, tok,
                   jax.ShapeDtypeStruct((1, D_MODEL), F32)],
        in_specs=[_rows(t, D_MODEL), _rows(t, D_MODEL), _rows(t, D_FF), _rows(t, D_FF), _full((1, D_MODEL)), ANY],
        out_specs=[_rows(t, D_MODEL), _rows(t, D_FF), _rows(t, D_FF), _rows(t, D_FF),
                   _rows(t, D_MODEL), _rows(t, D_MODEL), _full((1, D_MODEL))],
        scratch_shapes=[pltpu.VMEM((3, D_FF, D_MODEL), BF16), pltpu.SemaphoreType.DMA],
        compiler_params=_cparams(1, VMEM_BIG),
    )


def _tn_matmul(a, b, into, p0, name, groups=1, m_chunk=None, deps=()):
    s = a.shape[0]
    m, n = a.shape[1] // groups, b.shape[1] // groups
    assert into.shape[1:] == (m, n)
    mc = m if m_chunk is None else m_chunk
    nm = m // mc
    t = min(1024, s)
    nt = s // t

    def body(a_ref, b_ref, into_ref, o_ref, acc):
        ti = pl.program_id(2)

        @pl.when(ti == 0)
        def _():
            acc[...] = jnp.zeros_like(acc)

        acc[...] += _tn(a_ref[...], b_ref[...])

        @pl.when(ti == nt - 1)
        def _():
            o_ref[...] = acc[...].astype(o_ref.dtype)

    return _pcall(
        body, (a, b, into), deps, name=name, grid=(groups, nm, nt),
        out_shape=jax.ShapeDtypeStruct(into.shape, into.dtype),
        in_specs=[pl.BlockSpec((t, mc), lambda gi, mi, ti: (ti, gi * nm + mi)),
                  pl.BlockSpec((t, n), lambda gi, mi, ti: (ti, gi)), ANY],
        out_specs=pl.BlockSpec((None, mc, n), lambda gi, mi, ti: (p0 + gi, mi, 0)),
        scratch_shapes=[pltpu.VMEM((mc, n), F32)],
        input_output_aliases={2: 0},
        compiler_params=_cparams(3, VMEM_BIG),
    )


def _rope_tables(positions):
    half = ROPE // 2
    inv = ROPE_THETA ** (-jnp.arange(half, dtype=F32) * 2.0 / ROPE)
    ang = positions.astype(F32)[:, None] * inv
    cos, sin = jnp.cos(ang), jnp.sin(ang)
    zero = jnp.zeros((positions.shape[0], LANES - ROPE), F32)
    return jnp.concatenate([cos, cos, zero], axis=1), jnp.concatenate([-sin, sin, zero], axis=1)


def _kv_specs(t):
    return [_full((1, D_MODEL)), _full((D_MODEL, KV_RANK)), _full((D_MODEL, LANES)), _full((1, KV_RANK)),
            _full((N_HEADS, KV_RANK, NOPE)), _full((N_HEADS, KV_RANK, V_DIM)),
            _full((1, NOPE)), _full((1, LANES)), _rows(t, LANES), _rows(t, LANES)]


def _kv_fwd(x, ln, wc, wpe, gl, wuk, wuv, gkn, gkr, cos, sin, name, deps=()):
    s = x.shape[0]
    t = min(PROJ_ROWS, s)

    def body(x_ref, ln_ref, wc_ref, wpe_ref, gl_ref, wuk_ref, wuv_ref, gkn_ref, gkr_ref, cos_ref, sin_ref,
             k_ref, v_ref):
        hn = _rms(x_ref[...], ln_ref[...], D_MODEL)[0].astype(BF16)
        clat = _nn(hn, wc_ref[...])
        kpe = _nn(hn, wpe_ref[...])
        cn = _rms(clat, gl_ref[...], KV_RANK)[0].astype(BF16)
        sspe = jnp.sum(kpe * kpe, axis=-1, keepdims=True)
        base = kpe * gkr_ref[...]
        rot = base * cos_ref[...] + _swap_halves(base, _swap_perm()) * sin_ref[...]
        for h in range(N_HEADS):
            kn = _nn(cn, wuk_ref[h])
            r = lax.rsqrt((jnp.sum(kn * kn, axis=-1, keepdims=True) + sspe) * (1.0 / QK_DIM) + EPS)
            k_ref[:, h * QK_PAD:h * QK_PAD + NOPE] = ((kn * r) * gkn_ref[...]).astype(BF16)
            k_ref[:, h * QK_PAD + NOPE:(h + 1) * QK_PAD] = (rot * r).astype(BF16)
            v_ref[:, h * V_DIM:(h + 1) * V_DIM] = _nn(cn, wuv_ref[h]).astype(BF16)

    return _pcall(
        body, (x, ln, wc, wpe, gl, wuk, wuv, gkn, gkr, cos, sin), deps, name=name, grid=(s // t,),
        out_shape=[jax.ShapeDtypeStruct((s, N_HEADS * QK_PAD), BF16), jax.ShapeDtypeStruct((s, N_HEADS * V_DIM), BF16)],
        in_specs=[_rows(t, D_MODEL)] + _kv_specs(t),
        out_specs=[_rows(t, N_HEADS * QK_PAD), _rows(t, N_HEADS * V_DIM)],
        compiler_params=_cparams(1, VMEM_MID),
    )


def _kv_bwd(x, dxin, dks, dvs, ln, wc, wpe, gl, wuk, wuv, gkn, gkr, cos, sin, name):
    s = x.shape[0]
    t = min(PROJ_ROWS, s)
    nk = len(dks)

    def body(*refs):
        x_ref, dxin_ref = refs[:2]
        dk_refs = refs[2:2 + nk]
        dv_refs = refs[2 + nk:2 + 2 * nk]
        (ln_ref, wc_ref, wpe_ref, gl_ref, wuk_ref, wuv_ref, gkn_ref, gkr_ref, cos_ref, sin_ref,
         dx_ref, hn_ref, cn_ref, dkn_ref, dvb_ref, dcc_ref, dpe_ref,
         dln_ref, dgl_ref, dgkn_ref, dgkr_ref) = refs[2 + 2 * nk:]
        x = x_ref[...]
        ln = ln_ref[...]
        y, rx = _rms(x, ln, D_MODEL)
        hn = y.astype(BF16)
        hn_ref[...] = hn
        clat = _nn(hn, wc_ref[...])
        kpe = _nn(hn, wpe_ref[...])
        gl = gl_ref[...]
        cy, rc = _rms(clat, gl, KV_RANK)
        cn = cy.astype(BF16)
        cn_ref[...] = cn
        sspe = jnp.sum(kpe * kpe, axis=-1, keepdims=True)
        cs, sn, perm = cos_ref[...], sin_ref[...], _swap_perm()
        gkn, gkr = gkn_ref[...], gkr_ref[...]
        base = kpe * gkr
        rot = base * cs + _swap_halves(base, perm) * sn
        dc = jnp.zeros((t, KV_RANK), F32)
        dkr_sum = jnp.zeros((t, LANES), F32)
        coef_sum = jnp.zeros((t, 1), F32)
        dgkn = jnp.zeros((1, NOPE), F32)
        for h in range(N_HEADS):
            kn = _nn(cn, wuk_ref[h])
            r = lax.rsqrt((jnp.sum(kn * kn, axis=-1, keepdims=True) + sspe) * (1.0 / QK_DIM) + EPS)
            lo, mid, hi = h * QK_PAD, h * QK_PAD + NOPE, (h + 1) * QK_PAD
            dko = dk_refs[0][:, lo:mid]
            dkr = dk_refs[0][:, mid:hi]
            dvh = dv_refs[0][:, h * V_DIM:(h + 1) * V_DIM]
            for j in range(1, nk):
                dko = dko + dk_refs[j][:, lo:mid]
                dkr = dkr + dk_refs[j][:, mid:hi]
                dvh = dvh + dv_refs[j][:, h * V_DIM:(h + 1) * V_DIM]
            un = dko * gkn
            sm = (jnp.sum(kn * un, axis=-1, keepdims=True) + jnp.sum(rot * dkr, axis=-1, keepdims=True)) * (1.0 / QK_DIM)
            coef = r * r * r * sm
            dkn = (r * un - kn * coef).astype(BF16)
            dkr_sum = dkr_sum + r * dkr
            coef_sum = coef_sum + coef
            dgkn = dgkn + jnp.sum(dko * (kn * r), axis=0, keepdims=True)
            dkn_ref[:, h * NOPE:(h + 1) * NOPE] = dkn
            dvb = dvh.astype(BF16)
            dvb_ref[:, h * V_DIM:(h + 1) * V_DIM] = dvb
            dc = dc + _nt(dkn, wuk_ref[h]) + _nt(dvb, wuv_ref[h])
        dz = dkr_sum * cs - _swap_halves(dkr_sum, perm) * sn
        dkpe = dz * gkr - kpe * coef_sum
        dgkr = jnp.sum(dz * kpe, axis=0, keepdims=True)
        dclat, dgl = _rms_bwd(clat, rc, gl, dc, KV_RANK)
        dcc = dclat.astype(BF16)
        dpe = dkpe.astype(BF16)
        dcc_ref[...] = dcc
        dpe_ref[...] = dpe
        dhn = _nt(dcc, wc_ref[...]) + _nt(dpe, wpe_ref[...])
        dxn, dln = _rms_bwd(x, rx, ln, dhn, D_MODEL)
        dx_ref[...] = dxin_ref[...] + dxn

        @pl.when(pl.program_id(0) == 0)
        def _():
            dln_ref[...] = jnp.zeros_like(dln_ref)
            dgl_ref[...] = jnp.zeros_like(dgl_ref)
            dgkn_ref[...] = jnp.zeros_like(dgkn_ref)
            dgkr_ref[...] = jnp.zeros_like(dgkr_ref)

        dln_ref[...] += dln
        dgl_ref[...] += dgl
        dgkn_ref[...] += dgkn
        dgkr_ref[...] += dgkr

    def tok(cols, dt):
        return jax.ShapeDtypeStruct((s, cols), dt)

    def vec(cols):
        return jax.ShapeDtypeStruct((1, cols), F32)

    return pl.pallas_call(
        body, name=name, grid=(s // t,),
        out_shape=[tok(D_MODEL, F32), tok(D_MODEL, BF16), tok(KV_RANK, BF16), tok(N_HEADS * NOPE, BF16),
                   tok(N_HEADS * V_DIM, BF16), tok(KV_RANK, BF16), tok(LANES, BF16),
                   vec(D_MODEL), vec(KV_RANK), vec(NOPE), vec(LANES)],
        in_specs=[_rows(t, D_MODEL), _rows(t, D_MODEL)] + [_rows(t, N_HEADS * QK_PAD)] * nk
                 + [_rows(t, N_HEADS * V_DIM)] * nk + _kv_specs(t),
        out_specs=[_rows(t, D_MODEL), _rows(t, D_MODEL), _rows(t, KV_RANK), _rows(t, N_HEADS * NOPE),
                   _rows(t, N_HEADS * V_DIM), _rows(t, KV_RANK), _rows(t, LANES),
                   _full((1, D_MODEL)), _full((1, KV_RANK)), _full((1, NOPE)), _full((1, LANES))],
        compiler_params=_cparams(1, VMEM_BIG),
    )(x, dxin, *dks, *dvs, ln, wc, wpe, gl, wuk, wuv, gkn, gkr, cos, sin)


def _q_specs(t):
    return [_full((1, D_MODEL)), _full((D_MODEL, Q_RANK)), _full((1, Q_RANK)), _full((N_HEADS, Q_RANK, QK_PAD)),
            _full((1, NOPE)), _full((1, LANES)), _rows(t, LANES), _rows(t, LANES)]


def _q_fwd(x, ln, wdq, gql, wuq, gqn, gqr, cos, sin, name, deps=()):
    s = x.shape[0]
    t = min(PROJ_ROWS, s)

    def body(x_ref, ln_ref, wdq_ref, gql_ref, wuq_ref, gqn_ref, gqr_ref, cos_ref, sin_ref, q_ref):
        hn = _rms(x_ref[...], ln_ref[...], D_MODEL)[0].astype(BF16)
        cqn = _rms(_nn(hn, wdq_ref[...]), gql_ref[...], Q_RANK)[0].astype(BF16)
        cs, sn, perm = cos_ref[...], sin_ref[...], _swap_perm()
        for h in range(N_HEADS):
            qa = _nn(cqn, wuq_ref[h])
            r = lax.rsqrt(jnp.sum(qa * qa, axis=-1, keepdims=True) * (1.0 / QK_DIM) + EPS)
            q_ref[:, h * QK_PAD:h * QK_PAD + NOPE] = ((qa[:, :NOPE] * r) * gqn_ref[...]).astype(BF16)
            z = (qa[:, NOPE:] * r) * gqr_ref[...]
            q_ref[:, h * QK_PAD + NOPE:(h + 1) * QK_PAD] = (z * cs + _swap_halves(z, perm) * sn).astype(BF16)

    return _pcall(
        body, (x, ln, wdq, gql, wuq, gqn, gqr, cos, sin), deps, name=name, grid=(s // t,),
        out_shape=jax.ShapeDtypeStruct((s, N_HEADS * QK_PAD), BF16),
        in_specs=[_rows(t, D_MODEL)] + _q_specs(t),
        out_specs=_rows(t, N_HEADS * QK_PAD),
        compiler_params=_cparams(1, VMEM_MID),
    )


def _q_bwd(x, dxin, dq, ln, wdq, gql, wuq, gqn, gqr, cos, sin, name):
    s = x.shape[0]
    t = min(PROJ_ROWS, s)

    def body(x_ref, dxin_ref, dq_ref, ln_ref, wdq_ref, gql_ref, wuq_ref, gqn_ref, gqr_ref, cos_ref, sin_ref,
             dx_ref, hn_ref, cqn_ref, dqa_ref, dcq_ref, dln_ref, dgql_ref, dgqn_ref, dgqr_ref):
        x = x_ref[...]
        ln = ln_ref[...]
        y, rx = _rms(x, ln, D_MODEL)
        hn = y.astype(BF16)
        hn_ref[...] = hn
        cqp = _nn(hn, wdq_ref[...])
        gql = gql_ref[...]
        cy, rc = _rms(cqp, gql, Q_RANK)
        cqn = cy.astype(BF16)
        cqn_ref[...] = cqn
        cs, sn, perm = cos_ref[...], sin_ref[...], _swap_perm()
        gqn, gqr = gqn_ref[...], gqr_ref[...]
        dcq = jnp.zeros((t, Q_RANK), F32)
        dgqn = jnp.zeros((1, NOPE), F32)
        dgqr = jnp.zeros((1, LANES), F32)
        for h in range(N_HEADS):
            qa = _nn(cqn, wuq_ref[h])
            qn, qr = qa[:, :NOPE], qa[:, NOPE:]
            r = lax.rsqrt(jnp.sum(qa * qa, axis=-1, keepdims=True) * (1.0 / QK_DIM) + EPS)
            dqo = dq_ref[:, h * QK_PAD:h * QK_PAD + NOPE]
            dqr = dq_ref[:, h * QK_PAD + NOPE:(h + 1) * QK_PAD]
            dz = dqr * cs - _swap_halves(dqr, perm) * sn
            un = dqo * gqn
            ur = dz * gqr
            sm = (jnp.sum(qn * un, axis=-1, keepdims=True) + jnp.sum(qr * ur, axis=-1, keepdims=True)) * (1.0 / QK_DIM)
            coef = r * r * r * sm
            dqa = jnp.concatenate([r * un - qn * coef, r * ur - qr * coef], axis=1).astype(BF16)
            dgqn = dgqn + jnp.sum(dqo * (qn * r), axis=0, keepdims=True)
            dgqr = dgqr + jnp.sum(dz * (qr * r), axis=0, keepdims=True)
            dqa_ref[:, h * QK_PAD:(h + 1) * QK_PAD] = dqa
            dcq = dcq + _nt(dqa, wuq_ref[h])
        dcqp, dgql = _rms_bwd(cqp, rc, gql, dcq, Q_RANK)
        dcqb = dcqp.astype(BF16)
        dcq_ref[...] = dcqb
        dhn = _nt(dcqb, wdq_ref[...])
        dxn, dln = _rms_bwd(x, rx, ln, dhn, D_MODEL)
        dx_ref[...] = dxin_ref[...] + dxn

        @pl.when(pl.program_id(0) == 0)
        def _():
            dln_ref[...] = jnp.zeros_like(dln_ref)
            dgql_ref[...] = jnp.zeros_like(dgql_ref)
            dgqn_ref[...] = jnp.zeros_like(dgqn_ref)
            dgqr_ref[...] = jnp.zeros_like(dgqr_ref)

        dln_ref[...] += dln
        dgql_ref[...] += dgql
        dgqn_ref[...] += dgqn
        dgqr_ref[...] += dgqr

    def tok(cols, dt):
        return jax.ShapeDtypeStruct((s, cols), dt)

    def vec(cols):
        return jax.ShapeDtypeStruct((1, cols), F32)

    return pl.pallas_call(
        body, name=name, grid=(s // t,),
        out_shape=[tok(D_MODEL, F32), tok(D_MODEL, BF16), tok(Q_RANK, BF16), tok(N_HEADS * QK_PAD, BF16),
                   tok(Q_RANK, BF16), vec(D_MODEL), vec(Q_RANK), vec(NOPE), vec(LANES)],
        in_specs=[_rows(t, D_MODEL), _rows(t, D_MODEL), _rows(t, N_HEADS * QK_PAD)] + _q_specs(t),
        out_specs=[_rows(t, D_MODEL), _rows(t, D_MODEL), _rows(t, Q_RANK), _rows(t, N_HEADS * QK_PAD),
                   _rows(t, Q_RANK), _full((1, D_MODEL)), _full((1, Q_RANK)), _full((1, NOPE)), _full((1, LANES))],
        compiler_params=_cparams(1, VMEM_MID),
    )(x, dxin, dq, ln, wdq, gql, wuq, gqn, gqr, cos, sin)


SM_SCALE = 1.0 / math.sqrt(QK_DIM)
LOG2_E = math.log2(math.e)
EXP2_SCALE = SM_SCALE * LOG2_E
NEG = -1e30


def _diag_mask(t):
    qpos = lax.broadcasted_iota(jnp.int32, (t, t), 0)
    kpos = lax.broadcasted_iota(jnp.int32, (t, t), 1)
    return lax.shift_right_logical(kpos, 6) <= lax.shift_right_logical(qpos, 6)


def _att_fwd(q, k, v, name):
    s = q.shape[0]
    t = min(512, s)
    nb = s // t

    def body(q_ref, k_ref, v_ref, o_ref, lse_ref):
        qi = pl.program_id(1)
        qq = q_ref[...]

        def block(ki, carry, masked):
            m_old, l_old, acc = carry
            rows = pl.ds(pl.multiple_of(ki * t, t), t)
            sc = _nt(qq, k_ref[rows, :])
            if masked:
                sc = jnp.where(_diag_mask(t), sc, NEG)
            m_new = jnp.maximum(m_old, jnp.max(sc, axis=-1, keepdims=True))
            p = jnp.exp2((sc - m_new) * EXP2_SCALE)
            alpha = jnp.exp2((m_old - m_new) * EXP2_SCALE)
            l_new = alpha * l_old + jnp.sum(p, axis=-1, keepdims=True)
            acc = alpha * acc + _nn(p.astype(BF16), v_ref[rows, :])
            return m_new, l_new, acc

        init = (jnp.full((t, 1), NEG, F32), jnp.zeros((t, 1), F32), jnp.zeros((t, V_DIM), F32))
        carry = lax.fori_loop(0, qi // 2, lambda j, c: block(2 * j + 1, block(2 * j, c, False), False), init)
        carry = lax.cond(qi % 2 == 1, lambda c: block(qi - 1, c, False), lambda c: c, carry)
        m_fin, l_fin, acc = block(qi, carry, True)
        o_ref[...] = (acc / l_fin).astype(BF16)
        lse_ref[...] = jnp.broadcast_to(m_fin * SM_SCALE + jnp.log(l_fin), (t, LANES))

    return pl.pallas_call(
        body, name=name, grid=(N_HEADS, nb),
        out_shape=[jax.ShapeDtypeStruct((s, N_HEADS * V_DIM), BF16), jax.ShapeDtypeStruct((s, N_HEADS * LANES), F32)],
        in_specs=[pl.BlockSpec((t, QK_PAD), lambda h, qi: (qi, h)),
                  pl.BlockSpec((s, QK_PAD), lambda h, qi: (0, h)),
                  pl.BlockSpec((s, V_DIM), lambda h, qi: (0, h))],
        out_specs=[pl.BlockSpec((t, V_DIM), lambda h, qi: (qi, h)),
                   pl.BlockSpec((t, LANES), lambda h, qi: (qi, h))],
        compiler_params=_cparams(2, VMEM_MID),
    )(q, k, v)


def _att_bwd(q, k, v, do, o, lse, name, deps=()):
    s = q.shape[0]
    t = min(512, s)
    nb = s // t

    def body(q_ref, k_ref, v_ref, do_ref, o_ref, lse_ref, dq_ref, dk_ref, dv_ref):
        ki = pl.program_id(1)
        kk, vv = k_ref[...], v_ref[...]

        @pl.when(ki == 0)
        def _():
            dq_ref[...] = jnp.zeros_like(dq_ref)

        def block(qi, carry, masked):
            dk, dv = carry
            rows = pl.ds(pl.multiple_of(qi * t, t), t)
            qq, dob = q_ref[rows, :], do_ref[rows, :]
            sc = _nt(qq, kk)
            if masked:
                sc = jnp.where(_diag_mask(t), sc, NEG)
            p = jnp.exp2(sc * EXP2_SCALE - lse_ref[rows, :][:, :1] * LOG2_E)
            dp = _nt(dob, vv)
            dsum = jnp.sum(dob.astype(F32) * o_ref[rows, :].astype(F32), axis=-1, keepdims=True)
            ds = (p * (dp - dsum)).astype(BF16)
            dq_ref[rows, :] += _nn(ds, kk)
            return dk + _tn(ds, qq), dv + _tn(p.astype(BF16), dob)

        carry = block(ki, (jnp.zeros((t, QK_PAD), F32), jnp.zeros((t, V_DIM), F32)), True)
        rest = nb - 1 - ki
        carry = lax.fori_loop(
            0, rest // 2, lambda j, c: block(ki + 2 * j + 2, block(ki + 2 * j + 1, c, False), False), carry)
        dk, dv = lax.cond(rest % 2 == 1, lambda c: block(nb - 1, c, False), lambda c: c, carry)
        dk_ref[...] = dk * SM_SCALE
        dv_ref[...] = dv

        @pl.when(ki == nb - 1)
        def _():
            dq_ref[...] = dq_ref[...] * SM_SCALE

    def head(h, ki):
        return (0, h)

    def kblock(h, ki):
        return (ki, h)

    return _pcall(
        body, (q, k, v, do, o, lse), deps, name=name, grid=(N_HEADS, nb),
        out_shape=[jax.ShapeDtypeStruct((s, N_HEADS * QK_PAD), F32), jax.ShapeDtypeStruct((s, N_HEADS * QK_PAD), F32),
                   jax.ShapeDtypeStruct((s, N_HEADS * V_DIM), F32)],
        in_specs=[pl.BlockSpec((s, QK_PAD), head), pl.BlockSpec((t, QK_PAD), kblock), pl.BlockSpec((t, V_DIM), kblock),
                  pl.BlockSpec((s, V_DIM), head), pl.BlockSpec((s, V_DIM), head), pl.BlockSpec((s, LANES), head)],
        out_specs=[pl.BlockSpec((s, QK_PAD), head), pl.BlockSpec((t, QK_PAD), kblock), pl.BlockSpec((t, V_DIM), kblock)],
        compiler_params=_cparams(2, VMEM_MID),
    )


def _o_fwd(x, o, wo, name):
    s = x.shape[0]
    t = min(512, s)

    def body(x_ref, o_ref, wo_ref, xo_ref):
        xo_ref[...] = x_ref[...] + _nn(o_ref[...], wo_ref[...])

    return pl.pallas_call(
        body, name=name, grid=(s // t,),
        out_shape=jax.ShapeDtypeStruct((s, D_MODEL), F32),
        in_specs=[_rows(t, D_MODEL), _rows(t, D_MODEL), _full((D_MODEL, D_MODEL))],
        out_specs=_rows(t, D_MODEL),
        compiler_params=_cparams(1, VMEM_MID),
    )(x, o, wo)


def _o_bwd(dx, wo, name, deps=()):
    s = dx.shape[0]
    t = min(512, s)

    def body(dx_ref, wo_ref, do_ref, dxb_ref):
        dxb = dx_ref[...].astype(BF16)
        dxb_ref[...] = dxb
        do_ref[...] = _nt(dxb, wo_ref[...]).astype(BF16)

    tok = jax.ShapeDtypeStruct((s, D_MODEL), BF16)
    return _pcall(
        body, (dx, wo), deps, name=name, grid=(s // t,),
        out_shape=[tok, tok],
        in_specs=[_rows(t, D_MODEL), _full((D_MODEL, D_MODEL))],
        out_specs=[_rows(t, D_MODEL), _rows(t, D_MODEL)],
        compiler_params=_cparams(1, VMEM_MID),
    )


def _loss_head(y, target, name):
    s = y.shape[0]
    t = min(512, s)

    def body(y_ref, t_ref, dy_ref, sq_ref):
        e = y_ref[...] - t_ref[...]
        dy_ref[...] = e * (1.0 / D_MODEL)

        @pl.when(pl.program_id(0) == 0)
        def _():
            sq_ref[...] = jnp.zeros_like(sq_ref)

        sq_ref[...] += jnp.sum(e * e, axis=0, keepdims=True)

    return pl.pallas_call(
        body, name=name, grid=(s // t,),
        out_shape=[jax.ShapeDtypeStruct((s, D_MODEL), F32), jax.ShapeDtypeStruct((1, D_MODEL), F32)],
        in_specs=[_rows(t, D_MODEL), _rows(t, D_MODEL)],
        out_specs=[_rows(t, D_MODEL), _full((1, D_MODEL))],
        compiler_params=_cparams(1),
    )(y, target)


def _adamw(w, g, m, v, name):
    shape = w.shape
    c = shape[-1]
    r = math.prod(shape[:-1])
    tb = r
    for cand in (512, 256, 128):
        if r % cand == 0 and r > cand:
            tb = cand
            break

    def body(w_ref, g_ref, m_ref, v_ref, d_ref, mo_ref, vo_ref):
        gr = g_ref[...]
        mn = ADAM_B1 * m_ref[...] + (1.0 - ADAM_B1) * gr
        vn = ADAM_B2 * v_ref[...] + (1.0 - ADAM_B2) * (gr * gr)
        m_hat = mn / (1.0 - ADAM_B1 ** ADAM_STEP)
        v_hat = vn / (1.0 - ADAM_B2 ** ADAM_STEP)
        d_ref[...] = -ADAM_LR * (m_hat / (jnp.sqrt(v_hat) + ADAM_EPS) + ADAM_WD * w_ref[...])
        mo_ref[...] = mn
        vo_ref[...] = vn

    spec = pl.BlockSpec((tb, c), lambda i: (i, 0))
    flat = jax.ShapeDtypeStruct((r, c), F32)
    outs = pl.pallas_call(
        body, name=name, grid=(r // tb,),
        out_shape=[flat, flat, flat],
        in_specs=[spec] * 4, out_specs=[spec] * 3,
        compiler_params=_cparams(1),
    )(w.reshape(r, c), g.reshape(r, c), m.reshape(r, c), v.reshape(r, c))
    return [a.reshape(shape) for a in outs]


def _pad_cols(a, width):
    return jnp.pad(a, [(0, 0)] * (a.ndim - 1) + [(0, width - a.shape[-1])])


def _owner_view(a, sz):
    return a.reshape(a.shape[0], N_CHIPS, 2, sz, a.shape[-1])


def kernel(x, positions, ln_mix_a, w_pool, b_pool, pool_scale, ln_ffn, w_gate, w_up, w_down, ln_kv, w_dkv, g_kv_latent, w_uk, w_uv, g_k, ln_mix_b, w_dq, g_q_latent, w_uq, g_q, w_o, loss_target, m_ln_mix_a, m_w_pool, m_b_pool, m_pool_scale, m_ln_ffn, m_w_gate, m_w_up, m_w_down, m_ln_kv, m_w_dkv, m_g_kv_latent, m_w_uk, m_w_uv, m_g_k, m_ln_mix_b, m_w_dq, m_g_q_latent, m_w_uq, m_g_q, m_w_o, v_ln_mix_a, v_w_pool, v_b_pool, v_pool_scale, v_ln_ffn, v_w_gate, v_w_up, v_w_down, v_ln_kv, v_w_dkv, v_g_kv_latent, v_w_uk, v_w_uv, v_g_k, v_ln_mix_b, v_w_dq, v_g_q_latent, v_w_uq, v_g_q, v_w_o):
    weights = dict(ln_mix_a=ln_mix_a, w_pool=w_pool, b_pool=b_pool, pool_scale=pool_scale, ln_ffn=ln_ffn,
                   w_gate=w_gate, w_up=w_up, w_down=w_down, ln_kv=ln_kv, w_dkv=w_dkv, g_kv_latent=g_kv_latent,
                   w_uk=w_uk, w_uv=w_uv, g_k=g_k, ln_mix_b=ln_mix_b, w_dq=w_dq, g_q_latent=g_q_latent,
                   w_uq=w_uq, g_q=g_q, w_o=w_o)
    mom1 = dict(ln_mix_a=m_ln_mix_a, w_pool=m_w_pool, b_pool=m_b_pool, pool_scale=m_pool_scale, ln_ffn=m_ln_ffn,
                w_gate=m_w_gate, w_up=m_w_up, w_down=m_w_down, ln_kv=m_ln_kv, w_dkv=m_w_dkv,
                g_kv_latent=m_g_kv_latent, w_uk=m_w_uk, w_uv=m_w_uv, g_k=m_g_k, ln_mix_b=m_ln_mix_b, w_dq=m_w_dq,
                g_q_latent=m_g_q_latent, w_uq=m_w_uq, g_q=m_g_q, w_o=m_w_o)
    mom2 = dict(ln_mix_a=v_ln_mix_a, w_pool=v_w_pool, b_pool=v_b_pool, pool_scale=v_pool_scale, ln_ffn=v_ln_ffn,
                w_gate=v_w_gate, w_up=v_w_up, w_down=v_w_down, ln_kv=v_ln_kv, w_dkv=v_w_dkv,
                g_kv_latent=v_g_kv_latent, w_uk=v_w_uk, w_uv=v_w_uv, g_k=v_g_k, ln_mix_b=v_ln_mix_b, w_dq=v_w_dq,
                g_q_latent=v_g_q_latent, w_uq=v_w_uq, g_q=v_g_q, w_o=v_w_o)
    names = list(weights)
    dev = 4 * lax.axis_index("x") + 2 * lax.axis_index("y") + lax.axis_index("c")
    core = lax.axis_index("c").astype(jnp.int32).reshape(1)
    chip = (2 * lax.axis_index("x") + lax.axis_index("y")).astype(jnp.int32).reshape(1)

    xs = x[0]
    target = loss_target[0]
    cos, sin = _rope_tables(positions[0])

    def placed(shard):
        buf = lax.empty((shard.shape[0], N_DEV) + shard.shape[1:], shard.dtype)
        return lax.dynamic_update_slice(buf, shard[:, None], (0, dev, 0, 0))

    groups = {f"ffn{l}": [placed(jnp.stack([w_gate[l].T, w_up[l].T, w_down[l]]).astype(BF16))] for l in range(4)}
    groups["att"] = [placed(a.astype(BF16)) for a in (
        w_dkv[None, :, :KV_RANK], _pad_cols(w_dkv[None, :, KV_RANK:], LANES), w_uk[None], w_uv[None],
        w_dq, _pad_cols(w_uq, QK_PAD), w_o)]
    sp0 = _copies_start(groups["ffn0"], 1, _gather_spread, "spread_ffn0")

    small_sh = jnp.concatenate([ln_mix_a.reshape(1, -1), pool_scale.reshape(1, -1), b_pool.reshape(1, -1)], axis=1)
    wp_g, small_g = _all_gather([w_pool.astype(BF16), small_sh], [2, 0], "gather_first", deps=[sp0[3]])
    wp_all = wp_g.reshape(2, 4, GROUP_DIM, GROUP_DIM)
    small_g = small_g.reshape(N_DEV, 3, 2, LANES)
    ln_a_all = small_g[:, 0].transpose(1, 0, 2).reshape(2, 1, D_MODEL)
    sc_all = small_g[:, 1].transpose(1, 0, 2).reshape(2, 1, D_MODEL)
    bp_all = small_g[:, 2].reshape(N_DEV, 2, 4, 32).transpose(1, 2, 0, 3).reshape(2, 1, D_MODEL)

    def spread_start(nm, deps):
        return _copies_start(groups[nm], len(groups[nm]), _gather_spread, f"spread_{nm}", deps=deps)

    def spread_wait(nm, state, after):
        ssem, rsem, bufs, _ = state
        return _copies_wait(bufs, ssem, rsem, after, _blocks_moved(4), f"spread_done_{nm}")

    def relay_start(nm, bufs, deps=()):
        return _copies_start(bufs, len(bufs), _gather_relay, f"relay_{nm}", deps=deps)

    def relay_wait(nm, state, after):
        ssem, rsem, bufs, _ = state
        return _copies_wait(bufs, ssem, rsem, after, _blocks_moved(3), f"relay_done_{nm}")

    gkn = g_k[:NOPE].reshape(1, NOPE)
    gkr = _pad_cols(g_k[NOPE:].reshape(1, ROPE), LANES)
    gl = g_kv_latent.reshape(1, KV_RANK)
    lnkv = ln_kv.reshape(1, D_MODEL)

    x_in, x_mid, pooled, gates, ups, w_ffn = [], [], [], [], [], []
    qs, outs, lses = [], [], []

    def mixer(l, cur, deps):
        x_in.append(cur)
        mid, dsave = _mix_fwd(cur, ln_a_all[l], wp_all[l], bp_all[l], sc_all[l], f"mix_fwd{l}", deps=deps)
        pooled.append(dsave)
        x_mid.append(mid)
        return mid

    def q_args(j):
        return (ln_mix_b[j].reshape(1, -1), wdq_all[j], g_q_latent[j].reshape(1, -1), wuq_all[j],
                g_q[j, :NOPE].reshape(1, -1), _pad_cols(g_q[j, NOPE:].reshape(1, -1), LANES), cos, sin)

    def attention(j, cur, deps):
        x_in.append(cur)
        q = _q_fwd(cur, *q_args(j), f"q_fwd{j}", deps=deps)
        o, lse = _att_fwd(q, k_sh, v_sh, f"att_fwd{j}")
        mid = _o_fwd(cur, o, wo_all[j], f"o_fwd{j}")
        qs.append(q)
        outs.append(o)
        lses.append(lse)
        x_mid.append(mid)
        return mid

    def ffn(l, mid, relayed):
        w_l = relayed[0].reshape(3, D_FF, D_MODEL)
        w_ffn.append(w_l)
        cur, gate, up = _ffn_fwd(mid, ln_ffn[l].reshape(1, -1), w_l, f"ffn_fwd{l}")
        gates.append(gate)
        ups.append(up)
        return cur

    mid = mixer(0, xs, [])
    landed0 = spread_wait("ffn0", sp0, mid)
    sp1 = spread_start("ffn1", [landed0[0]])
    rl0 = relay_start("ffn0", landed0, [sp1[3]])
    cur = ffn(0, mid, relay_wait("ffn0", rl0, rl0[3]))

    landed1 = spread_wait("ffn1", sp1, cur)
    sp_att = spread_start("att", [landed1[0]])
    sp2 = spread_start("ffn2", [landed1[0]])
    rl1 = relay_start("ffn1", landed1, [sp_att[3], sp2[3]])
    mid = mixer(1, cur, [rl1[3]])
    cur = ffn(1, mid, relay_wait("ffn1", rl1, mid))
    x_kv = cur

    landed_att = spread_wait("att", sp_att, cur)
    landed2 = spread_wait("ffn2", sp2, cur)
    sp3 = spread_start("ffn3", [landed2[0]])
    rl_att = relay_start("att", landed_att, [sp3[3]])
    rl2 = relay_start("ffn2", landed2, [sp3[3]])
    att_bufs = relay_wait("att", rl_att, rl2[3])
    wc = att_bufs[0].reshape(D_MODEL, KV_RANK)
    wpe = att_bufs[1].reshape(D_MODEL, LANES)
    wuk_g = att_bufs[2].reshape(N_HEADS, KV_RANK, NOPE)
    wuv_g = att_bufs[3].reshape(N_HEADS, KV_RANK, V_DIM)
    wdq_all = att_bufs[4].reshape(2, D_MODEL, Q_RANK)
    wuq_all = att_bufs[5]
    wo_all = att_bufs[6].reshape(2, D_MODEL, D_MODEL)
    k_sh, v_sh = _kv_fwd(cur, lnkv, wc, wpe, gl, wuk_g, wuv_g, gkn, gkr, cos, sin, "kv_fwd")
    mid = attention(0, cur, [])
    cur = ffn(2, mid, relay_wait("ffn2", rl2, mid))

    landed3 = spread_wait("ffn3", sp3, cur)
    rl3 = relay_start("ffn3", landed3)
    mid = attention(1, cur, [rl3[3]])
    cur = ffn(3, mid, relay_wait("ffn3", rl3, mid))

    dx, sq_cols = _loss_head(cur, target, "loss_head")

    small = {}
    sizes = dict(ffn0=FF_SHARD, ffn1=FF_SHARD, ffn2=FF_SHARD, ffn3=FF_SHARD, wo=128, kv512=128, dkv_pe=128,
                 wdq=128, wuqT=QK_PAD, wpool=32)
    big = dict(wo=lax.empty((2, D_MODEL, D_MODEL), BF16), kv512=lax.empty((3, D_MODEL, KV_RANK), BF16),
               dkv_pe=lax.empty((1, D_MODEL, LANES), BF16), wdq=lax.empty((2, D_MODEL, Q_RANK), BF16),
               wuqT=lax.empty((2, N_HEADS * QK_PAD, Q_RANK), BF16), wpool=lax.empty((8, GROUP_DIM, GROUP_DIM), BF16))
    for l in range(4):
        big[f"ffn{l}"] = lax.empty((3, D_FF, D_MODEL), BF16)
    red = {}

    def pair_start(nms, tag):
        arrs = []
        for nm in nms:
            view = _owner_view(big[nm], sizes[nm])
            arrs += [view, lax.empty((view.shape[0], N_CHIPS) + view.shape[3:], BF16)]
        return nms, tag, _copies_start(arrs, len(nms), _pair_send, f"pair_start_{tag}")

    def chip_start(state, after):
        nms, tag, (ssem, rsem, arrs, _) = state
        arrs = _copies_wait(arrs, ssem, rsem, after, _landed, f"pair_done_{tag}")
        out = []
        for t, nm in enumerate(nms):
            part = _pair_sum(arrs[2 * t], arrs[2 * t + 1], core, f"pair_sum_{nm}")
            out += [part, lax.empty((3, part.shape[0]) + part.shape[2:], BF16)]
        return nms, tag, _copies_start(out, len(nms), _chip_send, f"chip_start_{tag}")

    def chip_finish(state, after):
        nms, tag, (ssem, rsem, arrs, _) = state
        arrs = _copies_wait(arrs, ssem, rsem, after, _landed, f"chip_done_{tag}")
        for t, nm in enumerate(nms):
            red[nm] = _chip_sum(arrs[2 * t], arrs[2 * t + 1], chip, f"chip_sum_{nm}")

    ffn_grads = {nm: lax.empty((4,) + weights[nm].shape[1:], F32) for nm in ("w_gate", "w_up", "w_down")}

    def place_ffn_grads(l):
        g = red[f"ffn{l}"]
        ffn_grads["w_gate"] = ffn_grads["w_gate"].at[l].set(g[0].T)
        ffn_grads["w_up"] = ffn_grads["w_up"].at[l].set(g[1].T)
        ffn_grads["w_down"] = ffn_grads["w_down"].at[l].set(g[2])

    dks, dvs = [], []
    pending = None
    bwd_deps = []
    for l in (3, 2, 1, 0):
        key = f"ffn{l}"
        dx, act, dgb, dub, hn, dyb, dln = _ffn_bwd(x_mid[l], dx, gates[l], ups[l], ln_ffn[l].reshape(1, -1),
                                                     w_ffn[l], f"ffn_bwd{l}", deps=bwd_deps)
        bwd_deps = []
        small[f"ln_ffn{l}"] = dln
        if l == 1:
            att_chip = chip_start(att_pair, dx)
            tn_deps = [att_chip[2][3]]
        else:
            tn_deps = []
        if pending:
            chip_finish(pending, dx)
            pending = None
            for done in (3, 2, 1):
                place_ffn_grads(done)
        big[key] = _tn_matmul(dgb, hn, big[key], 0, f"dw_gate{l}", m_chunk=FF_HALF, deps=tn_deps)
        big[key] = _tn_matmul(dub, hn, big[key], 1, f"dw_up{l}", m_chunk=FF_HALF)
        big[key] = _tn_matmul(act, dyb, big[key], 2, f"dw_down{l}", m_chunk=FF_HALF)
        if l == 1:
            chip_finish(att_chip, big[key])
        ffn_pair = pair_start([key], key)
        if l >= 2:
            j = l - 2
            do, dxb = _o_bwd(dx, wo_all[j], f"o_bwd{j}", deps=[ffn_pair[2][3]])
            big["wo"] = _tn_matmul(outs[j], dxb, big["wo"], j, f"dw_o{j}")
            ffn_chip = chip_start(ffn_pair, big["wo"])
            dq, dk, dv = _att_bwd(qs[j], k_sh, v_sh, do, outs[j], lses[j], f"att_bwd{j}", deps=[ffn_chip[2][3]])
            chip_finish(ffn_chip, dq)
            dks.append(dk)
            dvs.append(dv)
            dx, hnq, cqn, dqa, dcq, dln, dgql, dgqn, dgqr = _q_bwd(x_in[l], dx, dq, *q_args(j), f"q_bwd{j}")
            small[f"ln_mix_b{j}"] = dln
            small[f"g_q_latent{j}"] = dgql
            small[f"g_q{j}"] = jnp.concatenate([dgqn, dgqr[:, :ROPE]], axis=1)
            big["wdq"] = _tn_matmul(hnq, dcq, big["wdq"], j, f"dw_dq{j}")
            big["wuqT"] = _tn_matmul(dqa, cqn, big["wuqT"], j, f"dw_uq{j}")
            if l == 2:
                (dx, hnk, cn, dknb, dvb, dccb, dpeb, dlnkv, dgl, dgkn, dgkr) = _kv_bwd(
                    x_kv, dx, dks, dvs, lnkv, wc, wpe, gl, wuk_g, wuv_g, gkn, gkr, cos, sin, "kv_bwd")
                small["ln_kv"] = dlnkv
                small["g_kv_latent"] = dgl
                small["g_k"] = jnp.concatenate([dgkn, dgkr[:, :ROPE]], axis=1)
                big["kv512"] = _tn_matmul(dknb, cn, big["kv512"], 0, "dw_uk")
                big["kv512"] = _tn_matmul(dvb, cn, big["kv512"], 1, "dw_uv")
                big["kv512"] = _tn_matmul(hnk, dccb, big["kv512"], 2, "dw_dkv_c")
                big["dkv_pe"] = _tn_matmul(hnk, dpeb, big["dkv_pe"], 0, "dw_dkv_pe")
                att_pair = pair_start(["wo", "kv512", "dkv_pe", "wdq", "wuqT"], "att")
                bwd_deps = [att_pair[2][3]]
        else:
            dx, dyp, dsc, db, dln = _mix_bwd(x_in[l], dx, pooled[l], ln_a_all[l], wp_all[l], bp_all[l], sc_all[l],
                                             f"mix_bwd{l}", deps=[ffn_pair[2][3]])
            small[f"ln_mix_a{l}"] = dln
            small[f"pool_scale{l}"] = dsc
            small[f"b_pool{l}"] = db
            ffn_chip = chip_start(ffn_pair, dx)
            big["wpool"] = _tn_matmul(pooled[l], dyp, big["wpool"], 4 * l, f"dw_pool{l}", groups=4,
                                      deps=[ffn_chip[2][3]])
            if l == 1:
                pending = ffn_chip
            else:
                chip_finish(ffn_chip, [big["wpool"]] + list(ffn_grads.values()))
                place_ffn_grads(0)
    grad_x = dx[None]
    pool_pair = pair_start(["wpool"], "wpool")
    pool_chip = chip_start(pool_pair, pool_pair[2][3])
    chip_finish(pool_chip, pool_chip[2][3])

    vec_names = (["loss"] + [f"ln_ffn{l}" for l in range(4)] + ["ln_kv", "g_kv_latent", "g_k"]
                 + [f"{p}{j}" for p in ("ln_mix_b", "g_q_latent", "g_q") for j in range(2)]
                 + [f"{p}{l}" for p in ("ln_mix_a", "pool_scale", "b_pool") for l in range(2)])
    small["loss"] = sq_cols
    widths = [small[nm].shape[1] for nm in vec_names]
    padded = [-(-w // LANES) * LANES for w in widths]
    packed = jnp.concatenate([_pad_cols(small[nm], pw) for nm, pw in zip(vec_names, padded)], axis=1)
    (all_vecs,) = _all_gather([packed], [0], "gather_vectors")
    total = _sum_lead(all_vecs, "sum_vectors")
    vec = {}
    off = 0
    for nm, w, pw in zip(vec_names, widths, padded):
        vec[nm] = total[0, off:off + w]
        off += pw
    loss = 0.5 * jnp.sum(vec["loss"]) * (1.0 / D_MODEL)

    def own_cols(full, width):
        return lax.dynamic_slice_in_dim(full, dev * width, width, axis=full.ndim - 1)

    grads = dict(
        ln_mix_a=own_cols(jnp.stack([vec["ln_mix_a0"], vec["ln_mix_a1"]]), LANES),
        w_pool=red["wpool"].reshape(2, 4, 32, GROUP_DIM),
        b_pool=own_cols(jnp.stack([vec["b_pool0"], vec["b_pool1"]]).reshape(2, 4, GROUP_DIM), 32),
        pool_scale=own_cols(jnp.stack([vec["pool_scale0"], vec["pool_scale1"]]), LANES),
        ln_ffn=jnp.stack([vec[f"ln_ffn{l}"] for l in range(4)]),
        w_gate=ffn_grads["w_gate"],
        w_up=ffn_grads["w_up"],
        w_down=ffn_grads["w_down"],
        ln_kv=vec["ln_kv"],
        w_dkv=jnp.concatenate([red["kv512"][2], red["dkv_pe"][0][:, :ROPE]], axis=1),
        g_kv_latent=vec["g_kv_latent"],
        w_uk=red["kv512"][0].T,
        w_uv=red["kv512"][1].T,
        g_k=vec["g_k"],
        ln_mix_b=jnp.stack([vec["ln_mix_b0"], vec["ln_mix_b1"]]),
        w_dq=red["wdq"],
        g_q_latent=jnp.stack([vec["g_q_latent0"], vec["g_q_latent1"]]),
        w_uq=red["wuqT"].transpose(0, 2, 1)[:, :, :QK_DIM],
        g_q=jnp.stack([vec["g_q0"], vec["g_q1"]]),
        w_o=red["wo"],
    )

    deltas, new_m, new_v = {}, {}, {}
    for nm in names:
        w = weights[nm]
        shape = w.shape if w.ndim > 1 else (1, w.shape[0])
        d, mo, vo = _adamw(w.reshape(shape), grads[nm].reshape(shape), mom1[nm].reshape(shape),
                           mom2[nm].reshape(shape), f"adamw_{nm}")
        deltas[nm], new_m[nm], new_v[nm] = d.reshape(w.shape), mo.reshape(w.shape), vo.reshape(w.shape)

    return (loss, grad_x, *[grads[nm].reshape(weights[nm].shape) for nm in names], *[deltas[nm] for nm in names],
            *[new_m[nm] for nm in names], *[new_v[nm] for nm in names])
```

```python
import functools
import math

import jax
import jax.numpy as jnp
from jax import lax
from jax.experimental import pallas as pl
from jax.experimental.pallas import tpu as pltpu

F32 = jnp.float32
BF16 = jnp.bfloat16
MESH = pl.DeviceIdType.MESH

D_MODEL = 1024
D_FF = 2816
N_DEV = 8
N_CHIPS = 4
FF_SHARD = D_FF // N_DEV
FF_HALF = D_FF // 2
N_HEADS = 8
NOPE = 128
ROPE = 64
QK_DIM = NOPE + ROPE
QK_PAD = 256
V_DIM = 128
Q_RANK = 256
KV_RANK = 512
POOL_WINDOWS = (2, 4, 8, 16)
GROUP_DIM = 256
HALO = 128
CHUNK = 64
ROPE_THETA = 10000.0
EPS = 1e-6
LANES = 128

ADAM_LR = 0.001
ADAM_B1 = 0.9
ADAM_B2 = 0.999
ADAM_EPS = 1e-08
ADAM_WD = 0.01
ADAM_STEP = 10

PROJ_ROWS = 256
VMEM_BIG = 56 * 2**20
VMEM_MID = 40 * 2**20


def _nn(a, b):
    return lax.dot_general(a, b, (((1,), (0,)), ((), ())), preferred_element_type=F32)


def _nt(a, b):
    return lax.dot_general(a, b, (((1,), (1,)), ((), ())), preferred_element_type=F32)


def _tn(a, b):
    return lax.dot_general(a, b, (((0,), (0,)), ((), ())), preferred_element_type=F32)


def _rms(x, g, n):
    r = lax.rsqrt(jnp.sum(x * x, axis=-1, keepdims=True) * (1.0 / n) + EPS)
    return (x * r) * g, r


def _rms_bwd(x, r, g, dy, n):
    u = dy * g
    s = jnp.sum(x * u, axis=-1, keepdims=True) * (1.0 / n)
    dx = r * u - x * (r * r * r * s)
    dg = jnp.sum(dy * (x * r), axis=0, keepdims=True)
    return dx, dg


def _swap_perm():
    i = lax.broadcasted_iota(jnp.int32, (LANES, LANES), 0)
    j = lax.broadcasted_iota(jnp.int32, (LANES, LANES), 1)
    half = ROPE // 2
    hit = ((j < half) & (i == j + half)) | ((j >= half) & (j < ROPE) & (i == j - half))
    return jnp.where(hit, 1.0, 0.0).astype(BF16)


def _swap_halves(z, perm):
    hi = z.astype(BF16)
    lo = (z - hi.astype(F32)).astype(BF16)
    return _nn(hi, perm) + _nn(lo, perm)


def _sigmoid(x):
    return 1.0 / (1.0 + jnp.exp(-x))


def _cparams(n_grid, vmem=None):
    return pltpu.CompilerParams(dimension_semantics=("arbitrary",) * n_grid, vmem_limit_bytes=vmem)


def _rows(t, cols):
    return pl.BlockSpec((t, cols), lambda i: (i, 0))


def _full(shape):
    nd = len(shape)
    return pl.BlockSpec(shape, lambda *_: (0,) * nd)


ANY = pl.BlockSpec(memory_space=pl.ANY)


def _pcall(body, args, deps, *, in_specs, **kw):
    n_in, n_dep = len(args), len(deps)

    def ordered(*refs):
        body(*refs[:n_in], *refs[n_in + n_dep:])

    return pl.pallas_call(ordered, in_specs=list(in_specs) + [ANY] * n_dep, **kw)(*args, *deps)


def _place():
    x, y, c = lax.axis_index("x"), lax.axis_index("y"), lax.axis_index("c")
    return x, y, c


def _all_gather(shards, axes, name, deps=()):
    n, nd = len(shards), len(deps)
    out_shape = [jax.ShapeDtypeStruct(s.shape[:a] + (N_DEV,) + s.shape[a:], s.dtype) for s, a in zip(shards, axes)]

    def body(*refs):
        ins, outs = refs[:n], refs[n + nd:2 * n + nd]
        send_sems, recv_sems, local_sems = refs[2 * n + nd:]
        x, y, c = _place()
        me, sibling = (x, y, c), (x, y, 1 - c)
        chips = [(1 - x, y), (x, 1 - y), (1 - x, 1 - y)]

        def slot(t, dev):
            idx = 4 * dev[0] + 2 * dev[1] + dev[2]
            return outs[t].at[(slice(None),) * axes[t] + (idx,)]

        def copy(t, k, block, to, src=None):
            return pltpu.make_async_remote_copy(
                src_ref=slot(t, block) if src is None else src, dst_ref=slot(t, block),
                send_sem=send_sems.at[t, k], recv_sem=recv_sems.at[t, k],
                device_id=to, device_id_type=MESH)

        mine = [pltpu.make_async_copy(ins[t], slot(t, me), local_sems.at[t]) for t in range(n)]
        for cp in mine:
            cp.start()
        first = []
        for t in range(n):
            first.append(copy(t, 0, me, sibling, src=ins[t]))
            first += [copy(t, 1 + j, me, (*chip, c), src=ins[t]) for j, chip in enumerate(chips)]
        for cp in first:
            cp.start()
        passed = []
        for j, chip in enumerate(chips):
            for t in range(n):
                copy(t, 1 + j, (*chip, c), me).wait_recv()
                cp = copy(t, 4 + j, (*chip, c), sibling)
                cp.start()
                passed.append(cp)
        for t in range(n):
            copy(t, 0, sibling, me).wait_recv()
            for j, chip in enumerate(chips):
                copy(t, 4 + j, (*chip, 1 - c), me).wait_recv()
        for cp in first + passed:
            cp.wait_send()
        for cp in mine:
            cp.wait()

    return pl.pallas_call(
        body, name=name, out_shape=out_shape,
        in_specs=[ANY] * (n + nd), out_specs=[ANY] * n,
        scratch_shapes=[pltpu.SemaphoreType.DMA((n, 7)), pltpu.SemaphoreType.DMA((n, 7)),
                        pltpu.SemaphoreType.DMA((n,))],
    )(*shards, *deps)


HBM = pl.BlockSpec(memory_space=pltpu.HBM)
SEM = pl.BlockSpec(memory_space=pltpu.SEMAPHORE)
EFFECT = pltpu.SideEffectType.DATAFLOW_SIDE_EFFECTING


def _copies_start(arrays, n_sems, plan, name, deps=()):
    n, nd = len(arrays), len(deps)

    def body(*refs):
        for cp in plan(refs[:n], refs[n + nd], refs[n + nd + 1]):
            cp.start()
        refs[-1][...] = jnp.zeros_like(refs[-1])

    outs = pl.pallas_call(
        body, name=name,
        out_shape=(pltpu.SemaphoreType.DMA((n_sems,)), pltpu.SemaphoreType.DMA((n_sems,)),
                   *[pltpu.HBM(a.shape, a.dtype) for a in arrays], jax.ShapeDtypeStruct((8, LANES), F32)),
        in_specs=[HBM] * n + [ANY] * nd,
        out_specs=(SEM, SEM, *[HBM] * n, pl.BlockSpec(memory_space=pltpu.VMEM)),
        input_output_aliases={i: 2 + i for i in range(n)},
        compiler_params=pltpu.CompilerParams(has_side_effects=EFFECT),
    )(*[pltpu.with_memory_space_constraint(a, pltpu.HBM) for a in arrays], *deps)
    return outs[0], outs[1], list(outs[2:2 + n]), outs[-1]


def _copies_wait(arrays, send_sems, recv_sems, after, plan, name):
    n = len(arrays)
    after = list(after) if isinstance(after, (list, tuple)) else [after]

    def body(*refs):
        for cp in plan(refs[:n], refs[n], refs[n + 1]):
            cp.wait_send()
            cp.wait_recv()

    outs = pl.pallas_call(
        body, name=name,
        out_shape=tuple(pltpu.HBM(a.shape, a.dtype) for a in arrays),
        in_specs=[HBM] * n + [SEM, SEM] + [ANY] * len(after), out_specs=tuple([HBM] * n),
        input_output_aliases={i: i for i in range(n)},
        compiler_params=pltpu.CompilerParams(has_side_effects=EFFECT),
    )(*arrays, send_sems, recv_sems, *after)
    return list(outs)


def _remote(src, dst, send_sems, recv_sems, t, to):
    return pltpu.make_async_remote_copy(src_ref=src, dst_ref=dst, send_sem=send_sems.at[t], recv_sem=recv_sems.at[t],
                                        device_id=to, device_id_type=MESH)


def _dev_index(x, y, c):
    return 4 * x + 2 * y + c


def _gather_spread(bufs, send_sems, recv_sems):
    x, y, c = _place()
    mine = _dev_index(x, y, c)
    peers = [(x, y, 1 - c), (1 - x, y, c), (x, 1 - y, c), (1 - x, 1 - y, c)]
    return [_remote(g.at[k, mine], g.at[k, mine], send_sems, recv_sems, t, peer)
            for t, g in enumerate(bufs) for peer in peers for k in range(g.shape[0])]


def _gather_relay(bufs, send_sems, recv_sems):
    x, y, c = _place()
    blocks = [_dev_index(1 - x, y, c), _dev_index(x, 1 - y, c), _dev_index(1 - x, 1 - y, c)]
    return [_remote(g.at[k, b], g.at[k, b], send_sems, recv_sems, t, (x, y, 1 - c))
            for t, g in enumerate(bufs) for b in blocks for k in range(g.shape[0])]


def _blocks_moved(count):
    def plan(bufs, send_sems, recv_sems):
        x, y, c = _place()
        return [_remote(g.at[:, pl.ds(0, count)], g.at[:, pl.ds(0, count)], send_sems, recv_sems, t, (x, y, 1 - c))
                for t, g in enumerate(bufs)]
    return plan


def _pair_send(arrs, send_sems, recv_sems):
    x, y, c = _place()
    return [_remote(arrs[2 * t].at[p, k, 1 - c], arrs[2 * t + 1].at[p, k], send_sems, recv_sems, t, (x, y, 1 - c))
            for t in range(len(arrs) // 2) for p in range(arrs[2 * t].shape[0]) for k in range(N_CHIPS)]


def _chip_send(arrs, send_sems, recv_sems):
    x, y, c = _place()
    chips = [(1 - x, y), (x, 1 - y), (1 - x, 1 - y)]
    return [_remote(arrs[2 * t].at[p, 2 * px + py], arrs[2 * t + 1].at[j, p], send_sems, recv_sems, t, (px, py, c))
            for t in range(len(arrs) // 2) for j, (px, py) in enumerate(chips) for p in range(arrs[2 * t].shape[0])]


def _landed(arrs, send_sems, recv_sems):
    x, y, c = _place()
    return [_remote(arrs[2 * t + 1], arrs[2 * t + 1], send_sems, recv_sems, t, (x, y, 1 - c))
            for t in range(len(arrs) // 2)]


def _rows_per_step(rows, row_elems):
    best = 1
    for cand in range(1, rows + 1):
        if rows % cand == 0 and cand * row_elems <= 256 * 1024:
            best = cand
    return best


def _pair_sum(grad, landed, core, name):
    p, _, _, sz, c = grad.shape
    r = _rows_per_step(p * N_CHIPS, sz * c)

    def body(core_ref, g_ref, l_ref, o_ref):
        o_ref[...] = (g_ref[...].astype(F32) + l_ref[...].astype(F32)).astype(o_ref.dtype)

    out = pl.pallas_call(
        body, name=name,
        grid_spec=pltpu.PrefetchScalarGridSpec(
            num_scalar_prefetch=1, grid=(p * N_CHIPS // r,),
            in_specs=[pl.BlockSpec((r, None, sz, c), lambda i, cr: (i, cr[0], 0, 0)),
                      pl.BlockSpec((r, sz, c), lambda i, cr: (i, 0, 0))],
            out_specs=pl.BlockSpec((r, sz, c), lambda i, cr: (i, 0, 0))),
        out_shape=jax.ShapeDtypeStruct((p * N_CHIPS, sz, c), grad.dtype),
        compiler_params=_cparams(1),
    )(core, grad.reshape(p * N_CHIPS, 2, sz, c), landed.reshape(p * N_CHIPS, sz, c))
    return out.reshape(p, N_CHIPS, sz, c)


def _chip_sum(parts, landed, chip, name, deps=()):
    p, _, sz, c = parts.shape
    r = _rows_per_step(p, sz * c)

    def body(chip_ref, a_ref, l_ref, o_ref):
        acc = a_ref[...].astype(F32)
        for j in range(3):
            acc = acc + l_ref[j].astype(F32)
        o_ref[...] = acc

    nd = len(deps)

    def ordered(chip_ref, a_ref, l_ref, *rest):
        body(chip_ref, a_ref, l_ref, rest[nd])

    return pl.pallas_call(
        ordered, name=name,
        grid_spec=pltpu.PrefetchScalarGridSpec(
            num_scalar_prefetch=1, grid=(p // r,),
            in_specs=[pl.BlockSpec((r, None, sz, c), lambda i, cr: (i, cr[0], 0, 0)),
                      pl.BlockSpec((3, r, sz, c), lambda i, cr: (0, i, 0, 0))] + [ANY] * nd,
            out_specs=pl.BlockSpec((r, sz, c), lambda i, cr: (i, 0, 0))),
        out_shape=jax.ShapeDtypeStruct((p, sz, c), F32),
        compiler_params=_cparams(1),
    )(chip, parts, landed, *deps)


def _sum_lead(a, name, out_dtype=F32):
    k = a.shape[0]
    rest = a.shape[1:]
    r, c = rest[-2], rest[-1]
    lead = math.prod(rest[:-2])
    a3 = a.reshape(k, lead * r, c)
    rows = lead * r
    tb = rows
    for cand in (512, 256, 128, 64, 32, 16, 8):
        if rows % cand == 0 and rows > cand:
            tb = cand
            break

    def body(a_ref, o_ref):
        acc = a_ref[0].astype(F32)
        for i in range(1, k):
            acc = acc + a_ref[i].astype(F32)
        o_ref[...] = acc.astype(out_dtype)

    out = pl.pallas_call(
        body, name=name, grid=(rows // tb,),
        out_shape=jax.ShapeDtypeStruct((rows, c), out_dtype),
        in_specs=[pl.BlockSpec((k, tb, c), lambda i: (0, i, 0))],
        out_specs=pl.BlockSpec((tb, c), lambda i: (i, 0)),
        compiler_params=_cparams(1),
    )(a3)
    return out.reshape(rest)


def _bands(t, causal):
    r = lax.broadcasted_iota(jnp.int32, (t, t + HALO), 0)
    col = lax.broadcasted_iota(jnp.int32, (t, t + HALO), 1)
    diff = r + HALO - col if causal else col - r
    return jnp.stack([jnp.where((diff >= 0) & (diff < w), 1.0, 0.0) for w in POOL_WINDOWS]).astype(BF16)


def _split_dot(band, v):
    hi = v.astype(BF16)
    lo = (v - hi.astype(F32)).astype(BF16)
    return _nn(band, hi) + _nn(band, lo)


def _mix_fwd(x, g, wp, b, sc, name, deps=()):
    s = x.shape[0]
    t = min(256, s)
    rb = t // HALO

    def body(x_ref, xh_ref, g_ref, wp_ref, b_ref, sc_ref, band_ref, xo_ref, d_ref):
        i = pl.program_id(0)
        gg = g_ref[...]
        h, _ = _rms(x_ref[...], gg, D_MODEL)
        hh, _ = _rms(xh_ref[...], gg, D_MODEL)
        hh = jnp.where(i > 0, hh, 0.0)
        hext = jnp.concatenate([hh, h], axis=0)
        tok = i * t + lax.broadcasted_iota(jnp.int32, (t, 1), 0)
        for gi, w in enumerate(POOL_WINDOWS):
            sl = slice(gi * GROUP_DIM, (gi + 1) * GROUP_DIM)
            win = _split_dot(band_ref[gi], hext[:, sl])
            cnt = jnp.minimum(tok + 1, w).astype(F32)
            dbf = (win / cnt - h[:, sl]).astype(BF16)
            d_ref[:, sl] = dbf
            ypre = _nn(dbf, wp_ref[gi]) + b_ref[:, sl]
            xo_ref[:, sl] = x_ref[:, sl] + ypre * sc_ref[:, sl]

    return _pcall(
        body, (x, x, g, wp, b, sc, _bands(t, True)), deps, name=name, grid=(s // t,),
        out_shape=[jax.ShapeDtypeStruct((s, D_MODEL), F32), jax.ShapeDtypeStruct((s, D_MODEL), BF16)],
        in_specs=[_rows(t, D_MODEL),
                  pl.BlockSpec((HALO, D_MODEL), lambda i: (jnp.maximum(i * rb - 1, 0), 0)),
                  _full((1, D_MODEL)), _full((4, GROUP_DIM, GROUP_DIM)), _full((1, D_MODEL)), _full((1, D_MODEL)),
                  _full((4, t, t + HALO))],
        out_specs=[_rows(t, D_MODEL), _rows(t, D_MODEL)],
        compiler_params=_cparams(1, VMEM_MID),
    )


def _mix_bwd(x, dy, d, g, wp, b, sc, name, deps=()):
    s = x.shape[0]
    t = min(256, s)
    rb = t // HALO
    nb = s // t
    last_halo = s // HALO - 1

    def body(x_ref, dy_ref, dyn_ref, d_ref, g_ref, wp_ref, b_ref, sc_ref, band_ref,
             dx_ref, dyp_ref, dsc_ref, db_ref, dln_ref):
        i = pl.program_id(0)
        x = x_ref[...]
        gg = g_ref[...]
        dy = dy_ref[...]
        sc = sc_ref[...]
        dyp32 = dy * sc
        dyp = dyp32.astype(BF16)
        dyph = (dyn_ref[...] * sc).astype(BF16)
        dyp_ref[...] = dyp
        tok = i * t + lax.broadcasted_iota(jnp.int32, (t + HALO, 1), 0)
        dh, dsc = [], []
        for gi, w in enumerate(POOL_WINDOWS):
            sl = slice(gi * GROUP_DIM, (gi + 1) * GROUP_DIM)
            ypre = _nn(d_ref[:, sl], wp_ref[gi]) + b_ref[:, sl]
            dsc.append(jnp.sum(dy[:, sl] * ypre, axis=0, keepdims=True))
            dd = _nt(dyp[:, sl], wp_ref[gi])
            ddh = jnp.where(i < nb - 1, _nt(dyph[:, sl], wp_ref[gi]), 0.0)
            cnt = jnp.minimum(tok + 1, w).astype(F32)
            ddext = jnp.concatenate([dd, ddh], axis=0) / cnt
            dh.append(_split_dot(band_ref[gi], ddext) - dd)
        dh = jnp.concatenate(dh, axis=1)
        _, r = _rms(x, gg, D_MODEL)
        dxn, dg = _rms_bwd(x, r, gg, dh, D_MODEL)
        dx_ref[...] = dy + dxn

        @pl.when(i == 0)
        def _():
            dsc_ref[...] = jnp.zeros_like(dsc_ref)
            db_ref[...] = jnp.zeros_like(db_ref)
            dln_ref[...] = jnp.zeros_like(dln_ref)

        dsc_ref[...] += jnp.concatenate(dsc, axis=1)
        db_ref[...] += jnp.sum(dyp32, axis=0, keepdims=True)
        dln_ref[...] += dg

    vec = jax.ShapeDtypeStruct((1, D_MODEL), F32)
    return _pcall(
        body, (x, dy, dy, d, g, wp, b, sc, _bands(t, False)), deps, name=name, grid=(nb,),
        out_shape=[jax.ShapeDtypeStruct((s, D_MODEL), F32), jax.ShapeDtypeStruct((s, D_MODEL), BF16), vec, vec, vec],
        in_specs=[_rows(t, D_MODEL), _rows(t, D_MODEL),
                  pl.BlockSpec((HALO, D_MODEL), lambda i: (jnp.minimum((i + 1) * rb, last_halo), 0)),
                  _rows(t, D_MODEL),
                  _full((1, D_MODEL)), _full((4, GROUP_DIM, GROUP_DIM)), _full((1, D_MODEL)), _full((1, D_MODEL)),
                  _full((4, t, t + HALO))],
        out_specs=[_rows(t, D_MODEL), _rows(t, D_MODEL), _full((1, D_MODEL)), _full((1, D_MODEL)), _full((1, D_MODEL))],
        compiler_params=_cparams(1, VMEM_MID),
    )


def _load_weights(w_hbm, w_vmem, sem):
    @pl.when(pl.program_id(0) == 0)
    def _():
        cp = pltpu.make_async_copy(w_hbm, w_vmem, sem)
        cp.start()
        cp.wait()


def _ffn_fwd(x, g, w, name):
    s = x.shape[0]
    t = min(512, s)

    def body(x_ref, g_ref, w_hbm, xo_ref, gate_ref, up_ref, w_ref, sem):
        _load_weights(w_hbm, w_ref, sem)
        x = x_ref[...]
        hn = _rms(x, g_ref[...], D_MODEL)[0].astype(BF16)
        acc = x
        for c in range(2):
            rs = slice(c * FF_HALF, (c + 1) * FF_HALF)
            gt = _nt(hn, w_ref[0, rs, :])
            up = _nt(hn, w_ref[1, rs, :])
            gate_ref[:, rs] = gt.astype(BF16)
            up_ref[:, rs] = up.astype(BF16)
            act = ((gt * _sigmoid(gt)) * up).astype(BF16)
            acc = acc + _nn(act, w_ref[2, rs, :])
        xo_ref[...] = acc

    hid = jax.ShapeDtypeStruct((s, D_FF), BF16)
    return pl.pallas_call(
        body, name=name, grid=(s // t,),
        out_shape=[jax.ShapeDtypeStruct((s, D_MODEL), F32), hid, hid],
        in_specs=[_rows(t, D_MODEL), _full((1, D_MODEL)), ANY],
        out_specs=[_rows(t, D_MODEL), _rows(t, D_FF), _rows(t, D_FF)],
        scratch_shapes=[pltpu.VMEM((3, D_FF, D_MODEL), BF16), pltpu.SemaphoreType.DMA],
        compiler_params=_cparams(1, VMEM_BIG),
    )(x, g, w)


def _ffn_bwd(x, dy, gate, up, g, w, name, deps=()):
    s = x.shape[0]
    t = min(256, s)

    def body(x_ref, dy_ref, gate_ref, up_ref, g_ref, w_hbm,
             dx_ref, act_ref, dg_ref, du_ref, hn_ref, dyb_ref, dln_ref, w_ref, sem):
        _load_weights(w_hbm, w_ref, sem)
        x = x_ref[...]
        gg = g_ref[...]
        y, r = _rms(x, gg, D_MODEL)
        hn = y.astype(BF16)
        hn_ref[...] = hn
        dy = dy_ref[...]
        dyb = dy.astype(BF16)
        dyb_ref[...] = dyb
        dh = jnp.zeros((t, D_MODEL), F32)
        for c in range(2):
            rs = slice(c * FF_HALF, (c + 1) * FF_HALF)
            gt = gate_ref[:, rs].astype(F32)
            u = up_ref[:, rs].astype(F32)
            sg = _sigmoid(gt)
            sl = gt * sg
            act_ref[:, rs] = (sl * u).astype(BF16)
            dact = _nt(dyb, w_ref[2, rs, :])
            dg = (dact * u * (sg * (1.0 + gt * (1.0 - sg)))).astype(BF16)
            du = (dact * sl).astype(BF16)
            dg_ref[:, rs] = dg
            du_ref[:, rs] = du
            dh = dh + _nn(dg, w_ref[0, rs, :]) + _nn(du, w_ref[1, rs, :])
        dxn, dgl = _rms_bwd(x, r, gg, dh, D_MODEL)
        dx_ref[...] = dy + dxn

        @pl.when(pl.program_id(0) == 0)
        def _():
            dln_ref[...] = jnp.zeros_like(dln_ref)

        dln_ref[...] += dgl

    hid = jax.ShapeDtypeStruct((s, D_FF), BF16)
    tok = jax.ShapeDtypeStruct((s, D_MODEL), BF16)
    return _pcall(
        body, (x, dy, gate, up, g, w), deps, name=name, grid=(s // t,),
        out_shape=[jax.ShapeDtypeStruct((s, D_MODEL), F32), hid, hid, hid, tok, tok,
                   jax.ShapeDtypeStruct((1, D_MODEL), F32)],
        in_specs=[_rows(t, D_MODEL), _rows(t, D_MODEL), _rows(t, D_FF), _rows(t, D_FF), _full((1, D_MODEL)), ANY],
        out_specs=[_rows(t, D_MODEL), _rows(t, D_FF), _rows(t, D_FF), _rows(t, D_FF),
                   _rows(t, D_MODEL), _rows(t, D_MODEL), _full((1, D_MODEL))],
        scratch_shapes=[pltpu.VMEM((3, D_FF, D_MODEL), BF16), pltpu.SemaphoreType.DMA],
        compiler_params=_cparams(1, VMEM_BIG),
    )


def _tn_matmul(a, b, into, p0, name, groups=1, m_chunk=None, deps=()):
    s = a.shape[0]
    m, n = a.shape[1] // groups, b.shape[1] // groups
    assert into.shape[1:] == (m, n)
    mc = m if m_chunk is None else m_chunk
    nm = m // mc
    t = min(1024, s)
    nt = s // t

    def body(a_ref, b_ref, into_ref, o_ref, acc):
        ti = pl.program_id(2)

        @pl.when(ti == 0)
        def _():
            acc[...] = jnp.zeros_like(acc)

        acc[...] += _tn(a_ref[...], b_ref[...])

        @pl.when(ti == nt - 1)
        def _():
            o_ref[...] = acc[...].astype(o_ref.dtype)

    return _pcall(
        body, (a, b, into), deps, name=name, grid=(groups, nm, nt),
        out_shape=jax.ShapeDtypeStruct(into.shape, into.dtype),
        in_specs=[pl.BlockSpec((t, mc), lambda gi, mi, ti: (ti, gi * nm + mi)),
                  pl.BlockSpec((t, n), lambda gi, mi, ti: (ti, gi)), ANY],
        out_specs=pl.BlockSpec((None, mc, n), lambda gi, mi, ti: (p0 + gi, mi, 0)),
        scratch_shapes=[pltpu.VMEM((mc, n), F32)],
        input_output_aliases={2: 0},
        compiler_params=_cparams(3, VMEM_BIG),
    )


def _rope_tables(positions):
    half = ROPE // 2
    inv = ROPE_THETA ** (-jnp.arange(half, dtype=F32) * 2.0 / ROPE)
    ang = positions.astype(F32)[:, None] * inv
    cos, sin = jnp.cos(ang), jnp.sin(ang)
    zero = jnp.zeros((positions.shape[0], LANES - ROPE), F32)
    return jnp.concatenate([cos, cos, zero], axis=1), jnp.concatenate([-sin, sin, zero], axis=1)


def _kv_specs(t):
    return [_full((1, D_MODEL)), _full((D_MODEL, KV_RANK)), _full((D_MODEL, LANES)), _full((1, KV_RANK)),
            _full((N_HEADS, KV_RANK, NOPE)), _full((N_HEADS, KV_RANK, V_DIM)),
            _full((1, NOPE)), _full((1, LANES)), _rows(t, LANES), _rows(t, LANES)]


def _kv_fwd(x, ln, wc, wpe, gl, wuk, wuv, gkn, gkr, cos, sin, name, deps=()):
    s = x.shape[0]
    t = min(PROJ_ROWS, s)

    def body(x_ref, ln_ref, wc_ref, wpe_ref, gl_ref, wuk_ref, wuv_ref, gkn_ref, gkr_ref, cos_ref, sin_ref,
             k_ref, v_ref):
        hn = _rms(x_ref[...], ln_ref[...], D_MODEL)[0].astype(BF16)
        clat = _nn(hn, wc_ref[...])
        kpe = _nn(hn, wpe_ref[...])
        cn = _rms(clat, gl_ref[...], KV_RANK)[0].astype(BF16)
        sspe = jnp.sum(kpe * kpe, axis=-1, keepdims=True)
        base = kpe * gkr_ref[...]
        rot = base * cos_ref[...] + _swap_halves(base, _swap_perm()) * sin_ref[...]
        for h in range(N_HEADS):
            kn = _nn(cn, wuk_ref[h])
            r = lax.rsqrt((jnp.sum(kn * kn, axis=-1, keepdims=True) + sspe) * (1.0 / QK_DIM) + EPS)
            k_ref[:, h * QK_PAD:h * QK_PAD + NOPE] = ((kn * r) * gkn_ref[...]).astype(BF16)
            k_ref[:, h * QK_PAD + NOPE:(h + 1) * QK_PAD] = (rot * r).astype(BF16)
            v_ref[:, h * V_DIM:(h + 1) * V_DIM] = _nn(cn, wuv_ref[h]).astype(BF16)

    return _pcall(
        body, (x, ln, wc, wpe, gl, wuk, wuv, gkn, gkr, cos, sin), deps, name=name, grid=(s // t,),
        out_shape=[jax.ShapeDtypeStruct((s, N_HEADS * QK_PAD), BF16), jax.ShapeDtypeStruct((s, N_HEADS * V_DIM), BF16)],
        in_specs=[_rows(t, D_MODEL)] + _kv_specs(t),
        out_specs=[_rows(t, N_HEADS * QK_PAD), _rows(t, N_HEADS * V_DIM)],
        compiler_params=_cparams(1, VMEM_MID),
    )


def _kv_bwd(x, dxin, dks, dvs, ln, wc, wpe, gl, wuk, wuv, gkn, gkr, cos, sin, name):
    s = x.shape[0]
    t = min(PROJ_ROWS, s)
    nk = len(dks)

    def body(*refs):
        x_ref, dxin_ref = refs[:2]
        dk_refs = refs[2:2 + nk]
        dv_refs = refs[2 + nk:2 + 2 * nk]
        (ln_ref, wc_ref, wpe_ref, gl_ref, wuk_ref, wuv_ref, gkn_ref, gkr_ref, cos_ref, sin_ref,
         dx_ref, hn_ref, cn_ref, dkn_ref, dvb_ref, dcc_ref, dpe_ref,
         dln_ref, dgl_ref, dgkn_ref, dgkr_ref) = refs[2 + 2 * nk:]
        x = x_ref[...]
        ln = ln_ref[...]
        y, rx = _rms(x, ln, D_MODEL)
        hn = y.astype(BF16)
        hn_ref[...] = hn
        clat = _nn(hn, wc_ref[...])
        kpe = _nn(hn, wpe_ref[...])
        gl = gl_ref[...]
        cy, rc = _rms(clat, gl, KV_RANK)
        cn = cy.astype(BF16)
        cn_ref[...] = cn
        sspe = jnp.sum(kpe * kpe, axis=-1, keepdims=True)
        cs, sn, perm = cos_ref[...], sin_ref[...], _swap_perm()
        gkn, gkr = gkn_ref[...], gkr_ref[...]
        base = kpe * gkr
        rot = base * cs + _swap_halves(base, perm) * sn
        dc = jnp.zeros((t, KV_RANK), F32)
        dkr_sum = jnp.zeros((t, LANES), F32)
        coef_sum = jnp.zeros((t, 1), F32)
        dgkn = jnp.zeros((1, NOPE), F32)
        for h in range(N_HEADS):
            kn = _nn(cn, wuk_ref[h])
            r = lax.rsqrt((jnp.sum(kn * kn, axis=-1, keepdims=True) + sspe) * (1.0 / QK_DIM) + EPS)
            lo, mid, hi = h * QK_PAD, h * QK_PAD + NOPE, (h + 1) * QK_PAD
            dko = dk_refs[0][:, lo:mid]
            dkr = dk_refs[0][:, mid:hi]
            dvh = dv_refs[0][:, h * V_DIM:(h + 1) * V_DIM]
            for j in range(1, nk):
                dko = dko + dk_refs[j][:, lo:mid]
                dkr = dkr + dk_refs[j][:, mid:hi]
                dvh = dvh + dv_refs[j][:, h * V_DIM:(h + 1) * V_DIM]
            un = dko * gkn
            sm = (jnp.sum(kn * un, axis=-1, keepdims=True) + jnp.sum(rot * dkr, axis=-1, keepdims=True)) * (1.0 / QK_DIM)
            coef = r * r * r * sm
            dkn = (r * un - kn * coef).astype(BF16)
            dkr_sum = dkr_sum + r * dkr
            coef_sum = coef_sum + coef
            dgkn = dgkn + jnp.sum(dko * (kn * r), axis=0, keepdims=True)
            dkn_ref[:, h * NOPE:(h + 1) * NOPE] = dkn
            dvb = dvh.astype(BF16)
            dvb_ref[:, h * V_DIM:(h + 1) * V_DIM] = dvb
            dc = dc + _nt(dkn, wuk_ref[h]) + _nt(dvb, wuv_ref[h])
        dz = dkr_sum * cs - _swap_halves(dkr_sum, perm) * sn
        dkpe = dz * gkr - kpe * coef_sum
        dgkr = jnp.sum(dz * kpe, axis=0, keepdims=True)
        dclat, dgl = _rms_bwd(clat, rc, gl, dc, KV_RANK)
        dcc = dclat.astype(BF16)
        dpe = dkpe.astype(BF16)
        dcc_ref[...] = dcc
        dpe_ref[...] = dpe
        dhn = _nt(dcc, wc_ref[...]) + _nt(dpe, wpe_ref[...])
        dxn, dln = _rms_bwd(x, rx, ln, dhn, D_MODEL)
        dx_ref[...] = dxin_ref[...] + dxn

        @pl.when(pl.program_id(0) == 0)
        def _():
            dln_ref[...] = jnp.zeros_like(dln_ref)
            dgl_ref[...] = jnp.zeros_like(dgl_ref)
            dgkn_ref[...] = jnp.zeros_like(dgkn_ref)
            dgkr_ref[...] = jnp.zeros_like(dgkr_ref)

        dln_ref[...] += dln
        dgl_ref[...] += dgl
        dgkn_ref[...] += dgkn
        dgkr_ref[...] += dgkr

    def tok(cols, dt):
        return jax.ShapeDtypeStruct((s, cols), dt)

    def vec(cols):
        return jax.ShapeDtypeStruct((1, cols), F32)

    return pl.pallas_call(
        body, name=name, grid=(s // t,),
        out_shape=[tok(D_MODEL, F32), tok(D_MODEL, BF16), tok(KV_RANK, BF16), tok(N_HEADS * NOPE, BF16),
                   tok(N_HEADS * V_DIM, BF16), tok(KV_RANK, BF16), tok(LANES, BF16),
                   vec(D_MODEL), vec(KV_RANK), vec(NOPE), vec(LANES)],
        in_specs=[_rows(t, D_MODEL), _rows(t, D_MODEL)] + [_rows(t, N_HEADS * QK_PAD)] * nk
                 + [_rows(t, N_HEADS * V_DIM)] * nk + _kv_specs(t),
        out_specs=[_rows(t, D_MODEL), _rows(t, D_MODEL), _rows(t, KV_RANK), _rows(t, N_HEADS * NOPE),
                   _rows(t, N_HEADS * V_DIM), _rows(t, KV_RANK), _rows(t, LANES),
                   _full((1, D_MODEL)), _full((1, KV_RANK)), _full((1, NOPE)), _full((1, LANES))],
        compiler_params=_cparams(1, VMEM_BIG),
    )(x, dxin, *dks, *dvs, ln, wc, wpe, gl, wuk, wuv, gkn, gkr, cos, sin)


def _q_specs(t):
    return [_full((1, D_MODEL)), _full((D_MODEL, Q_RANK)), _full((1, Q_RANK)), _full((N_HEADS, Q_RANK, QK_PAD)),
            _full((1, NOPE)), _full((1, LANES)), _rows(t, LANES), _rows(t, LANES)]


def _q_fwd(x, ln, wdq, gql, wuq, gqn, gqr, cos, sin, name, deps=()):
    s = x.shape[0]
    t = min(PROJ_ROWS, s)

    def body(x_ref, ln_ref, wdq_ref, gql_ref, wuq_ref, gqn_ref, gqr_ref, cos_ref, sin_ref, q_ref):
        hn = _rms(x_ref[...], ln_ref[...], D_MODEL)[0].astype(BF16)
        cqn = _rms(_nn(hn, wdq_ref[...]), gql_ref[...], Q_RANK)[0].astype(BF16)
        cs, sn, perm = cos_ref[...], sin_ref[...], _swap_perm()
        for h in range(N_HEADS):
            qa = _nn(cqn, wuq_ref[h])
            r = lax.rsqrt(jnp.sum(qa * qa, axis=-1, keepdims=True) * (1.0 / QK_DIM) + EPS)
            q_ref[:, h * QK_PAD:h * QK_PAD + NOPE] = ((qa[:, :NOPE] * r) * gqn_ref[...]).astype(BF16)
            z = (qa[:, NOPE:] * r) * gqr_ref[...]
            q_ref[:, h * QK_PAD + NOPE:(h + 1) * QK_PAD] = (z * cs + _swap_halves(z, perm) * sn).astype(BF16)

    return _pcall(
        body, (x, ln, wdq, gql, wuq, gqn, gqr, cos, sin), deps, name=name, grid=(s // t,),
        out_shape=jax.ShapeDtypeStruct((s, N_HEADS * QK_PAD), BF16),
        in_specs=[_rows(t, D_MODEL)] + _q_specs(t),
        out_specs=_rows(t, N_HEADS * QK_PAD),
        compiler_params=_cparams(1, VMEM_MID),
    )


def _q_bwd(x, dxin, dq, ln, wdq, gql, wuq, gqn, gqr, cos, sin, name):
    s = x.shape[0]
    t = min(PROJ_ROWS, s)

    def body(x_ref, dxin_ref, dq_ref, ln_ref, wdq_ref, gql_ref, wuq_ref, gqn_ref, gqr_ref, cos_ref, sin_ref,
             dx_ref, hn_ref, cqn_ref, dqa_ref, dcq_ref, dln_ref, dgql_ref, dgqn_ref, dgqr_ref):
        x = x_ref[...]
        ln = ln_ref[...]
        y, rx = _rms(x, ln, D_MODEL)
        hn = y.astype(BF16)
        hn_ref[...] = hn
        cqp = _nn(hn, wdq_ref[...])
        gql = gql_ref[...]
        cy, rc = _rms(cqp, gql, Q_RANK)
        cqn = cy.astype(BF16)
        cqn_ref[...] = cqn
        cs, sn, perm = cos_ref[...], sin_ref[...], _swap_perm()
        gqn, gqr = gqn_ref[...], gqr_ref[...]
        dcq = jnp.zeros((t, Q_RANK), F32)
        dgqn = jnp.zeros((1, NOPE), F32)
        dgqr = jnp.zeros((1, LANES), F32)
        for h in range(N_HEADS):
            qa = _nn(cqn, wuq_ref[h])
            qn, qr = qa[:, :NOPE], qa[:, NOPE:]
            r = lax.rsqrt(jnp.sum(qa * qa, axis=-1, keepdims=True) * (1.0 / QK_DIM) + EPS)
            dqo = dq_ref[:, h * QK_PAD:h * QK_PAD + NOPE]
            dqr = dq_ref[:, h * QK_PAD + NOPE:(h + 1) * QK_PAD]
            dz = dqr * cs - _swap_halves(dqr, perm) * sn
            un = dqo * gqn
            ur = dz * gqr
            sm = (jnp.sum(qn * un, axis=-1, keepdims=True) + jnp.sum(qr * ur, axis=-1, keepdims=True)) * (1.0 / QK_DIM)
            coef = r * r * r * sm
            dqa = jnp.concatenate([r * un - qn * coef, r * ur - qr * coef], axis=1).astype(BF16)
            dgqn = dgqn + jnp.sum(dqo * (qn * r), axis=0, keepdims=True)
            dgqr = dgqr + jnp.sum(dz * (qr * r), axis=0, keepdims=True)
            dqa_ref[:, h * QK_PAD:(h + 1) * QK_PAD] = dqa
            dcq = dcq + _nt(dqa, wuq_ref[h])
        dcqp, dgql = _rms_bwd(cqp, rc, gql, dcq, Q_RANK)
        dcqb = dcqp.astype(BF16)
        dcq_ref[...] = dcqb
        dhn = _nt(dcqb, wdq_ref[...])
        dxn, dln = _rms_bwd(x, rx, ln, dhn, D_MODEL)
        dx_ref[...] = dxin_ref[...] + dxn

        @pl.when(pl.program_id(0) == 0)
        def _():
            dln_ref[...] = jnp.zeros_like(dln_ref)
            dgql_ref[...] = jnp.zeros_like(dgql_ref)
            dgqn_ref[...] = jnp.zeros_like(dgqn_ref)
            dgqr_ref[...] = jnp.zeros_like(dgqr_ref)

        dln_ref[...] += dln
        dgql_ref[...] += dgql
        dgqn_ref[...] += dgqn
        dgqr_ref[...] += dgqr

    def tok(cols, dt):
        return jax.ShapeDtypeStruct((s, cols), dt)

    def vec(cols):
        return jax.ShapeDtypeStruct((1, cols), F32)

    return pl.pallas_call(
        body, name=name, grid=(s // t,),
        out_shape=[tok(D_MODEL, F32), tok(D_MODEL, BF16), tok(Q_RANK, BF16), tok(N_HEADS * QK_PAD, BF16),
                   tok(Q_RANK, BF16), vec(D_MODEL), vec(Q_RANK), vec(NOPE), vec(LANES)],
        in_specs=[_rows(t, D_MODEL), _rows(t, D_MODEL), _rows(t, N_HEADS * QK_PAD)] + _q_specs(t),
        out_specs=[_rows(t, D_MODEL), _rows(t, D_MODEL), _rows(t, Q_RANK), _rows(t, N_HEADS * QK_PAD),
                   _rows(t, Q_RANK), _full((1, D_MODEL)), _full((1, Q_RANK)), _full((1, NOPE)), _full((1, LANES))],
        compiler_params=_cparams(1, VMEM_MID),
    )(x, dxin, dq, ln, wdq, gql, wuq, gqn, gqr, cos, sin)


SM_SCALE = 1.0 / math.sqrt(QK_DIM)
LOG2_E = math.log2(math.e)
EXP2_SCALE = SM_SCALE * LOG2_E
NEG = -1e30


def _diag_mask(t):
    qpos = lax.broadcasted_iota(jnp.int32, (t, t), 0)
    kpos = lax.broadcasted_iota(jnp.int32, (t, t), 1)
    return lax.shift_right_logical(kpos, 6) <= lax.shift_right_logical(qpos, 6)


def _att_fwd(q, k, v, name):
    s = q.shape[0]
    t = min(512, s)
    nb = s // t

    def body(q_ref, k_ref, v_ref, o_ref, lse_ref):
        qi = pl.program_id(1)
        qq = q_ref[...]

        def block(ki, carry, masked):
            m_old, l_old, acc = carry
            rows = pl.ds(pl.multiple_of(ki * t, t), t)
            sc = _nt(qq, k_ref[rows, :])
            if masked:
                sc = jnp.where(_diag_mask(t), sc, NEG)
            m_new = jnp.maximum(m_old, jnp.max(sc, axis=-1, keepdims=True))
            p = jnp.exp2((sc - m_new) * EXP2_SCALE)
            alpha = jnp.exp2((m_old - m_new) * EXP2_SCALE)
            l_new = alpha * l_old + jnp.sum(p, axis=-1, keepdims=True)
            acc = alpha * acc + _nn(p.astype(BF16), v_ref[rows, :])
            return m_new, l_new, acc

        init = (jnp.full((t, 1), NEG, F32), jnp.zeros((t, 1), F32), jnp.zeros((t, V_DIM), F32))
        carry = lax.fori_loop(0, qi // 2, lambda j, c: block(2 * j + 1, block(2 * j, c, False), False), init)
        carry = lax.cond(qi % 2 == 1, lambda c: block(qi - 1, c, False), lambda c: c, carry)
        m_fin, l_fin, acc = block(qi, carry, True)
        o_ref[...] = (acc / l_fin).astype(BF16)
        lse_ref[...] = jnp.broadcast_to(m_fin * SM_SCALE + jnp.log(l_fin), (t, LANES))

    return pl.pallas_call(
        body, name=name, grid=(N_HEADS, nb),
        out_shape=[jax.ShapeDtypeStruct((s, N_HEADS * V_DIM), BF16), jax.ShapeDtypeStruct((s, N_HEADS * LANES), F32)],
        in_specs=[pl.BlockSpec((t, QK_PAD), lambda h, qi: (qi, h)),
                  pl.BlockSpec((s, QK_PAD), lambda h, qi: (0, h)),
                  pl.BlockSpec((s, V_DIM), lambda h, qi: (0, h))],
        out_specs=[pl.BlockSpec((t, V_DIM), lambda h, qi: (qi, h)),
                   pl.BlockSpec((t, LANES), lambda h, qi: (qi, h))],
        compiler_params=_cparams(2, VMEM_MID),
    )(q, k, v)


def _att_bwd(q, k, v, do, o, lse, name, deps=()):
    s = q.shape[0]
    t = min(512, s)
    nb = s // t

    def body(q_ref, k_ref, v_ref, do_ref, o_ref, lse_ref, dq_ref, dk_ref, dv_ref):
        ki = pl.program_id(1)
        kk, vv = k_ref[...], v_ref[...]

        @pl.when(ki == 0)
        def _():
            dq_ref[...] = jnp.zeros_like(dq_ref)

        def block(qi, carry, masked):
            dk, dv = carry
            rows = pl.ds(pl.multiple_of(qi * t, t), t)
            qq, dob = q_ref[rows, :], do_ref[rows, :]
            sc = _nt(qq, kk)
            if masked:
                sc = jnp.where(_diag_mask(t), sc, NEG)
            p = jnp.exp2(sc * EXP2_SCALE - lse_ref[rows, :][:, :1] * LOG2_E)
            dp = _nt(dob, vv)
            dsum = jnp.sum(dob.astype(F32) * o_ref[rows, :].astype(F32), axis=-1, keepdims=True)
            ds = (p * (dp - dsum)).astype(BF16)
            dq_ref[rows, :] += _nn(ds, kk)
            return dk + _tn(ds, qq), dv + _tn(p.astype(BF16), dob)

        carry = block(ki, (jnp.zeros((t, QK_PAD), F32), jnp.zeros((t, V_DIM), F32)), True)
        rest = nb - 1 - ki
        carry = lax.fori_loop(
            0, rest // 2, lambda j, c: block(ki + 2 * j + 2, block(ki + 2 * j + 1, c, False), False), carry)
        dk, dv = lax.cond(rest % 2 == 1, lambda c: block(nb - 1, c, False), lambda c: c, carry)
        dk_ref[...] = dk * SM_SCALE
        dv_ref[...] = dv

        @pl.when(ki == nb - 1)
        def _():
            dq_ref[...] = dq_ref[...] * SM_SCALE

    def head(h, ki):
        return (0, h)

    def kblock(h, ki):
        return (ki, h)

    return _pcall(
        body, (q, k, v, do, o, lse), deps, name=name, grid=(N_HEADS, nb),
        out_shape=[jax.ShapeDtypeStruct((s, N_HEADS * QK_PAD), F32), jax.ShapeDtypeStruct((s, N_HEADS * QK_PAD), F32),
                   jax.ShapeDtypeStruct((s, N_HEADS * V_DIM), F32)],
        in_specs=[pl.BlockSpec((s, QK_PAD), head), pl.BlockSpec((t, QK_PAD), kblock), pl.BlockSpec((t, V_DIM), kblock),
                  pl.BlockSpec((s, V_DIM), head), pl.BlockSpec((s, V_DIM), head), pl.BlockSpec((s, LANES), head)],
        out_specs=[pl.BlockSpec((s, QK_PAD), head), pl.BlockSpec((t, QK_PAD), kblock), pl.BlockSpec((t, V_DIM), kblock)],
        compiler_params=_cparams(2, VMEM_MID),
    )


def _o_fwd(x, o, wo, name):
    s = x.shape[0]
    t = min(512, s)

    def body(x_ref, o_ref, wo_ref, xo_ref):
        xo_ref[...] = x_ref[...] + _nn(o_ref[...], wo_ref[...])

    return pl.pallas_call(
        body, name=name, grid=(s // t,),
        out_shape=jax.ShapeDtypeStruct((s, D_MODEL), F32),
        in_specs=[_rows(t, D_MODEL), _rows(t, D_MODEL), _full((D_MODEL, D_MODEL))],
        out_specs=_rows(t, D_MODEL),
        compiler_params=_cparams(1, VMEM_MID),
    )(x, o, wo)


def _o_bwd(dx, wo, name, deps=()):
    s = dx.shape[0]
    t = min(512, s)

    def body(dx_ref, wo_ref, do_ref, dxb_ref):
        dxb = dx_ref[...].astype(BF16)
        dxb_ref[...] = dxb
        do_ref[...] = _nt(dxb, wo_ref[...]).astype(BF16)

    tok = jax.ShapeDtypeStruct((s, D_MODEL), BF16)
    return _pcall(
        body, (dx, wo), deps, name=name, grid=(s // t,),
        out_shape=[tok, tok],
        in_specs=[_rows(t, D_MODEL), _full((D_MODEL, D_MODEL))],
        out_specs=[_rows(t, D_MODEL), _rows(t, D_MODEL)],
        compiler_params=_cparams(1, VMEM_MID),
    )


def _loss_head(y, target, name):
    s = y.shape[0]
    t = min(512, s)

    def body(y_ref, t_ref, dy_ref, sq_ref):
        e = y_ref[...] - t_ref[...]
        dy_ref[...] = e * (1.0 / D_MODEL)

        @pl.when(pl.program_id(0) == 0)
        def _():
            sq_ref[...] = jnp.zeros_like(sq_ref)

        sq_ref[...] += jnp.sum(e * e, axis=0, keepdims=True)

    return pl.pallas_call(
        body, name=name, grid=(s // t,),
        out_shape=[jax.ShapeDtypeStruct((s, D_MODEL), F32), jax.ShapeDtypeStruct((1, D_MODEL), F32)],
        in_specs=[_rows(t, D_MODEL), _rows(t, D_MODEL)],
        out_specs=[_rows(t, D_MODEL), _full((1, D_MODEL))],
        compiler_params=_cparams(1),
    )(y, target)


def _adamw(w, g, m, v, name):
    shape = w.shape
    c = shape[-1]
    r = math.prod(shape[:-1])
    tb = r
    for cand in (512, 256, 128):
        if r % cand == 0 and r > cand:
            tb = cand
            break

    def body(w_ref, g_ref, m_ref, v_ref, d_ref, mo_ref, vo_ref):
        gr = g_ref[...]
        mn = ADAM_B1 * m_ref[...] + (1.0 - ADAM_B1) * gr
        vn = ADAM_B2 * v_ref[...] + (1.0 - ADAM_B2) * (gr * gr)
        m_hat = mn / (1.0 - ADAM_B1 ** ADAM_STEP)
        v_hat = vn / (1.0 - ADAM_B2 ** ADAM_STEP)
        d_ref[...] = -ADAM_LR * (m_hat / (jnp.sqrt(v_hat) + ADAM_EPS) + ADAM_WD * w_ref[...])
        mo_ref[...] = mn
        vo_ref[...] = vn

    spec = pl.BlockSpec((tb, c), lambda i: (i, 0))
    flat = jax.ShapeDtypeStruct((r, c), F32)
    outs = pl.pallas_call(
        body, name=name, grid=(r // tb,),
        out_shape=[flat, flat, flat],
        in_specs=[spec] * 4, out_specs=[spec] * 3,
        compiler_params=_cparams(1),
    )(w.reshape(r, c), g.reshape(r, c), m.reshape(r, c), v.reshape(r, c))
    return [a.reshape(shape) for a in outs]


def _pad_cols(a, width):
    return jnp.pad(a, [(0, 0)] * (a.ndim - 1) + [(0, width - a.shape[-1])])


def _owner_view(a, sz):
    return a.reshape(a.shape[0], N_CHIPS, 2, sz, a.shape[-1])


def kernel(x, positions, ln_mix_a, w_pool, b_pool, pool_scale, ln_ffn, w_gate, w_up, w_down, ln_kv, w_dkv, g_kv_latent, w_uk, w_uv, g_k, ln_mix_b, w_dq, g_q_latent, w_uq, g_q, w_o, loss_target, m_ln_mix_a, m_w_pool, m_b_pool, m_pool_scale, m_ln_ffn, m_w_gate, m_w_up, m_w_down, m_ln_kv, m_w_dkv, m_g_kv_latent, m_w_uk, m_w_uv, m_g_k, m_ln_mix_b, m_w_dq, m_g_q_latent, m_w_uq, m_g_q, m_w_o, v_ln_mix_a, v_w_pool, v_b_pool, v_pool_scale, v_ln_ffn, v_w_gate, v_w_up, v_w_down, v_ln_kv, v_w_dkv, v_g_kv_latent, v_w_uk, v_w_uv, v_g_k, v_ln_mix_b, v_w_dq, v_g_q_latent, v_w_uq, v_g_q, v_w_o):
    weights = dict(ln_mix_a=ln_mix_a, w_pool=w_pool, b_pool=b_pool, pool_scale=pool_scale, ln_ffn=ln_ffn,
                   w_gate=w_gate, w_up=w_up, w_down=w_down, ln_kv=ln_kv, w_dkv=w_dkv, g_kv_latent=g_kv_latent,
                   w_uk=w_uk, w_uv=w_uv, g_k=g_k, ln_mix_b=ln_mix_b, w_dq=w_dq, g_q_latent=g_q_latent,
                   w_uq=w_uq, g_q=g_q, w_o=w_o)
    mom1 = dict(ln_mix_a=m_ln_mix_a, w_pool=m_w_pool, b_pool=m_b_pool, pool_scale=m_pool_scale, ln_ffn=m_ln_ffn,
                w_gate=m_w_gate, w_up=m_w_up, w_down=m_w_down, ln_kv=m_ln_kv, w_dkv=m_w_dkv,
                g_kv_latent=m_g_kv_latent, w_uk=m_w_uk, w_uv=m_w_uv, g_k=m_g_k, ln_mix_b=m_ln_mix_b, w_dq=m_w_dq,
                g_q_latent=m_g_q_latent, w_uq=m_w_uq, g_q=m_g_q, w_o=m_w_o)
    mom2 = dict(ln_mix_a=v_ln_mix_a, w_pool=v_w_pool, b_pool=v_b_pool, pool_scale=v_pool_scale, ln_ffn=v_ln_ffn,
                w_gate=v_w_gate, w_up=v_w_up, w_down=v_w_down, ln_kv=v_ln_kv, w_dkv=v_w_dkv,
                g_kv_latent=v_g_kv_latent, w_uk=v_w_uk, w_uv=v_w_uv, g_k=v_g_k, ln_mix_b=v_ln_mix_b, w_dq=v_w_dq,
                g_q_latent=v_g_q_latent, w_uq=v_w_uq, g_q=v_g_q, w_o=v_w_o)
    names = list(weights)
    dev = 4 * lax.axis_index("x") + 2 * lax.axis_index("y") + lax.axis_index("c")
    core = lax.axis_index("c").astype(jnp.int32).reshape(1)
    chip = (2 * lax.axis_index("x") + lax.axis_index("y")).astype(jnp.int32).reshape(1)

    xs = x[0]
    target = loss_target[0]
    cos, sin = _rope_tables(positions[0])

    def placed(shard):
        buf = lax.empty((shard.shape[0], N_DEV) + shard.shape[1:], shard.dtype)
        return lax.dynamic_update_slice(buf, shard[:, None], (0, dev, 0, 0))

    groups = {f"ffn{l}": [placed(jnp.stack([w_gate[l].T, w_up[l].T, w_down[l]]).astype(BF16))] for l in range(4)}
    groups["att"] = [placed(a.astype(BF16)) for a in (
        w_dkv[None, :, :KV_RANK], _pad_cols(w_dkv[None, :, KV_RANK:], LANES), w_uk[None], w_uv[None],
        w_dq, _pad_cols(w_uq, QK_PAD), w_o)]
    small_sh = jnp.concatenate([ln_mix_a.reshape(1, -1), pool_scale.reshape(1, -1), b_pool.reshape(1, -1)], axis=1)
    wp_g, small_g = _all_gather([w_pool.astype(BF16), small_sh], [2, 0], "gather_first")
    wp_all = wp_g.reshape(2, 4, GROUP_DIM, GROUP_DIM)
    small_g = small_g.reshape(N_DEV, 3, 2, LANES)
    ln_a_all = small_g[:, 0].transpose(1, 0, 2).reshape(2, 1, D_MODEL)
    sc_all = small_g[:, 1].transpose(1, 0, 2).reshape(2, 1, D_MODEL)
    bp_all = small_g[:, 2].reshape(N_DEV, 2, 4, 32).transpose(1, 2, 0, 3).reshape(2, 1, D_MODEL)
    sp0 = _copies_start(groups["ffn0"], 1, _gather_spread, "spread_ffn0", deps=[small_g])

    def spread_start(nm, deps):
        return _copies_start(groups[nm], len(groups[nm]), _gather_spread, f"spread_{nm}", deps=deps)

    def spread_wait(nm, state, after):
        ssem, rsem, bufs, _ = state
        return _copies_wait(bufs, ssem, rsem, after, _blocks_moved(4), f"spread_done_{nm}")

    def relay_start(nm, bufs, deps=()):
        return _copies_start(bufs, len(bufs), _gather_relay, f"relay_{nm}", deps=deps)

    def relay_wait(nm, state, after):
        ssem, rsem, bufs, _ = state
        return _copies_wait(bufs, ssem, rsem, after, _blocks_moved(3), f"relay_done_{nm}")

    gkn = g_k[:NOPE].reshape(1, NOPE)
    gkr = _pad_cols(g_k[NOPE:].reshape(1, ROPE), LANES)
    gl = g_kv_latent.reshape(1, KV_RANK)
    lnkv = ln_kv.reshape(1, D_MODEL)

    x_in, x_mid, pooled, gates, ups, w_ffn = [], [], [], [], [], []
    qs, outs, lses = [], [], []

    def mixer(l, cur, deps):
        x_in.append(cur)
        mid, dsave = _mix_fwd(cur, ln_a_all[l], wp_all[l], bp_all[l], sc_all[l], f"mix_fwd{l}", deps=deps)
        pooled.append(dsave)
        x_mid.append(mid)
        return mid

    def q_args(j):
        return (ln_mix_b[j].reshape(1, -1), wdq_all[j], g_q_latent[j].reshape(1, -1), wuq_all[j],
                g_q[j, :NOPE].reshape(1, -1), _pad_cols(g_q[j, NOPE:].reshape(1, -1), LANES), cos, sin)

    def attention(j, cur, deps):
        x_in.append(cur)
        q = _q_fwd(cur, *q_args(j), f"q_fwd{j}", deps=deps)
        o, lse = _att_fwd(q, k_sh, v_sh, f"att_fwd{j}")
        mid = _o_fwd(cur, o, wo_all[j], f"o_fwd{j}")
        qs.append(q)
        outs.append(o)
        lses.append(lse)
        x_mid.append(mid)
        return mid

    def ffn(l, mid, relayed):
        w_l = relayed[0].reshape(3, D_FF, D_MODEL)
        w_ffn.append(w_l)
        cur, gate, up = _ffn_fwd(mid, ln_ffn[l].reshape(1, -1), w_l, f"ffn_fwd{l}")
        gates.append(gate)
        ups.append(up)
        return cur

    mid = mixer(0, xs, [sp0[3]])
    landed0 = spread_wait("ffn0", sp0, mid)
    sp1 = spread_start("ffn1", [landed0[0]])
    rl0 = relay_start("ffn0", landed0, [sp1[3]])
    cur = ffn(0, mid, relay_wait("ffn0", rl0, rl0[3]))

    landed1 = spread_wait("ffn1", sp1, cur)
    sp_att = spread_start("att", [landed1[0]])
    sp2 = spread_start("ffn2", [landed1[0]])
    rl1 = relay_start("ffn1", landed1, [sp_att[3], sp2[3]])
    mid = mixer(1, cur, [rl1[3]])
    cur = ffn(1, mid, relay_wait("ffn1", rl1, mid))
    x_kv = cur

    landed_att = spread_wait("att", sp_att, cur)
    landed2 = spread_wait("ffn2", sp2, cur)
    sp3 = spread_start("ffn3", [landed2[0]])
    rl_att = relay_start("att", landed_att, [sp3[3]])
    rl2 = relay_start("ffn2", landed2, [sp3[3]])
    att_bufs = relay_wait("att", rl_att, rl2[3])
    wc = att_bufs[0].reshape(D_MODEL, KV_RANK)
    wpe = att_bufs[1].reshape(D_MODEL, LANES)
    wuk_g = att_bufs[2].reshape(N_HEADS, KV_RANK, NOPE)
    wuv_g = att_bufs[3].reshape(N_HEADS, KV_RANK, V_DIM)
    wdq_all = att_bufs[4].reshape(2, D_MODEL, Q_RANK)
    wuq_all = att_bufs[5]
    wo_all = att_bufs[6].reshape(2, D_MODEL, D_MODEL)
    k_sh, v_sh = _kv_fwd(cur, lnkv, wc, wpe, gl, wuk_g, wuv_g, gkn, gkr, cos, sin, "kv_fwd")
    mid = attention(0, cur, [])
    cur = ffn(2, mid, relay_wait("ffn2", rl2, mid))

    landed3 = spread_wait("ffn3", sp3, cur)
    rl3 = relay_start("ffn3", landed3)
    mid = attention(1, cur, [rl3[3]])
    cur = ffn(3, mid, relay_wait("ffn3", rl3, mid))

    dx, sq_cols = _loss_head(cur, target, "loss_head")

    small = {}
    sizes = dict(ffn0=FF_SHARD, ffn1=FF_SHARD, ffn2=FF_SHARD, ffn3=FF_SHARD, wo=128, kv512=128, dkv_pe=128,
                 wdq=128, wuqT=QK_PAD, wpool=32)
    big = dict(wo=lax.empty((2, D_MODEL, D_MODEL), BF16), kv512=lax.empty((3, D_MODEL, KV_RANK), BF16),
               dkv_pe=lax.empty((1, D_MODEL, LANES), BF16), wdq=lax.empty((2, D_MODEL, Q_RANK), BF16),
               wuqT=lax.empty((2, N_HEADS * QK_PAD, Q_RANK), BF16), wpool=lax.empty((8, GROUP_DIM, GROUP_DIM), BF16))
    for l in range(4):
        big[f"ffn{l}"] = lax.empty((3, D_FF, D_MODEL), BF16)
    red = {}

    def pair_start(nms, tag):
        arrs = []
        for nm in nms:
            view = _owner_view(big[nm], sizes[nm])
            arrs += [view, lax.empty((view.shape[0], N_CHIPS) + view.shape[3:], BF16)]
        return nms, tag, _copies_start(arrs, len(nms), _pair_send, f"pair_start_{tag}")

    def chip_start(state, after):
        nms, tag, (ssem, rsem, arrs, _) = state
        arrs = _copies_wait(arrs, ssem, rsem, after, _landed, f"pair_done_{tag}")
        out = []
        for t, nm in enumerate(nms):
            part = _pair_sum(arrs[2 * t], arrs[2 * t + 1], core, f"pair_sum_{nm}")
            out += [part, lax.empty((3, part.shape[0]) + part.shape[2:], BF16)]
        return nms, tag, _copies_start(out, len(nms), _chip_send, f"chip_start_{tag}")

    deferred = []

    def chip_finish(state, after, defer=False):
        nms, tag, (ssem, rsem, arrs, _) = state
        arrs = _copies_wait(arrs, ssem, rsem, after, _landed, f"chip_done_{tag}")
        for t, nm in enumerate(nms):
            if defer:
                deferred.append((nm, arrs[2 * t], arrs[2 * t + 1]))
            else:
                red[nm] = _chip_sum(arrs[2 * t], arrs[2 * t + 1], chip, f"chip_sum_{nm}")

    ffn_grads = {nm: lax.empty((4,) + weights[nm].shape[1:], F32) for nm in ("w_gate", "w_up", "w_down")}

    def place_ffn_grads(l):
        g = red[f"ffn{l}"]
        ffn_grads["w_gate"] = ffn_grads["w_gate"].at[l].set(g[0].T)
        ffn_grads["w_up"] = ffn_grads["w_up"].at[l].set(g[1].T)
        ffn_grads["w_down"] = ffn_grads["w_down"].at[l].set(g[2])

    dks, dvs = [], []
    pending = None
    bwd_deps = []
    for l in (3, 2, 1, 0):
        key = f"ffn{l}"
        dx, act, dgb, dub, hn, dyb, dln = _ffn_bwd(x_mid[l], dx, gates[l], ups[l], ln_ffn[l].reshape(1, -1),
                                                     w_ffn[l], f"ffn_bwd{l}", deps=bwd_deps)
        bwd_deps = []
        small[f"ln_ffn{l}"] = dln
        if l == 1:
            att_chip = chip_start(att_pair, dx)
            tn_deps = [att_chip[2][3]]
        else:
            tn_deps = []
        if pending:
            chip_finish(pending, dx, defer=True)
            pending = None
        big[key] = _tn_matmul(dgb, hn, big[key], 0, f"dw_gate{l}", m_chunk=FF_HALF, deps=tn_deps)
        big[key] = _tn_matmul(dub, hn, big[key], 1, f"dw_up{l}", m_chunk=FF_HALF)
        big[key] = _tn_matmul(act, dyb, big[key], 2, f"dw_down{l}", m_chunk=FF_HALF)
        if l == 1:
            chip_finish(att_chip, big[key])
        ffn_pair = pair_start([key], key)
        if l >= 2:
            j = l - 2
            do, dxb = _o_bwd(dx, wo_all[j], f"o_bwd{j}", deps=[ffn_pair[2][3]])
            big["wo"] = _tn_matmul(outs[j], dxb, big["wo"], j, f"dw_o{j}")
            ffn_chip = chip_start(ffn_pair, big["wo"])
            dq, dk, dv = _att_bwd(qs[j], k_sh, v_sh, do, outs[j], lses[j], f"att_bwd{j}", deps=[ffn_chip[2][3]])
            chip_finish(ffn_chip, dq, defer=True)
            dks.append(dk)
            dvs.append(dv)
            dx, hnq, cqn, dqa, dcq, dln, dgql, dgqn, dgqr = _q_bwd(x_in[l], dx, dq, *q_args(j), f"q_bwd{j}")
            small[f"ln_mix_b{j}"] = dln
            small[f"g_q_latent{j}"] = dgql
            small[f"g_q{j}"] = jnp.concatenate([dgqn, dgqr[:, :ROPE]], axis=1)
            big["wdq"] = _tn_matmul(hnq, dcq, big["wdq"], j, f"dw_dq{j}")
            big["wuqT"] = _tn_matmul(dqa, cqn, big["wuqT"], j, f"dw_uq{j}")
            if l == 2:
                (dx, hnk, cn, dknb, dvb, dccb, dpeb, dlnkv, dgl, dgkn, dgkr) = _kv_bwd(
                    x_kv, dx, dks, dvs, lnkv, wc, wpe, gl, wuk_g, wuv_g, gkn, gkr, cos, sin, "kv_bwd")
                small["ln_kv"] = dlnkv
                small["g_kv_latent"] = dgl
                small["g_k"] = jnp.concatenate([dgkn, dgkr[:, :ROPE]], axis=1)
                big["kv512"] = _tn_matmul(dknb, cn, big["kv512"], 0, "dw_uk")
                big["kv512"] = _tn_matmul(dvb, cn, big["kv512"], 1, "dw_uv")
                big["kv512"] = _tn_matmul(hnk, dccb, big["kv512"], 2, "dw_dkv_c")
                big["dkv_pe"] = _tn_matmul(hnk, dpeb, big["dkv_pe"], 0, "dw_dkv_pe")
                att_pair = pair_start(["wo", "kv512", "dkv_pe", "wdq", "wuqT"], "att")
                bwd_deps = [att_pair[2][3]]
        else:
            dx, dyp, dsc, db, dln = _mix_bwd(x_in[l], dx, pooled[l], ln_a_all[l], wp_all[l], bp_all[l], sc_all[l],
                                             f"mix_bwd{l}", deps=[ffn_pair[2][3]])
            small[f"ln_mix_a{l}"] = dln
            small[f"pool_scale{l}"] = dsc
            small[f"b_pool{l}"] = db
            ffn_chip = chip_start(ffn_pair, dx)
            big["wpool"] = _tn_matmul(pooled[l], dyp, big["wpool"], 4 * l, f"dw_pool{l}", groups=4,
                                      deps=[ffn_chip[2][3]])
            if l == 1:
                pending = ffn_chip
            else:
                for nm, part, land in deferred:
                    red[nm] = _chip_sum(part, land, chip, f"chip_sum_{nm}", deps=[ffn_chip[2][3]])
                    place_ffn_grads(int(nm[-1]))
                chip_finish(ffn_chip, [big["wpool"]] + list(ffn_grads.values()))
                place_ffn_grads(0)
    grad_x = dx[None]
    pool_pair = pair_start(["wpool"], "wpool")
    pool_chip = chip_start(pool_pair, pool_pair[2][3])
    chip_finish(pool_chip, pool_chip[2][3])

    vec_names = (["loss"] + [f"ln_ffn{l}" for l in range(4)] + ["ln_kv", "g_kv_latent", "g_k"]
                 + [f"{p}{j}" for p in ("ln_mix_b", "g_q_latent", "g_q") for j in range(2)]
                 + [f"{p}{l}" for p in ("ln_mix_a", "pool_scale", "b_pool") for l in range(2)])
    small["loss"] = sq_cols
    widths = [small[nm].shape[1] for nm in vec_names]
    padded = [-(-w // LANES) * LANES for w in widths]
    packed = jnp.concatenate([_pad_cols(small[nm], pw) for nm, pw in zip(vec_names, padded)], axis=1)
    (all_vecs,) = _all_gather([packed], [0], "gather_vectors")
    total = _sum_lead(all_vecs, "sum_vectors")
    vec = {}
    off = 0
    for nm, w, pw in zip(vec_names, widths, padded):
        vec[nm] = total[0, off:off + w]
        off += pw
    loss = 0.5 * jnp.sum(vec["loss"]) * (1.0 / D_MODEL)

    def own_cols(full, width):
        return lax.dynamic_slice_in_dim(full, dev * width, width, axis=full.ndim - 1)

    grads = dict(
        ln_mix_a=own_cols(jnp.stack([vec["ln_mix_a0"], vec["ln_mix_a1"]]), LANES),
        w_pool=red["wpool"].reshape(2, 4, 32, GROUP_DIM),
        b_pool=own_cols(jnp.stack([vec["b_pool0"], vec["b_pool1"]]).reshape(2, 4, GROUP_DIM), 32),
        pool_scale=own_cols(jnp.stack([vec["pool_scale0"], vec["pool_scale1"]]), LANES),
        ln_ffn=jnp.stack([vec[f"ln_ffn{l}"] for l in range(4)]),
        w_gate=ffn_grads["w_gate"],
        w_up=ffn_grads["w_up"],
        w_down=ffn_grads["w_down"],
        ln_kv=vec["ln_kv"],
        w_dkv=jnp.concatenate([red["kv512"][2], red["dkv_pe"][0][:, :ROPE]], axis=1),
        g_kv_latent=vec["g_kv_latent"],
        w_uk=red["kv512"][0].T,
        w_uv=red["kv512"][1].T,
        g_k=vec["g_k"],
        ln_mix_b=jnp.stack([vec["ln_mix_b0"], vec["ln_mix_b1"]]),
        w_dq=red["wdq"],
        g_q_latent=jnp.stack([vec["g_q_latent0"], vec["g_q_latent1"]]),
        w_uq=red["wuqT"].transpose(0, 2, 1)[:, :, :QK_DIM],
        g_q=jnp.stack([vec["g_q0"], vec["g_q1"]]),
        w_o=red["wo"],
    )

    deltas, new_m, new_v = {}, {}, {}
    for nm in names:
        w = weights[nm]
        shape = w.shape if w.ndim > 1 else (1, w.shape[0])
        d, mo, vo = _adamw(w.reshape(shape), grads[nm].reshape(shape), mom1[nm].reshape(shape),
                           mom2[nm].reshape(shape), f"adamw_{nm}")
        deltas[nm], new_m[nm], new_v[nm] = d.reshape(w.shape), mo.reshape(w.shape), vo.reshape(w.shape)

    return (loss, grad_x, *[grads[nm].reshape(weights[nm].shape) for nm in names], *[deltas[nm] for nm in names],
            *[new_m[nm] for nm in names], *[new_v[nm] for nm in names])
```

```python
import functools
import math

import jax
import jax.numpy as jnp
from jax import lax
from jax.experimental import pallas as pl
from jax.experimental.pallas import tpu as pltpu

F32 = jnp.float32
BF16 = jnp.bfloat16
MESH = pl.DeviceIdType.MESH

D_MODEL = 1024
D_FF = 2816
N_DEV = 8
N_CHIPS = 4
FF_SHARD = D_FF // N_DEV
FF_HALF = D_FF // 2
N_HEADS = 8
NOPE = 128
ROPE = 64
QK_DIM = NOPE + ROPE
QK_PAD = 256
V_DIM = 128
Q_RANK = 256
KV_RANK = 512
POOL_WINDOWS = (2, 4, 8, 16)
GROUP_DIM = 256
HALO = 128
CHUNK = 64
ROPE_THETA = 10000.0
EPS = 1e-6
LANES = 128

ADAM_LR = 0.001
ADAM_B1 = 0.9
ADAM_B2 = 0.999
ADAM_EPS = 1e-08
ADAM_WD = 0.01
ADAM_STEP = 10

PROJ_ROWS = 256
VMEM_BIG = 56 * 2**20
VMEM_MID = 40 * 2**20


def _nn(a, b):
    return lax.dot_general(a, b, (((1,), (0,)), ((), ())), preferred_element_type=F32)


def _nt(a, b):
    return lax.dot_general(a, b, (((1,), (1,)), ((), ())), preferred_element_type=F32)


def _tn(a, b):
    return lax.dot_general(a, b, (((0,), (0,)), ((), ())), preferred_element_type=F32)


def _rms(x, g, n):
    r = lax.rsqrt(jnp.sum(x * x, axis=-1, keepdims=True) * (1.0 / n) + EPS)
    return (x * r) * g, r


def _rms_bwd(x, r, g, dy, n):
    u = dy * g
    s = jnp.sum(x * u, axis=-1, keepdims=True) * (1.0 / n)
    dx = r * u - x * (r * r * r * s)
    dg = jnp.sum(dy * (x * r), axis=0, keepdims=True)
    return dx, dg


def _swap_perm():
    i = lax.broadcasted_iota(jnp.int32, (LANES, LANES), 0)
    j = lax.broadcasted_iota(jnp.int32, (LANES, LANES), 1)
    half = ROPE // 2
    hit = ((j < half) & (i == j + half)) | ((j >= half) & (j < ROPE) & (i == j - half))
    return jnp.where(hit, 1.0, 0.0).astype(BF16)


def _swap_halves(z, perm):
    hi = z.astype(BF16)
    lo = (z - hi.astype(F32)).astype(BF16)
    return _nn(hi, perm) + _nn(lo, perm)


def _sigmoid(x):
    return 1.0 / (1.0 + jnp.exp(-x))


def _cparams(n_grid, vmem=None):
    return pltpu.CompilerParams(dimension_semantics=("arbitrary",) * n_grid, vmem_limit_bytes=vmem)


def _rows(t, cols):
    return pl.BlockSpec((t, cols), lambda i: (i, 0))


def _full(shape):
    nd = len(shape)
    return pl.BlockSpec(shape, lambda *_: (0,) * nd)


ANY = pl.BlockSpec(memory_space=pl.ANY)


def _pcall(body, args, deps, *, in_specs, **kw):
    n_in, n_dep = len(args), len(deps)

    def ordered(*refs):
        body(*refs[:n_in], *refs[n_in + n_dep:])

    return pl.pallas_call(ordered, in_specs=list(in_specs) + [ANY] * n_dep, **kw)(*args, *deps)


def _place():
    x, y, c = lax.axis_index("x"), lax.axis_index("y"), lax.axis_index("c")
    return x, y, c


def _all_gather(shards, axes, name, deps=()):
    n, nd = len(shards), len(deps)
    out_shape = [jax.ShapeDtypeStruct(s.shape[:a] + (N_DEV,) + s.shape[a:], s.dtype) for s, a in zip(shards, axes)]

    def body(*refs):
        ins, outs = refs[:n], refs[n + nd:2 * n + nd]
        send_sems, recv_sems, local_sems = refs[2 * n + nd:]
        x, y, c = _place()
        me, sibling = (x, y, c), (x, y, 1 - c)
        chips = [(1 - x, y), (x, 1 - y), (1 - x, 1 - y)]

        def slot(t, dev):
            idx = 4 * dev[0] + 2 * dev[1] + dev[2]
            return outs[t].at[(slice(None),) * axes[t] + (idx,)]

        def copy(t, k, block, to, src=None):
            return pltpu.make_async_remote_copy(
                src_ref=slot(t, block) if src is None else src, dst_ref=slot(t, block),
                send_sem=send_sems.at[t, k], recv_sem=recv_sems.at[t, k],
                device_id=to, device_id_type=MESH)

        mine = [pltpu.make_async_copy(ins[t], slot(t, me), local_sems.at[t]) for t in range(n)]
        for cp in mine:
            cp.start()
        first = []
        for t in range(n):
            first.append(copy(t, 0, me, sibling, src=ins[t]))
            first += [copy(t, 1 + j, me, (*chip, c), src=ins[t]) for j, chip in enumerate(chips)]
        for cp in first:
            cp.start()
        passed = []
        for j, chip in enumerate(chips):
            for t in range(n):
                copy(t, 1 + j, (*chip, c), me).wait_recv()
                cp = copy(t, 4 + j, (*chip, c), sibling)
                cp.start()
                passed.append(cp)
        for t in range(n):
            copy(t, 0, sibling, me).wait_recv()
            for j, chip in enumerate(chips):
                copy(t, 4 + j, (*chip, 1 - c), me).wait_recv()
        for cp in first + passed:
            cp.wait_send()
        for cp in mine:
            cp.wait()

    return pl.pallas_call(
        body, name=name, out_shape=out_shape,
        in_specs=[ANY] * (n + nd), out_specs=[ANY] * n,
        scratch_shapes=[pltpu.SemaphoreType.DMA((n, 7)), pltpu.SemaphoreType.DMA((n, 7)),
                        pltpu.SemaphoreType.DMA((n,))],
    )(*shards, *deps)


HBM = pl.BlockSpec(memory_space=pltpu.HBM)
SEM = pl.BlockSpec(memory_space=pltpu.SEMAPHORE)
EFFECT = pltpu.SideEffectType.DATAFLOW_SIDE_EFFECTING


def _copies_start(arrays, n_sems, plan, name, deps=()):
    n, nd = len(arrays), len(deps)

    def body(*refs):
        for cp in plan(refs[:n], refs[n + nd], refs[n + nd + 1]):
            cp.start()
        refs[-1][...] = jnp.zeros_like(refs[-1])

    outs = pl.pallas_call(
        body, name=name,
        out_shape=(pltpu.SemaphoreType.DMA((n_sems,)), pltpu.SemaphoreType.DMA((n_sems,)),
                   *[pltpu.HBM(a.shape, a.dtype) for a in arrays], jax.ShapeDtypeStruct((8, LANES), F32)),
        in_specs=[HBM] * n + [ANY] * nd,
        out_specs=(SEM, SEM, *[HBM] * n, pl.BlockSpec(memory_space=pltpu.VMEM)),
        input_output_aliases={i: 2 + i for i in range(n)},
        compiler_params=pltpu.CompilerParams(has_side_effects=EFFECT),
    )(*[pltpu.with_memory_space_constraint(a, pltpu.HBM) for a in arrays], *deps)
    return outs[0], outs[1], list(outs[2:2 + n]), outs[-1]


def _copies_wait(arrays, send_sems, recv_sems, after, plan, name):
    n = len(arrays)
    after = list(after) if isinstance(after, (list, tuple)) else [after]

    def body(*refs):
        for cp in plan(refs[:n], refs[n], refs[n + 1]):
            cp.wait_send()
            cp.wait_recv()

    outs = pl.pallas_call(
        body, name=name,
        out_shape=tuple(pltpu.HBM(a.shape, a.dtype) for a in arrays),
        in_specs=[HBM] * n + [SEM, SEM] + [ANY] * len(after), out_specs=tuple([HBM] * n),
        input_output_aliases={i: i for i in range(n)},
        compiler_params=pltpu.CompilerParams(has_side_effects=EFFECT),
    )(*arrays, send_sems, recv_sems, *after)
    return list(outs)


def _remote(src, dst, send_sems, recv_sems, t, to):
    return pltpu.make_async_remote_copy(src_ref=src, dst_ref=dst, send_sem=send_sems.at[t], recv_sem=recv_sems.at[t],
                                        device_id=to, device_id_type=MESH)


def _dev_index(x, y, c):
    return 4 * x + 2 * y + c


def _gather_spread(bufs, send_sems, recv_sems):
    x, y, c = _place()
    mine = _dev_index(x, y, c)
    peers = [(x, y, 1 - c), (1 - x, y, c), (x, 1 - y, c), (1 - x, 1 - y, c)]
    return [_remote(g.at[k, mine], g.at[k, mine], send_sems, recv_sems, t, peer)
            for t, g in enumerate(bufs) for peer in peers for k in range(g.shape[0])]


def _gather_relay(bufs, send_sems, recv_sems):
    x, y, c = _place()
    blocks = [_dev_index(1 - x, y, c), _dev_index(x, 1 - y, c), _dev_index(1 - x, 1 - y, c)]
    return [_remote(g.at[k, b], g.at[k, b], send_sems, recv_sems, t, (x, y, 1 - c))
            for t, g in enumerate(bufs) for b in blocks for k in range(g.shape[0])]


def _blocks_moved(count):
    def plan(bufs, send_sems, recv_sems):
        x, y, c = _place()
        return [_remote(g.at[:, pl.ds(0, count)], g.at[:, pl.ds(0, count)], send_sems, recv_sems, t, (x, y, 1 - c))
                for t, g in enumerate(bufs)]
    return plan


def _pair_send(arrs, send_sems, recv_sems):
    x, y, c = _place()
    return [_remote(arrs[2 * t].at[p, k, 1 - c], arrs[2 * t + 1].at[p, k], send_sems, recv_sems, t, (x, y, 1 - c))
            for t in range(len(arrs) // 2) for p in range(arrs[2 * t].shape[0]) for k in range(N_CHIPS)]


def _chip_send(arrs, send_sems, recv_sems):
    x, y, c = _place()
    chips = [(1 - x, y), (x, 1 - y), (1 - x, 1 - y)]
    return [_remote(arrs[2 * t].at[p, 2 * px + py], arrs[2 * t + 1].at[j, p], send_sems, recv_sems, t, (px, py, c))
            for t in range(len(arrs) // 2) for j, (px, py) in enumerate(chips) for p in range(arrs[2 * t].shape[0])]


def _landed(arrs, send_sems, recv_sems):
    x, y, c = _place()
    return [_remote(arrs[2 * t + 1], arrs[2 * t + 1], send_sems, recv_sems, t, (x, y, 1 - c))
            for t in range(len(arrs) // 2)]


def _rows_per_step(rows, row_elems):
    best = 1
    for cand in range(1, rows + 1):
        if rows % cand == 0 and cand * row_elems <= 256 * 1024:
            best = cand
    return best


def _pair_sum(grad, landed, core, name):
    p, _, _, sz, c = grad.shape
    r = _rows_per_step(p * N_CHIPS, sz * c)

    def body(core_ref, g_ref, l_ref, o_ref):
        o_ref[...] = (g_ref[...].astype(F32) + l_ref[...].astype(F32)).astype(o_ref.dtype)

    out = pl.pallas_call(
        body, name=name,
        grid_spec=pltpu.PrefetchScalarGridSpec(
            num_scalar_prefetch=1, grid=(p * N_CHIPS // r,),
            in_specs=[pl.BlockSpec((r, None, sz, c), lambda i, cr: (i, cr[0], 0, 0)),
                      pl.BlockSpec((r, sz, c), lambda i, cr: (i, 0, 0))],
            out_specs=pl.BlockSpec((r, sz, c), lambda i, cr: (i, 0, 0))),
        out_shape=jax.ShapeDtypeStruct((p * N_CHIPS, sz, c), grad.dtype),
        compiler_params=_cparams(1),
    )(core, grad.reshape(p * N_CHIPS, 2, sz, c), landed.reshape(p * N_CHIPS, sz, c))
    return out.reshape(p, N_CHIPS, sz, c)


def _chip_sum(parts, landed, chip, name, deps=()):
    p, _, sz, c = parts.shape
    r = _rows_per_step(p, sz * c)

    def body(chip_ref, a_ref, l_ref, o_ref):
        acc = a_ref[...].astype(F32)
        for j in range(3):
            acc = acc + l_ref[j].astype(F32)
        o_ref[...] = acc

    nd = len(deps)

    def ordered(chip_ref, a_ref, l_ref, *rest):
        body(chip_ref, a_ref, l_ref, rest[nd])

    return pl.pallas_call(
        ordered, name=name,
        grid_spec=pltpu.PrefetchScalarGridSpec(
            num_scalar_prefetch=1, grid=(p // r,),
            in_specs=[pl.BlockSpec((r, None, sz, c), lambda i, cr: (i, cr[0], 0, 0)),
                      pl.BlockSpec((3, r, sz, c), lambda i, cr: (0, i, 0, 0))] + [ANY] * nd,
            out_specs=pl.BlockSpec((r, sz, c), lambda i, cr: (i, 0, 0))),
        out_shape=jax.ShapeDtypeStruct((p, sz, c), F32),
        compiler_params=_cparams(1),
    )(chip, parts, landed, *deps)


def _sum_lead(a, name, out_dtype=F32):
    k = a.shape[0]
    rest = a.shape[1:]
    r, c = rest[-2], rest[-1]
    lead = math.prod(rest[:-2])
    a3 = a.reshape(k, lead * r, c)
    rows = lead * r
    tb = rows
    for cand in (512, 256, 128, 64, 32, 16, 8):
        if rows % cand == 0 and rows > cand:
            tb = cand
            break

    def body(a_ref, o_ref):
        acc = a_ref[0].astype(F32)
        for i in range(1, k):
            acc = acc + a_ref[i].astype(F32)
        o_ref[...] = acc.astype(out_dtype)

    out = pl.pallas_call(
        body, name=name, grid=(rows // tb,),
        out_shape=jax.ShapeDtypeStruct((rows, c), out_dtype),
        in_specs=[pl.BlockSpec((k, tb, c), lambda i: (0, i, 0))],
        out_specs=pl.BlockSpec((tb, c), lambda i: (i, 0)),
        compiler_params=_cparams(1),
    )(a3)
    return out.reshape(rest)


def _bands(t, causal):
    r = lax.broadcasted_iota(jnp.int32, (t, t + HALO), 0)
    col = lax.broadcasted_iota(jnp.int32, (t, t + HALO), 1)
    diff = r + HALO - col if causal else col - r
    return jnp.stack([jnp.where((diff >= 0) & (diff < w), 1.0, 0.0) for w in POOL_WINDOWS]).astype(BF16)


def _split_dot(band, v):
    hi = v.astype(BF16)
    lo = (v - hi.astype(F32)).astype(BF16)
    return _nn(band, hi) + _nn(band, lo)


def _mix_fwd(x, g, wp, b, sc, name, deps=()):
    s = x.shape[0]
    t = min(256, s)
    rb = t // HALO

    def body(x_ref, xh_ref, g_ref, wp_ref, b_ref, sc_ref, band_ref, xo_ref, d_ref):
        i = pl.program_id(0)
        gg = g_ref[...]
        h, _ = _rms(x_ref[...], gg, D_MODEL)
        hh, _ = _rms(xh_ref[...], gg, D_MODEL)
        hh = jnp.where(i > 0, hh, 0.0)
        hext = jnp.concatenate([hh, h], axis=0)
        tok = i * t + lax.broadcasted_iota(jnp.int32, (t, 1), 0)
        for gi, w in enumerate(POOL_WINDOWS):
            sl = slice(gi * GROUP_DIM, (gi + 1) * GROUP_DIM)
            win = _split_dot(band_ref[gi], hext[:, sl])
            cnt = jnp.minimum(tok + 1, w).astype(F32)
            dbf = (win / cnt - h[:, sl]).astype(BF16)
            d_ref[:, sl] = dbf
            ypre = _nn(dbf, wp_ref[gi]) + b_ref[:, sl]
            xo_ref[:, sl] = x_ref[:, sl] + ypre * sc_ref[:, sl]

    return _pcall(
        body, (x, x, g, wp, b, sc, _bands(t, True)), deps, name=name, grid=(s // t,),
        out_shape=[jax.ShapeDtypeStruct((s, D_MODEL), F32), jax.ShapeDtypeStruct((s, D_MODEL), BF16)],
        in_specs=[_rows(t, D_MODEL),
                  pl.BlockSpec((HALO, D_MODEL), lambda i: (jnp.maximum(i * rb - 1, 0), 0)),
                  _full((1, D_MODEL)), _full((4, GROUP_DIM, GROUP_DIM)), _full((1, D_MODEL)), _full((1, D_MODEL)),
                  _full((4, t, t + HALO))],
        out_specs=[_rows(t, D_MODEL), _rows(t, D_MODEL)],
        compiler_params=_cparams(1, VMEM_MID),
    )


def _mix_bwd(x, dy, d, g, wp, b, sc, name, deps=()):
    s = x.shape[0]
    t = min(256, s)
    rb = t // HALO
    nb = s // t
    last_halo = s // HALO - 1

    def body(x_ref, dy_ref, dyn_ref, d_ref, g_ref, wp_ref, b_ref, sc_ref, band_ref,
             dx_ref, dyp_ref, dsc_ref, db_ref, dln_ref):
        i = pl.program_id(0)
        x = x_ref[...]
        gg = g_ref[...]
        dy = dy_ref[...]
        sc = sc_ref[...]
        dyp32 = dy * sc
        dyp = dyp32.astype(BF16)
        dyph = (dyn_ref[...] * sc).astype(BF16)
        dyp_ref[...] = dyp
        tok = i * t + lax.broadcasted_iota(jnp.int32, (t + HALO, 1), 0)
        dh, dsc = [], []
        for gi, w in enumerate(POOL_WINDOWS):
            sl = slice(gi * GROUP_DIM, (gi + 1) * GROUP_DIM)
            ypre = _nn(d_ref[:, sl], wp_ref[gi]) + b_ref[:, sl]
            dsc.append(jnp.sum(dy[:, sl] * ypre, axis=0, keepdims=True))
            dd = _nt(dyp[:, sl], wp_ref[gi])
            ddh = jnp.where(i < nb - 1, _nt(dyph[:, sl], wp_ref[gi]), 0.0)
            cnt = jnp.minimum(tok + 1, w).astype(F32)
            ddext = jnp.concatenate([dd, ddh], axis=0) / cnt
            dh.append(_split_dot(band_ref[gi], ddext) - dd)
        dh = jnp.concatenate(dh, axis=1)
        _, r = _rms(x, gg, D_MODEL)
        dxn, dg = _rms_bwd(x, r, gg, dh, D_MODEL)
        dx_ref[...] = dy + dxn

        @pl.when(i == 0)
        def _():
            dsc_ref[...] = jnp.zeros_like(dsc_ref)
            db_ref[...] = jnp.zeros_like(db_ref)
            dln_ref[...] = jnp.zeros_like(dln_ref)

        dsc_ref[...] += jnp.concatenate(dsc, axis=1)
        db_ref[...] += jnp.sum(dyp32, axis=0, keepdims=True)
        dln_ref[...] += dg

    vec = jax.ShapeDtypeStruct((1, D_MODEL), F32)
    return _pcall(
        body, (x, dy, dy, d, g, wp, b, sc, _bands(t, False)), deps, name=name, grid=(nb,),
        out_shape=[jax.ShapeDtypeStruct((s, D_MODEL), F32), jax.ShapeDtypeStruct((s, D_MODEL), BF16), vec, vec, vec],
        in_specs=[_rows(t, D_MODEL), _rows(t, D_MODEL),
                  pl.BlockSpec((HALO, D_MODEL), lambda i: (jnp.minimum((i + 1) * rb, last_halo), 0)),
                  _rows(t, D_MODEL),
                  _full((1, D_MODEL)), _full((4, GROUP_DIM, GROUP_DIM)), _full((1, D_MODEL)), _full((1, D_MODEL)),
                  _full((4, t, t + HALO))],
        out_specs=[_rows(t, D_MODEL), _rows(t, D_MODEL), _full((1, D_MODEL)), _full((1, D_MODEL)), _full((1, D_MODEL))],
        compiler_params=_cparams(1, VMEM_MID),
    )


def _load_weights(w_hbm, w_vmem, sem):
    @pl.when(pl.program_id(0) == 0)
    def _():
        cp = pltpu.make_async_copy(w_hbm, w_vmem, sem)
        cp.start()
        cp.wait()


def _ffn_fwd(x, g, w, name):
    s = x.shape[0]
    t = min(512, s)

    def body(x_ref, g_ref, w_hbm, xo_ref, gate_ref, up_ref, w_ref, sem):
        _load_weights(w_hbm, w_ref, sem)
        x = x_ref[...]
        hn = _rms(x, g_ref[...], D_MODEL)[0].astype(BF16)
        acc = x
        for c in range(2):
            rs = slice(c * FF_HALF, (c + 1) * FF_HALF)
            gt = _nt(hn, w_ref[0, rs, :])
            up = _nt(hn, w_ref[1, rs, :])
            gate_ref[:, rs] = gt.astype(BF16)
            up_ref[:, rs] = up.astype(BF16)
            act = ((gt * _sigmoid(gt)) * up).astype(BF16)
            acc = acc + _nn(act, w_ref[2, rs, :])
        xo_ref[...] = acc

    hid = jax.ShapeDtypeStruct((s, D_FF), BF16)
    return pl.pallas_call(
        body, name=name, grid=(s // t,),
        out_shape=[jax.ShapeDtypeStruct((s, D_MODEL), F32), hid, hid],
        in_specs=[_rows(t, D_MODEL), _full((1, D_MODEL)), ANY],
        out_specs=[_rows(t, D_MODEL), _rows(t, D_FF), _rows(t, D_FF)],
        scratch_shapes=[pltpu.VMEM((3, D_FF, D_MODEL), BF16), pltpu.SemaphoreType.DMA],
        compiler_params=_cparams(1, VMEM_BIG),
    )(x, g, w)


def _ffn_bwd(x, dy, gate, up, g, w, name, deps=()):
    s = x.shape[0]
    t = min(256, s)

    def body(x_ref, dy_ref, gate_ref, up_ref, g_ref, w_hbm,
             dx_ref, act_ref, dg_ref, du_ref, hn_ref, dyb_ref, dln_ref, w_ref, sem):
        _load_weights(w_hbm, w_ref, sem)
        x = x_ref[...]
        gg = g_ref[...]
        y, r = _rms(x, gg, D_MODEL)
        hn = y.astype(BF16)
        hn_ref[...] = hn
        dy = dy_ref[...]
        dyb = dy.astype(BF16)
        dyb_ref[...] = dyb
        dh = jnp.zeros((t, D_MODEL), F32)
        for c in range(2):
            rs = slice(c * FF_HALF, (c + 1) * FF_HALF)
            gt = gate_ref[:, rs].astype(F32)
            u = up_ref[:, rs].astype(F32)
            sg = _sigmoid(gt)
            sl = gt * sg
            act_ref[:, rs] = (sl * u).astype(BF16)
            dact = _nt(dyb, w_ref[2, rs, :])
            dg = (dact * u * (sg * (1.0 + gt * (1.0 - sg)))).astype(BF16)
            du = (dact * sl).astype(BF16)
            dg_ref[:, rs] = dg
            du_ref[:, rs] = du
            dh = dh + _nn(dg, w_ref[0, rs, :]) + _nn(du, w_ref[1, rs, :])
        dxn, dgl = _rms_bwd(x, r, gg, dh, D_MODEL)
        dx_ref[...] = dy + dxn

        @pl.when(pl.program_id(0) == 0)
        def _():
            dln_ref[...] = jnp.zeros_like(dln_ref)

        dln_ref[...] += dgl

    hid = jax.ShapeDtypeStruct((s, D_FF), BF16)
    tok = jax.ShapeDtypeStruct((s, D_MODEL), BF16)
    return _pcall(
        body, (x, dy, gate, up, g, w), deps, name=name, grid=(s // t,),
        out_shape=[jax.ShapeDtypeStruct((s, D_MODEL), F32), hid, hid, hid, tok, tok,
                   jax.ShapeDtypeStruct((1, D_MODEL), F32)],
        in_specs=[_rows(t, D_MODEL), _rows(t, D_MODEL), _rows(t, D_FF), _rows(t, D_FF), _full((1, D_MODEL)), ANY],
        out_specs=[_rows(t, D_MODEL), _rows(t, D_FF), _rows(t, D_FF), _rows(t, D_FF),
                   _rows(t, D_MODEL), _rows(t, D_MODEL), _full((1, D_MODEL))],
        scratch_shapes=[pltpu.VMEM((3, D_FF, D_MODEL), BF16), pltpu.SemaphoreType.DMA],
        compiler_params=_cparams(1, VMEM_BIG),
    )


def _tn_matmul(a, b, into, p0, name, groups=1, m_chunk=None, deps=()):
    s = a.shape[0]
    m, n = a.shape[1] // groups, b.shape[1] // groups
    assert into.shape[1:] == (m, n)
    mc = m if m_chunk is None else m_chunk
    nm = m // mc
    t = min(1024, s)
    nt = s // t

    def body(a_ref, b_ref, into_ref, o_ref, acc):
        ti = pl.program_id(2)

        @pl.when(ti == 0)
        def _():
            acc[...] = jnp.zeros_like(acc)

        acc[...] += _tn(a_ref[...], b_ref[...])

        @pl.when(ti == nt - 1)
        def _():
            o_ref[...] = acc[...].astype(o_ref.dtype)

    return _pcall(
        body, (a, b, into), deps, name=name, grid=(groups, nm, nt),
        out_shape=jax.ShapeDtypeStruct(into.shape, into.dtype),
        in_specs=[pl.BlockSpec((t, mc), lambda gi, mi, ti: (ti, gi * nm + mi)),
                  pl.BlockSpec((t, n), lambda gi, mi, ti: (ti, gi)), ANY],
        out_specs=pl.BlockSpec((None, mc, n), lambda gi, mi, ti: (p0 + gi, mi, 0)),
        scratch_shapes=[pltpu.VMEM((mc, n), F32)],
        input_output_aliases={2: 0},
        compiler_params=_cparams(3, VMEM_BIG),
    )


def _rope_tables(positions):
    half = ROPE // 2
    inv = ROPE_THETA ** (-jnp.arange(half, dtype=F32) * 2.0 / ROPE)
    ang = positions.astype(F32)[:, None] * inv
    cos, sin = jnp.cos(ang), jnp.sin(ang)
    zero = jnp.zeros((positions.shape[0], LANES - ROPE), F32)
    return jnp.concatenate([cos, cos, zero], axis=1), jnp.concatenate([-sin, sin, zero], axis=1)


def _kv_specs(t):
    return [_full((1, D_MODEL)), _full((D_MODEL, KV_RANK)), _full((D_MODEL, LANES)), _full((1, KV_RANK)),
            _full((N_HEADS, KV_RANK, NOPE)), _full((N_HEADS, KV_RANK, V_DIM)),
            _full((1, NOPE)), _full((1, LANES)), _rows(t, LANES), _rows(t, LANES)]


def _kv_fwd(x, ln, wc, wpe, gl, wuk, wuv, gkn, gkr, cos, sin, name, deps=()):
    s = x.shape[0]
    t = min(PROJ_ROWS, s)

    def body(x_ref, ln_ref, wc_ref, wpe_ref, gl_ref, wuk_ref, wuv_ref, gkn_ref, gkr_ref, cos_ref, sin_ref,
             k_ref, v_ref):
        hn = _rms(x_ref[...], ln_ref[...], D_MODEL)[0].astype(BF16)
        clat = _nn(hn, wc_ref[...])
        kpe = _nn(hn, wpe_ref[...])
        cn = _rms(clat, gl_ref[...], KV_RANK)[0].astype(BF16)
        sspe = jnp.sum(kpe * kpe, axis=-1, keepdims=True)
        base = kpe * gkr_ref[...]
        rot = base * cos_ref[...] + _swap_halves(base, _swap_perm()) * sin_ref[...]
        for h in range(N_HEADS):
            kn = _nn(cn, wuk_ref[h])
            r = lax.rsqrt((jnp.sum(kn * kn, axis=-1, keepdims=True) + sspe) * (1.0 / QK_DIM) + EPS)
            k_ref[:, h * QK_PAD:h * QK_PAD + NOPE] = ((kn * r) * gkn_ref[...]).astype(BF16)
            k_ref[:, h * QK_PAD + NOPE:(h + 1) * QK_PAD] = (rot * r).astype(BF16)
            v_ref[:, h * V_DIM:(h + 1) * V_DIM] = _nn(cn, wuv_ref[h]).astype(BF16)

    return _pcall(
        body, (x, ln, wc, wpe, gl, wuk, wuv, gkn, gkr, cos, sin), deps, name=name, grid=(s // t,),
        out_shape=[jax.ShapeDtypeStruct((s, N_HEADS * QK_PAD), BF16), jax.ShapeDtypeStruct((s, N_HEADS * V_DIM), BF16)],
        in_specs=[_rows(t, D_MODEL)] + _kv_specs(t),
        out_specs=[_rows(t, N_HEADS * QK_PAD), _rows(t, N_HEADS * V_DIM)],
        compiler_params=_cparams(1, VMEM_MID),
    )


def _kv_bwd(x, dxin, dks, dvs, ln, wc, wpe, gl, wuk, wuv, gkn, gkr, cos, sin, name):
    s = x.shape[0]
    t = min(PROJ_ROWS, s)
    nk = len(dks)

    def body(*refs):
        x_ref, dxin_ref = refs[:2]
        dk_refs = refs[2:2 + nk]
        dv_refs = refs[2 + nk:2 + 2 * nk]
        (ln_ref, wc_ref, wpe_ref, gl_ref, wuk_ref, wuv_ref, gkn_ref, gkr_ref, cos_ref, sin_ref,
         dx_ref, hn_ref, cn_ref, dkn_ref, dvb_ref, dcc_ref, dpe_ref,
         dln_ref, dgl_ref, dgkn_ref, dgkr_ref) = refs[2 + 2 * nk:]
        x = x_ref[...]
        ln = ln_ref[...]
        y, rx = _rms(x, ln, D_MODEL)
        hn = y.astype(BF16)
        hn_ref[...] = hn
        clat = _nn(hn, wc_ref[...])
        kpe = _nn(hn, wpe_ref[...])
        gl = gl_ref[...]
        cy, rc = _rms(clat, gl, KV_RANK)
        cn = cy.astype(BF16)
        cn_ref[...] = cn
        sspe = jnp.sum(kpe * kpe, axis=-1, keepdims=True)
        cs, sn, perm = cos_ref[...], sin_ref[...], _swap_perm()
        gkn, gkr = gkn_ref[...], gkr_ref[...]
        base = kpe * gkr
        rot = base * cs + _swap_halves(base, perm) * sn
        dc = jnp.zeros((t, KV_RANK), F32)
        dkr_sum = jnp.zeros((t, LANES), F32)
        coef_sum = jnp.zeros((t, 1), F32)
        dgkn = jnp.zeros((1, NOPE), F32)
        for h in range(N_HEADS):
            kn = _nn(cn, wuk_ref[h])
            r = lax.rsqrt((jnp.sum(kn * kn, axis=-1, keepdims=True) + sspe) * (1.0 / QK_DIM) + EPS)
            lo, mid, hi = h * QK_PAD, h * QK_PAD + NOPE, (h + 1) * QK_PAD
            dko = dk_refs[0][:, lo:mid]
            dkr = dk_refs[0][:, mid:hi]
            dvh = dv_refs[0][:, h * V_DIM:(h + 1) * V_DIM]
            for j in range(1, nk):
                dko = dko + dk_refs[j][:, lo:mid]
                dkr = dkr + dk_refs[j][:, mid:hi]
                dvh = dvh + dv_refs[j][:, h * V_DIM:(h + 1) * V_DIM]
            un = dko * gkn
            sm = (jnp.sum(kn * un, axis=-1, keepdims=True) + jnp.sum(rot * dkr, axis=-1, keepdims=True)) * (1.0 / QK_DIM)
            coef = r * r * r * sm
            dkn = (r * un - kn * coef).astype(BF16)
            dkr_sum = dkr_sum + r * dkr
            coef_sum = coef_sum + coef
            dgkn = dgkn + jnp.sum(dko * (kn * r), axis=0, keepdims=True)
            dkn_ref[:, h * NOPE:(h + 1) * NOPE] = dkn
            dvb = dvh.astype(BF16)
            dvb_ref[:, h * V_DIM:(h + 1) * V_DIM] = dvb
            dc = dc + _nt(dkn, wuk_ref[h]) + _nt(dvb, wuv_ref[h])
        dz = dkr_sum * cs - _swap_halves(dkr_sum, perm) * sn
        dkpe = dz * gkr - kpe * coef_sum
        dgkr = jnp.sum(dz * kpe, axis=0, keepdims=True)
        dclat, dgl = _rms_bwd(clat, rc, gl, dc, KV_RANK)
        dcc = dclat.astype(BF16)
        dpe = dkpe.astype(BF16)
        dcc_ref[...] = dcc
        dpe_ref[...] = dpe
        dhn = _nt(dcc, wc_ref[...]) + _nt(dpe, wpe_ref[...])
        dxn, dln = _rms_bwd(x, rx, ln, dhn, D_MODEL)
        dx_ref[...] = dxin_ref[...] + dxn

        @pl.when(pl.program_id(0) == 0)
        def _():
            dln_ref[...] = jnp.zeros_like(dln_ref)
            dgl_ref[...] = jnp.zeros_like(dgl_ref)
            dgkn_ref[...] = jnp.zeros_like(dgkn_ref)
            dgkr_ref[...] = jnp.zeros_like(dgkr_ref)

        dln_ref[...] += dln
        dgl_ref[...] += dgl
        dgkn_ref[...] += dgkn
        dgkr_ref[...] += dgkr

    def tok(cols, dt):
        return jax.ShapeDtypeStruct((s, cols), dt)

    def vec(cols):
        return jax.ShapeDtypeStruct((1, cols), F32)

    return pl.pallas_call(
        body, name=name, grid=(s // t,),
        out_shape=[tok(D_MODEL, F32), tok(D_MODEL, BF16), tok(KV_RANK, BF16), tok(N_HEADS * NOPE, BF16),
                   tok(N_HEADS * V_DIM, BF16), tok(KV_RANK, BF16), tok(LANES, BF16),
                   vec(D_MODEL), vec(KV_RANK), vec(NOPE), vec(LANES)],
        in_specs=[_rows(t, D_MODEL), _rows(t, D_MODEL)] + [_rows(t, N_HEADS * QK_PAD)] * nk
                 + [_rows(t, N_HEADS * V_DIM)] * nk + _kv_specs(t),
        out_specs=[_rows(t, D_MODEL), _rows(t, D_MODEL), _rows(t, KV_RANK), _rows(t, N_HEADS * NOPE),
                   _rows(t, N_HEADS * V_DIM), _rows(t, KV_RANK), _rows(t, LANES),
                   _full((1, D_MODEL)), _full((1, KV_RANK)), _full((1, NOPE)), _full((1, LANES))],
        compiler_params=_cparams(1, VMEM_BIG),
    )(x, dxin, *dks, *dvs, ln, wc, wpe, gl, wuk, wuv, gkn, gkr, cos, sin)


def _q_specs(t):
    return [_full((1, D_MODEL)), _full((D_MODEL, Q_RANK)), _full((1, Q_RANK)), _full((N_HEADS, Q_RANK, QK_PAD)),
            _full((1, NOPE)), _full((1, LANES)), _rows(t, LANES), _rows(t, LANES)]


def _q_fwd(x, ln, wdq, gql, wuq, gqn, gqr, cos, sin, name, deps=()):
    s = x.shape[0]
    t = min(PROJ_ROWS, s)

    def body(x_ref, ln_ref, wdq_ref, gql_ref, wuq_ref, gqn_ref, gqr_ref, cos_ref, sin_ref, q_ref):
        hn = _rms(x_ref[...], ln_ref[...], D_MODEL)[0].astype(BF16)
        cqn = _rms(_nn(hn, wdq_ref[...]), gql_ref[...], Q_RANK)[0].astype(BF16)
        cs, sn, perm = cos_ref[...], sin_ref[...], _swap_perm()
        for h in range(N_HEADS):
            qa = _nn(cqn, wuq_ref[h])
            r = lax.rsqrt(jnp.sum(qa * qa, axis=-1, keepdims=True) * (1.0 / QK_DIM) + EPS)
            q_ref[:, h * QK_PAD:h * QK_PAD + NOPE] = ((qa[:, :NOPE] * r) * gqn_ref[...]).astype(BF16)
            z = (qa[:, NOPE:] * r) * gqr_ref[...]
            q_ref[:, h * QK_PAD + NOPE:(h + 1) * QK_PAD] = (z * cs + _swap_halves(z, perm) * sn).astype(BF16)

    return _pcall(
        body, (x, ln, wdq, gql, wuq, gqn, gqr, cos, sin), deps, name=name, grid=(s // t,),
        out_shape=jax.ShapeDtypeStruct((s, N_HEADS * QK_PAD), BF16),
        in_specs=[_rows(t, D_MODEL)] + _q_specs(t),
        out_specs=_rows(t, N_HEADS * QK_PAD),
        compiler_params=_cparams(1, VMEM_MID),
    )


def _q_bwd(x, dxin, dq, ln, wdq, gql, wuq, gqn, gqr, cos, sin, name):
    s = x.shape[0]
    t = min(PROJ_ROWS, s)

    def body(x_ref, dxin_ref, dq_ref, ln_ref, wdq_ref, gql_ref, wuq_ref, gqn_ref, gqr_ref, cos_ref, sin_ref,
             dx_ref, hn_ref, cqn_ref, dqa_ref, dcq_ref, dln_ref, dgql_ref, dgqn_ref, dgqr_ref):
        x = x_ref[...]
        ln = ln_ref[...]
        y, rx = _rms(x, ln, D_MODEL)
        hn = y.astype(BF16)
        hn_ref[...] = hn
        cqp = _nn(hn, wdq_ref[...])
        gql = gql_ref[...]
        cy, rc = _rms(cqp, gql, Q_RANK)
        cqn = cy.astype(BF16)
        cqn_ref[...] = cqn
        cs, sn, perm = cos_ref[...], sin_ref[...], _swap_perm()
        gqn, gqr = gqn_ref[...], gqr_ref[...]
        dcq = jnp.zeros((t, Q_RANK), F32)
        dgqn = jnp.zeros((1, NOPE), F32)
        dgqr = jnp.zeros((1, LANES), F32)
        for h in range(N_HEADS):
            qa = _nn(cqn, wuq_ref[h])
            qn, qr = qa[:, :NOPE], qa[:, NOPE:]
            r = lax.rsqrt(jnp.sum(qa * qa, axis=-1, keepdims=True) * (1.0 / QK_DIM) + EPS)
            dqo = dq_ref[:, h * QK_PAD:h * QK_PAD + NOPE]
            dqr = dq_ref[:, h * QK_PAD + NOPE:(h + 1) * QK_PAD]
            dz = dqr * cs - _swap_halves(dqr, perm) * sn
            un = dqo * gqn
            ur = dz * gqr
            sm = (jnp.sum(qn * un, axis=-1, keepdims=True) + jnp.sum(qr * ur, axis=-1, keepdims=True)) * (1.0 / QK_DIM)
            coef = r * r * r * sm
            dqa = jnp.concatenate([r * un - qn * coef, r * ur - qr * coef], axis=1).astype(BF16)
            dgqn = dgqn + jnp.sum(dqo * (qn * r), axis=0, keepdims=True)
            dgqr = dgqr + jnp.sum(dz * (qr * r), axis=0, keepdims=True)
            dqa_ref[:, h * QK_PAD:(h + 1) * QK_PAD] = dqa
            dcq = dcq + _nt(dqa, wuq_ref[h])
        dcqp, dgql = _rms_bwd(cqp, rc, gql, dcq, Q_RANK)
        dcqb = dcqp.astype(BF16)
        dcq_ref[...] = dcqb
        dhn = _nt(dcqb, wdq_ref[...])
        dxn, dln = _rms_bwd(x, rx, ln, dhn, D_MODEL)
        dx_ref[...] = dxin_ref[...] + dxn

        @pl.when(pl.program_id(0) == 0)
        def _():
            dln_ref[...] = jnp.zeros_like(dln_ref)
            dgql_ref[...] = jnp.zeros_like(dgql_ref)
            dgqn_ref[...] = jnp.zeros_like(dgqn_ref)
            dgqr_ref[...] = jnp.zeros_like(dgqr_ref)

        dln_ref[...] += dln
        dgql_ref[...] += dgql
        dgqn_ref[...] += dgqn
        dgqr_ref[...] += dgqr

    def tok(cols, dt):
        return jax.ShapeDtypeStruct((s, cols), dt)

    def vec(cols):
        return jax.ShapeDtypeStruct((1, cols), F32)

    return pl.pallas_call(
        body, name=name, grid=(s // t,),
        out_shape=[tok(D_MODEL, F32), tok(D_MODEL, BF16), tok(Q_RANK, BF16), tok(N_HEADS * QK_PAD, BF16),
                   tok(Q_RANK, BF16), vec(D_MODEL), vec(Q_RANK), vec(NOPE), vec(LANES)],
        in_specs=[_rows(t, D_MODEL), _rows(t, D_MODEL), _rows(t, N_HEADS * QK_PAD)] + _q_specs(t),
        out_specs=[_rows(t, D_MODEL), _rows(t, D_MODEL), _rows(t, Q_RANK), _rows(t, N_HEADS * QK_PAD),
                   _rows(t, Q_RANK), _full((1, D_MODEL)), _full((1, Q_RANK)), _full((1, NOPE)), _full((1, LANES))],
        compiler_params=_cparams(1, VMEM_MID),
    )(x, dxin, dq, ln, wdq, gql, wuq, gqn, gqr, cos, sin)


SM_SCALE = 1.0 / math.sqrt(QK_DIM)
LOG2_E = math.log2(math.e)
EXP2_SCALE = SM_SCALE * LOG2_E
NEG = -1e30


def _diag_mask(t):
    qpos = lax.broadcasted_iota(jnp.int32, (t, t), 0)
    kpos = lax.broadcasted_iota(jnp.int32, (t, t), 1)
    return lax.shift_right_logical(kpos, 6) <= lax.shift_right_logical(qpos, 6)


def _att_fwd(q, k, v, name):
    s = q.shape[0]
    t = min(512, s)
    nb = s // t

    def body(q_ref, k_ref, v_ref, o_ref, lse_ref):
        qi = pl.program_id(1)
        qq = q_ref[...]

        def block(ki, carry, masked):
            m_old, l_old, acc = carry
            rows = pl.ds(pl.multiple_of(ki * t, t), t)
            sc = _nt(qq, k_ref[rows, :])
            if masked:
                sc = jnp.where(_diag_mask(t), sc, NEG)
            m_new = jnp.maximum(m_old, jnp.max(sc, axis=-1, keepdims=True))
            p = jnp.exp2((sc - m_new) * EXP2_SCALE)
            alpha = jnp.exp2((m_old - m_new) * EXP2_SCALE)
            l_new = alpha * l_old + jnp.sum(p, axis=-1, keepdims=True)
            acc = alpha * acc + _nn(p.astype(BF16), v_ref[rows, :])
            return m_new, l_new, acc

        init = (jnp.full((t, 1), NEG, F32), jnp.zeros((t, 1), F32), jnp.zeros((t, V_DIM), F32))
        carry = lax.fori_loop(0, qi // 2, lambda j, c: block(2 * j + 1, block(2 * j, c, False), False), init)
        carry = lax.cond(qi % 2 == 1, lambda c: block(qi - 1, c, False), lambda c: c, carry)
        m_fin, l_fin, acc = block(qi, carry, True)
        o_ref[...] = (acc / l_fin).astype(BF16)
        lse_ref[...] = jnp.broadcast_to(m_fin * SM_SCALE + jnp.log(l_fin), (t, LANES))

    return pl.pallas_call(
        body, name=name, grid=(N_HEADS, nb),
        out_shape=[jax.ShapeDtypeStruct((s, N_HEADS * V_DIM), BF16), jax.ShapeDtypeStruct((s, N_HEADS * LANES), F32)],
        in_specs=[pl.BlockSpec((t, QK_PAD), lambda h, qi: (qi, h)),
                  pl.BlockSpec((s, QK_PAD), lambda h, qi: (0, h)),
                  pl.BlockSpec((s, V_DIM), lambda h, qi: (0, h))],
        out_specs=[pl.BlockSpec((t, V_DIM), lambda h, qi: (qi, h)),
                   pl.BlockSpec((t, LANES), lambda h, qi: (qi, h))],
        compiler_params=_cparams(2, VMEM_MID),
    )(q, k, v)


def _att_bwd(q, k, v, do, o, lse, name, deps=()):
    s = q.shape[0]
    t = min(512, s)
    nb = s // t

    def body(q_ref, k_ref, v_ref, do_ref, o_ref, lse_ref, dq_ref, dk_ref, dv_ref):
        ki = pl.program_id(1)
        kk, vv = k_ref[...], v_ref[...]

        @pl.when(ki == 0)
        def _():
            dq_ref[...] = jnp.zeros_like(dq_ref)

        def block(qi, carry, masked):
            dk, dv = carry
            rows = pl.ds(pl.multiple_of(qi * t, t), t)
            qq, dob = q_ref[rows, :], do_ref[rows, :]
            sc = _nt(qq, kk)
            if masked:
                sc = jnp.where(_diag_mask(t), sc, NEG)
            p = jnp.exp2(sc * EXP2_SCALE - lse_ref[rows, :][:, :1] * LOG2_E)
            dp = _nt(dob, vv)
            dsum = jnp.sum(dob.astype(F32) * o_ref[rows, :].astype(F32), axis=-1, keepdims=True)
            ds = (p * (dp - dsum)).astype(BF16)
            dq_ref[rows, :] += _nn(ds, kk)
            return dk + _tn(ds, qq), dv + _tn(p.astype(BF16), dob)

        carry = block(ki, (jnp.zeros((t, QK_PAD), F32), jnp.zeros((t, V_DIM), F32)), True)
        rest = nb - 1 - ki
        carry = lax.fori_loop(
            0, rest // 2, lambda j, c: block(ki + 2 * j + 2, block(ki + 2 * j + 1, c, False), False), carry)
        dk, dv = lax.cond(rest % 2 == 1, lambda c: block(nb - 1, c, False), lambda c: c, carry)
        dk_ref[...] = dk * SM_SCALE
        dv_ref[...] = dv

        @pl.when(ki == nb - 1)
        def _():
            dq_ref[...] = dq_ref[...] * SM_SCALE

    def head(h, ki):
        return (0, h)

    def kblock(h, ki):
        return (ki, h)

    return _pcall(
        body, (q, k, v, do, o, lse), deps, name=name, grid=(N_HEADS, nb),
        out_shape=[jax.ShapeDtypeStruct((s, N_HEADS * QK_PAD), F32), jax.ShapeDtypeStruct((s, N_HEADS * QK_PAD), F32),
                   jax.ShapeDtypeStruct((s, N_HEADS * V_DIM), F32)],
        in_specs=[pl.BlockSpec((s, QK_PAD), head), pl.BlockSpec((t, QK_PAD), kblock), pl.BlockSpec((t, V_DIM), kblock),
                  pl.BlockSpec((s, V_DIM), head), pl.BlockSpec((s, V_DIM), head), pl.BlockSpec((s, LANES), head)],
        out_specs=[pl.BlockSpec((s, QK_PAD), head), pl.BlockSpec((t, QK_PAD), kblock), pl.BlockSpec((t, V_DIM), kblock)],
        compiler_params=_cparams(2, VMEM_MID),
    )


def _o_fwd(x, o, wo, name):
    s = x.shape[0]
    t = min(512, s)

    def body(x_ref, o_ref, wo_ref, xo_ref):
        xo_ref[...] = x_ref[...] + _nn(o_ref[...], wo_ref[...])

    return pl.pallas_call(
        body, name=name, grid=(s // t,),
        out_shape=jax.ShapeDtypeStruct((s, D_MODEL), F32),
        in_specs=[_rows(t, D_MODEL), _rows(t, D_MODEL), _full((D_MODEL, D_MODEL))],
        out_specs=_rows(t, D_MODEL),
        compiler_params=_cparams(1, VMEM_MID),
    )(x, o, wo)


def _o_bwd(dx, wo, name, deps=()):
    s = dx.shape[0]
    t = min(512, s)

    def body(dx_ref, wo_ref, do_ref, dxb_ref):
        dxb = dx_ref[...].astype(BF16)
        dxb_ref[...] = dxb
        do_ref[...] = _nt(dxb, wo_ref[...]).astype(BF16)

    tok = jax.ShapeDtypeStruct((s, D_MODEL), BF16)
    return _pcall(
        body, (dx, wo), deps, name=name, grid=(s // t,),
        out_shape=[tok, tok],
        in_specs=[_rows(t, D_MODEL), _full((D_MODEL, D_MODEL))],
        out_specs=[_rows(t, D_MODEL), _rows(t, D_MODEL)],
        compiler_params=_cparams(1, VMEM_MID),
    )


def _loss_head(y, target, name):
    s = y.shape[0]
    t = min(512, s)

    def body(y_ref, t_ref, dy_ref, sq_ref):
        e = y_ref[...] - t_ref[...]
        dy_ref[...] = e * (1.0 / D_MODEL)

        @pl.when(pl.program_id(0) == 0)
        def _():
            sq_ref[...] = jnp.zeros_like(sq_ref)

        sq_ref[...] += jnp.sum(e * e, axis=0, keepdims=True)

    return pl.pallas_call(
        body, name=name, grid=(s // t,),
        out_shape=[jax.ShapeDtypeStruct((s, D_MODEL), F32), jax.ShapeDtypeStruct((1, D_MODEL), F32)],
        in_specs=[_rows(t, D_MODEL), _rows(t, D_MODEL)],
        out_specs=[_rows(t, D_MODEL), _full((1, D_MODEL))],
        compiler_params=_cparams(1),
    )(y, target)


def _adamw(w, g, m, v, name):
    shape = w.shape
    c = shape[-1]
    r = math.prod(shape[:-1])
    tb = r
    for cand in (512, 256, 128):
        if r % cand == 0 and r > cand:
            tb = cand
            break

    def body(w_ref, g_ref, m_ref, v_ref, d_ref, mo_ref, vo_ref):
        gr = g_ref[...]
        mn = ADAM_B1 * m_ref[...] + (1.0 - ADAM_B1) * gr
        vn = ADAM_B2 * v_ref[...] + (1.0 - ADAM_B2) * (gr * gr)
        m_hat = mn / (1.0 - ADAM_B1 ** ADAM_STEP)
        v_hat = vn / (1.0 - ADAM_B2 ** ADAM_STEP)
        d_ref[...] = -ADAM_LR * (m_hat / (jnp.sqrt(v_hat) + ADAM_EPS) + ADAM_WD * w_ref[...])
        mo_ref[...] = mn
        vo_ref[...] = vn

    spec = pl.BlockSpec((tb, c), lambda i: (i, 0))
    flat = jax.ShapeDtypeStruct((r, c), F32)
    outs = pl.pallas_call(
        body, name=name, grid=(r // tb,),
        out_shape=[flat, flat, flat],
        in_specs=[spec] * 4, out_specs=[spec] * 3,
        compiler_params=_cparams(1),
    )(w.reshape(r, c), g.reshape(r, c), m.reshape(r, c), v.reshape(r, c))
    return [a.reshape(shape) for a in outs]


def _pad_cols(a, width):
    return jnp.pad(a, [(0, 0)] * (a.ndim - 1) + [(0, width - a.shape[-1])])


def _owner_view(a, sz):
    return a.reshape(a.shape[0], N_CHIPS, 2, sz, a.shape[-1])


def kernel(x, positions, ln_mix_a, w_pool, b_pool, pool_scale, ln_ffn, w_gate, w_up, w_down, ln_kv, w_dkv, g_kv_latent, w_uk, w_uv, g_k, ln_mix_b, w_dq, g_q_latent, w_uq, g_q, w_o, loss_target, m_ln_mix_a, m_w_pool, m_b_pool, m_pool_scale, m_ln_ffn, m_w_gate, m_w_up, m_w_down, m_ln_kv, m_w_dkv, m_g_kv_latent, m_w_uk, m_w_uv, m_g_k, m_ln_mix_b, m_w_dq, m_g_q_latent, m_w_uq, m_g_q, m_w_o, v_ln_mix_a, v_w_pool, v_b_pool, v_pool_scale, v_ln_ffn, v_w_gate, v_w_up, v_w_down, v_ln_kv, v_w_dkv, v_g_kv_latent, v_w_uk, v_w_uv, v_g_k, v_ln_mix_b, v_w_dq, v_g_q_latent, v_w_uq, v_g_q, v_w_o):
    weights = dict(ln_mix_a=ln_mix_a, w_pool=w_pool, b_pool=b_pool, pool_scale=pool_scale, ln_ffn=ln_ffn,
                   w_gate=w_gate, w_up=w_up, w_down=w_down, ln_kv=ln_kv, w_dkv=w_dkv, g_kv_latent=g_kv_latent,
                   w_uk=w_uk, w_uv=w_uv, g_k=g_k, ln_mix_b=ln_mix_b, w_dq=w_dq, g_q_latent=g_q_latent,
                   w_uq=w_uq, g_q=g_q, w_o=w_o)
    mom1 = dict(ln_mix_a=m_ln_mix_a, w_pool=m_w_pool, b_pool=m_b_pool, pool_scale=m_pool_scale, ln_ffn=m_ln_ffn,
                w_gate=m_w_gate, w_up=m_w_up, w_down=m_w_down, ln_kv=m_ln_kv, w_dkv=m_w_dkv,
                g_kv_latent=m_g_kv_latent, w_uk=m_w_uk, w_uv=m_w_uv, g_k=m_g_k, ln_mix_b=m_ln_mix_b, w_dq=m_w_dq,
                g_q_latent=m_g_q_latent, w_uq=m_w_uq, g_q=m_g_q, w_o=m_w_o)
    mom2 = dict(ln_mix_a=v_ln_mix_a, w_pool=v_w_pool, b_pool=v_b_pool, pool_scale=v_pool_scale, ln_ffn=v_ln_ffn,
                w_gate=v_w_gate, w_up=v_w_up, w_down=v_w_down, ln_kv=v_ln_kv, w_dkv=v_w_dkv,
                g_kv_latent=v_g_kv_latent, w_uk=v_w_uk, w_uv=v_w_uv, g_k=v_g_k, ln_mix_b=v_ln_mix_b, w_dq=v_w_dq,
                g_q_latent=v_g_q_latent, w_uq=v_w_uq, g_q=v_g_q, w_o=v_w_o)
    names = list(weights)
    dev = 4 * lax.axis_index("x") + 2 * lax.axis_index("y") + lax.axis_index("c")
    core = lax.axis_index("c").astype(jnp.int32).reshape(1)
    chip = (2 * lax.axis_index("x") + lax.axis_index("y")).astype(jnp.int32).reshape(1)

    xs = x[0]
    target = loss_target[0]
    cos, sin = _rope_tables(positions[0])

    def placed(shard):
        buf = lax.empty((shard.shape[0], N_DEV) + shard.shape[1:], shard.dtype)
        return lax.dynamic_update_slice(buf, shard[:, None], (0, dev, 0, 0))

    groups = {f"ffn{l}": [placed(jnp.stack([w_gate[l].T, w_up[l].T, w_down[l]]).astype(BF16))] for l in range(4)}
    groups["att"] = [placed(a.astype(BF16)) for a in (
        w_dkv[None, :, :KV_RANK], _pad_cols(w_dkv[None, :, KV_RANK:], LANES), w_uk[None], w_uv[None],
        w_dq, _pad_cols(w_uq, QK_PAD), w_o)]
    small_sh = jnp.concatenate([ln_mix_a.reshape(1, -1), pool_scale.reshape(1, -1), b_pool.reshape(1, -1)], axis=1)
    wp_g, small_g = _all_gather([w_pool.astype(BF16), small_sh], [2, 0], "gather_first")
    wp_all = wp_g.reshape(2, 4, GROUP_DIM, GROUP_DIM)
    small_g = small_g.reshape(N_DEV, 3, 2, LANES)
    ln_a_all = small_g[:, 0].transpose(1, 0, 2).reshape(2, 1, D_MODEL)
    sc_all = small_g[:, 1].transpose(1, 0, 2).reshape(2, 1, D_MODEL)
    bp_all = small_g[:, 2].reshape(N_DEV, 2, 4, 32).transpose(1, 2, 0, 3).reshape(2, 1, D_MODEL)
    sp0 = _copies_start(groups["ffn0"], 1, _gather_spread, "spread_ffn0", deps=[small_g])

    def spread_start(nm, deps):
        return _copies_start(groups[nm], len(groups[nm]), _gather_spread, f"spread_{nm}", deps=deps)

    def spread_wait(nm, state, after):
        ssem, rsem, bufs, _ = state
        return _copies_wait(bufs, ssem, rsem, after, _blocks_moved(4), f"spread_done_{nm}")

    def relay_start(nm, bufs, deps=()):
        return _copies_start(bufs, len(bufs), _gather_relay, f"relay_{nm}", deps=deps)

    def relay_wait(nm, state, after):
        ssem, rsem, bufs, _ = state
        return _copies_wait(bufs, ssem, rsem, after, _blocks_moved(3), f"relay_done_{nm}")

    gkn = g_k[:NOPE].reshape(1, NOPE)
    gkr = _pad_cols(g_k[NOPE:].reshape(1, ROPE), LANES)
    gl = g_kv_latent.reshape(1, KV_RANK)
    lnkv = ln_kv.reshape(1, D_MODEL)

    x_in, x_mid, pooled, gates, ups, w_ffn = [], [], [], [], [], []
    qs, outs, lses = [], [], []

    def mixer(l, cur, deps):
        x_in.append(cur)
        mid, dsave = _mix_fwd(cur, ln_a_all[l], wp_all[l], bp_all[l], sc_all[l], f"mix_fwd{l}", deps=deps)
        pooled.append(dsave)
        x_mid.append(mid)
        return mid

    def q_args(j):
        return (ln_mix_b[j].reshape(1, -1), wdq_all[j], g_q_latent[j].reshape(1, -1), wuq_all[j],
                g_q[j, :NOPE].reshape(1, -1), _pad_cols(g_q[j, NOPE:].reshape(1, -1), LANES), cos, sin)

    def attention(j, cur, deps):
        x_in.append(cur)
        q = _q_fwd(cur, *q_args(j), f"q_fwd{j}", deps=deps)
        o, lse = _att_fwd(q, k_sh, v_sh, f"att_fwd{j}")
        mid = _o_fwd(cur, o, wo_all[j], f"o_fwd{j}")
        qs.append(q)
        outs.append(o)
        lses.append(lse)
        x_mid.append(mid)
        return mid

    def ffn(l, mid, relayed):
        w_l = relayed[0].reshape(3, D_FF, D_MODEL)
        w_ffn.append(w_l)
        cur, gate, up = _ffn_fwd(mid, ln_ffn[l].reshape(1, -1), w_l, f"ffn_fwd{l}")
        gates.append(gate)
        ups.append(up)
        return cur

    mid = mixer(0, xs, [sp0[3]])
    landed0 = spread_wait("ffn0", sp0, mid)
    sp1 = spread_start("ffn1", [landed0[0]])
    rl0 = relay_start("ffn0", landed0, [sp1[3]])
    cur = ffn(0, mid, relay_wait("ffn0", rl0, rl0[3]))

    landed1 = spread_wait("ffn1", sp1, cur)
    sp_att = spread_start("att", [landed1[0]])
    sp2 = spread_start("ffn2", [landed1[0]])
    rl1 = relay_start("ffn1", landed1, [sp_att[3], sp2[3]])
    mid = mixer(1, cur, [rl1[3]])
    cur = ffn(1, mid, relay_wait("ffn1", rl1, mid))
    x_kv = cur

    landed_att = spread_wait("att", sp_att, cur)
    landed2 = spread_wait("ffn2", sp2, cur)
    sp3 = spread_start("ffn3", [landed2[0]])
    rl_att = relay_start("att", landed_att, [sp3[3]])
    rl2 = relay_start("ffn2", landed2, [sp3[3]])
    att_bufs = relay_wait("att", rl_att, rl2[3])
    wc = att_bufs[0].reshape(D_MODEL, KV_RANK)
    wpe = att_bufs[1].reshape(D_MODEL, LANES)
    wuk_g = att_bufs[2].reshape(N_HEADS, KV_RANK, NOPE)
    wuv_g = att_bufs[3].reshape(N_HEADS, KV_RANK, V_DIM)
    wdq_all = att_bufs[4].reshape(2, D_MODEL, Q_RANK)
    wuq_all = att_bufs[5]
    wo_all = att_bufs[6].reshape(2, D_MODEL, D_MODEL)
    k_sh, v_sh = _kv_fwd(cur, lnkv, wc, wpe, gl, wuk_g, wuv_g, gkn, gkr, cos, sin, "kv_fwd")
    mid = attention(0, cur, [])
    cur = ffn(2, mid, relay_wait("ffn2", rl2, mid))

    landed3 = spread_wait("ffn3", sp3, cur)
    rl3 = relay_start("ffn3", landed3)
    mid = attention(1, cur, [rl3[3]])
    cur = ffn(3, mid, relay_wait("ffn3", rl3, mid))

    dx, sq_cols = _loss_head(cur, target, "loss_head")

    small = {}
    sizes = dict(ffn0=FF_SHARD, ffn1=FF_SHARD, ffn2=FF_SHARD, ffn3=FF_SHARD, wo=128, kv512=128, dkv_pe=128,
                 wdq=128, wuqT=QK_PAD, wpool=32)
    big = dict(wo=lax.empty((2, D_MODEL, D_MODEL), BF16), kv512=lax.empty((3, D_MODEL, KV_RANK), BF16),
               dkv_pe=lax.empty((1, D_MODEL, LANES), BF16), wdq=lax.empty((2, D_MODEL, Q_RANK), BF16),
               wuqT=lax.empty((2, N_HEADS * QK_PAD, Q_RANK), BF16), wpool=lax.empty((8, GROUP_DIM, GROUP_DIM), BF16))
    for l in range(4):
        big[f"ffn{l}"] = lax.empty((3, D_FF, D_MODEL), BF16)
    red = {}

    def pair_start(nms, tag):
        arrs = []
        for nm in nms:
            view = _owner_view(big[nm], sizes[nm])
            arrs += [view, lax.empty((view.shape[0], N_CHIPS) + view.shape[3:], BF16)]
        return nms, tag, _copies_start(arrs, len(nms), _pair_send, f"pair_start_{tag}")

    def chip_start(state, after):
        nms, tag, (ssem, rsem, arrs, _) = state
        arrs = _copies_wait(arrs, ssem, rsem, after, _landed, f"pair_done_{tag}")
        out = []
        for t, nm in enumerate(nms):
            part = _pair_sum(arrs[2 * t], arrs[2 * t + 1], core, f"pair_sum_{nm}")
            out += [part, lax.empty((3, part.shape[0]) + part.shape[2:], BF16)]
        return nms, tag, _copies_start(out, len(nms), _chip_send, f"chip_start_{tag}")

    deferred = []

    def chip_finish(state, after, defer=False):
        nms, tag, (ssem, rsem, arrs, _) = state
        arrs = _copies_wait(arrs, ssem, rsem, after, _landed, f"chip_done_{tag}")
        for t, nm in enumerate(nms):
            if defer:
                deferred.append((nm, arrs[2 * t], arrs[2 * t + 1]))
            else:
                red[nm] = _chip_sum(arrs[2 * t], arrs[2 * t + 1], chip, f"chip_sum_{nm}")

    ffn_grads = {nm: lax.empty((4, FF_SHARD, D_MODEL), F32) for nm in ("w_gate", "w_up", "w_down")}

    def place_ffn_grads(l):
        g = red[f"ffn{l}"]
        for k, nm in enumerate(("w_gate", "w_up", "w_down")):
            ffn_grads[nm] = ffn_grads[nm].at[l].set(g[k])

    dks, dvs = [], []
    pending = None
    bwd_deps = []
    for l in (3, 2, 1, 0):
        key = f"ffn{l}"
        dx, act, dgb, dub, hn, dyb, dln = _ffn_bwd(x_mid[l], dx, gates[l], ups[l], ln_ffn[l].reshape(1, -1),
                                                     w_ffn[l], f"ffn_bwd{l}", deps=bwd_deps)
        bwd_deps = []
        small[f"ln_ffn{l}"] = dln
        if l == 1:
            att_chip = chip_start(att_pair, dx)
            tn_deps = [att_chip[2][3]]
        else:
            tn_deps = []
        if pending:
            chip_finish(pending, dx, defer=True)
            pending = None
        big[key] = _tn_matmul(dgb, hn, big[key], 0, f"dw_gate{l}", m_chunk=FF_HALF, deps=tn_deps)
        big[key] = _tn_matmul(dub, hn, big[key], 1, f"dw_up{l}", m_chunk=FF_HALF)
        big[key] = _tn_matmul(act, dyb, big[key], 2, f"dw_down{l}", m_chunk=FF_HALF)
        if l == 1:
            chip_finish(att_chip, big[key])
        ffn_pair = pair_start([key], key)
        if l >= 2:
            j = l - 2
            do, dxb = _o_bwd(dx, wo_all[j], f"o_bwd{j}", deps=[ffn_pair[2][3]])
            big["wo"] = _tn_matmul(outs[j], dxb, big["wo"], j, f"dw_o{j}")
            ffn_chip = chip_start(ffn_pair, big["wo"])
            dq, dk, dv = _att_bwd(qs[j], k_sh, v_sh, do, outs[j], lses[j], f"att_bwd{j}", deps=[ffn_chip[2][3]])
            chip_finish(ffn_chip, dq, defer=True)
            dks.append(dk)
            dvs.append(dv)
            dx, hnq, cqn, dqa, dcq, dln, dgql, dgqn, dgqr = _q_bwd(x_in[l], dx, dq, *q_args(j), f"q_bwd{j}")
            small[f"ln_mix_b{j}"] = dln
            small[f"g_q_latent{j}"] = dgql
            small[f"g_q{j}"] = jnp.concatenate([dgqn, dgqr[:, :ROPE]], axis=1)
            big["wdq"] = _tn_matmul(hnq, dcq, big["wdq"], j, f"dw_dq{j}")
            big["wuqT"] = _tn_matmul(dqa, cqn, big["wuqT"], j, f"dw_uq{j}")
            if l == 2:
                (dx, hnk, cn, dknb, dvb, dccb, dpeb, dlnkv, dgl, dgkn, dgkr) = _kv_bwd(
                    x_kv, dx, dks, dvs, lnkv, wc, wpe, gl, wuk_g, wuv_g, gkn, gkr, cos, sin, "kv_bwd")
                small["ln_kv"] = dlnkv
                small["g_kv_latent"] = dgl
                small["g_k"] = jnp.concatenate([dgkn, dgkr[:, :ROPE]], axis=1)
                big["kv512"] = _tn_matmul(dknb, cn, big["kv512"], 0, "dw_uk")
                big["kv512"] = _tn_matmul(dvb, cn, big["kv512"], 1, "dw_uv")
                big["kv512"] = _tn_matmul(hnk, dccb, big["kv512"], 2, "dw_dkv_c")
                big["dkv_pe"] = _tn_matmul(hnk, dpeb, big["dkv_pe"], 0, "dw_dkv_pe")
                att_pair = pair_start(["wo", "kv512", "dkv_pe", "wdq", "wuqT"], "att")
                bwd_deps = [att_pair[2][3]]
        else:
            dx, dyp, dsc, db, dln = _mix_bwd(x_in[l], dx, pooled[l], ln_a_all[l], wp_all[l], bp_all[l], sc_all[l],
                                             f"mix_bwd{l}", deps=[ffn_pair[2][3]])
            small[f"ln_mix_a{l}"] = dln
            small[f"pool_scale{l}"] = dsc
            small[f"b_pool{l}"] = db
            ffn_chip = chip_start(ffn_pair, dx)
            big["wpool"] = _tn_matmul(pooled[l], dyp, big["wpool"], 4 * l, f"dw_pool{l}", groups=4,
                                      deps=[ffn_chip[2][3]])
            if l == 1:
                pending = ffn_chip
                bwd_deps = [ffn_chip[2][3]]
            else:
                for nm, part, land in deferred:
                    red[nm] = _chip_sum(part, land, chip, f"chip_sum_{nm}", deps=[ffn_chip[2][3]])
                    place_ffn_grads(int(nm[-1]))
                chip_finish(ffn_chip, [big["wpool"]] + list(ffn_grads.values()))
                place_ffn_grads(0)
    grad_x = dx[None]
    pool_pair = pair_start(["wpool"], "wpool")
    pool_chip = chip_start(pool_pair, pool_pair[2][3])
    chip_finish(pool_chip, pool_chip[2][3])

    vec_names = (["loss"] + [f"ln_ffn{l}" for l in range(4)] + ["ln_kv", "g_kv_latent", "g_k"]
                 + [f"{p}{j}" for p in ("ln_mix_b", "g_q_latent", "g_q") for j in range(2)]
                 + [f"{p}{l}" for p in ("ln_mix_a", "pool_scale", "b_pool") for l in range(2)])
    small["loss"] = sq_cols
    widths = [small[nm].shape[1] for nm in vec_names]
    padded = [-(-w // LANES) * LANES for w in widths]
    packed = jnp.concatenate([_pad_cols(small[nm], pw) for nm, pw in zip(vec_names, padded)], axis=1)
    (all_vecs,) = _all_gather([packed], [0], "gather_vectors")
    total = _sum_lead(all_vecs, "sum_vectors")
    vec = {}
    off = 0
    for nm, w, pw in zip(vec_names, widths, padded):
        vec[nm] = total[0, off:off + w]
        off += pw
    loss = 0.5 * jnp.sum(vec["loss"]) * (1.0 / D_MODEL)

    def own_cols(full, width):
        return lax.dynamic_slice_in_dim(full, dev * width, width, axis=full.ndim - 1)

    grads = dict(
        ln_mix_a=own_cols(jnp.stack([vec["ln_mix_a0"], vec["ln_mix_a1"]]), LANES),
        w_pool=red["wpool"].reshape(2, 4, 32, GROUP_DIM),
        b_pool=own_cols(jnp.stack([vec["b_pool0"], vec["b_pool1"]]).reshape(2, 4, GROUP_DIM), 32),
        pool_scale=own_cols(jnp.stack([vec["pool_scale0"], vec["pool_scale1"]]), LANES),
        ln_ffn=jnp.stack([vec[f"ln_ffn{l}"] for l in range(4)]),
        w_gate=ffn_grads["w_gate"],
        w_up=ffn_grads["w_up"],
        w_down=ffn_grads["w_down"],
        ln_kv=vec["ln_kv"],
        w_dkv=jnp.concatenate([red["kv512"][2], red["dkv_pe"][0][:, :ROPE]], axis=1),
        g_kv_latent=vec["g_kv_latent"],
        w_uk=red["kv512"][0].T,
        w_uv=red["kv512"][1].T,
        g_k=vec["g_k"],
        ln_mix_b=jnp.stack([vec["ln_mix_b0"], vec["ln_mix_b1"]]),
        w_dq=red["wdq"],
        g_q_latent=jnp.stack([vec["g_q_latent0"], vec["g_q_latent1"]]),
        w_uq=red["wuqT"].transpose(0, 2, 1)[:, :, :QK_DIM],
        g_q=jnp.stack([vec["g_q0"], vec["g_q1"]]),
        w_o=red["wo"],
    )

    deltas, new_m, new_v = {}, {}, {}
    for nm in names:
        w = weights[nm]
        if nm in ("w_gate", "w_up"):
            def swap(a):
                return a.transpose(0, 2, 1)
            d, mo, vo = _adamw(swap(w), grads[nm], swap(mom1[nm]), swap(mom2[nm]), f"adamw_{nm}")
            deltas[nm], new_m[nm], new_v[nm], grads[nm] = swap(d), swap(mo), swap(vo), swap(grads[nm])
            continue
        shape = w.shape if w.ndim > 1 else (1, w.shape[0])
        d, mo, vo = _adamw(w.reshape(shape), grads[nm].reshape(shape), mom1[nm].reshape(shape),
                           mom2[nm].reshape(shape), f"adamw_{nm}")
        deltas[nm], new_m[nm], new_v[nm] = d.reshape(w.shape), mo.reshape(w.shape), vo.reshape(w.shape)

    return (loss, grad_x, *[grads[nm].reshape(weights[nm].shape) for nm in names], *[deltas[nm] for nm in names],
            *[new_m[nm] for nm in names], *[new_v[nm] for nm in names])
```

```python
import functools
import math

import jax
import jax.numpy as jnp
from jax import lax
from jax.experimental import pallas as pl
from jax.experimental.pallas import tpu as pltpu

F32 = jnp.float32
BF16 = jnp.bfloat16
MESH = pl.DeviceIdType.MESH

D_MODEL = 1024
D_FF = 2816
N_DEV = 8
N_CHIPS = 4
FF_SHARD = D_FF // N_DEV
FF_HALF = D_FF // 2
N_HEADS = 8
NOPE = 128
ROPE = 64
QK_DIM = NOPE + ROPE
QK_PAD = 256
V_DIM = 128
Q_RANK = 256
KV_RANK = 512
POOL_WINDOWS = (2, 4, 8, 16)
GROUP_DIM = 256
HALO = 128
CHUNK = 64
ROPE_THETA = 10000.0
EPS = 1e-6
LANES = 128

ADAM_LR = 0.001
ADAM_B1 = 0.9
ADAM_B2 = 0.999
ADAM_EPS = 1e-08
ADAM_WD = 0.01
ADAM_STEP = 10

PROJ_ROWS = 256
VMEM_BIG = 56 * 2**20
VMEM_MID = 40 * 2**20


def _nn(a, b):
    return lax.dot_general(a, b, (((1,), (0,)), ((), ())), preferred_element_type=F32)


def _nt(a, b):
    return lax.dot_general(a, b, (((1,), (1,)), ((), ())), preferred_element_type=F32)


def _tn(a, b):
    return lax.dot_general(a, b, (((0,), (0,)), ((), ())), preferred_element_type=F32)


def _rms(x, g, n):
    r = lax.rsqrt(jnp.sum(x * x, axis=-1, keepdims=True) * (1.0 / n) + EPS)
    return (x * r) * g, r


def _rms_bwd(x, r, g, dy, n):
    u = dy * g
    s = jnp.sum(x * u, axis=-1, keepdims=True) * (1.0 / n)
    dx = r * u - x * (r * r * r * s)
    dg = jnp.sum(dy * (x * r), axis=0, keepdims=True)
    return dx, dg


def _swap_perm():
    i = lax.broadcasted_iota(jnp.int32, (LANES, LANES), 0)
    j = lax.broadcasted_iota(jnp.int32, (LANES, LANES), 1)
    half = ROPE // 2
    hit = ((j < half) & (i == j + half)) | ((j >= half) & (j < ROPE) & (i == j - half))
    return jnp.where(hit, 1.0, 0.0).astype(BF16)


def _swap_halves(z, perm):
    hi = z.astype(BF16)
    lo = (z - hi.astype(F32)).astype(BF16)
    return _nn(hi, perm) + _nn(lo, perm)


def _sigmoid(x):
    return 1.0 / (1.0 + jnp.exp(-x))


def _cparams(n_grid, vmem=None):
    return pltpu.CompilerParams(dimension_semantics=("arbitrary",) * n_grid, vmem_limit_bytes=vmem)


def _rows(t, cols):
    return pl.BlockSpec((t, cols), lambda i: (i, 0))


def _full(shape):
    nd = len(shape)
    return pl.BlockSpec(shape, lambda *_: (0,) * nd)


ANY = pl.BlockSpec(memory_space=pl.ANY)


def _pcall(body, args, deps, *, in_specs, **kw):
    n_in, n_dep = len(args), len(deps)

    def ordered(*refs):
        body(*refs[:n_in], *refs[n_in + n_dep:])

    return pl.pallas_call(ordered, in_specs=list(in_specs) + [ANY] * n_dep, **kw)(*args, *deps)


def _place():
    x, y, c = lax.axis_index("x"), lax.axis_index("y"), lax.axis_index("c")
    return x, y, c


def _all_gather(shards, axes, name, deps=()):
    n, nd = len(shards), len(deps)
    out_shape = [jax.ShapeDtypeStruct(s.shape[:a] + (N_DEV,) + s.shape[a:], s.dtype) for s, a in zip(shards, axes)]

    def body(*refs):
        ins, outs = refs[:n], refs[n + nd:2 * n + nd]
        send_sems, recv_sems, local_sems = refs[2 * n + nd:]
        x, y, c = _place()
        me, sibling = (x, y, c), (x, y, 1 - c)
        chips = [(1 - x, y), (x, 1 - y), (1 - x, 1 - y)]

        def slot(t, dev):
            idx = 4 * dev[0] + 2 * dev[1] + dev[2]
            return outs[t].at[(slice(None),) * axes[t] + (idx,)]

        def copy(t, k, block, to, src=None):
            return pltpu.make_async_remote_copy(
                src_ref=slot(t, block) if src is None else src, dst_ref=slot(t, block),
                send_sem=send_sems.at[t, k], recv_sem=recv_sems.at[t, k],
                device_id=to, device_id_type=MESH)

        mine = [pltpu.make_async_copy(ins[t], slot(t, me), local_sems.at[t]) for t in range(n)]
        for cp in mine:
            cp.start()
        first = []
        for t in range(n):
            first.append(copy(t, 0, me, sibling, src=ins[t]))
            first += [copy(t, 1 + j, me, (*chip, c), src=ins[t]) for j, chip in enumerate(chips)]
        for cp in first:
            cp.start()
        passed = []
        for j, chip in enumerate(chips):
            for t in range(n):
                copy(t, 1 + j, (*chip, c), me).wait_recv()
                cp = copy(t, 4 + j, (*chip, c), sibling)
                cp.start()
                passed.append(cp)
        for t in range(n):
            copy(t, 0, sibling, me).wait_recv()
            for j, chip in enumerate(chips):
                copy(t, 4 + j, (*chip, 1 - c), me).wait_recv()
        for cp in first + passed:
            cp.wait_send()
        for cp in mine:
            cp.wait()

    return pl.pallas_call(
        body, name=name, out_shape=out_shape,
        in_specs=[ANY] * (n + nd), out_specs=[ANY] * n,
        scratch_shapes=[pltpu.SemaphoreType.DMA((n, 7)), pltpu.SemaphoreType.DMA((n, 7)),
                        pltpu.SemaphoreType.DMA((n,))],
    )(*shards, *deps)


HBM = pl.BlockSpec(memory_space=pltpu.HBM)
SEM = pl.BlockSpec(memory_space=pltpu.SEMAPHORE)
EFFECT = pltpu.SideEffectType.DATAFLOW_SIDE_EFFECTING


def _copies_start(arrays, n_sems, plan, name, deps=()):
    n, nd = len(arrays), len(deps)

    def body(*refs):
        for cp in plan(refs[:n], refs[n + nd], refs[n + nd + 1]):
            cp.start()
        refs[-1][...] = jnp.zeros_like(refs[-1])

    outs = pl.pallas_call(
        body, name=name,
        out_shape=(pltpu.SemaphoreType.DMA((n_sems,)), pltpu.SemaphoreType.DMA((n_sems,)),
                   *[pltpu.HBM(a.shape, a.dtype) for a in arrays], jax.ShapeDtypeStruct((8, LANES), F32)),
        in_specs=[HBM] * n + [ANY] * nd,
        out_specs=(SEM, SEM, *[HBM] * n, pl.BlockSpec(memory_space=pltpu.VMEM)),
        input_output_aliases={i: 2 + i for i in range(n)},
        compiler_params=pltpu.CompilerParams(has_side_effects=EFFECT),
    )(*[pltpu.with_memory_space_constraint(a, pltpu.HBM) for a in arrays], *deps)
    return outs[0], outs[1], list(outs[2:2 + n]), outs[-1]


def _copies_wait(arrays, send_sems, recv_sems, after, plan, name):
    n = len(arrays)
    after = list(after) if isinstance(after, (list, tuple)) else [after]

    def body(*refs):
        for cp in plan(refs[:n], refs[n], refs[n + 1]):
            cp.wait_send()
            cp.wait_recv()

    outs = pl.pallas_call(
        body, name=name,
        out_shape=tuple(pltpu.HBM(a.shape, a.dtype) for a in arrays),
        in_specs=[HBM] * n + [SEM, SEM] + [ANY] * len(after), out_specs=tuple([HBM] * n),
        input_output_aliases={i: i for i in range(n)},
        compiler_params=pltpu.CompilerParams(has_side_effects=EFFECT),
    )(*arrays, send_sems, recv_sems, *after)
    return list(outs)


def _remote(src, dst, send_sems, recv_sems, t, to):
    return pltpu.make_async_remote_copy(src_ref=src, dst_ref=dst, send_sem=send_sems.at[t], recv_sem=recv_sems.at[t],
                                        device_id=to, device_id_type=MESH)


def _dev_index(x, y, c):
    return 4 * x + 2 * y + c


def _gather_spread(bufs, send_sems, recv_sems):
    x, y, c = _place()
    mine = _dev_index(x, y, c)
    peers = [(x, y, 1 - c), (1 - x, y, c), (x, 1 - y, c), (1 - x, 1 - y, c)]
    return [_remote(g.at[k, mine], g.at[k, mine], send_sems, recv_sems, t, peer)
            for t, g in enumerate(bufs) for peer in peers for k in range(g.shape[0])]


def _gather_relay(bufs, send_sems, recv_sems):
    x, y, c = _place()
    blocks = [_dev_index(1 - x, y, c), _dev_index(x, 1 - y, c), _dev_index(1 - x, 1 - y, c)]
    return [_remote(g.at[k, b], g.at[k, b], send_sems, recv_sems, t, (x, y, 1 - c))
            for t, g in enumerate(bufs) for b in blocks for k in range(g.shape[0])]


def _blocks_moved(count):
    def plan(bufs, send_sems, recv_sems):
        x, y, c = _place()
        return [_remote(g.at[:, pl.ds(0, count)], g.at[:, pl.ds(0, count)], send_sems, recv_sems, t, (x, y, 1 - c))
                for t, g in enumerate(bufs)]
    return plan


def _pair_send(arrs, send_sems, recv_sems):
    x, y, c = _place()
    return [_remote(arrs[2 * t].at[p, k, 1 - c], arrs[2 * t + 1].at[p, k], send_sems, recv_sems, t, (x, y, 1 - c))
            for t in range(len(arrs) // 2) for p in range(arrs[2 * t].shape[0]) for k in range(N_CHIPS)]


def _chip_send(arrs, send_sems, recv_sems):
    x, y, c = _place()
    chips = [(1 - x, y), (x, 1 - y), (1 - x, 1 - y)]
    return [_remote(arrs[2 * t].at[p, 2 * px + py], arrs[2 * t + 1].at[j, p], send_sems, recv_sems, t, (px, py, c))
            for t in range(len(arrs) // 2) for j, (px, py) in enumerate(chips) for p in range(arrs[2 * t].shape[0])]


def _landed(arrs, send_sems, recv_sems):
    x, y, c = _place()
    return [_remote(arrs[2 * t + 1], arrs[2 * t + 1], send_sems, recv_sems, t, (x, y, 1 - c))
            for t in range(len(arrs) // 2)]


def _rows_per_step(rows, row_elems):
    best = 1
    for cand in range(1, rows + 1):
        if rows % cand == 0 and cand * row_elems <= 256 * 1024:
            best = cand
    return best


def _pair_sum(grad, landed, core, name):
    p, _, _, sz, c = grad.shape
    r = _rows_per_step(p * N_CHIPS, sz * c)

    def body(core_ref, g_ref, l_ref, o_ref):
        o_ref[...] = (g_ref[...].astype(F32) + l_ref[...].astype(F32)).astype(o_ref.dtype)

    out = pl.pallas_call(
        body, name=name,
        grid_spec=pltpu.PrefetchScalarGridSpec(
            num_scalar_prefetch=1, grid=(p * N_CHIPS // r,),
            in_specs=[pl.BlockSpec((r, None, sz, c), lambda i, cr: (i, cr[0], 0, 0)),
                      pl.BlockSpec((r, sz, c), lambda i, cr: (i, 0, 0))],
            out_specs=pl.BlockSpec((r, sz, c), lambda i, cr: (i, 0, 0))),
        out_shape=jax.ShapeDtypeStruct((p * N_CHIPS, sz, c), grad.dtype),
        compiler_params=_cparams(1),
    )(core, grad.reshape(p * N_CHIPS, 2, sz, c), landed.reshape(p * N_CHIPS, sz, c))
    return out.reshape(p, N_CHIPS, sz, c)


def _chip_sum(parts, landed, chip, name, deps=()):
    p, _, sz, c = parts.shape
    r = _rows_per_step(p, sz * c)

    def body(chip_ref, a_ref, l_ref, o_ref):
        acc = a_ref[...].astype(F32)
        for j in range(3):
            acc = acc + l_ref[j].astype(F32)
        o_ref[...] = acc

    nd = len(deps)

    def ordered(chip_ref, a_ref, l_ref, *rest):
        body(chip_ref, a_ref, l_ref, rest[nd])

    return pl.pallas_call(
        ordered, name=name,
        grid_spec=pltpu.PrefetchScalarGridSpec(
            num_scalar_prefetch=1, grid=(p // r,),
            in_specs=[pl.BlockSpec((r, None, sz, c), lambda i, cr: (i, cr[0], 0, 0)),
                      pl.BlockSpec((3, r, sz, c), lambda i, cr: (0, i, 0, 0))] + [ANY] * nd,
            out_specs=pl.BlockSpec((r, sz, c), lambda i, cr: (i, 0, 0))),
        out_shape=jax.ShapeDtypeStruct((p, sz, c), F32),
        compiler_params=_cparams(1),
    )(chip, parts, landed, *deps)


def _sum_lead(a, name, out_dtype=F32):
    k = a.shape[0]
    rest = a.shape[1:]
    r, c = rest[-2], rest[-1]
    lead = math.prod(rest[:-2])
    a3 = a.reshape(k, lead * r, c)
    rows = lead * r
    tb = rows
    for cand in (512, 256, 128, 64, 32, 16, 8):
        if rows % cand == 0 and rows > cand:
            tb = cand
            break

    def body(a_ref, o_ref):
        acc = a_ref[0].astype(F32)
        for i in range(1, k):
            acc = acc + a_ref[i].astype(F32)
        o_ref[...] = acc.astype(out_dtype)

    out = pl.pallas_call(
        body, name=name, grid=(rows // tb,),
        out_shape=jax.ShapeDtypeStruct((rows, c), out_dtype),
        in_specs=[pl.BlockSpec((k, tb, c), lambda i: (0, i, 0))],
        out_specs=pl.BlockSpec((tb, c), lambda i: (i, 0)),
        compiler_params=_cparams(1),
    )(a3)
    return out.reshape(rest)


def _bands(t, causal):
    r = lax.broadcasted_iota(jnp.int32, (t, t + HALO), 0)
    col = lax.broadcasted_iota(jnp.int32, (t, t + HALO), 1)
    diff = r + HALO - col if causal else col - r
    return jnp.stack([jnp.where((diff >= 0) & (diff < w), 1.0, 0.0) for w in POOL_WINDOWS]).astype(BF16)


def _split_dot(band, v):
    hi = v.astype(BF16)
    lo = (v - hi.astype(F32)).astype(BF16)
    return _nn(band, hi) + _nn(band, lo)


def _mix_fwd(x, g, wp, b, sc, name, deps=()):
    s = x.shape[0]
    t = min(256, s)
    rb = t // HALO

    def body(x_ref, xh_ref, g_ref, wp_ref, b_ref, sc_ref, band_ref, xo_ref, d_ref):
        i = pl.program_id(0)
        gg = g_ref[...]
        h, _ = _rms(x_ref[...], gg, D_MODEL)
        hh, _ = _rms(xh_ref[...], gg, D_MODEL)
        hh = jnp.where(i > 0, hh, 0.0)
        hext = jnp.concatenate([hh, h], axis=0)
        tok = i * t + lax.broadcasted_iota(jnp.int32, (t, 1), 0)
        for gi, w in enumerate(POOL_WINDOWS):
            sl = slice(gi * GROUP_DIM, (gi + 1) * GROUP_DIM)
            win = _split_dot(band_ref[gi], hext[:, sl])
            inv = 1.0 / jnp.minimum(tok + 1, w).astype(F32)
            dbf = (win * inv - h[:, sl]).astype(BF16)
            d_ref[:, sl] = dbf
            ypre = _nn(dbf, wp_ref[gi]) + b_ref[:, sl]
            xo_ref[:, sl] = x_ref[:, sl] + ypre * sc_ref[:, sl]

    return _pcall(
        body, (x, x, g, wp, b, sc, _bands(t, True)), deps, name=name, grid=(s // t,),
        out_shape=[jax.ShapeDtypeStruct((s, D_MODEL), F32), jax.ShapeDtypeStruct((s, D_MODEL), BF16)],
        in_specs=[_rows(t, D_MODEL),
                  pl.BlockSpec((HALO, D_MODEL), lambda i: (jnp.maximum(i * rb - 1, 0), 0)),
                  _full((1, D_MODEL)), _full((4, GROUP_DIM, GROUP_DIM)), _full((1, D_MODEL)), _full((1, D_MODEL)),
                  _full((4, t, t + HALO))],
        out_specs=[_rows(t, D_MODEL), _rows(t, D_MODEL)],
        compiler_params=_cparams(1, VMEM_MID),
    )


def _mix_bwd(x, dy, d, g, wp, b, sc, name, deps=()):
    s = x.shape[0]
    t = min(256, s)
    rb = t // HALO
    nb = s // t
    last_halo = s // HALO - 1

    def body(x_ref, dy_ref, dyn_ref, d_ref, g_ref, wp_ref, b_ref, sc_ref, band_ref,
             dx_ref, dyp_ref, dsc_ref, db_ref, dln_ref):
        i = pl.program_id(0)
        x = x_ref[...]
        gg = g_ref[...]
        dy = dy_ref[...]
        sc = sc_ref[...]
        dyp32 = dy * sc
        dyp = dyp32.astype(BF16)
        dyph = (dyn_ref[...] * sc).astype(BF16)
        dyp_ref[...] = dyp
        tok = i * t + lax.broadcasted_iota(jnp.int32, (t + HALO, 1), 0)
        dh, dsc = [], []
        for gi, w in enumerate(POOL_WINDOWS):
            sl = slice(gi * GROUP_DIM, (gi + 1) * GROUP_DIM)
            ypre = _nn(d_ref[:, sl], wp_ref[gi]) + b_ref[:, sl]
            dsc.append(jnp.sum(dy[:, sl] * ypre, axis=0, keepdims=True))
            dd = _nt(dyp[:, sl], wp_ref[gi])
            ddh = jnp.where(i < nb - 1, _nt(dyph[:, sl], wp_ref[gi]), 0.0)
            inv = 1.0 / jnp.minimum(tok + 1, w).astype(F32)
            ddext = jnp.concatenate([dd, ddh], axis=0) * inv
            dh.append(_split_dot(band_ref[gi], ddext) - dd)
        dh = jnp.concatenate(dh, axis=1)
        _, r = _rms(x, gg, D_MODEL)
        dxn, dg = _rms_bwd(x, r, gg, dh, D_MODEL)
        dx_ref[...] = dy + dxn

        @pl.when(i == 0)
        def _():
            dsc_ref[...] = jnp.zeros_like(dsc_ref)
            db_ref[...] = jnp.zeros_like(db_ref)
            dln_ref[...] = jnp.zeros_like(dln_ref)

        dsc_ref[...] += jnp.concatenate(dsc, axis=1)
        db_ref[...] += jnp.sum(dyp32, axis=0, keepdims=True)
        dln_ref[...] += dg

    vec = jax.ShapeDtypeStruct((1, D_MODEL), F32)
    return _pcall(
        body, (x, dy, dy, d, g, wp, b, sc, _bands(t, False)), deps, name=name, grid=(nb,),
        out_shape=[jax.ShapeDtypeStruct((s, D_MODEL), F32), jax.ShapeDtypeStruct((s, D_MODEL), BF16), vec, vec, vec],
        in_specs=[_rows(t, D_MODEL), _rows(t, D_MODEL),
                  pl.BlockSpec((HALO, D_MODEL), lambda i: (jnp.minimum((i + 1) * rb, last_halo), 0)),
                  _rows(t, D_MODEL),
                  _full((1, D_MODEL)), _full((4, GROUP_DIM, GROUP_DIM)), _full((1, D_MODEL)), _full((1, D_MODEL)),
                  _full((4, t, t + HALO))],
        out_specs=[_rows(t, D_MODEL), _rows(t, D_MODEL), _full((1, D_MODEL)), _full((1, D_MODEL)), _full((1, D_MODEL))],
        compiler_params=_cparams(1, VMEM_MID),
    )


def _load_weights(w_hbm, w_vmem, sem):
    @pl.when(pl.program_id(0) == 0)
    def _():
        cp = pltpu.make_async_copy(w_hbm, w_vmem, sem)
        cp.start()
        cp.wait()


def _ffn_fwd(x, g, w, name):
    s = x.shape[0]
    t = min(512, s)

    def body(x_ref, g_ref, w_hbm, xo_ref, gate_ref, up_ref, w_ref, sem):
        _load_weights(w_hbm, w_ref, sem)
        x = x_ref[...]
        hn = _rms(x, g_ref[...], D_MODEL)[0].astype(BF16)
        acc = x
        for c in range(2):
            rs = slice(c * FF_HALF, (c + 1) * FF_HALF)
            gt = _nt(hn, w_ref[0, rs, :])
            up = _nt(hn, w_ref[1, rs, :])
            gate_ref[:, rs] = gt.astype(BF16)
            up_ref[:, rs] = up.astype(BF16)
            act = ((gt * _sigmoid(gt)) * up).astype(BF16)
            acc = acc + _nn(act, w_ref[2, rs, :])
        xo_ref[...] = acc

    hid = jax.ShapeDtypeStruct((s, D_FF), BF16)
    return pl.pallas_call(
        body, name=name, grid=(s // t,),
        out_shape=[jax.ShapeDtypeStruct((s, D_MODEL), F32), hid, hid],
        in_specs=[_rows(t, D_MODEL), _full((1, D_MODEL)), ANY],
        out_specs=[_rows(t, D_MODEL), _rows(t, D_FF), _rows(t, D_FF)],
        scratch_shapes=[pltpu.VMEM((3, D_FF, D_MODEL), BF16), pltpu.SemaphoreType.DMA],
        compiler_params=_cparams(1, VMEM_BIG),
    )(x, g, w)


def _ffn_bwd(x, dy, gate, up, g, w, name, deps=()):
    s = x.shape[0]
    t = min(256, s)

    def body(x_ref, dy_ref, gate_ref, up_ref, g_ref, w_hbm,
             dx_ref, act_ref, dg_ref, du_ref, hn_ref, dyb_ref, dln_ref, w_ref, sem):
        _load_weights(w_hbm, w_ref, sem)
        x = x_ref[...]
        gg = g_ref[...]
        y, r = _rms(x, gg, D_MODEL)
        hn = y.astype(BF16)
        hn_ref[...] = hn
        dy = dy_ref[...]
        dyb = dy.astype(BF16)
        dyb_ref[...] = dyb
        dh = jnp.zeros((t, D_MODEL), F32)
        for c in range(2):
            rs = slice(c * FF_HALF, (c + 1) * FF_HALF)
            gt = gate_ref[:, rs].astype(F32)
            u = up_ref[:, rs].astype(F32)
            sg = _sigmoid(gt)
            sl = gt * sg
            act_ref[:, rs] = (sl * u).astype(BF16)
            dact = _nt(dyb, w_ref[2, rs, :])
            dg = (dact * u * (sg * (1.0 + gt * (1.0 - sg)))).astype(BF16)
            du = (dact * sl).astype(BF16)
            dg_ref[:, rs] = dg
            du_ref[:, rs] = du
            dh = dh + _nn(dg, w_ref[0, rs, :]) + _nn(du, w_ref[1, rs, :])
        dxn, dgl = _rms_bwd(x, r, gg, dh, D_MODEL)
        dx_ref[...] = dy + dxn

        @pl.when(pl.program_id(0) == 0)
        def _():
            dln_ref[...] = jnp.zeros_like(dln_ref)

        dln_ref[...] += dgl

    hid = jax.ShapeDtypeStruct((s, D_FF), BF16)
    tok = jax.ShapeDtypeStruct((s, D_MODEL), BF16)
    return _pcall(
        body, (x, dy, gate, up, g, w), deps, name=name, grid=(s // t,),
        out_shape=[jax.ShapeDtypeStruct((s, D_MODEL), F32), hid, hid, hid, tok, tok,
                   jax.ShapeDtypeStruct((1, D_MODEL), F32)],
        in_specs=[_rows(t, D_MODEL), _rows(t, D_MODEL), _rows(t, D_FF), _rows(t, D_FF), _full((1, D_MODEL)), ANY],
        out_specs=[_rows(t, D_MODEL), _rows(t, D_FF), _rows(t, D_FF), _rows(t, D_FF),
                   _rows(t, D_MODEL), _rows(t, D_MODEL), _full((1, D_MODEL))],
        scratch_shapes=[pltpu.VMEM((3, D_FF, D_MODEL), BF16), pltpu.SemaphoreType.DMA],
        compiler_params=_cparams(1, VMEM_BIG),
    )


def _tn_matmul(a, b, into, p0, name, groups=1, m_chunk=None, deps=()):
    s = a.shape[0]
    m, n = a.shape[1] // groups, b.shape[1] // groups
    assert into.shape[1:] == (m, n)
    mc = m if m_chunk is None else m_chunk
    nm = m // mc
    t = min(1024, s)
    nt = s // t

    def body(a_ref, b_ref, into_ref, o_ref, acc):
        ti = pl.program_id(2)

        @pl.when(ti == 0)
        def _():
            acc[...] = jnp.zeros_like(acc)

        acc[...] += _tn(a_ref[...], b_ref[...])

        @pl.when(ti == nt - 1)
        def _():
            o_ref[...] = acc[...].astype(o_ref.dtype)

    return _pcall(
        body, (a, b, into), deps, name=name, grid=(groups, nm, nt),
        out_shape=jax.ShapeDtypeStruct(into.shape, into.dtype),
        in_specs=[pl.BlockSpec((t, mc), lambda gi, mi, ti: (ti, gi * nm + mi)),
                  pl.BlockSpec((t, n), lambda gi, mi, ti: (ti, gi)), ANY],
        out_specs=pl.BlockSpec((None, mc, n), lambda gi, mi, ti: (p0 + gi, mi, 0)),
        scratch_shapes=[pltpu.VMEM((mc, n), F32)],
        input_output_aliases={2: 0},
        compiler_params=_cparams(3, VMEM_BIG),
    )


def _rope_tables(positions):
    half = ROPE // 2
    inv = ROPE_THETA ** (-jnp.arange(half, dtype=F32) * 2.0 / ROPE)
    ang = positions.astype(F32)[:, None] * inv
    cos, sin = jnp.cos(ang), jnp.sin(ang)
    zero = jnp.zeros((positions.shape[0], LANES - ROPE), F32)
    return jnp.concatenate([cos, cos, zero], axis=1), jnp.concatenate([-sin, sin, zero], axis=1)


def _kv_specs(t):
    return [_full((1, D_MODEL)), _full((D_MODEL, KV_RANK)), _full((D_MODEL, LANES)), _full((1, KV_RANK)),
            _full((KV_RANK, N_HEADS * NOPE)), _full((KV_RANK, N_HEADS * V_DIM)),
            _full((1, NOPE)), _full((1, LANES)), _rows(t, LANES), _rows(t, LANES)]


def _kv_fwd(x, ln, wc, wpe, gl, wuk, wuv, gkn, gkr, cos, sin, name, deps=()):
    s = x.shape[0]
    t = min(PROJ_ROWS, s)

    def body(x_ref, ln_ref, wc_ref, wpe_ref, gl_ref, wuk_ref, wuv_ref, gkn_ref, gkr_ref, cos_ref, sin_ref,
             k_ref, v_ref):
        hn = _rms(x_ref[...], ln_ref[...], D_MODEL)[0].astype(BF16)
        clat = _nn(hn, wc_ref[...])
        kpe = _nn(hn, wpe_ref[...])
        cn = _rms(clat, gl_ref[...], KV_RANK)[0].astype(BF16)
        sspe = jnp.sum(kpe * kpe, axis=-1, keepdims=True)
        base = kpe * gkr_ref[...]
        rot = base * cos_ref[...] + _swap_halves(base, _swap_perm()) * sin_ref[...]
        kn_all = _nn(cn, wuk_ref[...])
        v_ref[...] = _nn(cn, wuv_ref[...]).astype(BF16)
        for h in range(N_HEADS):
            kn = kn_all[:, h * NOPE:(h + 1) * NOPE]
            r = lax.rsqrt((jnp.sum(kn * kn, axis=-1, keepdims=True) + sspe) * (1.0 / QK_DIM) + EPS)
            k_ref[:, h * QK_PAD:h * QK_PAD + NOPE] = ((kn * r) * gkn_ref[...]).astype(BF16)
            k_ref[:, h * QK_PAD + NOPE:(h + 1) * QK_PAD] = (rot * r).astype(BF16)

    return _pcall(
        body, (x, ln, wc, wpe, gl, wuk, wuv, gkn, gkr, cos, sin), deps, name=name, grid=(s // t,),
        out_shape=[jax.ShapeDtypeStruct((s, N_HEADS * QK_PAD), BF16), jax.ShapeDtypeStruct((s, N_HEADS * V_DIM), BF16)],
        in_specs=[_rows(t, D_MODEL)] + _kv_specs(t),
        out_specs=[_rows(t, N_HEADS * QK_PAD), _rows(t, N_HEADS * V_DIM)],
        compiler_params=_cparams(1, VMEM_MID),
    )


def _kv_bwd(x, dxin, dks, dvs, ln, wc, wpe, gl, wuk, wuv, gkn, gkr, cos, sin, name):
    s = x.shape[0]
    t = min(PROJ_ROWS, s)
    nk = len(dks)

    def body(*refs):
        x_ref, dxin_ref = refs[:2]
        dk_refs = refs[2:2 + nk]
        dv_refs = refs[2 + nk:2 + 2 * nk]
        (ln_ref, wc_ref, wpe_ref, gl_ref, wuk_ref, wuv_ref, gkn_ref, gkr_ref, cos_ref, sin_ref,
         dx_ref, hn_ref, cn_ref, dkn_ref, dvb_ref, dcc_ref, dpe_ref,
         dln_ref, dgl_ref, dgkn_ref, dgkr_ref) = refs[2 + 2 * nk:]
        x = x_ref[...]
        ln = ln_ref[...]
        y, rx = _rms(x, ln, D_MODEL)
        hn = y.astype(BF16)
        hn_ref[...] = hn
        clat = _nn(hn, wc_ref[...])
        kpe = _nn(hn, wpe_ref[...])
        gl = gl_ref[...]
        cy, rc = _rms(clat, gl, KV_RANK)
        cn = cy.astype(BF16)
        cn_ref[...] = cn
        sspe = jnp.sum(kpe * kpe, axis=-1, keepdims=True)
        cs, sn, perm = cos_ref[...], sin_ref[...], _swap_perm()
        gkn, gkr = gkn_ref[...], gkr_ref[...]
        base = kpe * gkr
        rot = base * cs + _swap_halves(base, perm) * sn
        dkr_sum = jnp.zeros((t, LANES), F32)
        coef_sum = jnp.zeros((t, 1), F32)
        dgkn = jnp.zeros((1, NOPE), F32)
        kn_all = _nn(cn, wuk_ref[...])
        dkn_heads = []
        for h in range(N_HEADS):
            kn = kn_all[:, h * NOPE:(h + 1) * NOPE]
            r = lax.rsqrt((jnp.sum(kn * kn, axis=-1, keepdims=True) + sspe) * (1.0 / QK_DIM) + EPS)
            lo, mid, hi = h * QK_PAD, h * QK_PAD + NOPE, (h + 1) * QK_PAD
            dko = dk_refs[0][:, lo:mid]
            dkr = dk_refs[0][:, mid:hi]
            for j in range(1, nk):
                dko = dko + dk_refs[j][:, lo:mid]
                dkr = dkr + dk_refs[j][:, mid:hi]
            un = dko * gkn
            sm = (jnp.sum(kn * un, axis=-1, keepdims=True) + jnp.sum(rot * dkr, axis=-1, keepdims=True)) * (1.0 / QK_DIM)
            coef = r * r * r * sm
            dkn = (r * un - kn * coef).astype(BF16)
            dkr_sum = dkr_sum + r * dkr
            coef_sum = coef_sum + coef
            dgkn = dgkn + jnp.sum(dko * (kn * r), axis=0, keepdims=True)
            dkn_heads.append(dkn)
        dkn_all = jnp.concatenate(dkn_heads, axis=1)
        dkn_ref[...] = dkn_all
        dv_all = dv_refs[0][...]
        for j in range(1, nk):
            dv_all = dv_all + dv_refs[j][...]
        dvb = dv_all.astype(BF16)
        dvb_ref[...] = dvb
        dc = _nt(dkn_all, wuk_ref[...]) + _nt(dvb, wuv_ref[...])
        dz = dkr_sum * cs - _swap_halves(dkr_sum, perm) * sn
        dkpe = dz * gkr - kpe * coef_sum
        dgkr = jnp.sum(dz * kpe, axis=0, keepdims=True)
        dclat, dgl = _rms_bwd(clat, rc, gl, dc, KV_RANK)
        dcc = dclat.astype(BF16)
        dpe = dkpe.astype(BF16)
        dcc_ref[...] = dcc
        dpe_ref[...] = dpe
        dhn = _nt(dcc, wc_ref[...]) + _nt(dpe, wpe_ref[...])
        dxn, dln = _rms_bwd(x, rx, ln, dhn, D_MODEL)
        dx_ref[...] = dxin_ref[...] + dxn

        @pl.when(pl.program_id(0) == 0)
        def _():
            dln_ref[...] = jnp.zeros_like(dln_ref)
            dgl_ref[...] = jnp.zeros_like(dgl_ref)
            dgkn_ref[...] = jnp.zeros_like(dgkn_ref)
            dgkr_ref[...] = jnp.zeros_like(dgkr_ref)

        dln_ref[...] += dln
        dgl_ref[...] += dgl
        dgkn_ref[...] += dgkn
        dgkr_ref[...] += dgkr

    def tok(cols, dt):
        return jax.ShapeDtypeStruct((s, cols), dt)

    def vec(cols):
        return jax.ShapeDtypeStruct((1, cols), F32)

    return pl.pallas_call(
        body, name=name, grid=(s // t,),
        out_shape=[tok(D_MODEL, F32), tok(D_MODEL, BF16), tok(KV_RANK, BF16), tok(N_HEADS * NOPE, BF16),
                   tok(N_HEADS * V_DIM, BF16), tok(KV_RANK, BF16), tok(LANES, BF16),
                   vec(D_MODEL), vec(KV_RANK), vec(NOPE), vec(LANES)],
        in_specs=[_rows(t, D_MODEL), _rows(t, D_MODEL)] + [_rows(t, N_HEADS * QK_PAD)] * nk
                 + [_rows(t, N_HEADS * V_DIM)] * nk + _kv_specs(t),
        out_specs=[_rows(t, D_MODEL), _rows(t, D_MODEL), _rows(t, KV_RANK), _rows(t, N_HEADS * NOPE),
                   _rows(t, N_HEADS * V_DIM), _rows(t, KV_RANK), _rows(t, LANES),
                   _full((1, D_MODEL)), _full((1, KV_RANK)), _full((1, NOPE)), _full((1, LANES))],
        compiler_params=_cparams(1, VMEM_BIG),
    )(x, dxin, *dks, *dvs, ln, wc, wpe, gl, wuk, wuv, gkn, gkr, cos, sin)


def _q_specs(t):
    return [_full((1, D_MODEL)), _full((D_MODEL, Q_RANK)), _full((1, Q_RANK)), _full((N_HEADS, Q_RANK, QK_PAD)),
            _full((1, NOPE)), _full((1, LANES)), _rows(t, LANES), _rows(t, LANES)]


def _q_fwd(x, ln, wdq, gql, wuq, gqn, gqr, cos, sin, name, deps=()):
    s = x.shape[0]
    t = min(PROJ_ROWS, s)

    def body(x_ref, ln_ref, wdq_ref, gql_ref, wuq_ref, gqn_ref, gqr_ref, cos_ref, sin_ref, q_ref):
        hn = _rms(x_ref[...], ln_ref[...], D_MODEL)[0].astype(BF16)
        cqn = _rms(_nn(hn, wdq_ref[...]), gql_ref[...], Q_RANK)[0].astype(BF16)
        cs, sn, perm = cos_ref[...], sin_ref[...], _swap_perm()
        for h in range(N_HEADS):
            qa = _nn(cqn, wuq_ref[h])
            r = lax.rsqrt(jnp.sum(qa * qa, axis=-1, keepdims=True) * (1.0 / QK_DIM) + EPS)
            q_ref[:, h * QK_PAD:h * QK_PAD + NOPE] = ((qa[:, :NOPE] * r) * gqn_ref[...]).astype(BF16)
            z = (qa[:, NOPE:] * r) * gqr_ref[...]
            q_ref[:, h * QK_PAD + NOPE:(h + 1) * QK_PAD] = (z * cs + _swap_halves(z, perm) * sn).astype(BF16)

    return _pcall(
        body, (x, ln, wdq, gql, wuq, gqn, gqr, cos, sin), deps, name=name, grid=(s // t,),
        out_shape=jax.ShapeDtypeStruct((s, N_HEADS * QK_PAD), BF16),
        in_specs=[_rows(t, D_MODEL)] + _q_specs(t),
        out_specs=_rows(t, N_HEADS * QK_PAD),
        compiler_params=_cparams(1, VMEM_MID),
    )


def _q_bwd(x, dxin, dq, ln, wdq, gql, wuq, gqn, gqr, cos, sin, name):
    s = x.shape[0]
    t = min(PROJ_ROWS, s)

    def body(x_ref, dxin_ref, dq_ref, ln_ref, wdq_ref, gql_ref, wuq_ref, gqn_ref, gqr_ref, cos_ref, sin_ref,
             dx_ref, hn_ref, cqn_ref, dqa_ref, dcq_ref, dln_ref, dgql_ref, dgqn_ref, dgqr_ref):
        x = x_ref[...]
        ln = ln_ref[...]
        y, rx = _rms(x, ln, D_MODEL)
        hn = y.astype(BF16)
        hn_ref[...] = hn
        cqp = _nn(hn, wdq_ref[...])
        gql = gql_ref[...]
        cy, rc = _rms(cqp, gql, Q_RANK)
        cqn = cy.astype(BF16)
        cqn_ref[...] = cqn
        cs, sn, perm = cos_ref[...], sin_ref[...], _swap_perm()
        gqn, gqr = gqn_ref[...], gqr_ref[...]
        dcq = jnp.zeros((t, Q_RANK), F32)
        dgqn = jnp.zeros((1, NOPE), F32)
        dgqr = jnp.zeros((1, LANES), F32)
        for h in range(N_HEADS):
            qa = _nn(cqn, wuq_ref[h])
            qn, qr = qa[:, :NOPE], qa[:, NOPE:]
            r = lax.rsqrt(jnp.sum(qa * qa, axis=-1, keepdims=True) * (1.0 / QK_DIM) + EPS)
            dqo = dq_ref[:, h * QK_PAD:h * QK_PAD + NOPE]
            dqr = dq_ref[:, h * QK_PAD + NOPE:(h + 1) * QK_PAD]
            dz = dqr * cs - _swap_halves(dqr, perm) * sn
            un = dqo * gqn
            ur = dz * gqr
            sm = (jnp.sum(qn * un, axis=-1, keepdims=True) + jnp.sum(qr * ur, axis=-1, keepdims=True)) * (1.0 / QK_DIM)
            coef = r * r * r * sm
            dqa = jnp.concatenate([r * un - qn * coef, r * ur - qr * coef], axis=1).astype(BF16)
            dgqn = dgqn + jnp.sum(dqo * (qn * r), axis=0, keepdims=True)
            dgqr = dgqr + jnp.sum(dz * (qr * r), axis=0, keepdims=True)
            dqa_ref[:, h * QK_PAD:(h + 1) * QK_PAD] = dqa
            dcq = dcq + _nt(dqa, wuq_ref[h])
        dcqp, dgql = _rms_bwd(cqp, rc, gql, dcq, Q_RANK)
        dcqb = dcqp.astype(BF16)
        dcq_ref[...] = dcqb
        dhn = _nt(dcqb, wdq_ref[...])
        dxn, dln = _rms_bwd(x, rx, ln, dhn, D_MODEL)
        dx_ref[...] = dxin_ref[...] + dxn

        @pl.when(pl.program_id(0) == 0)
        def _():
            dln_ref[...] = jnp.zeros_like(dln_ref)
            dgql_ref[...] = jnp.zeros_like(dgql_ref)
            dgqn_ref[...] = jnp.zeros_like(dgqn_ref)
            dgqr_ref[...] = jnp.zeros_like(dgqr_ref)

        dln_ref[...] += dln
        dgql_ref[...] += dgql
        dgqn_ref[...] += dgqn
        dgqr_ref[...] += dgqr

    def tok(cols, dt):
        return jax.ShapeDtypeStruct((s, cols), dt)

    def vec(cols):
        return jax.ShapeDtypeStruct((1, cols), F32)

    return pl.pallas_call(
        body, name=name, grid=(s // t,),
        out_shape=[tok(D_MODEL, F32), tok(D_MODEL, BF16), tok(Q_RANK, BF16), tok(N_HEADS * QK_PAD, BF16),
                   tok(Q_RANK, BF16), vec(D_MODEL), vec(Q_RANK), vec(NOPE), vec(LANES)],
        in_specs=[_rows(t, D_MODEL), _rows(t, D_MODEL), _rows(t, N_HEADS * QK_PAD)] + _q_specs(t),
        out_specs=[_rows(t, D_MODEL), _rows(t, D_MODEL), _rows(t, Q_RANK), _rows(t, N_HEADS * QK_PAD),
                   _rows(t, Q_RANK), _full((1, D_MODEL)), _full((1, Q_RANK)), _full((1, NOPE)), _full((1, LANES))],
        compiler_params=_cparams(1, VMEM_MID),
    )(x, dxin, dq, ln, wdq, gql, wuq, gqn, gqr, cos, sin)


SM_SCALE = 1.0 / math.sqrt(QK_DIM)
LOG2_E = math.log2(math.e)
EXP2_SCALE = SM_SCALE * LOG2_E
NEG = -1e30


def _diag_mask(t):
    qpos = lax.broadcasted_iota(jnp.int32, (t, t), 0)
    kpos = lax.broadcasted_iota(jnp.int32, (t, t), 1)
    return lax.shift_right_logical(kpos, 6) <= lax.shift_right_logical(qpos, 6)


def _att_fwd(q, k, v, name):
    s = q.shape[0]
    t = min(512, s)
    nb = s // t

    def body(q_ref, k_ref, v_ref, o_ref, lse_ref):
        qi = pl.program_id(1)
        qq = q_ref[...]

        def block(ki, carry, masked):
            m_old, l_old, acc = carry
            rows = pl.ds(pl.multiple_of(ki * t, t), t)
            sc = _nt(qq, k_ref[rows, :])
            if masked:
                sc = jnp.where(_diag_mask(t), sc, NEG)
            m_new = jnp.maximum(m_old, jnp.max(sc, axis=-1, keepdims=True))
            p = jnp.exp2((sc - m_new) * EXP2_SCALE)
            alpha = jnp.exp2((m_old - m_new) * EXP2_SCALE)
            l_new = alpha * l_old + jnp.sum(p, axis=-1, keepdims=True)
            acc = alpha * acc + _nn(p.astype(BF16), v_ref[rows, :])
            return m_new, l_new, acc

        init = (jnp.full((t, 1), NEG, F32), jnp.zeros((t, 1), F32), jnp.zeros((t, V_DIM), F32))
        carry = lax.fori_loop(0, qi // 2, lambda j, c: block(2 * j + 1, block(2 * j, c, False), False), init)
        carry = lax.cond(qi % 2 == 1, lambda c: block(qi - 1, c, False), lambda c: c, carry)
        m_fin, l_fin, acc = block(qi, carry, True)
        o_ref[...] = (acc / l_fin).astype(BF16)
        lse_ref[...] = jnp.broadcast_to(m_fin * SM_SCALE + jnp.log(l_fin), (t, LANES))

    return pl.pallas_call(
        body, name=name, grid=(N_HEADS, nb),
        out_shape=[jax.ShapeDtypeStruct((s, N_HEADS * V_DIM), BF16), jax.ShapeDtypeStruct((s, N_HEADS * LANES), F32)],
        in_specs=[pl.BlockSpec((t, QK_PAD), lambda h, qi: (qi, h)),
                  pl.BlockSpec((s, QK_PAD), lambda h, qi: (0, h)),
                  pl.BlockSpec((s, V_DIM), lambda h, qi: (0, h))],
        out_specs=[pl.BlockSpec((t, V_DIM), lambda h, qi: (qi, h)),
                   pl.BlockSpec((t, LANES), lambda h, qi: (qi, h))],
        compiler_params=_cparams(2, VMEM_MID),
    )(q, k, v)


def _att_bwd(q, k, v, do, o, lse, name, deps=()):
    s = q.shape[0]
    t = min(512, s)
    nb = s // t

    def body(q_ref, k_ref, v_ref, do_ref, o_ref, lse_ref, dq_ref, dk_ref, dv_ref):
        ki = pl.program_id(1)
        kk, vv = k_ref[...], v_ref[...]

        @pl.when(ki == 0)
        def _():
            dq_ref[...] = jnp.zeros_like(dq_ref)

        def block(qi, carry, masked):
            dk, dv = carry
            rows = pl.ds(pl.multiple_of(qi * t, t), t)
            qq, dob = q_ref[rows, :], do_ref[rows, :]
            sc = _nt(qq, kk)
            if masked:
                sc = jnp.where(_diag_mask(t), sc, NEG)
            p = jnp.exp2(sc * EXP2_SCALE - lse_ref[rows, :][:, :1] * LOG2_E)
            dp = _nt(dob, vv)
            dsum = jnp.sum(dob.astype(F32) * o_ref[rows, :].astype(F32), axis=-1, keepdims=True)
            ds = (p * (dp - dsum)).astype(BF16)
            dq_ref[rows, :] += _nn(ds, kk)
            return dk + _tn(ds, qq), dv + _tn(p.astype(BF16), dob)

        carry = block(ki, (jnp.zeros((t, QK_PAD), F32), jnp.zeros((t, V_DIM), F32)), True)
        rest = nb - 1 - ki
        carry = lax.fori_loop(
            0, rest // 2, lambda j, c: block(ki + 2 * j + 2, block(ki + 2 * j + 1, c, False), False), carry)
        dk, dv = lax.cond(rest % 2 == 1, lambda c: block(nb - 1, c, False), lambda c: c, carry)
        dk_ref[...] = dk * SM_SCALE
        dv_ref[...] = dv

        @pl.when(ki == nb - 1)
        def _():
            dq_ref[...] = dq_ref[...] * SM_SCALE

    def head(h, ki):
        return (0, h)

    def kblock(h, ki):
        return (ki, h)

    return _pcall(
        body, (q, k, v, do, o, lse), deps, name=name, grid=(N_HEADS, nb),
        out_shape=[jax.ShapeDtypeStruct((s, N_HEADS * QK_PAD), F32), jax.ShapeDtypeStruct((s, N_HEADS * QK_PAD), F32),
                   jax.ShapeDtypeStruct((s, N_HEADS * V_DIM), F32)],
        in_specs=[pl.BlockSpec((s, QK_PAD), head), pl.BlockSpec((t, QK_PAD), kblock), pl.BlockSpec((t, V_DIM), kblock),
                  pl.BlockSpec((s, V_DIM), head), pl.BlockSpec((s, V_DIM), head), pl.BlockSpec((s, LANES), head)],
        out_specs=[pl.BlockSpec((s, QK_PAD), head), pl.BlockSpec((t, QK_PAD), kblock), pl.BlockSpec((t, V_DIM), kblock)],
        compiler_params=_cparams(2, VMEM_MID),
    )


def _o_fwd(x, o, wo, name):
    s = x.shape[0]
    t = min(512, s)

    def body(x_ref, o_ref, wo_ref, xo_ref):
        xo_ref[...] = x_ref[...] + _nn(o_ref[...], wo_ref[...])

    return pl.pallas_call(
        body, name=name, grid=(s // t,),
        out_shape=jax.ShapeDtypeStruct((s, D_MODEL), F32),
        in_specs=[_rows(t, D_MODEL), _rows(t, D_MODEL), _full((D_MODEL, D_MODEL))],
        out_specs=_rows(t, D_MODEL),
        compiler_params=_cparams(1, VMEM_MID),
    )(x, o, wo)


def _o_bwd(dx, wo, name, deps=()):
    s = dx.shape[0]
    t = min(512, s)

    def body(dx_ref, wo_ref, do_ref, dxb_ref):
        dxb = dx_ref[...].astype(BF16)
        dxb_ref[...] = dxb
        do_ref[...] = _nt(dxb, wo_ref[...]).astype(BF16)

    tok = jax.ShapeDtypeStruct((s, D_MODEL), BF16)
    return _pcall(
        body, (dx, wo), deps, name=name, grid=(s // t,),
        out_shape=[tok, tok],
        in_specs=[_rows(t, D_MODEL), _full((D_MODEL, D_MODEL))],
        out_specs=[_rows(t, D_MODEL), _rows(t, D_MODEL)],
        compiler_params=_cparams(1, VMEM_MID),
    )


def _loss_head(y, target, name):
    s = y.shape[0]
    t = min(512, s)

    def body(y_ref, t_ref, dy_ref, sq_ref):
        e = y_ref[...] - t_ref[...]
        dy_ref[...] = e * (1.0 / D_MODEL)

        @pl.when(pl.program_id(0) == 0)
        def _():
            sq_ref[...] = jnp.zeros_like(sq_ref)

        sq_ref[...] += jnp.sum(e * e, axis=0, keepdims=True)

    return pl.pallas_call(
        body, name=name, grid=(s // t,),
        out_shape=[jax.ShapeDtypeStruct((s, D_MODEL), F32), jax.ShapeDtypeStruct((1, D_MODEL), F32)],
        in_specs=[_rows(t, D_MODEL), _rows(t, D_MODEL)],
        out_specs=[_rows(t, D_MODEL), _full((1, D_MODEL))],
        compiler_params=_cparams(1),
    )(y, target)


def _adamw(w, g, m, v, name):
    shape = w.shape
    c = shape[-1]
    r = math.prod(shape[:-1])
    tb = r
    for cand in (512, 256, 128):
        if r % cand == 0 and r > cand:
            tb = cand
            break

    def body(w_ref, g_ref, m_ref, v_ref, d_ref, mo_ref, vo_ref):
        gr = g_ref[...]
        mn = ADAM_B1 * m_ref[...] + (1.0 - ADAM_B1) * gr
        vn = ADAM_B2 * v_ref[...] + (1.0 - ADAM_B2) * (gr * gr)
        m_hat = mn / (1.0 - ADAM_B1 ** ADAM_STEP)
        v_hat = vn / (1.0 - ADAM_B2 ** ADAM_STEP)
        d_ref[...] = -ADAM_LR * (m_hat / (jnp.sqrt(v_hat) + ADAM_EPS) + ADAM_WD * w_ref[...])
        mo_ref[...] = mn
        vo_ref[...] = vn

    spec = pl.BlockSpec((tb, c), lambda i: (i, 0))
    flat = jax.ShapeDtypeStruct((r, c), F32)
    outs = pl.pallas_call(
        body, name=name, grid=(r // tb,),
        out_shape=[flat, flat, flat],
        in_specs=[spec] * 4, out_specs=[spec] * 3,
        compiler_params=_cparams(1),
    )(w.reshape(r, c), g.reshape(r, c), m.reshape(r, c), v.reshape(r, c))
    return [a.reshape(shape) for a in outs]


def _pad_cols(a, width):
    return jnp.pad(a, [(0, 0)] * (a.ndim - 1) + [(0, width - a.shape[-1])])


def _owner_view(a, sz):
    return a.reshape(a.shape[0], N_CHIPS, 2, sz, a.shape[-1])


def kernel(x, positions, ln_mix_a, w_pool, b_pool, pool_scale, ln_ffn, w_gate, w_up, w_down, ln_kv, w_dkv, g_kv_latent, w_uk, w_uv, g_k, ln_mix_b, w_dq, g_q_latent, w_uq, g_q, w_o, loss_target, m_ln_mix_a, m_w_pool, m_b_pool, m_pool_scale, m_ln_ffn, m_w_gate, m_w_up, m_w_down, m_ln_kv, m_w_dkv, m_g_kv_latent, m_w_uk, m_w_uv, m_g_k, m_ln_mix_b, m_w_dq, m_g_q_latent, m_w_uq, m_g_q, m_w_o, v_ln_mix_a, v_w_pool, v_b_pool, v_pool_scale, v_ln_ffn, v_w_gate, v_w_up, v_w_down, v_ln_kv, v_w_dkv, v_g_kv_latent, v_w_uk, v_w_uv, v_g_k, v_ln_mix_b, v_w_dq, v_g_q_latent, v_w_uq, v_g_q, v_w_o):
    weights = dict(ln_mix_a=ln_mix_a, w_pool=w_pool, b_pool=b_pool, pool_scale=pool_scale, ln_ffn=ln_ffn,
                   w_gate=w_gate, w_up=w_up, w_down=w_down, ln_kv=ln_kv, w_dkv=w_dkv, g_kv_latent=g_kv_latent,
                   w_uk=w_uk, w_uv=w_uv, g_k=g_k, ln_mix_b=ln_mix_b, w_dq=w_dq, g_q_latent=g_q_latent,
                   w_uq=w_uq, g_q=g_q, w_o=w_o)
    mom1 = dict(ln_mix_a=m_ln_mix_a, w_pool=m_w_pool, b_pool=m_b_pool, pool_scale=m_pool_scale, ln_ffn=m_ln_ffn,
                w_gate=m_w_gate, w_up=m_w_up, w_down=m_w_down, ln_kv=m_ln_kv, w_dkv=m_w_dkv,
                g_kv_latent=m_g_kv_latent, w_uk=m_w_uk, w_uv=m_w_uv, g_k=m_g_k, ln_mix_b=m_ln_mix_b, w_dq=m_w_dq,
                g_q_latent=m_g_q_latent, w_uq=m_w_uq, g_q=m_g_q, w_o=m_w_o)
    mom2 = dict(ln_mix_a=v_ln_mix_a, w_pool=v_w_pool, b_pool=v_b_pool, pool_scale=v_pool_scale, ln_ffn=v_ln_ffn,
                w_gate=v_w_gate, w_up=v_w_up, w_down=v_w_down, ln_kv=v_ln_kv, w_dkv=v_w_dkv,
                g_kv_latent=v_g_kv_latent, w_uk=v_w_uk, w_uv=v_w_uv, g_k=v_g_k, ln_mix_b=v_ln_mix_b, w_dq=v_w_dq,
                g_q_latent=v_g_q_latent, w_uq=v_w_uq, g_q=v_g_q, w_o=v_w_o)
    names = list(weights)
    dev = 4 * lax.axis_index("x") + 2 * lax.axis_index("y") + lax.axis_index("c")
    core = lax.axis_index("c").astype(jnp.int32).reshape(1)
    chip = (2 * lax.axis_index("x") + lax.axis_index("y")).astype(jnp.int32).reshape(1)

    xs = x[0]
    target = loss_target[0]
    cos, sin = _rope_tables(positions[0])

    def placed(shard):
        buf = lax.empty((shard.shape[0], N_DEV) + shard.shape[1:], shard.dtype)
        return lax.dynamic_update_slice(buf, shard[:, None], (0, dev, 0, 0))

    groups = {f"ffn{l}": [placed(jnp.stack([w_gate[l].T, w_up[l].T, w_down[l]]).astype(BF16))] for l in range(4)}
    groups["att"] = [placed(a.astype(BF16)) for a in (
        w_dkv[None, :, :KV_RANK], _pad_cols(w_dkv[None, :, KV_RANK:], LANES), w_uk[None], w_uv[None],
        w_dq, _pad_cols(w_uq, QK_PAD), w_o)]
    small_sh = jnp.concatenate([ln_mix_a.reshape(1, -1), pool_scale.reshape(1, -1), b_pool.reshape(1, -1)], axis=1)
    wp_g, small_g = _all_gather([w_pool.astype(BF16), small_sh], [2, 0], "gather_first")
    wp_all = wp_g.reshape(2, 4, GROUP_DIM, GROUP_DIM)
    small_g = small_g.reshape(N_DEV, 3, 2, LANES)
    ln_a_all = small_g[:, 0].transpose(1, 0, 2).reshape(2, 1, D_MODEL)
    sc_all = small_g[:, 1].transpose(1, 0, 2).reshape(2, 1, D_MODEL)
    bp_all = small_g[:, 2].reshape(N_DEV, 2, 4, 32).transpose(1, 2, 0, 3).reshape(2, 1, D_MODEL)
    sp0 = _copies_start(groups["ffn0"], 1, _gather_spread, "spread_ffn0", deps=[small_g])

    def spread_start(nm, deps):
        return _copies_start(groups[nm], len(groups[nm]), _gather_spread, f"spread_{nm}", deps=deps)

    def spread_wait(nm, state, after):
        ssem, rsem, bufs, _ = state
        return _copies_wait(bufs, ssem, rsem, after, _blocks_moved(4), f"spread_done_{nm}")

    def relay_start(nm, bufs, deps=()):
        return _copies_start(bufs, len(bufs), _gather_relay, f"relay_{nm}", deps=deps)

    def relay_wait(nm, state, after):
        ssem, rsem, bufs, _ = state
        return _copies_wait(bufs, ssem, rsem, after, _blocks_moved(3), f"relay_done_{nm}")

    gkn = g_k[:NOPE].reshape(1, NOPE)
    gkr = _pad_cols(g_k[NOPE:].reshape(1, ROPE), LANES)
    gl = g_kv_latent.reshape(1, KV_RANK)
    lnkv = ln_kv.reshape(1, D_MODEL)

    x_in, x_mid, pooled, gates, ups, w_ffn = [], [], [], [], [], []
    qs, outs, lses = [], [], []

    def mixer(l, cur, deps):
        x_in.append(cur)
        mid, dsave = _mix_fwd(cur, ln_a_all[l], wp_all[l], bp_all[l], sc_all[l], f"mix_fwd{l}", deps=deps)
        pooled.append(dsave)
        x_mid.append(mid)
        return mid

    def q_args(j):
        return (ln_mix_b[j].reshape(1, -1), wdq_all[j], g_q_latent[j].reshape(1, -1), wuq_all[j],
                g_q[j, :NOPE].reshape(1, -1), _pad_cols(g_q[j, NOPE:].reshape(1, -1), LANES), cos, sin)

    def attention(j, cur, deps):
        x_in.append(cur)
        q = _q_fwd(cur, *q_args(j), f"q_fwd{j}", deps=deps)
        o, lse = _att_fwd(q, k_sh, v_sh, f"att_fwd{j}")
        mid = _o_fwd(cur, o, wo_all[j], f"o_fwd{j}")
        qs.append(q)
        outs.append(o)
        lses.append(lse)
        x_mid.append(mid)
        return mid

    def ffn(l, mid, relayed):
        w_l = relayed[0].reshape(3, D_FF, D_MODEL)
        w_ffn.append(w_l)
        cur, gate, up = _ffn_fwd(mid, ln_ffn[l].reshape(1, -1), w_l, f"ffn_fwd{l}")
        gates.append(gate)
        ups.append(up)
        return cur

    mid = mixer(0, xs, [sp0[3]])
    landed0 = spread_wait("ffn0", sp0, mid)
    sp1 = spread_start("ffn1", [landed0[0]])
    rl0 = relay_start("ffn0", landed0, [sp1[3]])
    cur = ffn(0, mid, relay_wait("ffn0", rl0, rl0[3]))

    landed1 = spread_wait("ffn1", sp1, cur)
    sp_att = spread_start("att", [landed1[0]])
    sp2 = spread_start("ffn2", [landed1[0]])
    rl1 = relay_start("ffn1", landed1, [sp_att[3], sp2[3]])
    mid = mixer(1, cur, [rl1[3]])
    cur = ffn(1, mid, relay_wait("ffn1", rl1, mid))
    x_kv = cur

    landed_att = spread_wait("att", sp_att, cur)
    landed2 = spread_wait("ffn2", sp2, cur)
    sp3 = spread_start("ffn3", [landed2[0]])
    rl_att = relay_start("att", landed_att, [sp3[3]])
    rl2 = relay_start("ffn2", landed2, [sp3[3]])
    att_bufs = relay_wait("att", rl_att, rl2[3])
    wc = att_bufs[0].reshape(D_MODEL, KV_RANK)
    wpe = att_bufs[1].reshape(D_MODEL, LANES)
    wuk_g = att_bufs[2].reshape(N_HEADS, KV_RANK, NOPE).transpose(1, 0, 2).reshape(KV_RANK, N_HEADS * NOPE)
    wuv_g = att_bufs[3].reshape(N_HEADS, KV_RANK, V_DIM).transpose(1, 0, 2).reshape(KV_RANK, N_HEADS * V_DIM)
    wdq_all = att_bufs[4].reshape(2, D_MODEL, Q_RANK)
    wuq_all = att_bufs[5]
    wo_all = att_bufs[6].reshape(2, D_MODEL, D_MODEL)
    k_sh, v_sh = _kv_fwd(cur, lnkv, wc, wpe, gl, wuk_g, wuv_g, gkn, gkr, cos, sin, "kv_fwd")
    mid = attention(0, cur, [])
    cur = ffn(2, mid, relay_wait("ffn2", rl2, mid))

    landed3 = spread_wait("ffn3", sp3, cur)
    rl3 = relay_start("ffn3", landed3)
    mid = attention(1, cur, [rl3[3]])
    cur = ffn(3, mid, relay_wait("ffn3", rl3, mid))

    dx, sq_cols = _loss_head(cur, target, "loss_head")

    small = {}
    sizes = dict(ffn0=FF_SHARD, ffn1=FF_SHARD, ffn2=FF_SHARD, ffn3=FF_SHARD, wo=128, kv512=128, dkv_pe=128,
                 wdq=128, wuqT=QK_PAD, wpool=32)
    big = dict(wo=lax.empty((2, D_MODEL, D_MODEL), BF16), kv512=lax.empty((3, D_MODEL, KV_RANK), BF16),
               dkv_pe=lax.empty((1, D_MODEL, LANES), BF16), wdq=lax.empty((2, D_MODEL, Q_RANK), BF16),
               wuqT=lax.empty((2, N_HEADS * QK_PAD, Q_RANK), BF16), wpool=lax.empty((8, GROUP_DIM, GROUP_DIM), BF16))
    for l in range(4):
        big[f"ffn{l}"] = lax.empty((3, D_FF, D_MODEL), BF16)
    red = {}

    def pair_start(nms, tag):
        arrs = []
        for nm in nms:
            view = _owner_view(big[nm], sizes[nm])
            arrs += [view, lax.empty((view.shape[0], N_CHIPS) + view.shape[3:], BF16)]
        return nms, tag, _copies_start(arrs, len(nms), _pair_send, f"pair_start_{tag}")

    def chip_start(state, after):
        nms, tag, (ssem, rsem, arrs, _) = state
        arrs = _copies_wait(arrs, ssem, rsem, after, _landed, f"pair_done_{tag}")
        out = []
        for t, nm in enumerate(nms):
            part = _pair_sum(arrs[2 * t], arrs[2 * t + 1], core, f"pair_sum_{nm}")
            out += [part, lax.empty((3, part.shape[0]) + part.shape[2:], BF16)]
        return nms, tag, _copies_start(out, len(nms), _chip_send, f"chip_start_{tag}")

    deferred = []

    def chip_finish(state, after, defer=False):
        nms, tag, (ssem, rsem, arrs, _) = state
        arrs = _copies_wait(arrs, ssem, rsem, after, _landed, f"chip_done_{tag}")
        for t, nm in enumerate(nms):
            if defer:
                deferred.append((nm, arrs[2 * t], arrs[2 * t + 1]))
            else:
                red[nm] = _chip_sum(arrs[2 * t], arrs[2 * t + 1], chip, f"chip_sum_{nm}")

    ffn_grads = {nm: lax.empty((4, FF_SHARD, D_MODEL), F32) for nm in ("w_gate", "w_up", "w_down")}

    def place_ffn_grads(l):
        g = red[f"ffn{l}"]
        for k, nm in enumerate(("w_gate", "w_up", "w_down")):
            ffn_grads[nm] = ffn_grads[nm].at[l].set(g[k])

    dks, dvs = [], []
    pending = None
    bwd_deps = []
    for l in (3, 2, 1, 0):
        key = f"ffn{l}"
        dx, act, dgb, dub, hn, dyb, dln = _ffn_bwd(x_mid[l], dx, gates[l], ups[l], ln_ffn[l].reshape(1, -1),
                                                     w_ffn[l], f"ffn_bwd{l}", deps=bwd_deps)
        bwd_deps = []
        small[f"ln_ffn{l}"] = dln
        if l == 1:
            att_chip = chip_start(att_pair, dx)
            tn_deps = [att_chip[2][3]]
        else:
            tn_deps = []
        if pending:
            chip_finish(pending, dx, defer=True)
            pending = None
        big[key] = _tn_matmul(dgb, hn, big[key], 0, f"dw_gate{l}", m_chunk=FF_HALF, deps=tn_deps)
        big[key] = _tn_matmul(dub, hn, big[key], 1, f"dw_up{l}", m_chunk=FF_HALF)
        big[key] = _tn_matmul(act, dyb, big[key], 2, f"dw_down{l}", m_chunk=FF_HALF)
        if l == 1:
            chip_finish(att_chip, big[key])
        ffn_pair = pair_start([key], key)
        if l >= 2:
            j = l - 2
            do, dxb = _o_bwd(dx, wo_all[j], f"o_bwd{j}", deps=[ffn_pair[2][3]])
            big["wo"] = _tn_matmul(outs[j], dxb, big["wo"], j, f"dw_o{j}")
            ffn_chip = chip_start(ffn_pair, big["wo"])
            dq, dk, dv = _att_bwd(qs[j], k_sh, v_sh, do, outs[j], lses[j], f"att_bwd{j}", deps=[ffn_chip[2][3]])
            chip_finish(ffn_chip, dq, defer=True)
            dks.append(dk)
            dvs.append(dv)
            dx, hnq, cqn, dqa, dcq, dln, dgql, dgqn, dgqr = _q_bwd(x_in[l], dx, dq, *q_args(j), f"q_bwd{j}")
            small[f"ln_mix_b{j}"] = dln
            small[f"g_q_latent{j}"] = dgql
            small[f"g_q{j}"] = jnp.concatenate([dgqn, dgqr[:, :ROPE]], axis=1)
            big["wdq"] = _tn_matmul(hnq, dcq, big["wdq"], j, f"dw_dq{j}")
            big["wuqT"] = _tn_matmul(dqa, cqn, big["wuqT"], j, f"dw_uq{j}")
            if l == 2:
                (dx, hnk, cn, dknb, dvb, dccb, dpeb, dlnkv, dgl, dgkn, dgkr) = _kv_bwd(
                    x_kv, dx, dks, dvs, lnkv, wc, wpe, gl, wuk_g, wuv_g, gkn, gkr, cos, sin, "kv_bwd")
                small["ln_kv"] = dlnkv
                small["g_kv_latent"] = dgl
                small["g_k"] = jnp.concatenate([dgkn, dgkr[:, :ROPE]], axis=1)
                big["kv512"] = _tn_matmul(dknb, cn, big["kv512"], 0, "dw_uk")
                big["kv512"] = _tn_matmul(dvb, cn, big["kv512"], 1, "dw_uv")
                big["kv512"] = _tn_matmul(hnk, dccb, big["kv512"], 2, "dw_dkv_c")
                big["dkv_pe"] = _tn_matmul(hnk, dpeb, big["dkv_pe"], 0, "dw_dkv_pe")
                att_pair = pair_start(["wo", "kv512", "dkv_pe", "wdq", "wuqT"], "att")
                bwd_deps = [att_pair[2][3]]
        else:
            dx, dyp, dsc, db, dln = _mix_bwd(x_in[l], dx, pooled[l], ln_a_all[l], wp_all[l], bp_all[l], sc_all[l],
                                             f"mix_bwd{l}", deps=[ffn_pair[2][3]])
            small[f"ln_mix_a{l}"] = dln
            small[f"pool_scale{l}"] = dsc
            small[f"b_pool{l}"] = db
            ffn_chip = chip_start(ffn_pair, dx)
            big["wpool"] = _tn_matmul(pooled[l], dyp, big["wpool"], 4 * l, f"dw_pool{l}", groups=4,
                                      deps=[ffn_chip[2][3]])
            if l == 1:
                pending = ffn_chip
                bwd_deps = [ffn_chip[2][3]]
            else:
                for nm, part, land in deferred:
                    red[nm] = _chip_sum(part, land, chip, f"chip_sum_{nm}", deps=[ffn_chip[2][3]])
                    place_ffn_grads(int(nm[-1]))
                chip_finish(ffn_chip, [big["wpool"]] + list(ffn_grads.values()))
                place_ffn_grads(0)
    grad_x = dx[None]
    pool_pair = pair_start(["wpool"], "wpool")
    pool_chip = chip_start(pool_pair, pool_pair[2][3])
    chip_finish(pool_chip, pool_chip[2][3])

    vec_names = (["loss"] + [f"ln_ffn{l}" for l in range(4)] + ["ln_kv", "g_kv_latent", "g_k"]
                 + [f"{p}{j}" for p in ("ln_mix_b", "g_q_latent", "g_q") for j in range(2)]
                 + [f"{p}{l}" for p in ("ln_mix_a", "pool_scale", "b_pool") for l in range(2)])
    small["loss"] = sq_cols
    widths = [small[nm].shape[1] for nm in vec_names]
    padded = [-(-w // LANES) * LANES for w in widths]
    packed = jnp.concatenate([_pad_cols(small[nm], pw) for nm, pw in zip(vec_names, padded)], axis=1)
    (all_vecs,) = _all_gather([packed], [0], "gather_vectors")
    total = _sum_lead(all_vecs, "sum_vectors")
    vec = {}
    off = 0
    for nm, w, pw in zip(vec_names, widths, padded):
        vec[nm] = total[0, off:off + w]
        off += pw
    loss = 0.5 * jnp.sum(vec["loss"]) * (1.0 / D_MODEL)

    def own_cols(full, width):
        return lax.dynamic_slice_in_dim(full, dev * width, width, axis=full.ndim - 1)

    grads = dict(
        ln_mix_a=own_cols(jnp.stack([vec["ln_mix_a0"], vec["ln_mix_a1"]]), LANES),
        w_pool=red["wpool"].reshape(2, 4, 32, GROUP_DIM),
        b_pool=own_cols(jnp.stack([vec["b_pool0"], vec["b_pool1"]]).reshape(2, 4, GROUP_DIM), 32),
        pool_scale=own_cols(jnp.stack([vec["pool_scale0"], vec["pool_scale1"]]), LANES),
        ln_ffn=jnp.stack([vec[f"ln_ffn{l}"] for l in range(4)]),
        w_gate=ffn_grads["w_gate"],
        w_up=ffn_grads["w_up"],
        w_down=ffn_grads["w_down"],
        ln_kv=vec["ln_kv"],
        w_dkv=jnp.concatenate([red["kv512"][2], red["dkv_pe"][0][:, :ROPE]], axis=1),
        g_kv_latent=vec["g_kv_latent"],
        w_uk=red["kv512"][0].T,
        w_uv=red["kv512"][1].T,
        g_k=vec["g_k"],
        ln_mix_b=jnp.stack([vec["ln_mix_b0"], vec["ln_mix_b1"]]),
        w_dq=red["wdq"],
        g_q_latent=jnp.stack([vec["g_q_latent0"], vec["g_q_latent1"]]),
        w_uq=red["wuqT"].transpose(0, 2, 1)[:, :, :QK_DIM],
        g_q=jnp.stack([vec["g_q0"], vec["g_q1"]]),
        w_o=red["wo"],
    )

    deltas, new_m, new_v = {}, {}, {}
    for nm in names:
        w = weights[nm]
        if nm in ("w_gate", "w_up"):
            def swap(a):
                return a.transpose(0, 2, 1)
            d, mo, vo = _adamw(swap(w), grads[nm], swap(mom1[nm]), swap(mom2[nm]), f"adamw_{nm}")
            deltas[nm], new_m[nm], new_v[nm], grads[nm] = swap(d), swap(mo), swap(vo), swap(grads[nm])
            continue
        shape = w.shape if w.ndim > 1 else (1, w.shape[0])
        d, mo, vo = _adamw(w.reshape(shape), grads[nm].reshape(shape), mom1[nm].reshape(shape),
                           mom2[nm].reshape(shape), f"adamw_{nm}")
        deltas[nm], new_m[nm], new_v[nm] = d.reshape(w.shape), mo.reshape(w.shape), vo.reshape(w.shape)

    return (loss, grad_x, *[grads[nm].reshape(weights[nm].shape) for nm in names], *[deltas[nm] for nm in names],
            *[new_m[nm] for nm in names], *[new_v[nm] for nm in names])
```

```python
import functools
import math

import jax
import jax.numpy as jnp
from jax import lax
from jax.experimental import pallas as pl
from jax.experimental.pallas import tpu as pltpu

F32 = jnp.float32
BF16 = jnp.bfloat16
MESH = pl.DeviceIdType.MESH

D_MODEL = 1024
D_FF = 2816
N_DEV = 8
N_CHIPS = 4
FF_SHARD = D_FF // N_DEV
FF_HALF = D_FF // 2
N_HEADS = 8
NOPE = 128
ROPE = 64
QK_DIM = NOPE + ROPE
QK_PAD = 256
V_DIM = 128
Q_RANK = 256
KV_RANK = 512
POOL_WINDOWS = (2, 4, 8, 16)
GROUP_DIM = 256
HALO = 128
CHUNK = 64
ROPE_THETA = 10000.0
EPS = 1e-6
LANES = 128

ADAM_LR = 0.001
ADAM_B1 = 0.9
ADAM_B2 = 0.999
ADAM_EPS = 1e-08
ADAM_WD = 0.01
ADAM_STEP = 10

PROJ_ROWS = 256
VMEM_BIG = 56 * 2**20
VMEM_MID = 40 * 2**20


def _nn(a, b):
    return lax.dot_general(a, b, (((1,), (0,)), ((), ())), preferred_element_type=F32)


def _nt(a, b):
    return lax.dot_general(a, b, (((1,), (1,)), ((), ())), preferred_element_type=F32)


def _tn(a, b):
    return lax.dot_general(a, b, (((0,), (0,)), ((), ())), preferred_element_type=F32)


def _rms(x, g, n):
    r = lax.rsqrt(jnp.sum(x * x, axis=-1, keepdims=True) * (1.0 / n) + EPS)
    return (x * r) * g, r


def _rms_bwd(x, r, g, dy, n):
    u = dy * g
    s = jnp.sum(x * u, axis=-1, keepdims=True) * (1.0 / n)
    dx = r * u - x * (r * r * r * s)
    dg = jnp.sum(dy * (x * r), axis=0, keepdims=True)
    return dx, dg


def _swap_perm():
    i = lax.broadcasted_iota(jnp.int32, (LANES, LANES), 0)
    j = lax.broadcasted_iota(jnp.int32, (LANES, LANES), 1)
    half = ROPE // 2
    hit = ((j < half) & (i == j + half)) | ((j >= half) & (j < ROPE) & (i == j - half))
    return jnp.where(hit, 1.0, 0.0).astype(BF16)


def _swap_halves(z, perm):
    hi = z.astype(BF16)
    lo = (z - hi.astype(F32)).astype(BF16)
    return _nn(hi, perm) + _nn(lo, perm)


def _sigmoid(x):
    return 1.0 / (1.0 + jnp.exp(-x))


def _cparams(n_grid, vmem=None):
    return pltpu.CompilerParams(dimension_semantics=("arbitrary",) * n_grid, vmem_limit_bytes=vmem)


def _rows(t, cols):
    return pl.BlockSpec((t, cols), lambda i: (i, 0))


def _full(shape):
    nd = len(shape)
    return pl.BlockSpec(shape, lambda *_: (0,) * nd)


ANY = pl.BlockSpec(memory_space=pl.ANY)


def _pcall(body, args, deps, *, in_specs, **kw):
    n_in, n_dep = len(args), len(deps)

    def ordered(*refs):
        body(*refs[:n_in], *refs[n_in + n_dep:])

    return pl.pallas_call(ordered, in_specs=list(in_specs) + [ANY] * n_dep, **kw)(*args, *deps)


def _place():
    x, y, c = lax.axis_index("x"), lax.axis_index("y"), lax.axis_index("c")
    return x, y, c


def _all_gather(shards, axes, name, deps=()):
    n, nd = len(shards), len(deps)
    out_shape = [jax.ShapeDtypeStruct(s.shape[:a] + (N_DEV,) + s.shape[a:], s.dtype) for s, a in zip(shards, axes)]

    def body(*refs):
        ins, outs = refs[:n], refs[n + nd:2 * n + nd]
        send_sems, recv_sems, local_sems = refs[2 * n + nd:]
        x, y, c = _place()
        me, sibling = (x, y, c), (x, y, 1 - c)
        chips = [(1 - x, y), (x, 1 - y), (1 - x, 1 - y)]

        def slot(t, dev):
            idx = 4 * dev[0] + 2 * dev[1] + dev[2]
            return outs[t].at[(slice(None),) * axes[t] + (idx,)]

        def copy(t, k, block, to, src=None):
            return pltpu.make_async_remote_copy(
                src_ref=slot(t, block) if src is None else src, dst_ref=slot(t, block),
                send_sem=send_sems.at[t, k], recv_sem=recv_sems.at[t, k],
                device_id=to, device_id_type=MESH)

        mine = [pltpu.make_async_copy(ins[t], slot(t, me), local_sems.at[t]) for t in range(n)]
        for cp in mine:
            cp.start()
        first = []
        for t in range(n):
            first.append(copy(t, 0, me, sibling, src=ins[t]))
            first += [copy(t, 1 + j, me, (*chip, c), src=ins[t]) for j, chip in enumerate(chips)]
        for cp in first:
            cp.start()
        passed = []
        for j, chip in enumerate(chips):
            for t in range(n):
                copy(t, 1 + j, (*chip, c), me).wait_recv()
                cp = copy(t, 4 + j, (*chip, c), sibling)
                cp.start()
                passed.append(cp)
        for t in range(n):
            copy(t, 0, sibling, me).wait_recv()
            for j, chip in enumerate(chips):
                copy(t, 4 + j, (*chip, 1 - c), me).wait_recv()
        for cp in first + passed:
            cp.wait_send()
        for cp in mine:
            cp.wait()

    return pl.pallas_call(
        body, name=name, out_shape=out_shape,
        in_specs=[ANY] * (n + nd), out_specs=[ANY] * n,
        scratch_shapes=[pltpu.SemaphoreType.DMA((n, 7)), pltpu.SemaphoreType.DMA((n, 7)),
                        pltpu.SemaphoreType.DMA((n,))],
    )(*shards, *deps)


HBM = pl.BlockSpec(memory_space=pltpu.HBM)
SEM = pl.BlockSpec(memory_space=pltpu.SEMAPHORE)
EFFECT = pltpu.SideEffectType.DATAFLOW_SIDE_EFFECTING


def _copies_start(arrays, n_sems, plan, name, deps=()):
    n, nd = len(arrays), len(deps)

    def body(*refs):
        for cp in plan(refs[:n], refs[n + nd], refs[n + nd + 1]):
            cp.start()
        refs[-1][...] = jnp.zeros_like(refs[-1])

    outs = pl.pallas_call(
        body, name=name,
        out_shape=(pltpu.SemaphoreType.DMA((n_sems,)), pltpu.SemaphoreType.DMA((n_sems,)),
                   *[pltpu.HBM(a.shape, a.dtype) for a in arrays], jax.ShapeDtypeStruct((8, LANES), F32)),
        in_specs=[HBM] * n + [ANY] * nd,
        out_specs=(SEM, SEM, *[HBM] * n, pl.BlockSpec(memory_space=pltpu.VMEM)),
        input_output_aliases={i: 2 + i for i in range(n)},
        compiler_params=pltpu.CompilerParams(has_side_effects=EFFECT),
    )(*[pltpu.with_memory_space_constraint(a, pltpu.HBM) for a in arrays], *deps)
    return outs[0], outs[1], list(outs[2:2 + n]), outs[-1]


def _copies_wait(arrays, send_sems, recv_sems, after, plan, name):
    n = len(arrays)
    after = list(after) if isinstance(after, (list, tuple)) else [after]

    def body(*refs):
        for cp in plan(refs[:n], refs[n], refs[n + 1]):
            cp.wait_send()
            cp.wait_recv()

    outs = pl.pallas_call(
        body, name=name,
        out_shape=tuple(pltpu.HBM(a.shape, a.dtype) for a in arrays),
        in_specs=[HBM] * n + [SEM, SEM] + [ANY] * len(after), out_specs=tuple([HBM] * n),
        input_output_aliases={i: i for i in range(n)},
        compiler_params=pltpu.CompilerParams(has_side_effects=EFFECT),
    )(*arrays, send_sems, recv_sems, *after)
    return list(outs)


def _remote(src, dst, send_sems, recv_sems, t, to):
    return pltpu.make_async_remote_copy(src_ref=src, dst_ref=dst, send_sem=send_sems.at[t], recv_sem=recv_sems.at[t],
                                        device_id=to, device_id_type=MESH)


def _dev_index(x, y, c):
    return 4 * x + 2 * y + c


def _gather_spread(bufs, send_sems, recv_sems):
    x, y, c = _place()
    mine = _dev_index(x, y, c)
    peers = [(x, y, 1 - c), (1 - x, y, c), (x, 1 - y, c), (1 - x, 1 - y, c)]
    return [_remote(g.at[k, mine], g.at[k, mine], send_sems, recv_sems, t, peer)
            for t, g in enumerate(bufs) for peer in peers for k in range(g.shape[0])]


def _gather_relay(bufs, send_sems, recv_sems):
    x, y, c = _place()
    blocks = [_dev_index(1 - x, y, c), _dev_index(x, 1 - y, c), _dev_index(1 - x, 1 - y, c)]
    return [_remote(g.at[k, b], g.at[k, b], send_sems, recv_sems, t, (x, y, 1 - c))
            for t, g in enumerate(bufs) for b in blocks for k in range(g.shape[0])]


def _blocks_moved(count):
    def plan(bufs, send_sems, recv_sems):
        x, y, c = _place()
        return [_remote(g.at[:, pl.ds(0, count)], g.at[:, pl.ds(0, count)], send_sems, recv_sems, t, (x, y, 1 - c))
                for t, g in enumerate(bufs)]
    return plan


def _pair_send(arrs, send_sems, recv_sems):
    x, y, c = _place()
    return [_remote(arrs[2 * t].at[p, k, 1 - c], arrs[2 * t + 1].at[p, k], send_sems, recv_sems, t, (x, y, 1 - c))
            for t in range(len(arrs) // 2) for p in range(arrs[2 * t].shape[0]) for k in range(N_CHIPS)]


def _chip_send(arrs, send_sems, recv_sems):
    x, y, c = _place()
    chips = [(1 - x, y), (x, 1 - y), (1 - x, 1 - y)]
    return [_remote(arrs[2 * t].at[p, 2 * px + py], arrs[2 * t + 1].at[j, p], send_sems, recv_sems, t, (px, py, c))
            for t in range(len(arrs) // 2) for j, (px, py) in enumerate(chips) for p in range(arrs[2 * t].shape[0])]


def _landed(arrs, send_sems, recv_sems):
    x, y, c = _place()
    return [_remote(arrs[2 * t + 1], arrs[2 * t + 1], send_sems, recv_sems, t, (x, y, 1 - c))
            for t in range(len(arrs) // 2)]


def _rows_per_step(rows, row_elems):
    best = 1
    for cand in range(1, rows + 1):
        if rows % cand == 0 and cand * row_elems <= 256 * 1024:
            best = cand
    return best


def _pair_sum(grad, landed, core, name):
    p, _, _, sz, c = grad.shape
    r = _rows_per_step(p * N_CHIPS, sz * c)

    def body(core_ref, g_ref, l_ref, o_ref):
        o_ref[...] = (g_ref[...].astype(F32) + l_ref[...].astype(F32)).astype(o_ref.dtype)

    out = pl.pallas_call(
        body, name=name,
        grid_spec=pltpu.PrefetchScalarGridSpec(
            num_scalar_prefetch=1, grid=(p * N_CHIPS // r,),
            in_specs=[pl.BlockSpec((r, None, sz, c), lambda i, cr: (i, cr[0], 0, 0)),
                      pl.BlockSpec((r, sz, c), lambda i, cr: (i, 0, 0))],
            out_specs=pl.BlockSpec((r, sz, c), lambda i, cr: (i, 0, 0))),
        out_shape=jax.ShapeDtypeStruct((p * N_CHIPS, sz, c), grad.dtype),
        compiler_params=_cparams(1),
    )(core, grad.reshape(p * N_CHIPS, 2, sz, c), landed.reshape(p * N_CHIPS, sz, c))
    return out.reshape(p, N_CHIPS, sz, c)


def _chip_sum(parts, landed, chip, name, deps=()):
    p, _, sz, c = parts.shape
    r = _rows_per_step(p, sz * c)

    def body(chip_ref, a_ref, l_ref, o_ref):
        acc = a_ref[...].astype(F32)
        for j in range(3):
            acc = acc + l_ref[j].astype(F32)
        o_ref[...] = acc

    nd = len(deps)

    def ordered(chip_ref, a_ref, l_ref, *rest):
        body(chip_ref, a_ref, l_ref, rest[nd])

    return pl.pallas_call(
        ordered, name=name,
        grid_spec=pltpu.PrefetchScalarGridSpec(
            num_scalar_prefetch=1, grid=(p // r,),
            in_specs=[pl.BlockSpec((r, None, sz, c), lambda i, cr: (i, cr[0], 0, 0)),
                      pl.BlockSpec((3, r, sz, c), lambda i, cr: (0, i, 0, 0))] + [ANY] * nd,
            out_specs=pl.BlockSpec((r, sz, c), lambda i, cr: (i, 0, 0))),
        out_shape=jax.ShapeDtypeStruct((p, sz, c), F32),
        compiler_params=_cparams(1),
    )(chip, parts, landed, *deps)


def _sum_lead(a, name, out_dtype=F32):
    k = a.shape[0]
    rest = a.shape[1:]
    r, c = rest[-2], rest[-1]
    lead = math.prod(rest[:-2])
    a3 = a.reshape(k, lead * r, c)
    rows = lead * r
    tb = rows
    for cand in (512, 256, 128, 64, 32, 16, 8):
        if rows % cand == 0 and rows > cand:
            tb = cand
            break

    def body(a_ref, o_ref):
        acc = a_ref[0].astype(F32)
        for i in range(1, k):
            acc = acc + a_ref[i].astype(F32)
        o_ref[...] = acc.astype(out_dtype)

    out = pl.pallas_call(
        body, name=name, grid=(rows // tb,),
        out_shape=jax.ShapeDtypeStruct((rows, c), out_dtype),
        in_specs=[pl.BlockSpec((k, tb, c), lambda i: (0, i, 0))],
        out_specs=pl.BlockSpec((tb, c), lambda i: (i, 0)),
        compiler_params=_cparams(1),
    )(a3)
    return out.reshape(rest)


def _bands(t, causal):
    r = lax.broadcasted_iota(jnp.int32, (t, t + HALO), 0)
    col = lax.broadcasted_iota(jnp.int32, (t, t + HALO), 1)
    diff = r + HALO - col if causal else col - r
    return jnp.stack([jnp.where((diff >= 0) & (diff < w), 1.0, 0.0) for w in POOL_WINDOWS]).astype(BF16)


def _split_dot(band, v):
    hi = v.astype(BF16)
    lo = (v - hi.astype(F32)).astype(BF16)
    return _nn(band, hi) + _nn(band, lo)


def _mix_fwd(x, g, wp, b, sc, name, deps=()):
    s = x.shape[0]
    t = min(256, s)
    rb = t // HALO

    def body(x_ref, xh_ref, g_ref, wp_ref, b_ref, sc_ref, band_ref, xo_ref, d_ref):
        i = pl.program_id(0)
        gg = g_ref[...]
        h, _ = _rms(x_ref[...], gg, D_MODEL)
        hh, _ = _rms(xh_ref[...], gg, D_MODEL)
        hh = jnp.where(i > 0, hh, 0.0)
        hext = jnp.concatenate([hh, h], axis=0)
        tok = i * t + lax.broadcasted_iota(jnp.int32, (t, 1), 0)
        for gi, w in enumerate(POOL_WINDOWS):
            sl = slice(gi * GROUP_DIM, (gi + 1) * GROUP_DIM)
            win = _split_dot(band_ref[gi], hext[:, sl])
            inv = 1.0 / jnp.minimum(tok + 1, w).astype(F32)
            dbf = (win * inv - h[:, sl]).astype(BF16)
            d_ref[:, sl] = dbf
            ypre = _nn(dbf, wp_ref[gi]) + b_ref[:, sl]
            xo_ref[:, sl] = x_ref[:, sl] + ypre * sc_ref[:, sl]

    return _pcall(
        body, (x, x, g, wp, b, sc, _bands(t, True)), deps, name=name, grid=(s // t,),
        out_shape=[jax.ShapeDtypeStruct((s, D_MODEL), F32), jax.ShapeDtypeStruct((s, D_MODEL), BF16)],
        in_specs=[_rows(t, D_MODEL),
                  pl.BlockSpec((HALO, D_MODEL), lambda i: (jnp.maximum(i * rb - 1, 0), 0)),
                  _full((1, D_MODEL)), _full((4, GROUP_DIM, GROUP_DIM)), _full((1, D_MODEL)), _full((1, D_MODEL)),
                  _full((4, t, t + HALO))],
        out_specs=[_rows(t, D_MODEL), _rows(t, D_MODEL)],
        compiler_params=_cparams(1, VMEM_MID),
    )


def _mix_bwd(x, dy, d, g, wp, b, sc, name, deps=()):
    s = x.shape[0]
    t = min(256, s)
    rb = t // HALO
    nb = s // t
    last_halo = s // HALO - 1

    def body(x_ref, dy_ref, dyn_ref, d_ref, g_ref, wp_ref, b_ref, sc_ref, band_ref,
             dx_ref, dyp_ref, dsc_ref, db_ref, dln_ref):
        i = pl.program_id(0)
        x = x_ref[...]
        gg = g_ref[...]
        dy = dy_ref[...]
        sc = sc_ref[...]
        dyp32 = dy * sc
        dyp = dyp32.astype(BF16)
        dyph = (dyn_ref[...] * sc).astype(BF16)
        dyp_ref[...] = dyp
        tok = i * t + lax.broadcasted_iota(jnp.int32, (t + HALO, 1), 0)
        dh, dsc = [], []
        for gi, w in enumerate(POOL_WINDOWS):
            sl = slice(gi * GROUP_DIM, (gi + 1) * GROUP_DIM)
            ypre = _nn(d_ref[:, sl], wp_ref[gi]) + b_ref[:, sl]
            dsc.append(jnp.sum(dy[:, sl] * ypre, axis=0, keepdims=True))
            dd = _nt(dyp[:, sl], wp_ref[gi])
            ddh = jnp.where(i < nb - 1, _nt(dyph[:, sl], wp_ref[gi]), 0.0)
            inv = 1.0 / jnp.minimum(tok + 1, w).astype(F32)
            ddext = jnp.concatenate([dd, ddh], axis=0) * inv
            dh.append(_split_dot(band_ref[gi], ddext) - dd)
        dh = jnp.concatenate(dh, axis=1)
        _, r = _rms(x, gg, D_MODEL)
        dxn, dg = _rms_bwd(x, r, gg, dh, D_MODEL)
        dx_ref[...] = dy + dxn

        @pl.when(i == 0)
        def _():
            dsc_ref[...] = jnp.zeros_like(dsc_ref)
            db_ref[...] = jnp.zeros_like(db_ref)
            dln_ref[...] = jnp.zeros_like(dln_ref)

        dsc_ref[...] += jnp.concatenate(dsc, axis=1)
        db_ref[...] += jnp.sum(dyp32, axis=0, keepdims=True)
        dln_ref[...] += dg

    vec = jax.ShapeDtypeStruct((1, D_MODEL), F32)
    return _pcall(
        body, (x, dy, dy, d, g, wp, b, sc, _bands(t, False)), deps, name=name, grid=(nb,),
        out_shape=[jax.ShapeDtypeStruct((s, D_MODEL), F32), jax.ShapeDtypeStruct((s, D_MODEL), BF16), vec, vec, vec],
        in_specs=[_rows(t, D_MODEL), _rows(t, D_MODEL),
                  pl.BlockSpec((HALO, D_MODEL), lambda i: (jnp.minimum((i + 1) * rb, last_halo), 0)),
                  _rows(t, D_MODEL),
                  _full((1, D_MODEL)), _full((4, GROUP_DIM, GROUP_DIM)), _full((1, D_MODEL)), _full((1, D_MODEL)),
                  _full((4, t, t + HALO))],
        out_specs=[_rows(t, D_MODEL), _rows(t, D_MODEL), _full((1, D_MODEL)), _full((1, D_MODEL)), _full((1, D_MODEL))],
        compiler_params=_cparams(1, VMEM_MID),
    )


def _load_weights(w_hbm, w_vmem, sem):
    @pl.when(pl.program_id(0) == 0)
    def _():
        cp = pltpu.make_async_copy(w_hbm, w_vmem, sem)
        cp.start()
        cp.wait()


def _ffn_fwd(x, g, w, name):
    s = x.shape[0]
    t = min(512, s)

    def body(x_ref, g_ref, w_hbm, xo_ref, gate_ref, up_ref, w_ref, sem):
        _load_weights(w_hbm, w_ref, sem)
        x = x_ref[...]
        hn = _rms(x, g_ref[...], D_MODEL)[0].astype(BF16)
        acc = x
        for c in range(2):
            rs = slice(c * FF_HALF, (c + 1) * FF_HALF)
            gt = _nt(hn, w_ref[0, rs, :])
            up = _nt(hn, w_ref[1, rs, :])
            gate_ref[:, rs] = gt.astype(BF16)
            up_ref[:, rs] = up.astype(BF16)
            act = ((gt * _sigmoid(gt)) * up).astype(BF16)
            acc = acc + _nn(act, w_ref[2, rs, :])
        xo_ref[...] = acc

    hid = jax.ShapeDtypeStruct((s, D_FF), BF16)
    return pl.pallas_call(
        body, name=name, grid=(s // t,),
        out_shape=[jax.ShapeDtypeStruct((s, D_MODEL), F32), hid, hid],
        in_specs=[_rows(t, D_MODEL), _full((1, D_MODEL)), ANY],
        out_specs=[_rows(t, D_MODEL), _rows(t, D_FF), _rows(t, D_FF)],
        scratch_shapes=[pltpu.VMEM((3, D_FF, D_MODEL), BF16), pltpu.SemaphoreType.DMA],
        compiler_params=_cparams(1, VMEM_BIG),
    )(x, g, w)


def _ffn_bwd(x, dy, gate, up, g, w, name, deps=()):
    s = x.shape[0]
    t = min(256, s)

    def body(x_ref, dy_ref, gate_ref, up_ref, g_ref, w_hbm,
             dx_ref, act_ref, dg_ref, du_ref, hn_ref, dyb_ref, dln_ref, w_ref, sem):
        _load_weights(w_hbm, w_ref, sem)
        x = x_ref[...]
        gg = g_ref[...]
        y, r = _rms(x, gg, D_MODEL)
        hn = y.astype(BF16)
        hn_ref[...] = hn
        dy = dy_ref[...]
        dyb = dy.astype(BF16)
        dyb_ref[...] = dyb
        dh = jnp.zeros((t, D_MODEL), F32)
        for c in range(2):
            rs = slice(c * FF_HALF, (c + 1) * FF_HALF)
            gt = gate_ref[:, rs].astype(F32)
            u = up_ref[:, rs].astype(F32)
            sg = _sigmoid(gt)
            sl = gt * sg
            act_ref[:, rs] = (sl * u).astype(BF16)
            dact = _nt(dyb, w_ref[2, rs, :])
            dg = (dact * u * (sg * (1.0 + gt * (1.0 - sg)))).astype(BF16)
            du = (dact * sl).astype(BF16)
            dg_ref[:, rs] = dg
            du_ref[:, rs] = du
            dh = dh + _nn(dg, w_ref[0, rs, :]) + _nn(du, w_ref[1, rs, :])
        dxn, dgl = _rms_bwd(x, r, gg, dh, D_MODEL)
        dx_ref[...] = dy + dxn

        @pl.when(pl.program_id(0) == 0)
        def _():
            dln_ref[...] = jnp.zeros_like(dln_ref)

        dln_ref[...] += dgl

    hid = jax.ShapeDtypeStruct((s, D_FF), BF16)
    tok = jax.ShapeDtypeStruct((s, D_MODEL), BF16)
    return _pcall(
        body, (x, dy, gate, up, g, w), deps, name=name, grid=(s // t,),
        out_shape=[jax.ShapeDtypeStruct((s, D_MODEL), F32), hid, hid, hid, tok, tok,
                   jax.ShapeDtypeStruct((1, D_MODEL), F32)],
        in_specs=[_rows(t, D_MODEL), _rows(t, D_MODEL), _rows(t, D_FF), _rows(t, D_FF), _full((1, D_MODEL)), ANY],
        out_specs=[_rows(t, D_MODEL), _rows(t, D_FF), _rows(t, D_FF), _rows(t, D_FF),
                   _rows(t, D_MODEL), _rows(t, D_MODEL), _full((1, D_MODEL))],
        scratch_shapes=[pltpu.VMEM((3, D_FF, D_MODEL), BF16), pltpu.SemaphoreType.DMA],
        compiler_params=_cparams(1, VMEM_BIG),
    )


def _tn_matmul(a, b, into, p0, name, groups=1, m_chunk=None, deps=()):
    s = a.shape[0]
    m, n = a.shape[1] // groups, b.shape[1] // groups
    assert into.shape[1:] == (m, n)
    mc = m if m_chunk is None else m_chunk
    nm = m // mc
    t = min(1024, s)
    nt = s // t

    def body(a_ref, b_ref, into_ref, o_ref, acc):
        ti = pl.program_id(2)

        @pl.when(ti == 0)
        def _():
            acc[...] = jnp.zeros_like(acc)

        acc[...] += _tn(a_ref[...], b_ref[...])

        @pl.when(ti == nt - 1)
        def _():
            o_ref[...] = acc[...].astype(o_ref.dtype)

    return _pcall(
        body, (a, b, into), deps, name=name, grid=(groups, nm, nt),
        out_shape=jax.ShapeDtypeStruct(into.shape, into.dtype),
        in_specs=[pl.BlockSpec((t, mc), lambda gi, mi, ti: (ti, gi * nm + mi)),
                  pl.BlockSpec((t, n), lambda gi, mi, ti: (ti, gi)), ANY],
        out_specs=pl.BlockSpec((None, mc, n), lambda gi, mi, ti: (p0 + gi, mi, 0)),
        scratch_shapes=[pltpu.VMEM((mc, n), F32)],
        input_output_aliases={2: 0},
        compiler_params=_cparams(3, VMEM_BIG),
    )


def _rope_tables(positions):
    half = ROPE // 2
    inv = ROPE_THETA ** (-jnp.arange(half, dtype=F32) * 2.0 / ROPE)
    ang = positions.astype(F32)[:, None] * inv
    cos, sin = jnp.cos(ang), jnp.sin(ang)
    zero = jnp.zeros((positions.shape[0], LANES - ROPE), F32)
    return jnp.concatenate([cos, cos, zero], axis=1), jnp.concatenate([-sin, sin, zero], axis=1)


def _kv_specs(t):
    return [_full((1, D_MODEL)), _full((D_MODEL, KV_RANK)), _full((D_MODEL, LANES)), _full((1, KV_RANK)),
            _full((KV_RANK, N_HEADS * NOPE)), _full((KV_RANK, N_HEADS * V_DIM)),
            _full((1, NOPE)), _full((1, LANES)), _rows(t, LANES), _rows(t, LANES)]


def _kv_fwd(x, ln, wc, wpe, gl, wuk, wuv, gkn, gkr, cos, sin, name, deps=()):
    s = x.shape[0]
    t = min(PROJ_ROWS, s)

    def body(x_ref, ln_ref, wc_ref, wpe_ref, gl_ref, wuk_ref, wuv_ref, gkn_ref, gkr_ref, cos_ref, sin_ref,
             k_ref, v_ref):
        hn = _rms(x_ref[...], ln_ref[...], D_MODEL)[0].astype(BF16)
        clat = _nn(hn, wc_ref[...])
        kpe = _nn(hn, wpe_ref[...])
        cn = _rms(clat, gl_ref[...], KV_RANK)[0].astype(BF16)
        sspe = jnp.sum(kpe * kpe, axis=-1, keepdims=True)
        base = kpe * gkr_ref[...]
        rot = base * cos_ref[...] + _swap_halves(base, _swap_perm()) * sin_ref[...]
        kn_all = _nn(cn, wuk_ref[...])
        v_ref[...] = _nn(cn, wuv_ref[...]).astype(BF16)
        for h in range(N_HEADS):
            kn = kn_all[:, h * NOPE:(h + 1) * NOPE]
            r = lax.rsqrt((jnp.sum(kn * kn, axis=-1, keepdims=True) + sspe) * (1.0 / QK_DIM) + EPS)
            k_ref[:, h * QK_PAD:h * QK_PAD + NOPE] = ((kn * r) * gkn_ref[...]).astype(BF16)
            k_ref[:, h * QK_PAD + NOPE:(h + 1) * QK_PAD] = (rot * r).astype(BF16)

    return _pcall(
        body, (x, ln, wc, wpe, gl, wuk, wuv, gkn, gkr, cos, sin), deps, name=name, grid=(s // t,),
        out_shape=[jax.ShapeDtypeStruct((s, N_HEADS * QK_PAD), BF16), jax.ShapeDtypeStruct((s, N_HEADS * V_DIM), BF16)],
        in_specs=[_rows(t, D_MODEL)] + _kv_specs(t),
        out_specs=[_rows(t, N_HEADS * QK_PAD), _rows(t, N_HEADS * V_DIM)],
        compiler_params=_cparams(1, VMEM_MID),
    )


def _kv_bwd(x, dxin, dks, dvs, ln, wc, wpe, gl, wuk, wuv, gkn, gkr, cos, sin, name):
    s = x.shape[0]
    t = min(PROJ_ROWS, s)
    nk = len(dks)

    def body(*refs):
        x_ref, dxin_ref = refs[:2]
        dk_refs = refs[2:2 + nk]
        dv_refs = refs[2 + nk:2 + 2 * nk]
        (ln_ref, wc_ref, wpe_ref, gl_ref, wuk_ref, wuv_ref, gkn_ref, gkr_ref, cos_ref, sin_ref,
         dx_ref, hn_ref, cn_ref, dkn_ref, dvb_ref, dcc_ref, dpe_ref,
         dln_ref, dgl_ref, dgkn_ref, dgkr_ref) = refs[2 + 2 * nk:]
        x = x_ref[...]
        ln = ln_ref[...]
        y, rx = _rms(x, ln, D_MODEL)
        hn = y.astype(BF16)
        hn_ref[...] = hn
        clat = _nn(hn, wc_ref[...])
        kpe = _nn(hn, wpe_ref[...])
        gl = gl_ref[...]
        cy, rc = _rms(clat, gl, KV_RANK)
        cn = cy.astype(BF16)
        cn_ref[...] = cn
        sspe = jnp.sum(kpe * kpe, axis=-1, keepdims=True)
        cs, sn, perm = cos_ref[...], sin_ref[...], _swap_perm()
        gkn, gkr = gkn_ref[...], gkr_ref[...]
        base = kpe * gkr
        rot = base * cs + _swap_halves(base, perm) * sn
        dkr_sum = jnp.zeros((t, LANES), F32)
        coef_sum = jnp.zeros((t, 1), F32)
        dgkn = jnp.zeros((1, NOPE), F32)
        kn_all = _nn(cn, wuk_ref[...])
        dkn_heads = []
        for h in range(N_HEADS):
            kn = kn_all[:, h * NOPE:(h + 1) * NOPE]
            r = lax.rsqrt((jnp.sum(kn * kn, axis=-1, keepdims=True) + sspe) * (1.0 / QK_DIM) + EPS)
            lo, mid, hi = h * QK_PAD, h * QK_PAD + NOPE, (h + 1) * QK_PAD
            dko = dk_refs[0][:, lo:mid]
            dkr = dk_refs[0][:, mid:hi]
            for j in range(1, nk):
                dko = dko + dk_refs[j][:, lo:mid]
                dkr = dkr + dk_refs[j][:, mid:hi]
            un = dko * gkn
            sm = (jnp.sum(kn * un, axis=-1, keepdims=True) + jnp.sum(rot * dkr, axis=-1, keepdims=True)) * (1.0 / QK_DIM)
            coef = r * r * r * sm
            dkn = (r * un - kn * coef).astype(BF16)
            dkr_sum = dkr_sum + r * dkr
            coef_sum = coef_sum + coef
            dgkn = dgkn + jnp.sum(dko * (kn * r), axis=0, keepdims=True)
            dkn_heads.append(dkn)
        dkn_all = jnp.concatenate(dkn_heads, axis=1)
        dkn_ref[...] = dkn_all
        dv_all = dv_refs[0][...]
        for j in range(1, nk):
            dv_all = dv_all + dv_refs[j][...]
        dvb = dv_all.astype(BF16)
        dvb_ref[...] = dvb
        dc = _nt(dkn_all, wuk_ref[...]) + _nt(dvb, wuv_ref[...])
        dz = dkr_sum * cs - _swap_halves(dkr_sum, perm) * sn
        dkpe = dz * gkr - kpe * coef_sum
        dgkr = jnp.sum(dz * kpe, axis=0, keepdims=True)
        dclat, dgl = _rms_bwd(clat, rc, gl, dc, KV_RANK)
        dcc = dclat.astype(BF16)
        dpe = dkpe.astype(BF16)
        dcc_ref[...] = dcc
        dpe_ref[...] = dpe
        dhn = _nt(dcc, wc_ref[...]) + _nt(dpe, wpe_ref[...])
        dxn, dln = _rms_bwd(x, rx, ln, dhn, D_MODEL)
        dx_ref[...] = dxin_ref[...] + dxn

        @pl.when(pl.program_id(0) == 0)
        def _():
            dln_ref[...] = jnp.zeros_like(dln_ref)
            dgl_ref[...] = jnp.zeros_like(dgl_ref)
            dgkn_ref[...] = jnp.zeros_like(dgkn_ref)
            dgkr_ref[...] = jnp.zeros_like(dgkr_ref)

        dln_ref[...] += dln
        dgl_ref[...] += dgl
        dgkn_ref[...] += dgkn
        dgkr_ref[...] += dgkr

    def tok(cols, dt):
        return jax.ShapeDtypeStruct((s, cols), dt)

    def vec(cols):
        return jax.ShapeDtypeStruct((1, cols), F32)

    return pl.pallas_call(
        body, name=name, grid=(s // t,),
        out_shape=[tok(D_MODEL, F32), tok(D_MODEL, BF16), tok(KV_RANK, BF16), tok(N_HEADS * NOPE, BF16),
                   tok(N_HEADS * V_DIM, BF16), tok(KV_RANK, BF16), tok(LANES, BF16),
                   vec(D_MODEL), vec(KV_RANK), vec(NOPE), vec(LANES)],
        in_specs=[_rows(t, D_MODEL), _rows(t, D_MODEL)] + [_rows(t, N_HEADS * QK_PAD)] * nk
                 + [_rows(t, N_HEADS * V_DIM)] * nk + _kv_specs(t),
        out_specs=[_rows(t, D_MODEL), _rows(t, D_MODEL), _rows(t, KV_RANK), _rows(t, N_HEADS * NOPE),
                   _rows(t, N_HEADS * V_DIM), _rows(t, KV_RANK), _rows(t, LANES),
                   _full((1, D_MODEL)), _full((1, KV_RANK)), _full((1, NOPE)), _full((1, LANES))],
        compiler_params=_cparams(1, VMEM_BIG),
    )(x, dxin, *dks, *dvs, ln, wc, wpe, gl, wuk, wuv, gkn, gkr, cos, sin)


def _q_specs(t):
    return [_full((1, D_MODEL)), _full((D_MODEL, Q_RANK)), _full((1, Q_RANK)), _full((N_HEADS, Q_RANK, QK_PAD)),
            _full((1, NOPE)), _full((1, LANES)), _rows(t, LANES), _rows(t, LANES)]


def _q_fwd(x, ln, wdq, gql, wuq, gqn, gqr, cos, sin, name, deps=()):
    s = x.shape[0]
    t = min(PROJ_ROWS, s)

    def body(x_ref, ln_ref, wdq_ref, gql_ref, wuq_ref, gqn_ref, gqr_ref, cos_ref, sin_ref, q_ref):
        hn = _rms(x_ref[...], ln_ref[...], D_MODEL)[0].astype(BF16)
        cqn = _rms(_nn(hn, wdq_ref[...]), gql_ref[...], Q_RANK)[0].astype(BF16)
        cs, sn, perm = cos_ref[...], sin_ref[...], _swap_perm()
        for h in range(N_HEADS):
            qa = _nn(cqn, wuq_ref[h])
            r = lax.rsqrt(jnp.sum(qa * qa, axis=-1, keepdims=True) * (1.0 / QK_DIM) + EPS)
            q_ref[:, h * QK_PAD:h * QK_PAD + NOPE] = ((qa[:, :NOPE] * r) * gqn_ref[...]).astype(BF16)
            z = (qa[:, NOPE:] * r) * gqr_ref[...]
            q_ref[:, h * QK_PAD + NOPE:(h + 1) * QK_PAD] = (z * cs + _swap_halves(z, perm) * sn).astype(BF16)

    return _pcall(
        body, (x, ln, wdq, gql, wuq, gqn, gqr, cos, sin), deps, name=name, grid=(s // t,),
        out_shape=jax.ShapeDtypeStruct((s, N_HEADS * QK_PAD), BF16),
        in_specs=[_rows(t, D_MODEL)] + _q_specs(t),
        out_specs=_rows(t, N_HEADS * QK_PAD),
        compiler_params=_cparams(1, VMEM_MID),
    )


def _q_bwd(x, dxin, dq, ln, wdq, gql, wuq, gqn, gqr, cos, sin, name):
    s = x.shape[0]
    t = min(PROJ_ROWS, s)

    def body(x_ref, dxin_ref, dq_ref, ln_ref, wdq_ref, gql_ref, wuq_ref, gqn_ref, gqr_ref, cos_ref, sin_ref,
             dx_ref, hn_ref, cqn_ref, dqa_ref, dcq_ref, dln_ref, dgql_ref, dgqn_ref, dgqr_ref):
        x = x_ref[...]
        ln = ln_ref[...]
        y, rx = _rms(x, ln, D_MODEL)
        hn = y.astype(BF16)
        hn_ref[...] = hn
        cqp = _nn(hn, wdq_ref[...])
        gql = gql_ref[...]
        cy, rc = _rms(cqp, gql, Q_RANK)
        cqn = cy.astype(BF16)
        cqn_ref[...] = cqn
        cs, sn, perm = cos_ref[...], sin_ref[...], _swap_perm()
        gqn, gqr = gqn_ref[...], gqr_ref[...]
        dcq = jnp.zeros((t, Q_RANK), F32)
        dgqn = jnp.zeros((1, NOPE), F32)
        dgqr = jnp.zeros((1, LANES), F32)
        for h in range(N_HEADS):
            qa = _nn(cqn, wuq_ref[h])
            qn, qr = qa[:, :NOPE], qa[:, NOPE:]
            r = lax.rsqrt(jnp.sum(qa * qa, axis=-1, keepdims=True) * (1.0 / QK_DIM) + EPS)
            dqo = dq_ref[:, h * QK_PAD:h * QK_PAD + NOPE]
            dqr = dq_ref[:, h * QK_PAD + NOPE:(h + 1) * QK_PAD]
            dz = dqr * cs - _swap_halves(dqr, perm) * sn
            un = dqo * gqn
            ur = dz * gqr
            sm = (jnp.sum(qn * un, axis=-1, keepdims=True) + jnp.sum(qr * ur, axis=-1, keepdims=True)) * (1.0 / QK_DIM)
            coef = r * r * r * sm
            dqa = jnp.concatenate([r * un - qn * coef, r * ur - qr * coef], axis=1).astype(BF16)
            dgqn = dgqn + jnp.sum(dqo * (qn * r), axis=0, keepdims=True)
            dgqr = dgqr + jnp.sum(dz * (qr * r), axis=0, keepdims=True)
            dqa_ref[:, h * QK_PAD:(h + 1) * QK_PAD] = dqa
            dcq = dcq + _nt(dqa, wuq_ref[h])
        dcqp, dgql = _rms_bwd(cqp, rc, gql, dcq, Q_RANK)
        dcqb = dcqp.astype(BF16)
        dcq_ref[...] = dcqb
        dhn = _nt(dcqb, wdq_ref[...])
        dxn, dln = _rms_bwd(x, rx, ln, dhn, D_MODEL)
        dx_ref[...] = dxin_ref[...] + dxn

        @pl.when(pl.program_id(0) == 0)
        def _():
            dln_ref[...] = jnp.zeros_like(dln_ref)
            dgql_ref[...] = jnp.zeros_like(dgql_ref)
            dgqn_ref[...] = jnp.zeros_like(dgqn_ref)
            dgqr_ref[...] = jnp.zeros_like(dgqr_ref)

        dln_ref[...] += dln
        dgql_ref[...] += dgql
        dgqn_ref[...] += dgqn
        dgqr_ref[...] += dgqr

    def tok(cols, dt):
        return jax.ShapeDtypeStruct((s, cols), dt)

    def vec(cols):
        return jax.ShapeDtypeStruct((1, cols), F32)

    return pl.pallas_call(
        body, name=name, grid=(s // t,),
        out_shape=[tok(D_MODEL, F32), tok(D_MODEL, BF16), tok(Q_RANK, BF16), tok(N_HEADS * QK_PAD, BF16),
                   tok(Q_RANK, BF16), vec(D_MODEL), vec(Q_RANK), vec(NOPE), vec(LANES)],
        in_specs=[_rows(t, D_MODEL), _rows(t, D_MODEL), _rows(t, N_HEADS * QK_PAD)] + _q_specs(t),
        out_specs=[_rows(t, D_MODEL), _rows(t, D_MODEL), _rows(t, Q_RANK), _rows(t, N_HEADS * QK_PAD),
                   _rows(t, Q_RANK), _full((1, D_MODEL)), _full((1, Q_RANK)), _full((1, NOPE)), _full((1, LANES))],
        compiler_params=_cparams(1, VMEM_MID),
    )(x, dxin, dq, ln, wdq, gql, wuq, gqn, gqr, cos, sin)


SM_SCALE = 1.0 / math.sqrt(QK_DIM)
LOG2_E = math.log2(math.e)
EXP2_SCALE = SM_SCALE * LOG2_E
NEG = -1e30


def _diag_mask(t):
    qpos = lax.broadcasted_iota(jnp.int32, (t, t), 0)
    kpos = lax.broadcasted_iota(jnp.int32, (t, t), 1)
    return lax.shift_right_logical(kpos, 6) <= lax.shift_right_logical(qpos, 6)


def _att_fwd(q, k, v, name):
    s = q.shape[0]
    t = min(512, s)
    nb = s // t

    def body(q_ref, k_ref, v_ref, o_ref, lse_ref):
        qi = pl.program_id(1)
        qq = q_ref[...]

        def block(ki, carry, masked):
            m_old, l_old, acc = carry
            rows = pl.ds(pl.multiple_of(ki * t, t), t)
            sc = _nt(qq, k_ref[rows, :])
            if masked:
                sc = jnp.where(_diag_mask(t), sc, NEG)
            m_new = jnp.maximum(m_old, jnp.max(sc, axis=-1, keepdims=True))
            p = jnp.exp2((sc - m_new) * EXP2_SCALE)
            alpha = jnp.exp2((m_old - m_new) * EXP2_SCALE)
            l_new = alpha * l_old + jnp.sum(p, axis=-1, keepdims=True)
            acc = alpha * acc + _nn(p.astype(BF16), v_ref[rows, :])
            return m_new, l_new, acc

        init = (jnp.full((t, 1), NEG, F32), jnp.zeros((t, 1), F32), jnp.zeros((t, V_DIM), F32))
        def pair(k0, c):
            return block(k0 + 1, block(k0, c, False), False)

        carry = lax.fori_loop(0, qi // 4, lambda j, c: pair(4 * j + 2, pair(4 * j, c)), init)
        done = 4 * (qi // 4)
        carry = lax.cond((qi & 2) != 0, lambda c: pair(done, c), lambda c: c, carry)
        carry = lax.cond((qi & 1) != 0, lambda c: block(qi - 1, c, False), lambda c: c, carry)
        m_fin, l_fin, acc = block(qi, carry, True)
        o_ref[...] = (acc / l_fin).astype(BF16)
        lse_ref[...] = jnp.broadcast_to(m_fin * SM_SCALE + jnp.log(l_fin), (t, LANES))

    return pl.pallas_call(
        body, name=name, grid=(N_HEADS, nb),
        out_shape=[jax.ShapeDtypeStruct((s, N_HEADS * V_DIM), BF16), jax.ShapeDtypeStruct((s, N_HEADS * LANES), F32)],
        in_specs=[pl.BlockSpec((t, QK_PAD), lambda h, qi: (qi, h)),
                  pl.BlockSpec((s, QK_PAD), lambda h, qi: (0, h)),
                  pl.BlockSpec((s, V_DIM), lambda h, qi: (0, h))],
        out_specs=[pl.BlockSpec((t, V_DIM), lambda h, qi: (qi, h)),
                   pl.BlockSpec((t, LANES), lambda h, qi: (qi, h))],
        compiler_params=_cparams(2, VMEM_MID),
    )(q, k, v)


def _att_bwd(q, k, v, do, stats, name, deps=()):
    s = q.shape[0]
    t = min(512, s)
    nb = s // t

    def body(q_ref, k_ref, v_ref, do_ref, st_ref, dq_ref, dk_ref, dv_ref):
        ki = pl.program_id(1)
        kk, vv = k_ref[...], v_ref[...]

        @pl.when(ki == 0)
        def _():
            dq_ref[...] = jnp.zeros_like(dq_ref)

        def block(qi, carry, masked):
            dk, dv = carry
            rows = pl.ds(pl.multiple_of(qi * t, t), t)
            qq, dob = q_ref[rows, :], do_ref[rows, :]
            sc = _nt(qq, kk)
            if masked:
                sc = jnp.where(_diag_mask(t), sc, NEG)
            st = st_ref[rows, :]
            p = jnp.exp2(sc * EXP2_SCALE - st[:, 0:1])
            dp = _nt(dob, vv)
            ds = (p * (dp - st[:, 1:2])).astype(BF16)
            dq_ref[rows, :] += _nn(ds, kk)
            return dk + _tn(ds, qq), dv + _tn(p.astype(BF16), dob)

        carry = block(ki, (jnp.zeros((t, QK_PAD), F32), jnp.zeros((t, V_DIM), F32)), True)
        rest = nb - 1 - ki
        carry = lax.fori_loop(
            0, rest // 2, lambda j, c: block(ki + 2 * j + 2, block(ki + 2 * j + 1, c, False), False), carry)
        dk, dv = lax.cond(rest % 2 == 1, lambda c: block(nb - 1, c, False), lambda c: c, carry)
        dk_ref[...] = dk * SM_SCALE
        dv_ref[...] = dv

        @pl.when(ki == nb - 1)
        def _():
            dq_ref[...] = dq_ref[...] * SM_SCALE

    def head(h, ki):
        return (0, h)

    def kblock(h, ki):
        return (ki, h)

    return _pcall(
        body, (q, k, v, do, stats), deps, name=name, grid=(N_HEADS, nb),
        out_shape=[jax.ShapeDtypeStruct((s, N_HEADS * QK_PAD), F32), jax.ShapeDtypeStruct((s, N_HEADS * QK_PAD), F32),
                   jax.ShapeDtypeStruct((s, N_HEADS * V_DIM), F32)],
        in_specs=[pl.BlockSpec((s, QK_PAD), head), pl.BlockSpec((t, QK_PAD), kblock), pl.BlockSpec((t, V_DIM), kblock),
                  pl.BlockSpec((s, V_DIM), head), pl.BlockSpec((s, LANES), head)],
        out_specs=[pl.BlockSpec((s, QK_PAD), head), pl.BlockSpec((t, QK_PAD), kblock), pl.BlockSpec((t, V_DIM), kblock)],
        compiler_params=_cparams(2, VMEM_MID),
    )


def _o_fwd(x, o, wo, name):
    s = x.shape[0]
    t = min(512, s)

    def body(x_ref, o_ref, wo_ref, xo_ref):
        xo_ref[...] = x_ref[...] + _nn(o_ref[...], wo_ref[...])

    return pl.pallas_call(
        body, name=name, grid=(s // t,),
        out_shape=jax.ShapeDtypeStruct((s, D_MODEL), F32),
        in_specs=[_rows(t, D_MODEL), _rows(t, D_MODEL), _full((D_MODEL, D_MODEL))],
        out_specs=_rows(t, D_MODEL),
        compiler_params=_cparams(1, VMEM_MID),
    )(x, o, wo)


def _o_bwd(dx, wo, o, lse, name, deps=()):
    s = dx.shape[0]
    t = min(512, s)

    def body(dx_ref, wo_ref, o_ref, lse_ref, do_ref, dxb_ref, st_ref):
        dxb = dx_ref[...].astype(BF16)
        dxb_ref[...] = dxb
        dob = _nt(dxb, wo_ref[...]).astype(BF16)
        do_ref[...] = dob
        lane = lax.broadcasted_iota(jnp.int32, (t, LANES), 1)
        for h in range(N_HEADS):
            sl = slice(h * V_DIM, (h + 1) * V_DIM)
            dsum = jnp.sum(dob[:, sl].astype(F32) * o_ref[:, sl].astype(F32), axis=-1, keepdims=True)
            st_ref[:, sl] = jnp.where(lane == 0, lse_ref[:, sl] * LOG2_E, jnp.where(lane == 1, dsum, 0.0))

    tok = jax.ShapeDtypeStruct((s, D_MODEL), BF16)
    return _pcall(
        body, (dx, wo, o, lse), deps, name=name, grid=(s // t,),
        out_shape=[tok, tok, jax.ShapeDtypeStruct((s, N_HEADS * LANES), F32)],
        in_specs=[_rows(t, D_MODEL), _full((D_MODEL, D_MODEL)), _rows(t, D_MODEL), _rows(t, N_HEADS * LANES)],
        out_specs=[_rows(t, D_MODEL), _rows(t, D_MODEL), _rows(t, N_HEADS * LANES)],
        compiler_params=_cparams(1, VMEM_MID),
    )


def _loss_head(y, target, name):
    s = y.shape[0]
    t = min(512, s)

    def body(y_ref, t_ref, dy_ref, sq_ref):
        e = y_ref[...] - t_ref[...]
        dy_ref[...] = e * (1.0 / D_MODEL)

        @pl.when(pl.program_id(0) == 0)
        def _():
            sq_ref[...] = jnp.zeros_like(sq_ref)

        sq_ref[...] += jnp.sum(e * e, axis=0, keepdims=True)

    return pl.pallas_call(
        body, name=name, grid=(s // t,),
        out_shape=[jax.ShapeDtypeStruct((s, D_MODEL), F32), jax.ShapeDtypeStruct((1, D_MODEL), F32)],
        in_specs=[_rows(t, D_MODEL), _rows(t, D_MODEL)],
        out_specs=[_rows(t, D_MODEL), _full((1, D_MODEL))],
        compiler_params=_cparams(1),
    )(y, target)


def _adamw(w, g, m, v, name):
    shape = w.shape
    c = shape[-1]
    r = math.prod(shape[:-1])
    tb = r
    for cand in (512, 256, 128):
        if r % cand == 0 and r > cand:
            tb = cand
            break

    def body(w_ref, g_ref, m_ref, v_ref, d_ref, mo_ref, vo_ref):
        gr = g_ref[...]
        mn = ADAM_B1 * m_ref[...] + (1.0 - ADAM_B1) * gr
        vn = ADAM_B2 * v_ref[...] + (1.0 - ADAM_B2) * (gr * gr)
        m_hat = mn / (1.0 - ADAM_B1 ** ADAM_STEP)
        v_hat = vn / (1.0 - ADAM_B2 ** ADAM_STEP)
        d_ref[...] = -ADAM_LR * (m_hat / (jnp.sqrt(v_hat) + ADAM_EPS) + ADAM_WD * w_ref[...])
        mo_ref[...] = mn
        vo_ref[...] = vn

    spec = pl.BlockSpec((tb, c), lambda i: (i, 0))
    flat = jax.ShapeDtypeStruct((r, c), F32)
    outs = pl.pallas_call(
        body, name=name, grid=(r // tb,),
        out_shape=[flat, flat, flat],
        in_specs=[spec] * 4, out_specs=[spec] * 3,
        compiler_params=_cparams(1),
    )(w.reshape(r, c), g.reshape(r, c), m.reshape(r, c), v.reshape(r, c))
    return [a.reshape(shape) for a in outs]


def _pad_cols(a, width):
    return jnp.pad(a, [(0, 0)] * (a.ndim - 1) + [(0, width - a.shape[-1])])


def _owner_view(a, sz):
    return a.reshape(a.shape[0], N_CHIPS, 2, sz, a.shape[-1])


def kernel(x, positions, ln_mix_a, w_pool, b_pool, pool_scale, ln_ffn, w_gate, w_up, w_down, ln_kv, w_dkv, g_kv_latent, w_uk, w_uv, g_k, ln_mix_b, w_dq, g_q_latent, w_uq, g_q, w_o, loss_target, m_ln_mix_a, m_w_pool, m_b_pool, m_pool_scale, m_ln_ffn, m_w_gate, m_w_up, m_w_down, m_ln_kv, m_w_dkv, m_g_kv_latent, m_w_uk, m_w_uv, m_g_k, m_ln_mix_b, m_w_dq, m_g_q_latent, m_w_uq, m_g_q, m_w_o, v_ln_mix_a, v_w_pool, v_b_pool, v_pool_scale, v_ln_ffn, v_w_gate, v_w_up, v_w_down, v_ln_kv, v_w_dkv, v_g_kv_latent, v_w_uk, v_w_uv, v_g_k, v_ln_mix_b, v_w_dq, v_g_q_latent, v_w_uq, v_g_q, v_w_o):
    weights = dict(ln_mix_a=ln_mix_a, w_pool=w_pool, b_pool=b_pool, pool_scale=pool_scale, ln_ffn=ln_ffn,
                   w_gate=w_gate, w_up=w_up, w_down=w_down, ln_kv=ln_kv, w_dkv=w_dkv, g_kv_latent=g_kv_latent,
                   w_uk=w_uk, w_uv=w_uv, g_k=g_k, ln_mix_b=ln_mix_b, w_dq=w_dq, g_q_latent=g_q_latent,
                   w_uq=w_uq, g_q=g_q, w_o=w_o)
    mom1 = dict(ln_mix_a=m_ln_mix_a, w_pool=m_w_pool, b_pool=m_b_pool, pool_scale=m_pool_scale, ln_ffn=m_ln_ffn,
                w_gate=m_w_gate, w_up=m_w_up, w_down=m_w_down, ln_kv=m_ln_kv, w_dkv=m_w_dkv,
                g_kv_latent=m_g_kv_latent, w_uk=m_w_uk, w_uv=m_w_uv, g_k=m_g_k, ln_mix_b=m_ln_mix_b, w_dq=m_w_dq,
                g_q_latent=m_g_q_latent, w_uq=m_w_uq, g_q=m_g_q, w_o=m_w_o)
    mom2 = dict(ln_mix_a=v_ln_mix_a, w_pool=v_w_pool, b_pool=v_b_pool, pool_scale=v_pool_scale, ln_ffn=v_ln_ffn,
                w_gate=v_w_gate, w_up=v_w_up, w_down=v_w_down, ln_kv=v_ln_kv, w_dkv=v_w_dkv,
                g_kv_latent=v_g_kv_latent, w_uk=v_w_uk, w_uv=v_w_uv, g_k=v_g_k, ln_mix_b=v_ln_mix_b, w_dq=v_w_dq,
                g_q_latent=v_g_q_latent, w_uq=v_w_uq, g_q=v_g_q, w_o=v_w_o)
    names = list(weights)
    dev = 4 * lax.axis_index("x") + 2 * lax.axis_index("y") + lax.axis_index("c")
    core = lax.axis_index("c").astype(jnp.int32).reshape(1)
    chip = (2 * lax.axis_index("x") + lax.axis_index("y")).astype(jnp.int32).reshape(1)

    xs = x[0]
    target = loss_target[0]
    cos, sin = _rope_tables(positions[0])

    def placed(shard):
        buf = lax.empty((shard.shape[0], N_DEV) + shard.shape[1:], shard.dtype)
        return lax.dynamic_update_slice(buf, shard[:, None], (0, dev, 0, 0))

    groups = {f"ffn{l}": [placed(jnp.stack([w_gate[l].T, w_up[l].T, w_down[l]]).astype(BF16))] for l in range(4)}
    groups["att"] = [placed(a.astype(BF16)) for a in (
        w_dkv[None, :, :KV_RANK], _pad_cols(w_dkv[None, :, KV_RANK:], LANES), w_uk[None], w_uv[None],
        w_dq, _pad_cols(w_uq, QK_PAD), w_o)]
    small_sh = jnp.concatenate([ln_mix_a.reshape(1, -1), pool_scale.reshape(1, -1), b_pool.reshape(1, -1)], axis=1)
    wp_g, small_g = _all_gather([w_pool.astype(BF16), small_sh], [2, 0], "gather_first")
    wp_all = wp_g.reshape(2, 4, GROUP_DIM, GROUP_DIM)
    small_g = small_g.reshape(N_DEV, 3, 2, LANES)
    ln_a_all = small_g[:, 0].transpose(1, 0, 2).reshape(2, 1, D_MODEL)
    sc_all = small_g[:, 1].transpose(1, 0, 2).reshape(2, 1, D_MODEL)
    bp_all = small_g[:, 2].reshape(N_DEV, 2, 4, 32).transpose(1, 2, 0, 3).reshape(2, 1, D_MODEL)
    sp0 = _copies_start(groups["ffn0"], 1, _gather_spread, "spread_ffn0", deps=[small_g])

    def spread_start(nm, deps):
        return _copies_start(groups[nm], len(groups[nm]), _gather_spread, f"spread_{nm}", deps=deps)

    def spread_wait(nm, state, after):
        ssem, rsem, bufs, _ = state
        return _copies_wait(bufs, ssem, rsem, after, _blocks_moved(4), f"spread_done_{nm}")

    def relay_start(nm, bufs, deps=()):
        return _copies_start(bufs, len(bufs), _gather_relay, f"relay_{nm}", deps=deps)

    def relay_wait(nm, state, after):
        ssem, rsem, bufs, _ = state
        return _copies_wait(bufs, ssem, rsem, after, _blocks_moved(3), f"relay_done_{nm}")

    gkn = g_k[:NOPE].reshape(1, NOPE)
    gkr = _pad_cols(g_k[NOPE:].reshape(1, ROPE), LANES)
    gl = g_kv_latent.reshape(1, KV_RANK)
    lnkv = ln_kv.reshape(1, D_MODEL)

    x_in, x_mid, pooled, gates, ups, w_ffn = [], [], [], [], [], []
    qs, outs, lses = [], [], []

    def mixer(l, cur, deps):
        x_in.append(cur)
        mid, dsave = _mix_fwd(cur, ln_a_all[l], wp_all[l], bp_all[l], sc_all[l], f"mix_fwd{l}", deps=deps)
        pooled.append(dsave)
        x_mid.append(mid)
        return mid

    def q_args(j):
        return (ln_mix_b[j].reshape(1, -1), wdq_all[j], g_q_latent[j].reshape(1, -1), wuq_all[j],
                g_q[j, :NOPE].reshape(1, -1), _pad_cols(g_q[j, NOPE:].reshape(1, -1), LANES), cos, sin)

    def attention(j, cur, deps):
        x_in.append(cur)
        q = _q_fwd(cur, *q_args(j), f"q_fwd{j}", deps=deps)
        o, lse = _att_fwd(q, k_sh, v_sh, f"att_fwd{j}")
        mid = _o_fwd(cur, o, wo_all[j], f"o_fwd{j}")
        qs.append(q)
        outs.append(o)
        lses.append(lse)
        x_mid.append(mid)
        return mid

    def ffn(l, mid, relayed):
        w_l = relayed[0].reshape(3, D_FF, D_MODEL)
        w_ffn.append(w_l)
        cur, gate, up = _ffn_fwd(mid, ln_ffn[l].reshape(1, -1), w_l, f"ffn_fwd{l}")
        gates.append(gate)
        ups.append(up)
        return cur

    mid = mixer(0, xs, [sp0[3]])
    landed0 = spread_wait("ffn0", sp0, mid)
    sp1 = spread_start("ffn1", [landed0[0]])
    rl0 = relay_start("ffn0", landed0, [sp1[3]])
    cur = ffn(0, mid, relay_wait("ffn0", rl0, rl0[3]))

    landed1 = spread_wait("ffn1", sp1, cur)
    sp_att = spread_start("att", [landed1[0]])
    sp2 = spread_start("ffn2", [landed1[0]])
    rl1 = relay_start("ffn1", landed1, [sp_att[3], sp2[3]])
    mid = mixer(1, cur, [rl1[3]])
    cur = ffn(1, mid, relay_wait("ffn1", rl1, mid))
    x_kv = cur

    landed_att = spread_wait("att", sp_att, cur)
    landed2 = spread_wait("ffn2", sp2, cur)
    sp3 = spread_start("ffn3", [landed2[0]])
    rl_att = relay_start("att", landed_att, [sp3[3]])
    rl2 = relay_start("ffn2", landed2, [sp3[3]])
    att_bufs = relay_wait("att", rl_att, rl2[3])
    wc = att_bufs[0].reshape(D_MODEL, KV_RANK)
    wpe = att_bufs[1].reshape(D_MODEL, LANES)
    wuk_g = att_bufs[2].reshape(N_HEADS, KV_RANK, NOPE).transpose(1, 0, 2).reshape(KV_RANK, N_HEADS * NOPE)
    wuv_g = att_bufs[3].reshape(N_HEADS, KV_RANK, V_DIM).transpose(1, 0, 2).reshape(KV_RANK, N_HEADS * V_DIM)
    wdq_all = att_bufs[4].reshape(2, D_MODEL, Q_RANK)
    wuq_all = att_bufs[5]
    wo_all = att_bufs[6].reshape(2, D_MODEL, D_MODEL)
    k_sh, v_sh = _kv_fwd(cur, lnkv, wc, wpe, gl, wuk_g, wuv_g, gkn, gkr, cos, sin, "kv_fwd")
    mid = attention(0, cur, [])
    cur = ffn(2, mid, relay_wait("ffn2", rl2, mid))

    landed3 = spread_wait("ffn3", sp3, cur)
    rl3 = relay_start("ffn3", landed3)
    mid = attention(1, cur, [rl3[3]])
    cur = ffn(3, mid, relay_wait("ffn3", rl3, mid))

    dx, sq_cols = _loss_head(cur, target, "loss_head")

    small = {}
    sizes = dict(ffn0=FF_SHARD, ffn1=FF_SHARD, ffn2=FF_SHARD, ffn3=FF_SHARD, wo=128, kv512=128, dkv_pe=128,
                 wdq=128, wuqT=QK_PAD, wpool=32)
    big = dict(wo=lax.empty((2, D_MODEL, D_MODEL), BF16), kv512=lax.empty((3, D_MODEL, KV_RANK), BF16),
               dkv_pe=lax.empty((1, D_MODEL, LANES), BF16), wdq=lax.empty((2, D_MODEL, Q_RANK), BF16),
               wuqT=lax.empty((2, N_HEADS * QK_PAD, Q_RANK), BF16), wpool=lax.empty((8, GROUP_DIM, GROUP_DIM), BF16))
    for l in range(4):
        big[f"ffn{l}"] = lax.empty((3, D_FF, D_MODEL), BF16)
    red = {}

    def pair_start(nms, tag):
        arrs = []
        for nm in nms:
            view = _owner_view(big[nm], sizes[nm])
            arrs += [view, lax.empty((view.shape[0], N_CHIPS) + view.shape[3:], BF16)]
        return nms, tag, _copies_start(arrs, len(nms), _pair_send, f"pair_start_{tag}")

    def chip_start(state, after):
        nms, tag, (ssem, rsem, arrs, _) = state
        arrs = _copies_wait(arrs, ssem, rsem, after, _landed, f"pair_done_{tag}")
        out = []
        for t, nm in enumerate(nms):
            part = _pair_sum(arrs[2 * t], arrs[2 * t + 1], core, f"pair_sum_{nm}")
            out += [part, lax.empty((3, part.shape[0]) + part.shape[2:], BF16)]
        return nms, tag, _copies_start(out, len(nms), _chip_send, f"chip_start_{tag}")

    deferred = []

    def chip_finish(state, after, defer=False):
        nms, tag, (ssem, rsem, arrs, _) = state
        arrs = _copies_wait(arrs, ssem, rsem, after, _landed, f"chip_done_{tag}")
        for t, nm in enumerate(nms):
            if defer:
                deferred.append((nm, arrs[2 * t], arrs[2 * t + 1]))
            else:
                red[nm] = _chip_sum(arrs[2 * t], arrs[2 * t + 1], chip, f"chip_sum_{nm}")

    ffn_grads = {nm: lax.empty((4, FF_SHARD, D_MODEL), F32) for nm in ("w_gate", "w_up", "w_down")}

    def place_ffn_grads(l):
        g = red[f"ffn{l}"]
        for k, nm in enumerate(("w_gate", "w_up", "w_down")):
            ffn_grads[nm] = ffn_grads[nm].at[l].set(g[k])

    dks, dvs = [], []
    pending = None
    bwd_deps = []
    for l in (3, 2, 1, 0):
        key = f"ffn{l}"
        dx, act, dgb, dub, hn, dyb, dln = _ffn_bwd(x_mid[l], dx, gates[l], ups[l], ln_ffn[l].reshape(1, -1),
                                                     w_ffn[l], f"ffn_bwd{l}", deps=bwd_deps)
        bwd_deps = []
        small[f"ln_ffn{l}"] = dln
        if l == 1:
            att_chip = chip_start(att_pair, dx)
            tn_deps = [att_chip[2][3]]
        else:
            tn_deps = []
        if pending:
            chip_finish(pending, dx, defer=True)
            pending = None
        big[key] = _tn_matmul(dgb, hn, big[key], 0, f"dw_gate{l}", m_chunk=FF_HALF, deps=tn_deps)
        big[key] = _tn_matmul(dub, hn, big[key], 1, f"dw_up{l}", m_chunk=FF_HALF)
        big[key] = _tn_matmul(act, dyb, big[key], 2, f"dw_down{l}", m_chunk=FF_HALF)
        if l == 1:
            chip_finish(att_chip, big[key])
        ffn_pair = pair_start([key], key)
        if l >= 2:
            j = l - 2
            do, dxb, stats = _o_bwd(dx, wo_all[j], outs[j], lses[j], f"o_bwd{j}", deps=[ffn_pair[2][3]])
            big["wo"] = _tn_matmul(outs[j], dxb, big["wo"], j, f"dw_o{j}")
            ffn_chip = chip_start(ffn_pair, big["wo"])
            dq, dk, dv = _att_bwd(qs[j], k_sh, v_sh, do, stats, f"att_bwd{j}", deps=[ffn_chip[2][3]])
            chip_finish(ffn_chip, dq, defer=True)
            dks.append(dk)
            dvs.append(dv)
            dx, hnq, cqn, dqa, dcq, dln, dgql, dgqn, dgqr = _q_bwd(x_in[l], dx, dq, *q_args(j), f"q_bwd{j}")
            small[f"ln_mix_b{j}"] = dln
            small[f"g_q_latent{j}"] = dgql
            small[f"g_q{j}"] = jnp.concatenate([dgqn, dgqr[:, :ROPE]], axis=1)
            big["wdq"] = _tn_matmul(hnq, dcq, big["wdq"], j, f"dw_dq{j}")
            big["wuqT"] = _tn_matmul(dqa, cqn, big["wuqT"], j, f"dw_uq{j}")
            if l == 2:
                (dx, hnk, cn, dknb, dvb, dccb, dpeb, dlnkv, dgl, dgkn, dgkr) = _kv_bwd(
                    x_kv, dx, dks, dvs, lnkv, wc, wpe, gl, wuk_g, wuv_g, gkn, gkr, cos, sin, "kv_bwd")
                small["ln_kv"] = dlnkv
                small["g_kv_latent"] = dgl
                small["g_k"] = jnp.concatenate([dgkn, dgkr[:, :ROPE]], axis=1)
                big["kv512"] = _tn_matmul(dknb, cn, big["kv512"], 0, "dw_uk")
                big["kv512"] = _tn_matmul(dvb, cn, big["kv512"], 1, "dw_uv")
                big["kv512"] = _tn_matmul(hnk, dccb, big["kv512"], 2, "dw_dkv_c")
                big["dkv_pe"] = _tn_matmul(hnk, dpeb, big["dkv_pe"], 0, "dw_dkv_pe")
                att_pair = pair_start(["wo", "kv512", "dkv_pe", "wdq", "wuqT"], "att")
                bwd_deps = [att_pair[2][3]]
        else:
            dx, dyp, dsc, db, dln = _mix_bwd(x_in[l], dx, pooled[l], ln_a_all[l], wp_all[l], bp_all[l], sc_all[l],
                                             f"mix_bwd{l}", deps=[ffn_pair[2][3]])
            small[f"ln_mix_a{l}"] = dln
            small[f"pool_scale{l}"] = dsc
            small[f"b_pool{l}"] = db
            ffn_chip = chip_start(ffn_pair, dx)
            big["wpool"] = _tn_matmul(pooled[l], dyp, big["wpool"], 4 * l, f"dw_pool{l}", groups=4,
                                      deps=[ffn_chip[2][3]])
            if l == 1:
                pending = ffn_chip
                bwd_deps = [ffn_chip[2][3]]
            else:
                for nm, part, land in deferred:
                    red[nm] = _chip_sum(part, land, chip, f"chip_sum_{nm}", deps=[ffn_chip[2][3]])
                    place_ffn_grads(int(nm[-1]))
                chip_finish(ffn_chip, [big["wpool"]] + list(ffn_grads.values()))
                place_ffn_grads(0)
    grad_x = dx[None]
    pool_pair = pair_start(["wpool"], "wpool")
    pool_chip = chip_start(pool_pair, pool_pair[2][3])
    chip_finish(pool_chip, pool_chip[2][3])

    vec_names = (["loss"] + [f"ln_ffn{l}" for l in range(4)] + ["ln_kv", "g_kv_latent", "g_k"]
                 + [f"{p}{j}" for p in ("ln_mix_b", "g_q_latent", "g_q") for j in range(2)]
                 + [f"{p}{l}" for p in ("ln_mix_a", "pool_scale", "b_pool") for l in range(2)])
    small["loss"] = sq_cols
    widths = [small[nm].shape[1] for nm in vec_names]
    padded = [-(-w // LANES) * LANES for w in widths]
    packed = jnp.concatenate([_pad_cols(small[nm], pw) for nm, pw in zip(vec_names, padded)], axis=1)
    (all_vecs,) = _all_gather([packed], [0], "gather_vectors")
    total = _sum_lead(all_vecs, "sum_vectors")
    vec = {}
    off = 0
    for nm, w, pw in zip(vec_names, widths, padded):
        vec[nm] = total[0, off:off + w]
        off += pw
    loss = 0.5 * jnp.sum(vec["loss"]) * (1.0 / D_MODEL)

    def own_cols(full, width):
        return lax.dynamic_slice_in_dim(full, dev * width, width, axis=full.ndim - 1)

    grads = dict(
        ln_mix_a=own_cols(jnp.stack([vec["ln_mix_a0"], vec["ln_mix_a1"]]), LANES),
        w_pool=red["wpool"].reshape(2, 4, 32, GROUP_DIM),
        b_pool=own_cols(jnp.stack([vec["b_pool0"], vec["b_pool1"]]).reshape(2, 4, GROUP_DIM), 32),
        pool_scale=own_cols(jnp.stack([vec["pool_scale0"], vec["pool_scale1"]]), LANES),
        ln_ffn=jnp.stack([vec[f"ln_ffn{l}"] for l in range(4)]),
        w_gate=ffn_grads["w_gate"],
        w_up=ffn_grads["w_up"],
        w_down=ffn_grads["w_down"],
        ln_kv=vec["ln_kv"],
        w_dkv=jnp.concatenate([red["kv512"][2], red["dkv_pe"][0][:, :ROPE]], axis=1),
        g_kv_latent=vec["g_kv_latent"],
        w_uk=red["kv512"][0].T,
        w_uv=red["kv512"][1].T,
        g_k=vec["g_k"],
        ln_mix_b=jnp.stack([vec["ln_mix_b0"], vec["ln_mix_b1"]]),
        w_dq=red["wdq"],
        g_q_latent=jnp.stack([vec["g_q_latent0"], vec["g_q_latent1"]]),
        w_uq=red["wuqT"].transpose(0, 2, 1)[:, :, :QK_DIM],
        g_q=jnp.stack([vec["g_q0"], vec["g_q1"]]),
        w_o=red["wo"],
    )

    deltas, new_m, new_v = {}, {}, {}
    for nm in names:
        w = weights[nm]
        if nm in ("w_gate", "w_up"):
            def swap(a):
                return a.transpose(0, 2, 1)
            d, mo, vo = _adamw(swap(w), grads[nm], swap(mom1[nm]), swap(mom2[nm]), f"adamw_{nm}")
            deltas[nm], new_m[nm], new_v[nm], grads[nm] = swap(d), swap(mo), swap(vo), swap(grads[nm])
            continue
        shape = w.shape if w.ndim > 1 else (1, w.shape[0])
        d, mo, vo = _adamw(w.reshape(shape), grads[nm].reshape(shape), mom1[nm].reshape(shape),
                           mom2[nm].reshape(shape), f"adamw_{nm}")
        deltas[nm], new_m[nm], new_v[nm] = d.reshape(w.shape), mo.reshape(w.shape), vo.reshape(w.shape)

    return (loss, grad_x, *[grads[nm].reshape(weights[nm].shape) for nm in names], *[deltas[nm] for nm in names],
            *[new_m[nm] for nm in names], *[new_v[nm] for nm in names])
```

```python
import functools
import math

import jax
import jax.numpy as jnp
from jax import lax
from jax.experimental import pallas as pl
from jax.experimental.pallas import tpu as pltpu

F32 = jnp.float32
BF16 = jnp.bfloat16
MESH = pl.DeviceIdType.MESH

D_MODEL = 1024
D_FF = 2816
N_DEV = 8
N_CHIPS = 4
FF_SHARD = D_FF // N_DEV
FF_HALF = D_FF // 2
N_HEADS = 8
NOPE = 128
ROPE = 64
QK_DIM = NOPE + ROPE
QK_PAD = 256
V_DIM = 128
Q_RANK = 256
KV_RANK = 512
POOL_WINDOWS = (2, 4, 8, 16)
GROUP_DIM = 256
HALO = 128
CHUNK = 64
ROPE_THETA = 10000.0
EPS = 1e-6
LANES = 128

ADAM_LR = 0.001
ADAM_B1 = 0.9
ADAM_B2 = 0.999
ADAM_EPS = 1e-08
ADAM_WD = 0.01
ADAM_STEP = 10

PROJ_ROWS = 256
VMEM_BIG = 56 * 2**20
VMEM_MID = 40 * 2**20


def _nn(a, b):
    return lax.dot_general(a, b, (((1,), (0,)), ((), ())), preferred_element_type=F32)


def _nt(a, b):
    return lax.dot_general(a, b, (((1,), (1,)), ((), ())), preferred_element_type=F32)


def _tn(a, b):
    return lax.dot_general(a, b, (((0,), (0,)), ((), ())), preferred_element_type=F32)


def _rms(x, g, n):
    r = lax.rsqrt(jnp.sum(x * x, axis=-1, keepdims=True) * (1.0 / n) + EPS)
    return (x * r) * g, r


def _rms_bwd(x, r, g, dy, n):
    u = dy * g
    s = jnp.sum(x * u, axis=-1, keepdims=True) * (1.0 / n)
    dx = r * u - x * (r * r * r * s)
    dg = jnp.sum(dy * (x * r), axis=0, keepdims=True)
    return dx, dg


def _swap_perm():
    i = lax.broadcasted_iota(jnp.int32, (LANES, LANES), 0)
    j = lax.broadcasted_iota(jnp.int32, (LANES, LANES), 1)
    half = ROPE // 2
    hit = ((j < half) & (i == j + half)) | ((j >= half) & (j < ROPE) & (i == j - half))
    return jnp.where(hit, 1.0, 0.0).astype(BF16)


def _swap_halves(z, perm):
    hi = z.astype(BF16)
    lo = (z - hi.astype(F32)).astype(BF16)
    return _nn(hi, perm) + _nn(lo, perm)


def _sigmoid(x):
    return 1.0 / (1.0 + jnp.exp(-x))


def _cparams(n_grid, vmem=None):
    return pltpu.CompilerParams(dimension_semantics=("arbitrary",) * n_grid, vmem_limit_bytes=vmem)


def _rows(t, cols):
    return pl.BlockSpec((t, cols), lambda i: (i, 0))


def _full(shape):
    nd = len(shape)
    return pl.BlockSpec(shape, lambda *_: (0,) * nd)


ANY = pl.BlockSpec(memory_space=pl.ANY)


def _pcall(body, args, deps, *, in_specs, **kw):
    n_in, n_dep = len(args), len(deps)

    def ordered(*refs):
        body(*refs[:n_in], *refs[n_in + n_dep:])

    return pl.pallas_call(ordered, in_specs=list(in_specs) + [ANY] * n_dep, **kw)(*args, *deps)


def _place():
    x, y, c = lax.axis_index("x"), lax.axis_index("y"), lax.axis_index("c")
    return x, y, c


def _all_gather(shards, axes, name, deps=()):
    n, nd = len(shards), len(deps)
    out_shape = [jax.ShapeDtypeStruct(s.shape[:a] + (N_DEV,) + s.shape[a:], s.dtype) for s, a in zip(shards, axes)]

    def body(*refs):
        ins, outs = refs[:n], refs[n + nd:2 * n + nd]
        send_sems, recv_sems, local_sems = refs[2 * n + nd:]
        x, y, c = _place()
        me, sibling = (x, y, c), (x, y, 1 - c)
        chips = [(1 - x, y), (x, 1 - y), (1 - x, 1 - y)]

        def slot(t, dev):
            idx = 4 * dev[0] + 2 * dev[1] + dev[2]
            return outs[t].at[(slice(None),) * axes[t] + (idx,)]

        def copy(t, k, block, to, src=None):
            return pltpu.make_async_remote_copy(
                src_ref=slot(t, block) if src is None else src, dst_ref=slot(t, block),
                send_sem=send_sems.at[t, k], recv_sem=recv_sems.at[t, k],
                device_id=to, device_id_type=MESH)

        mine = [pltpu.make_async_copy(ins[t], slot(t, me), local_sems.at[t]) for t in range(n)]
        for cp in mine:
            cp.start()
        first = []
        for t in range(n):
            first.append(copy(t, 0, me, sibling, src=ins[t]))
            first += [copy(t, 1 + j, me, (*chip, c), src=ins[t]) for j, chip in enumerate(chips)]
        for cp in first:
            cp.start()
        passed = []
        for j, chip in enumerate(chips):
            for t in range(n):
                copy(t, 1 + j, (*chip, c), me).wait_recv()
                cp = copy(t, 4 + j, (*chip, c), sibling)
                cp.start()
                passed.append(cp)
        for t in range(n):
            copy(t, 0, sibling, me).wait_recv()
            for j, chip in enumerate(chips):
                copy(t, 4 + j, (*chip, 1 - c), me).wait_recv()
        for cp in first + passed:
            cp.wait_send()
        for cp in mine:
            cp.wait()

    return pl.pallas_call(
        body, name=name, out_shape=out_shape,
        in_specs=[ANY] * (n + nd), out_specs=[ANY] * n,
        scratch_shapes=[pltpu.SemaphoreType.DMA((n, 7)), pltpu.SemaphoreType.DMA((n, 7)),
                        pltpu.SemaphoreType.DMA((n,))],
    )(*shards, *deps)


HBM = pl.BlockSpec(memory_space=pltpu.HBM)
SEM = pl.BlockSpec(memory_space=pltpu.SEMAPHORE)
EFFECT = pltpu.SideEffectType.DATAFLOW_SIDE_EFFECTING


def _copies_start(arrays, n_sems, plan, name, deps=()):
    n, nd = len(arrays), len(deps)

    def body(*refs):
        for cp in plan(refs[:n], refs[n + nd], refs[n + nd + 1]):
            cp.start()
        refs[-1][...] = jnp.zeros_like(refs[-1])

    outs = pl.pallas_call(
        body, name=name,
        out_shape=(pltpu.SemaphoreType.DMA((n_sems,)), pltpu.SemaphoreType.DMA((n_sems,)),
                   *[pltpu.HBM(a.shape, a.dtype) for a in arrays], jax.ShapeDtypeStruct((8, LANES), F32)),
        in_specs=[HBM] * n + [ANY] * nd,
        out_specs=(SEM, SEM, *[HBM] * n, pl.BlockSpec(memory_space=pltpu.VMEM)),
        input_output_aliases={i: 2 + i for i in range(n)},
        compiler_params=pltpu.CompilerParams(has_side_effects=EFFECT),
    )(*[pltpu.with_memory_space_constraint(a, pltpu.HBM) for a in arrays], *deps)
    return outs[0], outs[1], list(outs[2:2 + n]), outs[-1]


def _copies_wait(arrays, send_sems, recv_sems, after, plan, name):
    n = len(arrays)
    after = list(after) if isinstance(after, (list, tuple)) else [after]

    def body(*refs):
        for cp in plan(refs[:n], refs[n], refs[n + 1]):
            cp.wait_send()
            cp.wait_recv()

    outs = pl.pallas_call(
        body, name=name,
        out_shape=tuple(pltpu.HBM(a.shape, a.dtype) for a in arrays),
        in_specs=[HBM] * n + [SEM, SEM] + [ANY] * len(after), out_specs=tuple([HBM] * n),
        input_output_aliases={i: i for i in range(n)},
        compiler_params=pltpu.CompilerParams(has_side_effects=EFFECT),
    )(*arrays, send_sems, recv_sems, *after)
    return list(outs)


def _remote(src, dst, send_sems, recv_sems, t, to):
    return pltpu.make_async_remote_copy(src_ref=src, dst_ref=dst, send_sem=send_sems.at[t], recv_sem=recv_sems.at[t],
                                        device_id=to, device_id_type=MESH)


def _dev_index(x, y, c):
    return 4 * x + 2 * y + c


def _gather_spread(bufs, send_sems, recv_sems):
    x, y, c = _place()
    mine = _dev_index(x, y, c)
    peers = [(x, y, 1 - c), (1 - x, y, c), (x, 1 - y, c), (1 - x, 1 - y, c)]
    return [_remote(g.at[k, mine], g.at[k, mine], send_sems, recv_sems, t, peer)
            for t, g in enumerate(bufs) for peer in peers for k in range(g.shape[0])]


def _gather_relay(bufs, send_sems, recv_sems):
    x, y, c = _place()
    blocks = [_dev_index(1 - x, y, c), _dev_index(x, 1 - y, c), _dev_index(1 - x, 1 - y, c)]
    return [_remote(g.at[k, b], g.at[k, b], send_sems, recv_sems, t, (x, y, 1 - c))
            for t, g in enumerate(bufs) for b in blocks for k in range(g.shape[0])]


def _blocks_moved(count):
    def plan(bufs, send_sems, recv_sems):
        x, y, c = _place()
        return [_remote(g.at[:, pl.ds(0, count)], g.at[:, pl.ds(0, count)], send_sems, recv_sems, t, (x, y, 1 - c))
                for t, g in enumerate(bufs)]
    return plan


def _pair_send(arrs, send_sems, recv_sems):
    x, y, c = _place()
    return [_remote(arrs[2 * t].at[p, k, 1 - c], arrs[2 * t + 1].at[p, k], send_sems, recv_sems, t, (x, y, 1 - c))
            for t in range(len(arrs) // 2) for p in range(arrs[2 * t].shape[0]) for k in range(N_CHIPS)]


def _chip_send(arrs, send_sems, recv_sems):
    x, y, c = _place()
    chips = [(1 - x, y), (x, 1 - y), (1 - x, 1 - y)]
    return [_remote(arrs[2 * t].at[p, 2 * px + py], arrs[2 * t + 1].at[j, p], send_sems, recv_sems, t, (px, py, c))
            for t in range(len(arrs) // 2) for j, (px, py) in enumerate(chips) for p in range(arrs[2 * t].shape[0])]


def _landed(arrs, send_sems, recv_sems):
    x, y, c = _place()
    return [_remote(arrs[2 * t + 1], arrs[2 * t + 1], send_sems, recv_sems, t, (x, y, 1 - c))
            for t in range(len(arrs) // 2)]


def _rows_per_step(rows, row_elems):
    best = 1
    for cand in range(1, rows + 1):
        if rows % cand == 0 and cand * row_elems <= 256 * 1024:
            best = cand
    return best


def _pair_sum(grad, landed, core, name):
    p, _, _, sz, c = grad.shape
    r = _rows_per_step(p * N_CHIPS, sz * c)

    def body(core_ref, g_ref, l_ref, o_ref):
        o_ref[...] = (g_ref[...].astype(F32) + l_ref[...].astype(F32)).astype(o_ref.dtype)

    out = pl.pallas_call(
        body, name=name,
        grid_spec=pltpu.PrefetchScalarGridSpec(
            num_scalar_prefetch=1, grid=(p * N_CHIPS // r,),
            in_specs=[pl.BlockSpec((r, None, sz, c), lambda i, cr: (i, cr[0], 0, 0)),
                      pl.BlockSpec((r, sz, c), lambda i, cr: (i, 0, 0))],
            out_specs=pl.BlockSpec((r, sz, c), lambda i, cr: (i, 0, 0))),
        out_shape=jax.ShapeDtypeStruct((p * N_CHIPS, sz, c), grad.dtype),
        compiler_params=_cparams(1),
    )(core, grad.reshape(p * N_CHIPS, 2, sz, c), landed.reshape(p * N_CHIPS, sz, c))
    return out.reshape(p, N_CHIPS, sz, c)


def _chip_sum(parts, landed, chip, name, deps=()):
    p, _, sz, c = parts.shape
    r = _rows_per_step(p, sz * c)

    def body(chip_ref, a_ref, l_ref, o_ref):
        acc = a_ref[...].astype(F32)
        for j in range(3):
            acc = acc + l_ref[j].astype(F32)
        o_ref[...] = acc

    nd = len(deps)

    def ordered(chip_ref, a_ref, l_ref, *rest):
        body(chip_ref, a_ref, l_ref, rest[nd])

    return pl.pallas_call(
        ordered, name=name,
        grid_spec=pltpu.PrefetchScalarGridSpec(
            num_scalar_prefetch=1, grid=(p // r,),
            in_specs=[pl.BlockSpec((r, None, sz, c), lambda i, cr: (i, cr[0], 0, 0)),
                      pl.BlockSpec((3, r, sz, c), lambda i, cr: (0, i, 0, 0))] + [ANY] * nd,
            out_specs=pl.BlockSpec((r, sz, c), lambda i, cr: (i, 0, 0))),
        out_shape=jax.ShapeDtypeStruct((p, sz, c), F32),
        compiler_params=_cparams(1),
    )(chip, parts, landed, *deps)


def _sum_lead(a, name, out_dtype=F32):
    k = a.shape[0]
    rest = a.shape[1:]
    r, c = rest[-2], rest[-1]
    lead = math.prod(rest[:-2])
    a3 = a.reshape(k, lead * r, c)
    rows = lead * r
    tb = rows
    for cand in (512, 256, 128, 64, 32, 16, 8):
        if rows % cand == 0 and rows > cand:
            tb = cand
            break

    def body(a_ref, o_ref):
        acc = a_ref[0].astype(F32)
        for i in range(1, k):
            acc = acc + a_ref[i].astype(F32)
        o_ref[...] = acc.astype(out_dtype)

    out = pl.pallas_call(
        body, name=name, grid=(rows // tb,),
        out_shape=jax.ShapeDtypeStruct((rows, c), out_dtype),
        in_specs=[pl.BlockSpec((k, tb, c), lambda i: (0, i, 0))],
        out_specs=pl.BlockSpec((tb, c), lambda i: (i, 0)),
        compiler_params=_cparams(1),
    )(a3)
    return out.reshape(rest)


def _bands(t, causal):
    r = lax.broadcasted_iota(jnp.int32, (t, t + HALO), 0)
    col = lax.broadcasted_iota(jnp.int32, (t, t + HALO), 1)
    diff = r + HALO - col if causal else col - r
    return jnp.stack([jnp.where((diff >= 0) & (diff < w), 1.0, 0.0) for w in POOL_WINDOWS]).astype(BF16)


def _split_dot(band, v):
    hi = v.astype(BF16)
    lo = (v - hi.astype(F32)).astype(BF16)
    return _nn(band, hi) + _nn(band, lo)


def _mix_fwd(x, g, wp, b, sc, name, deps=()):
    s = x.shape[0]
    t = min(256, s)
    rb = t // HALO

    def body(x_ref, xh_ref, g_ref, wp_ref, b_ref, sc_ref, band_ref, xo_ref, d_ref):
        i = pl.program_id(0)
        gg = g_ref[...]
        h, _ = _rms(x_ref[...], gg, D_MODEL)
        hh, _ = _rms(xh_ref[...], gg, D_MODEL)
        hh = jnp.where(i > 0, hh, 0.0)
        hext = jnp.concatenate([hh, h], axis=0)
        tok = i * t + lax.broadcasted_iota(jnp.int32, (t, 1), 0)
        for gi, w in enumerate(POOL_WINDOWS):
            sl = slice(gi * GROUP_DIM, (gi + 1) * GROUP_DIM)
            win = _split_dot(band_ref[gi], hext[:, sl])
            inv = 1.0 / jnp.minimum(tok + 1, w).astype(F32)
            dbf = (win * inv - h[:, sl]).astype(BF16)
            d_ref[:, sl] = dbf
            ypre = _nn(dbf, wp_ref[gi]) + b_ref[:, sl]
            xo_ref[:, sl] = x_ref[:, sl] + ypre * sc_ref[:, sl]

    return _pcall(
        body, (x, x, g, wp, b, sc, _bands(t, True)), deps, name=name, grid=(s // t,),
        out_shape=[jax.ShapeDtypeStruct((s, D_MODEL), F32), jax.ShapeDtypeStruct((s, D_MODEL), BF16)],
        in_specs=[_rows(t, D_MODEL),
                  pl.BlockSpec((HALO, D_MODEL), lambda i: (jnp.maximum(i * rb - 1, 0), 0)),
                  _full((1, D_MODEL)), _full((4, GROUP_DIM, GROUP_DIM)), _full((1, D_MODEL)), _full((1, D_MODEL)),
                  _full((4, t, t + HALO))],
        out_specs=[_rows(t, D_MODEL), _rows(t, D_MODEL)],
        compiler_params=_cparams(1, VMEM_MID),
    )


def _mix_bwd(x, dy, d, g, wp, b, sc, name, deps=()):
    s = x.shape[0]
    t = min(256, s)
    rb = t // HALO
    nb = s // t
    last_halo = s // HALO - 1

    def body(x_ref, dy_ref, dyn_ref, d_ref, g_ref, wp_ref, b_ref, sc_ref, band_ref,
             dx_ref, dyp_ref, dsc_ref, db_ref, dln_ref):
        i = pl.program_id(0)
        x = x_ref[...]
        gg = g_ref[...]
        dy = dy_ref[...]
        sc = sc_ref[...]
        dyp32 = dy * sc
        dyp = dyp32.astype(BF16)
        dyph = (dyn_ref[...] * sc).astype(BF16)
        dyp_ref[...] = dyp
        tok = i * t + lax.broadcasted_iota(jnp.int32, (t + HALO, 1), 0)
        dh, dsc = [], []
        for gi, w in enumerate(POOL_WINDOWS):
            sl = slice(gi * GROUP_DIM, (gi + 1) * GROUP_DIM)
            ypre = _nn(d_ref[:, sl], wp_ref[gi]) + b_ref[:, sl]
            dsc.append(jnp.sum(dy[:, sl] * ypre, axis=0, keepdims=True))
            dd = _nt(dyp[:, sl], wp_ref[gi])
            ddh = jnp.where(i < nb - 1, _nt(dyph[:, sl], wp_ref[gi]), 0.0)
            inv = 1.0 / jnp.minimum(tok + 1, w).astype(F32)
            ddext = jnp.concatenate([dd, ddh], axis=0) * inv
            dh.append(_split_dot(band_ref[gi], ddext) - dd)
        dh = jnp.concatenate(dh, axis=1)
        _, r = _rms(x, gg, D_MODEL)
        dxn, dg = _rms_bwd(x, r, gg, dh, D_MODEL)
        dx_ref[...] = dy + dxn

        @pl.when(i == 0)
        def _():
            dsc_ref[...] = jnp.zeros_like(dsc_ref)
            db_ref[...] = jnp.zeros_like(db_ref)
            dln_ref[...] = jnp.zeros_like(dln_ref)

        dsc_ref[...] += jnp.concatenate(dsc, axis=1)
        db_ref[...] += jnp.sum(dyp32, axis=0, keepdims=True)
        dln_ref[...] += dg

    vec = jax.ShapeDtypeStruct((1, D_MODEL), F32)
    return _pcall(
        body, (x, dy, dy, d, g, wp, b, sc, _bands(t, False)), deps, name=name, grid=(nb,),
        out_shape=[jax.ShapeDtypeStruct((s, D_MODEL), F32), jax.ShapeDtypeStruct((s, D_MODEL), BF16), vec, vec, vec],
        in_specs=[_rows(t, D_MODEL), _rows(t, D_MODEL),
                  pl.BlockSpec((HALO, D_MODEL), lambda i: (jnp.minimum((i + 1) * rb, last_halo), 0)),
                  _rows(t, D_MODEL),
                  _full((1, D_MODEL)), _full((4, GROUP_DIM, GROUP_DIM)), _full((1, D_MODEL)), _full((1, D_MODEL)),
                  _full((4, t, t + HALO))],
        out_specs=[_rows(t, D_MODEL), _rows(t, D_MODEL), _full((1, D_MODEL)), _full((1, D_MODEL)), _full((1, D_MODEL))],
        compiler_params=_cparams(1, VMEM_MID),
    )


def _load_weights(w_hbm, w_vmem, sem):
    @pl.when(pl.program_id(0) == 0)
    def _():
        cp = pltpu.make_async_copy(w_hbm, w_vmem, sem)
        cp.start()
        cp.wait()


def _ffn_fwd(x, g, w, name):
    s = x.shape[0]
    t = min(512, s)

    def body(x_ref, g_ref, w_hbm, xo_ref, gate_ref, up_ref, w_ref, sem):
        _load_weights(w_hbm, w_ref, sem)
        x = x_ref[...]
        hn = _rms(x, g_ref[...], D_MODEL)[0].astype(BF16)
        acc = x
        for c in range(2):
            rs = slice(c * FF_HALF, (c + 1) * FF_HALF)
            gt = _nt(hn, w_ref[0, rs, :])
            up = _nt(hn, w_ref[1, rs, :])
            gate_ref[:, rs] = gt.astype(BF16)
            up_ref[:, rs] = up.astype(BF16)
            act = ((gt * _sigmoid(gt)) * up).astype(BF16)
            acc = acc + _nn(act, w_ref[2, rs, :])
        xo_ref[...] = acc

    hid = jax.ShapeDtypeStruct((s, D_FF), BF16)
    return pl.pallas_call(
        body, name=name, grid=(s // t,),
        out_shape=[jax.ShapeDtypeStruct((s, D_MODEL), F32), hid, hid],
        in_specs=[_rows(t, D_MODEL), _full((1, D_MODEL)), ANY],
        out_specs=[_rows(t, D_MODEL), _rows(t, D_FF), _rows(t, D_FF)],
        scratch_shapes=[pltpu.VMEM((3, D_FF, D_MODEL), BF16), pltpu.SemaphoreType.DMA],
        compiler_params=_cparams(1, VMEM_BIG),
    )(x, g, w)


def _ffn_bwd(x, dy, gate, up, g, w, name, deps=()):
    s = x.shape[0]
    t = min(256, s)

    def body(x_ref, dy_ref, gate_ref, up_ref, g_ref, w_hbm,
             dx_ref, act_ref, dg_ref, du_ref, hn_ref, dyb_ref, dln_ref, w_ref, sem):
        _load_weights(w_hbm, w_ref, sem)
        x = x_ref[...]
        gg = g_ref[...]
        y, r = _rms(x, gg, D_MODEL)
        hn = y.astype(BF16)
        hn_ref[...] = hn
        dy = dy_ref[...]
        dyb = dy.astype(BF16)
        dyb_ref[...] = dyb
        dh = jnp.zeros((t, D_MODEL), F32)
        for c in range(2):
            rs = slice(c * FF_HALF, (c + 1) * FF_HALF)
            gt = gate_ref[:, rs].astype(F32)
            u = up_ref[:, rs].astype(F32)
            sg = _sigmoid(gt)
            sl = gt * sg
            act_ref[:, rs] = (sl * u).astype(BF16)
            dact = _nt(dyb, w_ref[2, rs, :])
            dg = (dact * u * (sg * (1.0 + gt * (1.0 - sg)))).astype(BF16)
            du = (dact * sl).astype(BF16)
            dg_ref[:, rs] = dg
            du_ref[:, rs] = du
            dh = dh + _nn(dg, w_ref[0, rs, :]) + _nn(du, w_ref[1, rs, :])
        dxn, dgl = _rms_bwd(x, r, gg, dh, D_MODEL)
        dx_ref[...] = dy + dxn

        @pl.when(pl.program_id(0) == 0)
        def _():
            dln_ref[...] = jnp.zeros_like(dln_ref)

        dln_ref[...] += dgl

    hid = jax.ShapeDtypeStruct((s, D_FF), BF16)
    tok = jax.ShapeDtypeStruct((s, D_MODEL), BF16)
    return _pcall(
        body, (x, dy, gate, up, g, w), deps, name=name, grid=(s // t,),
        out_shape=[jax.ShapeDtypeStruct((s, D_MODEL), F32), hid, hid, hid, tok, tok,
                   jax.ShapeDtypeStruct((1, D_MODEL), F32)],
        in_specs=[_rows(t, D_MODEL), _rows(t, D_MODEL), _rows(t, D_FF), _rows(t, D_FF), _full((1, D_MODEL)), ANY],
        out_specs=[_rows(t, D_MODEL), _rows(t, D_FF), _rows(t, D_FF), _rows(t, D_FF),
                   _rows(t, D_MODEL), _rows(t, D_MODEL), _full((1, D_MODEL))],
        scratch_shapes=[pltpu.VMEM((3, D_FF, D_MODEL), BF16), pltpu.SemaphoreType.DMA],
        compiler_params=_cparams(1, VMEM_BIG),
    )


def _tn_matmul(a, b, into, p0, name, groups=1, m_chunk=None, deps=()):
    s = a.shape[0]
    m, n = a.shape[1] // groups, b.shape[1] // groups
    assert into.shape[1:] == (m, n)
    mc = m if m_chunk is None else m_chunk
    nm = m // mc
    t = min(1024, s)
    nt = s // t

    def body(a_ref, b_ref, into_ref, o_ref, acc):
        ti = pl.program_id(2)

        @pl.when(ti == 0)
        def _():
            acc[...] = jnp.zeros_like(acc)

        acc[...] += _tn(a_ref[...], b_ref[...])

        @pl.when(ti == nt - 1)
        def _():
            o_ref[...] = acc[...].astype(o_ref.dtype)

    return _pcall(
        body, (a, b, into), deps, name=name, grid=(groups, nm, nt),
        out_shape=jax.ShapeDtypeStruct(into.shape, into.dtype),
        in_specs=[pl.BlockSpec((t, mc), lambda gi, mi, ti: (ti, gi * nm + mi)),
                  pl.BlockSpec((t, n), lambda gi, mi, ti: (ti, gi)), ANY],
        out_specs=pl.BlockSpec((None, mc, n), lambda gi, mi, ti: (p0 + gi, mi, 0)),
        scratch_shapes=[pltpu.VMEM((mc, n), F32)],
        input_output_aliases={2: 0},
        compiler_params=_cparams(3, VMEM_BIG),
    )


def _rope_tables(positions):
    half = ROPE // 2
    inv = ROPE_THETA ** (-jnp.arange(half, dtype=F32) * 2.0 / ROPE)
    ang = positions.astype(F32)[:, None] * inv
    cos, sin = jnp.cos(ang), jnp.sin(ang)
    zero = jnp.zeros((positions.shape[0], LANES - ROPE), F32)
    return jnp.concatenate([cos, cos, zero], axis=1), jnp.concatenate([-sin, sin, zero], axis=1)


def _kv_specs(t):
    return [_full((1, D_MODEL)), _full((D_MODEL, KV_RANK)), _full((D_MODEL, LANES)), _full((1, KV_RANK)),
            _full((KV_RANK, N_HEADS * NOPE)), _full((KV_RANK, N_HEADS * V_DIM)),
            _full((1, NOPE)), _full((1, LANES)), _rows(t, LANES), _rows(t, LANES)]


def _kv_fwd(x, ln, wc, wpe, gl, wuk, wuv, gkn, gkr, cos, sin, name, deps=()):
    s = x.shape[0]
    t = min(PROJ_ROWS, s)

    def body(x_ref, ln_ref, wc_ref, wpe_ref, gl_ref, wuk_ref, wuv_ref, gkn_ref, gkr_ref, cos_ref, sin_ref,
             k_ref, v_ref):
        hn = _rms(x_ref[...], ln_ref[...], D_MODEL)[0].astype(BF16)
        clat = _nn(hn, wc_ref[...])
        kpe = _nn(hn, wpe_ref[...])
        cn = _rms(clat, gl_ref[...], KV_RANK)[0].astype(BF16)
        sspe = jnp.sum(kpe * kpe, axis=-1, keepdims=True)
        base = kpe * gkr_ref[...]
        rot = base * cos_ref[...] + _swap_halves(base, _swap_perm()) * sin_ref[...]
        kn_all = _nn(cn, wuk_ref[...])
        v_ref[...] = _nn(cn, wuv_ref[...]).astype(BF16)
        for h in range(N_HEADS):
            kn = kn_all[:, h * NOPE:(h + 1) * NOPE]
            r = lax.rsqrt((jnp.sum(kn * kn, axis=-1, keepdims=True) + sspe) * (1.0 / QK_DIM) + EPS)
            k_ref[:, h * QK_PAD:h * QK_PAD + NOPE] = ((kn * r) * gkn_ref[...]).astype(BF16)
            k_ref[:, h * QK_PAD + NOPE:(h + 1) * QK_PAD] = (rot * r).astype(BF16)

    return _pcall(
        body, (x, ln, wc, wpe, gl, wuk, wuv, gkn, gkr, cos, sin), deps, name=name, grid=(s // t,),
        out_shape=[jax.ShapeDtypeStruct((s, N_HEADS * QK_PAD), BF16), jax.ShapeDtypeStruct((s, N_HEADS * V_DIM), BF16)],
        in_specs=[_rows(t, D_MODEL)] + _kv_specs(t),
        out_specs=[_rows(t, N_HEADS * QK_PAD), _rows(t, N_HEADS * V_DIM)],
        compiler_params=_cparams(1, VMEM_MID),
    )


def _kv_bwd(x, dxin, dks, dvs, ln, wc, wpe, gl, wuk, wuv, gkn, gkr, cos, sin, name):
    s = x.shape[0]
    t = min(PROJ_ROWS, s)
    nk = len(dks)

    def body(*refs):
        x_ref, dxin_ref = refs[:2]
        dk_refs = refs[2:2 + nk]
        dv_refs = refs[2 + nk:2 + 2 * nk]
        (ln_ref, wc_ref, wpe_ref, gl_ref, wuk_ref, wuv_ref, gkn_ref, gkr_ref, cos_ref, sin_ref,
         dx_ref, hn_ref, cn_ref, dkn_ref, dvb_ref, dcc_ref, dpe_ref,
         dln_ref, dgl_ref, dgkn_ref, dgkr_ref) = refs[2 + 2 * nk:]
        x = x_ref[...]
        ln = ln_ref[...]
        y, rx = _rms(x, ln, D_MODEL)
        hn = y.astype(BF16)
        hn_ref[...] = hn
        clat = _nn(hn, wc_ref[...])
        kpe = _nn(hn, wpe_ref[...])
        gl = gl_ref[...]
        cy, rc = _rms(clat, gl, KV_RANK)
        cn = cy.astype(BF16)
        cn_ref[...] = cn
        sspe = jnp.sum(kpe * kpe, axis=-1, keepdims=True)
        cs, sn, perm = cos_ref[...], sin_ref[...], _swap_perm()
        gkn, gkr = gkn_ref[...], gkr_ref[...]
        base = kpe * gkr
        rot = base * cs + _swap_halves(base, perm) * sn
        dkr_sum = jnp.zeros((t, LANES), F32)
        coef_sum = jnp.zeros((t, 1), F32)
        dgkn = jnp.zeros((1, NOPE), F32)
        kn_all = _nn(cn, wuk_ref[...])
        dkn_heads = []
        for h in range(N_HEADS):
            kn = kn_all[:, h * NOPE:(h + 1) * NOPE]
            r = lax.rsqrt((jnp.sum(kn * kn, axis=-1, keepdims=True) + sspe) * (1.0 / QK_DIM) + EPS)
            lo, mid, hi = h * QK_PAD, h * QK_PAD + NOPE, (h + 1) * QK_PAD
            dko = dk_refs[0][:, lo:mid]
            dkr = dk_refs[0][:, mid:hi]
            for j in range(1, nk):
                dko = dko + dk_refs[j][:, lo:mid]
                dkr = dkr + dk_refs[j][:, mid:hi]
            un = dko * gkn
            sm = (jnp.sum(kn * un, axis=-1, keepdims=True) + jnp.sum(rot * dkr, axis=-1, keepdims=True)) * (1.0 / QK_DIM)
            coef = r * r * r * sm
            dkn = (r * un - kn * coef).astype(BF16)
            dkr_sum = dkr_sum + r * dkr
            coef_sum = coef_sum + coef
            dgkn = dgkn + jnp.sum(dko * (kn * r), axis=0, keepdims=True)
            dkn_heads.append(dkn)
        dkn_all = jnp.concatenate(dkn_heads, axis=1)
        dkn_ref[...] = dkn_all
        dv_all = dv_refs[0][...]
        for j in range(1, nk):
            dv_all = dv_all + dv_refs[j][...]
        dvb = dv_all.astype(BF16)
        dvb_ref[...] = dvb
        dc = _nt(dkn_all, wuk_ref[...]) + _nt(dvb, wuv_ref[...])
        dz = dkr_sum * cs - _swap_halves(dkr_sum, perm) * sn
        dkpe = dz * gkr - kpe * coef_sum
        dgkr = jnp.sum(dz * kpe, axis=0, keepdims=True)
        dclat, dgl = _rms_bwd(clat, rc, gl, dc, KV_RANK)
        dcc = dclat.astype(BF16)
        dpe = dkpe.astype(BF16)
        dcc_ref[...] = dcc
        dpe_ref[...] = dpe
        dhn = _nt(dcc, wc_ref[...]) + _nt(dpe, wpe_ref[...])
        dxn, dln = _rms_bwd(x, rx, ln, dhn, D_MODEL)
        dx_ref[...] = dxin_ref[...] + dxn

        @pl.when(pl.program_id(0) == 0)
        def _():
            dln_ref[...] = jnp.zeros_like(dln_ref)
            dgl_ref[...] = jnp.zeros_like(dgl_ref)
            dgkn_ref[...] = jnp.zeros_like(dgkn_ref)
            dgkr_ref[...] = jnp.zeros_like(dgkr_ref)

        dln_ref[...] += dln
        dgl_ref[...] += dgl
        dgkn_ref[...] += dgkn
        dgkr_ref[...] += dgkr

    def tok(cols, dt):
        return jax.ShapeDtypeStruct((s, cols), dt)

    def vec(cols):
        return jax.ShapeDtypeStruct((1, cols), F32)

    return pl.pallas_call(
        body, name=name, grid=(s // t,),
        out_shape=[tok(D_MODEL, F32), tok(D_MODEL, BF16), tok(KV_RANK, BF16), tok(N_HEADS * NOPE, BF16),
                   tok(N_HEADS * V_DIM, BF16), tok(KV_RANK, BF16), tok(LANES, BF16),
                   vec(D_MODEL), vec(KV_RANK), vec(NOPE), vec(LANES)],
        in_specs=[_rows(t, D_MODEL), _rows(t, D_MODEL)] + [_rows(t, N_HEADS * QK_PAD)] * nk
                 + [_rows(t, N_HEADS * V_DIM)] * nk + _kv_specs(t),
        out_specs=[_rows(t, D_MODEL), _rows(t, D_MODEL), _rows(t, KV_RANK), _rows(t, N_HEADS * NOPE),
                   _rows(t, N_HEADS * V_DIM), _rows(t, KV_RANK), _rows(t, LANES),
                   _full((1, D_MODEL)), _full((1, KV_RANK)), _full((1, NOPE)), _full((1, LANES))],
        compiler_params=_cparams(1, VMEM_BIG),
    )(x, dxin, *dks, *dvs, ln, wc, wpe, gl, wuk, wuv, gkn, gkr, cos, sin)


def _q_specs(t):
    return [_full((1, D_MODEL)), _full((D_MODEL, Q_RANK)), _full((1, Q_RANK)), _full((N_HEADS, Q_RANK, QK_PAD)),
            _full((1, NOPE)), _full((1, LANES)), _rows(t, LANES), _rows(t, LANES)]


def _q_fwd(x, ln, wdq, gql, wuq, gqn, gqr, cos, sin, name, deps=()):
    s = x.shape[0]
    t = min(PROJ_ROWS, s)

    def body(x_ref, ln_ref, wdq_ref, gql_ref, wuq_ref, gqn_ref, gqr_ref, cos_ref, sin_ref, q_ref):
        hn = _rms(x_ref[...], ln_ref[...], D_MODEL)[0].astype(BF16)
        cqn = _rms(_nn(hn, wdq_ref[...]), gql_ref[...], Q_RANK)[0].astype(BF16)
        cs, sn, perm = cos_ref[...], sin_ref[...], _swap_perm()
        for h in range(N_HEADS):
            qa = _nn(cqn, wuq_ref[h])
            r = lax.rsqrt(jnp.sum(qa * qa, axis=-1, keepdims=True) * (1.0 / QK_DIM) + EPS)
            q_ref[:, h * QK_PAD:h * QK_PAD + NOPE] = ((qa[:, :NOPE] * r) * gqn_ref[...]).astype(BF16)
            z = (qa[:, NOPE:] * r) * gqr_ref[...]
            q_ref[:, h * QK_PAD + NOPE:(h + 1) * QK_PAD] = (z * cs + _swap_halves(z, perm) * sn).astype(BF16)

    return _pcall(
        body, (x, ln, wdq, gql, wuq, gqn, gqr, cos, sin), deps, name=name, grid=(s // t,),
        out_shape=jax.ShapeDtypeStruct((s, N_HEADS * QK_PAD), BF16),
        in_specs=[_rows(t, D_MODEL)] + _q_specs(t),
        out_specs=_rows(t, N_HEADS * QK_PAD),
        compiler_params=_cparams(1, VMEM_MID),
    )


def _q_bwd(x, dxin, dq, ln, wdq, gql, wuq, gqn, gqr, cos, sin, name):
    s = x.shape[0]
    t = min(PROJ_ROWS, s)

    def body(x_ref, dxin_ref, dq_ref, ln_ref, wdq_ref, gql_ref, wuq_ref, gqn_ref, gqr_ref, cos_ref, sin_ref,
             dx_ref, hn_ref, cqn_ref, dqa_ref, dcq_ref, dln_ref, dgql_ref, dgqn_ref, dgqr_ref):
        x = x_ref[...]
        ln = ln_ref[...]
        y, rx = _rms(x, ln, D_MODEL)
        hn = y.astype(BF16)
        hn_ref[...] = hn
        cqp = _nn(hn, wdq_ref[...])
        gql = gql_ref[...]
        cy, rc = _rms(cqp, gql, Q_RANK)
        cqn = cy.astype(BF16)
        cqn_ref[...] = cqn
        cs, sn, perm = cos_ref[...], sin_ref[...], _swap_perm()
        gqn, gqr = gqn_ref[...], gqr_ref[...]
        dcq = jnp.zeros((t, Q_RANK), F32)
        dgqn = jnp.zeros((1, NOPE), F32)
        dgqr = jnp.zeros((1, LANES), F32)
        for h in range(N_HEADS):
            qa = _nn(cqn, wuq_ref[h])
            qn, qr = qa[:, :NOPE], qa[:, NOPE:]
            r = lax.rsqrt(jnp.sum(qa * qa, axis=-1, keepdims=True) * (1.0 / QK_DIM) + EPS)
            dqo = dq_ref[:, h * QK_PAD:h * QK_PAD + NOPE]
            dqr = dq_ref[:, h * QK_PAD + NOPE:(h + 1) * QK_PAD]
            dz = dqr * cs - _swap_halves(dqr, perm) * sn
            un = dqo * gqn
            ur = dz * gqr
            sm = (jnp.sum(qn * un, axis=-1, keepdims=True) + jnp.sum(qr * ur, axis=-1, keepdims=True)) * (1.0 / QK_DIM)
            coef = r * r * r * sm
            dqa = jnp.concatenate([r * un - qn * coef, r * ur - qr * coef], axis=1).astype(BF16)
            dgqn = dgqn + jnp.sum(dqo * (qn * r), axis=0, keepdims=True)
            dgqr = dgqr + jnp.sum(dz * (qr * r), axis=0, keepdims=True)
            dqa_ref[:, h * QK_PAD:(h + 1) * QK_PAD] = dqa
            dcq = dcq + _nt(dqa, wuq_ref[h])
        dcqp, dgql = _rms_bwd(cqp, rc, gql, dcq, Q_RANK)
        dcqb = dcqp.astype(BF16)
        dcq_ref[...] = dcqb
        dhn = _nt(dcqb, wdq_ref[...])
        dxn, dln = _rms_bwd(x, rx, ln, dhn, D_MODEL)
        dx_ref[...] = dxin_ref[...] + dxn

        @pl.when(pl.program_id(0) == 0)
        def _():
            dln_ref[...] = jnp.zeros_like(dln_ref)
            dgql_ref[...] = jnp.zeros_like(dgql_ref)
            dgqn_ref[...] = jnp.zeros_like(dgqn_ref)
            dgqr_ref[...] = jnp.zeros_like(dgqr_ref)

        dln_ref[...] += dln
        dgql_ref[...] += dgql
        dgqn_ref[...] += dgqn
        dgqr_ref[...] += dgqr

    def tok(cols, dt):
        return jax.ShapeDtypeStruct((s, cols), dt)

    def vec(cols):
        return jax.ShapeDtypeStruct((1, cols), F32)

    return pl.pallas_call(
        body, name=name, grid=(s // t,),
        out_shape=[tok(D_MODEL, F32), tok(D_MODEL, BF16), tok(Q_RANK, BF16), tok(N_HEADS * QK_PAD, BF16),
                   tok(Q_RANK, BF16), vec(D_MODEL), vec(Q_RANK), vec(NOPE), vec(LANES)],
        in_specs=[_rows(t, D_MODEL), _rows(t, D_MODEL), _rows(t, N_HEADS * QK_PAD)] + _q_specs(t),
        out_specs=[_rows(t, D_MODEL), _rows(t, D_MODEL), _rows(t, Q_RANK), _rows(t, N_HEADS * QK_PAD),
                   _rows(t, Q_RANK), _full((1, D_MODEL)), _full((1, Q_RANK)), _full((1, NOPE)), _full((1, LANES))],
        compiler_params=_cparams(1, VMEM_MID),
    )(x, dxin, dq, ln, wdq, gql, wuq, gqn, gqr, cos, sin)


SM_SCALE = 1.0 / math.sqrt(QK_DIM)
LOG2_E = math.log2(math.e)
EXP2_SCALE = SM_SCALE * LOG2_E
NEG = -1e30


def _diag_mask(t):
    qpos = lax.broadcasted_iota(jnp.int32, (t, t), 0)
    kpos = lax.broadcasted_iota(jnp.int32, (t, t), 1)
    return lax.shift_right_logical(kpos, 6) <= lax.shift_right_logical(qpos, 6)


def _att_fwd(q, k, v, name):
    s = q.shape[0]
    t = min(512, s)
    nb = s // t

    def body(q_ref, k_ref, v_ref, o_ref, lse_ref):
        qi = pl.program_id(1)
        qq = q_ref[...]

        def block(ki, carry, masked):
            m_old, l_old, acc = carry
            rows = pl.ds(pl.multiple_of(ki * t, t), t)
            sc = _nt(qq, k_ref[rows, :])
            if masked:
                sc = jnp.where(_diag_mask(t), sc, NEG)
            m_new = jnp.maximum(m_old, jnp.max(sc, axis=-1, keepdims=True))
            p = jnp.exp2((sc - m_new) * EXP2_SCALE)
            alpha = jnp.exp2((m_old - m_new) * EXP2_SCALE)
            l_new = alpha * l_old + jnp.sum(p, axis=-1, keepdims=True)
            acc = alpha * acc + _nn(p.astype(BF16), v_ref[rows, :])
            return m_new, l_new, acc

        init = (jnp.full((t, 1), NEG, F32), jnp.zeros((t, 1), F32), jnp.zeros((t, V_DIM), F32))
        def pair(k0, c):
            return block(k0 + 1, block(k0, c, False), False)

        carry = lax.fori_loop(0, qi // 4, lambda j, c: pair(4 * j + 2, pair(4 * j, c)), init)
        done = 4 * (qi // 4)
        carry = lax.cond((qi & 2) != 0, lambda c: pair(done, c), lambda c: c, carry)
        carry = lax.cond((qi & 1) != 0, lambda c: block(qi - 1, c, False), lambda c: c, carry)
        m_fin, l_fin, acc = block(qi, carry, True)
        o_ref[...] = (acc / l_fin).astype(BF16)
        lse_ref[...] = jnp.broadcast_to(m_fin * SM_SCALE + jnp.log(l_fin), (t, LANES))

    return pl.pallas_call(
        body, name=name, grid=(N_HEADS, nb),
        out_shape=[jax.ShapeDtypeStruct((s, N_HEADS * V_DIM), BF16), jax.ShapeDtypeStruct((s, N_HEADS * LANES), F32)],
        in_specs=[pl.BlockSpec((t, QK_PAD), lambda h, qi: (qi, h)),
                  pl.BlockSpec((s, QK_PAD), lambda h, qi: (0, h)),
                  pl.BlockSpec((s, V_DIM), lambda h, qi: (0, h))],
        out_specs=[pl.BlockSpec((t, V_DIM), lambda h, qi: (qi, h)),
                   pl.BlockSpec((t, LANES), lambda h, qi: (qi, h))],
        compiler_params=_cparams(2, VMEM_MID),
    )(q, k, v)


def _att_bwd(q, k, v, do, stats, name, deps=()):
    s = q.shape[0]
    t = min(512, s)
    nb = s // t

    def body(q_ref, k_ref, v_ref, do_ref, st_ref, dq_ref, dk_ref, dv_ref):
        ki = pl.program_id(1)
        kk, vv = k_ref[...], v_ref[...]

        @pl.when(ki == 0)
        def _():
            dq_ref[...] = jnp.zeros_like(dq_ref)

        def block(qi, carry, masked):
            dk, dv = carry
            rows = pl.ds(pl.multiple_of(qi * t, t), t)
            qq, dob = q_ref[rows, :], do_ref[rows, :]
            sc = _nt(qq, kk)
            if masked:
                sc = jnp.where(_diag_mask(t), sc, NEG)
            st = st_ref[rows, :]
            p = jnp.exp2(sc * EXP2_SCALE - st[:, 0:1])
            dp = _nt(dob, vv)
            ds = (p * (dp - st[:, 1:2])).astype(BF16)
            dq_ref[rows, :] += _nn(ds, kk)
            return dk + _tn(ds, qq), dv + _tn(p.astype(BF16), dob)

        carry = block(ki, (jnp.zeros((t, QK_PAD), F32), jnp.zeros((t, V_DIM), F32)), True)
        rest = nb - 1 - ki
        carry = lax.fori_loop(
            0, rest // 2, lambda j, c: block(ki + 2 * j + 2, block(ki + 2 * j + 1, c, False), False), carry)
        dk, dv = lax.cond(rest % 2 == 1, lambda c: block(nb - 1, c, False), lambda c: c, carry)
        dk_ref[...] = dk * SM_SCALE
        dv_ref[...] = dv

        @pl.when(ki == nb - 1)
        def _():
            dq_ref[...] = dq_ref[...] * SM_SCALE

    def head(h, ki):
        return (0, h)

    def kblock(h, ki):
        return (ki, h)

    return _pcall(
        body, (q, k, v, do, stats), deps, name=name, grid=(N_HEADS, nb),
        out_shape=[jax.ShapeDtypeStruct((s, N_HEADS * QK_PAD), F32), jax.ShapeDtypeStruct((s, N_HEADS * QK_PAD), F32),
                   jax.ShapeDtypeStruct((s, N_HEADS * V_DIM), F32)],
        in_specs=[pl.BlockSpec((s, QK_PAD), head), pl.BlockSpec((t, QK_PAD), kblock), pl.BlockSpec((t, V_DIM), kblock),
                  pl.BlockSpec((s, V_DIM), head), pl.BlockSpec((s, LANES), head)],
        out_specs=[pl.BlockSpec((s, QK_PAD), head), pl.BlockSpec((t, QK_PAD), kblock), pl.BlockSpec((t, V_DIM), kblock)],
        compiler_params=_cparams(2, VMEM_MID),
    )


def _o_fwd(x, o, wo, name):
    s = x.shape[0]
    t = min(512, s)

    def body(x_ref, o_ref, wo_ref, xo_ref):
        xo_ref[...] = x_ref[...] + _nn(o_ref[...], wo_ref[...])

    return pl.pallas_call(
        body, name=name, grid=(s // t,),
        out_shape=jax.ShapeDtypeStruct((s, D_MODEL), F32),
        in_specs=[_rows(t, D_MODEL), _rows(t, D_MODEL), _full((D_MODEL, D_MODEL))],
        out_specs=_rows(t, D_MODEL),
        compiler_params=_cparams(1, VMEM_MID),
    )(x, o, wo)


def _o_bwd(dx, wo, o, lse, name, deps=()):
    s = dx.shape[0]
    t = min(512, s)

    def body(dx_ref, wo_ref, o_ref, lse_ref, do_ref, dxb_ref, st_ref):
        dxb = dx_ref[...].astype(BF16)
        dxb_ref[...] = dxb
        dob = _nt(dxb, wo_ref[...]).astype(BF16)
        do_ref[...] = dob
        lane = lax.broadcasted_iota(jnp.int32, (t, LANES), 1)
        for h in range(N_HEADS):
            sl = slice(h * V_DIM, (h + 1) * V_DIM)
            dsum = jnp.sum(dob[:, sl].astype(F32) * o_ref[:, sl].astype(F32), axis=-1, keepdims=True)
            st_ref[:, sl] = jnp.where(lane == 0, lse_ref[:, sl] * LOG2_E, jnp.where(lane == 1, dsum, 0.0))

    tok = jax.ShapeDtypeStruct((s, D_MODEL), BF16)
    return _pcall(
        body, (dx, wo, o, lse), deps, name=name, grid=(s // t,),
        out_shape=[tok, tok, jax.ShapeDtypeStruct((s, N_HEADS * LANES), F32)],
        in_specs=[_rows(t, D_MODEL), _full((D_MODEL, D_MODEL)), _rows(t, D_MODEL), _rows(t, N_HEADS * LANES)],
        out_specs=[_rows(t, D_MODEL), _rows(t, D_MODEL), _rows(t, N_HEADS * LANES)],
        compiler_params=_cparams(1, VMEM_MID),
    )


def _loss_head(y, target, name):
    s = y.shape[0]
    t = min(512, s)

    def body(y_ref, t_ref, dy_ref, sq_ref):
        e = y_ref[...] - t_ref[...]
        dy_ref[...] = e * (1.0 / D_MODEL)

        @pl.when(pl.program_id(0) == 0)
        def _():
            sq_ref[...] = jnp.zeros_like(sq_ref)

        sq_ref[...] += jnp.sum(e * e, axis=0, keepdims=True)

    return pl.pallas_call(
        body, name=name, grid=(s // t,),
        out_shape=[jax.ShapeDtypeStruct((s, D_MODEL), F32), jax.ShapeDtypeStruct((1, D_MODEL), F32)],
        in_specs=[_rows(t, D_MODEL), _rows(t, D_MODEL)],
        out_specs=[_rows(t, D_MODEL), _full((1, D_MODEL))],
        compiler_params=_cparams(1),
    )(y, target)


def _adamw(w, g, m, v, name):
    shape = w.shape
    c = shape[-1]
    r = math.prod(shape[:-1])
    tb = r
    for cand in (512, 256, 128):
        if r % cand == 0 and r > cand:
            tb = cand
            break

    def body(w_ref, g_ref, m_ref, v_ref, d_ref, mo_ref, vo_ref):
        gr = g_ref[...]
        mn = ADAM_B1 * m_ref[...] + (1.0 - ADAM_B1) * gr
        vn = ADAM_B2 * v_ref[...] + (1.0 - ADAM_B2) * (gr * gr)
        m_hat = mn / (1.0 - ADAM_B1 ** ADAM_STEP)
        v_hat = vn / (1.0 - ADAM_B2 ** ADAM_STEP)
        d_ref[...] = -ADAM_LR * (m_hat / (jnp.sqrt(v_hat) + ADAM_EPS) + ADAM_WD * w_ref[...])
        mo_ref[...] = mn
        vo_ref[...] = vn

    spec = pl.BlockSpec((tb, c), lambda i: (i, 0))
    flat = jax.ShapeDtypeStruct((r, c), F32)
    outs = pl.pallas_call(
        body, name=name, grid=(r // tb,),
        out_shape=[flat, flat, flat],
        in_specs=[spec] * 4, out_specs=[spec] * 3,
        compiler_params=_cparams(1),
    )(w.reshape(r, c), g.reshape(r, c), m.reshape(r, c), v.reshape(r, c))
    return [a.reshape(shape) for a in outs]


def _pad_cols(a, width):
    return jnp.pad(a, [(0, 0)] * (a.ndim - 1) + [(0, width - a.shape[-1])])


def _owner_view(a, sz):
    return a.reshape(a.shape[0], N_CHIPS, 2, sz, a.shape[-1])


def kernel(x, positions, ln_mix_a, w_pool, b_pool, pool_scale, ln_ffn, w_gate, w_up, w_down, ln_kv, w_dkv, g_kv_latent, w_uk, w_uv, g_k, ln_mix_b, w_dq, g_q_latent, w_uq, g_q, w_o, loss_target, m_ln_mix_a, m_w_pool, m_b_pool, m_pool_scale, m_ln_ffn, m_w_gate, m_w_up, m_w_down, m_ln_kv, m_w_dkv, m_g_kv_latent, m_w_uk, m_w_uv, m_g_k, m_ln_mix_b, m_w_dq, m_g_q_latent, m_w_uq, m_g_q, m_w_o, v_ln_mix_a, v_w_pool, v_b_pool, v_pool_scale, v_ln_ffn, v_w_gate, v_w_up, v_w_down, v_ln_kv, v_w_dkv, v_g_kv_latent, v_w_uk, v_w_uv, v_g_k, v_ln_mix_b, v_w_dq, v_g_q_latent, v_w_uq, v_g_q, v_w_o):
    weights = dict(ln_mix_a=ln_mix_a, w_pool=w_pool, b_pool=b_pool, pool_scale=pool_scale, ln_ffn=ln_ffn,
                   w_gate=w_gate, w_up=w_up, w_down=w_down, ln_kv=ln_kv, w_dkv=w_dkv, g_kv_latent=g_kv_latent,
                   w_uk=w_uk, w_uv=w_uv, g_k=g_k, ln_mix_b=ln_mix_b, w_dq=w_dq, g_q_latent=g_q_latent,
                   w_uq=w_uq, g_q=g_q, w_o=w_o)
    mom1 = dict(ln_mix_a=m_ln_mix_a, w_pool=m_w_pool, b_pool=m_b_pool, pool_scale=m_pool_scale, ln_ffn=m_ln_ffn,
                w_gate=m_w_gate, w_up=m_w_up, w_down=m_w_down, ln_kv=m_ln_kv, w_dkv=m_w_dkv,
                g_kv_latent=m_g_kv_latent, w_uk=m_w_uk, w_uv=m_w_uv, g_k=m_g_k, ln_mix_b=m_ln_mix_b, w_dq=m_w_dq,
                g_q_latent=m_g_q_latent, w_uq=m_w_uq, g_q=m_g_q, w_o=m_w_o)
    mom2 = dict(ln_mix_a=v_ln_mix_a, w_pool=v_w_pool, b_pool=v_b_pool, pool_scale=v_pool_scale, ln_ffn=v_ln_ffn,
                w_gate=v_w_gate, w_up=v_w_up, w_down=v_w_down, ln_kv=v_ln_kv, w_dkv=v_w_dkv,
                g_kv_latent=v_g_kv_latent, w_uk=v_w_uk, w_uv=v_w_uv, g_k=v_g_k, ln_mix_b=v_ln_mix_b, w_dq=v_w_dq,
                g_q_latent=v_g_q_latent, w_uq=v_w_uq, g_q=v_g_q, w_o=v_w_o)
    names = list(weights)
    dev = 4 * lax.axis_index("x") + 2 * lax.axis_index("y") + lax.axis_index("c")
    core = lax.axis_index("c").astype(jnp.int32).reshape(1)
    chip = (2 * lax.axis_index("x") + lax.axis_index("y")).astype(jnp.int32).reshape(1)

    xs = x[0]
    target = loss_target[0]
    cos, sin = _rope_tables(positions[0])

    def placed(shard):
        buf = lax.empty((shard.shape[0], N_DEV) + shard.shape[1:], shard.dtype)
        return lax.dynamic_update_slice(buf, shard[:, None], (0, dev, 0, 0))

    groups = {f"ffn{l}": [placed(jnp.stack([w_gate[l].T, w_up[l].T, w_down[l]]).astype(BF16))] for l in range(4)}
    groups["att"] = [placed(a.astype(BF16)) for a in (
        w_dkv[None, :, :KV_RANK], _pad_cols(w_dkv[None, :, KV_RANK:], LANES), w_uk[None], w_uv[None],
        w_dq, _pad_cols(w_uq, QK_PAD), w_o)]
    small_sh = jnp.concatenate([ln_mix_a.reshape(1, -1), pool_scale.reshape(1, -1), b_pool.reshape(1, -1)], axis=1)
    wp_g, small_g = _all_gather([w_pool.astype(BF16), small_sh], [2, 0], "gather_first")
    wp_all = wp_g.reshape(2, 4, GROUP_DIM, GROUP_DIM)
    small_g = small_g.reshape(N_DEV, 3, 2, LANES)
    ln_a_all = small_g[:, 0].transpose(1, 0, 2).reshape(2, 1, D_MODEL)
    sc_all = small_g[:, 1].transpose(1, 0, 2).reshape(2, 1, D_MODEL)
    bp_all = small_g[:, 2].reshape(N_DEV, 2, 4, 32).transpose(1, 2, 0, 3).reshape(2, 1, D_MODEL)
    sp0 = _copies_start(groups["ffn0"], 1, _gather_spread, "spread_ffn0", deps=[small_g])

    def spread_start(nm, deps):
        return _copies_start(groups[nm], len(groups[nm]), _gather_spread, f"spread_{nm}", deps=deps)

    def spread_wait(nm, state, after):
        ssem, rsem, bufs, _ = state
        return _copies_wait(bufs, ssem, rsem, after, _blocks_moved(4), f"spread_done_{nm}")

    def relay_start(nm, bufs, deps=()):
        return _copies_start(bufs, len(bufs), _gather_relay, f"relay_{nm}", deps=deps)

    def relay_wait(nm, state, after):
        ssem, rsem, bufs, _ = state
        return _copies_wait(bufs, ssem, rsem, after, _blocks_moved(3), f"relay_done_{nm}")

    gkn = g_k[:NOPE].reshape(1, NOPE)
    gkr = _pad_cols(g_k[NOPE:].reshape(1, ROPE), LANES)
    gl = g_kv_latent.reshape(1, KV_RANK)
    lnkv = ln_kv.reshape(1, D_MODEL)

    x_in, x_mid, pooled, gates, ups, w_ffn = [], [], [], [], [], []
    qs, outs, lses = [], [], []

    def mixer(l, cur, deps):
        x_in.append(cur)
        mid, dsave = _mix_fwd(cur, ln_a_all[l], wp_all[l], bp_all[l], sc_all[l], f"mix_fwd{l}", deps=deps)
        pooled.append(dsave)
        x_mid.append(mid)
        return mid

    def q_args(j):
        return (ln_mix_b[j].reshape(1, -1), wdq_all[j], g_q_latent[j].reshape(1, -1), wuq_all[j],
                g_q[j, :NOPE].reshape(1, -1), _pad_cols(g_q[j, NOPE:].reshape(1, -1), LANES), cos, sin)

    def attention(j, cur, deps):
        x_in.append(cur)
        q = _q_fwd(cur, *q_args(j), f"q_fwd{j}", deps=deps)
        o, lse = _att_fwd(q, k_sh, v_sh, f"att_fwd{j}")
        mid = _o_fwd(cur, o, wo_all[j], f"o_fwd{j}")
        qs.append(q)
        outs.append(o)
        lses.append(lse)
        x_mid.append(mid)
        return mid

    def ffn(l, mid, relayed):
        w_l = relayed[0].reshape(3, D_FF, D_MODEL)
        w_ffn.append(w_l)
        cur, gate, up = _ffn_fwd(mid, ln_ffn[l].reshape(1, -1), w_l, f"ffn_fwd{l}")
        gates.append(gate)
        ups.append(up)
        return cur

    mid = mixer(0, xs, [sp0[3]])
    landed0 = spread_wait("ffn0", sp0, mid)
    sp1 = spread_start("ffn1", [landed0[0]])
    rl0 = relay_start("ffn0", landed0, [sp1[3]])
    cur = ffn(0, mid, relay_wait("ffn0", rl0, rl0[3]))

    landed1 = spread_wait("ffn1", sp1, cur)
    sp_att = spread_start("att", [landed1[0]])
    sp2 = spread_start("ffn2", [landed1[0]])
    rl1 = relay_start("ffn1", landed1, [sp_att[3], sp2[3]])
    mid = mixer(1, cur, [rl1[3]])
    cur = ffn(1, mid, relay_wait("ffn1", rl1, mid))
    x_kv = cur

    landed_att = spread_wait("att", sp_att, cur)
    landed2 = spread_wait("ffn2", sp2, cur)
    sp3 = spread_start("ffn3", [landed2[0]])
    rl_att = relay_start("att", landed_att, [sp3[3]])
    rl2 = relay_start("ffn2", landed2, [sp3[3]])
    att_bufs = relay_wait("att", rl_att, rl2[3])
    wc = att_bufs[0].reshape(D_MODEL, KV_RANK)
    wpe = att_bufs[1].reshape(D_MODEL, LANES)
    wuk_g = att_bufs[2].reshape(N_HEADS, KV_RANK, NOPE).transpose(1, 0, 2).reshape(KV_RANK, N_HEADS * NOPE)
    wuv_g = att_bufs[3].reshape(N_HEADS, KV_RANK, V_DIM).transpose(1, 0, 2).reshape(KV_RANK, N_HEADS * V_DIM)
    wdq_all = att_bufs[4].reshape(2, D_MODEL, Q_RANK)
    wuq_all = att_bufs[5]
    wo_all = att_bufs[6].reshape(2, D_MODEL, D_MODEL)
    k_sh, v_sh = _kv_fwd(cur, lnkv, wc, wpe, gl, wuk_g, wuv_g, gkn, gkr, cos, sin, "kv_fwd")
    mid = attention(0, cur, [])
    cur = ffn(2, mid, relay_wait("ffn2", rl2, mid))

    landed3 = spread_wait("ffn3", sp3, cur)
    rl3 = relay_start("ffn3", landed3)
    mid = attention(1, cur, [rl3[3]])
    cur = ffn(3, mid, relay_wait("ffn3", rl3, mid))

    dx, sq_cols = _loss_head(cur, target, "loss_head")

    small = {}
    sizes = dict(ffn0=FF_SHARD, ffn1=FF_SHARD, ffn2=FF_SHARD, ffn3=FF_SHARD, wo=128, kv512=128, dkv_pe=128,
                 wdq=128, wuqT=QK_PAD, wpool=32)
    big = dict(wo=lax.empty((2, D_MODEL, D_MODEL), BF16), kv512=lax.empty((3, D_MODEL, KV_RANK), BF16),
               dkv_pe=lax.empty((1, D_MODEL, LANES), BF16), wdq=lax.empty((2, D_MODEL, Q_RANK), BF16),
               wuqT=lax.empty((2, N_HEADS * QK_PAD, Q_RANK), BF16), wpool=lax.empty((8, GROUP_DIM, GROUP_DIM), BF16))
    for l in range(4):
        big[f"ffn{l}"] = lax.empty((3, D_FF, D_MODEL), BF16)
    red = {}

    def pair_start(nms, tag):
        arrs = []
        for nm in nms:
            view = _owner_view(big[nm], sizes[nm])
            arrs += [view, lax.empty((view.shape[0], N_CHIPS) + view.shape[3:], BF16)]
        return nms, tag, _copies_start(arrs, len(nms), _pair_send, f"pair_start_{tag}")

    def chip_start(state, after):
        nms, tag, (ssem, rsem, arrs, _) = state
        arrs = _copies_wait(arrs, ssem, rsem, after, _landed, f"pair_done_{tag}")
        out = []
        for t, nm in enumerate(nms):
            part = _pair_sum(arrs[2 * t], arrs[2 * t + 1], core, f"pair_sum_{nm}")
            out += [part, lax.empty((3, part.shape[0]) + part.shape[2:], BF16)]
        return nms, tag, _copies_start(out, len(nms), _chip_send, f"chip_start_{tag}")

    deferred = []
    updates = {}

    def chip_finish(state, after, defer=False):
        nms, tag, (ssem, rsem, arrs, _) = state
        arrs = _copies_wait(arrs, ssem, rsem, after, _landed, f"chip_done_{tag}")
        for t, nm in enumerate(nms):
            if defer:
                deferred.append((nm, arrs[2 * t], arrs[2 * t + 1]))
            else:
                red[nm] = _chip_sum(arrs[2 * t], arrs[2 * t + 1], chip, f"chip_sum_{nm}")

    ffn_grads = {nm: lax.empty((4, FF_SHARD, D_MODEL), F32) for nm in ("w_gate", "w_up", "w_down")}

    def place_ffn_grads(l):
        g = red[f"ffn{l}"]
        for k, nm in enumerate(("w_gate", "w_up", "w_down")):
            ffn_grads[nm] = ffn_grads[nm].at[l].set(g[k])

    dks, dvs = [], []
    pending = None
    bwd_deps = []
    for l in (3, 2, 1, 0):
        key = f"ffn{l}"
        dx, act, dgb, dub, hn, dyb, dln = _ffn_bwd(x_mid[l], dx, gates[l], ups[l], ln_ffn[l].reshape(1, -1),
                                                     w_ffn[l], f"ffn_bwd{l}", deps=bwd_deps)
        bwd_deps = []
        small[f"ln_ffn{l}"] = dln
        if l == 1:
            att_chip = chip_start(att_pair, dx)
            tn_deps = [att_chip[2][3]]
        else:
            tn_deps = []
        if pending:
            chip_finish(pending, dx, defer=True)
            pending = None
        big[key] = _tn_matmul(dgb, hn, big[key], 0, f"dw_gate{l}", m_chunk=FF_HALF, deps=tn_deps)
        big[key] = _tn_matmul(dub, hn, big[key], 1, f"dw_up{l}", m_chunk=FF_HALF)
        big[key] = _tn_matmul(act, dyb, big[key], 2, f"dw_down{l}", m_chunk=FF_HALF)
        if l == 1:
            chip_finish(att_chip, big[key], defer=True)
        ffn_pair = pair_start([key], key)
        if l >= 2:
            j = l - 2
            do, dxb, stats = _o_bwd(dx, wo_all[j], outs[j], lses[j], f"o_bwd{j}", deps=[ffn_pair[2][3]])
            big["wo"] = _tn_matmul(outs[j], dxb, big["wo"], j, f"dw_o{j}")
            ffn_chip = chip_start(ffn_pair, big["wo"])
            dq, dk, dv = _att_bwd(qs[j], k_sh, v_sh, do, stats, f"att_bwd{j}", deps=[ffn_chip[2][3]])
            chip_finish(ffn_chip, dq, defer=True)
            dks.append(dk)
            dvs.append(dv)
            dx, hnq, cqn, dqa, dcq, dln, dgql, dgqn, dgqr = _q_bwd(x_in[l], dx, dq, *q_args(j), f"q_bwd{j}")
            small[f"ln_mix_b{j}"] = dln
            small[f"g_q_latent{j}"] = dgql
            small[f"g_q{j}"] = jnp.concatenate([dgqn, dgqr[:, :ROPE]], axis=1)
            big["wdq"] = _tn_matmul(hnq, dcq, big["wdq"], j, f"dw_dq{j}")
            big["wuqT"] = _tn_matmul(dqa, cqn, big["wuqT"], j, f"dw_uq{j}")
            if l == 2:
                (dx, hnk, cn, dknb, dvb, dccb, dpeb, dlnkv, dgl, dgkn, dgkr) = _kv_bwd(
                    x_kv, dx, dks, dvs, lnkv, wc, wpe, gl, wuk_g, wuv_g, gkn, gkr, cos, sin, "kv_bwd")
                small["ln_kv"] = dlnkv
                small["g_kv_latent"] = dgl
                small["g_k"] = jnp.concatenate([dgkn, dgkr[:, :ROPE]], axis=1)
                big["kv512"] = _tn_matmul(dknb, cn, big["kv512"], 0, "dw_uk")
                big["kv512"] = _tn_matmul(dvb, cn, big["kv512"], 1, "dw_uv")
                big["kv512"] = _tn_matmul(hnk, dccb, big["kv512"], 2, "dw_dkv_c")
                big["dkv_pe"] = _tn_matmul(hnk, dpeb, big["dkv_pe"], 0, "dw_dkv_pe")
                att_pair = pair_start(["wo", "kv512", "dkv_pe", "wdq", "wuqT"], "att")
                bwd_deps = [att_pair[2][3]]
        else:
            dx, dyp, dsc, db, dln = _mix_bwd(x_in[l], dx, pooled[l], ln_a_all[l], wp_all[l], bp_all[l], sc_all[l],
                                             f"mix_bwd{l}", deps=[ffn_pair[2][3]])
            small[f"ln_mix_a{l}"] = dln
            small[f"pool_scale{l}"] = dsc
            small[f"b_pool{l}"] = db
            ffn_chip = chip_start(ffn_pair, dx)
            big["wpool"] = _tn_matmul(pooled[l], dyp, big["wpool"], 4 * l, f"dw_pool{l}", groups=4,
                                      deps=[ffn_chip[2][3]])
            if l == 1:
                pending = ffn_chip
                bwd_deps = [ffn_chip[2][3]]
            else:
                for nm, part, land in deferred:
                    red[nm] = _chip_sum(part, land, chip, f"chip_sum_{nm}", deps=[ffn_chip[2][3]])
                    if nm.startswith("ffn"):
                        place_ffn_grads(int(nm[-1]))
                early_grads = dict(
                    w_dkv=jnp.concatenate([red["kv512"][2], red["dkv_pe"][0][:, :ROPE]], axis=1),
                    w_uk=red["kv512"][0].T, w_uv=red["kv512"][1].T, w_dq=red["wdq"],
                    w_uq=red["wuqT"].transpose(0, 2, 1)[:, :, :QK_DIM], w_o=red["wo"])
                for nm, g in early_grads.items():
                    updates[nm] = _adamw(weights[nm], g, mom1[nm], mom2[nm], f"adamw_{nm}")
                chip_finish(ffn_chip, [big["wpool"]] + list(ffn_grads.values()) + [u[0] for u in updates.values()])
                place_ffn_grads(0)
    grad_x = dx[None]
    pool_pair = pair_start(["wpool"], "wpool")
    pool_chip = chip_start(pool_pair, pool_pair[2][3])
    chip_finish(pool_chip, pool_chip[2][3])

    vec_names = (["loss"] + [f"ln_ffn{l}" for l in range(4)] + ["ln_kv", "g_kv_latent", "g_k"]
                 + [f"{p}{j}" for p in ("ln_mix_b", "g_q_latent", "g_q") for j in range(2)]
                 + [f"{p}{l}" for p in ("ln_mix_a", "pool_scale", "b_pool") for l in range(2)])
    small["loss"] = sq_cols
    widths = [small[nm].shape[1] for nm in vec_names]
    padded = [-(-w // LANES) * LANES for w in widths]
    packed = jnp.concatenate([_pad_cols(small[nm], pw) for nm, pw in zip(vec_names, padded)], axis=1)
    (all_vecs,) = _all_gather([packed], [0], "gather_vectors")
    total = _sum_lead(all_vecs, "sum_vectors")
    vec = {}
    off = 0
    for nm, w, pw in zip(vec_names, widths, padded):
        vec[nm] = total[0, off:off + w]
        off += pw
    loss = 0.5 * jnp.sum(vec["loss"]) * (1.0 / D_MODEL)

    def own_cols(full, width):
        return lax.dynamic_slice_in_dim(full, dev * width, width, axis=full.ndim - 1)

    grads = dict(
        ln_mix_a=own_cols(jnp.stack([vec["ln_mix_a0"], vec["ln_mix_a1"]]), LANES),
        w_pool=red["wpool"].reshape(2, 4, 32, GROUP_DIM),
        b_pool=own_cols(jnp.stack([vec["b_pool0"], vec["b_pool1"]]).reshape(2, 4, GROUP_DIM), 32),
        pool_scale=own_cols(jnp.stack([vec["pool_scale0"], vec["pool_scale1"]]), LANES),
        ln_ffn=jnp.stack([vec[f"ln_ffn{l}"] for l in range(4)]),
        w_gate=ffn_grads["w_gate"],
        w_up=ffn_grads["w_up"],
        w_down=ffn_grads["w_down"],
        ln_kv=vec["ln_kv"],
        g_kv_latent=vec["g_kv_latent"],
        g_k=vec["g_k"],
        ln_mix_b=jnp.stack([vec["ln_mix_b0"], vec["ln_mix_b1"]]),
        g_q_latent=jnp.stack([vec["g_q_latent0"], vec["g_q_latent1"]]),
        g_q=jnp.stack([vec["g_q0"], vec["g_q1"]]),
        **early_grads,
    )

    deltas, new_m, new_v = {}, {}, {}
    for nm in names:
        w = weights[nm]
        if nm in updates:
            deltas[nm], new_m[nm], new_v[nm] = updates[nm]
            continue
        if nm in ("w_gate", "w_up"):
            def swap(a):
                return a.transpose(0, 2, 1)
            d, mo, vo = _adamw(swap(w), grads[nm], swap(mom1[nm]), swap(mom2[nm]), f"adamw_{nm}")
            deltas[nm], new_m[nm], new_v[nm], grads[nm] = swap(d), swap(mo), swap(vo), swap(grads[nm])
            continue
        shape = w.shape if w.ndim > 1 else (1, w.shape[0])
        d, mo, vo = _adamw(w.reshape(shape), grads[nm].reshape(shape), mom1[nm].reshape(shape),
                           mom2[nm].reshape(shape), f"adamw_{nm}")
        deltas[nm], new_m[nm], new_v[nm] = d.reshape(w.shape), mo.reshape(w.shape), vo.reshape(w.shape)

    return (loss, grad_x, *[grads[nm].reshape(weights[nm].shape) for nm in names], *[deltas[nm] for nm in names],
            *[new_m[nm] for nm in names], *[new_v[nm] for nm in names])
```

```python
import functools
import math

import jax
import jax.numpy as jnp
from jax import lax
from jax.experimental import pallas as pl
from jax.experimental.pallas import tpu as pltpu

F32 = jnp.float32
BF16 = jnp.bfloat16
MESH = pl.DeviceIdType.MESH

D_MODEL = 1024
D_FF = 2816
N_DEV = 8
N_CHIPS = 4
FF_SHARD = D_FF // N_DEV
FF_HALF = D_FF // 2
N_HEADS = 8
NOPE = 128
ROPE = 64
QK_DIM = NOPE + ROPE
QK_PAD = 256
V_DIM = 128
Q_RANK = 256
KV_RANK = 512
POOL_WINDOWS = (2, 4, 8, 16)
GROUP_DIM = 256
HALO = 128
CHUNK = 64
ROPE_THETA = 10000.0
EPS = 1e-6
LANES = 128

ADAM_LR = 0.001
ADAM_B1 = 0.9
ADAM_B2 = 0.999
ADAM_EPS = 1e-08
ADAM_WD = 0.01
ADAM_STEP = 10

PROJ_ROWS = 256
MIX_ROWS = 256
VMEM_BIG = 56 * 2**20
VMEM_MID = 40 * 2**20


def _nn(a, b):
    return lax.dot_general(a, b, (((1,), (0,)), ((), ())), preferred_element_type=F32)


def _nt(a, b):
    return lax.dot_general(a, b, (((1,), (1,)), ((), ())), preferred_element_type=F32)


def _tn(a, b):
    return lax.dot_general(a, b, (((0,), (0,)), ((), ())), preferred_element_type=F32)


def _rms(x, g, n):
    r = lax.rsqrt(jnp.sum(x * x, axis=-1, keepdims=True) * (1.0 / n) + EPS)
    return (x * r) * g, r


def _rms_bwd(x, r, g, dy, n):
    u = dy * g
    s = jnp.sum(x * u, axis=-1, keepdims=True) * (1.0 / n)
    dx = r * u - x * (r * r * r * s)
    dg = jnp.sum(dy * (x * r), axis=0, keepdims=True)
    return dx, dg


def _swap_perm():
    i = lax.broadcasted_iota(jnp.int32, (LANES, LANES), 0)
    j = lax.broadcasted_iota(jnp.int32, (LANES, LANES), 1)
    half = ROPE // 2
    hit = ((j < half) & (i == j + half)) | ((j >= half) & (j < ROPE) & (i == j - half))
    return jnp.where(hit, 1.0, 0.0).astype(BF16)


def _swap_halves(z, perm):
    hi = z.astype(BF16)
    lo = (z - hi.astype(F32)).astype(BF16)
    return _nn(hi, perm) + _nn(lo, perm)


def _sigmoid(x):
    return 1.0 / (1.0 + jnp.exp(-x))


def _cparams(n_grid, vmem=None):
    return pltpu.CompilerParams(dimension_semantics=("arbitrary",) * n_grid, vmem_limit_bytes=vmem)


def _rows(t, cols):
    return pl.BlockSpec((t, cols), lambda i: (i, 0))


def _full(shape):
    nd = len(shape)
    return pl.BlockSpec(shape, lambda *_: (0,) * nd)


ANY = pl.BlockSpec(memory_space=pl.ANY)


def _pcall(body, args, deps, *, in_specs, **kw):
    n_in, n_dep = len(args), len(deps)

    def ordered(*refs):
        body(*refs[:n_in], *refs[n_in + n_dep:])

    return pl.pallas_call(ordered, in_specs=list(in_specs) + [ANY] * n_dep, **kw)(*args, *deps)


def _place():
    x, y, c = lax.axis_index("x"), lax.axis_index("y"), lax.axis_index("c")
    return x, y, c


def _all_gather(shards, axes, name, deps=()):
    n, nd = len(shards), len(deps)
    out_shape = [jax.ShapeDtypeStruct(s.shape[:a] + (N_DEV,) + s.shape[a:], s.dtype) for s, a in zip(shards, axes)]

    def body(*refs):
        ins, outs = refs[:n], refs[n + nd:2 * n + nd]
        send_sems, recv_sems, local_sems = refs[2 * n + nd:]
        x, y, c = _place()
        me, sibling = (x, y, c), (x, y, 1 - c)
        chips = [(1 - x, y), (x, 1 - y), (1 - x, 1 - y)]

        def slot(t, dev):
            idx = 4 * dev[0] + 2 * dev[1] + dev[2]
            return outs[t].at[(slice(None),) * axes[t] + (idx,)]

        def copy(t, k, block, to, src=None):
            return pltpu.make_async_remote_copy(
                src_ref=slot(t, block) if src is None else src, dst_ref=slot(t, block),
                send_sem=send_sems.at[t, k], recv_sem=recv_sems.at[t, k],
                device_id=to, device_id_type=MESH)

        mine = [pltpu.make_async_copy(ins[t], slot(t, me), local_sems.at[t]) for t in range(n)]
        for cp in mine:
            cp.start()
        first = []
        for t in range(n):
            first.append(copy(t, 0, me, sibling, src=ins[t]))
            first += [copy(t, 1 + j, me, (*chip, c), src=ins[t]) for j, chip in enumerate(chips)]
        for cp in first:
            cp.start()
        passed = []
        for j, chip in enumerate(chips):
            for t in range(n):
                copy(t, 1 + j, (*chip, c), me).wait_recv()
                cp = copy(t, 4 + j, (*chip, c), sibling)
                cp.start()
                passed.append(cp)
        for t in range(n):
            copy(t, 0, sibling, me).wait_recv()
            for j, chip in enumerate(chips):
                copy(t, 4 + j, (*chip, 1 - c), me).wait_recv()
        for cp in first + passed:
            cp.wait_send()
        for cp in mine:
            cp.wait()

    return pl.pallas_call(
        body, name=name, out_shape=out_shape,
        in_specs=[ANY] * (n + nd), out_specs=[ANY] * n,
        scratch_shapes=[pltpu.SemaphoreType.DMA((n, 7)), pltpu.SemaphoreType.DMA((n, 7)),
                        pltpu.SemaphoreType.DMA((n,))],
    )(*shards, *deps)


HBM = pl.BlockSpec(memory_space=pltpu.HBM)
SEM = pl.BlockSpec(memory_space=pltpu.SEMAPHORE)
EFFECT = pltpu.SideEffectType.DATAFLOW_SIDE_EFFECTING


def _copies_start(arrays, n_sems, plan, name, deps=()):
    n, nd = len(arrays), len(deps)

    def body(*refs):
        for cp in plan(refs[:n], refs[n + nd], refs[n + nd + 1]):
            cp.start()
        refs[-1][...] = jnp.zeros_like(refs[-1])

    outs = pl.pallas_call(
        body, name=name,
        out_shape=(pltpu.SemaphoreType.DMA((n_sems,)), pltpu.SemaphoreType.DMA((n_sems,)),
                   *[pltpu.HBM(a.shape, a.dtype) for a in arrays], jax.ShapeDtypeStruct((8, LANES), F32)),
        in_specs=[HBM] * n + [ANY] * nd,
        out_specs=(SEM, SEM, *[HBM] * n, pl.BlockSpec(memory_space=pltpu.VMEM)),
        input_output_aliases={i: 2 + i for i in range(n)},
        compiler_params=pltpu.CompilerParams(has_side_effects=EFFECT),
    )(*[pltpu.with_memory_space_constraint(a, pltpu.HBM) for a in arrays], *deps)
    return outs[0], outs[1], list(outs[2:2 + n]), outs[-1]


def _copies_wait(arrays, send_sems, recv_sems, after, plan, name):
    n = len(arrays)
    after = list(after) if isinstance(after, (list, tuple)) else [after]

    def body(*refs):
        for cp in plan(refs[:n], refs[n], refs[n + 1]):
            cp.wait_send()
            cp.wait_recv()

    outs = pl.pallas_call(
        body, name=name,
        out_shape=tuple(pltpu.HBM(a.shape, a.dtype) for a in arrays),
        in_specs=[HBM] * n + [SEM, SEM] + [ANY] * len(after), out_specs=tuple([HBM] * n),
        input_output_aliases={i: i for i in range(n)},
        compiler_params=pltpu.CompilerParams(has_side_effects=EFFECT),
    )(*arrays, send_sems, recv_sems, *after)
    return list(outs)


def _remote(src, dst, send_sems, recv_sems, t, to):
    return pltpu.make_async_remote_copy(src_ref=src, dst_ref=dst, send_sem=send_sems.at[t], recv_sem=recv_sems.at[t],
                                        device_id=to, device_id_type=MESH)


def _dev_index(x, y, c):
    return 4 * x + 2 * y + c


def _gather_spread(bufs, send_sems, recv_sems):
    x, y, c = _place()
    mine = _dev_index(x, y, c)
    peers = [(x, y, 1 - c), (1 - x, y, c), (x, 1 - y, c), (1 - x, 1 - y, c)]
    return [_remote(g.at[k, mine], g.at[k, mine], send_sems, recv_sems, t, peer)
            for t, g in enumerate(bufs) for peer in peers for k in range(g.shape[0])]


def _gather_relay(bufs, send_sems, recv_sems):
    x, y, c = _place()
    blocks = [_dev_index(1 - x, y, c), _dev_index(x, 1 - y, c), _dev_index(1 - x, 1 - y, c)]
    return [_remote(g.at[k, b], g.at[k, b], send_sems, recv_sems, t, (x, y, 1 - c))
            for t, g in enumerate(bufs) for b in blocks for k in range(g.shape[0])]


def _blocks_moved(count):
    def plan(bufs, send_sems, recv_sems):
        x, y, c = _place()
        return [_remote(g.at[:, pl.ds(0, count)], g.at[:, pl.ds(0, count)], send_sems, recv_sems, t, (x, y, 1 - c))
                for t, g in enumerate(bufs)]
    return plan


def _pair_send(arrs, send_sems, recv_sems):
    x, y, c = _place()
    return [_remote(arrs[2 * t].at[p, k, 1 - c], arrs[2 * t + 1].at[p, k], send_sems, recv_sems, t, (x, y, 1 - c))
            for t in range(len(arrs) // 2) for p in range(arrs[2 * t].shape[0]) for k in range(N_CHIPS)]


def _chip_send(arrs, send_sems, recv_sems):
    x, y, c = _place()
    chips = [(1 - x, y), (x, 1 - y), (1 - x, 1 - y)]
    return [_remote(arrs[2 * t].at[p, 2 * px + py], arrs[2 * t + 1].at[j, p], send_sems, recv_sems, t, (px, py, c))
            for t in range(len(arrs) // 2) for j, (px, py) in enumerate(chips) for p in range(arrs[2 * t].shape[0])]


def _landed(arrs, send_sems, recv_sems):
    x, y, c = _place()
    return [_remote(arrs[2 * t + 1], arrs[2 * t + 1], send_sems, recv_sems, t, (x, y, 1 - c))
            for t in range(len(arrs) // 2)]


def _rows_per_step(rows, row_elems):
    best = 1
    for cand in range(1, rows + 1):
        if rows % cand == 0 and cand * row_elems <= 256 * 1024:
            best = cand
    return best


def _pair_sum(grad, landed, core, name):
    p, _, _, sz, c = grad.shape
    r = _rows_per_step(p * N_CHIPS, sz * c)

    def body(core_ref, g_ref, l_ref, o_ref):
        o_ref[...] = (g_ref[...].astype(F32) + l_ref[...].astype(F32)).astype(o_ref.dtype)

    out = pl.pallas_call(
        body, name=name,
        grid_spec=pltpu.PrefetchScalarGridSpec(
            num_scalar_prefetch=1, grid=(p * N_CHIPS // r,),
            in_specs=[pl.BlockSpec((r, None, sz, c), lambda i, cr: (i, cr[0], 0, 0)),
                      pl.BlockSpec((r, sz, c), lambda i, cr: (i, 0, 0))],
            out_specs=pl.BlockSpec((r, sz, c), lambda i, cr: (i, 0, 0))),
        out_shape=jax.ShapeDtypeStruct((p * N_CHIPS, sz, c), grad.dtype),
        compiler_params=_cparams(1),
    )(core, grad.reshape(p * N_CHIPS, 2, sz, c), landed.reshape(p * N_CHIPS, sz, c))
    return out.reshape(p, N_CHIPS, sz, c)


def _chip_sum(parts, landed, chip, name, deps=()):
    p, _, sz, c = parts.shape
    r = _rows_per_step(p, sz * c)

    def body(chip_ref, a_ref, l_ref, o_ref):
        acc = a_ref[...].astype(F32)
        for j in range(3):
            acc = acc + l_ref[j].astype(F32)
        o_ref[...] = acc

    nd = len(deps)

    def ordered(chip_ref, a_ref, l_ref, *rest):
        body(chip_ref, a_ref, l_ref, rest[nd])

    return pl.pallas_call(
        ordered, name=name,
        grid_spec=pltpu.PrefetchScalarGridSpec(
            num_scalar_prefetch=1, grid=(p // r,),
            in_specs=[pl.BlockSpec((r, None, sz, c), lambda i, cr: (i, cr[0], 0, 0)),
                      pl.BlockSpec((3, r, sz, c), lambda i, cr: (0, i, 0, 0))] + [ANY] * nd,
            out_specs=pl.BlockSpec((r, sz, c), lambda i, cr: (i, 0, 0))),
        out_shape=jax.ShapeDtypeStruct((p, sz, c), F32),
        compiler_params=_cparams(1),
    )(chip, parts, landed, *deps)


def _sum_lead(a, name, out_dtype=F32):
    k = a.shape[0]
    rest = a.shape[1:]
    r, c = rest[-2], rest[-1]
    lead = math.prod(rest[:-2])
    a3 = a.reshape(k, lead * r, c)
    rows = lead * r
    tb = rows
    for cand in (512, 256, 128, 64, 32, 16, 8):
        if rows % cand == 0 and rows > cand:
            tb = cand
            break

    def body(a_ref, o_ref):
        acc = a_ref[0].astype(F32)
        for i in range(1, k):
            acc = acc + a_ref[i].astype(F32)
        o_ref[...] = acc.astype(out_dtype)

    out = pl.pallas_call(
        body, name=name, grid=(rows // tb,),
        out_shape=jax.ShapeDtypeStruct((rows, c), out_dtype),
        in_specs=[pl.BlockSpec((k, tb, c), lambda i: (0, i, 0))],
        out_specs=pl.BlockSpec((tb, c), lambda i: (i, 0)),
        compiler_params=_cparams(1),
    )(a3)
    return out.reshape(rest)


def _bands(t, causal):
    r = lax.broadcasted_iota(jnp.int32, (t, t + HALO), 0)
    col = lax.broadcasted_iota(jnp.int32, (t, t + HALO), 1)
    diff = r + HALO - col if causal else col - r
    return jnp.stack([jnp.where((diff >= 0) & (diff < w), 1.0, 0.0) for w in POOL_WINDOWS]).astype(BF16)


def _split_dot(band, v):
    hi = v.astype(BF16)
    lo = (v - hi.astype(F32)).astype(BF16)
    return _nn(band, hi) + _nn(band, lo)


def _mix_fwd(x, g, wp, b, sc, name, deps=()):
    s = x.shape[0]
    t = min(MIX_ROWS, s)
    rb = t // HALO

    def body(x_ref, xh_ref, g_ref, wp_ref, b_ref, sc_ref, band_ref, xo_ref, d_ref):
        i = pl.program_id(0)
        gg = g_ref[...]
        h, _ = _rms(x_ref[...], gg, D_MODEL)
        hh, _ = _rms(xh_ref[...], gg, D_MODEL)
        hh = jnp.where(i > 0, hh, 0.0)
        hext = jnp.concatenate([hh, h], axis=0)
        tok = i * t + lax.broadcasted_iota(jnp.int32, (t, 1), 0)
        for gi, w in enumerate(POOL_WINDOWS):
            sl = slice(gi * GROUP_DIM, (gi + 1) * GROUP_DIM)
            win = _split_dot(band_ref[gi], hext[:, sl])
            inv = 1.0 / jnp.minimum(tok + 1, w).astype(F32)
            dbf = (win * inv - h[:, sl]).astype(BF16)
            d_ref[:, sl] = dbf
            ypre = _nn(dbf, wp_ref[gi]) + b_ref[:, sl]
            xo_ref[:, sl] = x_ref[:, sl] + ypre * sc_ref[:, sl]

    return _pcall(
        body, (x, x, g, wp, b, sc, _bands(t, True)), deps, name=name, grid=(s // t,),
        out_shape=[jax.ShapeDtypeStruct((s, D_MODEL), F32), jax.ShapeDtypeStruct((s, D_MODEL), BF16)],
        in_specs=[_rows(t, D_MODEL),
                  pl.BlockSpec((HALO, D_MODEL), lambda i: (jnp.maximum(i * rb - 1, 0), 0)),
                  _full((1, D_MODEL)), _full((4, GROUP_DIM, GROUP_DIM)), _full((1, D_MODEL)), _full((1, D_MODEL)),
                  _full((4, t, t + HALO))],
        out_specs=[_rows(t, D_MODEL), _rows(t, D_MODEL)],
        compiler_params=_cparams(1, VMEM_MID),
    )


def _mix_bwd(x, dy, d, g, wp, b, sc, name, deps=()):
    s = x.shape[0]
    t = min(MIX_ROWS, s)
    rb = t // HALO
    nb = s // t
    last_halo = s // HALO - 1

    def body(x_ref, dy_ref, dyn_ref, d_ref, g_ref, wp_ref, b_ref, sc_ref, band_ref,
             dx_ref, dyp_ref, dsc_ref, db_ref, dln_ref):
        i = pl.program_id(0)
        x = x_ref[...]
        gg = g_ref[...]
        dy = dy_ref[...]
        sc = sc_ref[...]
        dyp32 = dy * sc
        dyp = dyp32.astype(BF16)
        dyph = (dyn_ref[...] * sc).astype(BF16)
        dyp_ref[...] = dyp
        tok = i * t + lax.broadcasted_iota(jnp.int32, (t + HALO, 1), 0)
        dh, dsc = [], []
        for gi, w in enumerate(POOL_WINDOWS):
            sl = slice(gi * GROUP_DIM, (gi + 1) * GROUP_DIM)
            ypre = _nn(d_ref[:, sl], wp_ref[gi]) + b_ref[:, sl]
            dsc.append(jnp.sum(dy[:, sl] * ypre, axis=0, keepdims=True))
            dd = _nt(dyp[:, sl], wp_ref[gi])
            ddh = jnp.where(i < nb - 1, _nt(dyph[:, sl], wp_ref[gi]), 0.0)
            inv = 1.0 / jnp.minimum(tok + 1, w).astype(F32)
            ddext = jnp.concatenate([dd, ddh], axis=0) * inv
            dh.append(_split_dot(band_ref[gi], ddext) - dd)
        dh = jnp.concatenate(dh, axis=1)
        _, r = _rms(x, gg, D_MODEL)
        dxn, dg = _rms_bwd(x, r, gg, dh, D_MODEL)
        dx_ref[...] = dy + dxn

        @pl.when(i == 0)
        def _():
            dsc_ref[...] = jnp.zeros_like(dsc_ref)
            db_ref[...] = jnp.zeros_like(db_ref)
            dln_ref[...] = jnp.zeros_like(dln_ref)

        dsc_ref[...] += jnp.concatenate(dsc, axis=1)
        db_ref[...] += jnp.sum(dyp32, axis=0, keepdims=True)
        dln_ref[...] += dg

    vec = jax.ShapeDtypeStruct((1, D_MODEL), F32)
    return _pcall(
        body, (x, dy, dy, d, g, wp, b, sc, _bands(t, False)), deps, name=name, grid=(nb,),
        out_shape=[jax.ShapeDtypeStruct((s, D_MODEL), F32), jax.ShapeDtypeStruct((s, D_MODEL), BF16), vec, vec, vec],
        in_specs=[_rows(t, D_MODEL), _rows(t, D_MODEL),
                  pl.BlockSpec((HALO, D_MODEL), lambda i: (jnp.minimum((i + 1) * rb, last_halo), 0)),
                  _rows(t, D_MODEL),
                  _full((1, D_MODEL)), _full((4, GROUP_DIM, GROUP_DIM)), _full((1, D_MODEL)), _full((1, D_MODEL)),
                  _full((4, t, t + HALO))],
        out_specs=[_rows(t, D_MODEL), _rows(t, D_MODEL), _full((1, D_MODEL)), _full((1, D_MODEL)), _full((1, D_MODEL))],
        compiler_params=_cparams(1, VMEM_MID),
    )


def _load_weights(w_hbm, w_vmem, sem):
    @pl.when(pl.program_id(0) == 0)
    def _():
        cp = pltpu.make_async_copy(w_hbm, w_vmem, sem)
        cp.start()
        cp.wait()


def _ffn_fwd(x, g, w, name):
    s = x.shape[0]
    t = min(512, s)

    def body(x_ref, g_ref, w_hbm, xo_ref, gate_ref, up_ref, w_ref, sem):
        _load_weights(w_hbm, w_ref, sem)
        x = x_ref[...]
        hn = _rms(x, g_ref[...], D_MODEL)[0].astype(BF16)
        acc = x
        for c in range(2):
            rs = slice(c * FF_HALF, (c + 1) * FF_HALF)
            gt = _nt(hn, w_ref[0, rs, :])
            up = _nt(hn, w_ref[1, rs, :])
            gate_ref[:, rs] = gt.astype(BF16)
            up_ref[:, rs] = up.astype(BF16)
            act = ((gt * _sigmoid(gt)) * up).astype(BF16)
            acc = acc + _nn(act, w_ref[2, rs, :])
        xo_ref[...] = acc

    hid = jax.ShapeDtypeStruct((s, D_FF), BF16)
    return pl.pallas_call(
        body, name=name, grid=(s // t,),
        out_shape=[jax.ShapeDtypeStruct((s, D_MODEL), F32), hid, hid],
        in_specs=[_rows(t, D_MODEL), _full((1, D_MODEL)), ANY],
        out_specs=[_rows(t, D_MODEL), _rows(t, D_FF), _rows(t, D_FF)],
        scratch_shapes=[pltpu.VMEM((3, D_FF, D_MODEL), BF16), pltpu.SemaphoreType.DMA],
        compiler_params=_cparams(1, VMEM_BIG),
    )(x, g, w)


def _ffn_bwd(x, dy, gate, up, g, w, name, deps=()):
    s = x.shape[0]
    t = min(256, s)

    def body(x_ref, dy_ref, gate_ref, up_ref, g_ref, w_hbm,
             dx_ref, act_ref, dg_ref, du_ref, hn_ref, dyb_ref, dln_ref, w_ref, sem):
        _load_weights(w_hbm, w_ref, sem)
        x = x_ref[...]
        gg = g_ref[...]
        y, r = _rms(x, gg, D_MODEL)
        hn = y.astype(BF16)
        hn_ref[...] = hn
        dy = dy_ref[...]
        dyb = dy.astype(BF16)
        dyb_ref[...] = dyb
        dh = jnp.zeros((t, D_MODEL), F32)
        for c in range(2):
            rs = slice(c * FF_HALF, (c + 1) * FF_HALF)
            gt = gate_ref[:, rs].astype(F32)
            u = up_ref[:, rs].astype(F32)
            sg = _sigmoid(gt)
            sl = gt * sg
            act_ref[:, rs] = (sl * u).astype(BF16)
            dact = _nt(dyb, w_ref[2, rs, :])
            dg = (dact * u * (sg * (1.0 + gt * (1.0 - sg)))).astype(BF16)
            du = (dact * sl).astype(BF16)
            dg_ref[:, rs] = dg
            du_ref[:, rs] = du
            dh = dh + _nn(dg, w_ref[0, rs, :]) + _nn(du, w_ref[1, rs, :])
        dxn, dgl = _rms_bwd(x, r, gg, dh, D_MODEL)
        dx_ref[...] = dy + dxn

        @pl.when(pl.program_id(0) == 0)
        def _():
            dln_ref[...] = jnp.zeros_like(dln_ref)

        dln_ref[...] += dgl

    hid = jax.ShapeDtypeStruct((s, D_FF), BF16)
    tok = jax.ShapeDtypeStruct((s, D_MODEL), BF16)
    return _pcall(
        body, (x, dy, gate, up, g, w), deps, name=name, grid=(s // t,),
        out_shape=[jax.ShapeDtypeStruct((s, D_MODEL), F32), hid, hid, hid, tok, tok,
                   jax.ShapeDtypeStruct((1, D_MODEL), F32)],
        in_specs=[_rows(t, D_MODEL), _rows(t, D_MODEL), _rows(t, D_FF), _rows(t, D_FF), _full((1, D_MODEL)), ANY],
        out_specs=[_rows(t, D_MODEL), _rows(t, D_FF), _rows(t, D_FF), _rows(t, D_FF),
                   _rows(t, D_MODEL), _rows(t, D_MODEL), _full((1, D_MODEL))],
        scratch_shapes=[pltpu.VMEM((3, D_FF, D_MODEL), BF16), pltpu.SemaphoreType.DMA],
        compiler_params=_cparams(1, VMEM_BIG),
    )


def _tn_matmul(a, b, into, p0, name, groups=1, m_chunk=None, deps=()):
    s = a.shape[0]
    m, n = a.shape[1] // groups, b.shape[1] // groups
    assert into.shape[1:] == (m, n)
    mc = m if m_chunk is None else m_chunk
    nm = m // mc
    t = min(1024, s)
    nt = s // t

    def body(a_ref, b_ref, into_ref, o_ref, acc):
        ti = pl.program_id(2)

        @pl.when(ti == 0)
        def _():
            acc[...] = jnp.zeros_like(acc)

        acc[...] += _tn(a_ref[...], b_ref[...])

        @pl.when(ti == nt - 1)
        def _():
            o_ref[...] = acc[...].astype(o_ref.dtype)

    return _pcall(
        body, (a, b, into), deps, name=name, grid=(groups, nm, nt),
        out_shape=jax.ShapeDtypeStruct(into.shape, into.dtype),
        in_specs=[pl.BlockSpec((t, mc), lambda gi, mi, ti: (ti, gi * nm + mi)),
                  pl.BlockSpec((t, n), lambda gi, mi, ti: (ti, gi)), ANY],
        out_specs=pl.BlockSpec((None, mc, n), lambda gi, mi, ti: (p0 + gi, mi, 0)),
        scratch_shapes=[pltpu.VMEM((mc, n), F32)],
        input_output_aliases={2: 0},
        compiler_params=_cparams(3, VMEM_BIG),
    )


def _rope_tables(positions):
    half = ROPE // 2
    inv = ROPE_THETA ** (-jnp.arange(half, dtype=F32) * 2.0 / ROPE)
    ang = positions.astype(F32)[:, None] * inv
    cos, sin = jnp.cos(ang), jnp.sin(ang)
    zero = jnp.zeros((positions.shape[0], LANES - ROPE), F32)
    return jnp.concatenate([cos, cos, zero], axis=1), jnp.concatenate([-sin, sin, zero], axis=1)


def _kv_specs(t):
    return [_full((1, D_MODEL)), _full((D_MODEL, KV_RANK)), _full((D_MODEL, LANES)), _full((1, KV_RANK)),
            _full((KV_RANK, N_HEADS * NOPE)), _full((KV_RANK, N_HEADS * V_DIM)),
            _full((1, NOPE)), _full((1, LANES)), _rows(t, LANES), _rows(t, LANES)]


def _kv_fwd(x, ln, wc, wpe, gl, wuk, wuv, gkn, gkr, cos, sin, name, deps=()):
    s = x.shape[0]
    t = min(PROJ_ROWS, s)

    def body(x_ref, ln_ref, wc_ref, wpe_ref, gl_ref, wuk_ref, wuv_ref, gkn_ref, gkr_ref, cos_ref, sin_ref,
             k_ref, v_ref):
        hn = _rms(x_ref[...], ln_ref[...], D_MODEL)[0].astype(BF16)
        clat = _nn(hn, wc_ref[...])
        kpe = _nn(hn, wpe_ref[...])
        cn = _rms(clat, gl_ref[...], KV_RANK)[0].astype(BF16)
        sspe = jnp.sum(kpe * kpe, axis=-1, keepdims=True)
        base = kpe * gkr_ref[...]
        rot = base * cos_ref[...] + _swap_halves(base, _swap_perm()) * sin_ref[...]
        kn_all = _nn(cn, wuk_ref[...])
        v_ref[...] = _nn(cn, wuv_ref[...]).astype(BF16)
        for h in range(N_HEADS):
            kn = kn_all[:, h * NOPE:(h + 1) * NOPE]
            r = lax.rsqrt((jnp.sum(kn * kn, axis=-1, keepdims=True) + sspe) * (1.0 / QK_DIM) + EPS)
            k_ref[:, h * QK_PAD:h * QK_PAD + NOPE] = ((kn * r) * gkn_ref[...]).astype(BF16)
            k_ref[:, h * QK_PAD + NOPE:(h + 1) * QK_PAD] = (rot * r).astype(BF16)

    return _pcall(
        body, (x, ln, wc, wpe, gl, wuk, wuv, gkn, gkr, cos, sin), deps, name=name, grid=(s // t,),
        out_shape=[jax.ShapeDtypeStruct((s, N_HEADS * QK_PAD), BF16), jax.ShapeDtypeStruct((s, N_HEADS * V_DIM), BF16)],
        in_specs=[_rows(t, D_MODEL)] + _kv_specs(t),
        out_specs=[_rows(t, N_HEADS * QK_PAD), _rows(t, N_HEADS * V_DIM)],
        compiler_params=_cparams(1, VMEM_MID),
    )


def _kv_bwd(x, dxin, dks, dvs, ln, wc, wpe, gl, wuk, wuv, gkn, gkr, cos, sin, name):
    s = x.shape[0]
    t = min(PROJ_ROWS, s)
    nk = len(dks)

    def body(*refs):
        x_ref, dxin_ref = refs[:2]
        dk_refs = refs[2:2 + nk]
        dv_refs = refs[2 + nk:2 + 2 * nk]
        (ln_ref, wc_ref, wpe_ref, gl_ref, wuk_ref, wuv_ref, gkn_ref, gkr_ref, cos_ref, sin_ref,
         dx_ref, hn_ref, cn_ref, dkn_ref, dvb_ref, dcc_ref, dpe_ref,
         dln_ref, dgl_ref, dgkn_ref, dgkr_ref) = refs[2 + 2 * nk:]
        x = x_ref[...]
        ln = ln_ref[...]
        y, rx = _rms(x, ln, D_MODEL)
        hn = y.astype(BF16)
        hn_ref[...] = hn
        clat = _nn(hn, wc_ref[...])
        kpe = _nn(hn, wpe_ref[...])
        gl = gl_ref[...]
        cy, rc = _rms(clat, gl, KV_RANK)
        cn = cy.astype(BF16)
        cn_ref[...] = cn
        sspe = jnp.sum(kpe * kpe, axis=-1, keepdims=True)
        cs, sn, perm = cos_ref[...], sin_ref[...], _swap_perm()
        gkn, gkr = gkn_ref[...], gkr_ref[...]
        base = kpe * gkr
        rot = base * cs + _swap_halves(base, perm) * sn
        dkr_sum = jnp.zeros((t, LANES), F32)
        coef_sum = jnp.zeros((t, 1), F32)
        dgkn = jnp.zeros((1, NOPE), F32)
        kn_all = _nn(cn, wuk_ref[...])
        dkn_heads = []
        for h in range(N_HEADS):
            kn = kn_all[:, h * NOPE:(h + 1) * NOPE]
            r = lax.rsqrt((jnp.sum(kn * kn, axis=-1, keepdims=True) + sspe) * (1.0 / QK_DIM) + EPS)
            lo, mid, hi = h * QK_PAD, h * QK_PAD + NOPE, (h + 1) * QK_PAD
            dko = dk_refs[0][:, lo:mid]
            dkr = dk_refs[0][:, mid:hi]
            for j in range(1, nk):
                dko = dko + dk_refs[j][:, lo:mid]
                dkr = dkr + dk_refs[j][:, mid:hi]
            un = dko * gkn
            sm = (jnp.sum(kn * un, axis=-1, keepdims=True) + jnp.sum(rot * dkr, axis=-1, keepdims=True)) * (1.0 / QK_DIM)
            coef = r * r * r * sm
            dkn = (r * un - kn * coef).astype(BF16)
            dkr_sum = dkr_sum + r * dkr
            coef_sum = coef_sum + coef
            dgkn = dgkn + jnp.sum(dko * (kn * r), axis=0, keepdims=True)
            dkn_heads.append(dkn)
        dkn_all = jnp.concatenate(dkn_heads, axis=1)
        dkn_ref[...] = dkn_all
        dv_all = dv_refs[0][...]
        for j in range(1, nk):
            dv_all = dv_all + dv_refs[j][...]
        dvb = dv_all.astype(BF16)
        dvb_ref[...] = dvb
        dc = _nt(dkn_all, wuk_ref[...]) + _nt(dvb, wuv_ref[...])
        dz = dkr_sum * cs - _swap_halves(dkr_sum, perm) * sn
        dkpe = dz * gkr - kpe * coef_sum
        dgkr = jnp.sum(dz * kpe, axis=0, keepdims=True)
        dclat, dgl = _rms_bwd(clat, rc, gl, dc, KV_RANK)
        dcc = dclat.astype(BF16)
        dpe = dkpe.astype(BF16)
        dcc_ref[...] = dcc
        dpe_ref[...] = dpe
        dhn = _nt(dcc, wc_ref[...]) + _nt(dpe, wpe_ref[...])
        dxn, dln = _rms_bwd(x, rx, ln, dhn, D_MODEL)
        dx_ref[...] = dxin_ref[...] + dxn

        @pl.when(pl.program_id(0) == 0)
        def _():
            dln_ref[...] = jnp.zeros_like(dln_ref)
            dgl_ref[...] = jnp.zeros_like(dgl_ref)
            dgkn_ref[...] = jnp.zeros_like(dgkn_ref)
            dgkr_ref[...] = jnp.zeros_like(dgkr_ref)

        dln_ref[...] += dln
        dgl_ref[...] += dgl
        dgkn_ref[...] += dgkn
        dgkr_ref[...] += dgkr

    def tok(cols, dt):
        return jax.ShapeDtypeStruct((s, cols), dt)

    def vec(cols):
        return jax.ShapeDtypeStruct((1, cols), F32)

    return pl.pallas_call(
        body, name=name, grid=(s // t,),
        out_shape=[tok(D_MODEL, F32), tok(D_MODEL, BF16), tok(KV_RANK, BF16), tok(N_HEADS * NOPE, BF16),
                   tok(N_HEADS * V_DIM, BF16), tok(KV_RANK, BF16), tok(LANES, BF16),
                   vec(D_MODEL), vec(KV_RANK), vec(NOPE), vec(LANES)],
        in_specs=[_rows(t, D_MODEL), _rows(t, D_MODEL)] + [_rows(t, N_HEADS * QK_PAD)] * nk
                 + [_rows(t, N_HEADS * V_DIM)] * nk + _kv_specs(t),
        out_specs=[_rows(t, D_MODEL), _rows(t, D_MODEL), _rows(t, KV_RANK), _rows(t, N_HEADS * NOPE),
                   _rows(t, N_HEADS * V_DIM), _rows(t, KV_RANK), _rows(t, LANES),
                   _full((1, D_MODEL)), _full((1, KV_RANK)), _full((1, NOPE)), _full((1, LANES))],
        compiler_params=_cparams(1, VMEM_BIG),
    )(x, dxin, *dks, *dvs, ln, wc, wpe, gl, wuk, wuv, gkn, gkr, cos, sin)


def _q_specs(t):
    return [_full((1, D_MODEL)), _full((D_MODEL, Q_RANK)), _full((1, Q_RANK)), _full((N_HEADS, Q_RANK, QK_PAD)),
            _full((1, NOPE)), _full((1, LANES)), _rows(t, LANES), _rows(t, LANES)]


def _q_fwd(x, ln, wdq, gql, wuq, gqn, gqr, cos, sin, name, deps=()):
    s = x.shape[0]
    t = min(PROJ_ROWS, s)

    def body(x_ref, ln_ref, wdq_ref, gql_ref, wuq_ref, gqn_ref, gqr_ref, cos_ref, sin_ref, q_ref):
        hn = _rms(x_ref[...], ln_ref[...], D_MODEL)[0].astype(BF16)
        cqn = _rms(_nn(hn, wdq_ref[...]), gql_ref[...], Q_RANK)[0].astype(BF16)
        cs, sn, perm = cos_ref[...], sin_ref[...], _swap_perm()
        for h in range(N_HEADS):
            qa = _nn(cqn, wuq_ref[h])
            r = lax.rsqrt(jnp.sum(qa * qa, axis=-1, keepdims=True) * (1.0 / QK_DIM) + EPS)
            q_ref[:, h * QK_PAD:h * QK_PAD + NOPE] = ((qa[:, :NOPE] * r) * gqn_ref[...]).astype(BF16)
            z = (qa[:, NOPE:] * r) * gqr_ref[...]
            q_ref[:, h * QK_PAD + NOPE:(h + 1) * QK_PAD] = (z * cs + _swap_halves(z, perm) * sn).astype(BF16)

    return _pcall(
        body, (x, ln, wdq, gql, wuq, gqn, gqr, cos, sin), deps, name=name, grid=(s // t,),
        out_shape=jax.ShapeDtypeStruct((s, N_HEADS * QK_PAD), BF16),
        in_specs=[_rows(t, D_MODEL)] + _q_specs(t),
        out_specs=_rows(t, N_HEADS * QK_PAD),
        compiler_params=_cparams(1, VMEM_MID),
    )


def _q_bwd(x, dxin, dq, ln, wdq, gql, wuq, gqn, gqr, cos, sin, name):
    s = x.shape[0]
    t = min(PROJ_ROWS, s)

    def body(x_ref, dxin_ref, dq_ref, ln_ref, wdq_ref, gql_ref, wuq_ref, gqn_ref, gqr_ref, cos_ref, sin_ref,
             dx_ref, hn_ref, cqn_ref, dqa_ref, dcq_ref, dln_ref, dgql_ref, dgqn_ref, dgqr_ref):
        x = x_ref[...]
        ln = ln_ref[...]
        y, rx = _rms(x, ln, D_MODEL)
        hn = y.astype(BF16)
        hn_ref[...] = hn
        cqp = _nn(hn, wdq_ref[...])
        gql = gql_ref[...]
        cy, rc = _rms(cqp, gql, Q_RANK)
        cqn = cy.astype(BF16)
        cqn_ref[...] = cqn
        cs, sn, perm = cos_ref[...], sin_ref[...], _swap_perm()
        gqn, gqr = gqn_ref[...], gqr_ref[...]
        dcq = jnp.zeros((t, Q_RANK), F32)
        dgqn = jnp.zeros((1, NOPE), F32)
        dgqr = jnp.zeros((1, LANES), F32)
        for h in range(N_HEADS):
            qa = _nn(cqn, wuq_ref[h])
            qn, qr = qa[:, :NOPE], qa[:, NOPE:]
            r = lax.rsqrt(jnp.sum(qa * qa, axis=-1, keepdims=True) * (1.0 / QK_DIM) + EPS)
            dqo = dq_ref[:, h * QK_PAD:h * QK_PAD + NOPE]
            dqr = dq_ref[:, h * QK_PAD + NOPE:(h + 1) * QK_PAD]
            dz = dqr * cs - _swap_halves(dqr, perm) * sn
            un = dqo * gqn
            ur = dz * gqr
            sm = (jnp.sum(qn * un, axis=-1, keepdims=True) + jnp.sum(qr * ur, axis=-1, keepdims=True)) * (1.0 / QK_DIM)
            coef = r * r * r * sm
            dqa = jnp.concatenate([r * un - qn * coef, r * ur - qr * coef], axis=1).astype(BF16)
            dgqn = dgqn + jnp.sum(dqo * (qn * r), axis=0, keepdims=True)
            dgqr = dgqr + jnp.sum(dz * (qr * r), axis=0, keepdims=True)
            dqa_ref[:, h * QK_PAD:(h + 1) * QK_PAD] = dqa
            dcq = dcq + _nt(dqa, wuq_ref[h])
        dcqp, dgql = _rms_bwd(cqp, rc, gql, dcq, Q_RANK)
        dcqb = dcqp.astype(BF16)
        dcq_ref[...] = dcqb
        dhn = _nt(dcqb, wdq_ref[...])
        dxn, dln = _rms_bwd(x, rx, ln, dhn, D_MODEL)
        dx_ref[...] = dxin_ref[...] + dxn

        @pl.when(pl.program_id(0) == 0)
        def _():
            dln_ref[...] = jnp.zeros_like(dln_ref)
            dgql_ref[...] = jnp.zeros_like(dgql_ref)
            dgqn_ref[...] = jnp.zeros_like(dgqn_ref)
            dgqr_ref[...] = jnp.zeros_like(dgqr_ref)

        dln_ref[...] += dln
        dgql_ref[...] += dgql
        dgqn_ref[...] += dgqn
        dgqr_ref[...] += dgqr

    def tok(cols, dt):
        return jax.ShapeDtypeStruct((s, cols), dt)

    def vec(cols):
        return jax.ShapeDtypeStruct((1, cols), F32)

    return pl.pallas_call(
        body, name=name, grid=(s // t,),
        out_shape=[tok(D_MODEL, F32), tok(D_MODEL, BF16), tok(Q_RANK, BF16), tok(N_HEADS * QK_PAD, BF16),
                   tok(Q_RANK, BF16), vec(D_MODEL), vec(Q_RANK), vec(NOPE), vec(LANES)],
        in_specs=[_rows(t, D_MODEL), _rows(t, D_MODEL), _rows(t, N_HEADS * QK_PAD)] + _q_specs(t),
        out_specs=[_rows(t, D_MODEL), _rows(t, D_MODEL), _rows(t, Q_RANK), _rows(t, N_HEADS * QK_PAD),
                   _rows(t, Q_RANK), _full((1, D_MODEL)), _full((1, Q_RANK)), _full((1, NOPE)), _full((1, LANES))],
        compiler_params=_cparams(1, VMEM_MID),
    )(x, dxin, dq, ln, wdq, gql, wuq, gqn, gqr, cos, sin)


SM_SCALE = 1.0 / math.sqrt(QK_DIM)
LOG2_E = math.log2(math.e)
EXP2_SCALE = SM_SCALE * LOG2_E
NEG = -1e30


def _diag_mask(t):
    qpos = lax.broadcasted_iota(jnp.int32, (t, t), 0)
    kpos = lax.broadcasted_iota(jnp.int32, (t, t), 1)
    return lax.shift_right_logical(kpos, 6) <= lax.shift_right_logical(qpos, 6)


def _att_fwd(q, k, v, name):
    s = q.shape[0]
    t = min(512, s)
    nb = s // t

    def body(q_ref, k_ref, v_ref, o_ref, lse_ref):
        qi = pl.program_id(1)
        qq = q_ref[...]

        def block(ki, carry, masked):
            m_old, l_old, acc = carry
            rows = pl.ds(pl.multiple_of(ki * t, t), t)
            sc = _nt(qq, k_ref[rows, :])
            if masked:
                sc = jnp.where(_diag_mask(t), sc, NEG)
            m_new = jnp.maximum(m_old, jnp.max(sc, axis=-1, keepdims=True))
            p = jnp.exp2((sc - m_new) * EXP2_SCALE)
            alpha = jnp.exp2((m_old - m_new) * EXP2_SCALE)
            l_new = alpha * l_old + jnp.sum(p, axis=-1, keepdims=True)
            acc = alpha * acc + _nn(p.astype(BF16), v_ref[rows, :])
            return m_new, l_new, acc

        init = (jnp.full((t, 1), NEG, F32), jnp.zeros((t, 1), F32), jnp.zeros((t, V_DIM), F32))
        def pair(k0, c):
            return block(k0 + 1, block(k0, c, False), False)

        carry = lax.fori_loop(0, qi // 4, lambda j, c: pair(4 * j + 2, pair(4 * j, c)), init)
        done = 4 * (qi // 4)
        carry = lax.cond((qi & 2) != 0, lambda c: pair(done, c), lambda c: c, carry)
        carry = lax.cond((qi & 1) != 0, lambda c: block(qi - 1, c, False), lambda c: c, carry)
        m_fin, l_fin, acc = block(qi, carry, True)
        o_ref[...] = (acc / l_fin).astype(BF16)
        lse_ref[...] = jnp.broadcast_to(m_fin * SM_SCALE + jnp.log(l_fin), (t, LANES))

    return pl.pallas_call(
        body, name=name, grid=(N_HEADS, nb),
        out_shape=[jax.ShapeDtypeStruct((s, N_HEADS * V_DIM), BF16), jax.ShapeDtypeStruct((s, N_HEADS * LANES), F32)],
        in_specs=[pl.BlockSpec((t, QK_PAD), lambda h, qi: (qi, h)),
                  pl.BlockSpec((s, QK_PAD), lambda h, qi: (0, h)),
                  pl.BlockSpec((s, V_DIM), lambda h, qi: (0, h))],
        out_specs=[pl.BlockSpec((t, V_DIM), lambda h, qi: (qi, h)),
                   pl.BlockSpec((t, LANES), lambda h, qi: (qi, h))],
        compiler_params=_cparams(2, VMEM_MID),
    )(q, k, v)


def _att_bwd(q, k, v, do, stats, name, deps=()):
    s = q.shape[0]
    t = min(512, s)
    nb = s // t

    def body(q_ref, k_ref, v_ref, do_ref, st_ref, dq_ref, dk_ref, dv_ref):
        ki = pl.program_id(1)
        kk, vv = k_ref[...], v_ref[...]

        @pl.when(ki == 0)
        def _():
            dq_ref[...] = jnp.zeros_like(dq_ref)

        def block(qi, carry, masked):
            dk, dv = carry
            rows = pl.ds(pl.multiple_of(qi * t, t), t)
            qq, dob = q_ref[rows, :], do_ref[rows, :]
            sc = _nt(qq, kk)
            if masked:
                sc = jnp.where(_diag_mask(t), sc, NEG)
            st = st_ref[rows, :]
            p = jnp.exp2(sc * EXP2_SCALE - st[:, 0:1])
            dp = _nt(dob, vv)
            ds = (p * (dp - st[:, 1:2])).astype(BF16)
            dq_ref[rows, :] += _nn(ds, kk)
            return dk + _tn(ds, qq), dv + _tn(p.astype(BF16), dob)

        carry = block(ki, (jnp.zeros((t, QK_PAD), F32), jnp.zeros((t, V_DIM), F32)), True)
        rest = nb - 1 - ki
        carry = lax.fori_loop(
            0, rest // 2, lambda j, c: block(ki + 2 * j + 2, block(ki + 2 * j + 1, c, False), False), carry)
        dk, dv = lax.cond(rest % 2 == 1, lambda c: block(nb - 1, c, False), lambda c: c, carry)
        dk_ref[...] = dk * SM_SCALE
        dv_ref[...] = dv

        @pl.when(ki == nb - 1)
        def _():
            dq_ref[...] = dq_ref[...] * SM_SCALE

    def head(h, ki):
        return (0, h)

    def kblock(h, ki):
        return (ki, h)

    return _pcall(
        body, (q, k, v, do, stats), deps, name=name, grid=(N_HEADS, nb),
        out_shape=[jax.ShapeDtypeStruct((s, N_HEADS * QK_PAD), F32), jax.ShapeDtypeStruct((s, N_HEADS * QK_PAD), F32),
                   jax.ShapeDtypeStruct((s, N_HEADS * V_DIM), F32)],
        in_specs=[pl.BlockSpec((s, QK_PAD), head), pl.BlockSpec((t, QK_PAD), kblock), pl.BlockSpec((t, V_DIM), kblock),
                  pl.BlockSpec((s, V_DIM), head), pl.BlockSpec((s, LANES), head)],
        out_specs=[pl.BlockSpec((s, QK_PAD), head), pl.BlockSpec((t, QK_PAD), kblock), pl.BlockSpec((t, V_DIM), kblock)],
        compiler_params=_cparams(2, VMEM_MID),
    )


def _o_fwd(x, o, wo, name):
    s = x.shape[0]
    t = min(512, s)

    def body(x_ref, o_ref, wo_ref, xo_ref):
        xo_ref[...] = x_ref[...] + _nn(o_ref[...], wo_ref[...])

    return pl.pallas_call(
        body, name=name, grid=(s // t,),
        out_shape=jax.ShapeDtypeStruct((s, D_MODEL), F32),
        in_specs=[_rows(t, D_MODEL), _rows(t, D_MODEL), _full((D_MODEL, D_MODEL))],
        out_specs=_rows(t, D_MODEL),
        compiler_params=_cparams(1, VMEM_MID),
    )(x, o, wo)


def _o_bwd(dx, wo, o, lse, name, deps=()):
    s = dx.shape[0]
    t = min(512, s)

    def body(dx_ref, wo_ref, o_ref, lse_ref, do_ref, dxb_ref, st_ref):
        dxb = dx_ref[...].astype(BF16)
        dxb_ref[...] = dxb
        dob = _nt(dxb, wo_ref[...]).astype(BF16)
        do_ref[...] = dob
        lane = lax.broadcasted_iota(jnp.int32, (t, LANES), 1)
        for h in range(N_HEADS):
            sl = slice(h * V_DIM, (h + 1) * V_DIM)
            dsum = jnp.sum(dob[:, sl].astype(F32) * o_ref[:, sl].astype(F32), axis=-1, keepdims=True)
            st_ref[:, sl] = jnp.where(lane == 0, lse_ref[:, sl] * LOG2_E, jnp.where(lane == 1, dsum, 0.0))

    tok = jax.ShapeDtypeStruct((s, D_MODEL), BF16)
    return _pcall(
        body, (dx, wo, o, lse), deps, name=name, grid=(s // t,),
        out_shape=[tok, tok, jax.ShapeDtypeStruct((s, N_HEADS * LANES), F32)],
        in_specs=[_rows(t, D_MODEL), _full((D_MODEL, D_MODEL)), _rows(t, D_MODEL), _rows(t, N_HEADS * LANES)],
        out_specs=[_rows(t, D_MODEL), _rows(t, D_MODEL), _rows(t, N_HEADS * LANES)],
        compiler_params=_cparams(1, VMEM_MID),
    )


def _loss_head(y, target, name):
    s = y.shape[0]
    t = min(512, s)

    def body(y_ref, t_ref, dy_ref, sq_ref):
        e = y_ref[...] - t_ref[...]
        dy_ref[...] = e * (1.0 / D_MODEL)

        @pl.when(pl.program_id(0) == 0)
        def _():
            sq_ref[...] = jnp.zeros_like(sq_ref)

        sq_ref[...] += jnp.sum(e * e, axis=0, keepdims=True)

    return pl.pallas_call(
        body, name=name, grid=(s // t,),
        out_shape=[jax.ShapeDtypeStruct((s, D_MODEL), F32), jax.ShapeDtypeStruct((1, D_MODEL), F32)],
        in_specs=[_rows(t, D_MODEL), _rows(t, D_MODEL)],
        out_specs=[_rows(t, D_MODEL), _full((1, D_MODEL))],
        compiler_params=_cparams(1),
    )(y, target)


def _adamw(w, g, m, v, name):
    shape = w.shape
    c = shape[-1]
    r = math.prod(shape[:-1])
    tb = r
    for cand in (512, 256, 128):
        if r % cand == 0 and r > cand:
            tb = cand
            break

    def body(w_ref, g_ref, m_ref, v_ref, d_ref, mo_ref, vo_ref):
        gr = g_ref[...]
        mn = ADAM_B1 * m_ref[...] + (1.0 - ADAM_B1) * gr
        vn = ADAM_B2 * v_ref[...] + (1.0 - ADAM_B2) * (gr * gr)
        m_hat = mn / (1.0 - ADAM_B1 ** ADAM_STEP)
        v_hat = vn / (1.0 - ADAM_B2 ** ADAM_STEP)
        d_ref[...] = -ADAM_LR * (m_hat / (jnp.sqrt(v_hat) + ADAM_EPS) + ADAM_WD * w_ref[...])
        mo_ref[...] = mn
        vo_ref[...] = vn

    spec = pl.BlockSpec((tb, c), lambda i: (i, 0))
    flat = jax.ShapeDtypeStruct((r, c), F32)
    outs = pl.pallas_call(
        body, name=name, grid=(r // tb,),
        out_shape=[flat, flat, flat],
        in_specs=[spec] * 4, out_specs=[spec] * 3,
        compiler_params=_cparams(1),
    )(w.reshape(r, c), g.reshape(r, c), m.reshape(r, c), v.reshape(r, c))
    return [a.reshape(shape) for a in outs]


def _pad_cols(a, width):
    return jnp.pad(a, [(0, 0)] * (a.ndim - 1) + [(0, width - a.shape[-1])])


def _owner_view(a, sz):
    return a.reshape(a.shape[0], N_CHIPS, 2, sz, a.shape[-1])


def kernel(x, positions, ln_mix_a, w_pool, b_pool, pool_scale, ln_ffn, w_gate, w_up, w_down, ln_kv, w_dkv, g_kv_latent, w_uk, w_uv, g_k, ln_mix_b, w_dq, g_q_latent, w_uq, g_q, w_o, loss_target, m_ln_mix_a, m_w_pool, m_b_pool, m_pool_scale, m_ln_ffn, m_w_gate, m_w_up, m_w_down, m_ln_kv, m_w_dkv, m_g_kv_latent, m_w_uk, m_w_uv, m_g_k, m_ln_mix_b, m_w_dq, m_g_q_latent, m_w_uq, m_g_q, m_w_o, v_ln_mix_a, v_w_pool, v_b_pool, v_pool_scale, v_ln_ffn, v_w_gate, v_w_up, v_w_down, v_ln_kv, v_w_dkv, v_g_kv_latent, v_w_uk, v_w_uv, v_g_k, v_ln_mix_b, v_w_dq, v_g_q_latent, v_w_uq, v_g_q, v_w_o):
    weights = dict(ln_mix_a=ln_mix_a, w_pool=w_pool, b_pool=b_pool, pool_scale=pool_scale, ln_ffn=ln_ffn,
                   w_gate=w_gate, w_up=w_up, w_down=w_down, ln_kv=ln_kv, w_dkv=w_dkv, g_kv_latent=g_kv_latent,
                   w_uk=w_uk, w_uv=w_uv, g_k=g_k, ln_mix_b=ln_mix_b, w_dq=w_dq, g_q_latent=g_q_latent,
                   w_uq=w_uq, g_q=g_q, w_o=w_o)
    mom1 = dict(ln_mix_a=m_ln_mix_a, w_pool=m_w_pool, b_pool=m_b_pool, pool_scale=m_pool_scale, ln_ffn=m_ln_ffn,
                w_gate=m_w_gate, w_up=m_w_up, w_down=m_w_down, ln_kv=m_ln_kv, w_dkv=m_w_dkv,
                g_kv_latent=m_g_kv_latent, w_uk=m_w_uk, w_uv=m_w_uv, g_k=m_g_k, ln_mix_b=m_ln_mix_b, w_dq=m_w_dq,
                g_q_latent=m_g_q_latent, w_uq=m_w_uq, g_q=m_g_q, w_o=m_w_o)
    mom2 = dict(ln_mix_a=v_ln_mix_a, w_pool=v_w_pool, b_pool=v_b_pool, pool_scale=v_pool_scale, ln_ffn=v_ln_ffn,
                w_gate=v_w_gate, w_up=v_w_up, w_down=v_w_down, ln_kv=v_ln_kv, w_dkv=v_w_dkv,
                g_kv_latent=v_g_kv_latent, w_uk=v_w_uk, w_uv=v_w_uv, g_k=v_g_k, ln_mix_b=v_ln_mix_b, w_dq=v_w_dq,
                g_q_latent=v_g_q_latent, w_uq=v_w_uq, g_q=v_g_q, w_o=v_w_o)
    names = list(weights)
    dev = 4 * lax.axis_index("x") + 2 * lax.axis_index("y") + lax.axis_index("c")
    core = lax.axis_index("c").astype(jnp.int32).reshape(1)
    chip = (2 * lax.axis_index("x") + lax.axis_index("y")).astype(jnp.int32).reshape(1)

    xs = x[0]
    target = loss_target[0]
    cos, sin = _rope_tables(positions[0])

    def placed(shard):
        buf = lax.empty((shard.shape[0], N_DEV) + shard.shape[1:], shard.dtype)
        return lax.dynamic_update_slice(buf, shard[:, None], (0, dev, 0, 0))

    def ffn_shard(l):
        return jnp.stack([w_gate[l].T, w_up[l].T, w_down[l]]).astype(BF16)

    groups = {"ffn0": [placed(ffn_shard(0))]}
    small_sh = jnp.concatenate([ln_mix_a.reshape(1, -1), pool_scale.reshape(1, -1), b_pool.reshape(1, -1)], axis=1)
    wp_g, small_g = _all_gather([w_pool.astype(BF16), small_sh], [2, 0], "gather_first")
    wp_all = wp_g.reshape(2, 4, GROUP_DIM, GROUP_DIM)
    small_g = small_g.reshape(N_DEV, 3, 2, LANES)
    ln_a_all = small_g[:, 0].transpose(1, 0, 2).reshape(2, 1, D_MODEL)
    sc_all = small_g[:, 1].transpose(1, 0, 2).reshape(2, 1, D_MODEL)
    bp_all = small_g[:, 2].reshape(N_DEV, 2, 4, 32).transpose(1, 2, 0, 3).reshape(2, 1, D_MODEL)
    sp0 = _copies_start(groups["ffn0"], 1, _gather_spread, "spread_ffn0", deps=[small_g])
    zero = sp0[3][0, 0].astype(BF16)
    for l in (1, 2, 3):
        groups[f"ffn{l}"] = [placed(ffn_shard(l) + zero)]
    groups["att"] = [placed(a.astype(BF16) + zero) for a in (
        w_dkv[None, :, :KV_RANK], _pad_cols(w_dkv[None, :, KV_RANK:], LANES), w_uk[None], w_uv[None],
        w_dq, _pad_cols(w_uq, QK_PAD), w_o)]

    def spread_start(nm, deps):
        return _copies_start(groups[nm], len(groups[nm]), _gather_spread, f"spread_{nm}", deps=deps)

    def spread_wait(nm, state, after):
        ssem, rsem, bufs, _ = state
        return _copies_wait(bufs, ssem, rsem, after, _blocks_moved(4), f"spread_done_{nm}")

    def relay_start(nm, bufs, deps=()):
        return _copies_start(bufs, len(bufs), _gather_relay, f"relay_{nm}", deps=deps)

    def relay_wait(nm, state, after):
        ssem, rsem, bufs, _ = state
        return _copies_wait(bufs, ssem, rsem, after, _blocks_moved(3), f"relay_done_{nm}")

    gkn = g_k[:NOPE].reshape(1, NOPE)
    gkr = _pad_cols(g_k[NOPE:].reshape(1, ROPE), LANES)
    gl = g_kv_latent.reshape(1, KV_RANK)
    lnkv = ln_kv.reshape(1, D_MODEL)

    x_in, x_mid, pooled, gates, ups, w_ffn = [], [], [], [], [], []
    qs, outs, lses = [], [], []

    def mixer(l, cur, deps):
        x_in.append(cur)
        mid, dsave = _mix_fwd(cur, ln_a_all[l], wp_all[l], bp_all[l], sc_all[l], f"mix_fwd{l}", deps=deps)
        pooled.append(dsave)
        x_mid.append(mid)
        return mid

    def q_args(j):
        return (ln_mix_b[j].reshape(1, -1), wdq_all[j], g_q_latent[j].reshape(1, -1), wuq_all[j],
                g_q[j, :NOPE].reshape(1, -1), _pad_cols(g_q[j, NOPE:].reshape(1, -1), LANES), cos, sin)

    def attention(j, cur, deps):
        x_in.append(cur)
        q = _q_fwd(cur, *q_args(j), f"q_fwd{j}", deps=deps)
        o, lse = _att_fwd(q, k_sh, v_sh, f"att_fwd{j}")
        mid = _o_fwd(cur, o, wo_all[j], f"o_fwd{j}")
        qs.append(q)
        outs.append(o)
        lses.append(lse)
        x_mid.append(mid)
        return mid

    def ffn(l, mid, relayed):
        w_l = relayed[0].reshape(3, D_FF, D_MODEL)
        w_ffn.append(w_l)
        cur, gate, up = _ffn_fwd(mid, ln_ffn[l].reshape(1, -1), w_l, f"ffn_fwd{l}")
        gates.append(gate)
        ups.append(up)
        return cur

    mid = mixer(0, xs, [sp0[3]])
    prepared = [buf for nm in ("ffn1", "att", "ffn2", "ffn3") for buf in groups[nm]]
    landed0 = spread_wait("ffn0", sp0, [mid] + prepared)
    sp1 = spread_start("ffn1", [landed0[0]])
    rl0 = relay_start("ffn0", landed0, [sp1[3]])
    cur = ffn(0, mid, relay_wait("ffn0", rl0, rl0[3]))

    landed1 = spread_wait("ffn1", sp1, cur)
    sp_att = spread_start("att", [landed1[0]])
    sp2 = spread_start("ffn2", [landed1[0]])
    rl1 = relay_start("ffn1", landed1, [sp_att[3], sp2[3]])
    mid = mixer(1, cur, [rl1[3]])
    cur = ffn(1, mid, relay_wait("ffn1", rl1, mid))
    x_kv = cur

    landed_att = spread_wait("att", sp_att, cur)
    landed2 = spread_wait("ffn2", sp2, cur)
    sp3 = spread_start("ffn3", [landed2[0]])
    rl_att = relay_start("att", landed_att, [sp3[3]])
    rl2 = relay_start("ffn2", landed2, [sp3[3]])
    att_bufs = relay_wait("att", rl_att, rl2[3])
    wc = att_bufs[0].reshape(D_MODEL, KV_RANK)
    wpe = att_bufs[1].reshape(D_MODEL, LANES)
    wuk_g = att_bufs[2].reshape(N_HEADS, KV_RANK, NOPE).transpose(1, 0, 2).reshape(KV_RANK, N_HEADS * NOPE)
    wuv_g = att_bufs[3].reshape(N_HEADS, KV_RANK, V_DIM).transpose(1, 0, 2).reshape(KV_RANK, N_HEADS * V_DIM)
    wdq_all = att_bufs[4].reshape(2, D_MODEL, Q_RANK)
    wuq_all = att_bufs[5]
    wo_all = att_bufs[6].reshape(2, D_MODEL, D_MODEL)
    k_sh, v_sh = _kv_fwd(cur, lnkv, wc, wpe, gl, wuk_g, wuv_g, gkn, gkr, cos, sin, "kv_fwd")
    mid = attention(0, cur, [])
    cur = ffn(2, mid, relay_wait("ffn2", rl2, mid))

    landed3 = spread_wait("ffn3", sp3, cur)
    rl3 = relay_start("ffn3", landed3)
    mid = attention(1, cur, [rl3[3]])
    cur = ffn(3, mid, relay_wait("ffn3", rl3, mid))

    dx, sq_cols = _loss_head(cur, target, "loss_head")

    small = {}
    sizes = dict(ffn0=FF_SHARD, ffn1=FF_SHARD, ffn2=FF_SHARD, ffn3=FF_SHARD, wo=128, kv512=128, dkv_pe=128,
                 wdq=128, wuqT=QK_PAD, wpool=32)
    big = dict(wo=lax.empty((2, D_MODEL, D_MODEL), BF16), kv512=lax.empty((3, D_MODEL, KV_RANK), BF16),
               dkv_pe=lax.empty((1, D_MODEL, LANES), BF16), wdq=lax.empty((2, D_MODEL, Q_RANK), BF16),
               wuqT=lax.empty((2, N_HEADS * QK_PAD, Q_RANK), BF16), wpool=lax.empty((8, GROUP_DIM, GROUP_DIM), BF16))
    for l in range(4):
        big[f"ffn{l}"] = lax.empty((3, D_FF, D_MODEL), BF16)
    red = {}

    def pair_start(nms, tag):
        arrs = []
        for nm in nms:
            view = _owner_view(big[nm], sizes[nm])
            arrs += [view, lax.empty((view.shape[0], N_CHIPS) + view.shape[3:], BF16)]
        return nms, tag, _copies_start(arrs, len(nms), _pair_send, f"pair_start_{tag}")

    def chip_start(state, after):
        nms, tag, (ssem, rsem, arrs, _) = state
        arrs = _copies_wait(arrs, ssem, rsem, after, _landed, f"pair_done_{tag}")
        out = []
        for t, nm in enumerate(nms):
            part = _pair_sum(arrs[2 * t], arrs[2 * t + 1], core, f"pair_sum_{nm}")
            out += [part, lax.empty((3, part.shape[0]) + part.shape[2:], BF16)]
        return nms, tag, _copies_start(out, len(nms), _chip_send, f"chip_start_{tag}")

    deferred = []
    updates = {}

    def chip_finish(state, after, defer=False):
        nms, tag, (ssem, rsem, arrs, _) = state
        arrs = _copies_wait(arrs, ssem, rsem, after, _landed, f"chip_done_{tag}")
        for t, nm in enumerate(nms):
            if defer:
                deferred.append((nm, arrs[2 * t], arrs[2 * t + 1]))
            else:
                red[nm] = _chip_sum(arrs[2 * t], arrs[2 * t + 1], chip, f"chip_sum_{nm}")

    ffn_grads = {nm: lax.empty((4, FF_SHARD, D_MODEL), F32) for nm in ("w_gate", "w_up", "w_down")}

    def place_ffn_grads(l):
        g = red[f"ffn{l}"]
        for k, nm in enumerate(("w_gate", "w_up", "w_down")):
            ffn_grads[nm] = ffn_grads[nm].at[l].set(g[k])

    dks, dvs = [], []
    pending = None
    bwd_deps = []
    for l in (3, 2, 1, 0):
        key = f"ffn{l}"
        dx, act, dgb, dub, hn, dyb, dln = _ffn_bwd(x_mid[l], dx, gates[l], ups[l], ln_ffn[l].reshape(1, -1),
                                                     w_ffn[l], f"ffn_bwd{l}", deps=bwd_deps)
        bwd_deps = []
        small[f"ln_ffn{l}"] = dln
        if l == 1:
            att_chip = chip_start(att_pair, dx)
            tn_deps = [att_chip[2][3]]
        else:
            tn_deps = []
        if pending:
            chip_finish(pending, dx, defer=True)
            pending = None
        big[key] = _tn_matmul(dgb, hn, big[key], 0, f"dw_gate{l}", m_chunk=FF_HALF, deps=tn_deps)
        big[key] = _tn_matmul(dub, hn, big[key], 1, f"dw_up{l}", m_chunk=FF_HALF)
        big[key] = _tn_matmul(act, dyb, big[key], 2, f"dw_down{l}", m_chunk=FF_HALF)
        if l == 1:
            chip_finish(att_chip, big[key], defer=True)
        ffn_pair = pair_start([key], key)
        if l >= 2:
            j = l - 2
            do, dxb, stats = _o_bwd(dx, wo_all[j], outs[j], lses[j], f"o_bwd{j}", deps=[ffn_pair[2][3]])
            big["wo"] = _tn_matmul(outs[j], dxb, big["wo"], j, f"dw_o{j}")
            ffn_chip = chip_start(ffn_pair, big["wo"])
            dq, dk, dv = _att_bwd(qs[j], k_sh, v_sh, do, stats, f"att_bwd{j}", deps=[ffn_chip[2][3]])
            chip_finish(ffn_chip, dq, defer=True)
            dks.append(dk)
            dvs.append(dv)
            dx, hnq, cqn, dqa, dcq, dln, dgql, dgqn, dgqr = _q_bwd(x_in[l], dx, dq, *q_args(j), f"q_bwd{j}")
            small[f"ln_mix_b{j}"] = dln
            small[f"g_q_latent{j}"] = dgql
            small[f"g_q{j}"] = jnp.concatenate([dgqn, dgqr[:, :ROPE]], axis=1)
            big["wdq"] = _tn_matmul(hnq, dcq, big["wdq"], j, f"dw_dq{j}")
            big["wuqT"] = _tn_matmul(dqa, cqn, big["wuqT"], j, f"dw_uq{j}")
            if l == 2:
                (dx, hnk, cn, dknb, dvb, dccb, dpeb, dlnkv, dgl, dgkn, dgkr) = _kv_bwd(
                    x_kv, dx, dks, dvs, lnkv, wc, wpe, gl, wuk_g, wuv_g, gkn, gkr, cos, sin, "kv_bwd")
                small["ln_kv"] = dlnkv
                small["g_kv_latent"] = dgl
                small["g_k"] = jnp.concatenate([dgkn, dgkr[:, :ROPE]], axis=1)
                big["kv512"] = _tn_matmul(dknb, cn, big["kv512"], 0, "dw_uk")
                big["kv512"] = _tn_matmul(dvb, cn, big["kv512"], 1, "dw_uv")
                big["kv512"] = _tn_matmul(hnk, dccb, big["kv512"], 2, "dw_dkv_c")
                big["dkv_pe"] = _tn_matmul(hnk, dpeb, big["dkv_pe"], 0, "dw_dkv_pe")
                att_pair = pair_start(["wo", "kv512", "dkv_pe", "wdq", "wuqT"], "att")
                bwd_deps = [att_pair[2][3]]
        else:
            dx, dyp, dsc, db, dln = _mix_bwd(x_in[l], dx, pooled[l], ln_a_all[l], wp_all[l], bp_all[l], sc_all[l],
                                             f"mix_bwd{l}", deps=[ffn_pair[2][3]])
            small[f"ln_mix_a{l}"] = dln
            small[f"pool_scale{l}"] = dsc
            small[f"b_pool{l}"] = db
            ffn_chip = chip_start(ffn_pair, dx)
            big["wpool"] = _tn_matmul(pooled[l], dyp, big["wpool"], 4 * l, f"dw_pool{l}", groups=4,
                                      deps=[ffn_chip[2][3]])
            if l == 1:
                pending = ffn_chip
                bwd_deps = [ffn_chip[2][3]]
            else:
                for nm, part, land in deferred:
                    red[nm] = _chip_sum(part, land, chip, f"chip_sum_{nm}", deps=[ffn_chip[2][3]])
                    if nm.startswith("ffn"):
                        place_ffn_grads(int(nm[-1]))
                early_grads = dict(
                    w_dkv=jnp.concatenate([red["kv512"][2], red["dkv_pe"][0][:, :ROPE]], axis=1),
                    w_uk=red["kv512"][0].T, w_uv=red["kv512"][1].T, w_dq=red["wdq"],
                    w_uq=red["wuqT"].transpose(0, 2, 1)[:, :, :QK_DIM], w_o=red["wo"])
                for nm, g in early_grads.items():
                    updates[nm] = _adamw(weights[nm], g, mom1[nm], mom2[nm], f"adamw_{nm}")
                chip_finish(ffn_chip, [big["wpool"]] + list(ffn_grads.values()) + [u[0] for u in updates.values()])
                place_ffn_grads(0)
    grad_x = dx[None]
    pool_pair = pair_start(["wpool"], "wpool")
    pool_chip = chip_start(pool_pair, pool_pair[2][3])
    chip_finish(pool_chip, pool_chip[2][3])

    vec_names = (["loss"] + [f"ln_ffn{l}" for l in range(4)] + ["ln_kv", "g_kv_latent", "g_k"]
                 + [f"{p}{j}" for p in ("ln_mix_b", "g_q_latent", "g_q") for j in range(2)]
                 + [f"{p}{l}" for p in ("ln_mix_a", "pool_scale", "b_pool") for l in range(2)])
    small["loss"] = sq_cols
    widths = [small[nm].shape[1] for nm in vec_names]
    padded = [-(-w // LANES) * LANES for w in widths]
    packed = jnp.concatenate([_pad_cols(small[nm], pw) for nm, pw in zip(vec_names, padded)], axis=1)
    (all_vecs,) = _all_gather([packed], [0], "gather_vectors")
    total = _sum_lead(all_vecs, "sum_vectors")
    vec = {}
    off = 0
    for nm, w, pw in zip(vec_names, widths, padded):
        vec[nm] = total[0, off:off + w]
        off += pw
    loss = 0.5 * jnp.sum(vec["loss"]) * (1.0 / D_MODEL)

    def own_cols(full, width):
        return lax.dynamic_slice_in_dim(full, dev * width, width, axis=full.ndim - 1)

    grads = dict(
        ln_mix_a=own_cols(jnp.stack([vec["ln_mix_a0"], vec["ln_mix_a1"]]), LANES),
        w_pool=red["wpool"].reshape(2, 4, 32, GROUP_DIM),
        b_pool=own_cols(jnp.stack([vec["b_pool0"], vec["b_pool1"]]).reshape(2, 4, GROUP_DIM), 32),
        pool_scale=own_cols(jnp.stack([vec["pool_scale0"], vec["pool_scale1"]]), LANES),
        ln_ffn=jnp.stack([vec[f"ln_ffn{l}"] for l in range(4)]),
        w_gate=ffn_grads["w_gate"],
        w_up=ffn_grads["w_up"],
        w_down=ffn_grads["w_down"],
        ln_kv=vec["ln_kv"],
        g_kv_latent=vec["g_kv_latent"],
        g_k=vec["g_k"],
        ln_mix_b=jnp.stack([vec["ln_mix_b0"], vec["ln_mix_b1"]]),
        g_q_latent=jnp.stack([vec["g_q_latent0"], vec["g_q_latent1"]]),
        g_q=jnp.stack([vec["g_q0"], vec["g_q1"]]),
        **early_grads,
    )

    deltas, new_m, new_v = {}, {}, {}
    for nm in names:
        w = weights[nm]
        if nm in updates:
            deltas[nm], new_m[nm], new_v[nm] = updates[nm]
            continue
        if nm in ("w_gate", "w_up"):
            def swap(a):
                return a.transpose(0, 2, 1)
            d, mo, vo = _adamw(swap(w), grads[nm], swap(mom1[nm]), swap(mom2[nm]), f"adamw_{nm}")
            deltas[nm], new_m[nm], new_v[nm], grads[nm] = swap(d), swap(mo), swap(vo), swap(grads[nm])
            continue
        shape = w.shape if w.ndim > 1 else (1, w.shape[0])
        d, mo, vo = _adamw(w.reshape(shape), grads[nm].reshape(shape), mom1[nm].reshape(shape),
                           mom2[nm].reshape(shape), f"adamw_{nm}")
        deltas[nm], new_m[nm], new_v[nm] = d.reshape(w.shape), mo.reshape(w.shape), vo.reshape(w.shape)

    return (loss, grad_x, *[grads[nm].reshape(weights[nm].shape) for nm in names], *[deltas[nm] for nm in names],
            *[new_m[nm] for nm in names], *[new_v[nm] for nm in names])
```

```python
import functools
import math

import jax
import jax.numpy as jnp
from jax import lax
from jax.experimental import pallas as pl
from jax.experimental.pallas import tpu as pltpu

F32 = jnp.float32
BF16 = jnp.bfloat16
MESH = pl.DeviceIdType.MESH

D_MODEL = 1024
D_FF = 2816
N_DEV = 8
N_CHIPS = 4
FF_SHARD = D_FF // N_DEV
FF_HALF = D_FF // 2
N_HEADS = 8
NOPE = 128
ROPE = 64
QK_DIM = NOPE + ROPE
QK_PAD = 256
V_DIM = 128
Q_RANK = 256
KV_RANK = 512
POOL_WINDOWS = (2, 4, 8, 16)
GROUP_DIM = 256
HALO = 128
CHUNK = 64
ROPE_THETA = 10000.0
EPS = 1e-6
LANES = 128

ADAM_LR = 0.001
ADAM_B1 = 0.9
ADAM_B2 = 0.999
ADAM_EPS = 1e-08
ADAM_WD = 0.01
ADAM_STEP = 10

PROJ_ROWS = 256
MIX_ROWS = 256
VMEM_BIG = 56 * 2**20
VMEM_MID = 40 * 2**20


def _nn(a, b):
    return lax.dot_general(a, b, (((1,), (0,)), ((), ())), preferred_element_type=F32)


def _nt(a, b):
    return lax.dot_general(a, b, (((1,), (1,)), ((), ())), preferred_element_type=F32)


def _tn(a, b):
    return lax.dot_general(a, b, (((0,), (0,)), ((), ())), preferred_element_type=F32)


def _rms(x, g, n):
    r = lax.rsqrt(jnp.sum(x * x, axis=-1, keepdims=True) * (1.0 / n) + EPS)
    return (x * r) * g, r


def _rms_bwd(x, r, g, dy, n):
    u = dy * g
    s = jnp.sum(x * u, axis=-1, keepdims=True) * (1.0 / n)
    dx = r * u - x * (r * r * r * s)
    dg = jnp.sum(dy * (x * r), axis=0, keepdims=True)
    return dx, dg


def _swap_perm():
    i = lax.broadcasted_iota(jnp.int32, (LANES, LANES), 0)
    j = lax.broadcasted_iota(jnp.int32, (LANES, LANES), 1)
    half = ROPE // 2
    hit = ((j < half) & (i == j + half)) | ((j >= half) & (j < ROPE) & (i == j - half))
    return jnp.where(hit, 1.0, 0.0).astype(BF16)


def _swap_halves(z, perm):
    hi = z.astype(BF16)
    lo = (z - hi.astype(F32)).astype(BF16)
    return _nn(hi, perm) + _nn(lo, perm)


def _sigmoid(x):
    return 1.0 / (1.0 + jnp.exp(-x))


def _cparams(n_grid, vmem=None):
    return pltpu.CompilerParams(dimension_semantics=("arbitrary",) * n_grid, vmem_limit_bytes=vmem)


def _rows(t, cols):
    return pl.BlockSpec((t, cols), lambda i: (i, 0))


def _full(shape):
    nd = len(shape)
    return pl.BlockSpec(shape, lambda *_: (0,) * nd)


ANY = pl.BlockSpec(memory_space=pl.ANY)


def _pcall(body, args, deps, *, in_specs, **kw):
    n_in, n_dep = len(args), len(deps)

    def ordered(*refs):
        body(*refs[:n_in], *refs[n_in + n_dep:])

    return pl.pallas_call(ordered, in_specs=list(in_specs) + [ANY] * n_dep, **kw)(*args, *deps)


def _place():
    x, y, c = lax.axis_index("x"), lax.axis_index("y"), lax.axis_index("c")
    return x, y, c


def _all_gather(shards, axes, name, deps=()):
    n, nd = len(shards), len(deps)
    out_shape = [jax.ShapeDtypeStruct(s.shape[:a] + (N_DEV,) + s.shape[a:], s.dtype) for s, a in zip(shards, axes)]

    def body(*refs):
        ins, outs = refs[:n], refs[n + nd:2 * n + nd]
        send_sems, recv_sems, local_sems = refs[2 * n + nd:]
        x, y, c = _place()
        me, sibling = (x, y, c), (x, y, 1 - c)
        chips = [(1 - x, y), (x, 1 - y), (1 - x, 1 - y)]

        def slot(t, dev):
            idx = 4 * dev[0] + 2 * dev[1] + dev[2]
            return outs[t].at[(slice(None),) * axes[t] + (idx,)]

        def copy(t, k, block, to, src=None):
            return pltpu.make_async_remote_copy(
                src_ref=slot(t, block) if src is None else src, dst_ref=slot(t, block),
                send_sem=send_sems.at[t, k], recv_sem=recv_sems.at[t, k],
                device_id=to, device_id_type=MESH)

        mine = [pltpu.make_async_copy(ins[t], slot(t, me), local_sems.at[t]) for t in range(n)]
        for cp in mine:
            cp.start()
        first = []
        for t in range(n):
            first.append(copy(t, 0, me, sibling, src=ins[t]))
            first += [copy(t, 1 + j, me, (*chip, c), src=ins[t]) for j, chip in enumerate(chips)]
        for cp in first:
            cp.start()
        passed = []
        for j, chip in enumerate(chips):
            for t in range(n):
                copy(t, 1 + j, (*chip, c), me).wait_recv()
                cp = copy(t, 4 + j, (*chip, c), sibling)
                cp.start()
                passed.append(cp)
        for t in range(n):
            copy(t, 0, sibling, me).wait_recv()
            for j, chip in enumerate(chips):
                copy(t, 4 + j, (*chip, 1 - c), me).wait_recv()
        for cp in first + passed:
            cp.wait_send()
        for cp in mine:
            cp.wait()

    return pl.pallas_call(
        body, name=name, out_shape=out_shape,
        in_specs=[ANY] * (n + nd), out_specs=[ANY] * n,
        scratch_shapes=[pltpu.SemaphoreType.DMA((n, 7)), pltpu.SemaphoreType.DMA((n, 7)),
                        pltpu.SemaphoreType.DMA((n,))],
    )(*shards, *deps)


HBM = pl.BlockSpec(memory_space=pltpu.HBM)
SEM = pl.BlockSpec(memory_space=pltpu.SEMAPHORE)
EFFECT = pltpu.SideEffectType.DATAFLOW_SIDE_EFFECTING


def _copies_start(arrays, n_sems, plan, name, deps=()):
    n, nd = len(arrays), len(deps)

    def body(*refs):
        for cp in plan(refs[:n], refs[n + nd], refs[n + nd + 1]):
            cp.start()
        refs[-1][...] = jnp.zeros_like(refs[-1])

    outs = pl.pallas_call(
        body, name=name,
        out_shape=(pltpu.SemaphoreType.DMA((n_sems,)), pltpu.SemaphoreType.DMA((n_sems,)),
                   *[pltpu.HBM(a.shape, a.dtype) for a in arrays], jax.ShapeDtypeStruct((8, LANES), F32)),
        in_specs=[HBM] * n + [ANY] * nd,
        out_specs=(SEM, SEM, *[HBM] * n, pl.BlockSpec(memory_space=pltpu.VMEM)),
        input_output_aliases={i: 2 + i for i in range(n)},
        compiler_params=pltpu.CompilerParams(has_side_effects=EFFECT),
    )(*[pltpu.with_memory_space_constraint(a, pltpu.HBM) for a in arrays], *deps)
    return outs[0], outs[1], list(outs[2:2 + n]), outs[-1]


def _copies_wait(arrays, send_sems, recv_sems, after, plan, name):
    n = len(arrays)
    after = list(after) if isinstance(after, (list, tuple)) else [after]

    def body(*refs):
        for cp in plan(refs[:n], refs[n], refs[n + 1]):
            cp.wait_send()
            cp.wait_recv()

    outs = pl.pallas_call(
        body, name=name,
        out_shape=tuple(pltpu.HBM(a.shape, a.dtype) for a in arrays),
        in_specs=[HBM] * n + [SEM, SEM] + [ANY] * len(after), out_specs=tuple([HBM] * n),
        input_output_aliases={i: i for i in range(n)},
        compiler_params=pltpu.CompilerParams(has_side_effects=EFFECT),
    )(*arrays, send_sems, recv_sems, *after)
    return list(outs)


def _remote(src, dst, send_sems, recv_sems, t, to):
    return pltpu.make_async_remote_copy(src_ref=src, dst_ref=dst, send_sem=send_sems.at[t], recv_sem=recv_sems.at[t],
                                        device_id=to, device_id_type=MESH)


def _dev_index(x, y, c):
    return 4 * x + 2 * y + c


def _gather_spread(bufs, send_sems, recv_sems):
    x, y, c = _place()
    mine = _dev_index(x, y, c)
    peers = [(x, y, 1 - c), (1 - x, y, c), (x, 1 - y, c), (1 - x, 1 - y, c)]
    return [_remote(g.at[k, mine], g.at[k, mine], send_sems, recv_sems, t, peer)
            for t, g in enumerate(bufs) for peer in peers for k in range(g.shape[0])]


def _gather_relay(bufs, send_sems, recv_sems):
    x, y, c = _place()
    blocks = [_dev_index(1 - x, y, c), _dev_index(x, 1 - y, c), _dev_index(1 - x, 1 - y, c)]
    return [_remote(g.at[k, b], g.at[k, b], send_sems, recv_sems, t, (x, y, 1 - c))
            for t, g in enumerate(bufs) for b in blocks for k in range(g.shape[0])]


def _blocks_moved(count):
    def plan(bufs, send_sems, recv_sems):
        x, y, c = _place()
        return [_remote(g.at[:, pl.ds(0, count)], g.at[:, pl.ds(0, count)], send_sems, recv_sems, t, (x, y, 1 - c))
                for t, g in enumerate(bufs)]
    return plan


def _pair_send(arrs, send_sems, recv_sems):
    x, y, c = _place()
    return [_remote(arrs[2 * t].at[p, k, 1 - c], arrs[2 * t + 1].at[p, k], send_sems, recv_sems, t, (x, y, 1 - c))
            for t in range(len(arrs) // 2) for p in range(arrs[2 * t].shape[0]) for k in range(N_CHIPS)]


def _chip_send(arrs, send_sems, recv_sems):
    x, y, c = _place()
    chips = [(1 - x, y), (x, 1 - y), (1 - x, 1 - y)]
    return [_remote(arrs[2 * t].at[p, 2 * px + py], arrs[2 * t + 1].at[j, p], send_sems, recv_sems, t, (px, py, c))
            for t in range(len(arrs) // 2) for j, (px, py) in enumerate(chips) for p in range(arrs[2 * t].shape[0])]


def _landed(arrs, send_sems, recv_sems):
    x, y, c = _place()
    return [_remote(arrs[2 * t + 1], arrs[2 * t + 1], send_sems, recv_sems, t, (x, y, 1 - c))
            for t in range(len(arrs) // 2)]


def _rows_per_step(rows, row_elems):
    best = 1
    for cand in range(1, rows + 1):
        if rows % cand == 0 and cand * row_elems <= 256 * 1024:
            best = cand
    return best


def _pair_sum(grad, landed, core, name):
    p, _, _, sz, c = grad.shape
    r = _rows_per_step(p * N_CHIPS, sz * c)

    def body(core_ref, g_ref, l_ref, o_ref):
        o_ref[...] = (g_ref[...].astype(F32) + l_ref[...].astype(F32)).astype(o_ref.dtype)

    out = pl.pallas_call(
        body, name=name,
        grid_spec=pltpu.PrefetchScalarGridSpec(
            num_scalar_prefetch=1, grid=(p * N_CHIPS // r,),
            in_specs=[pl.BlockSpec((r, None, sz, c), lambda i, cr: (i, cr[0], 0, 0)),
                      pl.BlockSpec((r, sz, c), lambda i, cr: (i, 0, 0))],
            out_specs=pl.BlockSpec((r, sz, c), lambda i, cr: (i, 0, 0))),
        out_shape=jax.ShapeDtypeStruct((p * N_CHIPS, sz, c), grad.dtype),
        compiler_params=_cparams(1),
    )(core, grad.reshape(p * N_CHIPS, 2, sz, c), landed.reshape(p * N_CHIPS, sz, c))
    return out.reshape(p, N_CHIPS, sz, c)


def _chip_sum(parts, landed, chip, name, deps=()):
    p, _, sz, c = parts.shape
    r = _rows_per_step(p, sz * c)

    def body(chip_ref, a_ref, l_ref, o_ref):
        acc = a_ref[...].astype(F32)
        for j in range(3):
            acc = acc + l_ref[j].astype(F32)
        o_ref[...] = acc

    nd = len(deps)

    def ordered(chip_ref, a_ref, l_ref, *rest):
        body(chip_ref, a_ref, l_ref, rest[nd])

    return pl.pallas_call(
        ordered, name=name,
        grid_spec=pltpu.PrefetchScalarGridSpec(
            num_scalar_prefetch=1, grid=(p // r,),
            in_specs=[pl.BlockSpec((r, None, sz, c), lambda i, cr: (i, cr[0], 0, 0)),
                      pl.BlockSpec((3, r, sz, c), lambda i, cr: (0, i, 0, 0))] + [ANY] * nd,
            out_specs=pl.BlockSpec((r, sz, c), lambda i, cr: (i, 0, 0))),
        out_shape=jax.ShapeDtypeStruct((p, sz, c), F32),
        compiler_params=_cparams(1),
    )(chip, parts, landed, *deps)


def _sum_lead(a, name, out_dtype=F32):
    k = a.shape[0]
    rest = a.shape[1:]
    r, c = rest[-2], rest[-1]
    lead = math.prod(rest[:-2])
    a3 = a.reshape(k, lead * r, c)
    rows = lead * r
    tb = rows
    for cand in (512, 256, 128, 64, 32, 16, 8):
        if rows % cand == 0 and rows > cand:
            tb = cand
            break

    def body(a_ref, o_ref):
        acc = a_ref[0].astype(F32)
        for i in range(1, k):
            acc = acc + a_ref[i].astype(F32)
        o_ref[...] = acc.astype(out_dtype)

    out = pl.pallas_call(
        body, name=name, grid=(rows // tb,),
        out_shape=jax.ShapeDtypeStruct((rows, c), out_dtype),
        in_specs=[pl.BlockSpec((k, tb, c), lambda i: (0, i, 0))],
        out_specs=pl.BlockSpec((tb, c), lambda i: (i, 0)),
        compiler_params=_cparams(1),
    )(a3)
    return out.reshape(rest)


def _bands(t, causal):
    r = lax.broadcasted_iota(jnp.int32, (t, t + HALO), 0)
    col = lax.broadcasted_iota(jnp.int32, (t, t + HALO), 1)
    diff = r + HALO - col if causal else col - r
    return jnp.stack([jnp.where((diff >= 0) & (diff < w), 1.0, 0.0) for w in POOL_WINDOWS]).astype(BF16)


def _split_dot(band, v):
    hi = v.astype(BF16)
    lo = (v - hi.astype(F32)).astype(BF16)
    return _nn(band, hi) + _nn(band, lo)


def _mix_fwd(x, g, wp, b, sc, name, deps=()):
    s = x.shape[0]
    t = min(MIX_ROWS, s)
    rb = t // HALO

    def body(x_ref, xh_ref, g_ref, wp_ref, b_ref, sc_ref, band_ref, xo_ref, d_ref):
        i = pl.program_id(0)
        gg = g_ref[...]
        h, _ = _rms(x_ref[...], gg, D_MODEL)
        hh, _ = _rms(xh_ref[...], gg, D_MODEL)
        hh = jnp.where(i > 0, hh, 0.0)
        hext = jnp.concatenate([hh, h], axis=0)
        tok = i * t + lax.broadcasted_iota(jnp.int32, (t, 1), 0)
        for gi, w in enumerate(POOL_WINDOWS):
            sl = slice(gi * GROUP_DIM, (gi + 1) * GROUP_DIM)
            win = _split_dot(band_ref[gi], hext[:, sl])
            inv = 1.0 / jnp.minimum(tok + 1, w).astype(F32)
            dbf = (win * inv - h[:, sl]).astype(BF16)
            d_ref[:, sl] = dbf
            ypre = _nn(dbf, wp_ref[gi]) + b_ref[:, sl]
            xo_ref[:, sl] = x_ref[:, sl] + ypre * sc_ref[:, sl]

    return _pcall(
        body, (x, x, g, wp, b, sc, _bands(t, True)), deps, name=name, grid=(s // t,),
        out_shape=[jax.ShapeDtypeStruct((s, D_MODEL), F32), jax.ShapeDtypeStruct((s, D_MODEL), BF16)],
        in_specs=[_rows(t, D_MODEL),
                  pl.BlockSpec((HALO, D_MODEL), lambda i: (jnp.maximum(i * rb - 1, 0), 0)),
                  _full((1, D_MODEL)), _full((4, GROUP_DIM, GROUP_DIM)), _full((1, D_MODEL)), _full((1, D_MODEL)),
                  _full((4, t, t + HALO))],
        out_specs=[_rows(t, D_MODEL), _rows(t, D_MODEL)],
        compiler_params=_cparams(1, VMEM_MID),
    )


def _mix_bwd(x, dy, d, g, wp, b, sc, name, deps=()):
    s = x.shape[0]
    t = min(MIX_ROWS, s)
    rb = t // HALO
    nb = s // t
    last_halo = s // HALO - 1

    def body(x_ref, dy_ref, dyn_ref, d_ref, g_ref, wp_ref, b_ref, sc_ref, band_ref,
             dx_ref, dyp_ref, dsc_ref, db_ref, dln_ref):
        i = pl.program_id(0)
        x = x_ref[...]
        gg = g_ref[...]
        dy = dy_ref[...]
        sc = sc_ref[...]
        dyp32 = dy * sc
        dyp = dyp32.astype(BF16)
        dyph = (dyn_ref[...] * sc).astype(BF16)
        dyp_ref[...] = dyp
        tok = i * t + lax.broadcasted_iota(jnp.int32, (t + HALO, 1), 0)
        dh, dsc = [], []
        for gi, w in enumerate(POOL_WINDOWS):
            sl = slice(gi * GROUP_DIM, (gi + 1) * GROUP_DIM)
            ypre = _nn(d_ref[:, sl], wp_ref[gi]) + b_ref[:, sl]
            dsc.append(jnp.sum(dy[:, sl] * ypre, axis=0, keepdims=True))
            dd = _nt(dyp[:, sl], wp_ref[gi])
            ddh = jnp.where(i < nb - 1, _nt(dyph[:, sl], wp_ref[gi]), 0.0)
            inv = 1.0 / jnp.minimum(tok + 1, w).astype(F32)
            ddext = jnp.concatenate([dd, ddh], axis=0) * inv
            dh.append(_split_dot(band_ref[gi], ddext) - dd)
        dh = jnp.concatenate(dh, axis=1)
        _, r = _rms(x, gg, D_MODEL)
        dxn, dg = _rms_bwd(x, r, gg, dh, D_MODEL)
        dx_ref[...] = dy + dxn

        @pl.when(i == 0)
        def _():
            dsc_ref[...] = jnp.zeros_like(dsc_ref)
            db_ref[...] = jnp.zeros_like(db_ref)
            dln_ref[...] = jnp.zeros_like(dln_ref)

        dsc_ref[...] += jnp.concatenate(dsc, axis=1)
        db_ref[...] += jnp.sum(dyp32, axis=0, keepdims=True)
        dln_ref[...] += dg

    vec = jax.ShapeDtypeStruct((1, D_MODEL), F32)
    return _pcall(
        body, (x, dy, dy, d, g, wp, b, sc, _bands(t, False)), deps, name=name, grid=(nb,),
        out_shape=[jax.ShapeDtypeStruct((s, D_MODEL), F32), jax.ShapeDtypeStruct((s, D_MODEL), BF16), vec, vec, vec],
        in_specs=[_rows(t, D_MODEL), _rows(t, D_MODEL),
                  pl.BlockSpec((HALO, D_MODEL), lambda i: (jnp.minimum((i + 1) * rb, last_halo), 0)),
                  _rows(t, D_MODEL),
                  _full((1, D_MODEL)), _full((4, GROUP_DIM, GROUP_DIM)), _full((1, D_MODEL)), _full((1, D_MODEL)),
                  _full((4, t, t + HALO))],
        out_specs=[_rows(t, D_MODEL), _rows(t, D_MODEL), _full((1, D_MODEL)), _full((1, D_MODEL)), _full((1, D_MODEL))],
        compiler_params=_cparams(1, VMEM_MID),
    )


def _load_weights(w_hbm, w_vmem, sem):
    @pl.when(pl.program_id(0) == 0)
    def _():
        cp = pltpu.make_async_copy(w_hbm, w_vmem, sem)
        cp.start()
        cp.wait()


def _ffn_fwd(x, g, w, name):
    s = x.shape[0]
    t = min(512, s)

    def body(x_ref, g_ref, w_hbm, xo_ref, gate_ref, up_ref, w_ref, sem):
        _load_weights(w_hbm, w_ref, sem)
        x = x_ref[...]
        hn = _rms(x, g_ref[...], D_MODEL)[0].astype(BF16)
        acc = x
        for c in range(2):
            rs = slice(c * FF_HALF, (c + 1) * FF_HALF)
            gt = _nt(hn, w_ref[0, rs, :])
            up = _nt(hn, w_ref[1, rs, :])
            gate_ref[:, rs] = gt.astype(BF16)
            up_ref[:, rs] = up.astype(BF16)
            act = ((gt * _sigmoid(gt)) * up).astype(BF16)
            acc = acc + _nn(act, w_ref[2, rs, :])
        xo_ref[...] = acc

    hid = jax.ShapeDtypeStruct((s, D_FF), BF16)
    return pl.pallas_call(
        body, name=name, grid=(s // t,),
        out_shape=[jax.ShapeDtypeStruct((s, D_MODEL), F32), hid, hid],
        in_specs=[_rows(t, D_MODEL), _full((1, D_MODEL)), ANY],
        out_specs=[_rows(t, D_MODEL), _rows(t, D_FF), _rows(t, D_FF)],
        scratch_shapes=[pltpu.VMEM((3, D_FF, D_MODEL), BF16), pltpu.SemaphoreType.DMA],
        compiler_params=_cparams(1, VMEM_BIG),
    )(x, g, w)


def _ffn_bwd(x, dy, gate, up, g, w, name, deps=()):
    s = x.shape[0]
    t = min(256, s)

    def body(x_ref, dy_ref, gate_ref, up_ref, g_ref, w_hbm,
             dx_ref, act_ref, dg_ref, du_ref, hn_ref, dyb_ref, dln_ref, w_ref, sem):
        _load_weights(w_hbm, w_ref, sem)
        x = x_ref[...]
        gg = g_ref[...]
        y, r = _rms(x, gg, D_MODEL)
        hn = y.astype(BF16)
        hn_ref[...] = hn
        dy = dy_ref[...]
        dyb = dy.astype(BF16)
        dyb_ref[...] = dyb
        dh = jnp.zeros((t, D_MODEL), F32)
        for c in range(2):
            rs = slice(c * FF_HALF, (c + 1) * FF_HALF)
            gt = gate_ref[:, rs].astype(F32)
            u = up_ref[:, rs].astype(F32)
            sg = _sigmoid(gt)
            sl = gt * sg
            act_ref[:, rs] = (sl * u).astype(BF16)
            dact = _nt(dyb, w_ref[2, rs, :])
            dg = (dact * u * (sg * (1.0 + gt * (1.0 - sg)))).astype(BF16)
            du = (dact * sl).astype(BF16)
            dg_ref[:, rs] = dg
            du_ref[:, rs] = du
            dh = dh + _nn(dg, w_ref[0, rs, :]) + _nn(du, w_ref[1, rs, :])
        dxn, dgl = _rms_bwd(x, r, gg, dh, D_MODEL)
        dx_ref[...] = dy + dxn

        @pl.when(pl.program_id(0) == 0)
        def _():
            dln_ref[...] = jnp.zeros_like(dln_ref)

        dln_ref[...] += dgl

    hid = jax.ShapeDtypeStruct((s, D_FF), BF16)
    tok = jax.ShapeDtypeStruct((s, D_MODEL), BF16)
    return _pcall(
        body, (x, dy, gate, up, g, w), deps, name=name, grid=(s // t,),
        out_shape=[jax.ShapeDtypeStruct((s, D_MODEL), F32), hid, hid, hid, tok, tok,
                   jax.ShapeDtypeStruct((1, D_MODEL), F32)],
        in_specs=[_rows(t, D_MODEL), _rows(t, D_MODEL), _rows(t, D_FF), _rows(t, D_FF), _full((1, D_MODEL)), ANY],
        out_specs=[_rows(t, D_MODEL), _rows(t, D_FF), _rows(t, D_FF), _rows(t, D_FF),
                   _rows(t, D_MODEL), _rows(t, D_MODEL), _full((1, D_MODEL))],
        scratch_shapes=[pltpu.VMEM((3, D_FF, D_MODEL), BF16), pltpu.SemaphoreType.DMA],
        compiler_params=_cparams(1, VMEM_BIG),
    )


def _tn_matmul(a, b, into, p0, name, groups=1, m_chunk=None, deps=()):
    s = a.shape[0]
    m, n = a.shape[1] // groups, b.shape[1] // groups
    assert into.shape[1:] == (m, n)
    mc = m if m_chunk is None else m_chunk
    nm = m // mc
    t = min(1024, s)
    nt = s // t

    def body(a_ref, b_ref, into_ref, o_ref, acc):
        ti = pl.program_id(2)

        @pl.when(ti == 0)
        def _():
            acc[...] = jnp.zeros_like(acc)

        acc[...] += _tn(a_ref[...], b_ref[...])

        @pl.when(ti == nt - 1)
        def _():
            o_ref[...] = acc[...].astype(o_ref.dtype)

    return _pcall(
        body, (a, b, into), deps, name=name, grid=(groups, nm, nt),
        out_shape=jax.ShapeDtypeStruct(into.shape, into.dtype),
        in_specs=[pl.BlockSpec((t, mc), lambda gi, mi, ti: (ti, gi * nm + mi)),
                  pl.BlockSpec((t, n), lambda gi, mi, ti: (ti, gi)), ANY],
        out_specs=pl.BlockSpec((None, mc, n), lambda gi, mi, ti: (p0 + gi, mi, 0)),
        scratch_shapes=[pltpu.VMEM((mc, n), F32)],
        input_output_aliases={2: 0},
        compiler_params=_cparams(3, VMEM_BIG),
    )


def _rope_tables(positions):
    half = ROPE // 2
    inv = ROPE_THETA ** (-jnp.arange(half, dtype=F32) * 2.0 / ROPE)
    ang = positions.astype(F32)[:, None] * inv
    cos, sin = jnp.cos(ang), jnp.sin(ang)
    zero = jnp.zeros((positions.shape[0], LANES - ROPE), F32)
    return jnp.concatenate([cos, cos, zero], axis=1), jnp.concatenate([-sin, sin, zero], axis=1)


def _kv_specs(t):
    return [_full((1, D_MODEL)), _full((D_MODEL, KV_RANK)), _full((D_MODEL, LANES)), _full((1, KV_RANK)),
            _full((KV_RANK, N_HEADS * NOPE)), _full((KV_RANK, N_HEADS * V_DIM)),
            _full((1, NOPE)), _full((1, LANES)), _rows(t, LANES), _rows(t, LANES)]


def _kv_fwd(x, ln, wc, wpe, gl, wuk, wuv, gkn, gkr, cos, sin, name, deps=()):
    s = x.shape[0]
    t = min(PROJ_ROWS, s)

    def body(x_ref, ln_ref, wc_ref, wpe_ref, gl_ref, wuk_ref, wuv_ref, gkn_ref, gkr_ref, cos_ref, sin_ref,
             k_ref, v_ref):
        hn = _rms(x_ref[...], ln_ref[...], D_MODEL)[0].astype(BF16)
        clat = _nn(hn, wc_ref[...])
        kpe = _nn(hn, wpe_ref[...])
        cn = _rms(clat, gl_ref[...], KV_RANK)[0].astype(BF16)
        sspe = jnp.sum(kpe * kpe, axis=-1, keepdims=True)
        base = kpe * gkr_ref[...]
        rot = base * cos_ref[...] + _swap_halves(base, _swap_perm()) * sin_ref[...]
        kn_all = _nn(cn, wuk_ref[...])
        v_ref[...] = _nn(cn, wuv_ref[...]).astype(BF16)
        for h in range(N_HEADS):
            kn = kn_all[:, h * NOPE:(h + 1) * NOPE]
            r = lax.rsqrt((jnp.sum(kn * kn, axis=-1, keepdims=True) + sspe) * (1.0 / QK_DIM) + EPS)
            k_ref[:, h * QK_PAD:h * QK_PAD + NOPE] = ((kn * r) * gkn_ref[...]).astype(BF16)
            k_ref[:, h * QK_PAD + NOPE:(h + 1) * QK_PAD] = (rot * r).astype(BF16)

    return _pcall(
        body, (x, ln, wc, wpe, gl, wuk, wuv, gkn, gkr, cos, sin), deps, name=name, grid=(s // t,),
        out_shape=[jax.ShapeDtypeStruct((s, N_HEADS * QK_PAD), BF16), jax.ShapeDtypeStruct((s, N_HEADS * V_DIM), BF16)],
        in_specs=[_rows(t, D_MODEL)] + _kv_specs(t),
        out_specs=[_rows(t, N_HEADS * QK_PAD), _rows(t, N_HEADS * V_DIM)],
        compiler_params=_cparams(1, VMEM_MID),
    )


def _kv_bwd(x, dxin, dks, dvs, ln, wc, wpe, gl, wuk, wuv, gkn, gkr, cos, sin, name):
    s = x.shape[0]
    t = min(PROJ_ROWS, s)
    nk = len(dks)

    def body(*refs):
        x_ref, dxin_ref = refs[:2]
        dk_refs = refs[2:2 + nk]
        dv_refs = refs[2 + nk:2 + 2 * nk]
        (ln_ref, wc_ref, wpe_ref, gl_ref, wuk_ref, wuv_ref, gkn_ref, gkr_ref, cos_ref, sin_ref,
         dx_ref, hn_ref, cn_ref, dkn_ref, dvb_ref, dcc_ref, dpe_ref,
         dln_ref, dgl_ref, dgkn_ref, dgkr_ref) = refs[2 + 2 * nk:]
        x = x_ref[...]
        ln = ln_ref[...]
        y, rx = _rms(x, ln, D_MODEL)
        hn = y.astype(BF16)
        hn_ref[...] = hn
        clat = _nn(hn, wc_ref[...])
        kpe = _nn(hn, wpe_ref[...])
        gl = gl_ref[...]
        cy, rc = _rms(clat, gl, KV_RANK)
        cn = cy.astype(BF16)
        cn_ref[...] = cn
        sspe = jnp.sum(kpe * kpe, axis=-1, keepdims=True)
        cs, sn, perm = cos_ref[...], sin_ref[...], _swap_perm()
        gkn, gkr = gkn_ref[...], gkr_ref[...]
        base = kpe * gkr
        rot = base * cs + _swap_halves(base, perm) * sn
        dkr_sum = jnp.zeros((t, LANES), F32)
        coef_sum = jnp.zeros((t, 1), F32)
        dgkn = jnp.zeros((1, NOPE), F32)
        kn_all = _nn(cn, wuk_ref[...])
        dkn_heads = []
        for h in range(N_HEADS):
            kn = kn_all[:, h * NOPE:(h + 1) * NOPE]
            r = lax.rsqrt((jnp.sum(kn * kn, axis=-1, keepdims=True) + sspe) * (1.0 / QK_DIM) + EPS)
            lo, mid, hi = h * QK_PAD, h * QK_PAD + NOPE, (h + 1) * QK_PAD
            dko = dk_refs[0][:, lo:mid]
            dkr = dk_refs[0][:, mid:hi]
            for j in range(1, nk):
                dko = dko + dk_refs[j][:, lo:mid]
                dkr = dkr + dk_refs[j][:, mid:hi]
            un = dko * gkn
            sm = (jnp.sum(kn * un, axis=-1, keepdims=True) + jnp.sum(rot * dkr, axis=-1, keepdims=True)) * (1.0 / QK_DIM)
            coef = r * r * r * sm
            dkn = (r * un - kn * coef).astype(BF16)
            dkr_sum = dkr_sum + r * dkr
            coef_sum = coef_sum + coef
            dgkn = dgkn + jnp.sum(dko * (kn * r), axis=0, keepdims=True)
            dkn_heads.append(dkn)
        dkn_all = jnp.concatenate(dkn_heads, axis=1)
        dkn_ref[...] = dkn_all
        dv_all = dv_refs[0][...]
        for j in range(1, nk):
            dv_all = dv_all + dv_refs[j][...]
        dvb = dv_all.astype(BF16)
        dvb_ref[...] = dvb
        dc = _nt(dkn_all, wuk_ref[...]) + _nt(dvb, wuv_ref[...])
        dz = dkr_sum * cs - _swap_halves(dkr_sum, perm) * sn
        dkpe = dz * gkr - kpe * coef_sum
        dgkr = jnp.sum(dz * kpe, axis=0, keepdims=True)
        dclat, dgl = _rms_bwd(clat, rc, gl, dc, KV_RANK)
        dcc = dclat.astype(BF16)
        dpe = dkpe.astype(BF16)
        dcc_ref[...] = dcc
        dpe_ref[...] = dpe
        dhn = _nt(dcc, wc_ref[...]) + _nt(dpe, wpe_ref[...])
        dxn, dln = _rms_bwd(x, rx, ln, dhn, D_MODEL)
        dx_ref[...] = dxin_ref[...] + dxn

        @pl.when(pl.program_id(0) == 0)
        def _():
            dln_ref[...] = jnp.zeros_like(dln_ref)
            dgl_ref[...] = jnp.zeros_like(dgl_ref)
            dgkn_ref[...] = jnp.zeros_like(dgkn_ref)
            dgkr_ref[...] = jnp.zeros_like(dgkr_ref)

        dln_ref[...] += dln
        dgl_ref[...] += dgl
        dgkn_ref[...] += dgkn
        dgkr_ref[...] += dgkr

    def tok(cols, dt):
        return jax.ShapeDtypeStruct((s, cols), dt)

    def vec(cols):
        return jax.ShapeDtypeStruct((1, cols), F32)

    return pl.pallas_call(
        body, name=name, grid=(s // t,),
        out_shape=[tok(D_MODEL, F32), tok(D_MODEL, BF16), tok(KV_RANK, BF16), tok(N_HEADS * NOPE, BF16),
                   tok(N_HEADS * V_DIM, BF16), tok(KV_RANK, BF16), tok(LANES, BF16),
                   vec(D_MODEL), vec(KV_RANK), vec(NOPE), vec(LANES)],
        in_specs=[_rows(t, D_MODEL), _rows(t, D_MODEL)] + [_rows(t, N_HEADS * QK_PAD)] * nk
                 + [_rows(t, N_HEADS * V_DIM)] * nk + _kv_specs(t),
        out_specs=[_rows(t, D_MODEL), _rows(t, D_MODEL), _rows(t, KV_RANK), _rows(t, N_HEADS * NOPE),
                   _rows(t, N_HEADS * V_DIM), _rows(t, KV_RANK), _rows(t, LANES),
                   _full((1, D_MODEL)), _full((1, KV_RANK)), _full((1, NOPE)), _full((1, LANES))],
        compiler_params=_cparams(1, VMEM_BIG),
    )(x, dxin, *dks, *dvs, ln, wc, wpe, gl, wuk, wuv, gkn, gkr, cos, sin)


def _q_specs(t):
    return [_full((1, D_MODEL)), _full((D_MODEL, Q_RANK)), _full((1, Q_RANK)), _full((N_HEADS, Q_RANK, QK_PAD)),
            _full((1, NOPE)), _full((1, LANES)), _rows(t, LANES), _rows(t, LANES)]


def _q_fwd(x, ln, wdq, gql, wuq, gqn, gqr, cos, sin, name, deps=()):
    s = x.shape[0]
    t = min(PROJ_ROWS, s)

    def body(x_ref, ln_ref, wdq_ref, gql_ref, wuq_ref, gqn_ref, gqr_ref, cos_ref, sin_ref, q_ref):
        hn = _rms(x_ref[...], ln_ref[...], D_MODEL)[0].astype(BF16)
        cqn = _rms(_nn(hn, wdq_ref[...]), gql_ref[...], Q_RANK)[0].astype(BF16)
        cs, sn, perm = cos_ref[...], sin_ref[...], _swap_perm()
        for h in range(N_HEADS):
            qa = _nn(cqn, wuq_ref[h])
            r = lax.rsqrt(jnp.sum(qa * qa, axis=-1, keepdims=True) * (1.0 / QK_DIM) + EPS)
            q_ref[:, h * QK_PAD:h * QK_PAD + NOPE] = ((qa[:, :NOPE] * r) * gqn_ref[...]).astype(BF16)
            z = (qa[:, NOPE:] * r) * gqr_ref[...]
            q_ref[:, h * QK_PAD + NOPE:(h + 1) * QK_PAD] = (z * cs + _swap_halves(z, perm) * sn).astype(BF16)

    return _pcall(
        body, (x, ln, wdq, gql, wuq, gqn, gqr, cos, sin), deps, name=name, grid=(s // t,),
        out_shape=jax.ShapeDtypeStruct((s, N_HEADS * QK_PAD), BF16),
        in_specs=[_rows(t, D_MODEL)] + _q_specs(t),
        out_specs=_rows(t, N_HEADS * QK_PAD),
        compiler_params=_cparams(1, VMEM_MID),
    )


def _q_bwd(x, dxin, dq, ln, wdq, gql, wuq, gqn, gqr, cos, sin, name):
    s = x.shape[0]
    t = min(PROJ_ROWS, s)

    def body(x_ref, dxin_ref, dq_ref, ln_ref, wdq_ref, gql_ref, wuq_ref, gqn_ref, gqr_ref, cos_ref, sin_ref,
             dx_ref, hn_ref, cqn_ref, dqa_ref, dcq_ref, dln_ref, dgql_ref, dgqn_ref, dgqr_ref):
        x = x_ref[...]
        ln = ln_ref[...]
        y, rx = _rms(x, ln, D_MODEL)
        hn = y.astype(BF16)
        hn_ref[...] = hn
        cqp = _nn(hn, wdq_ref[...])
        gql = gql_ref[...]
        cy, rc = _rms(cqp, gql, Q_RANK)
        cqn = cy.astype(BF16)
        cqn_ref[...] = cqn
        cs, sn, perm = cos_ref[...], sin_ref[...], _swap_perm()
        gqn, gqr = gqn_ref[...], gqr_ref[...]
        dcq = jnp.zeros((t, Q_RANK), F32)
        dgqn = jnp.zeros((1, NOPE), F32)
        dgqr = jnp.zeros((1, LANES), F32)
        for h in range(N_HEADS):
            qa = _nn(cqn, wuq_ref[h])
            qn, qr = qa[:, :NOPE], qa[:, NOPE:]
            r = lax.rsqrt(jnp.sum(qa * qa, axis=-1, keepdims=True) * (1.0 / QK_DIM) + EPS)
            dqo = dq_ref[:, h * QK_PAD:h * QK_PAD + NOPE]
            dqr = dq_ref[:, h * QK_PAD + NOPE:(h + 1) * QK_PAD]
            dz = dqr * cs - _swap_halves(dqr, perm) * sn
            un = dqo * gqn
            ur = dz * gqr
            sm = (jnp.sum(qn * un, axis=-1, keepdims=True) + jnp.sum(qr * ur, axis=-1, keepdims=True)) * (1.0 / QK_DIM)
            coef = r * r * r * sm
            dqa = jnp.concatenate([r * un - qn * coef, r * ur - qr * coef], axis=1).astype(BF16)
            dgqn = dgqn + jnp.sum(dqo * (qn * r), axis=0, keepdims=True)
            dgqr = dgqr + jnp.sum(dz * (qr * r), axis=0, keepdims=True)
            dqa_ref[:, h * QK_PAD:(h + 1) * QK_PAD] = dqa
            dcq = dcq + _nt(dqa, wuq_ref[h])
        dcqp, dgql = _rms_bwd(cqp, rc, gql, dcq, Q_RANK)
        dcqb = dcqp.astype(BF16)
        dcq_ref[...] = dcqb
        dhn = _nt(dcqb, wdq_ref[...])
        dxn, dln = _rms_bwd(x, rx, ln, dhn, D_MODEL)
        dx_ref[...] = dxin_ref[...] + dxn

        @pl.when(pl.program_id(0) == 0)
        def _():
            dln_ref[...] = jnp.zeros_like(dln_ref)
            dgql_ref[...] = jnp.zeros_like(dgql_ref)
            dgqn_ref[...] = jnp.zeros_like(dgqn_ref)
            dgqr_ref[...] = jnp.zeros_like(dgqr_ref)

        dln_ref[...] += dln
        dgql_ref[...] += dgql
        dgqn_ref[...] += dgqn
        dgqr_ref[...] += dgqr

    def tok(cols, dt):
        return jax.ShapeDtypeStruct((s, cols), dt)

    def vec(cols):
        return jax.ShapeDtypeStruct((1, cols), F32)

    return pl.pallas_call(
        body, name=name, grid=(s // t,),
        out_shape=[tok(D_MODEL, F32), tok(D_MODEL, BF16), tok(Q_RANK, BF16), tok(N_HEADS * QK_PAD, BF16),
                   tok(Q_RANK, BF16), vec(D_MODEL), vec(Q_RANK), vec(NOPE), vec(LANES)],
        in_specs=[_rows(t, D_MODEL), _rows(t, D_MODEL), _rows(t, N_HEADS * QK_PAD)] + _q_specs(t),
        out_specs=[_rows(t, D_MODEL), _rows(t, D_MODEL), _rows(t, Q_RANK), _rows(t, N_HEADS * QK_PAD),
                   _rows(t, Q_RANK), _full((1, D_MODEL)), _full((1, Q_RANK)), _full((1, NOPE)), _full((1, LANES))],
        compiler_params=_cparams(1, VMEM_MID),
    )(x, dxin, dq, ln, wdq, gql, wuq, gqn, gqr, cos, sin)


SM_SCALE = 1.0 / math.sqrt(QK_DIM)
LOG2_E = math.log2(math.e)
EXP2_SCALE = SM_SCALE * LOG2_E
NEG = -1e30


def _diag_mask(t):
    qpos = lax.broadcasted_iota(jnp.int32, (t, t), 0)
    kpos = lax.broadcasted_iota(jnp.int32, (t, t), 1)
    return lax.shift_right_logical(kpos, 6) <= lax.shift_right_logical(qpos, 6)


def _att_fwd(q, k, v, name):
    s = q.shape[0]
    t = min(512, s)
    nb = s // t

    def body(q_ref, k_ref, v_ref, o_ref, lse_ref):
        qi = pl.program_id(1)
        qq = q_ref[...]

        def block(ki, carry, masked):
            m_old, l_old, acc = carry
            rows = pl.ds(pl.multiple_of(ki * t, t), t)
            sc = _nt(qq, k_ref[rows, :])
            if masked:
                sc = jnp.where(_diag_mask(t), sc, NEG)
            m_new = jnp.maximum(m_old, jnp.max(sc, axis=-1, keepdims=True))
            p = jnp.exp2((sc - m_new) * EXP2_SCALE)
            alpha = jnp.exp2((m_old - m_new) * EXP2_SCALE)
            l_new = alpha * l_old + jnp.sum(p, axis=-1, keepdims=True)
            acc = alpha * acc + _nn(p.astype(BF16), v_ref[rows, :])
            return m_new, l_new, acc

        init = (jnp.full((t, 1), NEG, F32), jnp.zeros((t, 1), F32), jnp.zeros((t, V_DIM), F32))
        def pair(k0, c):
            return block(k0 + 1, block(k0, c, False), False)

        carry = lax.fori_loop(0, qi // 4, lambda j, c: pair(4 * j + 2, pair(4 * j, c)), init)
        done = 4 * (qi // 4)
        carry = lax.cond((qi & 2) != 0, lambda c: pair(done, c), lambda c: c, carry)
        carry = lax.cond((qi & 1) != 0, lambda c: block(qi - 1, c, False), lambda c: c, carry)
        m_fin, l_fin, acc = block(qi, carry, True)
        o_ref[...] = (acc / l_fin).astype(BF16)
        lse_ref[...] = jnp.broadcast_to(m_fin * SM_SCALE + jnp.log(l_fin), (t, LANES))

    return pl.pallas_call(
        body, name=name, grid=(N_HEADS, nb),
        out_shape=[jax.ShapeDtypeStruct((s, N_HEADS * V_DIM), BF16), jax.ShapeDtypeStruct((s, N_HEADS * LANES), F32)],
        in_specs=[pl.BlockSpec((t, QK_PAD), lambda h, qi: (qi, h)),
                  pl.BlockSpec((s, QK_PAD), lambda h, qi: (0, h)),
                  pl.BlockSpec((s, V_DIM), lambda h, qi: (0, h))],
        out_specs=[pl.BlockSpec((t, V_DIM), lambda h, qi: (qi, h)),
                   pl.BlockSpec((t, LANES), lambda h, qi: (qi, h))],
        compiler_params=_cparams(2, VMEM_MID),
    )(q, k, v)


def _att_bwd(q, k, v, do, stats, name, deps=()):
    s = q.shape[0]
    t = min(512, s)
    nb = s // t

    def body(q_ref, k_ref, v_ref, do_ref, st_ref, dq_ref, dk_ref, dv_ref):
        ki = pl.program_id(1)
        kk, vv = k_ref[...], v_ref[...]

        @pl.when(ki == 0)
        def _():
            dq_ref[...] = jnp.zeros_like(dq_ref)

        def block(qi, carry, masked):
            dk, dv = carry
            rows = pl.ds(pl.multiple_of(qi * t, t), t)
            qq, dob = q_ref[rows, :], do_ref[rows, :]
            sc = _nt(qq, kk)
            if masked:
                sc = jnp.where(_diag_mask(t), sc, NEG)
            st = st_ref[rows, :]
            p = jnp.exp2(sc * EXP2_SCALE - st[:, 0:1])
            dp = _nt(dob, vv)
            ds = (p * (dp - st[:, 1:2])).astype(BF16)
            dq_ref[rows, :] += _nn(ds, kk)
            return dk + _tn(ds, qq), dv + _tn(p.astype(BF16), dob)

        carry = block(ki, (jnp.zeros((t, QK_PAD), F32), jnp.zeros((t, V_DIM), F32)), True)
        rest = nb - 1 - ki
        carry = lax.fori_loop(
            0, rest // 2, lambda j, c: block(ki + 2 * j + 2, block(ki + 2 * j + 1, c, False), False), carry)
        dk, dv = lax.cond(rest % 2 == 1, lambda c: block(nb - 1, c, False), lambda c: c, carry)
        dk_ref[...] = dk * SM_SCALE
        dv_ref[...] = dv

        @pl.when(ki == nb - 1)
        def _():
            dq_ref[...] = dq_ref[...] * SM_SCALE

    def head(h, ki):
        return (0, h)

    def kblock(h, ki):
        return (ki, h)

    return _pcall(
        body, (q, k, v, do, stats), deps, name=name, grid=(N_HEADS, nb),
        out_shape=[jax.ShapeDtypeStruct((s, N_HEADS * QK_PAD), F32), jax.ShapeDtypeStruct((s, N_HEADS * QK_PAD), F32),
                   jax.ShapeDtypeStruct((s, N_HEADS * V_DIM), F32)],
        in_specs=[pl.BlockSpec((s, QK_PAD), head), pl.BlockSpec((t, QK_PAD), kblock), pl.BlockSpec((t, V_DIM), kblock),
                  pl.BlockSpec((s, V_DIM), head), pl.BlockSpec((s, LANES), head)],
        out_specs=[pl.BlockSpec((s, QK_PAD), head), pl.BlockSpec((t, QK_PAD), kblock), pl.BlockSpec((t, V_DIM), kblock)],
        compiler_params=_cparams(2, VMEM_MID),
    )


def _o_fwd(x, o, wo, name):
    s = x.shape[0]
    t = min(512, s)

    def body(x_ref, o_ref, wo_ref, xo_ref):
        xo_ref[...] = x_ref[...] + _nn(o_ref[...], wo_ref[...])

    return pl.pallas_call(
        body, name=name, grid=(s // t,),
        out_shape=jax.ShapeDtypeStruct((s, D_MODEL), F32),
        in_specs=[_rows(t, D_MODEL), _rows(t, D_MODEL), _full((D_MODEL, D_MODEL))],
        out_specs=_rows(t, D_MODEL),
        compiler_params=_cparams(1, VMEM_MID),
    )(x, o, wo)


def _o_bwd(dx, wo, o, lse, name, deps=()):
    s = dx.shape[0]
    t = min(512, s)

    def body(dx_ref, wo_ref, o_ref, lse_ref, do_ref, dxb_ref, st_ref):
        dxb = dx_ref[...].astype(BF16)
        dxb_ref[...] = dxb
        dob = _nt(dxb, wo_ref[...]).astype(BF16)
        do_ref[...] = dob
        lane = lax.broadcasted_iota(jnp.int32, (t, LANES), 1)
        for h in range(N_HEADS):
            sl = slice(h * V_DIM, (h + 1) * V_DIM)
            dsum = jnp.sum(dob[:, sl].astype(F32) * o_ref[:, sl].astype(F32), axis=-1, keepdims=True)
            st_ref[:, sl] = jnp.where(lane == 0, lse_ref[:, sl] * LOG2_E, jnp.where(lane == 1, dsum, 0.0))

    tok = jax.ShapeDtypeStruct((s, D_MODEL), BF16)
    return _pcall(
        body, (dx, wo, o, lse), deps, name=name, grid=(s // t,),
        out_shape=[tok, tok, jax.ShapeDtypeStruct((s, N_HEADS * LANES), F32)],
        in_specs=[_rows(t, D_MODEL), _full((D_MODEL, D_MODEL)), _rows(t, D_MODEL), _rows(t, N_HEADS * LANES)],
        out_specs=[_rows(t, D_MODEL), _rows(t, D_MODEL), _rows(t, N_HEADS * LANES)],
        compiler_params=_cparams(1, VMEM_MID),
    )


def _loss_head(y, target, name):
    s = y.shape[0]
    t = min(512, s)

    def body(y_ref, t_ref, dy_ref, sq_ref):
        e = y_ref[...] - t_ref[...]
        dy_ref[...] = e * (1.0 / D_MODEL)

        @pl.when(pl.program_id(0) == 0)
        def _():
            sq_ref[...] = jnp.zeros_like(sq_ref)

        sq_ref[...] += jnp.sum(e * e, axis=0, keepdims=True)

    return pl.pallas_call(
        body, name=name, grid=(s // t,),
        out_shape=[jax.ShapeDtypeStruct((s, D_MODEL), F32), jax.ShapeDtypeStruct((1, D_MODEL), F32)],
        in_specs=[_rows(t, D_MODEL), _rows(t, D_MODEL)],
        out_specs=[_rows(t, D_MODEL), _full((1, D_MODEL))],
        compiler_params=_cparams(1),
    )(y, target)


def _adamw(w, g, m, v, name):
    shape = w.shape
    c = shape[-1]
    r = math.prod(shape[:-1])
    tb = r
    for cand in (512, 256, 128):
        if r % cand == 0 and r > cand:
            tb = cand
            break

    def body(w_ref, g_ref, m_ref, v_ref, d_ref, mo_ref, vo_ref):
        gr = g_ref[...]
        mn = ADAM_B1 * m_ref[...] + (1.0 - ADAM_B1) * gr
        vn = ADAM_B2 * v_ref[...] + (1.0 - ADAM_B2) * (gr * gr)
        m_hat = mn / (1.0 - ADAM_B1 ** ADAM_STEP)
        v_hat = vn / (1.0 - ADAM_B2 ** ADAM_STEP)
        d_ref[...] = -ADAM_LR * (m_hat / (jnp.sqrt(v_hat) + ADAM_EPS) + ADAM_WD * w_ref[...])
        mo_ref[...] = mn
        vo_ref[...] = vn

    spec = pl.BlockSpec((tb, c), lambda i: (i, 0))
    flat = jax.ShapeDtypeStruct((r, c), F32)
    outs = pl.pallas_call(
        body, name=name, grid=(r // tb,),
        out_shape=[flat, flat, flat],
        in_specs=[spec] * 4, out_specs=[spec] * 3,
        compiler_params=_cparams(1),
    )(w.reshape(r, c), g.reshape(r, c), m.reshape(r, c), v.reshape(r, c))
    return [a.reshape(shape) for a in outs]


def _pad_cols(a, width):
    return jnp.pad(a, [(0, 0)] * (a.ndim - 1) + [(0, width - a.shape[-1])])


def _owner_view(a, sz):
    return a.reshape(a.shape[0], N_CHIPS, 2, sz, a.shape[-1])


def kernel(x, positions, ln_mix_a, w_pool, b_pool, pool_scale, ln_ffn, w_gate, w_up, w_down, ln_kv, w_dkv, g_kv_latent, w_uk, w_uv, g_k, ln_mix_b, w_dq, g_q_latent, w_uq, g_q, w_o, loss_target, m_ln_mix_a, m_w_pool, m_b_pool, m_pool_scale, m_ln_ffn, m_w_gate, m_w_up, m_w_down, m_ln_kv, m_w_dkv, m_g_kv_latent, m_w_uk, m_w_uv, m_g_k, m_ln_mix_b, m_w_dq, m_g_q_latent, m_w_uq, m_g_q, m_w_o, v_ln_mix_a, v_w_pool, v_b_pool, v_pool_scale, v_ln_ffn, v_w_gate, v_w_up, v_w_down, v_ln_kv, v_w_dkv, v_g_kv_latent, v_w_uk, v_w_uv, v_g_k, v_ln_mix_b, v_w_dq, v_g_q_latent, v_w_uq, v_g_q, v_w_o):
    weights = dict(ln_mix_a=ln_mix_a, w_pool=w_pool, b_pool=b_pool, pool_scale=pool_scale, ln_ffn=ln_ffn,
                   w_gate=w_gate, w_up=w_up, w_down=w_down, ln_kv=ln_kv, w_dkv=w_dkv, g_kv_latent=g_kv_latent,
                   w_uk=w_uk, w_uv=w_uv, g_k=g_k, ln_mix_b=ln_mix_b, w_dq=w_dq, g_q_latent=g_q_latent,
                   w_uq=w_uq, g_q=g_q, w_o=w_o)
    mom1 = dict(ln_mix_a=m_ln_mix_a, w_pool=m_w_pool, b_pool=m_b_pool, pool_scale=m_pool_scale, ln_ffn=m_ln_ffn,
                w_gate=m_w_gate, w_up=m_w_up, w_down=m_w_down, ln_kv=m_ln_kv, w_dkv=m_w_dkv,
                g_kv_latent=m_g_kv_latent, w_uk=m_w_uk, w_uv=m_w_uv, g_k=m_g_k, ln_mix_b=m_ln_mix_b, w_dq=m_w_dq,
                g_q_latent=m_g_q_latent, w_uq=m_w_uq, g_q=m_g_q, w_o=m_w_o)
    mom2 = dict(ln_mix_a=v_ln_mix_a, w_pool=v_w_pool, b_pool=v_b_pool, pool_scale=v_pool_scale, ln_ffn=v_ln_ffn,
                w_gate=v_w_gate, w_up=v_w_up, w_down=v_w_down, ln_kv=v_ln_kv, w_dkv=v_w_dkv,
                g_kv_latent=v_g_kv_latent, w_uk=v_w_uk, w_uv=v_w_uv, g_k=v_g_k, ln_mix_b=v_ln_mix_b, w_dq=v_w_dq,
                g_q_latent=v_g_q_latent, w_uq=v_w_uq, g_q=v_g_q, w_o=v_w_o)
    names = list(weights)
    dev = 4 * lax.axis_index("x") + 2 * lax.axis_index("y") + lax.axis_index("c")
    core = lax.axis_index("c").astype(jnp.int32).reshape(1)
    chip = (2 * lax.axis_index("x") + lax.axis_index("y")).astype(jnp.int32).reshape(1)

    xs = x[0]
    target = loss_target[0]
    cos, sin = _rope_tables(positions[0])

    def placed(shard):
        buf = lax.empty((shard.shape[0], N_DEV) + shard.shape[1:], shard.dtype)
        return lax.dynamic_update_slice(buf, shard[:, None], (0, dev, 0, 0))

    def ffn_shard(l):
        return jnp.stack([w_gate[l].T, w_up[l].T, w_down[l]]).astype(BF16)

    groups = {"ffn0": [placed(ffn_shard(0))]}
    small_sh = jnp.concatenate([ln_mix_a.reshape(1, -1), pool_scale.reshape(1, -1), b_pool.reshape(1, -1)], axis=1)
    wp_g, small_g = _all_gather([w_pool.astype(BF16), small_sh], [2, 0], "gather_first")
    wp_all = wp_g.reshape(2, 4, GROUP_DIM, GROUP_DIM)
    small_g = small_g.reshape(N_DEV, 3, 2, LANES)
    ln_a_all = small_g[:, 0].transpose(1, 0, 2).reshape(2, 1, D_MODEL)
    sc_all = small_g[:, 1].transpose(1, 0, 2).reshape(2, 1, D_MODEL)
    bp_all = small_g[:, 2].reshape(N_DEV, 2, 4, 32).transpose(1, 2, 0, 3).reshape(2, 1, D_MODEL)
    sp0 = _copies_start(groups["ffn0"], 1, _gather_spread, "spread_ffn0", deps=[small_g])
    zero = sp0[3][0, 0].astype(BF16)
    for l in (1, 2, 3):
        groups[f"ffn{l}"] = [placed(ffn_shard(l) + zero)]
    groups["att"] = [placed(a.astype(BF16) + zero) for a in (
        w_dkv[None, :, :KV_RANK], _pad_cols(w_dkv[None, :, KV_RANK:], LANES), w_uk[None], w_uv[None],
        w_dq, _pad_cols(w_uq, QK_PAD), w_o)]

    def spread_start(nm, deps):
        return _copies_start(groups[nm], len(groups[nm]), _gather_spread, f"spread_{nm}", deps=deps)

    def spread_wait(nm, state, after):
        ssem, rsem, bufs, _ = state
        return _copies_wait(bufs, ssem, rsem, after, _blocks_moved(4), f"spread_done_{nm}")

    def relay_start(nm, bufs, deps=()):
        return _copies_start(bufs, len(bufs), _gather_relay, f"relay_{nm}", deps=deps)

    def relay_wait(nm, state, after):
        ssem, rsem, bufs, _ = state
        return _copies_wait(bufs, ssem, rsem, after, _blocks_moved(3), f"relay_done_{nm}")

    gkn = g_k[:NOPE].reshape(1, NOPE)
    gkr = _pad_cols(g_k[NOPE:].reshape(1, ROPE), LANES)
    gl = g_kv_latent.reshape(1, KV_RANK)
    lnkv = ln_kv.reshape(1, D_MODEL)

    x_in, x_mid, pooled, gates, ups, w_ffn = [], [], [], [], [], []
    qs, outs, lses = [], [], []

    def mixer(l, cur, deps):
        x_in.append(cur)
        mid, dsave = _mix_fwd(cur, ln_a_all[l], wp_all[l], bp_all[l], sc_all[l], f"mix_fwd{l}", deps=deps)
        pooled.append(dsave)
        x_mid.append(mid)
        return mid

    def q_args(j):
        return (ln_mix_b[j].reshape(1, -1), wdq_all[j], g_q_latent[j].reshape(1, -1), wuq_all[j],
                g_q[j, :NOPE].reshape(1, -1), _pad_cols(g_q[j, NOPE:].reshape(1, -1), LANES), cos, sin)

    def attention(j, cur, deps):
        x_in.append(cur)
        q = _q_fwd(cur, *q_args(j), f"q_fwd{j}", deps=deps)
        o, lse = _att_fwd(q, k_sh, v_sh, f"att_fwd{j}")
        mid = _o_fwd(cur, o, wo_all[j], f"o_fwd{j}")
        qs.append(q)
        outs.append(o)
        lses.append(lse)
        x_mid.append(mid)
        return mid

    def ffn(l, mid, relayed):
        w_l = relayed[0].reshape(3, D_FF, D_MODEL)
        w_ffn.append(w_l)
        cur, gate, up = _ffn_fwd(mid, ln_ffn[l].reshape(1, -1), w_l, f"ffn_fwd{l}")
        gates.append(gate)
        ups.append(up)
        return cur

    mid = mixer(0, xs, [sp0[3]])
    prepared = [buf for nm in ("ffn1", "att", "ffn2", "ffn3") for buf in groups[nm]]
    landed0 = spread_wait("ffn0", sp0, [mid] + prepared)
    sp1 = spread_start("ffn1", [landed0[0]])
    rl0 = relay_start("ffn0", landed0, [sp1[3]])
    cur = ffn(0, mid, relay_wait("ffn0", rl0, rl0[3]))

    landed1 = spread_wait("ffn1", sp1, cur)
    sp_att = spread_start("att", [landed1[0]])
    sp2 = spread_start("ffn2", [landed1[0]])
    rl1 = relay_start("ffn1", landed1, [sp_att[3], sp2[3]])
    mid = mixer(1, cur, [rl1[3]])
    cur = ffn(1, mid, relay_wait("ffn1", rl1, mid))
    x_kv = cur

    landed_att = spread_wait("att", sp_att, cur)
    landed2 = spread_wait("ffn2", sp2, cur)
    sp3 = spread_start("ffn3", [landed2[0]])
    rl_att = relay_start("att", landed_att, [sp3[3]])
    rl2 = relay_start("ffn2", landed2, [sp3[3]])
    att_bufs = relay_wait("att", rl_att, rl2[3])
    wc = att_bufs[0].reshape(D_MODEL, KV_RANK)
    wpe = att_bufs[1].reshape(D_MODEL, LANES)
    wuk_g = att_bufs[2].reshape(N_HEADS, KV_RANK, NOPE).transpose(1, 0, 2).reshape(KV_RANK, N_HEADS * NOPE)
    wuv_g = att_bufs[3].reshape(N_HEADS, KV_RANK, V_DIM).transpose(1, 0, 2).reshape(KV_RANK, N_HEADS * V_DIM)
    wdq_all = att_bufs[4].reshape(2, D_MODEL, Q_RANK)
    wuq_all = att_bufs[5]
    wo_all = att_bufs[6].reshape(2, D_MODEL, D_MODEL)
    k_sh, v_sh = _kv_fwd(cur, lnkv, wc, wpe, gl, wuk_g, wuv_g, gkn, gkr, cos, sin, "kv_fwd")
    mid = attention(0, cur, [])
    cur = ffn(2, mid, relay_wait("ffn2", rl2, mid))

    landed3 = spread_wait("ffn3", sp3, cur)
    rl3 = relay_start("ffn3", landed3)
    mid = attention(1, cur, [rl3[3]])
    cur = ffn(3, mid, relay_wait("ffn3", rl3, mid))

    dx, sq_cols = _loss_head(cur, target, "loss_head")

    small = {}
    sizes = dict(ffn0=FF_SHARD, ffn1=FF_SHARD, ffn2=FF_SHARD, ffn3=FF_SHARD, wo=128, kv512=128, dkv_pe=128,
                 wdq=128, wuqT=QK_PAD, wpool=32)
    big = dict(wo=lax.empty((2, D_MODEL, D_MODEL), BF16), kv512=lax.empty((3, D_MODEL, KV_RANK), BF16),
               dkv_pe=lax.empty((1, D_MODEL, LANES), BF16), wdq=lax.empty((2, D_MODEL, Q_RANK), BF16),
               wuqT=lax.empty((2, N_HEADS * QK_PAD, Q_RANK), BF16), wpool=lax.empty((8, GROUP_DIM, GROUP_DIM), BF16))
    for l in range(4):
        big[f"ffn{l}"] = lax.empty((3, D_FF, D_MODEL), BF16)
    red = {}

    def pair_start(nms, tag):
        arrs = []
        for nm in nms:
            view = _owner_view(big[nm], sizes[nm])
            arrs += [view, lax.empty((view.shape[0], N_CHIPS) + view.shape[3:], BF16)]
        return nms, tag, _copies_start(arrs, len(nms), _pair_send, f"pair_start_{tag}")

    def chip_start(state, after):
        nms, tag, (ssem, rsem, arrs, _) = state
        arrs = _copies_wait(arrs, ssem, rsem, after, _landed, f"pair_done_{tag}")
        out = []
        for t, nm in enumerate(nms):
            part = _pair_sum(arrs[2 * t], arrs[2 * t + 1], core, f"pair_sum_{nm}")
            out += [part, lax.empty((3, part.shape[0]) + part.shape[2:], BF16)]
        return nms, tag, _copies_start(out, len(nms), _chip_send, f"chip_start_{tag}")

    deferred = []
    updates = {}

    def chip_finish(state, after, defer=False):
        nms, tag, (ssem, rsem, arrs, _) = state
        arrs = _copies_wait(arrs, ssem, rsem, after, _landed, f"chip_done_{tag}")
        for t, nm in enumerate(nms):
            if defer:
                deferred.append((nm, arrs[2 * t], arrs[2 * t + 1]))
            else:
                red[nm] = _chip_sum(arrs[2 * t], arrs[2 * t + 1], chip, f"chip_sum_{nm}")

    ffn_grads = {nm: lax.empty((4, FF_SHARD, D_MODEL), F32) for nm in ("w_gate", "w_up", "w_down")}

    def place_ffn_grads(l):
        g = red[f"ffn{l}"]
        for k, nm in enumerate(("w_gate", "w_up", "w_down")):
            ffn_grads[nm] = ffn_grads[nm].at[l].set(g[k])

    dks, dvs = [], []
    pending = None
    bwd_deps = []
    for l in (3, 2, 1, 0):
        key = f"ffn{l}"
        dx, act, dgb, dub, hn, dyb, dln = _ffn_bwd(x_mid[l], dx, gates[l], ups[l], ln_ffn[l].reshape(1, -1),
                                                     w_ffn[l], f"ffn_bwd{l}", deps=bwd_deps)
        bwd_deps = []
        small[f"ln_ffn{l}"] = dln
        if l == 1:
            att_chip = chip_start(att_pair, dx)
            tn_deps = [att_chip[2][3]]
        else:
            tn_deps = []
        if pending:
            chip_finish(pending, dx, defer=True)
            pending = None
        big[key] = _tn_matmul(dgb, hn, big[key], 0, f"dw_gate{l}", m_chunk=FF_HALF, deps=tn_deps)
        big[key] = _tn_matmul(dub, hn, big[key], 1, f"dw_up{l}", m_chunk=FF_HALF)
        big[key] = _tn_matmul(act, dyb, big[key], 2, f"dw_down{l}", m_chunk=FF_HALF)
        if l == 1:
            chip_finish(att_chip, big[key], defer=True)
        ffn_pair = pair_start([key], key)
        if l >= 2:
            j = l - 2
            do, dxb, stats = _o_bwd(dx, wo_all[j], outs[j], lses[j], f"o_bwd{j}", deps=[ffn_pair[2][3]])
            big["wo"] = _tn_matmul(outs[j], dxb, big["wo"], j, f"dw_o{j}")
            ffn_chip = chip_start(ffn_pair, big["wo"])
            dq, dk, dv = _att_bwd(qs[j], k_sh, v_sh, do, stats, f"att_bwd{j}", deps=[ffn_chip[2][3]])
            chip_finish(ffn_chip, dq, defer=True)
            dks.append(dk)
            dvs.append(dv)
            dx, hnq, cqn, dqa, dcq, dln, dgql, dgqn, dgqr = _q_bwd(x_in[l], dx, dq, *q_args(j), f"q_bwd{j}")
            small[f"ln_mix_b{j}"] = dln
            small[f"g_q_latent{j}"] = dgql
            small[f"g_q{j}"] = jnp.concatenate([dgqn, dgqr[:, :ROPE]], axis=1)
            big["wdq"] = _tn_matmul(hnq, dcq, big["wdq"], j, f"dw_dq{j}")
            big["wuqT"] = _tn_matmul(dqa, cqn, big["wuqT"], j, f"dw_uq{j}")
            if l == 2:
                (dx, hnk, cn, dknb, dvb, dccb, dpeb, dlnkv, dgl, dgkn, dgkr) = _kv_bwd(
                    x_kv, dx, dks, dvs, lnkv, wc, wpe, gl, wuk_g, wuv_g, gkn, gkr, cos, sin, "kv_bwd")
                small["ln_kv"] = dlnkv
                small["g_kv_latent"] = dgl
                small["g_k"] = jnp.concatenate([dgkn, dgkr[:, :ROPE]], axis=1)
                big["kv512"] = _tn_matmul(dknb, cn, big["kv512"], 0, "dw_uk")
                big["kv512"] = _tn_matmul(dvb, cn, big["kv512"], 1, "dw_uv")
                big["kv512"] = _tn_matmul(hnk, dccb, big["kv512"], 2, "dw_dkv_c")
                big["dkv_pe"] = _tn_matmul(hnk, dpeb, big["dkv_pe"], 0, "dw_dkv_pe")
                att_pair = pair_start(["wo", "kv512", "dkv_pe", "wdq", "wuqT"], "att")
                bwd_deps = [att_pair[2][3]]
        else:
            dx, dyp, dsc, db, dln = _mix_bwd(x_in[l], dx, pooled[l], ln_a_all[l], wp_all[l], bp_all[l], sc_all[l],
                                             f"mix_bwd{l}", deps=[ffn_pair[2][3]])
            small[f"ln_mix_a{l}"] = dln
            small[f"pool_scale{l}"] = dsc
            small[f"b_pool{l}"] = db
            ffn_chip = chip_start(ffn_pair, dx)
            big["wpool"] = _tn_matmul(pooled[l], dyp, big["wpool"], 4 * l, f"dw_pool{l}", groups=4,
                                      deps=[ffn_chip[2][3]])
            if l == 1:
                pending = ffn_chip
                bwd_deps = [ffn_chip[2][3]]
            else:
                for nm, part, land in deferred:
                    red[nm] = _chip_sum(part, land, chip, f"chip_sum_{nm}", deps=[ffn_chip[2][3]])
                    if nm.startswith("ffn"):
                        place_ffn_grads(int(nm[-1]))
                early_grads = dict(
                    w_dkv=jnp.concatenate([red["kv512"][2], red["dkv_pe"][0][:, :ROPE]], axis=1),
                    w_uk=red["kv512"][0].T, w_uv=red["kv512"][1].T, w_dq=red["wdq"],
                    w_uq=red["wuqT"].transpose(0, 2, 1)[:, :, :QK_DIM], w_o=red["wo"])
                for nm, g in early_grads.items():
                    updates[nm] = _adamw(weights[nm], g, mom1[nm], mom2[nm], f"adamw_{nm}")
                pool_pair = pair_start(["wpool"], "wpool")
                filler = list(ffn_grads.values()) + [u[0] for u in updates.values()]
                chip_finish(ffn_chip, filler + [pool_pair[2][3]])
                place_ffn_grads(0)
                pool_chip = chip_start(pool_pair, ffn_grads["w_down"])

                def swap(a):
                    return a.transpose(0, 2, 1)

                for nm in ("w_gate", "w_up"):
                    d, mo, vo = _adamw(swap(weights[nm]), ffn_grads[nm], swap(mom1[nm]), swap(mom2[nm]), f"adamw_{nm}")
                    updates[nm] = [swap(d), swap(mo), swap(vo)]
                    ffn_grads[nm] = swap(ffn_grads[nm])
                updates["w_down"] = _adamw(w_down, ffn_grads["w_down"], m_w_down, v_w_down, "adamw_w_down")
                chip_finish(pool_chip, [updates[nm][0] for nm in ("w_gate", "w_up", "w_down")])
    grad_x = dx[None]

    vec_names = (["loss"] + [f"ln_ffn{l}" for l in range(4)] + ["ln_kv", "g_kv_latent", "g_k"]
                 + [f"{p}{j}" for p in ("ln_mix_b", "g_q_latent", "g_q") for j in range(2)]
                 + [f"{p}{l}" for p in ("ln_mix_a", "pool_scale", "b_pool") for l in range(2)])
    small["loss"] = sq_cols
    widths = [small[nm].shape[1] for nm in vec_names]
    padded = [-(-w // LANES) * LANES for w in widths]
    packed = jnp.concatenate([_pad_cols(small[nm], pw) for nm, pw in zip(vec_names, padded)], axis=1)
    (all_vecs,) = _all_gather([packed], [0], "gather_vectors")
    total = _sum_lead(all_vecs, "sum_vectors")
    vec = {}
    off = 0
    for nm, w, pw in zip(vec_names, widths, padded):
        vec[nm] = total[0, off:off + w]
        off += pw
    loss = 0.5 * jnp.sum(vec["loss"]) * (1.0 / D_MODEL)

    def own_cols(full, width):
        return lax.dynamic_slice_in_dim(full, dev * width, width, axis=full.ndim - 1)

    grads = dict(
        ln_mix_a=own_cols(jnp.stack([vec["ln_mix_a0"], vec["ln_mix_a1"]]), LANES),
        w_pool=red["wpool"].reshape(2, 4, 32, GROUP_DIM),
        b_pool=own_cols(jnp.stack([vec["b_pool0"], vec["b_pool1"]]).reshape(2, 4, GROUP_DIM), 32),
        pool_scale=own_cols(jnp.stack([vec["pool_scale0"], vec["pool_scale1"]]), LANES),
        ln_ffn=jnp.stack([vec[f"ln_ffn{l}"] for l in range(4)]),
        w_gate=ffn_grads["w_gate"],
        w_up=ffn_grads["w_up"],
        w_down=ffn_grads["w_down"],
        ln_kv=vec["ln_kv"],
        g_kv_latent=vec["g_kv_latent"],
        g_k=vec["g_k"],
        ln_mix_b=jnp.stack([vec["ln_mix_b0"], vec["ln_mix_b1"]]),
        g_q_latent=jnp.stack([vec["g_q_latent0"], vec["g_q_latent1"]]),
        g_q=jnp.stack([vec["g_q0"], vec["g_q1"]]),
        **early_grads,
    )

    deltas, new_m, new_v = {}, {}, {}
    for nm in names:
        w = weights[nm]
        if nm in updates:
            deltas[nm], new_m[nm], new_v[nm] = updates[nm]
            continue
        shape = w.shape if w.ndim > 1 else (1, w.shape[0])
        d, mo, vo = _adamw(w.reshape(shape), grads[nm].reshape(shape), mom1[nm].reshape(shape),
                           mom2[nm].reshape(shape), f"adamw_{nm}")
        deltas[nm], new_m[nm], new_v[nm] = d.reshape(w.shape), mo.reshape(w.shape), vo.reshape(w.shape)

    return (loss, grad_x, *[grads[nm].reshape(weights[nm].shape) for nm in names], *[deltas[nm] for nm in names],
            *[new_m[nm] for nm in names], *[new_v[nm] for nm in names])
```

```python
import functools
import math

import jax
import jax.numpy as jnp
from jax import lax
from jax.experimental import pallas as pl
from jax.experimental.pallas import tpu as pltpu

F32 = jnp.float32
BF16 = jnp.bfloat16
MESH = pl.DeviceIdType.MESH

D_MODEL = 1024
D_FF = 2816
N_DEV = 8
N_CHIPS = 4
FF_SHARD = D_FF // N_DEV
FF_HALF = D_FF // 2
N_HEADS = 8
NOPE = 128
ROPE = 64
QK_DIM = NOPE + ROPE
QK_PAD = 256
V_DIM = 128
Q_RANK = 256
KV_RANK = 512
POOL_WINDOWS = (2, 4, 8, 16)
GROUP_DIM = 256
HALO = 128
CHUNK = 64
ROPE_THETA = 10000.0
EPS = 1e-6
LANES = 128

ADAM_LR = 0.001
ADAM_B1 = 0.9
ADAM_B2 = 0.999
ADAM_EPS = 1e-08
ADAM_WD = 0.01
ADAM_STEP = 10

PROJ_ROWS = 256
MIX_ROWS = 256
VMEM_BIG = 56 * 2**20
VMEM_MID = 40 * 2**20


def _nn(a, b):
    return lax.dot_general(a, b, (((1,), (0,)), ((), ())), preferred_element_type=F32)


def _nt(a, b):
    return lax.dot_general(a, b, (((1,), (1,)), ((), ())), preferred_element_type=F32)


def _tn(a, b):
    return lax.dot_general(a, b, (((0,), (0,)), ((), ())), preferred_element_type=F32)


def _rms(x, g, n):
    r = lax.rsqrt(jnp.sum(x * x, axis=-1, keepdims=True) * (1.0 / n) + EPS)
    return (x * r) * g, r


def _rms_bwd(x, r, g, dy, n):
    u = dy * g
    s = jnp.sum(x * u, axis=-1, keepdims=True) * (1.0 / n)
    dx = r * u - x * (r * r * r * s)
    dg = jnp.sum(dy * (x * r), axis=0, keepdims=True)
    return dx, dg


def _swap_perm():
    i = lax.broadcasted_iota(jnp.int32, (LANES, LANES), 0)
    j = lax.broadcasted_iota(jnp.int32, (LANES, LANES), 1)
    half = ROPE // 2
    hit = ((j < half) & (i == j + half)) | ((j >= half) & (j < ROPE) & (i == j - half))
    return jnp.where(hit, 1.0, 0.0).astype(BF16)


def _swap_halves(z, perm):
    hi = z.astype(BF16)
    lo = (z - hi.astype(F32)).astype(BF16)
    return _nn(hi, perm) + _nn(lo, perm)


def _sigmoid(x):
    return 1.0 / (1.0 + jnp.exp(-x))


def _cparams(n_grid, vmem=None):
    return pltpu.CompilerParams(dimension_semantics=("arbitrary",) * n_grid, vmem_limit_bytes=vmem)


def _rows(t, cols):
    return pl.BlockSpec((t, cols), lambda i: (i, 0))


def _full(shape):
    nd = len(shape)
    return pl.BlockSpec(shape, lambda *_: (0,) * nd)


ANY = pl.BlockSpec(memory_space=pl.ANY)


def _pcall(body, args, deps, *, in_specs, **kw):
    n_in, n_dep = len(args), len(deps)

    def ordered(*refs):
        body(*refs[:n_in], *refs[n_in + n_dep:])

    return pl.pallas_call(ordered, in_specs=list(in_specs) + [ANY] * n_dep, **kw)(*args, *deps)


def _place():
    x, y, c = lax.axis_index("x"), lax.axis_index("y"), lax.axis_index("c")
    return x, y, c


def _all_gather(shards, axes, name, deps=()):
    n, nd = len(shards), len(deps)
    out_shape = [jax.ShapeDtypeStruct(s.shape[:a] + (N_DEV,) + s.shape[a:], s.dtype) for s, a in zip(shards, axes)]

    def body(*refs):
        ins, outs = refs[:n], refs[n + nd:2 * n + nd]
        send_sems, recv_sems, local_sems = refs[2 * n + nd:]
        x, y, c = _place()
        me, sibling = (x, y, c), (x, y, 1 - c)
        chips = [(1 - x, y), (x, 1 - y), (1 - x, 1 - y)]

        def slot(t, dev):
            idx = 4 * dev[0] + 2 * dev[1] + dev[2]
            return outs[t].at[(slice(None),) * axes[t] + (idx,)]

        def copy(t, k, block, to, src=None):
            return pltpu.make_async_remote_copy(
                src_ref=slot(t, block) if src is None else src, dst_ref=slot(t, block),
                send_sem=send_sems.at[t, k], recv_sem=recv_sems.at[t, k],
                device_id=to, device_id_type=MESH)

        mine = [pltpu.make_async_copy(ins[t], slot(t, me), local_sems.at[t]) for t in range(n)]
        for cp in mine:
            cp.start()
        first = []
        for t in range(n):
            first.append(copy(t, 0, me, sibling, src=ins[t]))
            first += [copy(t, 1 + j, me, (*chip, c), src=ins[t]) for j, chip in enumerate(chips)]
        for cp in first:
            cp.start()
        passed = []
        for j, chip in enumerate(chips):
            for t in range(n):
                copy(t, 1 + j, (*chip, c), me).wait_recv()
                cp = copy(t, 4 + j, (*chip, c), sibling)
                cp.start()
                passed.append(cp)
        for t in range(n):
            copy(t, 0, sibling, me).wait_recv()
            for j, chip in enumerate(chips):
                copy(t, 4 + j, (*chip, 1 - c), me).wait_recv()
        for cp in first + passed:
            cp.wait_send()
        for cp in mine:
            cp.wait()

    return pl.pallas_call(
        body, name=name, out_shape=out_shape,
        in_specs=[ANY] * (n + nd), out_specs=[ANY] * n,
        scratch_shapes=[pltpu.SemaphoreType.DMA((n, 7)), pltpu.SemaphoreType.DMA((n, 7)),
                        pltpu.SemaphoreType.DMA((n,))],
    )(*shards, *deps)


HBM = pl.BlockSpec(memory_space=pltpu.HBM)
SEM = pl.BlockSpec(memory_space=pltpu.SEMAPHORE)
EFFECT = pltpu.SideEffectType.DATAFLOW_SIDE_EFFECTING


def _copies_start(arrays, n_sems, plan, name, deps=()):
    n, nd = len(arrays), len(deps)

    def body(*refs):
        for cp in plan(refs[:n], refs[n + nd], refs[n + nd + 1]):
            cp.start()
        refs[-1][...] = jnp.zeros_like(refs[-1])

    outs = pl.pallas_call(
        body, name=name,
        out_shape=(pltpu.SemaphoreType.DMA((n_sems,)), pltpu.SemaphoreType.DMA((n_sems,)),
                   *[pltpu.HBM(a.shape, a.dtype) for a in arrays], jax.ShapeDtypeStruct((8, LANES), F32)),
        in_specs=[HBM] * n + [ANY] * nd,
        out_specs=(SEM, SEM, *[HBM] * n, pl.BlockSpec(memory_space=pltpu.VMEM)),
        input_output_aliases={i: 2 + i for i in range(n)},
        compiler_params=pltpu.CompilerParams(has_side_effects=EFFECT),
    )(*[pltpu.with_memory_space_constraint(a, pltpu.HBM) for a in arrays], *deps)
    return outs[0], outs[1], list(outs[2:2 + n]), outs[-1]


def _copies_wait(arrays, send_sems, recv_sems, after, plan, name):
    n = len(arrays)
    after = list(after) if isinstance(after, (list, tuple)) else [after]

    def body(*refs):
        for cp in plan(refs[:n], refs[n], refs[n + 1]):
            cp.wait_send()
            cp.wait_recv()

    outs = pl.pallas_call(
        body, name=name,
        out_shape=tuple(pltpu.HBM(a.shape, a.dtype) for a in arrays),
        in_specs=[HBM] * n + [SEM, SEM] + [ANY] * len(after), out_specs=tuple([HBM] * n),
        input_output_aliases={i: i for i in range(n)},
        compiler_params=pltpu.CompilerParams(has_side_effects=EFFECT),
    )(*arrays, send_sems, recv_sems, *after)
    return list(outs)


def _remote(src, dst, send_sems, recv_sems, t, to):
    return pltpu.make_async_remote_copy(src_ref=src, dst_ref=dst, send_sem=send_sems.at[t], recv_sem=recv_sems.at[t],
                                        device_id=to, device_id_type=MESH)


def _dev_index(x, y, c):
    return 4 * x + 2 * y + c


def _gather_spread(bufs, send_sems, recv_sems):
    x, y, c = _place()
    mine = _dev_index(x, y, c)
    peers = [(x, y, 1 - c), (1 - x, y, c), (x, 1 - y, c), (1 - x, 1 - y, c)]
    return [_remote(g.at[k, mine], g.at[k, mine], send_sems, recv_sems, t, peer)
            for t, g in enumerate(bufs) for peer in peers for k in range(g.shape[0])]


def _gather_relay(bufs, send_sems, recv_sems):
    x, y, c = _place()
    blocks = [_dev_index(1 - x, y, c), _dev_index(x, 1 - y, c), _dev_index(1 - x, 1 - y, c)]
    return [_remote(g.at[k, b], g.at[k, b], send_sems, recv_sems, t, (x, y, 1 - c))
            for t, g in enumerate(bufs) for b in blocks for k in range(g.shape[0])]


def _blocks_moved(count):
    def plan(bufs, send_sems, recv_sems):
        x, y, c = _place()
        return [_remote(g.at[:, pl.ds(0, count)], g.at[:, pl.ds(0, count)], send_sems, recv_sems, t, (x, y, 1 - c))
                for t, g in enumerate(bufs)]
    return plan


def _pair_send(arrs, send_sems, recv_sems):
    x, y, c = _place()
    return [_remote(arrs[2 * t].at[p, k, 1 - c], arrs[2 * t + 1].at[p, k], send_sems, recv_sems, t, (x, y, 1 - c))
            for t in range(len(arrs) // 2) for p in range(arrs[2 * t].shape[0]) for k in range(N_CHIPS)]


def _chip_send(arrs, send_sems, recv_sems):
    x, y, c = _place()
    chips = [(1 - x, y), (x, 1 - y), (1 - x, 1 - y)]
    return [_remote(arrs[2 * t].at[p, 2 * px + py], arrs[2 * t + 1].at[j, p], send_sems, recv_sems, t, (px, py, c))
            for t in range(len(arrs) // 2) for j, (px, py) in enumerate(chips) for p in range(arrs[2 * t].shape[0])]


def _landed(arrs, send_sems, recv_sems):
    x, y, c = _place()
    return [_remote(arrs[2 * t + 1], arrs[2 * t + 1], send_sems, recv_sems, t, (x, y, 1 - c))
            for t in range(len(arrs) // 2)]


def _rows_per_step(rows, row_elems):
    best = 1
    for cand in range(1, rows + 1):
        if rows % cand == 0 and cand * row_elems <= 256 * 1024:
            best = cand
    return best


def _pair_sum(grad, landed, core, name):
    p, _, _, sz, c = grad.shape
    r = _rows_per_step(p * N_CHIPS, sz * c)

    def body(core_ref, g_ref, l_ref, o_ref):
        o_ref[...] = (g_ref[...].astype(F32) + l_ref[...].astype(F32)).astype(o_ref.dtype)

    out = pl.pallas_call(
        body, name=name,
        grid_spec=pltpu.PrefetchScalarGridSpec(
            num_scalar_prefetch=1, grid=(p * N_CHIPS // r,),
            in_specs=[pl.BlockSpec((r, None, sz, c), lambda i, cr: (i, cr[0], 0, 0)),
                      pl.BlockSpec((r, sz, c), lambda i, cr: (i, 0, 0))],
            out_specs=pl.BlockSpec((r, sz, c), lambda i, cr: (i, 0, 0))),
        out_shape=jax.ShapeDtypeStruct((p * N_CHIPS, sz, c), grad.dtype),
        compiler_params=_cparams(1),
    )(core, grad.reshape(p * N_CHIPS, 2, sz, c), landed.reshape(p * N_CHIPS, sz, c))
    return out.reshape(p, N_CHIPS, sz, c)


def _chip_sum(parts, landed, chip, name, deps=()):
    p, _, sz, c = parts.shape
    r = _rows_per_step(p, sz * c)

    def body(chip_ref, a_ref, l_ref, o_ref):
        acc = a_ref[...].astype(F32)
        for j in range(3):
            acc = acc + l_ref[j].astype(F32)
        o_ref[...] = acc

    nd = len(deps)

    def ordered(chip_ref, a_ref, l_ref, *rest):
        body(chip_ref, a_ref, l_ref, rest[nd])

    return pl.pallas_call(
        ordered, name=name,
        grid_spec=pltpu.PrefetchScalarGridSpec(
            num_scalar_prefetch=1, grid=(p // r,),
            in_specs=[pl.BlockSpec((r, None, sz, c), lambda i, cr: (i, cr[0], 0, 0)),
                      pl.BlockSpec((3, r, sz, c), lambda i, cr: (0, i, 0, 0))] + [ANY] * nd,
            out_specs=pl.BlockSpec((r, sz, c), lambda i, cr: (i, 0, 0))),
        out_shape=jax.ShapeDtypeStruct((p, sz, c), F32),
        compiler_params=_cparams(1),
    )(chip, parts, landed, *deps)


def _sum_lead(a, name, out_dtype=F32):
    k = a.shape[0]
    rest = a.shape[1:]
    r, c = rest[-2], rest[-1]
    lead = math.prod(rest[:-2])
    a3 = a.reshape(k, lead * r, c)
    rows = lead * r
    tb = rows
    for cand in (512, 256, 128, 64, 32, 16, 8):
        if rows % cand == 0 and rows > cand:
            tb = cand
            break

    def body(a_ref, o_ref):
        acc = a_ref[0].astype(F32)
        for i in range(1, k):
            acc = acc + a_ref[i].astype(F32)
        o_ref[...] = acc.astype(out_dtype)

    out = pl.pallas_call(
        body, name=name, grid=(rows // tb,),
        out_shape=jax.ShapeDtypeStruct((rows, c), out_dtype),
        in_specs=[pl.BlockSpec((k, tb, c), lambda i: (0, i, 0))],
        out_specs=pl.BlockSpec((tb, c), lambda i: (i, 0)),
        compiler_params=_cparams(1),
    )(a3)
    return out.reshape(rest)


def _bands(t, causal):
    r = lax.broadcasted_iota(jnp.int32, (t, t + HALO), 0)
    col = lax.broadcasted_iota(jnp.int32, (t, t + HALO), 1)
    diff = r + HALO - col if causal else col - r
    return jnp.stack([jnp.where((diff >= 0) & (diff < w), 1.0, 0.0) for w in POOL_WINDOWS]).astype(BF16)


def _split_dot(band, v):
    hi = v.astype(BF16)
    lo = (v - hi.astype(F32)).astype(BF16)
    return _nn(band, hi) + _nn(band, lo)


def _mix_fwd(x, g, wp, b, sc, name, deps=()):
    s = x.shape[0]
    t = min(MIX_ROWS, s)
    rb = t // HALO

    def body(x_ref, xh_ref, g_ref, wp_ref, b_ref, sc_ref, band_ref, xo_ref, d_ref):
        i = pl.program_id(0)
        gg = g_ref[...]
        h, _ = _rms(x_ref[...], gg, D_MODEL)
        hh, _ = _rms(xh_ref[...], gg, D_MODEL)
        hh = jnp.where(i > 0, hh, 0.0)
        hext = jnp.concatenate([hh, h], axis=0)
        tok = i * t + lax.broadcasted_iota(jnp.int32, (t, 1), 0)
        for gi, w in enumerate(POOL_WINDOWS):
            sl = slice(gi * GROUP_DIM, (gi + 1) * GROUP_DIM)
            win = _split_dot(band_ref[gi], hext[:, sl])
            inv = 1.0 / jnp.minimum(tok + 1, w).astype(F32)
            dbf = (win * inv - h[:, sl]).astype(BF16)
            d_ref[:, sl] = dbf
            ypre = _nn(dbf, wp_ref[gi]) + b_ref[:, sl]
            xo_ref[:, sl] = x_ref[:, sl] + ypre * sc_ref[:, sl]

    return _pcall(
        body, (x, x, g, wp, b, sc, _bands(t, True)), deps, name=name, grid=(s // t,),
        out_shape=[jax.ShapeDtypeStruct((s, D_MODEL), F32), jax.ShapeDtypeStruct((s, D_MODEL), BF16)],
        in_specs=[_rows(t, D_MODEL),
                  pl.BlockSpec((HALO, D_MODEL), lambda i: (jnp.maximum(i * rb - 1, 0), 0)),
                  _full((1, D_MODEL)), _full((4, GROUP_DIM, GROUP_DIM)), _full((1, D_MODEL)), _full((1, D_MODEL)),
                  _full((4, t, t + HALO))],
        out_specs=[_rows(t, D_MODEL), _rows(t, D_MODEL)],
        compiler_params=_cparams(1, VMEM_MID),
    )


def _mix_bwd(x, dy, d, g, wp, b, sc, name, deps=()):
    s = x.shape[0]
    t = min(MIX_ROWS, s)
    rb = t // HALO
    nb = s // t
    last_halo = s // HALO - 1

    def body(x_ref, dy_ref, dyn_ref, d_ref, g_ref, wp_ref, b_ref, sc_ref, band_ref,
             dx_ref, dyp_ref, dsc_ref, db_ref, dln_ref):
        i = pl.program_id(0)
        x = x_ref[...]
        gg = g_ref[...]
        dy = dy_ref[...]
        sc = sc_ref[...]
        dyp32 = dy * sc
        dyp = dyp32.astype(BF16)
        dyph = (dyn_ref[...] * sc).astype(BF16)
        dyp_ref[...] = dyp
        tok = i * t + lax.broadcasted_iota(jnp.int32, (t + HALO, 1), 0)
        dh, dsc = [], []
        for gi, w in enumerate(POOL_WINDOWS):
            sl = slice(gi * GROUP_DIM, (gi + 1) * GROUP_DIM)
            ypre = _nn(d_ref[:, sl], wp_ref[gi]) + b_ref[:, sl]
            dsc.append(jnp.sum(dy[:, sl] * ypre, axis=0, keepdims=True))
            dd = _nt(dyp[:, sl], wp_ref[gi])
            ddh = jnp.where(i < nb - 1, _nt(dyph[:, sl], wp_ref[gi]), 0.0)
            inv = 1.0 / jnp.minimum(tok + 1, w).astype(F32)
            ddext = jnp.concatenate([dd, ddh], axis=0) * inv
            dh.append(_split_dot(band_ref[gi], ddext) - dd)
        dh = jnp.concatenate(dh, axis=1)
        _, r = _rms(x, gg, D_MODEL)
        dxn, dg = _rms_bwd(x, r, gg, dh, D_MODEL)
        dx_ref[...] = dy + dxn

        @pl.when(i == 0)
        def _():
            dsc_ref[...] = jnp.zeros_like(dsc_ref)
            db_ref[...] = jnp.zeros_like(db_ref)
            dln_ref[...] = jnp.zeros_like(dln_ref)

        dsc_ref[...] += jnp.concatenate(dsc, axis=1)
        db_ref[...] += jnp.sum(dyp32, axis=0, keepdims=True)
        dln_ref[...] += dg

    vec = jax.ShapeDtypeStruct((1, D_MODEL), F32)
    return _pcall(
        body, (x, dy, dy, d, g, wp, b, sc, _bands(t, False)), deps, name=name, grid=(nb,),
        out_shape=[jax.ShapeDtypeStruct((s, D_MODEL), F32), jax.ShapeDtypeStruct((s, D_MODEL), BF16), vec, vec, vec],
        in_specs=[_rows(t, D_MODEL), _rows(t, D_MODEL),
                  pl.BlockSpec((HALO, D_MODEL), lambda i: (jnp.minimum((i + 1) * rb, last_halo), 0)),
                  _rows(t, D_MODEL),
                  _full((1, D_MODEL)), _full((4, GROUP_DIM, GROUP_DIM)), _full((1, D_MODEL)), _full((1, D_MODEL)),
                  _full((4, t, t + HALO))],
        out_specs=[_rows(t, D_MODEL), _rows(t, D_MODEL), _full((1, D_MODEL)), _full((1, D_MODEL)), _full((1, D_MODEL))],
        compiler_params=_cparams(1, VMEM_MID),
    )


def _load_weights(w_hbm, w_vmem, sem):
    @pl.when(pl.program_id(0) == 0)
    def _():
        cp = pltpu.make_async_copy(w_hbm, w_vmem, sem)
        cp.start()
        cp.wait()


def _ffn_fwd(x, g, w, name):
    s = x.shape[0]
    t = min(512, s)

    def body(x_ref, g_ref, w_hbm, xo_ref, gate_ref, up_ref, w_ref, sem):
        _load_weights(w_hbm, w_ref, sem)
        x = x_ref[...]
        hn = _rms(x, g_ref[...], D_MODEL)[0].astype(BF16)
        acc = x
        for c in range(2):
            rs = slice(c * FF_HALF, (c + 1) * FF_HALF)
            gt = _nt(hn, w_ref[0, rs, :])
            up = _nt(hn, w_ref[1, rs, :])
            gate_ref[:, rs] = gt.astype(BF16)
            up_ref[:, rs] = up.astype(BF16)
            act = ((gt * _sigmoid(gt)) * up).astype(BF16)
            acc = acc + _nn(act, w_ref[2, rs, :])
        xo_ref[...] = acc

    hid = jax.ShapeDtypeStruct((s, D_FF), BF16)
    return pl.pallas_call(
        body, name=name, grid=(s // t,),
        out_shape=[jax.ShapeDtypeStruct((s, D_MODEL), F32), hid, hid],
        in_specs=[_rows(t, D_MODEL), _full((1, D_MODEL)), ANY],
        out_specs=[_rows(t, D_MODEL), _rows(t, D_FF), _rows(t, D_FF)],
        scratch_shapes=[pltpu.VMEM((3, D_FF, D_MODEL), BF16), pltpu.SemaphoreType.DMA],
        compiler_params=_cparams(1, VMEM_BIG),
    )(x, g, w)


def _ffn_bwd(x, dy, gate, up, g, w, name, deps=()):
    s = x.shape[0]
    t = min(256, s)

    def body(x_ref, dy_ref, gate_ref, up_ref, g_ref, w_hbm,
             dx_ref, act_ref, dg_ref, du_ref, hn_ref, dyb_ref, dln_ref, w_ref, sem):
        _load_weights(w_hbm, w_ref, sem)
        x = x_ref[...]
        gg = g_ref[...]
        y, r = _rms(x, gg, D_MODEL)
        hn = y.astype(BF16)
        hn_ref[...] = hn
        dy = dy_ref[...]
        dyb = dy.astype(BF16)
        dyb_ref[...] = dyb
        dh = jnp.zeros((t, D_MODEL), F32)
        for c in range(2):
            rs = slice(c * FF_HALF, (c + 1) * FF_HALF)
            gt = gate_ref[:, rs].astype(F32)
            u = up_ref[:, rs].astype(F32)
            sg = _sigmoid(gt)
            sl = gt * sg
            act_ref[:, rs] = (sl * u).astype(BF16)
            dact = _nt(dyb, w_ref[2, rs, :])
            dg = (dact * u * (sg * (1.0 + gt * (1.0 - sg)))).astype(BF16)
            du = (dact * sl).astype(BF16)
            dg_ref[:, rs] = dg
            du_ref[:, rs] = du
            dh = dh + _nn(dg, w_ref[0, rs, :]) + _nn(du, w_ref[1, rs, :])
        dxn, dgl = _rms_bwd(x, r, gg, dh, D_MODEL)
        dx_ref[...] = dy + dxn

        @pl.when(pl.program_id(0) == 0)
        def _():
            dln_ref[...] = jnp.zeros_like(dln_ref)

        dln_ref[...] += dgl

    hid = jax.ShapeDtypeStruct((s, D_FF), BF16)
    tok = jax.ShapeDtypeStruct((s, D_MODEL), BF16)
    return _pcall(
        body, (x, dy, gate, up, g, w), deps, name=name, grid=(s // t,),
        out_shape=[jax.ShapeDtypeStruct((s, D_MODEL), F32), hid, hid, hid, tok, tok,
                   jax.ShapeDtypeStruct((1, D_MODEL), F32)],
        in_specs=[_rows(t, D_MODEL), _rows(t, D_MODEL), _rows(t, D_FF), _rows(t, D_FF), _full((1, D_MODEL)), ANY],
        out_specs=[_rows(t, D_MODEL), _rows(t, D_FF), _rows(t, D_FF), _rows(t, D_FF),
                   _rows(t, D_MODEL), _rows(t, D_MODEL), _full((1, D_MODEL))],
        scratch_shapes=[pltpu.VMEM((3, D_FF, D_MODEL), BF16), pltpu.SemaphoreType.DMA],
        compiler_params=_cparams(1, VMEM_BIG),
    )


def _tn_matmul(a, b, into, p0, name, groups=1, m_chunk=None, deps=()):
    s = a.shape[0]
    m, n = a.shape[1] // groups, b.shape[1] // groups
    assert into.shape[1:] == (m, n)
    mc = m if m_chunk is None else m_chunk
    nm = m // mc
    t = min(1024, s)
    nt = s // t

    def body(a_ref, b_ref, into_ref, o_ref, acc):
        ti = pl.program_id(2)

        @pl.when(ti == 0)
        def _():
            acc[...] = jnp.zeros_like(acc)

        acc[...] += _tn(a_ref[...], b_ref[...])

        @pl.when(ti == nt - 1)
        def _():
            o_ref[...] = acc[...].astype(o_ref.dtype)

    return _pcall(
        body, (a, b, into), deps, name=name, grid=(groups, nm, nt),
        out_shape=jax.ShapeDtypeStruct(into.shape, into.dtype),
        in_specs=[pl.BlockSpec((t, mc), lambda gi, mi, ti: (ti, gi * nm + mi)),
                  pl.BlockSpec((t, n), lambda gi, mi, ti: (ti, gi)), ANY],
        out_specs=pl.BlockSpec((None, mc, n), lambda gi, mi, ti: (p0 + gi, mi, 0)),
        scratch_shapes=[pltpu.VMEM((mc, n), F32)],
        input_output_aliases={2: 0},
        compiler_params=_cparams(3, VMEM_BIG),
    )


def _rope_tables(positions):
    half = ROPE // 2
    inv = ROPE_THETA ** (-jnp.arange(half, dtype=F32) * 2.0 / ROPE)
    ang = positions.astype(F32)[:, None] * inv
    cos, sin = jnp.cos(ang), jnp.sin(ang)
    zero = jnp.zeros((positions.shape[0], LANES - ROPE), F32)
    return jnp.concatenate([cos, cos, zero], axis=1), jnp.concatenate([-sin, sin, zero], axis=1)


def _kv_specs(t):
    return [_full((1, D_MODEL)), _full((D_MODEL, KV_RANK)), _full((D_MODEL, LANES)), _full((1, KV_RANK)),
            _full((KV_RANK, N_HEADS * NOPE)), _full((KV_RANK, N_HEADS * V_DIM)),
            _full((1, NOPE)), _full((1, LANES)), _rows(t, LANES), _rows(t, LANES)]


def _kv_fwd(x, ln, wc, wpe, gl, wuk, wuv, gkn, gkr, cos, sin, name, deps=()):
    s = x.shape[0]
    t = min(PROJ_ROWS, s)

    def body(x_ref, ln_ref, wc_ref, wpe_ref, gl_ref, wuk_ref, wuv_ref, gkn_ref, gkr_ref, cos_ref, sin_ref,
             k_ref, v_ref):
        hn = _rms(x_ref[...], ln_ref[...], D_MODEL)[0].astype(BF16)
        clat = _nn(hn, wc_ref[...])
        kpe = _nn(hn, wpe_ref[...])
        cn = _rms(clat, gl_ref[...], KV_RANK)[0].astype(BF16)
        sspe = jnp.sum(kpe * kpe, axis=-1, keepdims=True)
        base = kpe * gkr_ref[...]
        rot = base * cos_ref[...] + _swap_halves(base, _swap_perm()) * sin_ref[...]
        kn_all = _nn(cn, wuk_ref[...])
        v_ref[...] = _nn(cn, wuv_ref[...]).astype(BF16)
        for h in range(N_HEADS):
            kn = kn_all[:, h * NOPE:(h + 1) * NOPE]
            r = lax.rsqrt((jnp.sum(kn * kn, axis=-1, keepdims=True) + sspe) * (1.0 / QK_DIM) + EPS)
            k_ref[:, h * QK_PAD:h * QK_PAD + NOPE] = ((kn * r) * gkn_ref[...]).astype(BF16)
            k_ref[:, h * QK_PAD + NOPE:(h + 1) * QK_PAD] = (rot * r).astype(BF16)

    return _pcall(
        body, (x, ln, wc, wpe, gl, wuk, wuv, gkn, gkr, cos, sin), deps, name=name, grid=(s // t,),
        out_shape=[jax.ShapeDtypeStruct((s, N_HEADS * QK_PAD), BF16), jax.ShapeDtypeStruct((s, N_HEADS * V_DIM), BF16)],
        in_specs=[_rows(t, D_MODEL)] + _kv_specs(t),
        out_specs=[_rows(t, N_HEADS * QK_PAD), _rows(t, N_HEADS * V_DIM)],
        compiler_params=_cparams(1, VMEM_MID),
    )


def _kv_bwd(x, dxin, dks, dvs, ln, wc, wpe, gl, wuk, wuv, gkn, gkr, cos, sin, name):
    s = x.shape[0]
    t = min(PROJ_ROWS, s)
    nk = len(dks)

    def body(*refs):
        x_ref, dxin_ref = refs[:2]
        dk_refs = refs[2:2 + nk]
        dv_refs = refs[2 + nk:2 + 2 * nk]
        (ln_ref, wc_ref, wpe_ref, gl_ref, wuk_ref, wuv_ref, gkn_ref, gkr_ref, cos_ref, sin_ref,
         dx_ref, hn_ref, cn_ref, dkn_ref, dvb_ref, dcc_ref, dpe_ref,
         dln_ref, dgl_ref, dgkn_ref, dgkr_ref) = refs[2 + 2 * nk:]
        x = x_ref[...]
        ln = ln_ref[...]
        y, rx = _rms(x, ln, D_MODEL)
        hn = y.astype(BF16)
        hn_ref[...] = hn
        clat = _nn(hn, wc_ref[...])
        kpe = _nn(hn, wpe_ref[...])
        gl = gl_ref[...]
        cy, rc = _rms(clat, gl, KV_RANK)
        cn = cy.astype(BF16)
        cn_ref[...] = cn
        sspe = jnp.sum(kpe * kpe, axis=-1, keepdims=True)
        cs, sn, perm = cos_ref[...], sin_ref[...], _swap_perm()
        gkn, gkr = gkn_ref[...], gkr_ref[...]
        base = kpe * gkr
        rot = base * cs + _swap_halves(base, perm) * sn
        dkr_sum = jnp.zeros((t, LANES), F32)
        coef_sum = jnp.zeros((t, 1), F32)
        dgkn = jnp.zeros((1, NOPE), F32)
        kn_all = _nn(cn, wuk_ref[...])
        dkn_heads = []
        for h in range(N_HEADS):
            kn = kn_all[:, h * NOPE:(h + 1) * NOPE]
            r = lax.rsqrt((jnp.sum(kn * kn, axis=-1, keepdims=True) + sspe) * (1.0 / QK_DIM) + EPS)
            lo, mid, hi = h * QK_PAD, h * QK_PAD + NOPE, (h + 1) * QK_PAD
            dko = dk_refs[0][:, lo:mid]
            dkr = dk_refs[0][:, mid:hi]
            for j in range(1, nk):
                dko = dko + dk_refs[j][:, lo:mid]
                dkr = dkr + dk_refs[j][:, mid:hi]
            un = dko * gkn
            sm = (jnp.sum(kn * un, axis=-1, keepdims=True) + jnp.sum(rot * dkr, axis=-1, keepdims=True)) * (1.0 / QK_DIM)
            coef = r * r * r * sm
            dkn = (r * un - kn * coef).astype(BF16)
            dkr_sum = dkr_sum + r * dkr
            coef_sum = coef_sum + coef
            dgkn = dgkn + jnp.sum(dko * (kn * r), axis=0, keepdims=True)
            dkn_heads.append(dkn)
        dkn_all = jnp.concatenate(dkn_heads, axis=1)
        dkn_ref[...] = dkn_all
        dv_all = dv_refs[0][...]
        for j in range(1, nk):
            dv_all = dv_all + dv_refs[j][...]
        dvb = dv_all.astype(BF16)
        dvb_ref[...] = dvb
        dc = _nt(dkn_all, wuk_ref[...]) + _nt(dvb, wuv_ref[...])
        dz = dkr_sum * cs - _swap_halves(dkr_sum, perm) * sn
        dkpe = dz * gkr - kpe * coef_sum
        dgkr = jnp.sum(dz * kpe, axis=0, keepdims=True)
        dclat, dgl = _rms_bwd(clat, rc, gl, dc, KV_RANK)
        dcc = dclat.astype(BF16)
        dpe = dkpe.astype(BF16)
        dcc_ref[...] = dcc
        dpe_ref[...] = dpe
        dhn = _nt(dcc, wc_ref[...]) + _nt(dpe, wpe_ref[...])
        dxn, dln = _rms_bwd(x, rx, ln, dhn, D_MODEL)
        dx_ref[...] = dxin_ref[...] + dxn

        @pl.when(pl.program_id(0) == 0)
        def _():
            dln_ref[...] = jnp.zeros_like(dln_ref)
            dgl_ref[...] = jnp.zeros_like(dgl_ref)
            dgkn_ref[...] = jnp.zeros_like(dgkn_ref)
            dgkr_ref[...] = jnp.zeros_like(dgkr_ref)

        dln_ref[...] += dln
        dgl_ref[...] += dgl
        dgkn_ref[...] += dgkn
        dgkr_ref[...] += dgkr

    def tok(cols, dt):
        return jax.ShapeDtypeStruct((s, cols), dt)

    def vec(cols):
        return jax.ShapeDtypeStruct((1, cols), F32)

    return pl.pallas_call(
        body, name=name, grid=(s // t,),
        out_shape=[tok(D_MODEL, F32), tok(D_MODEL, BF16), tok(KV_RANK, BF16), tok(N_HEADS * NOPE, BF16),
                   tok(N_HEADS * V_DIM, BF16), tok(KV_RANK, BF16), tok(LANES, BF16),
                   vec(D_MODEL), vec(KV_RANK), vec(NOPE), vec(LANES)],
        in_specs=[_rows(t, D_MODEL), _rows(t, D_MODEL)] + [_rows(t, N_HEADS * QK_PAD)] * nk
                 + [_rows(t, N_HEADS * V_DIM)] * nk + _kv_specs(t),
        out_specs=[_rows(t, D_MODEL), _rows(t, D_MODEL), _rows(t, KV_RANK), _rows(t, N_HEADS * NOPE),
                   _rows(t, N_HEADS * V_DIM), _rows(t, KV_RANK), _rows(t, LANES),
                   _full((1, D_MODEL)), _full((1, KV_RANK)), _full((1, NOPE)), _full((1, LANES))],
        compiler_params=_cparams(1, VMEM_BIG),
    )(x, dxin, *dks, *dvs, ln, wc, wpe, gl, wuk, wuv, gkn, gkr, cos, sin)


def _q_specs(t):
    return [_full((1, D_MODEL)), _full((D_MODEL, Q_RANK)), _full((1, Q_RANK)), _full((N_HEADS, Q_RANK, QK_PAD)),
            _full((1, NOPE)), _full((1, LANES)), _rows(t, LANES), _rows(t, LANES)]


def _q_fwd(x, ln, wdq, gql, wuq, gqn, gqr, cos, sin, name, deps=()):
    s = x.shape[0]
    t = min(PROJ_ROWS, s)

    def body(x_ref, ln_ref, wdq_ref, gql_ref, wuq_ref, gqn_ref, gqr_ref, cos_ref, sin_ref, q_ref):
        hn = _rms(x_ref[...], ln_ref[...], D_MODEL)[0].astype(BF16)
        cqn = _rms(_nn(hn, wdq_ref[...]), gql_ref[...], Q_RANK)[0].astype(BF16)
        cs, sn, perm = cos_ref[...], sin_ref[...], _swap_perm()
        for h in range(N_HEADS):
            qa = _nn(cqn, wuq_ref[h])
            r = lax.rsqrt(jnp.sum(qa * qa, axis=-1, keepdims=True) * (1.0 / QK_DIM) + EPS)
            q_ref[:, h * QK_PAD:h * QK_PAD + NOPE] = ((qa[:, :NOPE] * r) * gqn_ref[...]).astype(BF16)
            z = (qa[:, NOPE:] * r) * gqr_ref[...]
            q_ref[:, h * QK_PAD + NOPE:(h + 1) * QK_PAD] = (z * cs + _swap_halves(z, perm) * sn).astype(BF16)

    return _pcall(
        body, (x, ln, wdq, gql, wuq, gqn, gqr, cos, sin), deps, name=name, grid=(s // t,),
        out_shape=jax.ShapeDtypeStruct((s, N_HEADS * QK_PAD), BF16),
        in_specs=[_rows(t, D_MODEL)] + _q_specs(t),
        out_specs=_rows(t, N_HEADS * QK_PAD),
        compiler_params=_cparams(1, VMEM_MID),
    )


def _q_bwd(x, dxin, dq, ln, wdq, gql, wuq, gqn, gqr, cos, sin, name):
    s = x.shape[0]
    t = min(PROJ_ROWS, s)

    def body(x_ref, dxin_ref, dq_ref, ln_ref, wdq_ref, gql_ref, wuq_ref, gqn_ref, gqr_ref, cos_ref, sin_ref,
             dx_ref, hn_ref, cqn_ref, dqa_ref, dcq_ref, dln_ref, dgql_ref, dgqn_ref, dgqr_ref):
        x = x_ref[...]
        ln = ln_ref[...]
        y, rx = _rms(x, ln, D_MODEL)
        hn = y.astype(BF16)
        hn_ref[...] = hn
        cqp = _nn(hn, wdq_ref[...])
        gql = gql_ref[...]
        cy, rc = _rms(cqp, gql, Q_RANK)
        cqn = cy.astype(BF16)
        cqn_ref[...] = cqn
        cs, sn, perm = cos_ref[...], sin_ref[...], _swap_perm()
        gqn, gqr = gqn_ref[...], gqr_ref[...]
        dcq = jnp.zeros((t, Q_RANK), F32)
        dgqn = jnp.zeros((1, NOPE), F32)
        dgqr = jnp.zeros((1, LANES), F32)
        for h in range(N_HEADS):
            qa = _nn(cqn, wuq_ref[h])
            qn, qr = qa[:, :NOPE], qa[:, NOPE:]
            r = lax.rsqrt(jnp.sum(qa * qa, axis=-1, keepdims=True) * (1.0 / QK_DIM) + EPS)
            dqo = dq_ref[:, h * QK_PAD:h * QK_PAD + NOPE]
            dqr = dq_ref[:, h * QK_PAD + NOPE:(h + 1) * QK_PAD]
            dz = dqr * cs - _swap_halves(dqr, perm) * sn
            un = dqo * gqn
            ur = dz * gqr
            sm = (jnp.sum(qn * un, axis=-1, keepdims=True) + jnp.sum(qr * ur, axis=-1, keepdims=True)) * (1.0 / QK_DIM)
            coef = r * r * r * sm
            dqa = jnp.concatenate([r * un - qn * coef, r * ur - qr * coef], axis=1).astype(BF16)
            dgqn = dgqn + jnp.sum(dqo * (qn * r), axis=0, keepdims=True)
            dgqr = dgqr + jnp.sum(dz * (qr * r), axis=0, keepdims=True)
            dqa_ref[:, h * QK_PAD:(h + 1) * QK_PAD] = dqa
            dcq = dcq + _nt(dqa, wuq_ref[h])
        dcqp, dgql = _rms_bwd(cqp, rc, gql, dcq, Q_RANK)
        dcqb = dcqp.astype(BF16)
        dcq_ref[...] = dcqb
        dhn = _nt(dcqb, wdq_ref[...])
        dxn, dln = _rms_bwd(x, rx, ln, dhn, D_MODEL)
        dx_ref[...] = dxin_ref[...] + dxn

        @pl.when(pl.program_id(0) == 0)
        def _():
            dln_ref[...] = jnp.zeros_like(dln_ref)
            dgql_ref[...] = jnp.zeros_like(dgql_ref)
            dgqn_ref[...] = jnp.zeros_like(dgqn_ref)
            dgqr_ref[...] = jnp.zeros_like(dgqr_ref)

        dln_ref[...] += dln
        dgql_ref[...] += dgql
        dgqn_ref[...] += dgqn
        dgqr_ref[...] += dgqr

    def tok(cols, dt):
        return jax.ShapeDtypeStruct((s, cols), dt)

    def vec(cols):
        return jax.ShapeDtypeStruct((1, cols), F32)

    return pl.pallas_call(
        body, name=name, grid=(s // t,),
        out_shape=[tok(D_MODEL, F32), tok(D_MODEL, BF16), tok(Q_RANK, BF16), tok(N_HEADS * QK_PAD, BF16),
                   tok(Q_RANK, BF16), vec(D_MODEL), vec(Q_RANK), vec(NOPE), vec(LANES)],
        in_specs=[_rows(t, D_MODEL), _rows(t, D_MODEL), _rows(t, N_HEADS * QK_PAD)] + _q_specs(t),
        out_specs=[_rows(t, D_MODEL), _rows(t, D_MODEL), _rows(t, Q_RANK), _rows(t, N_HEADS * QK_PAD),
                   _rows(t, Q_RANK), _full((1, D_MODEL)), _full((1, Q_RANK)), _full((1, NOPE)), _full((1, LANES))],
        compiler_params=_cparams(1, VMEM_MID),
    )(x, dxin, dq, ln, wdq, gql, wuq, gqn, gqr, cos, sin)


SM_SCALE = 1.0 / math.sqrt(QK_DIM)
LOG2_E = math.log2(math.e)
EXP2_SCALE = SM_SCALE * LOG2_E
NEG = -1e30


def _diag_mask(t):
    qpos = lax.broadcasted_iota(jnp.int32, (t, t), 0)
    kpos = lax.broadcasted_iota(jnp.int32, (t, t), 1)
    return lax.shift_right_logical(kpos, 6) <= lax.shift_right_logical(qpos, 6)


def _att_fwd(q, k, v, name):
    s = q.shape[0]
    t = min(512, s)
    nb = s // t

    def body(q_ref, k_ref, v_ref, o_ref, lse_ref):
        qi = pl.program_id(1)
        qq = q_ref[...]

        def block(ki, carry, masked):
            m_old, l_old, acc = carry
            rows = pl.ds(pl.multiple_of(ki * t, t), t)
            sc = _nt(qq, k_ref[rows, :])
            if masked:
                sc = jnp.where(_diag_mask(t), sc, NEG)
            m_new = jnp.maximum(m_old, jnp.max(sc, axis=-1, keepdims=True))
            p = jnp.exp2((sc - m_new) * EXP2_SCALE)
            alpha = jnp.exp2((m_old - m_new) * EXP2_SCALE)
            l_new = alpha * l_old + jnp.sum(p, axis=-1, keepdims=True)
            acc = alpha * acc + _nn(p.astype(BF16), v_ref[rows, :])
            return m_new, l_new, acc

        init = (jnp.full((t, 1), NEG, F32), jnp.zeros((t, 1), F32), jnp.zeros((t, V_DIM), F32))
        def pair(k0, c):
            return block(k0 + 1, block(k0, c, False), False)

        carry = lax.fori_loop(0, qi // 4, lambda j, c: pair(4 * j + 2, pair(4 * j, c)), init)
        done = 4 * (qi // 4)
        carry = lax.cond((qi & 2) != 0, lambda c: pair(done, c), lambda c: c, carry)
        carry = lax.cond((qi & 1) != 0, lambda c: block(qi - 1, c, False), lambda c: c, carry)
        m_fin, l_fin, acc = block(qi, carry, True)
        o_ref[...] = (acc / l_fin).astype(BF16)
        lse_ref[...] = jnp.broadcast_to(m_fin * SM_SCALE + jnp.log(l_fin), (t, LANES))

    return pl.pallas_call(
        body, name=name, grid=(N_HEADS, nb),
        out_shape=[jax.ShapeDtypeStruct((s, N_HEADS * V_DIM), BF16), jax.ShapeDtypeStruct((s, N_HEADS * LANES), F32)],
        in_specs=[pl.BlockSpec((t, QK_PAD), lambda h, qi: (qi, h)),
                  pl.BlockSpec((s, QK_PAD), lambda h, qi: (0, h)),
                  pl.BlockSpec((s, V_DIM), lambda h, qi: (0, h))],
        out_specs=[pl.BlockSpec((t, V_DIM), lambda h, qi: (qi, h)),
                   pl.BlockSpec((t, LANES), lambda h, qi: (qi, h))],
        compiler_params=_cparams(2, VMEM_MID),
    )(q, k, v)


def _att_bwd(q, k, v, do, stats, name, deps=()):
    s = q.shape[0]
    t = min(512, s)
    nb = s // t

    def body(q_ref, k_ref, v_ref, do_ref, st_ref, dq_ref, dk_ref, dv_ref):
        ki = pl.program_id(1)
        kk, vv = k_ref[...], v_ref[...]

        @pl.when(ki == 0)
        def _():
            dq_ref[...] = jnp.zeros_like(dq_ref)

        def block(qi, carry, masked):
            dk, dv = carry
            rows = pl.ds(pl.multiple_of(qi * t, t), t)
            qq, dob = q_ref[rows, :], do_ref[rows, :]
            sc = _nt(qq, kk)
            if masked:
                sc = jnp.where(_diag_mask(t), sc, NEG)
            st = st_ref[rows, :]
            p = jnp.exp2(sc * EXP2_SCALE - st[:, 0:1])
            dp = _nt(dob, vv)
            ds = (p * (dp - st[:, 1:2])).astype(BF16)
            dq_ref[rows, :] += _nn(ds, kk)
            return dk + _tn(ds, qq), dv + _tn(p.astype(BF16), dob)

        carry = block(ki, (jnp.zeros((t, QK_PAD), F32), jnp.zeros((t, V_DIM), F32)), True)
        rest = nb - 1 - ki

        def pair(q0, c):
            return block(q0 + 1, block(q0, c, False), False)

        carry = lax.fori_loop(0, rest // 4, lambda j, c: pair(ki + 4 * j + 3, pair(ki + 4 * j + 1, c)), carry)
        done = ki + 1 + 4 * (rest // 4)
        carry = lax.cond((rest & 2) != 0, lambda c: pair(done, c), lambda c: c, carry)
        dk, dv = lax.cond((rest & 1) != 0, lambda c: block(nb - 1, c, False), lambda c: c, carry)
        dk_ref[...] = dk * SM_SCALE
        dv_ref[...] = dv

        @pl.when(ki == nb - 1)
        def _():
            dq_ref[...] = dq_ref[...] * SM_SCALE

    def head(h, ki):
        return (0, h)

    def kblock(h, ki):
        return (ki, h)

    return _pcall(
        body, (q, k, v, do, stats), deps, name=name, grid=(N_HEADS, nb),
        out_shape=[jax.ShapeDtypeStruct((s, N_HEADS * QK_PAD), F32), jax.ShapeDtypeStruct((s, N_HEADS * QK_PAD), F32),
                   jax.ShapeDtypeStruct((s, N_HEADS * V_DIM), F32)],
        in_specs=[pl.BlockSpec((s, QK_PAD), head), pl.BlockSpec((t, QK_PAD), kblock), pl.BlockSpec((t, V_DIM), kblock),
                  pl.BlockSpec((s, V_DIM), head), pl.BlockSpec((s, LANES), head)],
        out_specs=[pl.BlockSpec((s, QK_PAD), head), pl.BlockSpec((t, QK_PAD), kblock), pl.BlockSpec((t, V_DIM), kblock)],
        compiler_params=_cparams(2, VMEM_MID),
    )


def _o_fwd(x, o, wo, name):
    s = x.shape[0]
    t = min(512, s)

    def body(x_ref, o_ref, wo_ref, xo_ref):
        xo_ref[...] = x_ref[...] + _nn(o_ref[...], wo_ref[...])

    return pl.pallas_call(
        body, name=name, grid=(s // t,),
        out_shape=jax.ShapeDtypeStruct((s, D_MODEL), F32),
        in_specs=[_rows(t, D_MODEL), _rows(t, D_MODEL), _full((D_MODEL, D_MODEL))],
        out_specs=_rows(t, D_MODEL),
        compiler_params=_cparams(1, VMEM_MID),
    )(x, o, wo)


def _o_bwd(dx, wo, o, lse, name, deps=()):
    s = dx.shape[0]
    t = min(512, s)

    def body(dx_ref, wo_ref, o_ref, lse_ref, do_ref, dxb_ref, st_ref):
        dxb = dx_ref[...].astype(BF16)
        dxb_ref[...] = dxb
        dob = _nt(dxb, wo_ref[...]).astype(BF16)
        do_ref[...] = dob
        lane = lax.broadcasted_iota(jnp.int32, (t, LANES), 1)
        for h in range(N_HEADS):
            sl = slice(h * V_DIM, (h + 1) * V_DIM)
            dsum = jnp.sum(dob[:, sl].astype(F32) * o_ref[:, sl].astype(F32), axis=-1, keepdims=True)
            st_ref[:, sl] = jnp.where(lane == 0, lse_ref[:, sl] * LOG2_E, jnp.where(lane == 1, dsum, 0.0))

    tok = jax.ShapeDtypeStruct((s, D_MODEL), BF16)
    return _pcall(
        body, (dx, wo, o, lse), deps, name=name, grid=(s // t,),
        out_shape=[tok, tok, jax.ShapeDtypeStruct((s, N_HEADS * LANES), F32)],
        in_specs=[_rows(t, D_MODEL), _full((D_MODEL, D_MODEL)), _rows(t, D_MODEL), _rows(t, N_HEADS * LANES)],
        out_specs=[_rows(t, D_MODEL), _rows(t, D_MODEL), _rows(t, N_HEADS * LANES)],
        compiler_params=_cparams(1, VMEM_MID),
    )


def _loss_head(y, target, name):
    s = y.shape[0]
    t = min(512, s)

    def body(y_ref, t_ref, dy_ref, sq_ref):
        e = y_ref[...] - t_ref[...]
        dy_ref[...] = e * (1.0 / D_MODEL)

        @pl.when(pl.program_id(0) == 0)
        def _():
            sq_ref[...] = jnp.zeros_like(sq_ref)

        sq_ref[...] += jnp.sum(e * e, axis=0, keepdims=True)

    return pl.pallas_call(
        body, name=name, grid=(s // t,),
        out_shape=[jax.ShapeDtypeStruct((s, D_MODEL), F32), jax.ShapeDtypeStruct((1, D_MODEL), F32)],
        in_specs=[_rows(t, D_MODEL), _rows(t, D_MODEL)],
        out_specs=[_rows(t, D_MODEL), _full((1, D_MODEL))],
        compiler_params=_cparams(1),
    )(y, target)


def _adamw(w, g, m, v, name):
    shape = w.shape
    c = shape[-1]
    r = math.prod(shape[:-1])
    tb = r
    for cand in (512, 256, 128):
        if r % cand == 0 and r > cand:
            tb = cand
            break

    def body(w_ref, g_ref, m_ref, v_ref, d_ref, mo_ref, vo_ref):
        gr = g_ref[...]
        mn = ADAM_B1 * m_ref[...] + (1.0 - ADAM_B1) * gr
        vn = ADAM_B2 * v_ref[...] + (1.0 - ADAM_B2) * (gr * gr)
        m_hat = mn / (1.0 - ADAM_B1 ** ADAM_STEP)
        v_hat = vn / (1.0 - ADAM_B2 ** ADAM_STEP)
        d_ref[...] = -ADAM_LR * (m_hat / (jnp.sqrt(v_hat) + ADAM_EPS) + ADAM_WD * w_ref[...])
        mo_ref[...] = mn
        vo_ref[...] = vn

    spec = pl.BlockSpec((tb, c), lambda i: (i, 0))
    flat = jax.ShapeDtypeStruct((r, c), F32)
    outs = pl.pallas_call(
        body, name=name, grid=(r // tb,),
        out_shape=[flat, flat, flat],
        in_specs=[spec] * 4, out_specs=[spec] * 3,
        compiler_params=_cparams(1),
    )(w.reshape(r, c), g.reshape(r, c), m.reshape(r, c), v.reshape(r, c))
    return [a.reshape(shape) for a in outs]


def _pad_cols(a, width):
    return jnp.pad(a, [(0, 0)] * (a.ndim - 1) + [(0, width - a.shape[-1])])


def _owner_view(a, sz):
    return a.reshape(a.shape[0], N_CHIPS, 2, sz, a.shape[-1])


def kernel(x, positions, ln_mix_a, w_pool, b_pool, pool_scale, ln_ffn, w_gate, w_up, w_down, ln_kv, w_dkv, g_kv_latent, w_uk, w_uv, g_k, ln_mix_b, w_dq, g_q_latent, w_uq, g_q, w_o, loss_target, m_ln_mix_a, m_w_pool, m_b_pool, m_pool_scale, m_ln_ffn, m_w_gate, m_w_up, m_w_down, m_ln_kv, m_w_dkv, m_g_kv_latent, m_w_uk, m_w_uv, m_g_k, m_ln_mix_b, m_w_dq, m_g_q_latent, m_w_uq, m_g_q, m_w_o, v_ln_mix_a, v_w_pool, v_b_pool, v_pool_scale, v_ln_ffn, v_w_gate, v_w_up, v_w_down, v_ln_kv, v_w_dkv, v_g_kv_latent, v_w_uk, v_w_uv, v_g_k, v_ln_mix_b, v_w_dq, v_g_q_latent, v_w_uq, v_g_q, v_w_o):
    weights = dict(ln_mix_a=ln_mix_a, w_pool=w_pool, b_pool=b_pool, pool_scale=pool_scale, ln_ffn=ln_ffn,
                   w_gate=w_gate, w_up=w_up, w_down=w_down, ln_kv=ln_kv, w_dkv=w_dkv, g_kv_latent=g_kv_latent,
                   w_uk=w_uk, w_uv=w_uv, g_k=g_k, ln_mix_b=ln_mix_b, w_dq=w_dq, g_q_latent=g_q_latent,
                   w_uq=w_uq, g_q=g_q, w_o=w_o)
    mom1 = dict(ln_mix_a=m_ln_mix_a, w_pool=m_w_pool, b_pool=m_b_pool, pool_scale=m_pool_scale, ln_ffn=m_ln_ffn,
                w_gate=m_w_gate, w_up=m_w_up, w_down=m_w_down, ln_kv=m_ln_kv, w_dkv=m_w_dkv,
                g_kv_latent=m_g_kv_latent, w_uk=m_w_uk, w_uv=m_w_uv, g_k=m_g_k, ln_mix_b=m_ln_mix_b, w_dq=m_w_dq,
                g_q_latent=m_g_q_latent, w_uq=m_w_uq, g_q=m_g_q, w_o=m_w_o)
    mom2 = dict(ln_mix_a=v_ln_mix_a, w_pool=v_w_pool, b_pool=v_b_pool, pool_scale=v_pool_scale, ln_ffn=v_ln_ffn,
                w_gate=v_w_gate, w_up=v_w_up, w_down=v_w_down, ln_kv=v_ln_kv, w_dkv=v_w_dkv,
                g_kv_latent=v_g_kv_latent, w_uk=v_w_uk, w_uv=v_w_uv, g_k=v_g_k, ln_mix_b=v_ln_mix_b, w_dq=v_w_dq,
                g_q_latent=v_g_q_latent, w_uq=v_w_uq, g_q=v_g_q, w_o=v_w_o)
    names = list(weights)
    dev = 4 * lax.axis_index("x") + 2 * lax.axis_index("y") + lax.axis_index("c")
    core = lax.axis_index("c").astype(jnp.int32).reshape(1)
    chip = (2 * lax.axis_index("x") + lax.axis_index("y")).astype(jnp.int32).reshape(1)

    xs = x[0]
    target = loss_target[0]
    cos, sin = _rope_tables(positions[0])

    def placed(shard):
        buf = lax.empty((shard.shape[0], N_DEV) + shard.shape[1:], shard.dtype)
        return lax.dynamic_update_slice(buf, shard[:, None], (0, dev, 0, 0))

    def ffn_shard(l):
        return jnp.stack([w_gate[l].T, w_up[l].T, w_down[l]]).astype(BF16)

    groups = {"ffn0": [placed(ffn_shard(0))]}
    small_sh = jnp.concatenate([ln_mix_a.reshape(1, -1), pool_scale.reshape(1, -1), b_pool.reshape(1, -1)], axis=1)
    wp_g, small_g = _all_gather([w_pool.astype(BF16), small_sh], [2, 0], "gather_first")
    wp_all = wp_g.reshape(2, 4, GROUP_DIM, GROUP_DIM)
    small_g = small_g.reshape(N_DEV, 3, 2, LANES)
    ln_a_all = small_g[:, 0].transpose(1, 0, 2).reshape(2, 1, D_MODEL)
    sc_all = small_g[:, 1].transpose(1, 0, 2).reshape(2, 1, D_MODEL)
    bp_all = small_g[:, 2].reshape(N_DEV, 2, 4, 32).transpose(1, 2, 0, 3).reshape(2, 1, D_MODEL)
    sp0 = _copies_start(groups["ffn0"], 1, _gather_spread, "spread_ffn0", deps=[small_g])
    zero = sp0[3][0, 0].astype(BF16)
    for l in (1, 2, 3):
        groups[f"ffn{l}"] = [placed(ffn_shard(l) + zero)]
    groups["att"] = [placed(a.astype(BF16) + zero) for a in (
        w_dkv[None, :, :KV_RANK], _pad_cols(w_dkv[None, :, KV_RANK:], LANES), w_uk[None], w_uv[None],
        w_dq, _pad_cols(w_uq, QK_PAD), w_o)]

    def spread_start(nm, deps):
        return _copies_start(groups[nm], len(groups[nm]), _gather_spread, f"spread_{nm}", deps=deps)

    def spread_wait(nm, state, after):
        ssem, rsem, bufs, _ = state
        return _copies_wait(bufs, ssem, rsem, after, _blocks_moved(4), f"spread_done_{nm}")

    def relay_start(nm, bufs, deps=()):
        return _copies_start(bufs, len(bufs), _gather_relay, f"relay_{nm}", deps=deps)

    def relay_wait(nm, state, after):
        ssem, rsem, bufs, _ = state
        return _copies_wait(bufs, ssem, rsem, after, _blocks_moved(3), f"relay_done_{nm}")

    gkn = g_k[:NOPE].reshape(1, NOPE)
    gkr = _pad_cols(g_k[NOPE:].reshape(1, ROPE), LANES)
    gl = g_kv_latent.reshape(1, KV_RANK)
    lnkv = ln_kv.reshape(1, D_MODEL)

    x_in, x_mid, pooled, gates, ups, w_ffn = [], [], [], [], [], []
    qs, outs, lses = [], [], []

    def mixer(l, cur, deps):
        x_in.append(cur)
        mid, dsave = _mix_fwd(cur, ln_a_all[l], wp_all[l], bp_all[l], sc_all[l], f"mix_fwd{l}", deps=deps)
        pooled.append(dsave)
        x_mid.append(mid)
        return mid

    def q_args(j):
        return (ln_mix_b[j].reshape(1, -1), wdq_all[j], g_q_latent[j].reshape(1, -1), wuq_all[j],
                g_q[j, :NOPE].reshape(1, -1), _pad_cols(g_q[j, NOPE:].reshape(1, -1), LANES), cos, sin)

    def attention(j, cur, deps):
        x_in.append(cur)
        q = _q_fwd(cur, *q_args(j), f"q_fwd{j}", deps=deps)
        o, lse = _att_fwd(q, k_sh, v_sh, f"att_fwd{j}")
        mid = _o_fwd(cur, o, wo_all[j], f"o_fwd{j}")
        qs.append(q)
        outs.append(o)
        lses.append(lse)
        x_mid.append(mid)
        return mid

    def ffn(l, mid, relayed):
        w_l = relayed[0].reshape(3, D_FF, D_MODEL)
        w_ffn.append(w_l)
        cur, gate, up = _ffn_fwd(mid, ln_ffn[l].reshape(1, -1), w_l, f"ffn_fwd{l}")
        gates.append(gate)
        ups.append(up)
        return cur

    mid = mixer(0, xs, [sp0[3]])
    prepared = [buf for nm in ("ffn1", "att", "ffn2", "ffn3") for buf in groups[nm]]
    landed0 = spread_wait("ffn0", sp0, [mid] + prepared)
    sp1 = spread_start("ffn1", [landed0[0]])
    rl0 = relay_start("ffn0", landed0, [sp1[3]])
    cur = ffn(0, mid, relay_wait("ffn0", rl0, rl0[3]))

    landed1 = spread_wait("ffn1", sp1, cur)
    sp_att = spread_start("att", [landed1[0]])
    sp2 = spread_start("ffn2", [landed1[0]])
    rl1 = relay_start("ffn1", landed1, [sp_att[3], sp2[3]])
    mid = mixer(1, cur, [rl1[3]])
    cur = ffn(1, mid, relay_wait("ffn1", rl1, mid))
    x_kv = cur

    landed_att = spread_wait("att", sp_att, cur)
    landed2 = spread_wait("ffn2", sp2, cur)
    sp3 = spread_start("ffn3", [landed2[0]])
    rl_att = relay_start("att", landed_att, [sp3[3]])
    rl2 = relay_start("ffn2", landed2, [sp3[3]])
    att_bufs = relay_wait("att", rl_att, rl2[3])
    wc = att_bufs[0].reshape(D_MODEL, KV_RANK)
    wpe = att_bufs[1].reshape(D_MODEL, LANES)
    wuk_g = att_bufs[2].reshape(N_HEADS, KV_RANK, NOPE).transpose(1, 0, 2).reshape(KV_RANK, N_HEADS * NOPE)
    wuv_g = att_bufs[3].reshape(N_HEADS, KV_RANK, V_DIM).transpose(1, 0, 2).reshape(KV_RANK, N_HEADS * V_DIM)
    wdq_all = att_bufs[4].reshape(2, D_MODEL, Q_RANK)
    wuq_all = att_bufs[5]
    wo_all = att_bufs[6].reshape(2, D_MODEL, D_MODEL)
    k_sh, v_sh = _kv_fwd(cur, lnkv, wc, wpe, gl, wuk_g, wuv_g, gkn, gkr, cos, sin, "kv_fwd")
    mid = attention(0, cur, [])
    cur = ffn(2, mid, relay_wait("ffn2", rl2, mid))

    landed3 = spread_wait("ffn3", sp3, cur)
    rl3 = relay_start("ffn3", landed3)
    mid = attention(1, cur, [rl3[3]])
    cur = ffn(3, mid, relay_wait("ffn3", rl3, mid))

    dx, sq_cols = _loss_head(cur, target, "loss_head")

    small = {}
    sizes = dict(ffn0=FF_SHARD, ffn1=FF_SHARD, ffn2=FF_SHARD, ffn3=FF_SHARD, wo=128, kv512=128, dkv_pe=128,
                 wdq=128, wuqT=QK_PAD, wpool=32)
    big = dict(wo=lax.empty((2, D_MODEL, D_MODEL), BF16), kv512=lax.empty((3, D_MODEL, KV_RANK), BF16),
               dkv_pe=lax.empty((1, D_MODEL, LANES), BF16), wdq=lax.empty((2, D_MODEL, Q_RANK), BF16),
               wuqT=lax.empty((2, N_HEADS * QK_PAD, Q_RANK), BF16), wpool=lax.empty((8, GROUP_DIM, GROUP_DIM), BF16))
    for l in range(4):
        big[f"ffn{l}"] = lax.empty((3, D_FF, D_MODEL), BF16)
    red = {}

    def pair_start(nms, tag):
        arrs = []
        for nm in nms:
            view = _owner_view(big[nm], sizes[nm])
            arrs += [view, lax.empty((view.shape[0], N_CHIPS) + view.shape[3:], BF16)]
        return nms, tag, _copies_start(arrs, len(nms), _pair_send, f"pair_start_{tag}")

    def chip_start(state, after):
        nms, tag, (ssem, rsem, arrs, _) = state
        arrs = _copies_wait(arrs, ssem, rsem, after, _landed, f"pair_done_{tag}")
        out = []
        for t, nm in enumerate(nms):
            part = _pair_sum(arrs[2 * t], arrs[2 * t + 1], core, f"pair_sum_{nm}")
            out += [part, lax.empty((3, part.shape[0]) + part.shape[2:], BF16)]
        return nms, tag, _copies_start(out, len(nms), _chip_send, f"chip_start_{tag}")

    deferred = []
    updates = {}

    def chip_finish(state, after, defer=False):
        nms, tag, (ssem, rsem, arrs, _) = state
        arrs = _copies_wait(arrs, ssem, rsem, after, _landed, f"chip_done_{tag}")
        for t, nm in enumerate(nms):
            if defer:
                deferred.append((nm, arrs[2 * t], arrs[2 * t + 1]))
            else:
                red[nm] = _chip_sum(arrs[2 * t], arrs[2 * t + 1], chip, f"chip_sum_{nm}")

    ffn_grads = {nm: lax.empty((4, FF_SHARD, D_MODEL), F32) for nm in ("w_gate", "w_up", "w_down")}

    def place_ffn_grads(l):
        g = red[f"ffn{l}"]
        for k, nm in enumerate(("w_gate", "w_up", "w_down")):
            ffn_grads[nm] = ffn_grads[nm].at[l].set(g[k])

    dks, dvs = [], []
    pending = None
    bwd_deps = []
    for l in (3, 2, 1, 0):
        key = f"ffn{l}"
        dx, act, dgb, dub, hn, dyb, dln = _ffn_bwd(x_mid[l], dx, gates[l], ups[l], ln_ffn[l].reshape(1, -1),
                                                     w_ffn[l], f"ffn_bwd{l}", deps=bwd_deps)
        bwd_deps = []
        small[f"ln_ffn{l}"] = dln
        if l == 1:
            att_chip = chip_start(att_pair, dx)
            tn_deps = [att_chip[2][3]]
        else:
            tn_deps = []
        if pending:
            chip_finish(pending, dx, defer=True)
            pending = None
        big[key] = _tn_matmul(dgb, hn, big[key], 0, f"dw_gate{l}", m_chunk=FF_HALF, deps=tn_deps)
        big[key] = _tn_matmul(dub, hn, big[key], 1, f"dw_up{l}", m_chunk=FF_HALF)
        big[key] = _tn_matmul(act, dyb, big[key], 2, f"dw_down{l}", m_chunk=FF_HALF)
        if l == 1:
            chip_finish(att_chip, big[key], defer=True)
        ffn_pair = pair_start([key], key)
        if l >= 2:
            j = l - 2
            do, dxb, stats = _o_bwd(dx, wo_all[j], outs[j], lses[j], f"o_bwd{j}", deps=[ffn_pair[2][3]])
            big["wo"] = _tn_matmul(outs[j], dxb, big["wo"], j, f"dw_o{j}")
            ffn_chip = chip_start(ffn_pair, big["wo"])
            dq, dk, dv = _att_bwd(qs[j], k_sh, v_sh, do, stats, f"att_bwd{j}", deps=[ffn_chip[2][3]])
            chip_finish(ffn_chip, dq, defer=True)
            dks.append(dk)
            dvs.append(dv)
            dx, hnq, cqn, dqa, dcq, dln, dgql, dgqn, dgqr = _q_bwd(x_in[l], dx, dq, *q_args(j), f"q_bwd{j}")
            small[f"ln_mix_b{j}"] = dln
            small[f"g_q_latent{j}"] = dgql
            small[f"g_q{j}"] = jnp.concatenate([dgqn, dgqr[:, :ROPE]], axis=1)
            big["wdq"] = _tn_matmul(hnq, dcq, big["wdq"], j, f"dw_dq{j}")
            big["wuqT"] = _tn_matmul(dqa, cqn, big["wuqT"], j, f"dw_uq{j}")
            if l == 2:
                (dx, hnk, cn, dknb, dvb, dccb, dpeb, dlnkv, dgl, dgkn, dgkr) = _kv_bwd(
                    x_kv, dx, dks, dvs, lnkv, wc, wpe, gl, wuk_g, wuv_g, gkn, gkr, cos, sin, "kv_bwd")
                small["ln_kv"] = dlnkv
                small["g_kv_latent"] = dgl
                small["g_k"] = jnp.concatenate([dgkn, dgkr[:, :ROPE]], axis=1)
                big["kv512"] = _tn_matmul(dknb, cn, big["kv512"], 0, "dw_uk")
                big["kv512"] = _tn_matmul(dvb, cn, big["kv512"], 1, "dw_uv")
                big["kv512"] = _tn_matmul(hnk, dccb, big["kv512"], 2, "dw_dkv_c")
                big["dkv_pe"] = _tn_matmul(hnk, dpeb, big["dkv_pe"], 0, "dw_dkv_pe")
                att_pair = pair_start(["wo", "kv512", "dkv_pe", "wdq", "wuqT"], "att")
                bwd_deps = [att_pair[2][3]]
        else:
            dx, dyp, dsc, db, dln = _mix_bwd(x_in[l], dx, pooled[l], ln_a_all[l], wp_all[l], bp_all[l], sc_all[l],
                                             f"mix_bwd{l}", deps=[ffn_pair[2][3]])
            small[f"ln_mix_a{l}"] = dln
            small[f"pool_scale{l}"] = dsc
            small[f"b_pool{l}"] = db
            ffn_chip = chip_start(ffn_pair, dx)
            big["wpool"] = _tn_matmul(pooled[l], dyp, big["wpool"], 4 * l, f"dw_pool{l}", groups=4,
                                      deps=[ffn_chip[2][3]])
            if l == 1:
                pending = ffn_chip
                bwd_deps = [ffn_chip[2][3]]
            else:
                for nm, part, land in deferred:
                    red[nm] = _chip_sum(part, land, chip, f"chip_sum_{nm}", deps=[ffn_chip[2][3]])
                    if nm.startswith("ffn"):
                        place_ffn_grads(int(nm[-1]))
                early_grads = dict(
                    w_dkv=jnp.concatenate([red["kv512"][2], red["dkv_pe"][0][:, :ROPE]], axis=1),
                    w_uk=red["kv512"][0].T, w_uv=red["kv512"][1].T, w_dq=red["wdq"],
                    w_uq=red["wuqT"].transpose(0, 2, 1)[:, :, :QK_DIM], w_o=red["wo"])
                for nm, g in early_grads.items():
                    updates[nm] = _adamw(weights[nm], g, mom1[nm], mom2[nm], f"adamw_{nm}")
                chip_finish(ffn_chip, [big["wpool"]] + list(ffn_grads.values()) + [u[0] for u in updates.values()])
                place_ffn_grads(0)
    grad_x = dx[None]
    pool_pair = pair_start(["wpool"], "wpool")
    pool_chip = chip_start(pool_pair, pool_pair[2][3])
    chip_finish(pool_chip, pool_chip[2][3])

    vec_names = (["loss"] + [f"ln_ffn{l}" for l in range(4)] + ["ln_kv", "g_kv_latent", "g_k"]
                 + [f"{p}{j}" for p in ("ln_mix_b", "g_q_latent", "g_q") for j in range(2)]
                 + [f"{p}{l}" for p in ("ln_mix_a", "pool_scale", "b_pool") for l in range(2)])
    small["loss"] = sq_cols
    widths = [small[nm].shape[1] for nm in vec_names]
    padded = [-(-w // LANES) * LANES for w in widths]
    packed = jnp.concatenate([_pad_cols(small[nm], pw) for nm, pw in zip(vec_names, padded)], axis=1)
    (all_vecs,) = _all_gather([packed], [0], "gather_vectors")
    total = _sum_lead(all_vecs, "sum_vectors")
    vec = {}
    off = 0
    for nm, w, pw in zip(vec_names, widths, padded):
        vec[nm] = total[0, off:off + w]
        off += pw
    loss = 0.5 * jnp.sum(vec["loss"]) * (1.0 / D_MODEL)

    def own_cols(full, width):
        return lax.dynamic_slice_in_dim(full, dev * width, width, axis=full.ndim - 1)

    grads = dict(
        ln_mix_a=own_cols(jnp.stack([vec["ln_mix_a0"], vec["ln_mix_a1"]]), LANES),
        w_pool=red["wpool"].reshape(2, 4, 32, GROUP_DIM),
        b_pool=own_cols(jnp.stack([vec["b_pool0"], vec["b_pool1"]]).reshape(2, 4, GROUP_DIM), 32),
        pool_scale=own_cols(jnp.stack([vec["pool_scale0"], vec["pool_scale1"]]), LANES),
        ln_ffn=jnp.stack([vec[f"ln_ffn{l}"] for l in range(4)]),
        w_gate=ffn_grads["w_gate"],
        w_up=ffn_grads["w_up"],
        w_down=ffn_grads["w_down"],
        ln_kv=vec["ln_kv"],
        g_kv_latent=vec["g_kv_latent"],
        g_k=vec["g_k"],
        ln_mix_b=jnp.stack([vec["ln_mix_b0"], vec["ln_mix_b1"]]),
        g_q_latent=jnp.stack([vec["g_q_latent0"], vec["g_q_latent1"]]),
        g_q=jnp.stack([vec["g_q0"], vec["g_q1"]]),
        **early_grads,
    )

    deltas, new_m, new_v = {}, {}, {}
    for nm in names:
        w = weights[nm]
        if nm in updates:
            deltas[nm], new_m[nm], new_v[nm] = updates[nm]
            continue
        if nm in ("w_gate", "w_up"):
            def swap(a):
                return a.transpose(0, 2, 1)
            d, mo, vo = _adamw(swap(w), grads[nm], swap(mom1[nm]), swap(mom2[nm]), f"adamw_{nm}")
            deltas[nm], new_m[nm], new_v[nm], grads[nm] = swap(d), swap(mo), swap(vo), swap(grads[nm])
            continue
        shape = w.shape if w.ndim > 1 else (1, w.shape[0])
        d, mo, vo = _adamw(w.reshape(shape), grads[nm].reshape(shape), mom1[nm].reshape(shape),
                           mom2[nm].reshape(shape), f"adamw_{nm}")
        deltas[nm], new_m[nm], new_v[nm] = d.reshape(w.shape), mo.reshape(w.shape), vo.reshape(w.shape)

    return (loss, grad_x, *[grads[nm].reshape(weights[nm].shape) for nm in names], *[deltas[nm] for nm in names],
            *[new_m[nm] for nm in names], *[new_v[nm] for nm in names])
```

```python
import functools
import math

import jax
import jax.numpy as jnp
from jax import lax
from jax.experimental import pallas as pl
from jax.experimental.pallas import tpu as pltpu

F32 = jnp.float32
BF16 = jnp.bfloat16
MESH = pl.DeviceIdType.MESH

D_MODEL = 1024
D_FF = 2816
N_DEV = 8
N_CHIPS = 4
FF_SHARD = D_FF // N_DEV
FF_HALF = D_FF // 2
N_HEADS = 8
NOPE = 128
ROPE = 64
QK_DIM = NOPE + ROPE
QK_PAD = 256
V_DIM = 128
Q_RANK = 256
KV_RANK = 512
POOL_WINDOWS = (2, 4, 8, 16)
GROUP_DIM = 256
HALO = 128
CHUNK = 64
ROPE_THETA = 10000.0
EPS = 1e-6
LANES = 128

ADAM_LR = 0.001
ADAM_B1 = 0.9
ADAM_B2 = 0.999
ADAM_EPS = 1e-08
ADAM_WD = 0.01
ADAM_STEP = 10

PROJ_ROWS = 256
MIX_ROWS = 256
VMEM_BIG = 56 * 2**20
VMEM_MID = 40 * 2**20


def _nn(a, b):
    return lax.dot_general(a, b, (((1,), (0,)), ((), ())), preferred_element_type=F32)


def _nt(a, b):
    return lax.dot_general(a, b, (((1,), (1,)), ((), ())), preferred_element_type=F32)


def _tn(a, b):
    return lax.dot_general(a, b, (((0,), (0,)), ((), ())), preferred_element_type=F32)


def _rms(x, g, n):
    r = lax.rsqrt(jnp.sum(x * x, axis=-1, keepdims=True) * (1.0 / n) + EPS)
    return (x * r) * g, r


def _rms_bwd(x, r, g, dy, n):
    u = dy * g
    s = jnp.sum(x * u, axis=-1, keepdims=True) * (1.0 / n)
    dx = r * u - x * (r * r * r * s)
    dg = jnp.sum(dy * (x * r), axis=0, keepdims=True)
    return dx, dg


def _swap_perm():
    i = lax.broadcasted_iota(jnp.int32, (LANES, LANES), 0)
    j = lax.broadcasted_iota(jnp.int32, (LANES, LANES), 1)
    half = ROPE // 2
    hit = ((j < half) & (i == j + half)) | ((j >= half) & (j < ROPE) & (i == j - half))
    return jnp.where(hit, 1.0, 0.0).astype(BF16)


def _swap_halves(z, perm):
    hi = z.astype(BF16)
    lo = (z - hi.astype(F32)).astype(BF16)
    return _nn(hi, perm) + _nn(lo, perm)


def _sigmoid(x):
    return 1.0 / (1.0 + jnp.exp(-x))


def _cparams(n_grid, vmem=None):
    return pltpu.CompilerParams(dimension_semantics=("arbitrary",) * n_grid, vmem_limit_bytes=vmem)


def _rows(t, cols):
    return pl.BlockSpec((t, cols), lambda i: (i, 0))


def _full(shape):
    nd = len(shape)
    return pl.BlockSpec(shape, lambda *_: (0,) * nd)


ANY = pl.BlockSpec(memory_space=pl.ANY)


def _pcall(body, args, deps, *, in_specs, **kw):
    n_in, n_dep = len(args), len(deps)

    def ordered(*refs):
        body(*refs[:n_in], *refs[n_in + n_dep:])

    return pl.pallas_call(ordered, in_specs=list(in_specs) + [ANY] * n_dep, **kw)(*args, *deps)


def _place():
    x, y, c = lax.axis_index("x"), lax.axis_index("y"), lax.axis_index("c")
    return x, y, c


def _all_gather(shards, axes, name, deps=()):
    n, nd = len(shards), len(deps)
    out_shape = [jax.ShapeDtypeStruct(s.shape[:a] + (N_DEV,) + s.shape[a:], s.dtype) for s, a in zip(shards, axes)]

    def body(*refs):
        ins, outs = refs[:n], refs[n + nd:2 * n + nd]
        send_sems, recv_sems, local_sems = refs[2 * n + nd:]
        x, y, c = _place()
        me, sibling = (x, y, c), (x, y, 1 - c)
        chips = [(1 - x, y), (x, 1 - y), (1 - x, 1 - y)]

        def slot(t, dev):
            idx = 4 * dev[0] + 2 * dev[1] + dev[2]
            return outs[t].at[(slice(None),) * axes[t] + (idx,)]

        def copy(t, k, block, to, src=None):
            return pltpu.make_async_remote_copy(
                src_ref=slot(t, block) if src is None else src, dst_ref=slot(t, block),
                send_sem=send_sems.at[t, k], recv_sem=recv_sems.at[t, k],
                device_id=to, device_id_type=MESH)

        mine = [pltpu.make_async_copy(ins[t], slot(t, me), local_sems.at[t]) for t in range(n)]
        for cp in mine:
            cp.start()
        first = []
        for t in range(n):
            first.append(copy(t, 0, me, sibling, src=ins[t]))
            first += [copy(t, 1 + j, me, (*chip, c), src=ins[t]) for j, chip in enumerate(chips)]
        for cp in first:
            cp.start()
        passed = []
        for j, chip in enumerate(chips):
            for t in range(n):
                copy(t, 1 + j, (*chip, c), me).wait_recv()
                cp = copy(t, 4 + j, (*chip, c), sibling)
                cp.start()
                passed.append(cp)
        for t in range(n):
            copy(t, 0, sibling, me).wait_recv()
            for j, chip in enumerate(chips):
                copy(t, 4 + j, (*chip, 1 - c), me).wait_recv()
        for cp in first + passed:
            cp.wait_send()
        for cp in mine:
            cp.wait()

    return pl.pallas_call(
        body, name=name, out_shape=out_shape,
        in_specs=[ANY] * (n + nd), out_specs=[ANY] * n,
        scratch_shapes=[pltpu.SemaphoreType.DMA((n, 7)), pltpu.SemaphoreType.DMA((n, 7)),
                        pltpu.SemaphoreType.DMA((n,))],
    )(*shards, *deps)


HBM = pl.BlockSpec(memory_space=pltpu.HBM)
SEM = pl.BlockSpec(memory_space=pltpu.SEMAPHORE)
EFFECT = pltpu.SideEffectType.DATAFLOW_SIDE_EFFECTING


def _copies_start(arrays, n_sems, plan, name, deps=()):
    n, nd = len(arrays), len(deps)

    def body(*refs):
        for cp in plan(refs[:n], refs[n + nd], refs[n + nd + 1]):
            cp.start()
        refs[-1][...] = jnp.zeros_like(refs[-1])

    outs = pl.pallas_call(
        body, name=name,
        out_shape=(pltpu.SemaphoreType.DMA((n_sems,)), pltpu.SemaphoreType.DMA((n_sems,)),
                   *[pltpu.HBM(a.shape, a.dtype) for a in arrays], jax.ShapeDtypeStruct((8, LANES), F32)),
        in_specs=[HBM] * n + [ANY] * nd,
        out_specs=(SEM, SEM, *[HBM] * n, pl.BlockSpec(memory_space=pltpu.VMEM)),
        input_output_aliases={i: 2 + i for i in range(n)},
        compiler_params=pltpu.CompilerParams(has_side_effects=EFFECT),
    )(*[pltpu.with_memory_space_constraint(a, pltpu.HBM) for a in arrays], *deps)
    return outs[0], outs[1], list(outs[2:2 + n]), outs[-1]


def _copies_wait(arrays, send_sems, recv_sems, after, plan, name):
    n = len(arrays)
    after = list(after) if isinstance(after, (list, tuple)) else [after]

    def body(*refs):
        for cp in plan(refs[:n], refs[n], refs[n + 1]):
            cp.wait_send()
            cp.wait_recv()

    outs = pl.pallas_call(
        body, name=name,
        out_shape=tuple(pltpu.HBM(a.shape, a.dtype) for a in arrays),
        in_specs=[HBM] * n + [SEM, SEM] + [ANY] * len(after), out_specs=tuple([HBM] * n),
        input_output_aliases={i: i for i in range(n)},
        compiler_params=pltpu.CompilerParams(has_side_effects=EFFECT),
    )(*arrays, send_sems, recv_sems, *after)
    return list(outs)


def _remote(src, dst, send_sems, recv_sems, t, to):
    return pltpu.make_async_remote_copy(src_ref=src, dst_ref=dst, send_sem=send_sems.at[t], recv_sem=recv_sems.at[t],
                                        device_id=to, device_id_type=MESH)


def _dev_index(x, y, c):
    return 4 * x + 2 * y + c


def _gather_spread(bufs, send_sems, recv_sems):
    x, y, c = _place()
    mine = _dev_index(x, y, c)
    peers = [(x, y, 1 - c), (1 - x, y, c), (x, 1 - y, c), (1 - x, 1 - y, c)]
    return [_remote(g.at[k, mine], g.at[k, mine], send_sems, recv_sems, t, peer)
            for t, g in enumerate(bufs) for peer in peers for k in range(g.shape[0])]


def _gather_relay(bufs, send_sems, recv_sems):
    x, y, c = _place()
    blocks = [_dev_index(1 - x, y, c), _dev_index(x, 1 - y, c), _dev_index(1 - x, 1 - y, c)]
    return [_remote(g.at[k, b], g.at[k, b], send_sems, recv_sems, t, (x, y, 1 - c))
            for t, g in enumerate(bufs) for b in blocks for k in range(g.shape[0])]


def _blocks_moved(count):
    def plan(bufs, send_sems, recv_sems):
        x, y, c = _place()
        return [_remote(g.at[:, pl.ds(0, count)], g.at[:, pl.ds(0, count)], send_sems, recv_sems, t, (x, y, 1 - c))
                for t, g in enumerate(bufs)]
    return plan


def _pair_send(arrs, send_sems, recv_sems):
    x, y, c = _place()
    return [_remote(arrs[2 * t].at[p, k, 1 - c], arrs[2 * t + 1].at[p, k], send_sems, recv_sems, t, (x, y, 1 - c))
            for t in range(len(arrs) // 2) for p in range(arrs[2 * t].shape[0]) for k in range(N_CHIPS)]


def _chip_send(arrs, send_sems, recv_sems):
    x, y, c = _place()
    chips = [(1 - x, y), (x, 1 - y), (1 - x, 1 - y)]
    return [_remote(arrs[2 * t].at[p, 2 * px + py], arrs[2 * t + 1].at[j, p], send_sems, recv_sems, t, (px, py, c))
            for t in range(len(arrs) // 2) for j, (px, py) in enumerate(chips) for p in range(arrs[2 * t].shape[0])]


def _landed(arrs, send_sems, recv_sems):
    x, y, c = _place()
    return [_remote(arrs[2 * t + 1], arrs[2 * t + 1], send_sems, recv_sems, t, (x, y, 1 - c))
            for t in range(len(arrs) // 2)]


def _rows_per_step(rows, row_elems):
    best = 1
    for cand in range(1, rows + 1):
        if rows % cand == 0 and cand * row_elems <= 256 * 1024:
            best = cand
    return best


def _pair_sum(grad, landed, core, name):
    p, _, _, sz, c = grad.shape
    r = _rows_per_step(p * N_CHIPS, sz * c)

    def body(core_ref, g_ref, l_ref, o_ref):
        o_ref[...] = (g_ref[...].astype(F32) + l_ref[...].astype(F32)).astype(o_ref.dtype)

    out = pl.pallas_call(
        body, name=name,
        grid_spec=pltpu.PrefetchScalarGridSpec(
            num_scalar_prefetch=1, grid=(p * N_CHIPS // r,),
            in_specs=[pl.BlockSpec((r, None, sz, c), lambda i, cr: (i, cr[0], 0, 0)),
                      pl.BlockSpec((r, sz, c), lambda i, cr: (i, 0, 0))],
            out_specs=pl.BlockSpec((r, sz, c), lambda i, cr: (i, 0, 0))),
        out_shape=jax.ShapeDtypeStruct((p * N_CHIPS, sz, c), grad.dtype),
        compiler_params=_cparams(1),
    )(core, grad.reshape(p * N_CHIPS, 2, sz, c), landed.reshape(p * N_CHIPS, sz, c))
    return out.reshape(p, N_CHIPS, sz, c)


def _chip_sum(parts, landed, chip, name, deps=()):
    p, _, sz, c = parts.shape
    r = _rows_per_step(p, sz * c)

    def body(chip_ref, a_ref, l_ref, o_ref):
        acc = a_ref[...].astype(F32)
        for j in range(3):
            acc = acc + l_ref[j].astype(F32)
        o_ref[...] = acc

    nd = len(deps)

    def ordered(chip_ref, a_ref, l_ref, *rest):
        body(chip_ref, a_ref, l_ref, rest[nd])

    return pl.pallas_call(
        ordered, name=name,
        grid_spec=pltpu.PrefetchScalarGridSpec(
            num_scalar_prefetch=1, grid=(p // r,),
            in_specs=[pl.BlockSpec((r, None, sz, c), lambda i, cr: (i, cr[0], 0, 0)),
                      pl.BlockSpec((3, r, sz, c), lambda i, cr: (0, i, 0, 0))] + [ANY] * nd,
            out_specs=pl.BlockSpec((r, sz, c), lambda i, cr: (i, 0, 0))),
        out_shape=jax.ShapeDtypeStruct((p, sz, c), F32),
        compiler_params=_cparams(1),
    )(chip, parts, landed, *deps)


def _sum_lead(a, name, out_dtype=F32):
    k = a.shape[0]
    rest = a.shape[1:]
    r, c = rest[-2], rest[-1]
    lead = math.prod(rest[:-2])
    a3 = a.reshape(k, lead * r, c)
    rows = lead * r
    tb = rows
    for cand in (512, 256, 128, 64, 32, 16, 8):
        if rows % cand == 0 and rows > cand:
            tb = cand
            break

    def body(a_ref, o_ref):
        acc = a_ref[0].astype(F32)
        for i in range(1, k):
            acc = acc + a_ref[i].astype(F32)
        o_ref[...] = acc.astype(out_dtype)

    out = pl.pallas_call(
        body, name=name, grid=(rows // tb,),
        out_shape=jax.ShapeDtypeStruct((rows, c), out_dtype),
        in_specs=[pl.BlockSpec((k, tb, c), lambda i: (0, i, 0))],
        out_specs=pl.BlockSpec((tb, c), lambda i: (i, 0)),
        compiler_params=_cparams(1),
    )(a3)
    return out.reshape(rest)


def _bands(t, causal):
    r = lax.broadcasted_iota(jnp.int32, (t, t + HALO), 0)
    col = lax.broadcasted_iota(jnp.int32, (t, t + HALO), 1)
    diff = r + HALO - col if causal else col - r
    return jnp.stack([jnp.where((diff >= 0) & (diff < w), 1.0, 0.0) for w in POOL_WINDOWS]).astype(BF16)


def _split_dot(band, v):
    hi = v.astype(BF16)
    lo = (v - hi.astype(F32)).astype(BF16)
    return _nn(band, hi) + _nn(band, lo)


def _mix_fwd(x, g, wp, b, sc, name, deps=()):
    s = x.shape[0]
    t = min(MIX_ROWS, s)
    rb = t // HALO

    def body(x_ref, xh_ref, g_ref, wp_ref, b_ref, sc_ref, band_ref, xo_ref, d_ref):
        i = pl.program_id(0)
        gg = g_ref[...]
        h, _ = _rms(x_ref[...], gg, D_MODEL)
        hh, _ = _rms(xh_ref[...], gg, D_MODEL)
        hh = jnp.where(i > 0, hh, 0.0)
        hext = jnp.concatenate([hh, h], axis=0)
        tok = i * t + lax.broadcasted_iota(jnp.int32, (t, 1), 0)
        for gi, w in enumerate(POOL_WINDOWS):
            sl = slice(gi * GROUP_DIM, (gi + 1) * GROUP_DIM)
            win = _split_dot(band_ref[gi], hext[:, sl])
            inv = 1.0 / jnp.minimum(tok + 1, w).astype(F32)
            dbf = (win * inv - h[:, sl]).astype(BF16)
            d_ref[:, sl] = dbf
            ypre = _nn(dbf, wp_ref[gi]) + b_ref[:, sl]
            xo_ref[:, sl] = x_ref[:, sl] + ypre * sc_ref[:, sl]

    return _pcall(
        body, (x, x, g, wp, b, sc, _bands(t, True)), deps, name=name, grid=(s // t,),
        out_shape=[jax.ShapeDtypeStruct((s, D_MODEL), F32), jax.ShapeDtypeStruct((s, D_MODEL), BF16)],
        in_specs=[_rows(t, D_MODEL),
                  pl.BlockSpec((HALO, D_MODEL), lambda i: (jnp.maximum(i * rb - 1, 0), 0)),
                  _full((1, D_MODEL)), _full((4, GROUP_DIM, GROUP_DIM)), _full((1, D_MODEL)), _full((1, D_MODEL)),
                  _full((4, t, t + HALO))],
        out_specs=[_rows(t, D_MODEL), _rows(t, D_MODEL)],
        compiler_params=_cparams(1, VMEM_MID),
    )


def _mix_bwd(x, dy, d, g, wp, b, sc, name, deps=()):
    s = x.shape[0]
    t = min(MIX_ROWS, s)
    rb = t // HALO
    nb = s // t
    last_halo = s // HALO - 1

    def body(x_ref, dy_ref, dyn_ref, d_ref, g_ref, wp_ref, b_ref, sc_ref, band_ref,
             dx_ref, dyp_ref, dsc_ref, db_ref, dln_ref):
        i = pl.program_id(0)
        x = x_ref[...]
        gg = g_ref[...]
        dy = dy_ref[...]
        sc = sc_ref[...]
        dyp32 = dy * sc
        dyp = dyp32.astype(BF16)
        dyph = (dyn_ref[...] * sc).astype(BF16)
        dyp_ref[...] = dyp
        tok = i * t + lax.broadcasted_iota(jnp.int32, (t + HALO, 1), 0)
        dh, dsc = [], []
        for gi, w in enumerate(POOL_WINDOWS):
            sl = slice(gi * GROUP_DIM, (gi + 1) * GROUP_DIM)
            ypre = _nn(d_ref[:, sl], wp_ref[gi]) + b_ref[:, sl]
            dsc.append(jnp.sum(dy[:, sl] * ypre, axis=0, keepdims=True))
            dd = _nt(dyp[:, sl], wp_ref[gi])
            ddh = jnp.where(i < nb - 1, _nt(dyph[:, sl], wp_ref[gi]), 0.0)
            inv = 1.0 / jnp.minimum(tok + 1, w).astype(F32)
            ddext = jnp.concatenate([dd, ddh], axis=0) * inv
            dh.append(_split_dot(band_ref[gi], ddext) - dd)
        dh = jnp.concatenate(dh, axis=1)
        _, r = _rms(x, gg, D_MODEL)
        dxn, dg = _rms_bwd(x, r, gg, dh, D_MODEL)
        dx_ref[...] = dy + dxn

        @pl.when(i == 0)
        def _():
            dsc_ref[...] = jnp.zeros_like(dsc_ref)
            db_ref[...] = jnp.zeros_like(db_ref)
            dln_ref[...] = jnp.zeros_like(dln_ref)

        dsc_ref[...] += jnp.concatenate(dsc, axis=1)
        db_ref[...] += jnp.sum(dyp32, axis=0, keepdims=True)
        dln_ref[...] += dg

    vec = jax.ShapeDtypeStruct((1, D_MODEL), F32)
    return _pcall(
        body, (x, dy, dy, d, g, wp, b, sc, _bands(t, False)), deps, name=name, grid=(nb,),
        out_shape=[jax.ShapeDtypeStruct((s, D_MODEL), F32), jax.ShapeDtypeStruct((s, D_MODEL), BF16), vec, vec, vec],
        in_specs=[_rows(t, D_MODEL), _rows(t, D_MODEL),
                  pl.BlockSpec((HALO, D_MODEL), lambda i: (jnp.minimum((i + 1) * rb, last_halo), 0)),
                  _rows(t, D_MODEL),
                  _full((1, D_MODEL)), _full((4, GROUP_DIM, GROUP_DIM)), _full((1, D_MODEL)), _full((1, D_MODEL)),
                  _full((4, t, t + HALO))],
        out_specs=[_rows(t, D_MODEL), _rows(t, D_MODEL), _full((1, D_MODEL)), _full((1, D_MODEL)), _full((1, D_MODEL))],
        compiler_params=_cparams(1, VMEM_MID),
    )


def _load_weights(w_hbm, w_vmem, sem):
    @pl.when(pl.program_id(0) == 0)
    def _():
        cp = pltpu.make_async_copy(w_hbm, w_vmem, sem)
        cp.start()
        cp.wait()


def _ffn_fwd(x, g, w, name):
    s = x.shape[0]
    t = min(512, s)

    def body(x_ref, g_ref, w_hbm, xo_ref, gate_ref, up_ref, w_ref, sem):
        _load_weights(w_hbm, w_ref, sem)
        x = x_ref[...]
        hn = _rms(x, g_ref[...], D_MODEL)[0].astype(BF16)
        acc = x
        for c in range(2):
            rs = slice(c * FF_HALF, (c + 1) * FF_HALF)
            gt = _nt(hn, w_ref[0, rs, :])
            up = _nt(hn, w_ref[1, rs, :])
            gate_ref[:, rs] = gt.astype(BF16)
            up_ref[:, rs] = up.astype(BF16)
            act = ((gt * _sigmoid(gt)) * up).astype(BF16)
            acc = acc + _nn(act, w_ref[2, rs, :])
        xo_ref[...] = acc

    hid = jax.ShapeDtypeStruct((s, D_FF), BF16)
    return pl.pallas_call(
        body, name=name, grid=(s // t,),
        out_shape=[jax.ShapeDtypeStruct((s, D_MODEL), F32), hid, hid],
        in_specs=[_rows(t, D_MODEL), _full((1, D_MODEL)), ANY],
        out_specs=[_rows(t, D_MODEL), _rows(t, D_FF), _rows(t, D_FF)],
        scratch_shapes=[pltpu.VMEM((3, D_FF, D_MODEL), BF16), pltpu.SemaphoreType.DMA],
        compiler_params=_cparams(1, VMEM_BIG),
    )(x, g, w)


def _ffn_bwd(x, dy, gate, up, g, w, name, deps=()):
    s = x.shape[0]
    t = min(256, s)

    def body(x_ref, dy_ref, gate_ref, up_ref, g_ref, w_hbm,
             dx_ref, act_ref, dg_ref, du_ref, hn_ref, dyb_ref, dln_ref, w_ref, sem):
        _load_weights(w_hbm, w_ref, sem)
        x = x_ref[...]
        gg = g_ref[...]
        y, r = _rms(x, gg, D_MODEL)
        hn = y.astype(BF16)
        hn_ref[...] = hn
        dy = dy_ref[...]
        dyb = dy.astype(BF16)
        dyb_ref[...] = dyb
        dh = jnp.zeros((t, D_MODEL), F32)
        for c in range(2):
            rs = slice(c * FF_HALF, (c + 1) * FF_HALF)
            gt = gate_ref[:, rs].astype(F32)
            u = up_ref[:, rs].astype(F32)
            sg = _sigmoid(gt)
            sl = gt * sg
            act_ref[:, rs] = (sl * u).astype(BF16)
            dact = _nt(dyb, w_ref[2, rs, :])
            dg = (dact * u * (sg * (1.0 + gt * (1.0 - sg)))).astype(BF16)
            du = (dact * sl).astype(BF16)
            dg_ref[:, rs] = dg
            du_ref[:, rs] = du
            dh = dh + _nn(dg, w_ref[0, rs, :]) + _nn(du, w_ref[1, rs, :])
        dxn, dgl = _rms_bwd(x, r, gg, dh, D_MODEL)
        dx_ref[...] = dy + dxn

        @pl.when(pl.program_id(0) == 0)
        def _():
            dln_ref[...] = jnp.zeros_like(dln_ref)

        dln_ref[...] += dgl

    hid = jax.ShapeDtypeStruct((s, D_FF), BF16)
    tok = jax.ShapeDtypeStruct((s, D_MODEL), BF16)
    return _pcall(
        body, (x, dy, gate, up, g, w), deps, name=name, grid=(s // t,),
        out_shape=[jax.ShapeDtypeStruct((s, D_MODEL), F32), hid, hid, hid, tok, tok,
                   jax.ShapeDtypeStruct((1, D_MODEL), F32)],
        in_specs=[_rows(t, D_MODEL), _rows(t, D_MODEL), _rows(t, D_FF), _rows(t, D_FF), _full((1, D_MODEL)), ANY],
        out_specs=[_rows(t, D_MODEL), _rows(t, D_FF), _rows(t, D_FF), _rows(t, D_FF),
                   _rows(t, D_MODEL), _rows(t, D_MODEL), _full((1, D_MODEL))],
        scratch_shapes=[pltpu.VMEM((3, D_FF, D_MODEL), BF16), pltpu.SemaphoreType.DMA],
        compiler_params=_cparams(1, VMEM_BIG),
    )


def _tn_matmul(a, b, into, p0, name, groups=1, m_chunk=None, deps=()):
    s = a.shape[0]
    m, n = a.shape[1] // groups, b.shape[1] // groups
    assert into.shape[1:] == (m, n)
    mc = m if m_chunk is None else m_chunk
    nm = m // mc
    t = min(1024, s)
    nt = s // t

    def body(a_ref, b_ref, into_ref, o_ref, acc):
        ti = pl.program_id(2)

        @pl.when(ti == 0)
        def _():
            acc[...] = jnp.zeros_like(acc)

        acc[...] += _tn(a_ref[...], b_ref[...])

        @pl.when(ti == nt - 1)
        def _():
            o_ref[...] = acc[...].astype(o_ref.dtype)

    return _pcall(
        body, (a, b, into), deps, name=name, grid=(groups, nm, nt),
        out_shape=jax.ShapeDtypeStruct(into.shape, into.dtype),
        in_specs=[pl.BlockSpec((t, mc), lambda gi, mi, ti: (ti, gi * nm + mi)),
                  pl.BlockSpec((t, n), lambda gi, mi, ti: (ti, gi)), ANY],
        out_specs=pl.BlockSpec((None, mc, n), lambda gi, mi, ti: (p0 + gi, mi, 0)),
        scratch_shapes=[pltpu.VMEM((mc, n), F32)],
        input_output_aliases={2: 0},
        compiler_params=_cparams(3, VMEM_BIG),
    )


def _rope_tables(positions):
    half = ROPE // 2
    inv = ROPE_THETA ** (-jnp.arange(half, dtype=F32) * 2.0 / ROPE)
    ang = positions.astype(F32)[:, None] * inv
    cos, sin = jnp.cos(ang), jnp.sin(ang)
    zero = jnp.zeros((positions.shape[0], LANES - ROPE), F32)
    return jnp.concatenate([cos, cos, zero], axis=1), jnp.concatenate([-sin, sin, zero], axis=1)


def _kv_specs(t):
    return [_full((1, D_MODEL)), _full((D_MODEL, KV_RANK)), _full((D_MODEL, LANES)), _full((1, KV_RANK)),
            _full((KV_RANK, N_HEADS * NOPE)), _full((KV_RANK, N_HEADS * V_DIM)),
            _full((1, NOPE)), _full((1, LANES)), _rows(t, LANES), _rows(t, LANES)]


def _kv_fwd(x, ln, wc, wpe, gl, wuk, wuv, gkn, gkr, cos, sin, name, deps=()):
    s = x.shape[0]
    t = min(PROJ_ROWS, s)

    def body(x_ref, ln_ref, wc_ref, wpe_ref, gl_ref, wuk_ref, wuv_ref, gkn_ref, gkr_ref, cos_ref, sin_ref,
             k_ref, v_ref):
        hn = _rms(x_ref[...], ln_ref[...], D_MODEL)[0].astype(BF16)
        clat = _nn(hn, wc_ref[...])
        kpe = _nn(hn, wpe_ref[...])
        cn = _rms(clat, gl_ref[...], KV_RANK)[0].astype(BF16)
        sspe = jnp.sum(kpe * kpe, axis=-1, keepdims=True)
        base = kpe * gkr_ref[...]
        rot = base * cos_ref[...] + _swap_halves(base, _swap_perm()) * sin_ref[...]
        kn_all = _nn(cn, wuk_ref[...])
        v_ref[...] = _nn(cn, wuv_ref[...]).astype(BF16)
        for h in range(N_HEADS):
            kn = kn_all[:, h * NOPE:(h + 1) * NOPE]
            r = lax.rsqrt((jnp.sum(kn * kn, axis=-1, keepdims=True) + sspe) * (1.0 / QK_DIM) + EPS)
            k_ref[:, h * QK_PAD:h * QK_PAD + NOPE] = ((kn * r) * gkn_ref[...]).astype(BF16)
            k_ref[:, h * QK_PAD + NOPE:(h + 1) * QK_PAD] = (rot * r).astype(BF16)

    return _pcall(
        body, (x, ln, wc, wpe, gl, wuk, wuv, gkn, gkr, cos, sin), deps, name=name, grid=(s // t,),
        out_shape=[jax.ShapeDtypeStruct((s, N_HEADS * QK_PAD), BF16), jax.ShapeDtypeStruct((s, N_HEADS * V_DIM), BF16)],
        in_specs=[_rows(t, D_MODEL)] + _kv_specs(t),
        out_specs=[_rows(t, N_HEADS * QK_PAD), _rows(t, N_HEADS * V_DIM)],
        compiler_params=_cparams(1, VMEM_MID),
    )


def _kv_bwd(x, dxin, dks, dvs, ln, wc, wpe, gl, wuk, wuv, gkn, gkr, cos, sin, name):
    s = x.shape[0]
    t = min(PROJ_ROWS, s)
    nk = len(dks)

    def body(*refs):
        x_ref, dxin_ref = refs[:2]
        dk_refs = refs[2:2 + nk]
        dv_refs = refs[2 + nk:2 + 2 * nk]
        (ln_ref, wc_ref, wpe_ref, gl_ref, wuk_ref, wuv_ref, gkn_ref, gkr_ref, cos_ref, sin_ref,
         dx_ref, hn_ref, cn_ref, dkn_ref, dvb_ref, dcc_ref, dpe_ref,
         dln_ref, dgl_ref, dgkn_ref, dgkr_ref) = refs[2 + 2 * nk:]
        x = x_ref[...]
        ln = ln_ref[...]
        y, rx = _rms(x, ln, D_MODEL)
        hn = y.astype(BF16)
        hn_ref[...] = hn
        clat = _nn(hn, wc_ref[...])
        kpe = _nn(hn, wpe_ref[...])
        gl = gl_ref[...]
        cy, rc = _rms(clat, gl, KV_RANK)
        cn = cy.astype(BF16)
        cn_ref[...] = cn
        sspe = jnp.sum(kpe * kpe, axis=-1, keepdims=True)
        cs, sn, perm = cos_ref[...], sin_ref[...], _swap_perm()
        gkn, gkr = gkn_ref[...], gkr_ref[...]
        base = kpe * gkr
        rot = base * cs + _swap_halves(base, perm) * sn
        dkr_sum = jnp.zeros((t, LANES), F32)
        coef_sum = jnp.zeros((t, 1), F32)
        dgkn = jnp.zeros((1, NOPE), F32)
        kn_all = _nn(cn, wuk_ref[...])
        dkn_heads = []
        for h in range(N_HEADS):
            kn = kn_all[:, h * NOPE:(h + 1) * NOPE]
            r = lax.rsqrt((jnp.sum(kn * kn, axis=-1, keepdims=True) + sspe) * (1.0 / QK_DIM) + EPS)
            lo, mid, hi = h * QK_PAD, h * QK_PAD + NOPE, (h + 1) * QK_PAD
            dko = dk_refs[0][:, lo:mid]
            dkr = dk_refs[0][:, mid:hi]
            for j in range(1, nk):
                dko = dko + dk_refs[j][:, lo:mid]
                dkr = dkr + dk_refs[j][:, mid:hi]
            un = dko * gkn
            sm = (jnp.sum(kn * un, axis=-1, keepdims=True) + jnp.sum(rot * dkr, axis=-1, keepdims=True)) * (1.0 / QK_DIM)
            coef = r * r * r * sm
            dkn = (r * un - kn * coef).astype(BF16)
            dkr_sum = dkr_sum + r * dkr
            coef_sum = coef_sum + coef
            dgkn = dgkn + jnp.sum(dko * (kn * r), axis=0, keepdims=True)
            dkn_heads.append(dkn)
        dkn_all = jnp.concatenate(dkn_heads, axis=1)
        dkn_ref[...] = dkn_all
        dv_all = dv_refs[0][...]
        for j in range(1, nk):
            dv_all = dv_all + dv_refs[j][...]
        dvb = dv_all.astype(BF16)
        dvb_ref[...] = dvb
        dc = _nt(dkn_all, wuk_ref[...]) + _nt(dvb, wuv_ref[...])
        dz = dkr_sum * cs - _swap_halves(dkr_sum, perm) * sn
        dkpe = dz * gkr - kpe * coef_sum
        dgkr = jnp.sum(dz * kpe, axis=0, keepdims=True)
        dclat, dgl = _rms_bwd(clat, rc, gl, dc, KV_RANK)
        dcc = dclat.astype(BF16)
        dpe = dkpe.astype(BF16)
        dcc_ref[...] = dcc
        dpe_ref[...] = dpe
        dhn = _nt(dcc, wc_ref[...]) + _nt(dpe, wpe_ref[...])
        dxn, dln = _rms_bwd(x, rx, ln, dhn, D_MODEL)
        dx_ref[...] = dxin_ref[...] + dxn

        @pl.when(pl.program_id(0) == 0)
        def _():
            dln_ref[...] = jnp.zeros_like(dln_ref)
            dgl_ref[...] = jnp.zeros_like(dgl_ref)
            dgkn_ref[...] = jnp.zeros_like(dgkn_ref)
            dgkr_ref[...] = jnp.zeros_like(dgkr_ref)

        dln_ref[...] += dln
        dgl_ref[...] += dgl
        dgkn_ref[...] += dgkn
        dgkr_ref[...] += dgkr

    def tok(cols, dt):
        return jax.ShapeDtypeStruct((s, cols), dt)

    def vec(cols):
        return jax.ShapeDtypeStruct((1, cols), F32)

    return pl.pallas_call(
        body, name=name, grid=(s // t,),
        out_shape=[tok(D_MODEL, F32), tok(D_MODEL, BF16), tok(KV_RANK, BF16), tok(N_HEADS * NOPE, BF16),
                   tok(N_HEADS * V_DIM, BF16), tok(KV_RANK, BF16), tok(LANES, BF16),
                   vec(D_MODEL), vec(KV_RANK), vec(NOPE), vec(LANES)],
        in_specs=[_rows(t, D_MODEL), _rows(t, D_MODEL)] + [_rows(t, N_HEADS * QK_PAD)] * nk
                 + [_rows(t, N_HEADS * V_DIM)] * nk + _kv_specs(t),
        out_specs=[_rows(t, D_MODEL), _rows(t, D_MODEL), _rows(t, KV_RANK), _rows(t, N_HEADS * NOPE),
                   _rows(t, N_HEADS * V_DIM), _rows(t, KV_RANK), _rows(t, LANES),
                   _full((1, D_MODEL)), _full((1, KV_RANK)), _full((1, NOPE)), _full((1, LANES))],
        compiler_params=_cparams(1, VMEM_BIG),
    )(x, dxin, *dks, *dvs, ln, wc, wpe, gl, wuk, wuv, gkn, gkr, cos, sin)


def _q_specs(t):
    return [_full((1, D_MODEL)), _full((D_MODEL, Q_RANK)), _full((1, Q_RANK)), _full((N_HEADS, Q_RANK, QK_PAD)),
            _full((1, NOPE)), _full((1, LANES)), _rows(t, LANES), _rows(t, LANES)]


def _q_fwd(x, ln, wdq, gql, wuq, gqn, gqr, cos, sin, name, deps=()):
    s = x.shape[0]
    t = min(PROJ_ROWS, s)

    def body(x_ref, ln_ref, wdq_ref, gql_ref, wuq_ref, gqn_ref, gqr_ref, cos_ref, sin_ref, q_ref):
        hn = _rms(x_ref[...], ln_ref[...], D_MODEL)[0].astype(BF16)
        cqn = _rms(_nn(hn, wdq_ref[...]), gql_ref[...], Q_RANK)[0].astype(BF16)
        cs, sn, perm = cos_ref[...], sin_ref[...], _swap_perm()
        for h in range(N_HEADS):
            qa = _nn(cqn, wuq_ref[h])
            r = lax.rsqrt(jnp.sum(qa * qa, axis=-1, keepdims=True) * (1.0 / QK_DIM) + EPS)
            q_ref[:, h * QK_PAD:h * QK_PAD + NOPE] = ((qa[:, :NOPE] * r) * gqn_ref[...]).astype(BF16)
            z = (qa[:, NOPE:] * r) * gqr_ref[...]
            q_ref[:, h * QK_PAD + NOPE:(h + 1) * QK_PAD] = (z * cs + _swap_halves(z, perm) * sn).astype(BF16)

    return _pcall(
        body, (x, ln, wdq, gql, wuq, gqn, gqr, cos, sin), deps, name=name, grid=(s // t,),
        out_shape=jax.ShapeDtypeStruct((s, N_HEADS * QK_PAD), BF16),
        in_specs=[_rows(t, D_MODEL)] + _q_specs(t),
        out_specs=_rows(t, N_HEADS * QK_PAD),
        compiler_params=_cparams(1, VMEM_MID),
    )


def _q_bwd(x, dxin, dq, ln, wdq, gql, wuq, gqn, gqr, cos, sin, name):
    s = x.shape[0]
    t = min(PROJ_ROWS, s)

    def body(x_ref, dxin_ref, dq_ref, ln_ref, wdq_ref, gql_ref, wuq_ref, gqn_ref, gqr_ref, cos_ref, sin_ref,
             dx_ref, hn_ref, cqn_ref, dqa_ref, dcq_ref, dln_ref, dgql_ref, dgqn_ref, dgqr_ref):
        x = x_ref[...]
        ln = ln_ref[...]
        y, rx = _rms(x, ln, D_MODEL)
        hn = y.astype(BF16)
        hn_ref[...] = hn
        cqp = _nn(hn, wdq_ref[...])
        gql = gql_ref[...]
        cy, rc = _rms(cqp, gql, Q_RANK)
        cqn = cy.astype(BF16)
        cqn_ref[...] = cqn
        cs, sn, perm = cos_ref[...], sin_ref[...], _swap_perm()
        gqn, gqr = gqn_ref[...], gqr_ref[...]
        dcq = jnp.zeros((t, Q_RANK), F32)
        dgqn = jnp.zeros((1, NOPE), F32)
        dgqr = jnp.zeros((1, LANES), F32)
        for h in range(N_HEADS):
            qa = _nn(cqn, wuq_ref[h])
            qn, qr = qa[:, :NOPE], qa[:, NOPE:]
            r = lax.rsqrt(jnp.sum(qa * qa, axis=-1, keepdims=True) * (1.0 / QK_DIM) + EPS)
            dqo = dq_ref[:, h * QK_PAD:h * QK_PAD + NOPE]
            dqr = dq_ref[:, h * QK_PAD + NOPE:(h + 1) * QK_PAD]
            dz = dqr * cs - _swap_halves(dqr, perm) * sn
            un = dqo * gqn
            ur = dz * gqr
            sm = (jnp.sum(qn * un, axis=-1, keepdims=True) + jnp.sum(qr * ur, axis=-1, keepdims=True)) * (1.0 / QK_DIM)
            coef = r * r * r * sm
            dqa = jnp.concatenate([r * un - qn * coef, r * ur - qr * coef], axis=1).astype(BF16)
            dgqn = dgqn + jnp.sum(dqo * (qn * r), axis=0, keepdims=True)
            dgqr = dgqr + jnp.sum(dz * (qr * r), axis=0, keepdims=True)
            dqa_ref[:, h * QK_PAD:(h + 1) * QK_PAD] = dqa
            dcq = dcq + _nt(dqa, wuq_ref[h])
        dcqp, dgql = _rms_bwd(cqp, rc, gql, dcq, Q_RANK)
        dcqb = dcqp.astype(BF16)
        dcq_ref[...] = dcqb
        dhn = _nt(dcqb, wdq_ref[...])
        dxn, dln = _rms_bwd(x, rx, ln, dhn, D_MODEL)
        dx_ref[...] = dxin_ref[...] + dxn

        @pl.when(pl.program_id(0) == 0)
        def _():
            dln_ref[...] = jnp.zeros_like(dln_ref)
            dgql_ref[...] = jnp.zeros_like(dgql_ref)
            dgqn_ref[...] = jnp.zeros_like(dgqn_ref)
            dgqr_ref[...] = jnp.zeros_like(dgqr_ref)

        dln_ref[...] += dln
        dgql_ref[...] += dgql
        dgqn_ref[...] += dgqn
        dgqr_ref[...] += dgqr

    def tok(cols, dt):
        return jax.ShapeDtypeStruct((s, cols), dt)

    def vec(cols):
        return jax.ShapeDtypeStruct((1, cols), F32)

    return pl.pallas_call(
        body, name=name, grid=(s // t,),
        out_shape=[tok(D_MODEL, F32), tok(D_MODEL, BF16), tok(Q_RANK, BF16), tok(N_HEADS * QK_PAD, BF16),
                   tok(Q_RANK, BF16), vec(D_MODEL), vec(Q_RANK), vec(NOPE), vec(LANES)],
        in_specs=[_rows(t, D_MODEL), _rows(t, D_MODEL), _rows(t, N_HEADS * QK_PAD)] + _q_specs(t),
        out_specs=[_rows(t, D_MODEL), _rows(t, D_MODEL), _rows(t, Q_RANK), _rows(t, N_HEADS * QK_PAD),
                   _rows(t, Q_RANK), _full((1, D_MODEL)), _full((1, Q_RANK)), _full((1, NOPE)), _full((1, LANES))],
        compiler_params=_cparams(1, VMEM_MID),
    )(x, dxin, dq, ln, wdq, gql, wuq, gqn, gqr, cos, sin)


SM_SCALE = 1.0 / math.sqrt(QK_DIM)
LOG2_E = math.log2(math.e)
EXP2_SCALE = SM_SCALE * LOG2_E
NEG = -1e30


def _diag_mask(t):
    qpos = lax.broadcasted_iota(jnp.int32, (t, t), 0)
    kpos = lax.broadcasted_iota(jnp.int32, (t, t), 1)
    return lax.shift_right_logical(kpos, 6) <= lax.shift_right_logical(qpos, 6)


def _att_fwd(q, k, v, name):
    s = q.shape[0]
    t = min(512, s)
    nb = s // t

    def body(q_ref, k_ref, v_ref, o_ref, lse_ref):
        qi = pl.program_id(1)
        qq = q_ref[...]

        def block(ki, carry, masked):
            m_old, l_old, acc = carry
            rows = pl.ds(pl.multiple_of(ki * t, t), t)
            sc = _nt(qq, k_ref[rows, :])
            if masked:
                sc = jnp.where(_diag_mask(t), sc, NEG)
            m_new = jnp.maximum(m_old, jnp.max(sc, axis=-1, keepdims=True))
            p = jnp.exp2((sc - m_new) * EXP2_SCALE)
            alpha = jnp.exp2((m_old - m_new) * EXP2_SCALE)
            l_new = alpha * l_old + jnp.sum(p, axis=-1, keepdims=True)
            acc = alpha * acc + _nn(p.astype(BF16), v_ref[rows, :])
            return m_new, l_new, acc

        init = (jnp.full((t, 1), NEG, F32), jnp.zeros((t, 1), F32), jnp.zeros((t, V_DIM), F32))
        def pair(k0, c):
            return block(k0 + 1, block(k0, c, False), False)

        carry = lax.fori_loop(0, qi // 4, lambda j, c: pair(4 * j + 2, pair(4 * j, c)), init)
        done = 4 * (qi // 4)
        carry = lax.cond((qi & 2) != 0, lambda c: pair(done, c), lambda c: c, carry)
        m_fin, l_fin, acc = lax.cond(
            (qi & 1) != 0, lambda c: block(qi, block(qi - 1, c, False), True), lambda c: block(qi, c, True), carry)
        o_ref[...] = (acc / l_fin).astype(BF16)
        lse_ref[...] = jnp.broadcast_to(m_fin * SM_SCALE + jnp.log(l_fin), (t, LANES))

    return pl.pallas_call(
        body, name=name, grid=(N_HEADS, nb),
        out_shape=[jax.ShapeDtypeStruct((s, N_HEADS * V_DIM), BF16), jax.ShapeDtypeStruct((s, N_HEADS * LANES), F32)],
        in_specs=[pl.BlockSpec((t, QK_PAD), lambda h, qi: (qi, h)),
                  pl.BlockSpec((s, QK_PAD), lambda h, qi: (0, h)),
                  pl.BlockSpec((s, V_DIM), lambda h, qi: (0, h))],
        out_specs=[pl.BlockSpec((t, V_DIM), lambda h, qi: (qi, h)),
                   pl.BlockSpec((t, LANES), lambda h, qi: (qi, h))],
        compiler_params=_cparams(2, VMEM_MID),
    )(q, k, v)


def _att_bwd(q, k, v, do, stats, name, deps=()):
    s = q.shape[0]
    t = min(512, s)
    nb = s // t

    def body(q_ref, k_ref, v_ref, do_ref, st_ref, dq_ref, dk_ref, dv_ref):
        ki = pl.program_id(1)
        kk, vv = k_ref[...], v_ref[...]

        @pl.when(ki == 0)
        def _():
            dq_ref[...] = jnp.zeros_like(dq_ref)

        def block(qi, carry, masked):
            dk, dv = carry
            rows = pl.ds(pl.multiple_of(qi * t, t), t)
            qq, dob = q_ref[rows, :], do_ref[rows, :]
            sc = _nt(qq, kk)
            if masked:
                sc = jnp.where(_diag_mask(t), sc, NEG)
            st = st_ref[rows, :]
            p = jnp.exp2(sc * EXP2_SCALE - st[:, 0:1])
            dp = _nt(dob, vv)
            ds = (p * (dp - st[:, 1:2])).astype(BF16)
            dq_ref[rows, :] += _nn(ds, kk)
            return dk + _tn(ds, qq), dv + _tn(p.astype(BF16), dob)

        rest = nb - 1 - ki
        zeros = (jnp.zeros((t, QK_PAD), F32), jnp.zeros((t, V_DIM), F32))
        carry = lax.cond((rest & 1) != 0, lambda c: block(nb - 1, block(ki, c, True), False),
                         lambda c: block(ki, c, True), zeros)

        def pair(q0, c):
            return block(q0 + 1, block(q0, c, False), False)

        carry = lax.fori_loop(0, rest // 4, lambda j, c: pair(ki + 4 * j + 3, pair(ki + 4 * j + 1, c)), carry)
        done = ki + 1 + 4 * (rest // 4)
        dk, dv = lax.cond((rest & 2) != 0, lambda c: pair(done, c), lambda c: c, carry)
        dk_ref[...] = dk * SM_SCALE
        dv_ref[...] = dv

        @pl.when(ki == nb - 1)
        def _():
            dq_ref[...] = dq_ref[...] * SM_SCALE

    def head(h, ki):
        return (0, h)

    def kblock(h, ki):
        return (ki, h)

    return _pcall(
        body, (q, k, v, do, stats), deps, name=name, grid=(N_HEADS, nb),
        out_shape=[jax.ShapeDtypeStruct((s, N_HEADS * QK_PAD), F32), jax.ShapeDtypeStruct((s, N_HEADS * QK_PAD), F32),
                   jax.ShapeDtypeStruct((s, N_HEADS * V_DIM), F32)],
        in_specs=[pl.BlockSpec((s, QK_PAD), head), pl.BlockSpec((t, QK_PAD), kblock), pl.BlockSpec((t, V_DIM), kblock),
                  pl.BlockSpec((s, V_DIM), head), pl.BlockSpec((s, LANES), head)],
        out_specs=[pl.BlockSpec((s, QK_PAD), head), pl.BlockSpec((t, QK_PAD), kblock), pl.BlockSpec((t, V_DIM), kblock)],
        compiler_params=_cparams(2, VMEM_MID),
    )


def _o_fwd(x, o, wo, name):
    s = x.shape[0]
    t = min(512, s)

    def body(x_ref, o_ref, wo_ref, xo_ref):
        xo_ref[...] = x_ref[...] + _nn(o_ref[...], wo_ref[...])

    return pl.pallas_call(
        body, name=name, grid=(s // t,),
        out_shape=jax.ShapeDtypeStruct((s, D_MODEL), F32),
        in_specs=[_rows(t, D_MODEL), _rows(t, D_MODEL), _full((D_MODEL, D_MODEL))],
        out_specs=_rows(t, D_MODEL),
        compiler_params=_cparams(1, VMEM_MID),
    )(x, o, wo)


def _o_bwd(dx, wo, o, lse, name, deps=()):
    s = dx.shape[0]
    t = min(512, s)

    def body(dx_ref, wo_ref, o_ref, lse_ref, do_ref, dxb_ref, st_ref):
        dxb = dx_ref[...].astype(BF16)
        dxb_ref[...] = dxb
        dob = _nt(dxb, wo_ref[...]).astype(BF16)
        do_ref[...] = dob
        lane = lax.broadcasted_iota(jnp.int32, (t, LANES), 1)
        for h in range(N_HEADS):
            sl = slice(h * V_DIM, (h + 1) * V_DIM)
            dsum = jnp.sum(dob[:, sl].astype(F32) * o_ref[:, sl].astype(F32), axis=-1, keepdims=True)
            st_ref[:, sl] = jnp.where(lane == 0, lse_ref[:, sl] * LOG2_E, jnp.where(lane == 1, dsum, 0.0))

    tok = jax.ShapeDtypeStruct((s, D_MODEL), BF16)
    return _pcall(
        body, (dx, wo, o, lse), deps, name=name, grid=(s // t,),
        out_shape=[tok, tok, jax.ShapeDtypeStruct((s, N_HEADS * LANES), F32)],
        in_specs=[_rows(t, D_MODEL), _full((D_MODEL, D_MODEL)), _rows(t, D_MODEL), _rows(t, N_HEADS * LANES)],
        out_specs=[_rows(t, D_MODEL), _rows(t, D_MODEL), _rows(t, N_HEADS * LANES)],
        compiler_params=_cparams(1, VMEM_MID),
    )


def _loss_head(y, target, name):
    s = y.shape[0]
    t = min(512, s)

    def body(y_ref, t_ref, dy_ref, sq_ref):
        e = y_ref[...] - t_ref[...]
        dy_ref[...] = e * (1.0 / D_MODEL)

        @pl.when(pl.program_id(0) == 0)
        def _():
            sq_ref[...] = jnp.zeros_like(sq_ref)

        sq_ref[...] += jnp.sum(e * e, axis=0, keepdims=True)

    return pl.pallas_call(
        body, name=name, grid=(s // t,),
        out_shape=[jax.ShapeDtypeStruct((s, D_MODEL), F32), jax.ShapeDtypeStruct((1, D_MODEL), F32)],
        in_specs=[_rows(t, D_MODEL), _rows(t, D_MODEL)],
        out_specs=[_rows(t, D_MODEL), _full((1, D_MODEL))],
        compiler_params=_cparams(1),
    )(y, target)


def _adamw(w, g, m, v, name):
    shape = w.shape
    c = shape[-1]
    r = math.prod(shape[:-1])
    tb = r
    for cand in (512, 256, 128):
        if r % cand == 0 and r > cand:
            tb = cand
            break

    def body(w_ref, g_ref, m_ref, v_ref, d_ref, mo_ref, vo_ref):
        gr = g_ref[...]
        mn = ADAM_B1 * m_ref[...] + (1.0 - ADAM_B1) * gr
        vn = ADAM_B2 * v_ref[...] + (1.0 - ADAM_B2) * (gr * gr)
        m_hat = mn / (1.0 - ADAM_B1 ** ADAM_STEP)
        v_hat = vn / (1.0 - ADAM_B2 ** ADAM_STEP)
        d_ref[...] = -ADAM_LR * (m_hat / (jnp.sqrt(v_hat) + ADAM_EPS) + ADAM_WD * w_ref[...])
        mo_ref[...] = mn
        vo_ref[...] = vn

    spec = pl.BlockSpec((tb, c), lambda i: (i, 0))
    flat = jax.ShapeDtypeStruct((r, c), F32)
    outs = pl.pallas_call(
        body, name=name, grid=(r // tb,),
        out_shape=[flat, flat, flat],
        in_specs=[spec] * 4, out_specs=[spec] * 3,
        compiler_params=_cparams(1),
    )(w.reshape(r, c), g.reshape(r, c), m.reshape(r, c), v.reshape(r, c))
    return [a.reshape(shape) for a in outs]


def _pad_cols(a, width):
    return jnp.pad(a, [(0, 0)] * (a.ndim - 1) + [(0, width - a.shape[-1])])


def _owner_view(a, sz):
    return a.reshape(a.shape[0], N_CHIPS, 2, sz, a.shape[-1])


def kernel(x, positions, ln_mix_a, w_pool, b_pool, pool_scale, ln_ffn, w_gate, w_up, w_down, ln_kv, w_dkv, g_kv_latent, w_uk, w_uv, g_k, ln_mix_b, w_dq, g_q_latent, w_uq, g_q, w_o, loss_target, m_ln_mix_a, m_w_pool, m_b_pool, m_pool_scale, m_ln_ffn, m_w_gate, m_w_up, m_w_down, m_ln_kv, m_w_dkv, m_g_kv_latent, m_w_uk, m_w_uv, m_g_k, m_ln_mix_b, m_w_dq, m_g_q_latent, m_w_uq, m_g_q, m_w_o, v_ln_mix_a, v_w_pool, v_b_pool, v_pool_scale, v_ln_ffn, v_w_gate, v_w_up, v_w_down, v_ln_kv, v_w_dkv, v_g_kv_latent, v_w_uk, v_w_uv, v_g_k, v_ln_mix_b, v_w_dq, v_g_q_latent, v_w_uq, v_g_q, v_w_o):
    weights = dict(ln_mix_a=ln_mix_a, w_pool=w_pool, b_pool=b_pool, pool_scale=pool_scale, ln_ffn=ln_ffn,
                   w_gate=w_gate, w_up=w_up, w_down=w_down, ln_kv=ln_kv, w_dkv=w_dkv, g_kv_latent=g_kv_latent,
                   w_uk=w_uk, w_uv=w_uv, g_k=g_k, ln_mix_b=ln_mix_b, w_dq=w_dq, g_q_latent=g_q_latent,
                   w_uq=w_uq, g_q=g_q, w_o=w_o)
    mom1 = dict(ln_mix_a=m_ln_mix_a, w_pool=m_w_pool, b_pool=m_b_pool, pool_scale=m_pool_scale, ln_ffn=m_ln_ffn,
                w_gate=m_w_gate, w_up=m_w_up, w_down=m_w_down, ln_kv=m_ln_kv, w_dkv=m_w_dkv,
                g_kv_latent=m_g_kv_latent, w_uk=m_w_uk, w_uv=m_w_uv, g_k=m_g_k, ln_mix_b=m_ln_mix_b, w_dq=m_w_dq,
                g_q_latent=m_g_q_latent, w_uq=m_w_uq, g_q=m_g_q, w_o=m_w_o)
    mom2 = dict(ln_mix_a=v_ln_mix_a, w_pool=v_w_pool, b_pool=v_b_pool, pool_scale=v_pool_scale, ln_ffn=v_ln_ffn,
                w_gate=v_w_gate, w_up=v_w_up, w_down=v_w_down, ln_kv=v_ln_kv, w_dkv=v_w_dkv,
                g_kv_latent=v_g_kv_latent, w_uk=v_w_uk, w_uv=v_w_uv, g_k=v_g_k, ln_mix_b=v_ln_mix_b, w_dq=v_w_dq,
                g_q_latent=v_g_q_latent, w_uq=v_w_uq, g_q=v_g_q, w_o=v_w_o)
    names = list(weights)
    dev = 4 * lax.axis_index("x") + 2 * lax.axis_index("y") + lax.axis_index("c")
    core = lax.axis_index("c").astype(jnp.int32).reshape(1)
    chip = (2 * lax.axis_index("x") + lax.axis_index("y")).astype(jnp.int32).reshape(1)

    xs = x[0]
    target = loss_target[0]
    cos, sin = _rope_tables(positions[0])

    def placed(shard):
        buf = lax.empty((shard.shape[0], N_DEV) + shard.shape[1:], shard.dtype)
        return lax.dynamic_update_slice(buf, shard[:, None], (0, dev, 0, 0))

    def ffn_shard(l):
        return jnp.stack([w_gate[l].T, w_up[l].T, w_down[l]]).astype(BF16)

    groups = {"ffn0": [placed(ffn_shard(0))]}
    small_sh = jnp.concatenate([ln_mix_a.reshape(1, -1), pool_scale.reshape(1, -1), b_pool.reshape(1, -1)], axis=1)
    wp_g, small_g = _all_gather([w_pool.astype(BF16), small_sh], [2, 0], "gather_first")
    wp_all = wp_g.reshape(2, 4, GROUP_DIM, GROUP_DIM)
    small_g = small_g.reshape(N_DEV, 3, 2, LANES)
    ln_a_all = small_g[:, 0].transpose(1, 0, 2).reshape(2, 1, D_MODEL)
    sc_all = small_g[:, 1].transpose(1, 0, 2).reshape(2, 1, D_MODEL)
    bp_all = small_g[:, 2].reshape(N_DEV, 2, 4, 32).transpose(1, 2, 0, 3).reshape(2, 1, D_MODEL)
    sp0 = _copies_start(groups["ffn0"], 1, _gather_spread, "spread_ffn0", deps=[small_g])
    zero = sp0[3][0, 0].astype(BF16)
    for l in (1, 2, 3):
        groups[f"ffn{l}"] = [placed(ffn_shard(l) + zero)]
    groups["att"] = [placed(a.astype(BF16) + zero) for a in (
        w_dkv[None, :, :KV_RANK], _pad_cols(w_dkv[None, :, KV_RANK:], LANES), w_uk[None], w_uv[None],
        w_dq, _pad_cols(w_uq, QK_PAD), w_o)]

    def spread_start(nm, deps):
        return _copies_start(groups[nm], len(groups[nm]), _gather_spread, f"spread_{nm}", deps=deps)

    def spread_wait(nm, state, after):
        ssem, rsem, bufs, _ = state
        return _copies_wait(bufs, ssem, rsem, after, _blocks_moved(4), f"spread_done_{nm}")

    def relay_start(nm, bufs, deps=()):
        return _copies_start(bufs, len(bufs), _gather_relay, f"relay_{nm}", deps=deps)

    def relay_wait(nm, state, after):
        ssem, rsem, bufs, _ = state
        return _copies_wait(bufs, ssem, rsem, after, _blocks_moved(3), f"relay_done_{nm}")

    gkn = g_k[:NOPE].reshape(1, NOPE)
    gkr = _pad_cols(g_k[NOPE:].reshape(1, ROPE), LANES)
    gl = g_kv_latent.reshape(1, KV_RANK)
    lnkv = ln_kv.reshape(1, D_MODEL)

    x_in, x_mid, pooled, gates, ups, w_ffn = [], [], [], [], [], []
    qs, outs, lses = [], [], []

    def mixer(l, cur, deps):
        x_in.append(cur)
        mid, dsave = _mix_fwd(cur, ln_a_all[l], wp_all[l], bp_all[l], sc_all[l], f"mix_fwd{l}", deps=deps)
        pooled.append(dsave)
        x_mid.append(mid)
        return mid

    def q_args(j):
        return (ln_mix_b[j].reshape(1, -1), wdq_all[j], g_q_latent[j].reshape(1, -1), wuq_all[j],
                g_q[j, :NOPE].reshape(1, -1), _pad_cols(g_q[j, NOPE:].reshape(1, -1), LANES), cos, sin)

    def attention(j, cur, deps):
        x_in.append(cur)
        q = _q_fwd(cur, *q_args(j), f"q_fwd{j}", deps=deps)
        o, lse = _att_fwd(q, k_sh, v_sh, f"att_fwd{j}")
        mid = _o_fwd(cur, o, wo_all[j], f"o_fwd{j}")
        qs.append(q)
        outs.append(o)
        lses.append(lse)
        x_mid.append(mid)
        return mid

    def ffn(l, mid, relayed):
        w_l = relayed[0].reshape(3, D_FF, D_MODEL)
        w_ffn.append(w_l)
        cur, gate, up = _ffn_fwd(mid, ln_ffn[l].reshape(1, -1), w_l, f"ffn_fwd{l}")
        gates.append(gate)
        ups.append(up)
        return cur

    mid = mixer(0, xs, [sp0[3]])
    prepared = [buf for nm in ("ffn1", "att", "ffn2", "ffn3") for buf in groups[nm]]
    landed0 = spread_wait("ffn0", sp0, [mid] + prepared)
    sp1 = spread_start("ffn1", [landed0[0]])
    rl0 = relay_start("ffn0", landed0, [sp1[3]])
    cur = ffn(0, mid, relay_wait("ffn0", rl0, rl0[3]))

    landed1 = spread_wait("ffn1", sp1, cur)
    sp_att = spread_start("att", [landed1[0]])
    sp2 = spread_start("ffn2", [landed1[0]])
    rl1 = relay_start("ffn1", landed1, [sp_att[3], sp2[3]])
    mid = mixer(1, cur, [rl1[3]])
    cur = ffn(1, mid, relay_wait("ffn1", rl1, mid))
    x_kv = cur

    landed_att = spread_wait("att", sp_att, cur)
    landed2 = spread_wait("ffn2", sp2, cur)
    sp3 = spread_start("ffn3", [landed2[0]])
    rl_att = relay_start("att", landed_att, [sp3[3]])
    rl2 = relay_start("ffn2", landed2, [sp3[3]])
    att_bufs = relay_wait("att", rl_att, rl2[3])
    wc = att_bufs[0].reshape(D_MODEL, KV_RANK)
    wpe = att_bufs[1].reshape(D_MODEL, LANES)
    wuk_g = att_bufs[2].reshape(N_HEADS, KV_RANK, NOPE).transpose(1, 0, 2).reshape(KV_RANK, N_HEADS * NOPE)
    wuv_g = att_bufs[3].reshape(N_HEADS, KV_RANK, V_DIM).transpose(1, 0, 2).reshape(KV_RANK, N_HEADS * V_DIM)
    wdq_all = att_bufs[4].reshape(2, D_MODEL, Q_RANK)
    wuq_all = att_bufs[5]
    wo_all = att_bufs[6].reshape(2, D_MODEL, D_MODEL)
    k_sh, v_sh = _kv_fwd(cur, lnkv, wc, wpe, gl, wuk_g, wuv_g, gkn, gkr, cos, sin, "kv_fwd")
    mid = attention(0, cur, [])
    cur = ffn(2, mid, relay_wait("ffn2", rl2, mid))

    landed3 = spread_wait("ffn3", sp3, cur)
    rl3 = relay_start("ffn3", landed3)
    mid = attention(1, cur, [rl3[3]])
    cur = ffn(3, mid, relay_wait("ffn3", rl3, mid))

    dx, sq_cols = _loss_head(cur, target, "loss_head")

    small = {}
    sizes = dict(ffn0=FF_SHARD, ffn1=FF_SHARD, ffn2=FF_SHARD, ffn3=FF_SHARD, wo=128, kv512=128, dkv_pe=128,
                 wdq=128, wuqT=QK_PAD, wpool=32)
    big = dict(wo=lax.empty((2, D_MODEL, D_MODEL), BF16), kv512=lax.empty((3, D_MODEL, KV_RANK), BF16),
               dkv_pe=lax.empty((1, D_MODEL, LANES), BF16), wdq=lax.empty((2, D_MODEL, Q_RANK), BF16),
               wuqT=lax.empty((2, N_HEADS * QK_PAD, Q_RANK), BF16), wpool=lax.empty((8, GROUP_DIM, GROUP_DIM), BF16))
    for l in range(4):
        big[f"ffn{l}"] = lax.empty((3, D_FF, D_MODEL), BF16)
    red = {}

    def pair_start(nms, tag):
        arrs = []
        for nm in nms:
            view = _owner_view(big[nm], sizes[nm])
            arrs += [view, lax.empty((view.shape[0], N_CHIPS) + view.shape[3:], BF16)]
        return nms, tag, _copies_start(arrs, len(nms), _pair_send, f"pair_start_{tag}")

    def chip_start(state, after):
        nms, tag, (ssem, rsem, arrs, _) = state
        arrs = _copies_wait(arrs, ssem, rsem, after, _landed, f"pair_done_{tag}")
        out = []
        for t, nm in enumerate(nms):
            part = _pair_sum(arrs[2 * t], arrs[2 * t + 1], core, f"pair_sum_{nm}")
            out += [part, lax.empty((3, part.shape[0]) + part.shape[2:], BF16)]
        return nms, tag, _copies_start(out, len(nms), _chip_send, f"chip_start_{tag}")

    deferred = []
    updates = {}

    def chip_finish(state, after, defer=False):
        nms, tag, (ssem, rsem, arrs, _) = state
        arrs = _copies_wait(arrs, ssem, rsem, after, _landed, f"chip_done_{tag}")
        for t, nm in enumerate(nms):
            if defer:
                deferred.append((nm, arrs[2 * t], arrs[2 * t + 1]))
            else:
                red[nm] = _chip_sum(arrs[2 * t], arrs[2 * t + 1], chip, f"chip_sum_{nm}")

    ffn_grads = {nm: lax.empty((4, FF_SHARD, D_MODEL), F32) for nm in ("w_gate", "w_up", "w_down")}

    def place_ffn_grads(l):
        g = red[f"ffn{l}"]
        for k, nm in enumerate(("w_gate", "w_up", "w_down")):
            ffn_grads[nm] = ffn_grads[nm].at[l].set(g[k])

    dks, dvs = [], []
    pending = None
    bwd_deps = []
    for l in (3, 2, 1, 0):
        key = f"ffn{l}"
        dx, act, dgb, dub, hn, dyb, dln = _ffn_bwd(x_mid[l], dx, gates[l], ups[l], ln_ffn[l].reshape(1, -1),
                                                     w_ffn[l], f"ffn_bwd{l}", deps=bwd_deps)
        bwd_deps = []
        small[f"ln_ffn{l}"] = dln
        if l == 1:
            att_chip = chip_start(att_pair, dx)
            tn_deps = [att_chip[2][3]]
        else:
            tn_deps = []
        if pending:
            chip_finish(pending, dx, defer=True)
            pending = None
        big[key] = _tn_matmul(dgb, hn, big[key], 0, f"dw_gate{l}", m_chunk=FF_HALF, deps=tn_deps)
        big[key] = _tn_matmul(dub, hn, big[key], 1, f"dw_up{l}", m_chunk=FF_HALF)
        big[key] = _tn_matmul(act, dyb, big[key], 2, f"dw_down{l}", m_chunk=FF_HALF)
        if l == 1:
            chip_finish(att_chip, big[key], defer=True)
        ffn_pair = pair_start([key], key)
        if l >= 2:
            j = l - 2
            do, dxb, stats = _o_bwd(dx, wo_all[j], outs[j], lses[j], f"o_bwd{j}", deps=[ffn_pair[2][3]])
            big["wo"] = _tn_matmul(outs[j], dxb, big["wo"], j, f"dw_o{j}")
            ffn_chip = chip_start(ffn_pair, big["wo"])
            dq, dk, dv = _att_bwd(qs[j], k_sh, v_sh, do, stats, f"att_bwd{j}", deps=[ffn_chip[2][3]])
            chip_finish(ffn_chip, dq, defer=True)
            dks.append(dk)
            dvs.append(dv)
            dx, hnq, cqn, dqa, dcq, dln, dgql, dgqn, dgqr = _q_bwd(x_in[l], dx, dq, *q_args(j), f"q_bwd{j}")
            small[f"ln_mix_b{j}"] = dln
            small[f"g_q_latent{j}"] = dgql
            small[f"g_q{j}"] = jnp.concatenate([dgqn, dgqr[:, :ROPE]], axis=1)
            big["wdq"] = _tn_matmul(hnq, dcq, big["wdq"], j, f"dw_dq{j}")
            big["wuqT"] = _tn_matmul(dqa, cqn, big["wuqT"], j, f"dw_uq{j}")
            if l == 2:
                (dx, hnk, cn, dknb, dvb, dccb, dpeb, dlnkv, dgl, dgkn, dgkr) = _kv_bwd(
                    x_kv, dx, dks, dvs, lnkv, wc, wpe, gl, wuk_g, wuv_g, gkn, gkr, cos, sin, "kv_bwd")
                small["ln_kv"] = dlnkv
                small["g_kv_latent"] = dgl
                small["g_k"] = jnp.concatenate([dgkn, dgkr[:, :ROPE]], axis=1)
                big["kv512"] = _tn_matmul(dknb, cn, big["kv512"], 0, "dw_uk")
                big["kv512"] = _tn_matmul(dvb, cn, big["kv512"], 1, "dw_uv")
                big["kv512"] = _tn_matmul(hnk, dccb, big["kv512"], 2, "dw_dkv_c")
                big["dkv_pe"] = _tn_matmul(hnk, dpeb, big["dkv_pe"], 0, "dw_dkv_pe")
                att_pair = pair_start(["wo", "kv512", "dkv_pe", "wdq", "wuqT"], "att")
                bwd_deps = [att_pair[2][3]]
        else:
            dx, dyp, dsc, db, dln = _mix_bwd(x_in[l], dx, pooled[l], ln_a_all[l], wp_all[l], bp_all[l], sc_all[l],
                                             f"mix_bwd{l}", deps=[ffn_pair[2][3]])
            small[f"ln_mix_a{l}"] = dln
            small[f"pool_scale{l}"] = dsc
            small[f"b_pool{l}"] = db
            ffn_chip = chip_start(ffn_pair, dx)
            big["wpool"] = _tn_matmul(pooled[l], dyp, big["wpool"], 4 * l, f"dw_pool{l}", groups=4,
                                      deps=[ffn_chip[2][3]])
            if l == 1:
                pending = ffn_chip
                bwd_deps = [ffn_chip[2][3]]
            else:
                for nm, part, land in deferred:
                    red[nm] = _chip_sum(part, land, chip, f"chip_sum_{nm}", deps=[ffn_chip[2][3]])
                    if nm.startswith("ffn"):
                        place_ffn_grads(int(nm[-1]))
                early_grads = dict(
                    w_dkv=jnp.concatenate([red["kv512"][2], red["dkv_pe"][0][:, :ROPE]], axis=1),
                    w_uk=red["kv512"][0].T, w_uv=red["kv512"][1].T, w_dq=red["wdq"],
                    w_uq=red["wuqT"].transpose(0, 2, 1)[:, :, :QK_DIM], w_o=red["wo"])
                for nm, g in early_grads.items():
                    updates[nm] = _adamw(weights[nm], g, mom1[nm], mom2[nm], f"adamw_{nm}")
                chip_finish(ffn_chip, [big["wpool"]] + list(ffn_grads.values()) + [u[0] for u in updates.values()])
                place_ffn_grads(0)
    grad_x = dx[None]
    pool_pair = pair_start(["wpool"], "wpool")
    pool_chip = chip_start(pool_pair, pool_pair[2][3])
    chip_finish(pool_chip, pool_chip[2][3])

    vec_names = (["loss"] + [f"ln_ffn{l}" for l in range(4)] + ["ln_kv", "g_kv_latent", "g_k"]
                 + [f"{p}{j}" for p in ("ln_mix_b", "g_q_latent", "g_q") for j in range(2)]
                 + [f"{p}{l}" for p in ("ln_mix_a", "pool_scale", "b_pool") for l in range(2)])
    small["loss"] = sq_cols
    widths = [small[nm].shape[1] for nm in vec_names]
    padded = [-(-w // LANES) * LANES for w in widths]
    packed = jnp.concatenate([_pad_cols(small[nm], pw) for nm, pw in zip(vec_names, padded)], axis=1)
    (all_vecs,) = _all_gather([packed], [0], "gather_vectors")
    total = _sum_lead(all_vecs, "sum_vectors")
    vec = {}
    off = 0
    for nm, w, pw in zip(vec_names, widths, padded):
        vec[nm] = total[0, off:off + w]
        off += pw
    loss = 0.5 * jnp.sum(vec["loss"]) * (1.0 / D_MODEL)

    def own_cols(full, width):
        return lax.dynamic_slice_in_dim(full, dev * width, width, axis=full.ndim - 1)

    grads = dict(
        ln_mix_a=own_cols(jnp.stack([vec["ln_mix_a0"], vec["ln_mix_a1"]]), LANES),
        w_pool=red["wpool"].reshape(2, 4, 32, GROUP_DIM),
        b_pool=own_cols(jnp.stack([vec["b_pool0"], vec["b_pool1"]]).reshape(2, 4, GROUP_DIM), 32),
        pool_scale=own_cols(jnp.stack([vec["pool_scale0"], vec["pool_scale1"]]), LANES),
        ln_ffn=jnp.stack([vec[f"ln_ffn{l}"] for l in range(4)]),
        w_gate=ffn_grads["w_gate"],
        w_up=ffn_grads["w_up"],
        w_down=ffn_grads["w_down"],
        ln_kv=vec["ln_kv"],
        g_kv_latent=vec["g_kv_latent"],
        g_k=vec["g_k"],
        ln_mix_b=jnp.stack([vec["ln_mix_b0"], vec["ln_mix_b1"]]),
        g_q_latent=jnp.stack([vec["g_q_latent0"], vec["g_q_latent1"]]),
        g_q=jnp.stack([vec["g_q0"], vec["g_q1"]]),
        **early_grads,
    )

    deltas, new_m, new_v = {}, {}, {}
    for nm in names:
        w = weights[nm]
        if nm in updates:
            deltas[nm], new_m[nm], new_v[nm] = updates[nm]
            continue
        if nm in ("w_gate", "w_up"):
            def swap(a):
                return a.transpose(0, 2, 1)
            d, mo, vo = _adamw(swap(w), grads[nm], swap(mom1[nm]), swap(mom2[nm]), f"adamw_{nm}")
            deltas[nm], new_m[nm], new_v[nm], grads[nm] = swap(d), swap(mo), swap(vo), swap(grads[nm])
            continue
        shape = w.shape if w.ndim > 1 else (1, w.shape[0])
        d, mo, vo = _adamw(w.reshape(shape), grads[nm].reshape(shape), mom1[nm].reshape(shape),
                           mom2[nm].reshape(shape), f"adamw_{nm}")
        deltas[nm], new_m[nm], new_v[nm] = d.reshape(w.shape), mo.reshape(w.shape), vo.reshape(w.shape)

    return (loss, grad_x, *[grads[nm].reshape(weights[nm].shape) for nm in names], *[deltas[nm] for nm in names],
            *[new_m[nm] for nm in names], *[new_v[nm] for nm in names])
```

```python
import functools
import math

import jax
import jax.numpy as jnp
from jax import lax
from jax.experimental import pallas as pl
from jax.experimental.pallas import tpu as pltpu

F32 = jnp.float32
BF16 = jnp.bfloat16
MESH = pl.DeviceIdType.MESH

D_MODEL = 1024
D_FF = 2816
N_DEV = 8
N_CHIPS = 4
FF_SHARD = D_FF // N_DEV
FF_HALF = D_FF // 2
N_HEADS = 8
NOPE = 128
ROPE = 64
QK_DIM = NOPE + ROPE
QK_PAD = 256
V_DIM = 128
Q_RANK = 256
KV_RANK = 512
POOL_WINDOWS = (2, 4, 8, 16)
GROUP_DIM = 256
HALO = 128
CHUNK = 64
ROPE_THETA = 10000.0
EPS = 1e-6
LANES = 128

ADAM_LR = 0.001
ADAM_B1 = 0.9
ADAM_B2 = 0.999
ADAM_EPS = 1e-08
ADAM_WD = 0.01
ADAM_STEP = 10

PROJ_ROWS = 256
MIX_ROWS = 256
VMEM_BIG = 56 * 2**20
VMEM_MID = 40 * 2**20


def _nn(a, b):
    return lax.dot_general(a, b, (((1,), (0,)), ((), ())), preferred_element_type=F32)


def _nt(a, b):
    return lax.dot_general(a, b, (((1,), (1,)), ((), ())), preferred_element_type=F32)


def _tn(a, b):
    return lax.dot_general(a, b, (((0,), (0,)), ((), ())), preferred_element_type=F32)


def _rms(x, g, n):
    r = lax.rsqrt(jnp.sum(x * x, axis=-1, keepdims=True) * (1.0 / n) + EPS)
    return (x * r) * g, r


def _rms_bwd(x, r, g, dy, n):
    u = dy * g
    s = jnp.sum(x * u, axis=-1, keepdims=True) * (1.0 / n)
    dx = r * u - x * (r * r * r * s)
    dg = jnp.sum(dy * (x * r), axis=0, keepdims=True)
    return dx, dg


def _swap_perm():
    i = lax.broadcasted_iota(jnp.int32, (LANES, LANES), 0)
    j = lax.broadcasted_iota(jnp.int32, (LANES, LANES), 1)
    half = ROPE // 2
    hit = ((j < half) & (i == j + half)) | ((j >= half) & (j < ROPE) & (i == j - half))
    return jnp.where(hit, 1.0, 0.0).astype(BF16)


def _swap_halves(z, perm):
    hi = z.astype(BF16)
    lo = (z - hi.astype(F32)).astype(BF16)
    return _nn(hi, perm) + _nn(lo, perm)


def _sigmoid(x):
    return 1.0 / (1.0 + jnp.exp(-x))


def _cparams(n_grid, vmem=None):
    return pltpu.CompilerParams(dimension_semantics=("arbitrary",) * n_grid, vmem_limit_bytes=vmem)


def _rows(t, cols):
    return pl.BlockSpec((t, cols), lambda i: (i, 0))


def _full(shape):
    nd = len(shape)
    return pl.BlockSpec(shape, lambda *_: (0,) * nd)


ANY = pl.BlockSpec(memory_space=pl.ANY)


def _pcall(body, args, deps, *, in_specs, **kw):
    n_in, n_dep = len(args), len(deps)

    def ordered(*refs):
        body(*refs[:n_in], *refs[n_in + n_dep:])

    return pl.pallas_call(ordered, in_specs=list(in_specs) + [ANY] * n_dep, **kw)(*args, *deps)


def _place():
    x, y, c = lax.axis_index("x"), lax.axis_index("y"), lax.axis_index("c")
    return x, y, c


def _all_gather(shards, axes, name, deps=()):
    n, nd = len(shards), len(deps)
    out_shape = [jax.ShapeDtypeStruct(s.shape[:a] + (N_DEV,) + s.shape[a:], s.dtype) for s, a in zip(shards, axes)]

    def body(*refs):
        ins, outs = refs[:n], refs[n + nd:2 * n + nd]
        send_sems, recv_sems, local_sems = refs[2 * n + nd:]
        x, y, c = _place()
        me, sibling = (x, y, c), (x, y, 1 - c)
        chips = [(1 - x, y), (x, 1 - y), (1 - x, 1 - y)]

        def slot(t, dev):
            idx = 4 * dev[0] + 2 * dev[1] + dev[2]
            return outs[t].at[(slice(None),) * axes[t] + (idx,)]

        def copy(t, k, block, to, src=None):
            return pltpu.make_async_remote_copy(
                src_ref=slot(t, block) if src is None else src, dst_ref=slot(t, block),
                send_sem=send_sems.at[t, k], recv_sem=recv_sems.at[t, k],
                device_id=to, device_id_type=MESH)

        mine = [pltpu.make_async_copy(ins[t], slot(t, me), local_sems.at[t]) for t in range(n)]
        for cp in mine:
            cp.start()
        first = []
        for t in range(n):
            first.append(copy(t, 0, me, sibling, src=ins[t]))
            first += [copy(t, 1 + j, me, (*chip, c), src=ins[t]) for j, chip in enumerate(chips)]
        for cp in first:
            cp.start()
        passed = []
        for j, chip in enumerate(chips):
            for t in range(n):
                copy(t, 1 + j, (*chip, c), me).wait_recv()
                cp = copy(t, 4 + j, (*chip, c), sibling)
                cp.start()
                passed.append(cp)
        for t in range(n):
            copy(t, 0, sibling, me).wait_recv()
            for j, chip in enumerate(chips):
                copy(t, 4 + j, (*chip, 1 - c), me).wait_recv()
        for cp in first + passed:
            cp.wait_send()
        for cp in mine:
            cp.wait()

    return pl.pallas_call(
        body, name=name, out_shape=out_shape,
        in_specs=[ANY] * (n + nd), out_specs=[ANY] * n,
        scratch_shapes=[pltpu.SemaphoreType.DMA((n, 7)), pltpu.SemaphoreType.DMA((n, 7)),
                        pltpu.SemaphoreType.DMA((n,))],
    )(*shards, *deps)


HBM = pl.BlockSpec(memory_space=pltpu.HBM)
SEM = pl.BlockSpec(memory_space=pltpu.SEMAPHORE)
EFFECT = pltpu.SideEffectType.DATAFLOW_SIDE_EFFECTING


def _copies_start(arrays, n_sems, plan, name, deps=()):
    n, nd = len(arrays), len(deps)

    def body(*refs):
        for cp in plan(refs[:n], refs[n + nd], refs[n + nd + 1]):
            cp.start()
        refs[-1][...] = jnp.zeros_like(refs[-1])

    outs = pl.pallas_call(
        body, name=name,
        out_shape=(pltpu.SemaphoreType.DMA((n_sems,)), pltpu.SemaphoreType.DMA((n_sems,)),
                   *[pltpu.HBM(a.shape, a.dtype) for a in arrays], jax.ShapeDtypeStruct((8, LANES), F32)),
        in_specs=[HBM] * n + [ANY] * nd,
        out_specs=(SEM, SEM, *[HBM] * n, pl.BlockSpec(memory_space=pltpu.VMEM)),
        input_output_aliases={i: 2 + i for i in range(n)},
        compiler_params=pltpu.CompilerParams(has_side_effects=EFFECT),
    )(*[pltpu.with_memory_space_constraint(a, pltpu.HBM) for a in arrays], *deps)
    return outs[0], outs[1], list(outs[2:2 + n]), outs[-1]


def _copies_wait(arrays, send_sems, recv_sems, after, plan, name):
    n = len(arrays)
    after = list(after) if isinstance(after, (list, tuple)) else [after]

    def body(*refs):
        for cp in plan(refs[:n], refs[n], refs[n + 1]):
            cp.wait_send()
            cp.wait_recv()

    outs = pl.pallas_call(
        body, name=name,
        out_shape=tuple(pltpu.HBM(a.shape, a.dtype) for a in arrays),
        in_specs=[HBM] * n + [SEM, SEM] + [ANY] * len(after), out_specs=tuple([HBM] * n),
        input_output_aliases={i: i for i in range(n)},
        compiler_params=pltpu.CompilerParams(has_side_effects=EFFECT),
    )(*arrays, send_sems, recv_sems, *after)
    return list(outs)


def _remote(src, dst, send_sems, recv_sems, t, to):
    return pltpu.make_async_remote_copy(src_ref=src, dst_ref=dst, send_sem=send_sems.at[t], recv_sem=recv_sems.at[t],
                                        device_id=to, device_id_type=MESH)


def _dev_index(x, y, c):
    return 4 * x + 2 * y + c


def _gather_spread(bufs, send_sems, recv_sems):
    x, y, c = _place()
    mine = _dev_index(x, y, c)
    peers = [(x, y, 1 - c), (1 - x, y, c), (x, 1 - y, c), (1 - x, 1 - y, c)]
    return [_remote(g.at[k, mine], g.at[k, mine], send_sems, recv_sems, t, peer)
            for t, g in enumerate(bufs) for peer in peers for k in range(g.shape[0])]


def _gather_relay(bufs, send_sems, recv_sems):
    x, y, c = _place()
    blocks = [_dev_index(1 - x, y, c), _dev_index(x, 1 - y, c), _dev_index(1 - x, 1 - y, c)]
    return [_remote(g.at[k, b], g.at[k, b], send_sems, recv_sems, t, (x, y, 1 - c))
            for t, g in enumerate(bufs) for b in blocks for k in range(g.shape[0])]


def _blocks_moved(count):
    def plan(bufs, send_sems, recv_sems):
        x, y, c = _place()
        return [_remote(g.at[:, pl.ds(0, count)], g.at[:, pl.ds(0, count)], send_sems, recv_sems, t, (x, y, 1 - c))
                for t, g in enumerate(bufs)]
    return plan


def _pair_send(arrs, send_sems, recv_sems):
    x, y, c = _place()
    return [_remote(arrs[2 * t].at[p, k, 1 - c], arrs[2 * t + 1].at[p, k], send_sems, recv_sems, t, (x, y, 1 - c))
            for t in range(len(arrs) // 2) for p in range(arrs[2 * t].shape[0]) for k in range(N_CHIPS)]


def _chip_send(arrs, send_sems, recv_sems):
    x, y, c = _place()
    chips = [(1 - x, y), (x, 1 - y), (1 - x, 1 - y)]
    return [_remote(arrs[2 * t].at[p, 2 * px + py], arrs[2 * t + 1].at[j, p], send_sems, recv_sems, t, (px, py, c))
            for t in range(len(arrs) // 2) for j, (px, py) in enumerate(chips) for p in range(arrs[2 * t].shape[0])]


def _landed(arrs, send_sems, recv_sems):
    x, y, c = _place()
    return [_remote(arrs[2 * t + 1], arrs[2 * t + 1], send_sems, recv_sems, t, (x, y, 1 - c))
            for t in range(len(arrs) // 2)]


def _rows_per_step(rows, row_elems):
    best = 1
    for cand in range(1, rows + 1):
        if rows % cand == 0 and cand * row_elems <= 256 * 1024:
            best = cand
    return best


def _pair_sum(grad, landed, core, name):
    p, _, _, sz, c = grad.shape
    r = _rows_per_step(p * N_CHIPS, sz * c)

    def body(core_ref, g_ref, l_ref, o_ref):
        o_ref[...] = (g_ref[...].astype(F32) + l_ref[...].astype(F32)).astype(o_ref.dtype)

    out = pl.pallas_call(
        body, name=name,
        grid_spec=pltpu.PrefetchScalarGridSpec(
            num_scalar_prefetch=1, grid=(p * N_CHIPS // r,),
            in_specs=[pl.BlockSpec((r, None, sz, c), lambda i, cr: (i, cr[0], 0, 0)),
                      pl.BlockSpec((r, sz, c), lambda i, cr: (i, 0, 0))],
            out_specs=pl.BlockSpec((r, sz, c), lambda i, cr: (i, 0, 0))),
        out_shape=jax.ShapeDtypeStruct((p * N_CHIPS, sz, c), grad.dtype),
        compiler_params=_cparams(1),
    )(core, grad.reshape(p * N_CHIPS, 2, sz, c), landed.reshape(p * N_CHIPS, sz, c))
    return out.reshape(p, N_CHIPS, sz, c)


def _chip_sum(parts, landed, chip, name, deps=()):
    p, _, sz, c = parts.shape
    r = _rows_per_step(p, sz * c)

    def body(chip_ref, a_ref, l_ref, o_ref):
        acc = a_ref[...].astype(F32)
        for j in range(3):
            acc = acc + l_ref[j].astype(F32)
        o_ref[...] = acc

    nd = len(deps)

    def ordered(chip_ref, a_ref, l_ref, *rest):
        body(chip_ref, a_ref, l_ref, rest[nd])

    return pl.pallas_call(
        ordered, name=name,
        grid_spec=pltpu.PrefetchScalarGridSpec(
            num_scalar_prefetch=1, grid=(p // r,),
            in_specs=[pl.BlockSpec((r, None, sz, c), lambda i, cr: (i, cr[0], 0, 0)),
                      pl.BlockSpec((3, r, sz, c), lambda i, cr: (0, i, 0, 0))] + [ANY] * nd,
            out_specs=pl.BlockSpec((r, sz, c), lambda i, cr: (i, 0, 0))),
        out_shape=jax.ShapeDtypeStruct((p, sz, c), F32),
        compiler_params=_cparams(1),
    )(chip, parts, landed, *deps)


def _sum_lead(a, name, out_dtype=F32):
    k = a.shape[0]
    rest = a.shape[1:]
    r, c = rest[-2], rest[-1]
    lead = math.prod(rest[:-2])
    a3 = a.reshape(k, lead * r, c)
    rows = lead * r
    tb = rows
    for cand in (512, 256, 128, 64, 32, 16, 8):
        if rows % cand == 0 and rows > cand:
            tb = cand
            break

    def body(a_ref, o_ref):
        acc = a_ref[0].astype(F32)
        for i in range(1, k):
            acc = acc + a_ref[i].astype(F32)
        o_ref[...] = acc.astype(out_dtype)

    out = pl.pallas_call(
        body, name=name, grid=(rows // tb,),
        out_shape=jax.ShapeDtypeStruct((rows, c), out_dtype),
        in_specs=[pl.BlockSpec((k, tb, c), lambda i: (0, i, 0))],
        out_specs=pl.BlockSpec((tb, c), lambda i: (i, 0)),
        compiler_params=_cparams(1),
    )(a3)
    return out.reshape(rest)


def _bands(t, causal):
    r = lax.broadcasted_iota(jnp.int32, (t, t + HALO), 0)
    col = lax.broadcasted_iota(jnp.int32, (t, t + HALO), 1)
    diff = r + HALO - col if causal else col - r
    return jnp.stack([jnp.where((diff >= 0) & (diff < w), 1.0, 0.0) for w in POOL_WINDOWS]).astype(BF16)


def _split_dot(band, v):
    hi = v.astype(BF16)
    lo = (v - hi.astype(F32)).astype(BF16)
    return _nn(band, hi) + _nn(band, lo)


def _mix_fwd(x, g, wp, b, sc, name, deps=()):
    s = x.shape[0]
    t = min(MIX_ROWS, s)
    rb = t // HALO

    def body(x_ref, xh_ref, g_ref, wp_ref, b_ref, sc_ref, band_ref, xo_ref, d_ref):
        i = pl.program_id(0)
        gg = g_ref[...]
        h, _ = _rms(x_ref[...], gg, D_MODEL)
        hh, _ = _rms(xh_ref[...], gg, D_MODEL)
        hh = jnp.where(i > 0, hh, 0.0)
        hext = jnp.concatenate([hh, h], axis=0)
        tok = i * t + lax.broadcasted_iota(jnp.int32, (t, 1), 0)
        for gi, w in enumerate(POOL_WINDOWS):
            sl = slice(gi * GROUP_DIM, (gi + 1) * GROUP_DIM)
            win = _split_dot(band_ref[gi], hext[:, sl])
            inv = 1.0 / jnp.minimum(tok + 1, w).astype(F32)
            dbf = (win * inv - h[:, sl]).astype(BF16)
            d_ref[:, sl] = dbf
            ypre = _nn(dbf, wp_ref[gi]) + b_ref[:, sl]
            xo_ref[:, sl] = x_ref[:, sl] + ypre * sc_ref[:, sl]

    return _pcall(
        body, (x, x, g, wp, b, sc, _bands(t, True)), deps, name=name, grid=(s // t,),
        out_shape=[jax.ShapeDtypeStruct((s, D_MODEL), F32), jax.ShapeDtypeStruct((s, D_MODEL), BF16)],
        in_specs=[_rows(t, D_MODEL),
                  pl.BlockSpec((HALO, D_MODEL), lambda i: (jnp.maximum(i * rb - 1, 0), 0)),
                  _full((1, D_MODEL)), _full((4, GROUP_DIM, GROUP_DIM)), _full((1, D_MODEL)), _full((1, D_MODEL)),
                  _full((4, t, t + HALO))],
        out_specs=[_rows(t, D_MODEL), _rows(t, D_MODEL)],
        compiler_params=_cparams(1, VMEM_MID),
    )


def _mix_bwd(x, dy, d, g, wp, b, sc, name, deps=()):
    s = x.shape[0]
    t = min(MIX_ROWS, s)
    rb = t // HALO
    nb = s // t
    last_halo = s // HALO - 1

    def body(x_ref, dy_ref, dyn_ref, d_ref, g_ref, wp_ref, b_ref, sc_ref, band_ref,
             dx_ref, dyp_ref, dsc_ref, db_ref, dln_ref):
        i = pl.program_id(0)
        x = x_ref[...]
        gg = g_ref[...]
        dy = dy_ref[...]
        sc = sc_ref[...]
        dyp32 = dy * sc
        dyp = dyp32.astype(BF16)
        dyph = (dyn_ref[...] * sc).astype(BF16)
        dyp_ref[...] = dyp
        tok = i * t + lax.broadcasted_iota(jnp.int32, (t + HALO, 1), 0)
        dh, dsc = [], []
        for gi, w in enumerate(POOL_WINDOWS):
            sl = slice(gi * GROUP_DIM, (gi + 1) * GROUP_DIM)
            ypre = _nn(d_ref[:, sl], wp_ref[gi]) + b_ref[:, sl]
            dsc.append(jnp.sum(dy[:, sl] * ypre, axis=0, keepdims=True))
            dd = _nt(dyp[:, sl], wp_ref[gi])
            ddh = jnp.where(i < nb - 1, _nt(dyph[:, sl], wp_ref[gi]), 0.0)
            inv = 1.0 / jnp.minimum(tok + 1, w).astype(F32)
            ddext = jnp.concatenate([dd, ddh], axis=0) * inv
            dh.append(_split_dot(band_ref[gi], ddext) - dd)
        dh = jnp.concatenate(dh, axis=1)
        _, r = _rms(x, gg, D_MODEL)
        dxn, dg = _rms_bwd(x, r, gg, dh, D_MODEL)
        dx_ref[...] = dy + dxn

        @pl.when(i == 0)
        def _():
            dsc_ref[...] = jnp.zeros_like(dsc_ref)
            db_ref[...] = jnp.zeros_like(db_ref)
            dln_ref[...] = jnp.zeros_like(dln_ref)

        dsc_ref[...] += jnp.concatenate(dsc, axis=1)
        db_ref[...] += jnp.sum(dyp32, axis=0, keepdims=True)
        dln_ref[...] += dg

    vec = jax.ShapeDtypeStruct((1, D_MODEL), F32)
    return _pcall(
        body, (x, dy, dy, d, g, wp, b, sc, _bands(t, False)), deps, name=name, grid=(nb,),
        out_shape=[jax.ShapeDtypeStruct((s, D_MODEL), F32), jax.ShapeDtypeStruct((s, D_MODEL), BF16), vec, vec, vec],
        in_specs=[_rows(t, D_MODEL), _rows(t, D_MODEL),
                  pl.BlockSpec((HALO, D_MODEL), lambda i: (jnp.minimum((i + 1) * rb, last_halo), 0)),
                  _rows(t, D_MODEL),
                  _full((1, D_MODEL)), _full((4, GROUP_DIM, GROUP_DIM)), _full((1, D_MODEL)), _full((1, D_MODEL)),
                  _full((4, t, t + HALO))],
        out_specs=[_rows(t, D_MODEL), _rows(t, D_MODEL), _full((1, D_MODEL)), _full((1, D_MODEL)), _full((1, D_MODEL))],
        compiler_params=_cparams(1, VMEM_MID),
    )


def _load_weights(w_hbm, w_vmem, sem):
    @pl.when(pl.program_id(0) == 0)
    def _():
        cp = pltpu.make_async_copy(w_hbm, w_vmem, sem)
        cp.start()
        cp.wait()


def _ffn_fwd(x, g, w, name):
    s = x.shape[0]
    t = min(512, s)

    def body(x_ref, g_ref, w_hbm, xo_ref, gate_ref, up_ref, w_ref, sem):
        _load_weights(w_hbm, w_ref, sem)
        x = x_ref[...]
        hn = _rms(x, g_ref[...], D_MODEL)[0].astype(BF16)
        acc = x
        for c in range(2):
            rs = slice(c * FF_HALF, (c + 1) * FF_HALF)
            gt = _nt(hn, w_ref[0, rs, :])
            up = _nt(hn, w_ref[1, rs, :])
            gate_ref[:, rs] = gt.astype(BF16)
            up_ref[:, rs] = up.astype(BF16)
            act = ((gt * _sigmoid(gt)) * up).astype(BF16)
            acc = acc + _nn(act, w_ref[2, rs, :])
        xo_ref[...] = acc

    hid = jax.ShapeDtypeStruct((s, D_FF), BF16)
    return pl.pallas_call(
        body, name=name, grid=(s // t,),
        out_shape=[jax.ShapeDtypeStruct((s, D_MODEL), F32), hid, hid],
        in_specs=[_rows(t, D_MODEL), _full((1, D_MODEL)), ANY],
        out_specs=[_rows(t, D_MODEL), _rows(t, D_FF), _rows(t, D_FF)],
        scratch_shapes=[pltpu.VMEM((3, D_FF, D_MODEL), BF16), pltpu.SemaphoreType.DMA],
        compiler_params=_cparams(1, VMEM_BIG),
    )(x, g, w)


def _ffn_bwd(x, dy, gate, up, g, w, name, deps=()):
    s = x.shape[0]
    t = min(256, s)

    def body(x_ref, dy_ref, gate_ref, up_ref, g_ref, w_hbm,
             dx_ref, act_ref, dg_ref, du_ref, hn_ref, dyb_ref, dln_ref, w_ref, sem):
        _load_weights(w_hbm, w_ref, sem)
        x = x_ref[...]
        gg = g_ref[...]
        y, r = _rms(x, gg, D_MODEL)
        hn = y.astype(BF16)
        hn_ref[...] = hn
        dy = dy_ref[...]
        dyb = dy.astype(BF16)
        dyb_ref[...] = dyb
        dh = jnp.zeros((t, D_MODEL), F32)
        for c in range(2):
            rs = slice(c * FF_HALF, (c + 1) * FF_HALF)
            gt = gate_ref[:, rs].astype(F32)
            u = up_ref[:, rs].astype(F32)
            sg = _sigmoid(gt)
            sl = gt * sg
            act_ref[:, rs] = (sl * u).astype(BF16)
            dact = _nt(dyb, w_ref[2, rs, :])
            dg = (dact * u * (sg * (1.0 + gt * (1.0 - sg)))).astype(BF16)
            du = (dact * sl).astype(BF16)
            dg_ref[:, rs] = dg
            du_ref[:, rs] = du
            dh = dh + _nn(dg, w_ref[0, rs, :]) + _nn(du, w_ref[1, rs, :])
        dxn, dgl = _rms_bwd(x, r, gg, dh, D_MODEL)
        dx_ref[...] = dy + dxn

        @pl.when(pl.program_id(0) == 0)
        def _():
            dln_ref[...] = jnp.zeros_like(dln_ref)

        dln_ref[...] += dgl

    hid = jax.ShapeDtypeStruct((s, D_FF), BF16)
    tok = jax.ShapeDtypeStruct((s, D_MODEL), BF16)
    return _pcall(
        body, (x, dy, gate, up, g, w), deps, name=name, grid=(s // t,),
        out_shape=[jax.ShapeDtypeStruct((s, D_MODEL), F32), hid, hid, hid, tok, tok,
                   jax.ShapeDtypeStruct((1, D_MODEL), F32)],
        in_specs=[_rows(t, D_MODEL), _rows(t, D_MODEL), _rows(t, D_FF), _rows(t, D_FF), _full((1, D_MODEL)), ANY],
        out_specs=[_rows(t, D_MODEL), _rows(t, D_FF), _rows(t, D_FF), _rows(t, D_FF),
                   _rows(t, D_MODEL), _rows(t, D_MODEL), _full((1, D_MODEL))],
        scratch_shapes=[pltpu.VMEM((3, D_FF, D_MODEL), BF16), pltpu.SemaphoreType.DMA],
        compiler_params=_cparams(1, VMEM_BIG),
    )


def _tn_matmul(a, b, into, p0, name, groups=1, m_chunk=None, deps=()):
    s = a.shape[0]
    m, n = a.shape[1] // groups, b.shape[1] // groups
    assert into.shape[1:] == (m, n)
    mc = m if m_chunk is None else m_chunk
    nm = m // mc
    t = min(1024, s)
    nt = s // t

    def body(a_ref, b_ref, into_ref, o_ref, acc):
        ti = pl.program_id(2)

        @pl.when(ti == 0)
        def _():
            acc[...] = jnp.zeros_like(acc)

        acc[...] += _tn(a_ref[...], b_ref[...])

        @pl.when(ti == nt - 1)
        def _():
            o_ref[...] = acc[...].astype(o_ref.dtype)

    return _pcall(
        body, (a, b, into), deps, name=name, grid=(groups, nm, nt),
        out_shape=jax.ShapeDtypeStruct(into.shape, into.dtype),
        in_specs=[pl.BlockSpec((t, mc), lambda gi, mi, ti: (ti, gi * nm + mi)),
                  pl.BlockSpec((t, n), lambda gi, mi, ti: (ti, gi)), ANY],
        out_specs=pl.BlockSpec((None, mc, n), lambda gi, mi, ti: (p0 + gi, mi, 0)),
        scratch_shapes=[pltpu.VMEM((mc, n), F32)],
        input_output_aliases={2: 0},
        compiler_params=_cparams(3, VMEM_BIG),
    )


def _rope_tables(positions):
    half = ROPE // 2
    inv = ROPE_THETA ** (-jnp.arange(half, dtype=F32) * 2.0 / ROPE)
    ang = positions.astype(F32)[:, None] * inv
    cos, sin = jnp.cos(ang), jnp.sin(ang)
    zero = jnp.zeros((positions.shape[0], LANES - ROPE), F32)
    return jnp.concatenate([cos, cos, zero], axis=1), jnp.concatenate([-sin, sin, zero], axis=1)


def _kv_specs(t):
    return [_full((1, D_MODEL)), _full((D_MODEL, KV_RANK)), _full((D_MODEL, LANES)), _full((1, KV_RANK)),
            _full((KV_RANK, N_HEADS * NOPE)), _full((KV_RANK, N_HEADS * V_DIM)),
            _full((1, NOPE)), _full((1, LANES)), _rows(t, LANES), _rows(t, LANES)]


def _kv_fwd(x, ln, wc, wpe, gl, wuk, wuv, gkn, gkr, cos, sin, name, deps=()):
    s = x.shape[0]
    t = min(PROJ_ROWS, s)

    def body(x_ref, ln_ref, wc_ref, wpe_ref, gl_ref, wuk_ref, wuv_ref, gkn_ref, gkr_ref, cos_ref, sin_ref,
             k_ref, v_ref):
        hn = _rms(x_ref[...], ln_ref[...], D_MODEL)[0].astype(BF16)
        clat = _nn(hn, wc_ref[...])
        kpe = _nn(hn, wpe_ref[...])
        cn = _rms(clat, gl_ref[...], KV_RANK)[0].astype(BF16)
        sspe = jnp.sum(kpe * kpe, axis=-1, keepdims=True)
        base = kpe * gkr_ref[...]
        rot = base * cos_ref[...] + _swap_halves(base, _swap_perm()) * sin_ref[...]
        kn_all = _nn(cn, wuk_ref[...])
        v_ref[...] = _nn(cn, wuv_ref[...]).astype(BF16)
        for h in range(N_HEADS):
            kn = kn_all[:, h * NOPE:(h + 1) * NOPE]
            r = lax.rsqrt((jnp.sum(kn * kn, axis=-1, keepdims=True) + sspe) * (1.0 / QK_DIM) + EPS)
            k_ref[:, h * QK_PAD:h * QK_PAD + NOPE] = ((kn * r) * gkn_ref[...]).astype(BF16)
            k_ref[:, h * QK_PAD + NOPE:(h + 1) * QK_PAD] = (rot * r).astype(BF16)

    return _pcall(
        body, (x, ln, wc, wpe, gl, wuk, wuv, gkn, gkr, cos, sin), deps, name=name, grid=(s // t,),
        out_shape=[jax.ShapeDtypeStruct((s, N_HEADS * QK_PAD), BF16), jax.ShapeDtypeStruct((s, N_HEADS * V_DIM), BF16)],
        in_specs=[_rows(t, D_MODEL)] + _kv_specs(t),
        out_specs=[_rows(t, N_HEADS * QK_PAD), _rows(t, N_HEADS * V_DIM)],
        compiler_params=_cparams(1, VMEM_MID),
    )


def _kv_bwd(x, dxin, dks, dvs, ln, wc, wpe, gl, wuk, wuv, gkn, gkr, cos, sin, name):
    s = x.shape[0]
    t = min(PROJ_ROWS, s)
    nk = len(dks)

    def body(*refs):
        x_ref, dxin_ref = refs[:2]
        dk_refs = refs[2:2 + nk]
        dv_refs = refs[2 + nk:2 + 2 * nk]
        (ln_ref, wc_ref, wpe_ref, gl_ref, wuk_ref, wuv_ref, gkn_ref, gkr_ref, cos_ref, sin_ref,
         dx_ref, hn_ref, cn_ref, dkn_ref, dvb_ref, dcc_ref, dpe_ref,
         dln_ref, dgl_ref, dgkn_ref, dgkr_ref) = refs[2 + 2 * nk:]
        x = x_ref[...]
        ln = ln_ref[...]
        y, rx = _rms(x, ln, D_MODEL)
        hn = y.astype(BF16)
        hn_ref[...] = hn
        clat = _nn(hn, wc_ref[...])
        kpe = _nn(hn, wpe_ref[...])
        gl = gl_ref[...]
        cy, rc = _rms(clat, gl, KV_RANK)
        cn = cy.astype(BF16)
        cn_ref[...] = cn
        sspe = jnp.sum(kpe * kpe, axis=-1, keepdims=True)
        cs, sn, perm = cos_ref[...], sin_ref[...], _swap_perm()
        gkn, gkr = gkn_ref[...], gkr_ref[...]
        base = kpe * gkr
        rot = base * cs + _swap_halves(base, perm) * sn
        dkr_sum = jnp.zeros((t, LANES), F32)
        coef_sum = jnp.zeros((t, 1), F32)
        dgkn = jnp.zeros((1, NOPE), F32)
        kn_all = _nn(cn, wuk_ref[...])
        dkn_heads = []
        for h in range(N_HEADS):
            kn = kn_all[:, h * NOPE:(h + 1) * NOPE]
            r = lax.rsqrt((jnp.sum(kn * kn, axis=-1, keepdims=True) + sspe) * (1.0 / QK_DIM) + EPS)
            lo, mid, hi = h * QK_PAD, h * QK_PAD + NOPE, (h + 1) * QK_PAD
            dko = dk_refs[0][:, lo:mid]
            dkr = dk_refs[0][:, mid:hi]
            for j in range(1, nk):
                dko = dko + dk_refs[j][:, lo:mid]
                dkr = dkr + dk_refs[j][:, mid:hi]
            un = dko * gkn
            sm = (jnp.sum(kn * un, axis=-1, keepdims=True) + jnp.sum(rot * dkr, axis=-1, keepdims=True)) * (1.0 / QK_DIM)
            coef = r * r * r * sm
            dkn = (r * un - kn * coef).astype(BF16)
            dkr_sum = dkr_sum + r * dkr
            coef_sum = coef_sum + coef
            dgkn = dgkn + jnp.sum(dko * (kn * r), axis=0, keepdims=True)
            dkn_heads.append(dkn)
        dkn_all = jnp.concatenate(dkn_heads, axis=1)
        dkn_ref[...] = dkn_all
        dv_all = dv_refs[0][...]
        for j in range(1, nk):
            dv_all = dv_all + dv_refs[j][...]
        dvb = dv_all.astype(BF16)
        dvb_ref[...] = dvb
        dc = _nt(dkn_all, wuk_ref[...]) + _nt(dvb, wuv_ref[...])
        dz = dkr_sum * cs - _swap_halves(dkr_sum, perm) * sn
        dkpe = dz * gkr - kpe * coef_sum
        dgkr = jnp.sum(dz * kpe, axis=0, keepdims=True)
        dclat, dgl = _rms_bwd(clat, rc, gl, dc, KV_RANK)
        dcc = dclat.astype(BF16)
        dpe = dkpe.astype(BF16)
        dcc_ref[...] = dcc
        dpe_ref[...] = dpe
        dhn = _nt(dcc, wc_ref[...]) + _nt(dpe, wpe_ref[...])
        dxn, dln = _rms_bwd(x, rx, ln, dhn, D_MODEL)
        dx_ref[...] = dxin_ref[...] + dxn

        @pl.when(pl.program_id(0) == 0)
        def _():
            dln_ref[...] = jnp.zeros_like(dln_ref)
            dgl_ref[...] = jnp.zeros_like(dgl_ref)
            dgkn_ref[...] = jnp.zeros_like(dgkn_ref)
            dgkr_ref[...] = jnp.zeros_like(dgkr_ref)

        dln_ref[...] += dln
        dgl_ref[...] += dgl
        dgkn_ref[...] += dgkn
        dgkr_ref[...] += dgkr

    def tok(cols, dt):
        return jax.ShapeDtypeStruct((s, cols), dt)

    def vec(cols):
        return jax.ShapeDtypeStruct((1, cols), F32)

    return pl.pallas_call(
        body, name=name, grid=(s // t,),
        out_shape=[tok(D_MODEL, F32), tok(D_MODEL, BF16), tok(KV_RANK, BF16), tok(N_HEADS * NOPE, BF16),
                   tok(N_HEADS * V_DIM, BF16), tok(KV_RANK, BF16), tok(LANES, BF16),
                   vec(D_MODEL), vec(KV_RANK), vec(NOPE), vec(LANES)],
        in_specs=[_rows(t, D_MODEL), _rows(t, D_MODEL)] + [_rows(t, N_HEADS * QK_PAD)] * nk
                 + [_rows(t, N_HEADS * V_DIM)] * nk + _kv_specs(t),
        out_specs=[_rows(t, D_MODEL), _rows(t, D_MODEL), _rows(t, KV_RANK), _rows(t, N_HEADS * NOPE),
                   _rows(t, N_HEADS * V_DIM), _rows(t, KV_RANK), _rows(t, LANES),
                   _full((1, D_MODEL)), _full((1, KV_RANK)), _full((1, NOPE)), _full((1, LANES))],
        compiler_params=_cparams(1, VMEM_BIG),
    )(x, dxin, *dks, *dvs, ln, wc, wpe, gl, wuk, wuv, gkn, gkr, cos, sin)


def _q_specs(t):
    return [_full((1, D_MODEL)), _full((D_MODEL, Q_RANK)), _full((1, Q_RANK)), _full((N_HEADS, Q_RANK, QK_PAD)),
            _full((1, NOPE)), _full((1, LANES)), _rows(t, LANES), _rows(t, LANES)]


def _q_fwd(x, ln, wdq, gql, wuq, gqn, gqr, cos, sin, name, deps=()):
    s = x.shape[0]
    t = min(PROJ_ROWS, s)

    def body(x_ref, ln_ref, wdq_ref, gql_ref, wuq_ref, gqn_ref, gqr_ref, cos_ref, sin_ref, q_ref):
        hn = _rms(x_ref[...], ln_ref[...], D_MODEL)[0].astype(BF16)
        cqn = _rms(_nn(hn, wdq_ref[...]), gql_ref[...], Q_RANK)[0].astype(BF16)
        cs, sn, perm = cos_ref[...], sin_ref[...], _swap_perm()
        for h in range(N_HEADS):
            qa = _nn(cqn, wuq_ref[h])
            r = lax.rsqrt(jnp.sum(qa * qa, axis=-1, keepdims=True) * (1.0 / QK_DIM) + EPS)
            q_ref[:, h * QK_PAD:h * QK_PAD + NOPE] = ((qa[:, :NOPE] * r) * gqn_ref[...]).astype(BF16)
            z = (qa[:, NOPE:] * r) * gqr_ref[...]
            q_ref[:, h * QK_PAD + NOPE:(h + 1) * QK_PAD] = (z * cs + _swap_halves(z, perm) * sn).astype(BF16)

    return _pcall(
        body, (x, ln, wdq, gql, wuq, gqn, gqr, cos, sin), deps, name=name, grid=(s // t,),
        out_shape=jax.ShapeDtypeStruct((s, N_HEADS * QK_PAD), BF16),
        in_specs=[_rows(t, D_MODEL)] + _q_specs(t),
        out_specs=_rows(t, N_HEADS * QK_PAD),
        compiler_params=_cparams(1, VMEM_MID),
    )


def _q_bwd(x, dxin, dq, ln, wdq, gql, wuq, gqn, gqr, cos, sin, name):
    s = x.shape[0]
    t = min(PROJ_ROWS, s)

    def body(x_ref, dxin_ref, dq_ref, ln_ref, wdq_ref, gql_ref, wuq_ref, gqn_ref, gqr_ref, cos_ref, sin_ref,
             dx_ref, hn_ref, cqn_ref, dqa_ref, dcq_ref, dln_ref, dgql_ref, dgqn_ref, dgqr_ref):
        x = x_ref[...]
        ln = ln_ref[...]
        y, rx = _rms(x, ln, D_MODEL)
        hn = y.astype(BF16)
        hn_ref[...] = hn
        cqp = _nn(hn, wdq_ref[...])
        gql = gql_ref[...]
        cy, rc = _rms(cqp, gql, Q_RANK)
        cqn = cy.astype(BF16)
        cqn_ref[...] = cqn
        cs, sn, perm = cos_ref[...], sin_ref[...], _swap_perm()
        gqn, gqr = gqn_ref[...], gqr_ref[...]
        dcq = jnp.zeros((t, Q_RANK), F32)
        dgqn = jnp.zeros((1, NOPE), F32)
        dgqr = jnp.zeros((1, LANES), F32)
        for h in range(N_HEADS):
            qa = _nn(cqn, wuq_ref[h])
            qn, qr = qa[:, :NOPE], qa[:, NOPE:]
            r = lax.rsqrt(jnp.sum(qa * qa, axis=-1, keepdims=True) * (1.0 / QK_DIM) + EPS)
            dqo = dq_ref[:, h * QK_PAD:h * QK_PAD + NOPE]
            dqr = dq_ref[:, h * QK_PAD + NOPE:(h + 1) * QK_PAD]
            dz = dqr * cs - _swap_halves(dqr, perm) * sn
            un = dqo * gqn
            ur = dz * gqr
            sm = (jnp.sum(qn * un, axis=-1, keepdims=True) + jnp.sum(qr * ur, axis=-1, keepdims=True)) * (1.0 / QK_DIM)
            coef = r * r * r * sm
            dqa = jnp.concatenate([r * un - qn * coef, r * ur - qr * coef], axis=1).astype(BF16)
            dgqn = dgqn + jnp.sum(dqo * (qn * r), axis=0, keepdims=True)
            dgqr = dgqr + jnp.sum(dz * (qr * r), axis=0, keepdims=True)
            dqa_ref[:, h * QK_PAD:(h + 1) * QK_PAD] = dqa
            dcq = dcq + _nt(dqa, wuq_ref[h])
        dcqp, dgql = _rms_bwd(cqp, rc, gql, dcq, Q_RANK)
        dcqb = dcqp.astype(BF16)
        dcq_ref[...] = dcqb
        dhn = _nt(dcqb, wdq_ref[...])
        dxn, dln = _rms_bwd(x, rx, ln, dhn, D_MODEL)
        dx_ref[...] = dxin_ref[...] + dxn

        @pl.when(pl.program_id(0) == 0)
        def _():
            dln_ref[...] = jnp.zeros_like(dln_ref)
            dgql_ref[...] = jnp.zeros_like(dgql_ref)
            dgqn_ref[...] = jnp.zeros_like(dgqn_ref)
            dgqr_ref[...] = jnp.zeros_like(dgqr_ref)

        dln_ref[...] += dln
        dgql_ref[...] += dgql
        dgqn_ref[...] += dgqn
        dgqr_ref[...] += dgqr

    def tok(cols, dt):
        return jax.ShapeDtypeStruct((s, cols), dt)

    def vec(cols):
        return jax.ShapeDtypeStruct((1, cols), F32)

    return pl.pallas_call(
        body, name=name, grid=(s // t,),
        out_shape=[tok(D_MODEL, F32), tok(D_MODEL, BF16), tok(Q_RANK, BF16), tok(N_HEADS * QK_PAD, BF16),
                   tok(Q_RANK, BF16), vec(D_MODEL), vec(Q_RANK), vec(NOPE), vec(LANES)],
        in_specs=[_rows(t, D_MODEL), _rows(t, D_MODEL), _rows(t, N_HEADS * QK_PAD)] + _q_specs(t),
        out_specs=[_rows(t, D_MODEL), _rows(t, D_MODEL), _rows(t, Q_RANK), _rows(t, N_HEADS * QK_PAD),
                   _rows(t, Q_RANK), _full((1, D_MODEL)), _full((1, Q_RANK)), _full((1, NOPE)), _full((1, LANES))],
        compiler_params=_cparams(1, VMEM_MID),
    )(x, dxin, dq, ln, wdq, gql, wuq, gqn, gqr, cos, sin)


SM_SCALE = 1.0 / math.sqrt(QK_DIM)
LOG2_E = math.log2(math.e)
EXP2_SCALE = SM_SCALE * LOG2_E
NEG = -1e30


def _diag_mask(t):
    qpos = lax.broadcasted_iota(jnp.int32, (t, t), 0)
    kpos = lax.broadcasted_iota(jnp.int32, (t, t), 1)
    return lax.shift_right_logical(kpos, 6) <= lax.shift_right_logical(qpos, 6)


def _att_fwd(q, k, v, name):
    s = q.shape[0]
    t = min(512, s)
    nb = s // t

    def body(q_ref, k_ref, v_ref, o_ref, lse_ref):
        qi = pl.program_id(1)
        qq = q_ref[...]

        def block(ki, carry, masked):
            m_old, l_old, acc = carry
            rows = pl.ds(pl.multiple_of(ki * t, t), t)
            sc = _nt(qq, k_ref[rows, :])
            if masked:
                sc = jnp.where(_diag_mask(t), sc, NEG)
            m_new = jnp.maximum(m_old, jnp.max(sc, axis=-1, keepdims=True))
            p = jnp.exp2((sc - m_new) * EXP2_SCALE)
            alpha = jnp.exp2((m_old - m_new) * EXP2_SCALE)
            l_new = alpha * l_old + jnp.sum(p, axis=-1, keepdims=True)
            acc = alpha * acc + _nn(p.astype(BF16), v_ref[rows, :])
            return m_new, l_new, acc

        init = (jnp.full((t, 1), NEG, F32), jnp.zeros((t, 1), F32), jnp.zeros((t, V_DIM), F32))
        def pair(k0, c):
            return block(k0 + 1, block(k0, c, False), False)

        carry = lax.fori_loop(0, qi // 4, lambda j, c: pair(4 * j + 2, pair(4 * j, c)), init)
        done = 4 * (qi // 4)
        tails = [lambda c: block(qi, c, True),
                 lambda c: block(qi, block(done, c, False), True),
                 lambda c: block(qi, pair(done, c), True),
                 lambda c: block(qi, block(done + 2, pair(done, c), False), True)]
        m_fin, l_fin, acc = lax.switch(qi & 3, tails, carry)
        o_ref[...] = (acc / l_fin).astype(BF16)
        lse_ref[...] = jnp.broadcast_to(m_fin * SM_SCALE + jnp.log(l_fin), (t, LANES))

    return pl.pallas_call(
        body, name=name, grid=(N_HEADS, nb),
        out_shape=[jax.ShapeDtypeStruct((s, N_HEADS * V_DIM), BF16), jax.ShapeDtypeStruct((s, N_HEADS * LANES), F32)],
        in_specs=[pl.BlockSpec((t, QK_PAD), lambda h, qi: (qi, h)),
                  pl.BlockSpec((s, QK_PAD), lambda h, qi: (0, h)),
                  pl.BlockSpec((s, V_DIM), lambda h, qi: (0, h))],
        out_specs=[pl.BlockSpec((t, V_DIM), lambda h, qi: (qi, h)),
                   pl.BlockSpec((t, LANES), lambda h, qi: (qi, h))],
        compiler_params=_cparams(2, VMEM_MID),
    )(q, k, v)


def _att_bwd(q, k, v, do, stats, name, deps=()):
    s = q.shape[0]
    t = min(512, s)
    nb = s // t

    def body(q_ref, k_ref, v_ref, do_ref, st_ref, dq_ref, dk_ref, dv_ref):
        ki = pl.program_id(1)
        kk, vv = k_ref[...], v_ref[...]

        @pl.when(ki == 0)
        def _():
            dq_ref[...] = jnp.zeros_like(dq_ref)

        def block(qi, carry, masked):
            dk, dv = carry
            rows = pl.ds(pl.multiple_of(qi * t, t), t)
            qq, dob = q_ref[rows, :], do_ref[rows, :]
            sc = _nt(qq, kk)
            if masked:
                sc = jnp.where(_diag_mask(t), sc, NEG)
            st = st_ref[rows, :]
            p = jnp.exp2(sc * EXP2_SCALE - st[:, 0:1])
            dp = _nt(dob, vv)
            ds = (p * (dp - st[:, 1:2])).astype(BF16)
            dq_ref[rows, :] += _nn(ds, kk)
            return dk + _tn(ds, qq), dv + _tn(p.astype(BF16), dob)

        rest = nb - 1 - ki
        zeros = (jnp.zeros((t, QK_PAD), F32), jnp.zeros((t, V_DIM), F32))
        first = ki + 1

        def pair(q0, c):
            return block(q0 + 1, block(q0, c, False), False)

        heads = [lambda c: block(ki, c, True),
                 lambda c: block(first, block(ki, c, True), False),
                 lambda c: pair(first, block(ki, c, True)),
                 lambda c: block(first + 2, pair(first, block(ki, c, True)), False)]
        carry = lax.switch(rest & 3, heads, zeros)
        start = first + (rest & 3)
        dk, dv = lax.fori_loop(0, rest // 4, lambda j, c: pair(start + 4 * j + 2, pair(start + 4 * j, c)), carry)
        dk_ref[...] = dk * SM_SCALE
        dv_ref[...] = dv

        @pl.when(ki == nb - 1)
        def _():
            dq_ref[...] = dq_ref[...] * SM_SCALE

    def head(h, ki):
        return (0, h)

    def kblock(h, ki):
        return (ki, h)

    return _pcall(
        body, (q, k, v, do, stats), deps, name=name, grid=(N_HEADS, nb),
        out_shape=[jax.ShapeDtypeStruct((s, N_HEADS * QK_PAD), F32), jax.ShapeDtypeStruct((s, N_HEADS * QK_PAD), F32),
                   jax.ShapeDtypeStruct((s, N_HEADS * V_DIM), F32)],
        in_specs=[pl.BlockSpec((s, QK_PAD), head), pl.BlockSpec((t, QK_PAD), kblock), pl.BlockSpec((t, V_DIM), kblock),
                  pl.BlockSpec((s, V_DIM), head), pl.BlockSpec((s, LANES), head)],
        out_specs=[pl.BlockSpec((s, QK_PAD), head), pl.BlockSpec((t, QK_PAD), kblock), pl.BlockSpec((t, V_DIM), kblock)],
        compiler_params=_cparams(2, VMEM_MID),
    )


def _o_fwd(x, o, wo, name):
    s = x.shape[0]
    t = min(512, s)

    def body(x_ref, o_ref, wo_ref, xo_ref):
        xo_ref[...] = x_ref[...] + _nn(o_ref[...], wo_ref[...])

    return pl.pallas_call(
        body, name=name, grid=(s // t,),
        out_shape=jax.ShapeDtypeStruct((s, D_MODEL), F32),
        in_specs=[_rows(t, D_MODEL), _rows(t, D_MODEL), _full((D_MODEL, D_MODEL))],
        out_specs=_rows(t, D_MODEL),
        compiler_params=_cparams(1, VMEM_MID),
    )(x, o, wo)


def _o_bwd(dx, wo, o, lse, name, deps=()):
    s = dx.shape[0]
    t = min(512, s)

    def body(dx_ref, wo_ref, o_ref, lse_ref, do_ref, dxb_ref, st_ref):
        dxb = dx_ref[...].astype(BF16)
        dxb_ref[...] = dxb
        dob = _nt(dxb, wo_ref[...]).astype(BF16)
        do_ref[...] = dob
        lane = lax.broadcasted_iota(jnp.int32, (t, LANES), 1)
        for h in range(N_HEADS):
            sl = slice(h * V_DIM, (h + 1) * V_DIM)
            dsum = jnp.sum(dob[:, sl].astype(F32) * o_ref[:, sl].astype(F32), axis=-1, keepdims=True)
            st_ref[:, sl] = jnp.where(lane == 0, lse_ref[:, sl] * LOG2_E, jnp.where(lane == 1, dsum, 0.0))

    tok = jax.ShapeDtypeStruct((s, D_MODEL), BF16)
    return _pcall(
        body, (dx, wo, o, lse), deps, name=name, grid=(s // t,),
        out_shape=[tok, tok, jax.ShapeDtypeStruct((s, N_HEADS * LANES), F32)],
        in_specs=[_rows(t, D_MODEL), _full((D_MODEL, D_MODEL)), _rows(t, D_MODEL), _rows(t, N_HEADS * LANES)],
        out_specs=[_rows(t, D_MODEL), _rows(t, D_MODEL), _rows(t, N_HEADS * LANES)],
        compiler_params=_cparams(1, VMEM_MID),
    )


def _loss_head(y, target, name):
    s = y.shape[0]
    t = min(512, s)

    def body(y_ref, t_ref, dy_ref, sq_ref):
        e = y_ref[...] - t_ref[...]
        dy_ref[...] = e * (1.0 / D_MODEL)

        @pl.when(pl.program_id(0) == 0)
        def _():
            sq_ref[...] = jnp.zeros_like(sq_ref)

        sq_ref[...] += jnp.sum(e * e, axis=0, keepdims=True)

    return pl.pallas_call(
        body, name=name, grid=(s // t,),
        out_shape=[jax.ShapeDtypeStruct((s, D_MODEL), F32), jax.ShapeDtypeStruct((1, D_MODEL), F32)],
        in_specs=[_rows(t, D_MODEL), _rows(t, D_MODEL)],
        out_specs=[_rows(t, D_MODEL), _full((1, D_MODEL))],
        compiler_params=_cparams(1),
    )(y, target)


def _adamw(w, g, m, v, name):
    shape = w.shape
    c = shape[-1]
    r = math.prod(shape[:-1])
    tb = r
    for cand in (512, 256, 128):
        if r % cand == 0 and r > cand:
            tb = cand
            break

    def body(w_ref, g_ref, m_ref, v_ref, d_ref, mo_ref, vo_ref):
        gr = g_ref[...]
        mn = ADAM_B1 * m_ref[...] + (1.0 - ADAM_B1) * gr
        vn = ADAM_B2 * v_ref[...] + (1.0 - ADAM_B2) * (gr * gr)
        m_hat = mn / (1.0 - ADAM_B1 ** ADAM_STEP)
        v_hat = vn / (1.0 - ADAM_B2 ** ADAM_STEP)
        d_ref[...] = -ADAM_LR * (m_hat / (jnp.sqrt(v_hat) + ADAM_EPS) + ADAM_WD * w_ref[...])
        mo_ref[...] = mn
        vo_ref[...] = vn

    spec = pl.BlockSpec((tb, c), lambda i: (i, 0))
    flat = jax.ShapeDtypeStruct((r, c), F32)
    outs = pl.pallas_call(
        body, name=name, grid=(r // tb,),
        out_shape=[flat, flat, flat],
        in_specs=[spec] * 4, out_specs=[spec] * 3,
        compiler_params=_cparams(1),
    )(w.reshape(r, c), g.reshape(r, c), m.reshape(r, c), v.reshape(r, c))
    return [a.reshape(shape) for a in outs]


def _pad_cols(a, width):
    return jnp.pad(a, [(0, 0)] * (a.ndim - 1) + [(0, width - a.shape[-1])])


def _owner_view(a, sz):
    return a.reshape(a.shape[0], N_CHIPS, 2, sz, a.shape[-1])


def kernel(x, positions, ln_mix_a, w_pool, b_pool, pool_scale, ln_ffn, w_gate, w_up, w_down, ln_kv, w_dkv, g_kv_latent, w_uk, w_uv, g_k, ln_mix_b, w_dq, g_q_latent, w_uq, g_q, w_o, loss_target, m_ln_mix_a, m_w_pool, m_b_pool, m_pool_scale, m_ln_ffn, m_w_gate, m_w_up, m_w_down, m_ln_kv, m_w_dkv, m_g_kv_latent, m_w_uk, m_w_uv, m_g_k, m_ln_mix_b, m_w_dq, m_g_q_latent, m_w_uq, m_g_q, m_w_o, v_ln_mix_a, v_w_pool, v_b_pool, v_pool_scale, v_ln_ffn, v_w_gate, v_w_up, v_w_down, v_ln_kv, v_w_dkv, v_g_kv_latent, v_w_uk, v_w_uv, v_g_k, v_ln_mix_b, v_w_dq, v_g_q_latent, v_w_uq, v_g_q, v_w_o):
    weights = dict(ln_mix_a=ln_mix_a, w_pool=w_pool, b_pool=b_pool, pool_scale=pool_scale, ln_ffn=ln_ffn,
                   w_gate=w_gate, w_up=w_up, w_down=w_down, ln_kv=ln_kv, w_dkv=w_dkv, g_kv_latent=g_kv_latent,
                   w_uk=w_uk, w_uv=w_uv, g_k=g_k, ln_mix_b=ln_mix_b, w_dq=w_dq, g_q_latent=g_q_latent,
                   w_uq=w_uq, g_q=g_q, w_o=w_o)
    mom1 = dict(ln_mix_a=m_ln_mix_a, w_pool=m_w_pool, b_pool=m_b_pool, pool_scale=m_pool_scale, ln_ffn=m_ln_ffn,
                w_gate=m_w_gate, w_up=m_w_up, w_down=m_w_down, ln_kv=m_ln_kv, w_dkv=m_w_dkv,
                g_kv_latent=m_g_kv_latent, w_uk=m_w_uk, w_uv=m_w_uv, g_k=m_g_k, ln_mix_b=m_ln_mix_b, w_dq=m_w_dq,
                g_q_latent=m_g_q_latent, w_uq=m_w_uq, g_q=m_g_q, w_o=m_w_o)
    mom2 = dict(ln_mix_a=v_ln_mix_a, w_pool=v_w_pool, b_pool=v_b_pool, pool_scale=v_pool_scale, ln_ffn=v_ln_ffn,
                w_gate=v_w_gate, w_up=v_w_up, w_down=v_w_down, ln_kv=v_ln_kv, w_dkv=v_w_dkv,
                g_kv_latent=v_g_kv_latent, w_uk=v_w_uk, w_uv=v_w_uv, g_k=v_g_k, ln_mix_b=v_ln_mix_b, w_dq=v_w_dq,
                g_q_latent=v_g_q_latent, w_uq=v_w_uq, g_q=v_g_q, w_o=v_w_o)
    names = list(weights)
    dev = 4 * lax.axis_index("x") + 2 * lax.axis_index("y") + lax.axis_index("c")
    core = lax.axis_index("c").astype(jnp.int32).reshape(1)
    chip = (2 * lax.axis_index("x") + lax.axis_index("y")).astype(jnp.int32).reshape(1)

    xs = x[0]
    target = loss_target[0]
    cos, sin = _rope_tables(positions[0])

    def placed(shard):
        buf = lax.empty((shard.shape[0], N_DEV) + shard.shape[1:], shard.dtype)
        return lax.dynamic_update_slice(buf, shard[:, None], (0, dev, 0, 0))

    def ffn_shard(l):
        return jnp.stack([w_gate[l].T, w_up[l].T, w_down[l]]).astype(BF16)

    groups = {"ffn0": [placed(ffn_shard(0))]}
    small_sh = jnp.concatenate([ln_mix_a.reshape(1, -1), pool_scale.reshape(1, -1), b_pool.reshape(1, -1)], axis=1)
    wp_g, small_g = _all_gather([w_pool.astype(BF16), small_sh], [2, 0], "gather_first")
    wp_all = wp_g.reshape(2, 4, GROUP_DIM, GROUP_DIM)
    small_g = small_g.reshape(N_DEV, 3, 2, LANES)
    ln_a_all = small_g[:, 0].transpose(1, 0, 2).reshape(2, 1, D_MODEL)
    sc_all = small_g[:, 1].transpose(1, 0, 2).reshape(2, 1, D_MODEL)
    bp_all = small_g[:, 2].reshape(N_DEV, 2, 4, 32).transpose(1, 2, 0, 3).reshape(2, 1, D_MODEL)
    sp0 = _copies_start(groups["ffn0"], 1, _gather_spread, "spread_ffn0", deps=[small_g])
    zero = sp0[3][0, 0].astype(BF16)
    for l in (1, 2, 3):
        groups[f"ffn{l}"] = [placed(ffn_shard(l) + zero)]
    groups["att"] = [placed(a.astype(BF16) + zero) for a in (
        w_dkv[None, :, :KV_RANK], _pad_cols(w_dkv[None, :, KV_RANK:], LANES), w_uk[None], w_uv[None],
        w_dq, _pad_cols(w_uq, QK_PAD), w_o)]

    def spread_start(nm, deps):
        return _copies_start(groups[nm], len(groups[nm]), _gather_spread, f"spread_{nm}", deps=deps)

    def spread_wait(nm, state, after):
        ssem, rsem, bufs, _ = state
        return _copies_wait(bufs, ssem, rsem, after, _blocks_moved(4), f"spread_done_{nm}")

    def relay_start(nm, bufs, deps=()):
        return _copies_start(bufs, len(bufs), _gather_relay, f"relay_{nm}", deps=deps)

    def relay_wait(nm, state, after):
        ssem, rsem, bufs, _ = state
        return _copies_wait(bufs, ssem, rsem, after, _blocks_moved(3), f"relay_done_{nm}")

    gkn = g_k[:NOPE].reshape(1, NOPE)
    gkr = _pad_cols(g_k[NOPE:].reshape(1, ROPE), LANES)
    gl = g_kv_latent.reshape(1, KV_RANK)
    lnkv = ln_kv.reshape(1, D_MODEL)

    x_in, x_mid, pooled, gates, ups, w_ffn = [], [], [], [], [], []
    qs, outs, lses = [], [], []

    def mixer(l, cur, deps):
        x_in.append(cur)
        mid, dsave = _mix_fwd(cur, ln_a_all[l], wp_all[l], bp_all[l], sc_all[l], f"mix_fwd{l}", deps=deps)
        pooled.append(dsave)
        x_mid.append(mid)
        return mid

    def q_args(j):
        return (ln_mix_b[j].reshape(1, -1), wdq_all[j], g_q_latent[j].reshape(1, -1), wuq_all[j],
                g_q[j, :NOPE].reshape(1, -1), _pad_cols(g_q[j, NOPE:].reshape(1, -1), LANES), cos, sin)

    def attention(j, cur, deps):
        x_in.append(cur)
        q = _q_fwd(cur, *q_args(j), f"q_fwd{j}", deps=deps)
        o, lse = _att_fwd(q, k_sh, v_sh, f"att_fwd{j}")
        mid = _o_fwd(cur, o, wo_all[j], f"o_fwd{j}")
        qs.append(q)
        outs.append(o)
        lses.append(lse)
        x_mid.append(mid)
        return mid

    def ffn(l, mid, relayed):
        w_l = relayed[0].reshape(3, D_FF, D_MODEL)
        w_ffn.append(w_l)
        cur, gate, up = _ffn_fwd(mid, ln_ffn[l].reshape(1, -1), w_l, f"ffn_fwd{l}")
        gates.append(gate)
        ups.append(up)
        return cur

    mid = mixer(0, xs, [sp0[3]])
    prepared = [buf for nm in ("ffn1", "att", "ffn2", "ffn3") for buf in groups[nm]]
    landed0 = spread_wait("ffn0", sp0, [mid] + prepared)
    sp1 = spread_start("ffn1", [landed0[0]])
    rl0 = relay_start("ffn0", landed0, [sp1[3]])
    cur = ffn(0, mid, relay_wait("ffn0", rl0, rl0[3]))

    landed1 = spread_wait("ffn1", sp1, cur)
    sp_att = spread_start("att", [landed1[0]])
    sp2 = spread_start("ffn2", [landed1[0]])
    rl1 = relay_start("ffn1", landed1, [sp_att[3], sp2[3]])
    mid = mixer(1, cur, [rl1[3]])
    cur = ffn(1, mid, relay_wait("ffn1", rl1, mid))
    x_kv = cur

    landed_att = spread_wait("att", sp_att, cur)
    landed2 = spread_wait("ffn2", sp2, cur)
    sp3 = spread_start("ffn3", [landed2[0]])
    rl_att = relay_start("att", landed_att, [sp3[3]])
    rl2 = relay_start("ffn2", landed2, [sp3[3]])
    att_bufs = relay_wait("att", rl_att, rl2[3])
    wc = att_bufs[0].reshape(D_MODEL, KV_RANK)
    wpe = att_bufs[1].reshape(D_MODEL, LANES)
    wuk_g = att_bufs[2].reshape(N_HEADS, KV_RANK, NOPE).transpose(1, 0, 2).reshape(KV_RANK, N_HEADS * NOPE)
    wuv_g = att_bufs[3].reshape(N_HEADS, KV_RANK, V_DIM).transpose(1, 0, 2).reshape(KV_RANK, N_HEADS * V_DIM)
    wdq_all = att_bufs[4].reshape(2, D_MODEL, Q_RANK)
    wuq_all = att_bufs[5]
    wo_all = att_bufs[6].reshape(2, D_MODEL, D_MODEL)
    k_sh, v_sh = _kv_fwd(cur, lnkv, wc, wpe, gl, wuk_g, wuv_g, gkn, gkr, cos, sin, "kv_fwd")
    mid = attention(0, cur, [])
    cur = ffn(2, mid, relay_wait("ffn2", rl2, mid))

    landed3 = spread_wait("ffn3", sp3, cur)
    rl3 = relay_start("ffn3", landed3)
    mid = attention(1, cur, [rl3[3]])
    cur = ffn(3, mid, relay_wait("ffn3", rl3, mid))

    dx, sq_cols = _loss_head(cur, target, "loss_head")

    small = {}
    sizes = dict(ffn0=FF_SHARD, ffn1=FF_SHARD, ffn2=FF_SHARD, ffn3=FF_SHARD, wo=128, kv512=128, dkv_pe=128,
                 wdq=128, wuqT=QK_PAD, wpool=32)
    big = dict(wo=lax.empty((2, D_MODEL, D_MODEL), BF16), kv512=lax.empty((3, D_MODEL, KV_RANK), BF16),
               dkv_pe=lax.empty((1, D_MODEL, LANES), BF16), wdq=lax.empty((2, D_MODEL, Q_RANK), BF16),
               wuqT=lax.empty((2, N_HEADS * QK_PAD, Q_RANK), BF16), wpool=lax.empty((8, GROUP_DIM, GROUP_DIM), BF16))
    for l in range(4):
        big[f"ffn{l}"] = lax.empty((3, D_FF, D_MODEL), BF16)
    red = {}

    def pair_start(nms, tag):
        arrs = []
        for nm in nms:
            view = _owner_view(big[nm], sizes[nm])
            arrs += [view, lax.empty((view.shape[0], N_CHIPS) + view.shape[3:], BF16)]
        return nms, tag, _copies_start(arrs, len(nms), _pair_send, f"pair_start_{tag}")

    def chip_start(state, after):
        nms, tag, (ssem, rsem, arrs, _) = state
        arrs = _copies_wait(arrs, ssem, rsem, after, _landed, f"pair_done_{tag}")
        out = []
        for t, nm in enumerate(nms):
            part = _pair_sum(arrs[2 * t], arrs[2 * t + 1], core, f"pair_sum_{nm}")
            out += [part, lax.empty((3, part.shape[0]) + part.shape[2:], BF16)]
        return nms, tag, _copies_start(out, len(nms), _chip_send, f"chip_start_{tag}")

    deferred = []
    updates = {}

    def chip_finish(state, after, defer=False):
        nms, tag, (ssem, rsem, arrs, _) = state
        arrs = _copies_wait(arrs, ssem, rsem, after, _landed, f"chip_done_{tag}")
        for t, nm in enumerate(nms):
            if defer:
                deferred.append((nm, arrs[2 * t], arrs[2 * t + 1]))
            else:
                red[nm] = _chip_sum(arrs[2 * t], arrs[2 * t + 1], chip, f"chip_sum_{nm}")

    ffn_grads = {nm: lax.empty((4, FF_SHARD, D_MODEL), F32) for nm in ("w_gate", "w_up", "w_down")}

    def place_ffn_grads(l):
        g = red[f"ffn{l}"]
        for k, nm in enumerate(("w_gate", "w_up", "w_down")):
            ffn_grads[nm] = ffn_grads[nm].at[l].set(g[k])

    dks, dvs = [], []
    pending = None
    bwd_deps = []
    for l in (3, 2, 1, 0):
        key = f"ffn{l}"
        dx, act, dgb, dub, hn, dyb, dln = _ffn_bwd(x_mid[l], dx, gates[l], ups[l], ln_ffn[l].reshape(1, -1),
                                                     w_ffn[l], f"ffn_bwd{l}", deps=bwd_deps)
        bwd_deps = []
        small[f"ln_ffn{l}"] = dln
        if l == 1:
            att_chip = chip_start(att_pair, dx)
            tn_deps = [att_chip[2][3]]
        else:
            tn_deps = []
        if pending:
            chip_finish(pending, dx, defer=True)
            pending = None
        big[key] = _tn_matmul(dgb, hn, big[key], 0, f"dw_gate{l}", m_chunk=FF_HALF, deps=tn_deps)
        big[key] = _tn_matmul(dub, hn, big[key], 1, f"dw_up{l}", m_chunk=FF_HALF)
        big[key] = _tn_matmul(act, dyb, big[key], 2, f"dw_down{l}", m_chunk=FF_HALF)
        if l == 1:
            chip_finish(att_chip, big[key], defer=True)
        ffn_pair = pair_start([key], key)
        if l >= 2:
            j = l - 2
            do, dxb, stats = _o_bwd(dx, wo_all[j], outs[j], lses[j], f"o_bwd{j}", deps=[ffn_pair[2][3]])
            big["wo"] = _tn_matmul(outs[j], dxb, big["wo"], j, f"dw_o{j}")
            ffn_chip = chip_start(ffn_pair, big["wo"])
            dq, dk, dv = _att_bwd(qs[j], k_sh, v_sh, do, stats, f"att_bwd{j}", deps=[ffn_chip[2][3]])
            chip_finish(ffn_chip, dq, defer=True)
            dks.append(dk)
            dvs.append(dv)
            dx, hnq, cqn, dqa, dcq, dln, dgql, dgqn, dgqr = _q_bwd(x_in[l], dx, dq, *q_args(j), f"q_bwd{j}")
            small[f"ln_mix_b{j}"] = dln
            small[f"g_q_latent{j}"] = dgql
            small[f"g_q{j}"] = jnp.concatenate([dgqn, dgqr[:, :ROPE]], axis=1)
            big["wdq"] = _tn_matmul(hnq, dcq, big["wdq"], j, f"dw_dq{j}")
            big["wuqT"] = _tn_matmul(dqa, cqn, big["wuqT"], j, f"dw_uq{j}")
            if l == 2:
                (dx, hnk, cn, dknb, dvb, dccb, dpeb, dlnkv, dgl, dgkn, dgkr) = _kv_bwd(
                    x_kv, dx, dks, dvs, lnkv, wc, wpe, gl, wuk_g, wuv_g, gkn, gkr, cos, sin, "kv_bwd")
                small["ln_kv"] = dlnkv
                small["g_kv_latent"] = dgl
                small["g_k"] = jnp.concatenate([dgkn, dgkr[:, :ROPE]], axis=1)
                big["kv512"] = _tn_matmul(dknb, cn, big["kv512"], 0, "dw_uk")
                big["kv512"] = _tn_matmul(dvb, cn, big["kv512"], 1, "dw_uv")
                big["kv512"] = _tn_matmul(hnk, dccb, big["kv512"], 2, "dw_dkv_c")
                big["dkv_pe"] = _tn_matmul(hnk, dpeb, big["dkv_pe"], 0, "dw_dkv_pe")
                att_pair = pair_start(["wo", "kv512", "dkv_pe", "wdq", "wuqT"], "att")
                bwd_deps = [att_pair[2][3]]
        else:
            dx, dyp, dsc, db, dln = _mix_bwd(x_in[l], dx, pooled[l], ln_a_all[l], wp_all[l], bp_all[l], sc_all[l],
                                             f"mix_bwd{l}", deps=[ffn_pair[2][3]])
            small[f"ln_mix_a{l}"] = dln
            small[f"pool_scale{l}"] = dsc
            small[f"b_pool{l}"] = db
            ffn_chip = chip_start(ffn_pair, dx)
            big["wpool"] = _tn_matmul(pooled[l], dyp, big["wpool"], 4 * l, f"dw_pool{l}", groups=4,
                                      deps=[ffn_chip[2][3]])
            if l == 1:
                pending = ffn_chip
                bwd_deps = [ffn_chip[2][3]]
            else:
                for nm, part, land in deferred:
                    red[nm] = _chip_sum(part, land, chip, f"chip_sum_{nm}", deps=[ffn_chip[2][3]])
                    if nm.startswith("ffn"):
                        place_ffn_grads(int(nm[-1]))
                early_grads = dict(
                    w_dkv=jnp.concatenate([red["kv512"][2], red["dkv_pe"][0][:, :ROPE]], axis=1),
                    w_uk=red["kv512"][0].T, w_uv=red["kv512"][1].T, w_dq=red["wdq"],
                    w_uq=red["wuqT"].transpose(0, 2, 1)[:, :, :QK_DIM], w_o=red["wo"])
                for nm, g in early_grads.items():
                    updates[nm] = _adamw(weights[nm], g, mom1[nm], mom2[nm], f"adamw_{nm}")
                chip_finish(ffn_chip, [big["wpool"]] + list(ffn_grads.values()) + [u[0] for u in updates.values()])
                place_ffn_grads(0)
    grad_x = dx[None]
    pool_pair = pair_start(["wpool"], "wpool")
    pool_chip = chip_start(pool_pair, pool_pair[2][3])
    chip_finish(pool_chip, pool_chip[2][3])

    vec_names = (["loss"] + [f"ln_ffn{l}" for l in range(4)] + ["ln_kv", "g_kv_latent", "g_k"]
                 + [f"{p}{j}" for p in ("ln_mix_b", "g_q_latent", "g_q") for j in range(2)]
                 + [f"{p}{l}" for p in ("ln_mix_a", "pool_scale", "b_pool") for l in range(2)])
    small["loss"] = sq_cols
    widths = [small[nm].shape[1] for nm in vec_names]
    padded = [-(-w // LANES) * LANES for w in widths]
    packed = jnp.concatenate([_pad_cols(small[nm], pw) for nm, pw in zip(vec_names, padded)], axis=1)
    (all_vecs,) = _all_gather([packed], [0], "gather_vectors")
    total = _sum_lead(all_vecs, "sum_vectors")
    vec = {}
    off = 0
    for nm, w, pw in zip(vec_names, widths, padded):
        vec[nm] = total[0, off:off + w]
        off += pw
    loss = 0.5 * jnp.sum(vec["loss"]) * (1.0 / D_MODEL)

    def own_cols(full, width):
        return lax.dynamic_slice_in_dim(full, dev * width, width, axis=full.ndim - 1)

    grads = dict(
        ln_mix_a=own_cols(jnp.stack([vec["ln_mix_a0"], vec["ln_mix_a1"]]), LANES),
        w_pool=red["wpool"].reshape(2, 4, 32, GROUP_DIM),
        b_pool=own_cols(jnp.stack([vec["b_pool0"], vec["b_pool1"]]).reshape(2, 4, GROUP_DIM), 32),
        pool_scale=own_cols(jnp.stack([vec["pool_scale0"], vec["pool_scale1"]]), LANES),
        ln_ffn=jnp.stack([vec[f"ln_ffn{l}"] for l in range(4)]),
        w_gate=ffn_grads["w_gate"],
        w_up=ffn_grads["w_up"],
        w_down=ffn_grads["w_down"],
        ln_kv=vec["ln_kv"],
        g_kv_latent=vec["g_kv_latent"],
        g_k=vec["g_k"],
        ln_mix_b=jnp.stack([vec["ln_mix_b0"], vec["ln_mix_b1"]]),
        g_q_latent=jnp.stack([vec["g_q_latent0"], vec["g_q_latent1"]]),
        g_q=jnp.stack([vec["g_q0"], vec["g_q1"]]),
        **early_grads,
    )

    deltas, new_m, new_v = {}, {}, {}
    for nm in names:
        w = weights[nm]
        if nm in updates:
            deltas[nm], new_m[nm], new_v[nm] = updates[nm]
            continue
        if nm in ("w_gate", "w_up"):
            def swap(a):
                return a.transpose(0, 2, 1)
            d, mo, vo = _adamw(swap(w), grads[nm], swap(mom1[nm]), swap(mom2[nm]), f"adamw_{nm}")
            deltas[nm], new_m[nm], new_v[nm], grads[nm] = swap(d), swap(mo), swap(vo), swap(grads[nm])
            continue
        shape = w.shape if w.ndim > 1 else (1, w.shape[0])
        d, mo, vo = _adamw(w.reshape(shape), grads[nm].reshape(shape), mom1[nm].reshape(shape),
                           mom2[nm].reshape(shape), f"adamw_{nm}")
        deltas[nm], new_m[nm], new_v[nm] = d.reshape(w.shape), mo.reshape(w.shape), vo.reshape(w.shape)

    return (loss, grad_x, *[grads[nm].reshape(weights[nm].shape) for nm in names], *[deltas[nm] for nm in names],
            *[new_m[nm] for nm in names], *[new_v[nm] for nm in names])
```

```python
import functools
import math

import jax
import jax.numpy as jnp
from jax import lax
from jax.experimental import pallas as pl
from jax.experimental.pallas import tpu as pltpu

F32 = jnp.float32
BF16 = jnp.bfloat16
MESH = pl.DeviceIdType.MESH

D_MODEL = 1024
D_FF = 2816
N_DEV = 8
N_CHIPS = 4
FF_SHARD = D_FF // N_DEV
FF_HALF = D_FF // 2
N_HEADS = 8
NOPE = 128
ROPE = 64
QK_DIM = NOPE + ROPE
QK_PAD = 256
V_DIM = 128
Q_RANK = 256
KV_RANK = 512
POOL_WINDOWS = (2, 4, 8, 16)
GROUP_DIM = 256
HALO = 128
CHUNK = 64
ROPE_THETA = 10000.0
EPS = 1e-6
LANES = 128

ADAM_LR = 0.001
ADAM_B1 = 0.9
ADAM_B2 = 0.999
ADAM_EPS = 1e-08
ADAM_WD = 0.01
ADAM_STEP = 10

PROJ_ROWS = 256
MIX_ROWS = 256
VMEM_BIG = 56 * 2**20
VMEM_MID = 40 * 2**20


def _nn(a, b):
    return lax.dot_general(a, b, (((1,), (0,)), ((), ())), preferred_element_type=F32)


def _nt(a, b):
    return lax.dot_general(a, b, (((1,), (1,)), ((), ())), preferred_element_type=F32)


def _tn(a, b):
    return lax.dot_general(a, b, (((0,), (0,)), ((), ())), preferred_element_type=F32)


def _rms(x, g, n):
    r = lax.rsqrt(jnp.sum(x * x, axis=-1, keepdims=True) * (1.0 / n) + EPS)
    return (x * r) * g, r


def _rms_bwd(x, r, g, dy, n):
    u = dy * g
    s = jnp.sum(x * u, axis=-1, keepdims=True) * (1.0 / n)
    dx = r * u - x * (r * r * r * s)
    dg = jnp.sum(dy * (x * r), axis=0, keepdims=True)
    return dx, dg


def _swap_perm():
    i = lax.broadcasted_iota(jnp.int32, (LANES, LANES), 0)
    j = lax.broadcasted_iota(jnp.int32, (LANES, LANES), 1)
    half = ROPE // 2
    hit = ((j < half) & (i == j + half)) | ((j >= half) & (j < ROPE) & (i == j - half))
    return jnp.where(hit, 1.0, 0.0).astype(BF16)


def _swap_halves(z, perm):
    hi = z.astype(BF16)
    lo = (z - hi.astype(F32)).astype(BF16)
    return _nn(hi, perm) + _nn(lo, perm)


def _sigmoid(x):
    return 1.0 / (1.0 + jnp.exp(-x))


def _cparams(n_grid, vmem=None):
    return pltpu.CompilerParams(dimension_semantics=("arbitrary",) * n_grid, vmem_limit_bytes=vmem)


def _rows(t, cols):
    return pl.BlockSpec((t, cols), lambda i: (i, 0))


def _full(shape):
    nd = len(shape)
    return pl.BlockSpec(shape, lambda *_: (0,) * nd)


ANY = pl.BlockSpec(memory_space=pl.ANY)


def _pcall(body, args, deps, *, in_specs, **kw):
    n_in, n_dep = len(args), len(deps)

    def ordered(*refs):
        body(*refs[:n_in], *refs[n_in + n_dep:])

    return pl.pallas_call(ordered, in_specs=list(in_specs) + [ANY] * n_dep, **kw)(*args, *deps)


def _place():
    x, y, c = lax.axis_index("x"), lax.axis_index("y"), lax.axis_index("c")
    return x, y, c


def _all_gather(shards, axes, name, deps=()):
    n, nd = len(shards), len(deps)
    out_shape = [jax.ShapeDtypeStruct(s.shape[:a] + (N_DEV,) + s.shape[a:], s.dtype) for s, a in zip(shards, axes)]

    def body(*refs):
        ins, outs = refs[:n], refs[n + nd:2 * n + nd]
        send_sems, recv_sems, local_sems = refs[2 * n + nd:]
        x, y, c = _place()
        me, sibling = (x, y, c), (x, y, 1 - c)
        chips = [(1 - x, y), (x, 1 - y), (1 - x, 1 - y)]

        def slot(t, dev):
            idx = 4 * dev[0] + 2 * dev[1] + dev[2]
            return outs[t].at[(slice(None),) * axes[t] + (idx,)]

        def copy(t, k, block, to, src=None):
            return pltpu.make_async_remote_copy(
                src_ref=slot(t, block) if src is None else src, dst_ref=slot(t, block),
                send_sem=send_sems.at[t, k], recv_sem=recv_sems.at[t, k],
                device_id=to, device_id_type=MESH)

        mine = [pltpu.make_async_copy(ins[t], slot(t, me), local_sems.at[t]) for t in range(n)]
        for cp in mine:
            cp.start()
        first = []
        for t in range(n):
            first.append(copy(t, 0, me, sibling, src=ins[t]))
            first += [copy(t, 1 + j, me, (*chip, c), src=ins[t]) for j, chip in enumerate(chips)]
        for cp in first:
            cp.start()
        passed = []
        for j, chip in enumerate(chips):
            for t in range(n):
                copy(t, 1 + j, (*chip, c), me).wait_recv()
                cp = copy(t, 4 + j, (*chip, c), sibling)
                cp.start()
                passed.append(cp)
        for t in range(n):
            copy(t, 0, sibling, me).wait_recv()
            for j, chip in enumerate(chips):
                copy(t, 4 + j, (*chip, 1 - c), me).wait_recv()
        for cp in first + passed:
            cp.wait_send()
        for cp in mine:
            cp.wait()

    return pl.pallas_call(
        body, name=name, out_shape=out_shape,
        in_specs=[ANY] * (n + nd), out_specs=[ANY] * n,
        scratch_shapes=[pltpu.SemaphoreType.DMA((n, 7)), pltpu.SemaphoreType.DMA((n, 7)),
                        pltpu.SemaphoreType.DMA((n,))],
    )(*shards, *deps)


HBM = pl.BlockSpec(memory_space=pltpu.HBM)
SEM = pl.BlockSpec(memory_space=pltpu.SEMAPHORE)
EFFECT = pltpu.SideEffectType.DATAFLOW_SIDE_EFFECTING


def _copies_start(arrays, n_sems, plan, name, deps=()):
    n, nd = len(arrays), len(deps)

    def body(*refs):
        for cp in plan(refs[:n], refs[n + nd], refs[n + nd + 1]):
            cp.start()
        refs[-1][...] = jnp.zeros_like(refs[-1])

    outs = pl.pallas_call(
        body, name=name,
        out_shape=(pltpu.SemaphoreType.DMA((n_sems,)), pltpu.SemaphoreType.DMA((n_sems,)),
                   *[pltpu.HBM(a.shape, a.dtype) for a in arrays], jax.ShapeDtypeStruct((8, LANES), F32)),
        in_specs=[HBM] * n + [ANY] * nd,
        out_specs=(SEM, SEM, *[HBM] * n, pl.BlockSpec(memory_space=pltpu.VMEM)),
        input_output_aliases={i: 2 + i for i in range(n)},
        compiler_params=pltpu.CompilerParams(has_side_effects=EFFECT),
    )(*[pltpu.with_memory_space_constraint(a, pltpu.HBM) for a in arrays], *deps)
    return outs[0], outs[1], list(outs[2:2 + n]), outs[-1]


def _copies_wait(arrays, send_sems, recv_sems, after, plan, name):
    n = len(arrays)
    after = list(after) if isinstance(after, (list, tuple)) else [after]

    def body(*refs):
        for cp in plan(refs[:n], refs[n], refs[n + 1]):
            cp.wait_send()
            cp.wait_recv()

    outs = pl.pallas_call(
        body, name=name,
        out_shape=tuple(pltpu.HBM(a.shape, a.dtype) for a in arrays),
        in_specs=[HBM] * n + [SEM, SEM] + [ANY] * len(after), out_specs=tuple([HBM] * n),
        input_output_aliases={i: i for i in range(n)},
        compiler_params=pltpu.CompilerParams(has_side_effects=EFFECT),
    )(*arrays, send_sems, recv_sems, *after)
    return list(outs)


def _remote(src, dst, send_sems, recv_sems, t, to):
    return pltpu.make_async_remote_copy(src_ref=src, dst_ref=dst, send_sem=send_sems.at[t], recv_sem=recv_sems.at[t],
                                        device_id=to, device_id_type=MESH)


def _dev_index(x, y, c):
    return 4 * x + 2 * y + c


def _gather_spread(bufs, send_sems, recv_sems):
    x, y, c = _place()
    mine = _dev_index(x, y, c)
    peers = [(x, y, 1 - c), (1 - x, y, c), (x, 1 - y, c), (1 - x, 1 - y, c)]
    return [_remote(g.at[k, mine], g.at[k, mine], send_sems, recv_sems, t, peer)
            for t, g in enumerate(bufs) for peer in peers for k in range(g.shape[0])]


def _gather_relay(bufs, send_sems, recv_sems):
    x, y, c = _place()
    blocks = [_dev_index(1 - x, y, c), _dev_index(x, 1 - y, c), _dev_index(1 - x, 1 - y, c)]
    return [_remote(g.at[k, b], g.at[k, b], send_sems, recv_sems, t, (x, y, 1 - c))
            for t, g in enumerate(bufs) for b in blocks for k in range(g.shape[0])]


def _blocks_moved(count):
    def plan(bufs, send_sems, recv_sems):
        x, y, c = _place()
        return [_remote(g.at[:, pl.ds(0, count)], g.at[:, pl.ds(0, count)], send_sems, recv_sems, t, (x, y, 1 - c))
                for t, g in enumerate(bufs)]
    return plan


def _pair_send(arrs, send_sems, recv_sems):
    x, y, c = _place()
    return [_remote(arrs[2 * t].at[p, k, 1 - c], arrs[2 * t + 1].at[p, k], send_sems, recv_sems, t, (x, y, 1 - c))
            for t in range(len(arrs) // 2) for p in range(arrs[2 * t].shape[0]) for k in range(N_CHIPS)]


def _chip_send(arrs, send_sems, recv_sems):
    x, y, c = _place()
    chips = [(1 - x, y), (x, 1 - y), (1 - x, 1 - y)]
    return [_remote(arrs[2 * t].at[p, 2 * px + py], arrs[2 * t + 1].at[j, p], send_sems, recv_sems, t, (px, py, c))
            for t in range(len(arrs) // 2) for j, (px, py) in enumerate(chips) for p in range(arrs[2 * t].shape[0])]


def _landed(arrs, send_sems, recv_sems):
    x, y, c = _place()
    return [_remote(arrs[2 * t + 1], arrs[2 * t + 1], send_sems, recv_sems, t, (x, y, 1 - c))
            for t in range(len(arrs) // 2)]


def _rows_per_step(rows, row_elems):
    best = 1
    for cand in range(1, rows + 1):
        if rows % cand == 0 and cand * row_elems <= 256 * 1024:
            best = cand
    return best


def _pair_sum(grad, landed, core, name):
    p, _, _, sz, c = grad.shape
    r = _rows_per_step(p * N_CHIPS, sz * c)

    def body(core_ref, g_ref, l_ref, o_ref):
        o_ref[...] = (g_ref[...].astype(F32) + l_ref[...].astype(F32)).astype(o_ref.dtype)

    out = pl.pallas_call(
        body, name=name,
        grid_spec=pltpu.PrefetchScalarGridSpec(
            num_scalar_prefetch=1, grid=(p * N_CHIPS // r,),
            in_specs=[pl.BlockSpec((r, None, sz, c), lambda i, cr: (i, cr[0], 0, 0)),
                      pl.BlockSpec((r, sz, c), lambda i, cr: (i, 0, 0))],
            out_specs=pl.BlockSpec((r, sz, c), lambda i, cr: (i, 0, 0))),
        out_shape=jax.ShapeDtypeStruct((p * N_CHIPS, sz, c), grad.dtype),
        compiler_params=_cparams(1),
    )(core, grad.reshape(p * N_CHIPS, 2, sz, c), landed.reshape(p * N_CHIPS, sz, c))
    return out.reshape(p, N_CHIPS, sz, c)


def _chip_sum(parts, landed, chip, name, deps=()):
    p, _, sz, c = parts.shape
    r = _rows_per_step(p, sz * c)

    def body(chip_ref, a_ref, l_ref, o_ref):
        acc = a_ref[...].astype(F32)
        for j in range(3):
            acc = acc + l_ref[j].astype(F32)
        o_ref[...] = acc

    nd = len(deps)

    def ordered(chip_ref, a_ref, l_ref, *rest):
        body(chip_ref, a_ref, l_ref, rest[nd])

    return pl.pallas_call(
        ordered, name=name,
        grid_spec=pltpu.PrefetchScalarGridSpec(
            num_scalar_prefetch=1, grid=(p // r,),
            in_specs=[pl.BlockSpec((r, None, sz, c), lambda i, cr: (i, cr[0], 0, 0)),
                      pl.BlockSpec((3, r, sz, c), lambda i, cr: (0, i, 0, 0))] + [ANY] * nd,
            out_specs=pl.BlockSpec((r, sz, c), lambda i, cr: (i, 0, 0))),
        out_shape=jax.ShapeDtypeStruct((p, sz, c), F32),
        compiler_params=_cparams(1),
    )(chip, parts, landed, *deps)


def _sum_lead(a, name, out_dtype=F32):
    k = a.shape[0]
    rest = a.shape[1:]
    r, c = rest[-2], rest[-1]
    lead = math.prod(rest[:-2])
    a3 = a.reshape(k, lead * r, c)
    rows = lead * r
    tb = rows
    for cand in (512, 256, 128, 64, 32, 16, 8):
        if rows % cand == 0 and rows > cand:
            tb = cand
            break

    def body(a_ref, o_ref):
        acc = a_ref[0].astype(F32)
        for i in range(1, k):
            acc = acc + a_ref[i].astype(F32)
        o_ref[...] = acc.astype(out_dtype)

    out = pl.pallas_call(
        body, name=name, grid=(rows // tb,),
        out_shape=jax.ShapeDtypeStruct((rows, c), out_dtype),
        in_specs=[pl.BlockSpec((k, tb, c), lambda i: (0, i, 0))],
        out_specs=pl.BlockSpec((tb, c), lambda i: (i, 0)),
        compiler_params=_cparams(1),
    )(a3)
    return out.reshape(rest)


def _bands(t, causal):
    r = lax.broadcasted_iota(jnp.int32, (t, t + HALO), 0)
    col = lax.broadcasted_iota(jnp.int32, (t, t + HALO), 1)
    diff = r + HALO - col if causal else col - r
    return jnp.stack([jnp.where((diff >= 0) & (diff < w), 1.0, 0.0) for w in POOL_WINDOWS]).astype(BF16)


def _split_dot(band, v):
    hi = v.astype(BF16)
    lo = (v - hi.astype(F32)).astype(BF16)
    return _nn(band, hi) + _nn(band, lo)


def _mix_fwd(x, g, wp, b, sc, name, deps=()):
    s = x.shape[0]
    t = min(MIX_ROWS, s)
    rb = t // HALO

    def body(x_ref, xh_ref, g_ref, wp_ref, b_ref, sc_ref, band_ref, xo_ref, d_ref):
        i = pl.program_id(0)
        gg = g_ref[...]
        h, _ = _rms(x_ref[...], gg, D_MODEL)
        hh, _ = _rms(xh_ref[...], gg, D_MODEL)
        hh = jnp.where(i > 0, hh, 0.0)
        hext = jnp.concatenate([hh, h], axis=0)
        tok = i * t + lax.broadcasted_iota(jnp.int32, (t, 1), 0)
        for gi, w in enumerate(POOL_WINDOWS):
            sl = slice(gi * GROUP_DIM, (gi + 1) * GROUP_DIM)
            win = _split_dot(band_ref[gi], hext[:, sl])
            inv = 1.0 / jnp.minimum(tok + 1, w).astype(F32)
            dbf = (win * inv - h[:, sl]).astype(BF16)
            d_ref[:, sl] = dbf
            ypre = _nn(dbf, wp_ref[gi]) + b_ref[:, sl]
            xo_ref[:, sl] = x_ref[:, sl] + ypre * sc_ref[:, sl]

    return _pcall(
        body, (x, x, g, wp, b, sc, _bands(t, True)), deps, name=name, grid=(s // t,),
        out_shape=[jax.ShapeDtypeStruct((s, D_MODEL), F32), jax.ShapeDtypeStruct((s, D_MODEL), BF16)],
        in_specs=[_rows(t, D_MODEL),
                  pl.BlockSpec((HALO, D_MODEL), lambda i: (jnp.maximum(i * rb - 1, 0), 0)),
                  _full((1, D_MODEL)), _full((4, GROUP_DIM, GROUP_DIM)), _full((1, D_MODEL)), _full((1, D_MODEL)),
                  _full((4, t, t + HALO))],
        out_specs=[_rows(t, D_MODEL), _rows(t, D_MODEL)],
        compiler_params=_cparams(1, VMEM_MID),
    )


def _mix_bwd(x, dy, d, g, wp, b, sc, name, deps=()):
    s = x.shape[0]
    t = min(MIX_ROWS, s)
    rb = t // HALO
    nb = s // t
    last_halo = s // HALO - 1

    def body(x_ref, dy_ref, dyn_ref, d_ref, g_ref, wp_ref, b_ref, sc_ref, band_ref,
             dx_ref, dyp_ref, dsc_ref, db_ref, dln_ref):
        i = pl.program_id(0)
        x = x_ref[...]
        gg = g_ref[...]
        dy = dy_ref[...]
        sc = sc_ref[...]
        dyp32 = dy * sc
        dyp = dyp32.astype(BF16)
        dyph = (dyn_ref[...] * sc).astype(BF16)
        dyp_ref[...] = dyp
        tok = i * t + lax.broadcasted_iota(jnp.int32, (t + HALO, 1), 0)
        dh, dsc = [], []
        for gi, w in enumerate(POOL_WINDOWS):
            sl = slice(gi * GROUP_DIM, (gi + 1) * GROUP_DIM)
            ypre = _nn(d_ref[:, sl], wp_ref[gi]) + b_ref[:, sl]
            dsc.append(jnp.sum(dy[:, sl] * ypre, axis=0, keepdims=True))
            dd = _nt(dyp[:, sl], wp_ref[gi])
            ddh = jnp.where(i < nb - 1, _nt(dyph[:, sl], wp_ref[gi]), 0.0)
            inv = 1.0 / jnp.minimum(tok + 1, w).astype(F32)
            ddext = jnp.concatenate([dd, ddh], axis=0) * inv
            dh.append(_split_dot(band_ref[gi], ddext) - dd)
        dh = jnp.concatenate(dh, axis=1)
        _, r = _rms(x, gg, D_MODEL)
        dxn, dg = _rms_bwd(x, r, gg, dh, D_MODEL)
        dx_ref[...] = dy + dxn

        @pl.when(i == 0)
        def _():
            dsc_ref[...] = jnp.zeros_like(dsc_ref)
            db_ref[...] = jnp.zeros_like(db_ref)
            dln_ref[...] = jnp.zeros_like(dln_ref)

        dsc_ref[...] += jnp.concatenate(dsc, axis=1)
        db_ref[...] += jnp.sum(dyp32, axis=0, keepdims=True)
        dln_ref[...] += dg

    vec = jax.ShapeDtypeStruct((1, D_MODEL), F32)
    return _pcall(
        body, (x, dy, dy, d, g, wp, b, sc, _bands(t, False)), deps, name=name, grid=(nb,),
        out_shape=[jax.ShapeDtypeStruct((s, D_MODEL), F32), jax.ShapeDtypeStruct((s, D_MODEL), BF16), vec, vec, vec],
        in_specs=[_rows(t, D_MODEL), _rows(t, D_MODEL),
                  pl.BlockSpec((HALO, D_MODEL), lambda i: (jnp.minimum((i + 1) * rb, last_halo), 0)),
                  _rows(t, D_MODEL),
                  _full((1, D_MODEL)), _full((4, GROUP_DIM, GROUP_DIM)), _full((1, D_MODEL)), _full((1, D_MODEL)),
                  _full((4, t, t + HALO))],
        out_specs=[_rows(t, D_MODEL), _rows(t, D_MODEL), _full((1, D_MODEL)), _full((1, D_MODEL)), _full((1, D_MODEL))],
        compiler_params=_cparams(1, VMEM_MID),
    )


def _load_weights(w_hbm, w_vmem, sem):
    @pl.when(pl.program_id(0) == 0)
    def _():
        cp = pltpu.make_async_copy(w_hbm, w_vmem, sem)
        cp.start()
        cp.wait()


def _ffn_fwd(x, g, w, name, attn=None):
    s = x.shape[0]
    t = min(512, s)
    fused = attn is not None

    def body(*refs):
        if fused:
            x_ref, o_ref, wo_ref, g_ref, w_hbm, mid_ref, xo_ref, gate_ref, up_ref, w_ref, sem = refs
        else:
            x_ref, g_ref, w_hbm, xo_ref, gate_ref, up_ref, w_ref, sem = refs
        _load_weights(w_hbm, w_ref, sem)
        x = x_ref[...]
        if fused:
            x = x + _nn(o_ref[...], wo_ref[...])
            mid_ref[...] = x
        hn = _rms(x, g_ref[...], D_MODEL)[0].astype(BF16)
        acc = x
        for c in range(2):
            rs = slice(c * FF_HALF, (c + 1) * FF_HALF)
            gt = _nt(hn, w_ref[0, rs, :])
            up = _nt(hn, w_ref[1, rs, :])
            gate_ref[:, rs] = gt.astype(BF16)
            up_ref[:, rs] = up.astype(BF16)
            act = ((gt * _sigmoid(gt)) * up).astype(BF16)
            acc = acc + _nn(act, w_ref[2, rs, :])
        xo_ref[...] = acc

    hid = jax.ShapeDtypeStruct((s, D_FF), BF16)
    tok = jax.ShapeDtypeStruct((s, D_MODEL), F32)
    extra_in = [_rows(t, D_MODEL), _full((D_MODEL, D_MODEL))] if fused else []
    extra_args = list(attn) if fused else []
    return pl.pallas_call(
        body, name=name, grid=(s // t,),
        out_shape=([tok] if fused else []) + [tok, hid, hid],
        in_specs=[_rows(t, D_MODEL)] + extra_in + [_full((1, D_MODEL)), ANY],
        out_specs=([_rows(t, D_MODEL)] if fused else []) + [_rows(t, D_MODEL), _rows(t, D_FF), _rows(t, D_FF)],
        scratch_shapes=[pltpu.VMEM((3, D_FF, D_MODEL), BF16), pltpu.SemaphoreType.DMA],
        compiler_params=_cparams(1, VMEM_BIG),
    )(x, *extra_args, g, w)


def _ffn_bwd(x, dy, gate, up, g, w, name, deps=()):
    s = x.shape[0]
    t = min(256, s)

    def body(x_ref, dy_ref, gate_ref, up_ref, g_ref, w_hbm,
             dx_ref, act_ref, dg_ref, du_ref, hn_ref, dyb_ref, dln_ref, w_ref, sem):
        _load_weights(w_hbm, w_ref, sem)
        x = x_ref[...]
        gg = g_ref[...]
        y, r = _rms(x, gg, D_MODEL)
        hn = y.astype(BF16)
        hn_ref[...] = hn
        dy = dy_ref[...]
        dyb = dy.astype(BF16)
        dyb_ref[...] = dyb
        dh = jnp.zeros((t, D_MODEL), F32)
        for c in range(2):
            rs = slice(c * FF_HALF, (c + 1) * FF_HALF)
            gt = gate_ref[:, rs].astype(F32)
            u = up_ref[:, rs].astype(F32)
            sg = _sigmoid(gt)
            sl = gt * sg
            act_ref[:, rs] = (sl * u).astype(BF16)
            dact = _nt(dyb, w_ref[2, rs, :])
            dg = (dact * u * (sg * (1.0 + gt * (1.0 - sg)))).astype(BF16)
            du = (dact * sl).astype(BF16)
            dg_ref[:, rs] = dg
            du_ref[:, rs] = du
            dh = dh + _nn(dg, w_ref[0, rs, :]) + _nn(du, w_ref[1, rs, :])
        dxn, dgl = _rms_bwd(x, r, gg, dh, D_MODEL)
        dx_ref[...] = dy + dxn

        @pl.when(pl.program_id(0) == 0)
        def _():
            dln_ref[...] = jnp.zeros_like(dln_ref)

        dln_ref[...] += dgl

    hid = jax.ShapeDtypeStruct((s, D_FF), BF16)
    tok = jax.ShapeDtypeStruct((s, D_MODEL), BF16)
    return _pcall(
        body, (x, dy, gate, up, g, w), deps, name=name, grid=(s // t,),
        out_shape=[jax.ShapeDtypeStruct((s, D_MODEL), F32), hid, hid, hid, tok, tok,
                   jax.ShapeDtypeStruct((1, D_MODEL), F32)],
        in_specs=[_rows(t, D_MODEL), _rows(t, D_MODEL), _rows(t, D_FF), _rows(t, D_FF), _full((1, D_MODEL)), ANY],
        out_specs=[_rows(t, D_MODEL), _rows(t, D_FF), _rows(t, D_FF), _rows(t, D_FF),
                   _rows(t, D_MODEL), _rows(t, D_MODEL), _full((1, D_MODEL))],
        scratch_shapes=[pltpu.VMEM((3, D_FF, D_MODEL), BF16), pltpu.SemaphoreType.DMA],
        compiler_params=_cparams(1, VMEM_BIG),
    )


def _tn_matmul(a, b, into, p0, name, groups=1, m_chunk=None, deps=()):
    s = a.shape[0]
    m, n = a.shape[1] // groups, b.shape[1] // groups
    assert into.shape[1:] == (m, n)
    mc = m if m_chunk is None else m_chunk
    nm = m // mc
    t = min(1024, s)
    nt = s // t

    def body(a_ref, b_ref, into_ref, o_ref, acc):
        ti = pl.program_id(2)

        @pl.when(ti == 0)
        def _():
            acc[...] = jnp.zeros_like(acc)

        acc[...] += _tn(a_ref[...], b_ref[...])

        @pl.when(ti == nt - 1)
        def _():
            o_ref[...] = acc[...].astype(o_ref.dtype)

    return _pcall(
        body, (a, b, into), deps, name=name, grid=(groups, nm, nt),
        out_shape=jax.ShapeDtypeStruct(into.shape, into.dtype),
        in_specs=[pl.BlockSpec((t, mc), lambda gi, mi, ti: (ti, gi * nm + mi)),
                  pl.BlockSpec((t, n), lambda gi, mi, ti: (ti, gi)), ANY],
        out_specs=pl.BlockSpec((None, mc, n), lambda gi, mi, ti: (p0 + gi, mi, 0)),
        scratch_shapes=[pltpu.VMEM((mc, n), F32)],
        input_output_aliases={2: 0},
        compiler_params=_cparams(3, VMEM_BIG),
    )


def _rope_tables(positions):
    half = ROPE // 2
    inv = ROPE_THETA ** (-jnp.arange(half, dtype=F32) * 2.0 / ROPE)
    ang = positions.astype(F32)[:, None] * inv
    cos, sin = jnp.cos(ang), jnp.sin(ang)
    zero = jnp.zeros((positions.shape[0], LANES - ROPE), F32)
    return jnp.concatenate([cos, cos, zero], axis=1), jnp.concatenate([-sin, sin, zero], axis=1)


def _kv_specs(t):
    return [_full((1, D_MODEL)), _full((D_MODEL, KV_RANK)), _full((D_MODEL, LANES)), _full((1, KV_RANK)),
            _full((KV_RANK, N_HEADS * NOPE)), _full((KV_RANK, N_HEADS * V_DIM)),
            _full((1, NOPE)), _full((1, LANES)), _rows(t, LANES), _rows(t, LANES)]


def _kv_fwd(x, ln, wc, wpe, gl, wuk, wuv, gkn, gkr, cos, sin, name, deps=()):
    s = x.shape[0]
    t = min(PROJ_ROWS, s)

    def body(x_ref, ln_ref, wc_ref, wpe_ref, gl_ref, wuk_ref, wuv_ref, gkn_ref, gkr_ref, cos_ref, sin_ref,
             k_ref, v_ref):
        hn = _rms(x_ref[...], ln_ref[...], D_MODEL)[0].astype(BF16)
        clat = _nn(hn, wc_ref[...])
        kpe = _nn(hn, wpe_ref[...])
        cn = _rms(clat, gl_ref[...], KV_RANK)[0].astype(BF16)
        sspe = jnp.sum(kpe * kpe, axis=-1, keepdims=True)
        base = kpe * gkr_ref[...]
        rot = base * cos_ref[...] + _swap_halves(base, _swap_perm()) * sin_ref[...]
        kn_all = _nn(cn, wuk_ref[...])
        v_ref[...] = _nn(cn, wuv_ref[...]).astype(BF16)
        for h in range(N_HEADS):
            kn = kn_all[:, h * NOPE:(h + 1) * NOPE]
            r = lax.rsqrt((jnp.sum(kn * kn, axis=-1, keepdims=True) + sspe) * (1.0 / QK_DIM) + EPS)
            k_ref[:, h * QK_PAD:h * QK_PAD + NOPE] = ((kn * r) * gkn_ref[...]).astype(BF16)
            k_ref[:, h * QK_PAD + NOPE:(h + 1) * QK_PAD] = (rot * r).astype(BF16)

    return _pcall(
        body, (x, ln, wc, wpe, gl, wuk, wuv, gkn, gkr, cos, sin), deps, name=name, grid=(s // t,),
        out_shape=[jax.ShapeDtypeStruct((s, N_HEADS * QK_PAD), BF16), jax.ShapeDtypeStruct((s, N_HEADS * V_DIM), BF16)],
        in_specs=[_rows(t, D_MODEL)] + _kv_specs(t),
        out_specs=[_rows(t, N_HEADS * QK_PAD), _rows(t, N_HEADS * V_DIM)],
        compiler_params=_cparams(1, VMEM_MID),
    )


def _kv_bwd(x, dxin, dks, dvs, ln, wc, wpe, gl, wuk, wuv, gkn, gkr, cos, sin, name):
    s = x.shape[0]
    t = min(PROJ_ROWS, s)
    nk = len(dks)

    def body(*refs):
        x_ref, dxin_ref = refs[:2]
        dk_refs = refs[2:2 + nk]
        dv_refs = refs[2 + nk:2 + 2 * nk]
        (ln_ref, wc_ref, wpe_ref, gl_ref, wuk_ref, wuv_ref, gkn_ref, gkr_ref, cos_ref, sin_ref,
         dx_ref, hn_ref, cn_ref, dkn_ref, dvb_ref, dcc_ref, dpe_ref,
         dln_ref, dgl_ref, dgkn_ref, dgkr_ref) = refs[2 + 2 * nk:]
        x = x_ref[...]
        ln = ln_ref[...]
        y, rx = _rms(x, ln, D_MODEL)
        hn = y.astype(BF16)
        hn_ref[...] = hn
        clat = _nn(hn, wc_ref[...])
        kpe = _nn(hn, wpe_ref[...])
        gl = gl_ref[...]
        cy, rc = _rms(clat, gl, KV_RANK)
        cn = cy.astype(BF16)
        cn_ref[...] = cn
        sspe = jnp.sum(kpe * kpe, axis=-1, keepdims=True)
        cs, sn, perm = cos_ref[...], sin_ref[...], _swap_perm()
        gkn, gkr = gkn_ref[...], gkr_ref[...]
        base = kpe * gkr
        rot = base * cs + _swap_halves(base, perm) * sn
        dkr_sum = jnp.zeros((t, LANES), F32)
        coef_sum = jnp.zeros((t, 1), F32)
        dgkn = jnp.zeros((1, NOPE), F32)
        kn_all = _nn(cn, wuk_ref[...])
        dkn_heads = []
        for h in range(N_HEADS):
            kn = kn_all[:, h * NOPE:(h + 1) * NOPE]
            r = lax.rsqrt((jnp.sum(kn * kn, axis=-1, keepdims=True) + sspe) * (1.0 / QK_DIM) + EPS)
            lo, mid, hi = h * QK_PAD, h * QK_PAD + NOPE, (h + 1) * QK_PAD
            dko = dk_refs[0][:, lo:mid]
            dkr = dk_refs[0][:, mid:hi]
            for j in range(1, nk):
                dko = dko + dk_refs[j][:, lo:mid]
                dkr = dkr + dk_refs[j][:, mid:hi]
            un = dko * gkn
            sm = (jnp.sum(kn * un, axis=-1, keepdims=True) + jnp.sum(rot * dkr, axis=-1, keepdims=True)) * (1.0 / QK_DIM)
            coef = r * r * r * sm
            dkn = (r * un - kn * coef).astype(BF16)
            dkr_sum = dkr_sum + r * dkr
            coef_sum = coef_sum + coef
            dgkn = dgkn + jnp.sum(dko * (kn * r), axis=0, keepdims=True)
            dkn_heads.append(dkn)
        dkn_all = jnp.concatenate(dkn_heads, axis=1)
        dkn_ref[...] = dkn_all
        dv_all = dv_refs[0][...]
        for j in range(1, nk):
            dv_all = dv_all + dv_refs[j][...]
        dvb = dv_all.astype(BF16)
        dvb_ref[...] = dvb
        dc = _nt(dkn_all, wuk_ref[...]) + _nt(dvb, wuv_ref[...])
        dz = dkr_sum * cs - _swap_halves(dkr_sum, perm) * sn
        dkpe = dz * gkr - kpe * coef_sum
        dgkr = jnp.sum(dz * kpe, axis=0, keepdims=True)
        dclat, dgl = _rms_bwd(clat, rc, gl, dc, KV_RANK)
        dcc = dclat.astype(BF16)
        dpe = dkpe.astype(BF16)
        dcc_ref[...] = dcc
        dpe_ref[...] = dpe
        dhn = _nt(dcc, wc_ref[...]) + _nt(dpe, wpe_ref[...])
        dxn, dln = _rms_bwd(x, rx, ln, dhn, D_MODEL)
        dx_ref[...] = dxin_ref[...] + dxn

        @pl.when(pl.program_id(0) == 0)
        def _():
            dln_ref[...] = jnp.zeros_like(dln_ref)
            dgl_ref[...] = jnp.zeros_like(dgl_ref)
            dgkn_ref[...] = jnp.zeros_like(dgkn_ref)
            dgkr_ref[...] = jnp.zeros_like(dgkr_ref)

        dln_ref[...] += dln
        dgl_ref[...] += dgl
        dgkn_ref[...] += dgkn
        dgkr_ref[...] += dgkr

    def tok(cols, dt):
        return jax.ShapeDtypeStruct((s, cols), dt)

    def vec(cols):
        return jax.ShapeDtypeStruct((1, cols), F32)

    return pl.pallas_call(
        body, name=name, grid=(s // t,),
        out_shape=[tok(D_MODEL, F32), tok(D_MODEL, BF16), tok(KV_RANK, BF16), tok(N_HEADS * NOPE, BF16),
                   tok(N_HEADS * V_DIM, BF16), tok(KV_RANK, BF16), tok(LANES, BF16),
                   vec(D_MODEL), vec(KV_RANK), vec(NOPE), vec(LANES)],
        in_specs=[_rows(t, D_MODEL), _rows(t, D_MODEL)] + [_rows(t, N_HEADS * QK_PAD)] * nk
                 + [_rows(t, N_HEADS * V_DIM)] * nk + _kv_specs(t),
        out_specs=[_rows(t, D_MODEL), _rows(t, D_MODEL), _rows(t, KV_RANK), _rows(t, N_HEADS * NOPE),
                   _rows(t, N_HEADS * V_DIM), _rows(t, KV_RANK), _rows(t, LANES),
                   _full((1, D_MODEL)), _full((1, KV_RANK)), _full((1, NOPE)), _full((1, LANES))],
        compiler_params=_cparams(1, VMEM_BIG),
    )(x, dxin, *dks, *dvs, ln, wc, wpe, gl, wuk, wuv, gkn, gkr, cos, sin)


def _q_specs(t):
    return [_full((1, D_MODEL)), _full((D_MODEL, Q_RANK)), _full((1, Q_RANK)), _full((N_HEADS, Q_RANK, QK_PAD)),
            _full((1, NOPE)), _full((1, LANES)), _rows(t, LANES), _rows(t, LANES)]


def _q_fwd(x, ln, wdq, gql, wuq, gqn, gqr, cos, sin, name, deps=()):
    s = x.shape[0]
    t = min(PROJ_ROWS, s)

    def body(x_ref, ln_ref, wdq_ref, gql_ref, wuq_ref, gqn_ref, gqr_ref, cos_ref, sin_ref, q_ref):
        hn = _rms(x_ref[...], ln_ref[...], D_MODEL)[0].astype(BF16)
        cqn = _rms(_nn(hn, wdq_ref[...]), gql_ref[...], Q_RANK)[0].astype(BF16)
        cs, sn, perm = cos_ref[...], sin_ref[...], _swap_perm()
        for h in range(N_HEADS):
            qa = _nn(cqn, wuq_ref[h])
            r = lax.rsqrt(jnp.sum(qa * qa, axis=-1, keepdims=True) * (1.0 / QK_DIM) + EPS)
            q_ref[:, h * QK_PAD:h * QK_PAD + NOPE] = ((qa[:, :NOPE] * r) * gqn_ref[...]).astype(BF16)
            z = (qa[:, NOPE:] * r) * gqr_ref[...]
            q_ref[:, h * QK_PAD + NOPE:(h + 1) * QK_PAD] = (z * cs + _swap_halves(z, perm) * sn).astype(BF16)

    return _pcall(
        body, (x, ln, wdq, gql, wuq, gqn, gqr, cos, sin), deps, name=name, grid=(s // t,),
        out_shape=jax.ShapeDtypeStruct((s, N_HEADS * QK_PAD), BF16),
        in_specs=[_rows(t, D_MODEL)] + _q_specs(t),
        out_specs=_rows(t, N_HEADS * QK_PAD),
        compiler_params=_cparams(1, VMEM_MID),
    )


def _q_bwd(x, dxin, dq, ln, wdq, gql, wuq, gqn, gqr, cos, sin, name):
    s = x.shape[0]
    t = min(PROJ_ROWS, s)

    def body(x_ref, dxin_ref, dq_ref, ln_ref, wdq_ref, gql_ref, wuq_ref, gqn_ref, gqr_ref, cos_ref, sin_ref,
             dx_ref, hn_ref, cqn_ref, dqa_ref, dcq_ref, dln_ref, dgql_ref, dgqn_ref, dgqr_ref):
        x = x_ref[...]
        ln = ln_ref[...]
        y, rx = _rms(x, ln, D_MODEL)
        hn = y.astype(BF16)
        hn_ref[...] = hn
        cqp = _nn(hn, wdq_ref[...])
        gql = gql_ref[...]
        cy, rc = _rms(cqp, gql, Q_RANK)
        cqn = cy.astype(BF16)
        cqn_ref[...] = cqn
        cs, sn, perm = cos_ref[...], sin_ref[...], _swap_perm()
        gqn, gqr = gqn_ref[...], gqr_ref[...]
        dcq = jnp.zeros((t, Q_RANK), F32)
        dgqn = jnp.zeros((1, NOPE), F32)
        dgqr = jnp.zeros((1, LANES), F32)
        for h in range(N_HEADS):
            qa = _nn(cqn, wuq_ref[h])
            qn, qr = qa[:, :NOPE], qa[:, NOPE:]
            r = lax.rsqrt(jnp.sum(qa * qa, axis=-1, keepdims=True) * (1.0 / QK_DIM) + EPS)
            dqo = dq_ref[:, h * QK_PAD:h * QK_PAD + NOPE]
            dqr = dq_ref[:, h * QK_PAD + NOPE:(h + 1) * QK_PAD]
            dz = dqr * cs - _swap_halves(dqr, perm) * sn
            un = dqo * gqn
            ur = dz * gqr
            sm = (jnp.sum(qn * un, axis=-1, keepdims=True) + jnp.sum(qr * ur, axis=-1, keepdims=True)) * (1.0 / QK_DIM)
            coef = r * r * r * sm
            dqa = jnp.concatenate([r * un - qn * coef, r * ur - qr * coef], axis=1).astype(BF16)
            dgqn = dgqn + jnp.sum(dqo * (qn * r), axis=0, keepdims=True)
            dgqr = dgqr + jnp.sum(dz * (qr * r), axis=0, keepdims=True)
            dqa_ref[:, h * QK_PAD:(h + 1) * QK_PAD] = dqa
            dcq = dcq + _nt(dqa, wuq_ref[h])
        dcqp, dgql = _rms_bwd(cqp, rc, gql, dcq, Q_RANK)
        dcqb = dcqp.astype(BF16)
        dcq_ref[...] = dcqb
        dhn = _nt(dcqb, wdq_ref[...])
        dxn, dln = _rms_bwd(x, rx, ln, dhn, D_MODEL)
        dx_ref[...] = dxin_ref[...] + dxn

        @pl.when(pl.program_id(0) == 0)
        def _():
            dln_ref[...] = jnp.zeros_like(dln_ref)
            dgql_ref[...] = jnp.zeros_like(dgql_ref)
            dgqn_ref[...] = jnp.zeros_like(dgqn_ref)
            dgqr_ref[...] = jnp.zeros_like(dgqr_ref)

        dln_ref[...] += dln
        dgql_ref[...] += dgql
        dgqn_ref[...] += dgqn
        dgqr_ref[...] += dgqr

    def tok(cols, dt):
        return jax.ShapeDtypeStruct((s, cols), dt)

    def vec(cols):
        return jax.ShapeDtypeStruct((1, cols), F32)

    return pl.pallas_call(
        body, name=name, grid=(s // t,),
        out_shape=[tok(D_MODEL, F32), tok(D_MODEL, BF16), tok(Q_RANK, BF16), tok(N_HEADS * QK_PAD, BF16),
                   tok(Q_RANK, BF16), vec(D_MODEL), vec(Q_RANK), vec(NOPE), vec(LANES)],
        in_specs=[_rows(t, D_MODEL), _rows(t, D_MODEL), _rows(t, N_HEADS * QK_PAD)] + _q_specs(t),
        out_specs=[_rows(t, D_MODEL), _rows(t, D_MODEL), _rows(t, Q_RANK), _rows(t, N_HEADS * QK_PAD),
                   _rows(t, Q_RANK), _full((1, D_MODEL)), _full((1, Q_RANK)), _full((1, NOPE)), _full((1, LANES))],
        compiler_params=_cparams(1, VMEM_MID),
    )(x, dxin, dq, ln, wdq, gql, wuq, gqn, gqr, cos, sin)


SM_SCALE = 1.0 / math.sqrt(QK_DIM)
LOG2_E = math.log2(math.e)
EXP2_SCALE = SM_SCALE * LOG2_E
NEG = -1e30


def _diag_mask(t):
    qpos = lax.broadcasted_iota(jnp.int32, (t, t), 0)
    kpos = lax.broadcasted_iota(jnp.int32, (t, t), 1)
    return lax.shift_right_logical(kpos, 6) <= lax.shift_right_logical(qpos, 6)


def _att_fwd(q, k, v, name):
    s = q.shape[0]
    t = min(512, s)
    nb = s // t

    def body(q_ref, k_ref, v_ref, o_ref, lse_ref):
        qi = pl.program_id(1)
        qq = q_ref[...]

        def block(ki, carry, masked):
            m_old, l_old, acc = carry
            rows = pl.ds(pl.multiple_of(ki * t, t), t)
            sc = _nt(qq, k_ref[rows, :])
            if masked:
                sc = jnp.where(_diag_mask(t), sc, NEG)
            m_new = jnp.maximum(m_old, jnp.max(sc, axis=-1, keepdims=True))
            p = jnp.exp2((sc - m_new) * EXP2_SCALE)
            alpha = jnp.exp2((m_old - m_new) * EXP2_SCALE)
            l_new = alpha * l_old + jnp.sum(p, axis=-1, keepdims=True)
            acc = alpha * acc + _nn(p.astype(BF16), v_ref[rows, :])
            return m_new, l_new, acc

        init = (jnp.full((t, 1), NEG, F32), jnp.zeros((t, 1), F32), jnp.zeros((t, V_DIM), F32))
        def pair(k0, c):
            return block(k0 + 1, block(k0, c, False), False)

        carry = lax.fori_loop(0, qi // 4, lambda j, c: pair(4 * j + 2, pair(4 * j, c)), init)
        done = 4 * (qi // 4)
        tails = [lambda c: block(qi, c, True),
                 lambda c: block(qi, block(done, c, False), True),
                 lambda c: block(qi, pair(done, c), True),
                 lambda c: block(qi, block(done + 2, pair(done, c), False), True)]
        m_fin, l_fin, acc = lax.switch(qi & 3, tails, carry)
        o_ref[...] = (acc / l_fin).astype(BF16)
        lse_ref[...] = jnp.broadcast_to(m_fin * SM_SCALE + jnp.log(l_fin), (t, LANES))

    return pl.pallas_call(
        body, name=name, grid=(N_HEADS, nb),
        out_shape=[jax.ShapeDtypeStruct((s, N_HEADS * V_DIM), BF16), jax.ShapeDtypeStruct((s, N_HEADS * LANES), F32)],
        in_specs=[pl.BlockSpec((t, QK_PAD), lambda h, qi: (qi, h)),
                  pl.BlockSpec((s, QK_PAD), lambda h, qi: (0, h)),
                  pl.BlockSpec((s, V_DIM), lambda h, qi: (0, h))],
        out_specs=[pl.BlockSpec((t, V_DIM), lambda h, qi: (qi, h)),
                   pl.BlockSpec((t, LANES), lambda h, qi: (qi, h))],
        compiler_params=_cparams(2, VMEM_MID),
    )(q, k, v)


def _att_bwd(q, k, v, do, stats, name, deps=()):
    s = q.shape[0]
    t = min(512, s)
    nb = s // t

    def body(q_ref, k_ref, v_ref, do_ref, st_ref, dq_ref, dk_ref, dv_ref):
        ki = pl.program_id(1)
        kk, vv = k_ref[...], v_ref[...]

        @pl.when(ki == 0)
        def _():
            dq_ref[...] = jnp.zeros_like(dq_ref)

        def block(qi, carry, masked):
            dk, dv = carry
            rows = pl.ds(pl.multiple_of(qi * t, t), t)
            qq, dob = q_ref[rows, :], do_ref[rows, :]
            sc = _nt(qq, kk)
            if masked:
                sc = jnp.where(_diag_mask(t), sc, NEG)
            st = st_ref[rows, :]
            p = jnp.exp2(sc * EXP2_SCALE - st[:, 0:1])
            dp = _nt(dob, vv)
            ds = (p * (dp - st[:, 1:2])).astype(BF16)
            dq_ref[rows, :] += _nn(ds, kk)
            return dk + _tn(ds, qq), dv + _tn(p.astype(BF16), dob)

        rest = nb - 1 - ki
        zeros = (jnp.zeros((t, QK_PAD), F32), jnp.zeros((t, V_DIM), F32))
        first = ki + 1

        def pair(q0, c):
            return block(q0 + 1, block(q0, c, False), False)

        heads = [lambda c: block(ki, c, True),
                 lambda c: block(first, block(ki, c, True), False),
                 lambda c: pair(first, block(ki, c, True)),
                 lambda c: block(first + 2, pair(first, block(ki, c, True)), False)]
        carry = lax.switch(rest & 3, heads, zeros)
        start = first + (rest & 3)
        dk, dv = lax.fori_loop(0, rest // 4, lambda j, c: pair(start + 4 * j + 2, pair(start + 4 * j, c)), carry)
        dk_ref[...] = dk * SM_SCALE
        dv_ref[...] = dv

        @pl.when(ki == nb - 1)
        def _():
            dq_ref[...] = dq_ref[...] * SM_SCALE

    def head(h, ki):
        return (0, h)

    def kblock(h, ki):
        return (ki, h)

    return _pcall(
        body, (q, k, v, do, stats), deps, name=name, grid=(N_HEADS, nb),
        out_shape=[jax.ShapeDtypeStruct((s, N_HEADS * QK_PAD), F32), jax.ShapeDtypeStruct((s, N_HEADS * QK_PAD), F32),
                   jax.ShapeDtypeStruct((s, N_HEADS * V_DIM), F32)],
        in_specs=[pl.BlockSpec((s, QK_PAD), head), pl.BlockSpec((t, QK_PAD), kblock), pl.BlockSpec((t, V_DIM), kblock),
                  pl.BlockSpec((s, V_DIM), head), pl.BlockSpec((s, LANES), head)],
        out_specs=[pl.BlockSpec((s, QK_PAD), head), pl.BlockSpec((t, QK_PAD), kblock), pl.BlockSpec((t, V_DIM), kblock)],
        compiler_params=_cparams(2, VMEM_MID),
    )


def _o_bwd(dx, wo, o, lse, name, deps=()):
    s = dx.shape[0]
    t = min(512, s)

    def body(dx_ref, wo_ref, o_ref, lse_ref, do_ref, dxb_ref, st_ref):
        dxb = dx_ref[...].astype(BF16)
        dxb_ref[...] = dxb
        dob = _nt(dxb, wo_ref[...]).astype(BF16)
        do_ref[...] = dob
        lane = lax.broadcasted_iota(jnp.int32, (t, LANES), 1)
        for h in range(N_HEADS):
            sl = slice(h * V_DIM, (h + 1) * V_DIM)
            dsum = jnp.sum(dob[:, sl].astype(F32) * o_ref[:, sl].astype(F32), axis=-1, keepdims=True)
            st_ref[:, sl] = jnp.where(lane == 0, lse_ref[:, sl] * LOG2_E, jnp.where(lane == 1, dsum, 0.0))

    tok = jax.ShapeDtypeStruct((s, D_MODEL), BF16)
    return _pcall(
        body, (dx, wo, o, lse), deps, name=name, grid=(s // t,),
        out_shape=[tok, tok, jax.ShapeDtypeStruct((s, N_HEADS * LANES), F32)],
        in_specs=[_rows(t, D_MODEL), _full((D_MODEL, D_MODEL)), _rows(t, D_MODEL), _rows(t, N_HEADS * LANES)],
        out_specs=[_rows(t, D_MODEL), _rows(t, D_MODEL), _rows(t, N_HEADS * LANES)],
        compiler_params=_cparams(1, VMEM_MID),
    )


def _loss_head(y, target, name):
    s = y.shape[0]
    t = min(512, s)

    def body(y_ref, t_ref, dy_ref, sq_ref):
        e = y_ref[...] - t_ref[...]
        dy_ref[...] = e * (1.0 / D_MODEL)

        @pl.when(pl.program_id(0) == 0)
        def _():
            sq_ref[...] = jnp.zeros_like(sq_ref)

        sq_ref[...] += jnp.sum(e * e, axis=0, keepdims=True)

    return pl.pallas_call(
        body, name=name, grid=(s // t,),
        out_shape=[jax.ShapeDtypeStruct((s, D_MODEL), F32), jax.ShapeDtypeStruct((1, D_MODEL), F32)],
        in_specs=[_rows(t, D_MODEL), _rows(t, D_MODEL)],
        out_specs=[_rows(t, D_MODEL), _full((1, D_MODEL))],
        compiler_params=_cparams(1),
    )(y, target)


def _adamw(w, g, m, v, name):
    shape = w.shape
    c = shape[-1]
    r = math.prod(shape[:-1])
    tb = r
    for cand in (512, 256, 128):
        if r % cand == 0 and r > cand:
            tb = cand
            break

    def body(w_ref, g_ref, m_ref, v_ref, d_ref, mo_ref, vo_ref):
        gr = g_ref[...]
        mn = ADAM_B1 * m_ref[...] + (1.0 - ADAM_B1) * gr
        vn = ADAM_B2 * v_ref[...] + (1.0 - ADAM_B2) * (gr * gr)
        m_hat = mn / (1.0 - ADAM_B1 ** ADAM_STEP)
        v_hat = vn / (1.0 - ADAM_B2 ** ADAM_STEP)
        d_ref[...] = -ADAM_LR * (m_hat / (jnp.sqrt(v_hat) + ADAM_EPS) + ADAM_WD * w_ref[...])
        mo_ref[...] = mn
        vo_ref[...] = vn

    spec = pl.BlockSpec((tb, c), lambda i: (i, 0))
    flat = jax.ShapeDtypeStruct((r, c), F32)
    outs = pl.pallas_call(
        body, name=name, grid=(r // tb,),
        out_shape=[flat, flat, flat],
        in_specs=[spec] * 4, out_specs=[spec] * 3,
        compiler_params=_cparams(1),
    )(w.reshape(r, c), g.reshape(r, c), m.reshape(r, c), v.reshape(r, c))
    return [a.reshape(shape) for a in outs]


def _pad_cols(a, width):
    return jnp.pad(a, [(0, 0)] * (a.ndim - 1) + [(0, width - a.shape[-1])])


def _owner_view(a, sz):
    return a.reshape(a.shape[0], N_CHIPS, 2, sz, a.shape[-1])


def kernel(x, positions, ln_mix_a, w_pool, b_pool, pool_scale, ln_ffn, w_gate, w_up, w_down, ln_kv, w_dkv, g_kv_latent, w_uk, w_uv, g_k, ln_mix_b, w_dq, g_q_latent, w_uq, g_q, w_o, loss_target, m_ln_mix_a, m_w_pool, m_b_pool, m_pool_scale, m_ln_ffn, m_w_gate, m_w_up, m_w_down, m_ln_kv, m_w_dkv, m_g_kv_latent, m_w_uk, m_w_uv, m_g_k, m_ln_mix_b, m_w_dq, m_g_q_latent, m_w_uq, m_g_q, m_w_o, v_ln_mix_a, v_w_pool, v_b_pool, v_pool_scale, v_ln_ffn, v_w_gate, v_w_up, v_w_down, v_ln_kv, v_w_dkv, v_g_kv_latent, v_w_uk, v_w_uv, v_g_k, v_ln_mix_b, v_w_dq, v_g_q_latent, v_w_uq, v_g_q, v_w_o):
    weights = dict(ln_mix_a=ln_mix_a, w_pool=w_pool, b_pool=b_pool, pool_scale=pool_scale, ln_ffn=ln_ffn,
                   w_gate=w_gate, w_up=w_up, w_down=w_down, ln_kv=ln_kv, w_dkv=w_dkv, g_kv_latent=g_kv_latent,
                   w_uk=w_uk, w_uv=w_uv, g_k=g_k, ln_mix_b=ln_mix_b, w_dq=w_dq, g_q_latent=g_q_latent,
                   w_uq=w_uq, g_q=g_q, w_o=w_o)
    mom1 = dict(ln_mix_a=m_ln_mix_a, w_pool=m_w_pool, b_pool=m_b_pool, pool_scale=m_pool_scale, ln_ffn=m_ln_ffn,
                w_gate=m_w_gate, w_up=m_w_up, w_down=m_w_down, ln_kv=m_ln_kv, w_dkv=m_w_dkv,
                g_kv_latent=m_g_kv_latent, w_uk=m_w_uk, w_uv=m_w_uv, g_k=m_g_k, ln_mix_b=m_ln_mix_b, w_dq=m_w_dq,
                g_q_latent=m_g_q_latent, w_uq=m_w_uq, g_q=m_g_q, w_o=m_w_o)
    mom2 = dict(ln_mix_a=v_ln_mix_a, w_pool=v_w_pool, b_pool=v_b_pool, pool_scale=v_pool_scale, ln_ffn=v_ln_ffn,
                w_gate=v_w_gate, w_up=v_w_up, w_down=v_w_down, ln_kv=v_ln_kv, w_dkv=v_w_dkv,
                g_kv_latent=v_g_kv_latent, w_uk=v_w_uk, w_uv=v_w_uv, g_k=v_g_k, ln_mix_b=v_ln_mix_b, w_dq=v_w_dq,
                g_q_latent=v_g_q_latent, w_uq=v_w_uq, g_q=v_g_q, w_o=v_w_o)
    names = list(weights)
    dev = 4 * lax.axis_index("x") + 2 * lax.axis_index("y") + lax.axis_index("c")
    core = lax.axis_index("c").astype(jnp.int32).reshape(1)
    chip = (2 * lax.axis_index("x") + lax.axis_index("y")).astype(jnp.int32).reshape(1)

    xs = x[0]
    target = loss_target[0]
    cos, sin = _rope_tables(positions[0])

    def placed(shard):
        buf = lax.empty((shard.shape[0], N_DEV) + shard.shape[1:], shard.dtype)
        return lax.dynamic_update_slice(buf, shard[:, None], (0, dev, 0, 0))

    def ffn_shard(l):
        return jnp.stack([w_gate[l].T, w_up[l].T, w_down[l]]).astype(BF16)

    groups = {"ffn0": [placed(ffn_shard(0))]}
    small_sh = jnp.concatenate([ln_mix_a.reshape(1, -1), pool_scale.reshape(1, -1), b_pool.reshape(1, -1)], axis=1)
    wp_g, small_g = _all_gather([w_pool.astype(BF16), small_sh], [2, 0], "gather_first")
    wp_all = wp_g.reshape(2, 4, GROUP_DIM, GROUP_DIM)
    small_g = small_g.reshape(N_DEV, 3, 2, LANES)
    ln_a_all = small_g[:, 0].transpose(1, 0, 2).reshape(2, 1, D_MODEL)
    sc_all = small_g[:, 1].transpose(1, 0, 2).reshape(2, 1, D_MODEL)
    bp_all = small_g[:, 2].reshape(N_DEV, 2, 4, 32).transpose(1, 2, 0, 3).reshape(2, 1, D_MODEL)
    sp0 = _copies_start(groups["ffn0"], 1, _gather_spread, "spread_ffn0", deps=[small_g])
    zero = sp0[3][0, 0].astype(BF16)
    for l in (1, 2, 3):
        groups[f"ffn{l}"] = [placed(ffn_shard(l) + zero)]
    groups["att"] = [placed(a.astype(BF16) + zero) for a in (
        w_dkv[None, :, :KV_RANK], _pad_cols(w_dkv[None, :, KV_RANK:], LANES), w_uk[None], w_uv[None],
        w_dq, _pad_cols(w_uq, QK_PAD), w_o)]

    def spread_start(nm, deps):
        return _copies_start(groups[nm], len(groups[nm]), _gather_spread, f"spread_{nm}", deps=deps)

    def spread_wait(nm, state, after):
        ssem, rsem, bufs, _ = state
        return _copies_wait(bufs, ssem, rsem, after, _blocks_moved(4), f"spread_done_{nm}")

    def relay_start(nm, bufs, deps=()):
        return _copies_start(bufs, len(bufs), _gather_relay, f"relay_{nm}", deps=deps)

    def relay_wait(nm, state, after):
        ssem, rsem, bufs, _ = state
        return _copies_wait(bufs, ssem, rsem, after, _blocks_moved(3), f"relay_done_{nm}")

    gkn = g_k[:NOPE].reshape(1, NOPE)
    gkr = _pad_cols(g_k[NOPE:].reshape(1, ROPE), LANES)
    gl = g_kv_latent.reshape(1, KV_RANK)
    lnkv = ln_kv.reshape(1, D_MODEL)

    x_in, x_mid, pooled, gates, ups, w_ffn = [], [], [], [], [], []
    qs, outs, lses = [], [], []

    def mixer(l, cur, deps):
        x_in.append(cur)
        mid, dsave = _mix_fwd(cur, ln_a_all[l], wp_all[l], bp_all[l], sc_all[l], f"mix_fwd{l}", deps=deps)
        pooled.append(dsave)
        x_mid.append(mid)
        return mid

    def q_args(j):
        return (ln_mix_b[j].reshape(1, -1), wdq_all[j], g_q_latent[j].reshape(1, -1), wuq_all[j],
                g_q[j, :NOPE].reshape(1, -1), _pad_cols(g_q[j, NOPE:].reshape(1, -1), LANES), cos, sin)

    def attention(j, cur, deps):
        x_in.append(cur)
        q = _q_fwd(cur, *q_args(j), f"q_fwd{j}", deps=deps)
        o, lse = _att_fwd(q, k_sh, v_sh, f"att_fwd{j}")
        qs.append(q)
        outs.append(o)
        lses.append(lse)
        return o

    def ffn(l, mid, relayed, attn=None):
        w_l = relayed[0].reshape(3, D_FF, D_MODEL)
        w_ffn.append(w_l)
        if attn is None:
            cur, gate, up = _ffn_fwd(mid, ln_ffn[l].reshape(1, -1), w_l, f"ffn_fwd{l}")
        else:
            mid, cur, gate, up = _ffn_fwd(mid, ln_ffn[l].reshape(1, -1), w_l, f"ffn_fwd{l}", attn=attn)
            x_mid.append(mid)
        gates.append(gate)
        ups.append(up)
        return cur

    mid = mixer(0, xs, [sp0[3]])
    prepared = [buf for nm in ("ffn1", "att", "ffn2", "ffn3") for buf in groups[nm]]
    landed0 = spread_wait("ffn0", sp0, [mid] + prepared)
    sp1 = spread_start("ffn1", [landed0[0]])
    rl0 = relay_start("ffn0", landed0, [sp1[3]])
    cur = ffn(0, mid, relay_wait("ffn0", rl0, rl0[3]))

    landed1 = spread_wait("ffn1", sp1, cur)
    sp_att = spread_start("att", [landed1[0]])
    sp2 = spread_start("ffn2", [landed1[0]])
    rl1 = relay_start("ffn1", landed1, [sp_att[3], sp2[3]])
    mid = mixer(1, cur, [rl1[3]])
    cur = ffn(1, mid, relay_wait("ffn1", rl1, mid))
    x_kv = cur

    landed_att = spread_wait("att", sp_att, cur)
    landed2 = spread_wait("ffn2", sp2, cur)
    sp3 = spread_start("ffn3", [landed2[0]])
    rl_att = relay_start("att", landed_att, [sp3[3]])
    rl2 = relay_start("ffn2", landed2, [sp3[3]])
    att_bufs = relay_wait("att", rl_att, rl2[3])
    wc = att_bufs[0].reshape(D_MODEL, KV_RANK)
    wpe = att_bufs[1].reshape(D_MODEL, LANES)
    wuk_g = att_bufs[2].reshape(N_HEADS, KV_RANK, NOPE).transpose(1, 0, 2).reshape(KV_RANK, N_HEADS * NOPE)
    wuv_g = att_bufs[3].reshape(N_HEADS, KV_RANK, V_DIM).transpose(1, 0, 2).reshape(KV_RANK, N_HEADS * V_DIM)
    wdq_all = att_bufs[4].reshape(2, D_MODEL, Q_RANK)
    wuq_all = att_bufs[5]
    wo_all = att_bufs[6].reshape(2, D_MODEL, D_MODEL)
    k_sh, v_sh = _kv_fwd(cur, lnkv, wc, wpe, gl, wuk_g, wuv_g, gkn, gkr, cos, sin, "kv_fwd")
    o = attention(0, cur, [])
    cur = ffn(2, cur, relay_wait("ffn2", rl2, o), attn=(o, wo_all[0]))

    landed3 = spread_wait("ffn3", sp3, cur)
    rl3 = relay_start("ffn3", landed3)
    o = attention(1, cur, [rl3[3]])
    cur = ffn(3, cur, relay_wait("ffn3", rl3, o), attn=(o, wo_all[1]))

    dx, sq_cols = _loss_head(cur, target, "loss_head")

    small = {}
    sizes = dict(ffn0=FF_SHARD, ffn1=FF_SHARD, ffn2=FF_SHARD, ffn3=FF_SHARD, wo=128, kv512=128, dkv_pe=128,
                 wdq=128, wuqT=QK_PAD, wpool=32)
    big = dict(wo=lax.empty((2, D_MODEL, D_MODEL), BF16), kv512=lax.empty((3, D_MODEL, KV_RANK), BF16),
               dkv_pe=lax.empty((1, D_MODEL, LANES), BF16), wdq=lax.empty((2, D_MODEL, Q_RANK), BF16),
               wuqT=lax.empty((2, N_HEADS * QK_PAD, Q_RANK), BF16), wpool=lax.empty((8, GROUP_DIM, GROUP_DIM), BF16))
    for l in range(4):
        big[f"ffn{l}"] = lax.empty((3, D_FF, D_MODEL), BF16)
    red = {}

    def pair_start(nms, tag):
        arrs = []
        for nm in nms:
            view = _owner_view(big[nm], sizes[nm])
            arrs += [view, lax.empty((view.shape[0], N_CHIPS) + view.shape[3:], BF16)]
        return nms, tag, _copies_start(arrs, len(nms), _pair_send, f"pair_start_{tag}")

    def chip_start(state, after):
        nms, tag, (ssem, rsem, arrs, _) = state
        arrs = _copies_wait(arrs, ssem, rsem, after, _landed, f"pair_done_{tag}")
        out = []
        for t, nm in enumerate(nms):
            part = _pair_sum(arrs[2 * t], arrs[2 * t + 1], core, f"pair_sum_{nm}")
            out += [part, lax.empty((3, part.shape[0]) + part.shape[2:], BF16)]
        return nms, tag, _copies_start(out, len(nms), _chip_send, f"chip_start_{tag}")

    deferred = []
    updates = {}

    def chip_finish(state, after, defer=False):
        nms, tag, (ssem, rsem, arrs, _) = state
        arrs = _copies_wait(arrs, ssem, rsem, after, _landed, f"chip_done_{tag}")
        for t, nm in enumerate(nms):
            if defer:
                deferred.append((nm, arrs[2 * t], arrs[2 * t + 1]))
            else:
                red[nm] = _chip_sum(arrs[2 * t], arrs[2 * t + 1], chip, f"chip_sum_{nm}")

    ffn_grads = {nm: lax.empty((4, FF_SHARD, D_MODEL), F32) for nm in ("w_gate", "w_up", "w_down")}

    def place_ffn_grads(l):
        g = red[f"ffn{l}"]
        for k, nm in enumerate(("w_gate", "w_up", "w_down")):
            ffn_grads[nm] = ffn_grads[nm].at[l].set(g[k])

    dks, dvs = [], []
    pending = None
    bwd_deps = []
    for l in (3, 2, 1, 0):
        key = f"ffn{l}"
        dx, act, dgb, dub, hn, dyb, dln = _ffn_bwd(x_mid[l], dx, gates[l], ups[l], ln_ffn[l].reshape(1, -1),
                                                     w_ffn[l], f"ffn_bwd{l}", deps=bwd_deps)
        bwd_deps = []
        small[f"ln_ffn{l}"] = dln
        if l == 1:
            att_chip = chip_start(att_pair, dx)
            tn_deps = [att_chip[2][3]]
        else:
            tn_deps = []
        if pending:
            chip_finish(pending, dx, defer=True)
            pending = None
        big[key] = _tn_matmul(dgb, hn, big[key], 0, f"dw_gate{l}", m_chunk=FF_HALF, deps=tn_deps)
        big[key] = _tn_matmul(dub, hn, big[key], 1, f"dw_up{l}", m_chunk=FF_HALF)
        big[key] = _tn_matmul(act, dyb, big[key], 2, f"dw_down{l}", m_chunk=FF_HALF)
        if l == 1:
            chip_finish(att_chip, big[key], defer=True)
        ffn_pair = pair_start([key], key)
        if l >= 2:
            j = l - 2
            do, dxb, stats = _o_bwd(dx, wo_all[j], outs[j], lses[j], f"o_bwd{j}", deps=[ffn_pair[2][3]])
            big["wo"] = _tn_matmul(outs[j], dxb, big["wo"], j, f"dw_o{j}")
            ffn_chip = chip_start(ffn_pair, big["wo"])
            dq, dk, dv = _att_bwd(qs[j], k_sh, v_sh, do, stats, f"att_bwd{j}", deps=[ffn_chip[2][3]])
            chip_finish(ffn_chip, dq, defer=True)
            dks.append(dk)
            dvs.append(dv)
            dx, hnq, cqn, dqa, dcq, dln, dgql, dgqn, dgqr = _q_bwd(x_in[l], dx, dq, *q_args(j), f"q_bwd{j}")
            small[f"ln_mix_b{j}"] = dln
            small[f"g_q_latent{j}"] = dgql
            small[f"g_q{j}"] = jnp.concatenate([dgqn, dgqr[:, :ROPE]], axis=1)
            big["wdq"] = _tn_matmul(hnq, dcq, big["wdq"], j, f"dw_dq{j}")
            big["wuqT"] = _tn_matmul(dqa, cqn, big["wuqT"], j, f"dw_uq{j}")
            if l == 2:
                (dx, hnk, cn, dknb, dvb, dccb, dpeb, dlnkv, dgl, dgkn, dgkr) = _kv_bwd(
                    x_kv, dx, dks, dvs, lnkv, wc, wpe, gl, wuk_g, wuv_g, gkn, gkr, cos, sin, "kv_bwd")
                small["ln_kv"] = dlnkv
                small["g_kv_latent"] = dgl
                small["g_k"] = jnp.concatenate([dgkn, dgkr[:, :ROPE]], axis=1)
                big["kv512"] = _tn_matmul(dknb, cn, big["kv512"], 0, "dw_uk")
                big["kv512"] = _tn_matmul(dvb, cn, big["kv512"], 1, "dw_uv")
                big["kv512"] = _tn_matmul(hnk, dccb, big["kv512"], 2, "dw_dkv_c")
                big["dkv_pe"] = _tn_matmul(hnk, dpeb, big["dkv_pe"], 0, "dw_dkv_pe")
                att_pair = pair_start(["wo", "kv512", "dkv_pe", "wdq", "wuqT"], "att")
                bwd_deps = [att_pair[2][3]]
        else:
            dx, dyp, dsc, db, dln = _mix_bwd(x_in[l], dx, pooled[l], ln_a_all[l], wp_all[l], bp_all[l], sc_all[l],
                                             f"mix_bwd{l}", deps=[ffn_pair[2][3]])
            small[f"ln_mix_a{l}"] = dln
            small[f"pool_scale{l}"] = dsc
            small[f"b_pool{l}"] = db
            ffn_chip = chip_start(ffn_pair, dx)
            big["wpool"] = _tn_matmul(pooled[l], dyp, big["wpool"], 4 * l, f"dw_pool{l}", groups=4,
                                      deps=[ffn_chip[2][3]])
            if l == 1:
                pending = ffn_chip
                bwd_deps = [ffn_chip[2][3]]
            else:
                for nm, part, land in deferred:
                    red[nm] = _chip_sum(part, land, chip, f"chip_sum_{nm}", deps=[ffn_chip[2][3]])
                    if nm.startswith("ffn"):
                        place_ffn_grads(int(nm[-1]))
                early_grads = dict(
                    w_dkv=jnp.concatenate([red["kv512"][2], red["dkv_pe"][0][:, :ROPE]], axis=1),
                    w_uk=red["kv512"][0].T, w_uv=red["kv512"][1].T, w_dq=red["wdq"],
                    w_uq=red["wuqT"].transpose(0, 2, 1)[:, :, :QK_DIM], w_o=red["wo"])
                for nm, g in early_grads.items():
                    updates[nm] = _adamw(weights[nm], g, mom1[nm], mom2[nm], f"adamw_{nm}")
                chip_finish(ffn_chip, [big["wpool"]] + list(ffn_grads.values()) + [u[0] for u in updates.values()])
                place_ffn_grads(0)
    grad_x = dx[None]
    pool_pair = pair_start(["wpool"], "wpool")
    pool_chip = chip_start(pool_pair, pool_pair[2][3])
    chip_finish(pool_chip, pool_chip[2][3])

    vec_names = (["loss"] + [f"ln_ffn{l}" for l in range(4)] + ["ln_kv", "g_kv_latent", "g_k"]
                 + [f"{p}{j}" for p in ("ln_mix_b", "g_q_latent", "g_q") for j in range(2)]
                 + [f"{p}{l}" for p in ("ln_mix_a", "pool_scale", "b_pool") for l in range(2)])
    small["loss"] = sq_cols
    widths = [small[nm].shape[1] for nm in vec_names]
    padded = [-(-w // LANES) * LANES for w in widths]
    packed = jnp.concatenate([_pad_cols(small[nm], pw) for nm, pw in zip(vec_names, padded)], axis=1)
    (all_vecs,) = _all_gather([packed], [0], "gather_vectors")
    total = _sum_lead(all_vecs, "sum_vectors")
    vec = {}
    off = 0
    for nm, w, pw in zip(vec_names, widths, padded):
        vec[nm] = total[0, off:off + w]
        off += pw
    loss = 0.5 * jnp.sum(vec["loss"]) * (1.0 / D_MODEL)

    def own_cols(full, width):
        return lax.dynamic_slice_in_dim(full, dev * width, width, axis=full.ndim - 1)

    grads = dict(
        ln_mix_a=own_cols(jnp.stack([vec["ln_mix_a0"], vec["ln_mix_a1"]]), LANES),
        w_pool=red["wpool"].reshape(2, 4, 32, GROUP_DIM),
        b_pool=own_cols(jnp.stack([vec["b_pool0"], vec["b_pool1"]]).reshape(2, 4, GROUP_DIM), 32),
        pool_scale=own_cols(jnp.stack([vec["pool_scale0"], vec["pool_scale1"]]), LANES),
        ln_ffn=jnp.stack([vec[f"ln_ffn{l}"] for l in range(4)]),
        w_gate=ffn_grads["w_gate"],
        w_up=ffn_grads["w_up"],
        w_down=ffn_grads["w_down"],
        ln_kv=vec["ln_kv"],
        g_kv_latent=vec["g_kv_latent"],
        g_k=vec["g_k"],
        ln_mix_b=jnp.stack([vec["ln_mix_b0"], vec["ln_mix_b1"]]),
        g_q_latent=jnp.stack([vec["g_q_latent0"], vec["g_q_latent1"]]),
        g_q=jnp.stack([vec["g_q0"], vec["g_q1"]]),
        **early_grads,
    )

    deltas, new_m, new_v = {}, {}, {}
    for nm in names:
        w = weights[nm]
        if nm in updates:
            deltas[nm], new_m[nm], new_v[nm] = updates[nm]
            continue
        if nm in ("w_gate", "w_up"):
            def swap(a):
                return a.transpose(0, 2, 1)
            d, mo, vo = _adamw(swap(w), grads[nm], swap(mom1[nm]), swap(mom2[nm]), f"adamw_{nm}")
            deltas[nm], new_m[nm], new_v[nm], grads[nm] = swap(d), swap(mo), swap(vo), swap(grads[nm])
            continue
        shape = w.shape if w.ndim > 1 else (1, w.shape[0])
        d, mo, vo = _adamw(w.reshape(shape), grads[nm].reshape(shape), mom1[nm].reshape(shape),
                           mom2[nm].reshape(shape), f"adamw_{nm}")
        deltas[nm], new_m[nm], new_v[nm] = d.reshape(w.shape), mo.reshape(w.shape), vo.reshape(w.shape)

    return (loss, grad_x, *[grads[nm].reshape(weights[nm].shape) for nm in names], *[deltas[nm] for nm in names],
            *[new_m[nm] for nm in names], *[new_v[nm] for nm in names])
```
